```python
import jax, jax.numpy as jnp
from jax import lax
import numpy as np

D_MODEL = 1024
BATCH = 8
SEQ = 8192
DEPTH = 1

HEAD_DIM = 64
NA_HEADS = 8
NB_HEADS = 8
NB_KV_HEADS = 2
GRID_W = 64
NA_KH_MAX = 8
NA_KW = 16
WIN = 128
BLOCK = 128
ROPE_THETA = 10000.0
D_FF = 2816
EPS = 1e-6

WIDTH_A = NA_HEADS * HEAD_DIM
WIDTH_BQ = NB_HEADS * HEAD_DIM
WIDTH_BKV = NB_KV_HEADS * HEAD_DIM
D_IN = 3 * WIDTH_A + WIDTH_BQ + 2 * WIDTH_BKV + 2 * D_MODEL
SPLITS = (WIDTH_A, 2 * WIDTH_A, 3 * WIDTH_A,
          3 * WIDTH_A + WIDTH_BQ,
          3 * WIDTH_A + WIDTH_BQ + WIDTH_BKV,
          3 * WIDTH_A + WIDTH_BQ + 2 * WIDTH_BKV,
          3 * WIDTH_A + WIDTH_BQ + 2 * WIDTH_BKV + D_MODEL)

kernel_name = "hybrid_natten_swa_gated_macaron"


def rms_norm(x, g):
    xf = x.astype(jnp.float32)
    y = xf * lax.rsqrt(jnp.mean(xf * xf, axis=-1, keepdims=True) + EPS)
    return (y * g.astype(jnp.float32)).astype(x.dtype)


def swiglu(x, w_gate, w_up, w_down):
    return (jax.nn.silu(x @ w_gate) * (x @ w_up)) @ w_down


def rope(x, pos):
    half = x.shape[-1] // 2
    inv = ROPE_THETA ** (-jnp.arange(half, dtype=jnp.float32) / half)
    ang = pos.astype(jnp.float32)[:, None] * inv[None, :]
    cos = jnp.cos(ang)[None, :, None, :]
    sin = jnp.sin(ang)[None, :, None, :]
    x1 = x[..., :half].astype(jnp.float32)
    x2 = x[..., half:].astype(jnp.float32)
    out = jnp.concatenate([x1 * cos - x2 * sin, x2 * cos + x1 * sin], axis=-1)
    return out.astype(x.dtype)


def neighbourhood_attention_2d(q, k, v, rpb):
    B, T, H, dh = q.shape
    rows = T // GRID_W
    kh = min(NA_KH_MAX, rows)
    q = q.reshape(B, rows, GRID_W, H, dh)
    k = k.reshape(B, rows, GRID_W, H, dh)
    v = v.reshape(B, rows, GRID_W, H, dh)
    cols = jnp.arange(GRID_W)
    col_start = jnp.clip(cols - NA_KW // 2, 0, GRID_W - NA_KW)
    col_idx = col_start[:, None] + jnp.arange(NA_KW)[None, :]
    col_off = col_idx - cols[:, None] + (NA_KW - 1)
    rpb = rpb.astype(jnp.float32)
    scale = dh ** -0.5

    def row_block(args):
        r, q_r = args
        rs = jnp.clip(r - kh // 2, 0, rows - kh)
        k_slab = lax.dynamic_slice_in_dim(k, rs, kh, axis=1)
        v_slab = lax.dynamic_slice_in_dim(v, rs, kh, axis=1)
        k_g = k_slab[:, :, col_idx]
        v_g = v_slab[:, :, col_idx]
        row_off = rs + jnp.arange(kh) - r + (NA_KH_MAX - 1)
        bias = rpb[:, row_off[None, :, None], col_off[:, None, :]]
        s = jnp.einsum('bchd,bicjhd->bhcij', q_r, k_g).astype(jnp.float32) * scale
        s = s + bias[None]
        p = jax.nn.softmax(s.reshape(B, H, GRID_W, kh * NA_KW), axis=-1)
        p = p.reshape(B, H, GRID_W, kh, NA_KW).astype(v.dtype)
        return jnp.einsum('bhcij,bicjhd->bchd', p, v_g)

    out = lax.map(row_block, (jnp.arange(rows), jnp.moveaxis(q, 1, 0)))
    return jnp.moveaxis(out, 0, 1).reshape(B, T, H * dh)


def windowed_gqa_sink(q, k, v, sink):
    B, T, Hq, dh = q.shape
    Hkv = k.shape[2]
    G = Hq // Hkv
    nb = T // BLOCK
    qb = q.reshape(B, nb, BLOCK, Hkv, G, dh)

    def band(a):
        ap = jnp.pad(a, ((0, 0), (BLOCK, BLOCK), (0, 0), (0, 0)))
        ab = ap.reshape(B, nb + 2, BLOCK, Hkv, dh)
        return jnp.concatenate([ab[:, :-2], ab[:, 1:-1], ab[:, 2:]], axis=2)

    kb = band(k)
    vb = band(v)
    qpos = jnp.arange(nb)[:, None] * BLOCK + jnp.arange(BLOCK)[None, :]
    kpos = jnp.arange(nb)[:, None] * BLOCK - BLOCK + jnp.arange(3 * BLOCK)[None, :]
    kp = kpos[:, None, :]
    mask = (jnp.abs(qpos[:, :, None] - kp) <= WIN) & (kp >= 0) & (kp < T)
    s = jnp.einsum('bnqhgd,bnkhd->bnhgqk', qb, kb).astype(jnp.float32) * (dh ** -0.5)
    s = jnp.where(mask[None, :, None, None], s, -jnp.inf)
    sink_l = sink.astype(jnp.float32).reshape(Hkv, G)[None, None, :, :, None, None]
    m = jnp.maximum(jnp.max(s, axis=-1, keepdims=True), sink_l)
    e = jnp.exp(s - m)
    p = e / (jnp.sum(e, axis=-1, keepdims=True) + jnp.exp(sink_l - m))
    o = jnp.einsum('bnhgqk,bnkhd->bnqhgd', p.astype(v.dtype), vb)
    return o.reshape(B, T, Hq * dh)


def _fwd_setup_inputs(seed: int = 0) -> dict:
    key = jax.random.key(seed)
    ks = jax.random.split(key, 20)
    f32 = jnp.float32

    def w(k, shape, fan_in):
        return jax.random.normal(k, shape, f32) * (fan_in ** -0.5)

    def gain(k, shape):
        return 1.0 + 0.02 * jax.random.normal(k, shape, f32)

    return {
        "x": jax.random.normal(ks[0], (BATCH, SEQ, D_MODEL), f32),
        "ffn1_norm": gain(ks[1], (DEPTH, D_MODEL)),
        "ffn1_w_gate": w(ks[2], (DEPTH, D_MODEL, D_FF), D_MODEL),
        "ffn1_w_up": w(ks[3], (DEPTH, D_MODEL, D_FF), D_MODEL),
        "ffn1_w_down": w(ks[4], (DEPTH, D_FF, D_MODEL), D_FF),
        "mix_norm": gain(ks[5], (DEPTH, D_MODEL)),
        "w_in": w(ks[6], (DEPTH, D_MODEL, D_IN), D_MODEL),
        "na_rpb": 0.1 * jax.random.normal(ks[7], (DEPTH, NA_HEADS, 2 * NA_KH_MAX - 1, 2 * NA_KW - 1), f32),
        "sink_logit": 0.5 * jax.random.normal(ks[8], (DEPTH, NB_HEADS), f32),
        "w_branch_a": w(ks[9], (DEPTH, WIDTH_A, D_MODEL), WIDTH_A),
        "w_branch_b": w(ks[10], (DEPTH, WIDTH_BQ, D_MODEL), WIDTH_BQ),
        "w_out": w(ks[11], (DEPTH, D_MODEL, D_MODEL), D_MODEL),
        "ffn2_norm": gain(ks[12], (DEPTH, D_MODEL)),
        "ffn2_w_gate": w(ks[13], (DEPTH, D_MODEL, D_FF), D_MODEL),
        "ffn2_w_up": w(ks[14], (DEPTH, D_MODEL, D_FF), D_MODEL),
        "ffn2_w_down": w(ks[15], (DEPTH, D_FF, D_MODEL), D_FF),
        "final_norm": gain(ks[16], (D_MODEL,)),
    }


def _fwd_reference(x, ffn1_norm, ffn1_w_gate, ffn1_w_up, ffn1_w_down, mix_norm, w_in, na_rpb,
              sink_logit, w_branch_a, w_branch_b, w_out, ffn2_norm, ffn2_w_gate, ffn2_w_up,
              ffn2_w_down, final_norm):
    B, T, _ = x.shape
    pos = jnp.arange(T)
    h = x
    for l in range(DEPTH):
        h = h + 0.5 * swiglu(rms_norm(h, ffn1_norm[l]), ffn1_w_gate[l], ffn1_w_up[l], ffn1_w_down[l])
        u = rms_norm(h, mix_norm[l])
        z = u @ w_in[l]
        qa, ka, va, qb, kb, vb, ga, gb = jnp.split(z, SPLITS, axis=-1)
        qa = qa.reshape(B, T, NA_HEADS, HEAD_DIM)
        ka = ka.reshape(B, T, NA_HEADS, HEAD_DIM)
        va = va.reshape(B, T, NA_HEADS, HEAD_DIM)
        ya = neighbourhood_attention_2d(qa, ka, va, na_rpb[l])
        qb = rope(qb.reshape(B, T, NB_HEADS, HEAD_DIM), pos)
        kb = rope(kb.reshape(B, T, NB_KV_HEADS, HEAD_DIM), pos)
        vb = vb.reshape(B, T, NB_KV_HEADS, HEAD_DIM)
        yb = windowed_gqa_sink(qb, kb, vb, sink_logit[l])
        merged = jax.nn.sigmoid(ga) * (ya @ w_branch_a[l]) + jax.nn.sigmoid(gb) * (yb @ w_branch_b[l])
        h = h + merged @ w_out[l]
        h = h + 0.5 * swiglu(rms_norm(h, ffn2_norm[l]), ffn2_w_gate[l], ffn2_w_up[l], ffn2_w_down[l])
    return rms_norm(h, final_norm)


import jax as _jax
import jax.numpy as _jnp

TWIN_FORMAT = 'train_step'
FWD_PARAMS = ['x', 'ffn1_norm', 'ffn1_w_gate', 'ffn1_w_up', 'ffn1_w_down', 'mix_norm', 'w_in', 'na_rpb', 'sink_logit', 'w_branch_a', 'w_branch_b', 'w_out', 'ffn2_norm', 'ffn2_w_gate', 'ffn2_w_up', 'ffn2_w_down', 'final_norm']
TWIN_WEIGHTS = ['ffn1_norm', 'ffn1_w_gate', 'ffn1_w_up', 'ffn1_w_down', 'mix_norm', 'w_in', 'na_rpb', 'sink_logit', 'w_branch_a', 'w_branch_b', 'w_out', 'ffn2_norm', 'ffn2_w_gate', 'ffn2_w_up', 'ffn2_w_down', 'final_norm']
TWIN_DIFF_INPUT = 'x'
TWIN_INPUTS = ['x', 'ffn1_norm', 'ffn1_w_gate', 'ffn1_w_up', 'ffn1_w_down', 'mix_norm', 'w_in', 'na_rpb', 'sink_logit', 'w_branch_a', 'w_branch_b', 'w_out', 'ffn2_norm', 'ffn2_w_gate', 'ffn2_w_up', 'ffn2_w_down', 'final_norm', 'loss_target', 'm_ffn1_norm', 'm_ffn1_w_gate', 'm_ffn1_w_up', 'm_ffn1_w_down', 'm_mix_norm', 'm_w_in', 'm_na_rpb', 'm_sink_logit', 'm_w_branch_a', 'm_w_branch_b', 'm_w_out', 'm_ffn2_norm', 'm_ffn2_w_gate', 'm_ffn2_w_up', 'm_ffn2_w_down', 'm_final_norm', 'v_ffn1_norm', 'v_ffn1_w_gate', 'v_ffn1_w_up', 'v_ffn1_w_down', 'v_mix_norm', 'v_w_in', 'v_na_rpb', 'v_sink_logit', 'v_w_branch_a', 'v_w_branch_b', 'v_w_out', 'v_ffn2_norm', 'v_ffn2_w_gate', 'v_ffn2_w_up', 'v_ffn2_w_down', 'v_final_norm']
TWIN_OUTPUTS = ['loss', 'grad_x', 'grad_ffn1_norm', 'grad_ffn1_w_gate', 'grad_ffn1_w_up', 'grad_ffn1_w_down', 'grad_mix_norm', 'grad_w_in', 'grad_na_rpb', 'grad_sink_logit', 'grad_w_branch_a', 'grad_w_branch_b', 'grad_w_out', 'grad_ffn2_norm', 'grad_ffn2_w_gate', 'grad_ffn2_w_up', 'grad_ffn2_w_down', 'grad_final_norm', 'delta_ffn1_norm', 'delta_ffn1_w_gate', 'delta_ffn1_w_up', 'delta_ffn1_w_down', 'delta_mix_norm', 'delta_w_in', 'delta_na_rpb', 'delta_sink_logit', 'delta_w_branch_a', 'delta_w_branch_b', 'delta_w_out', 'delta_ffn2_norm', 'delta_ffn2_w_gate', 'delta_ffn2_w_up', 'delta_ffn2_w_down', 'delta_final_norm', 'new_m_ffn1_norm', 'new_m_ffn1_w_gate', 'new_m_ffn1_w_up', 'new_m_ffn1_w_down', 'new_m_mix_norm', 'new_m_w_in', 'new_m_na_rpb', 'new_m_sink_logit', 'new_m_w_branch_a', 'new_m_w_branch_b', 'new_m_w_out', 'new_m_ffn2_norm', 'new_m_ffn2_w_gate', 'new_m_ffn2_w_up', 'new_m_ffn2_w_down', 'new_m_final_norm', 'new_v_ffn1_norm', 'new_v_ffn1_w_gate', 'new_v_ffn1_w_up', 'new_v_ffn1_w_down', 'new_v_mix_norm', 'new_v_w_in', 'new_v_na_rpb', 'new_v_sink_logit', 'new_v_w_branch_a', 'new_v_w_branch_b', 'new_v_w_out', 'new_v_ffn2_norm', 'new_v_ffn2_w_gate', 'new_v_ffn2_w_up', 'new_v_ffn2_w_down', 'new_v_final_norm']
TWIN_LEAF_KINDS = {'loss': 'loss', 'grad_x': 'grad_x', 'grad_ffn1_norm': 'grad_w', 'grad_ffn1_w_gate': 'grad_w', 'grad_ffn1_w_up': 'grad_w', 'grad_ffn1_w_down': 'grad_w', 'grad_mix_norm': 'grad_w', 'grad_w_in': 'grad_w', 'grad_na_rpb': 'grad_w', 'grad_sink_logit': 'grad_w', 'grad_w_branch_a': 'grad_w', 'grad_w_branch_b': 'grad_w', 'grad_w_out': 'grad_w', 'grad_ffn2_norm': 'grad_w', 'grad_ffn2_w_gate': 'grad_w', 'grad_ffn2_w_up': 'grad_w', 'grad_ffn2_w_down': 'grad_w', 'grad_final_norm': 'grad_w', 'delta_ffn1_norm': 'delta_w', 'delta_ffn1_w_gate': 'delta_w', 'delta_ffn1_w_up': 'delta_w', 'delta_ffn1_w_down': 'delta_w', 'delta_mix_norm': 'delta_w', 'delta_w_in': 'delta_w', 'delta_na_rpb': 'delta_w', 'delta_sink_logit': 'delta_w', 'delta_w_branch_a': 'delta_w', 'delta_w_branch_b': 'delta_w', 'delta_w_out': 'delta_w', 'delta_ffn2_norm': 'delta_w', 'delta_ffn2_w_gate': 'delta_w', 'delta_ffn2_w_up': 'delta_w', 'delta_ffn2_w_down': 'delta_w', 'delta_final_norm': 'delta_w', 'new_m_ffn1_norm': 'new_m', 'new_m_ffn1_w_gate': 'new_m', 'new_m_ffn1_w_up': 'new_m', 'new_m_ffn1_w_down': 'new_m', 'new_m_mix_norm': 'new_m', 'new_m_w_in': 'new_m', 'new_m_na_rpb': 'new_m', 'new_m_sink_logit': 'new_m', 'new_m_w_branch_a': 'new_m', 'new_m_w_branch_b': 'new_m', 'new_m_w_out': 'new_m', 'new_m_ffn2_norm': 'new_m', 'new_m_ffn2_w_gate': 'new_m', 'new_m_ffn2_w_up': 'new_m', 'new_m_ffn2_w_down': 'new_m', 'new_m_final_norm': 'new_m', 'new_v_ffn1_norm': 'new_v', 'new_v_ffn1_w_gate': 'new_v', 'new_v_ffn1_w_up': 'new_v', 'new_v_ffn1_w_down': 'new_v', 'new_v_mix_norm': 'new_v', 'new_v_w_in': 'new_v', 'new_v_na_rpb': 'new_v', 'new_v_sink_logit': 'new_v', 'new_v_w_branch_a': 'new_v', 'new_v_w_branch_b': 'new_v', 'new_v_w_out': 'new_v', 'new_v_ffn2_norm': 'new_v', 'new_v_ffn2_w_gate': 'new_v', 'new_v_ffn2_w_up': 'new_v', 'new_v_ffn2_w_down': 'new_v', 'new_v_final_norm': 'new_v'}


def _forward(args):
    return _fwd_reference(*[args[k] for k in FWD_PARAMS])


def _output_shape():
    out = _jax.eval_shape(lambda: _forward(_fwd_setup_inputs(0)))
    return out.shape, out.dtype

N_MICROBATCH = 1
ADAM_LR = 0.001
ADAM_B1 = 0.9
ADAM_B2 = 0.999
ADAM_EPS = 1e-08
ADAM_WD = 0.01
ADAM_STEP = 10
PER_EXAMPLE_BATCH_AXIS = {'x': 0, 'loss_target': 0}
SHARED_INPUTS = []
_WEIGHT_DTYPES = {'ffn1_norm': _jnp.float32, 'ffn1_w_gate': _jnp.float32, 'ffn1_w_up': _jnp.float32, 'ffn1_w_down': _jnp.float32, 'mix_norm': _jnp.float32, 'w_in': _jnp.float32, 'na_rpb': _jnp.float32, 'sink_logit': _jnp.float32, 'w_branch_a': _jnp.float32, 'w_branch_b': _jnp.float32, 'w_out': _jnp.float32, 'ffn2_norm': _jnp.float32, 'ffn2_w_gate': _jnp.float32, 'ffn2_w_up': _jnp.float32, 'ffn2_w_down': _jnp.float32, 'final_norm': _jnp.float32}
MOMENT_SCALE = {'ffn1_norm': 1.252474e-01, 'ffn1_w_gate': 4.703177e-02, 'ffn1_w_up': 4.546602e-02, 'ffn1_w_down': 7.542655e-02, 'mix_norm': 5.411835e-02, 'w_in': 2.637344e-02, 'na_rpb': 1.303400e-02, 'sink_logit': 3.591346e-04, 'w_branch_a': 2.689508e-02, 'w_branch_b': 1.678307e-02, 'w_out': 3.166320e-02, 'ffn2_norm': 1.126987e-01, 'ffn2_w_gate': 4.303136e-02, 'ffn2_w_up': 4.154917e-02, 'ffn2_w_down': 6.904006e-02, 'final_norm': 6.392211e+01}


def _to_microbatches(a, axis):
    t = _jnp.moveaxis(a, axis, 0)
    t = t.reshape((N_MICROBATCH, t.shape[0] // N_MICROBATCH) + t.shape[1:])
    return _jnp.moveaxis(t, 1, axis + 1)


def setup_inputs(seed: int = 0) -> dict:
    inp = _fwd_setup_inputs(seed)
    key = _jax.random.fold_in(_jax.random.key(seed), 7919)
    shape, _ = _output_shape()
    out = dict(inp)
    out["loss_target"] = _jax.random.normal(_jax.random.fold_in(key, 0), shape, _jnp.float32)
    for i, name in enumerate(TWIN_WEIGHTS):
        w = inp[name].astype(_jnp.float32)
        if MOMENT_SCALE is None:
            s = _jnp.sqrt(_jnp.mean(_jnp.square(w)) + 1e-30)
        else:
            s = MOMENT_SCALE[name]
        km, kv = _jax.random.split(_jax.random.fold_in(key, i + 1))
        out[name] = w
        out["m_" + name] = s * _jax.random.normal(km, w.shape, _jnp.float32)
        out["v_" + name] = (s * s) * _jax.random.uniform(kv, w.shape, _jnp.float32, 0.5, 1.5)
    if N_MICROBATCH > 1:
        for name, axis in PER_EXAMPLE_BATCH_AXIS.items():
            out[name] = _to_microbatches(out[name], axis)
    return {'x': out['x'], 'ffn1_norm': out['ffn1_norm'], 'ffn1_w_gate': out['ffn1_w_gate'], 'ffn1_w_up': out['ffn1_w_up'], 'ffn1_w_down': out['ffn1_w_down'], 'mix_norm': out['mix_norm'], 'w_in': out['w_in'], 'na_rpb': out['na_rpb'], 'sink_logit': out['sink_logit'], 'w_branch_a': out['w_branch_a'], 'w_branch_b': out['w_branch_b'], 'w_out': out['w_out'], 'ffn2_norm': out['ffn2_norm'], 'ffn2_w_gate': out['ffn2_w_gate'], 'ffn2_w_up': out['ffn2_w_up'], 'ffn2_w_down': out['ffn2_w_down'], 'final_norm': out['final_norm'], 'loss_target': out['loss_target'], 'm_ffn1_norm': out['m_ffn1_norm'], 'm_ffn1_w_gate': out['m_ffn1_w_gate'], 'm_ffn1_w_up': out['m_ffn1_w_up'], 'm_ffn1_w_down': out['m_ffn1_w_down'], 'm_mix_norm': out['m_mix_norm'], 'm_w_in': out['m_w_in'], 'm_na_rpb': out['m_na_rpb'], 'm_sink_logit': out['m_sink_logit'], 'm_w_branch_a': out['m_w_branch_a'], 'm_w_branch_b': out['m_w_branch_b'], 'm_w_out': out['m_w_out'], 'm_ffn2_norm': out['m_ffn2_norm'], 'm_ffn2_w_gate': out['m_ffn2_w_gate'], 'm_ffn2_w_up': out['m_ffn2_w_up'], 'm_ffn2_w_down': out['m_ffn2_w_down'], 'm_final_norm': out['m_final_norm'], 'v_ffn1_norm': out['v_ffn1_norm'], 'v_ffn1_w_gate': out['v_ffn1_w_gate'], 'v_ffn1_w_up': out['v_ffn1_w_up'], 'v_ffn1_w_down': out['v_ffn1_w_down'], 'v_mix_norm': out['v_mix_norm'], 'v_w_in': out['v_w_in'], 'v_na_rpb': out['v_na_rpb'], 'v_sink_logit': out['v_sink_logit'], 'v_w_branch_a': out['v_w_branch_a'], 'v_w_branch_b': out['v_w_branch_b'], 'v_w_out': out['v_w_out'], 'v_ffn2_norm': out['v_ffn2_norm'], 'v_ffn2_w_gate': out['v_ffn2_w_gate'], 'v_ffn2_w_up': out['v_ffn2_w_up'], 'v_ffn2_w_down': out['v_ffn2_w_down'], 'v_final_norm': out['v_final_norm']}


def _loss(weights, diff, rest, loss_target):
    with _jax.named_scope("forward"):
        args = {**rest, TWIN_DIFF_INPUT: diff, **{k: w.astype(_WEIGHT_DTYPES[k]) for k, w in weights.items()}}
        y = _forward(args)
    with _jax.named_scope("loss_head"):
        err = _jnp.square(y.astype(_jnp.float32) - loss_target)
        return 0.5 * _jnp.sum(_jnp.mean(err, axis=-1)) if err.ndim else 0.5 * err


def _adamw(w, g, m, v):
    m = ADAM_B1 * m + (1.0 - ADAM_B1) * g
    v = ADAM_B2 * v + (1.0 - ADAM_B2) * _jnp.square(g)
    m_hat = m / (1.0 - ADAM_B1 ** ADAM_STEP)
    v_hat = v / (1.0 - ADAM_B2 ** ADAM_STEP)
    delta = -ADAM_LR * (m_hat / (_jnp.sqrt(v_hat) + ADAM_EPS) + ADAM_WD * w)
    return delta, m, v


def reference(x, ffn1_norm, ffn1_w_gate, ffn1_w_up, ffn1_w_down, mix_norm, w_in, na_rpb, sink_logit, w_branch_a, w_branch_b, w_out, ffn2_norm, ffn2_w_gate, ffn2_w_up, ffn2_w_down, final_norm, loss_target, m_ffn1_norm, m_ffn1_w_gate, m_ffn1_w_up, m_ffn1_w_down, m_mix_norm, m_w_in, m_na_rpb, m_sink_logit, m_w_branch_a, m_w_branch_b, m_w_out, m_ffn2_norm, m_ffn2_w_gate, m_ffn2_w_up, m_ffn2_w_down, m_final_norm, v_ffn1_norm, v_ffn1_w_gate, v_ffn1_w_up, v_ffn1_w_down, v_mix_norm, v_w_in, v_na_rpb, v_sink_logit, v_w_branch_a, v_w_branch_b, v_w_out, v_ffn2_norm, v_ffn2_w_gate, v_ffn2_w_up, v_ffn2_w_down, v_final_norm):
    given = dict(x=x, ffn1_norm=ffn1_norm, ffn1_w_gate=ffn1_w_gate, ffn1_w_up=ffn1_w_up, ffn1_w_down=ffn1_w_down, mix_norm=mix_norm, w_in=w_in, na_rpb=na_rpb, sink_logit=sink_logit, w_branch_a=w_branch_a, w_branch_b=w_branch_b, w_out=w_out, ffn2_norm=ffn2_norm, ffn2_w_gate=ffn2_w_gate, ffn2_w_up=ffn2_w_up, ffn2_w_down=ffn2_w_down, final_norm=final_norm, loss_target=loss_target, m_ffn1_norm=m_ffn1_norm, m_ffn1_w_gate=m_ffn1_w_gate, m_ffn1_w_up=m_ffn1_w_up, m_ffn1_w_down=m_ffn1_w_down, m_mix_norm=m_mix_norm, m_w_in=m_w_in, m_na_rpb=m_na_rpb, m_sink_logit=m_sink_logit, m_w_branch_a=m_w_branch_a, m_w_branch_b=m_w_branch_b, m_w_out=m_w_out, m_ffn2_norm=m_ffn2_norm, m_ffn2_w_gate=m_ffn2_w_gate, m_ffn2_w_up=m_ffn2_w_up, m_ffn2_w_down=m_ffn2_w_down, m_final_norm=m_final_norm, v_ffn1_norm=v_ffn1_norm, v_ffn1_w_gate=v_ffn1_w_gate, v_ffn1_w_up=v_ffn1_w_up, v_ffn1_w_down=v_ffn1_w_down, v_mix_norm=v_mix_norm, v_w_in=v_w_in, v_na_rpb=v_na_rpb, v_sink_logit=v_sink_logit, v_w_branch_a=v_w_branch_a, v_w_branch_b=v_w_branch_b, v_w_out=v_w_out, v_ffn2_norm=v_ffn2_norm, v_ffn2_w_gate=v_ffn2_w_gate, v_ffn2_w_up=v_ffn2_w_up, v_ffn2_w_down=v_ffn2_w_down, v_final_norm=v_final_norm)
    weights = {n: given[n] for n in TWIN_WEIGHTS}
    shared = {n: given[n] for n in SHARED_INPUTS}
    per_example = {n: given[n] for n in ['x']}
    grad_fn = _jax.value_and_grad(_loss, argnums=(0, 1))

    def one_microbatch(ex, loss_target):
        ex = dict(ex)
        diff = ex.pop(TWIN_DIFF_INPUT)
        return grad_fn(weights, diff, {**shared, **ex}, loss_target)

    if N_MICROBATCH == 1:
        loss, (grad_w, grad_x) = one_microbatch(per_example, given["loss_target"])
    else:
        def body(carry, xs):
            loss_sum, grad_sum = carry
            l_k, (gw_k, gx_k) = one_microbatch(xs[0], xs[1])
            with _jax.named_scope("update"):
                return (loss_sum + l_k, _jax.tree.map(_jnp.add, grad_sum, gw_k)), gx_k

        init = (_jnp.zeros((), _jnp.float32), _jax.tree.map(_jnp.zeros_like, weights))
        (loss, grad_w), grad_x = _jax.lax.scan(body, init, (per_example, given["loss_target"]))
    with _jax.named_scope("update"):
        delta_w, new_m, new_v = {}, {}, {}
        for n in TWIN_WEIGHTS:
            delta_w[n], new_m[n], new_v[n] = _adamw(weights[n], grad_w[n], given["m_" + n], given["v_" + n])
    return (loss, grad_x, *[grad_w[n] for n in TWIN_WEIGHTS], *[delta_w[n] for n in TWIN_WEIGHTS],
            *[new_m[n] for n in TWIN_WEIGHTS], *[new_v[n] for n in TWIN_WEIGHTS])
```

```python
import functools
import math

import jax
import jax.numpy as jnp
from jax import lax
from jax.experimental import pallas as pl
from jax.experimental.pallas import tpu as pltpu

F32 = jnp.float32
BF16 = jnp.bfloat16

D_MODEL = 1024
HEAD_DIM = 64
NA_HEADS = 8
NB_HEADS = 8
GRID_W = 64
NA_KH = 8
NA_KW = 16
WIN = 128
ROPE_THETA = 10000.0
EPS = 1e-6
N_CHIPS = 4
QK_SCALE = HEAD_DIM ** -0.5
NEG = -1e30
LANES = 128
VMEM_LIMIT = 56 * 1024 * 1024

C_QKVA = 3 * NA_HEADS * HEAD_DIM
C_QB = NB_HEADS * HEAD_DIM
C_KB = 2 * HEAD_DIM
C_ROPE = C_QB + C_KB
C_GATES = 2 * D_MODEL
D_IN = C_QKVA + C_QB + 2 * C_KB + C_GATES
O_QB = C_QKVA
O_KB = O_QB + C_QB
O_VB = O_KB + C_KB
O_G = O_VB + C_KB

ADAM_LR = 0.001
ADAM_B1 = 0.9
ADAM_B2 = 0.999
ADAM_EPS = 1e-08
ADAM_WD = 0.01
ADAM_STEP = 10

MESH = pl.DeviceIdType.MESH


def _dot(a, b):
    return jnp.dot(a, b, preferred_element_type=F32)


def _dot_nt(a, b):
    return lax.dot_general(a, b, (((1,), (1,)), ((), ())), preferred_element_type=F32)


def _dot_tn(a, b):
    return lax.dot_general(a, b, (((0,), (0,)), ((), ())), preferred_element_type=F32)


def _params(n_axes):
    return pltpu.CompilerParams(dimension_semantics=("arbitrary",) * n_axes, vmem_limit_bytes=VMEM_LIMIT)


def _rstd(xf):
    return lax.rsqrt(jnp.mean(xf * xf, axis=-1, keepdims=True) + EPS)


def _norm_bwd(dn, xf, g, r):
    xhat = xf * r
    dxh = dn * g
    dx = r * (dxh - xhat * jnp.mean(dxh * xhat, axis=-1, keepdims=True))
    return dx, dn * xhat


def _ffn_fwd(x, g, wg, wu, wd, *, name, tm=512):
    T, D = x.shape
    F = wg.shape[2]

    def body(x_ref, g_ref, wg_ref, wu_ref, wd_ref, h_ref, n_ref, a_ref, b_ref):
        s = pl.program_id(1)

        @pl.when(s == 0)
        def _():
            xf = x_ref[...]
            n_ref[...] = ((xf * _rstd(xf)) * g_ref[...]).astype(BF16)
            h_ref[...] = xf

        n = n_ref[...]
        a = _dot(n, wg_ref[0])
        b = _dot(n, wu_ref[0])
        a_ref[0] = a.astype(BF16)
        b_ref[0] = b.astype(BF16)
        hdn = (a * jax.nn.sigmoid(a) * b).astype(BF16)
        h_ref[...] += 0.5 * _dot(hdn, wd_ref[0])

    tok = pl.BlockSpec((tm, D), lambda i, s: (i, 0))
    hid = pl.BlockSpec((1, tm, F), lambda i, s: (s, i, 0))
    return pl.pallas_call(
        body, name=name, grid=(T // tm, N_CHIPS),
        in_specs=[tok, pl.BlockSpec((1, D), lambda i, s: (0, 0)),
                  pl.BlockSpec((1, D, F), lambda i, s: (s, 0, 0)), pl.BlockSpec((1, D, F), lambda i, s: (s, 0, 0)),
                  pl.BlockSpec((1, F, D), lambda i, s: (s, 0, 0))],
        out_specs=[tok, tok, hid, hid],
        out_shape=[jax.ShapeDtypeStruct((T, D), F32), jax.ShapeDtypeStruct((T, D), BF16),
                   jax.ShapeDtypeStruct((N_CHIPS, T, F), BF16), jax.ShapeDtypeStruct((N_CHIPS, T, F), BF16)],
        compiler_params=_params(2),
    )(x, g, wg, wu, wd)


def _ffn_bwd(dh, x, g, a, b, wg, wu, wd, *, name, tm=512):
    T, D = x.shape
    F = wg.shape[2]

    def body(dh_ref, x_ref, g_ref, a_ref, b_ref, wg_ref, wu_ref, wd_ref, dx_ref, da_ref, db_ref, hdn_ref, dg_ref):
        i, s = pl.program_id(0), pl.program_id(1)

        @pl.when((i == 0) & (s == 0))
        def _():
            dg_ref[...] = jnp.zeros_like(dg_ref)

        @pl.when(s == 0)
        def _():
            dx_ref[...] = jnp.zeros_like(dx_ref)

        dhdn = _dot_nt((0.5 * dh_ref[...]).astype(BF16), wd_ref[0])
        af = a_ref[0].astype(F32)
        bf = b_ref[0].astype(F32)
        sg = jax.nn.sigmoid(af)
        silu = af * sg
        hdn_ref[0] = (silu * bf).astype(BF16)
        da = (dhdn * bf * (sg * (1.0 + af * (1.0 - sg)))).astype(BF16)
        db = (dhdn * silu).astype(BF16)
        da_ref[0] = da
        db_ref[0] = db
        dx_ref[...] += _dot_nt(da, wg_ref[0]) + _dot_nt(db, wu_ref[0])

        @pl.when(s == N_CHIPS - 1)
        def _():
            xf = x_ref[...]
            dx, dgr = _norm_bwd(dx_ref[...], xf, g_ref[...], _rstd(xf))
            dg_ref[...] += jnp.sum(dgr, axis=0, keepdims=True)
            dx_ref[...] = dh_ref[...] + dx

    tok = pl.BlockSpec((tm, D), lambda i, s: (i, 0))
    hid = pl.BlockSpec((1, tm, F), lambda i, s: (s, i, 0))
    vec = pl.BlockSpec((1, D), lambda i, s: (0, 0))
    hshape = jax.ShapeDtypeStruct((N_CHIPS, T, F), BF16)
    return pl.pallas_call(
        body, name=name, grid=(T // tm, N_CHIPS),
        in_specs=[tok, tok, vec, hid, hid,
                  pl.BlockSpec((1, D, F), lambda i, s: (s, 0, 0)), pl.BlockSpec((1, D, F), lambda i, s: (s, 0, 0)),
                  pl.BlockSpec((1, F, D), lambda i, s: (s, 0, 0))],
        out_specs=[tok, hid, hid, hid, vec],
        out_shape=[jax.ShapeDtypeStruct((T, D), F32), hshape, hshape, hshape, jax.ShapeDtypeStruct((1, D), F32)],
        compiler_params=_params(2),
    )(dh, x, g, a, b, wg, wu, wd)


def _wgrad(a, b, *, a_block, a_map, b_block, b_map, out_shape, o_block, o_map, grid, scale=1.0, name):
    def body(a_ref, b_ref, o_ref):
        @pl.when(pl.program_id(len(grid) - 1) == 0)
        def _():
            o_ref[...] = jnp.zeros_like(o_ref)

        av = a_ref[...]
        bv = b_ref[...]
        av = av.reshape(av.shape[-2:]).astype(BF16)
        bv = bv.reshape(bv.shape[-2:])
        if scale != 1.0:
            bv = scale * bv
        o_ref[...] += _dot_tn(av, bv.astype(BF16)).reshape(o_ref.shape)

    return pl.pallas_call(
        body, name=name, grid=grid,
        in_specs=[pl.BlockSpec(a_block, a_map), pl.BlockSpec(b_block, b_map)],
        out_specs=pl.BlockSpec(o_block, o_map),
        out_shape=jax.ShapeDtypeStruct(out_shape, F32),
        compiler_params=_params(len(grid)),
    )(a, b)


def _wgrad_shard_b(a, b, *, name, scale=1.0, tk=512):
    T, M = a.shape
    S, _, N = b.shape
    return _wgrad(a, b, a_block=(tk, M), a_map=lambda s, k: (k, 0), b_block=(1, tk, N), b_map=lambda s, k: (s, k, 0),
                  out_shape=(S, M, N), o_block=(1, M, N), o_map=lambda s, k: (s, 0, 0), grid=(S, T // tk), scale=scale, name=name)


def _wgrad_shard_a(a, b, *, name, scale=1.0, tk=512):
    S, T, M = a.shape
    N = b.shape[1]
    return _wgrad(a, b, a_block=(1, tk, M), a_map=lambda s, k: (s, k, 0), b_block=(tk, N), b_map=lambda s, k: (k, 0),
                  out_shape=(S, M, N), o_block=(1, M, N), o_map=lambda s, k: (s, 0, 0), grid=(S, T // tk), scale=scale, name=name)


def _wgrad_cols(a, b, n_blocks, *, name, tk=512):
    T, M = a.shape
    N = b.shape[1] // n_blocks
    return _wgrad(a, b, a_block=(tk, M), a_map=lambda s, k: (k, 0), b_block=(tk, N), b_map=lambda s, k: (k, s),
                  out_shape=(n_blocks, M, N), o_block=(1, M, N), o_map=lambda s, k: (s, 0, 0), grid=(n_blocks, T // tk), name=name)


def _rope_tables(T):
    half = HEAD_DIM // 2
    inv = ROPE_THETA ** (-jnp.arange(half, dtype=F32) / half)
    ang = jnp.arange(T, dtype=F32)[:, None] * inv[None, :]
    cos, sin, zero = jnp.cos(ang), jnp.sin(ang), jnp.zeros_like(ang)
    reps = LANES // HEAD_DIM
    return (jnp.tile(jnp.concatenate([cos, cos], axis=1), (1, reps)),
            jnp.tile(jnp.concatenate([-sin, zero], axis=1), (1, reps)),
            jnp.tile(jnp.concatenate([zero, sin], axis=1), (1, reps)))


def _rope(x, cos, sa, sb, sign):
    half = HEAD_DIM // 2
    return x * cos + sign * (pltpu.roll(x, LANES - half, 1) * sa + pltpu.roll(x, half, 1) * sb)


def _mix_in_fwd(h, g, w_in, tables, *, tm=256):
    T, D = h.shape

    def body(h_ref, g_ref, w_ref, cos_ref, sa_ref, sb_ref, u_ref, qkva_ref, qb_ref, kvb_ref, gates_ref):
        hf = h_ref[...]
        u = ((hf * _rstd(hf)) * g_ref[...]).astype(BF16)
        u_ref[...] = u
        qkva_ref[...] = _dot(u, w_ref[:, 0:C_QKVA]).astype(BF16)
        zr = _dot(u, w_ref[:, O_QB:O_QB + C_ROPE])
        cos, sa, sb = cos_ref[...], sa_ref[...], sb_ref[...]
        for j in range(C_ROPE // LANES):
            rj = _rope(zr[:, j * LANES:(j + 1) * LANES], cos, sa, sb, 1.0).astype(BF16)
            if j < C_QB // LANES:
                qb_ref[:, j * LANES:(j + 1) * LANES] = rj
            else:
                kvb_ref[:, 0:C_KB] = rj
        kvb_ref[:, C_KB:2 * C_KB] = _dot(u, w_ref[:, O_VB:O_VB + C_KB]).astype(BF16)
        gates_ref[...] = _dot(u, w_ref[:, O_G:O_G + C_GATES])

    def tok(n):
        return pl.BlockSpec((tm, n), lambda i: (i, 0))

    return pl.pallas_call(
        body, name="mix_in_fwd", grid=(T // tm,),
        in_specs=[tok(D), pl.BlockSpec((1, D), lambda i: (0, 0)), pl.BlockSpec((D, D_IN), lambda i: (0, 0)),
                  tok(LANES), tok(LANES), tok(LANES)],
        out_specs=[tok(D), tok(C_QKVA), tok(C_QB), tok(2 * C_KB), tok(C_GATES)],
        out_shape=[jax.ShapeDtypeStruct((T, D), BF16), jax.ShapeDtypeStruct((T, C_QKVA), BF16),
                   jax.ShapeDtypeStruct((T, C_QB), BF16), jax.ShapeDtypeStruct((T, 2 * C_KB), BF16),
                   jax.ShapeDtypeStruct((T, C_GATES), F32)],
        compiler_params=_params(1),
    )(h, g, w_in, *tables)


def _mix_in_bwd(dqa, dka, dva, dqb, dkb, dvb, dgates, h, g, dres, w_in, tables, *, tm=256):
    T, D = h.shape

    def body(dqa_ref, dka_ref, dva_ref, dqb_ref, dkb_ref, dvb_ref, dgt_ref, h_ref, g_ref, dres_ref, w_ref,
             cos_ref, sa_ref, sb_ref, dz_ref, dh_ref, dg_ref):
        @pl.when(pl.program_id(0) == 0)
        def _():
            dg_ref[...] = jnp.zeros_like(dg_ref)

        na = NA_HEADS * HEAD_DIM
        dz_ref[:, 0:na] = dqa_ref[...].astype(BF16)
        dz_ref[:, na:2 * na] = dka_ref[...].astype(BF16)
        dz_ref[:, 2 * na:3 * na] = dva_ref[...].astype(BF16)
        cos, sa, sb = cos_ref[...], sa_ref[...], sb_ref[...]
        for j in range(C_QB // LANES):
            dz_ref[:, O_QB + j * LANES:O_QB + (j + 1) * LANES] = _rope(
                dqb_ref[:, j * LANES:(j + 1) * LANES], cos, sa, sb, -1.0).astype(BF16)
        dz_ref[:, O_KB:O_KB + C_KB] = _rope(dkb_ref[...], cos, sa, sb, -1.0).astype(BF16)
        dz_ref[:, O_VB:O_VB + C_KB] = dvb_ref[...].astype(BF16)
        dz_ref[:, O_G:O_G + C_GATES] = dgt_ref[...].astype(BF16)
        du = _dot_nt(dz_ref[...], w_ref[...])
        hf = h_ref[...]
        dx, dgr = _norm_bwd(du, hf, g_ref[...], _rstd(hf))
        dg_ref[...] += jnp.sum(dgr, axis=0, keepdims=True)
        dh_ref[...] = dres_ref[...] + dx

    def tok(n):
        return pl.BlockSpec((tm, n), lambda i: (i, 0))

    vec = pl.BlockSpec((1, D), lambda i: (0, 0))
    na = NA_HEADS * HEAD_DIM
    return pl.pallas_call(
        body, name="mix_in_bwd", grid=(T // tm,),
        in_specs=[tok(na), tok(na), tok(na), tok(C_QB), tok(C_KB), tok(C_KB), tok(C_GATES), tok(D), vec, tok(D),
                  pl.BlockSpec((D, D_IN), lambda i: (0, 0)), tok(LANES), tok(LANES), tok(LANES)],
        out_specs=[tok(D_IN), tok(D), vec],
        out_shape=[jax.ShapeDtypeStruct((T, D_IN), BF16), jax.ShapeDtypeStruct((T, D), F32),
                   jax.ShapeDtypeStruct((1, D), F32)],
        compiler_params=_params(1),
    )(dqa, dka, dva, dqb, dkb, dvb, dgates, h, g, dres, w_in, *tables)


def _na_bias_slabs(rpb):
    c = jnp.arange(GRID_W)
    k = jnp.arange(GRID_W)
    cs = jnp.clip(c - NA_KW // 2, 0, GRID_W - NA_KW)
    co = k[None, :] - c[:, None] + (NA_KW - 1)
    inwin = (k[None, :] >= cs[:, None]) & (k[None, :] < cs[:, None] + NA_KW)
    ro = jnp.arange(NA_KH)[:, None] + jnp.arange(NA_KH)[None, :]
    gth = rpb[:, ro[:, :, None, None], jnp.clip(co, 0, 2 * NA_KW - 2)[None, None, :, :]]
    gth = jnp.where(inwin[None, None, None], gth, NEG)
    return gth.transpose(0, 1, 3, 2, 4).reshape(rpb.shape[0], NA_KH, GRID_W, NA_KH * GRID_W)


def _half_masks(rows):
    lane = lax.broadcasted_iota(jnp.int32, (rows, LANES), 1)
    left = lane < HEAD_DIM
    return left, (left, jnp.logical_not(left))


def _keep(mask, x):
    return jnp.where(mask, x.astype(F32), 0.0).astype(BF16)


def _na_row(j, t, rb, rows):
    r = j * rb + t
    rs = jnp.clip(r - NA_KH // 2, 0, rows - NA_KH)
    return pl.multiple_of(t * GRID_W, GRID_W), pl.multiple_of(rs * GRID_W, GRID_W), rs - r + (NA_KH - 1)


def _na_specs(T, rb):
    qrows = GRID_W * rb
    pairs = NA_HEADS // 2
    return ([pl.BlockSpec((qrows, LANES), lambda p, j: (j, p)),
             pl.BlockSpec((T, LANES), lambda p, j: (0, pairs + p)),
             pl.BlockSpec((T, LANES), lambda p, j: (0, 2 * pairs + p))],
            pl.BlockSpec((2, NA_KH, GRID_W, NA_KH * GRID_W), lambda p, j: (p, 0, 0, 0)))


def _na_fwd(qkva, bias, *, rb=8):
    T = qkva.shape[0]
    rows = T // GRID_W
    nkeys = NA_KH * GRID_W

    def body(q_ref, k_ref, v_ref, bias_ref, y_ref):
        j = pl.program_id(1)
        left, halves = _half_masks(GRID_W)

        def row(t, carry):
            q0, k0, ro0 = _na_row(j, t, rb, rows)
            q = q_ref[pl.ds(q0, GRID_W), :]
            ks = k_ref[pl.ds(k0, nkeys), :]
            vs = v_ref[pl.ds(k0, nkeys), :]
            outs = []
            for hh in range(2):
                s = _dot_nt(_keep(halves[hh], q), ks) * QK_SCALE + bias_ref[hh, ro0]
                p = jnp.exp(s - jnp.max(s, axis=-1, keepdims=True))
                p = p / jnp.sum(p, axis=-1, keepdims=True)
                outs.append(_dot(p.astype(BF16), vs))
            y_ref[pl.ds(q0, GRID_W), :] = jnp.where(left, outs[0], outs[1]).astype(BF16)
            return carry

        lax.fori_loop(0, rb, row, 0)

    qkv_specs, bias_spec = _na_specs(T, rb)
    return pl.pallas_call(
        body, name="na_fwd", grid=(NA_HEADS // 2, rows // rb),
        in_specs=qkv_specs + [bias_spec],
        out_specs=qkv_specs[0],
        out_shape=jax.ShapeDtypeStruct((T, NA_HEADS * HEAD_DIM), BF16),
        compiler_params=_params(2),
    )(qkva, qkva, qkva, bias)


def _na_bwd(qkva, dy, bias, *, rb=8):
    T = qkva.shape[0]
    rows = T // GRID_W
    nkeys = NA_KH * GRID_W

    def body(q_ref, k_ref, v_ref, dy_ref, bias_ref, dq_ref, dk_ref, dv_ref, dbias_ref):
        j = pl.program_id(1)
        left, halves = _half_masks(GRID_W)

        @pl.when(j == 0)
        def _():
            dk_ref[...] = jnp.zeros_like(dk_ref)
            dv_ref[...] = jnp.zeros_like(dv_ref)
            dbias_ref[...] = jnp.zeros_like(dbias_ref)

        def row(t, carry):
            q0, k0, ro0 = _na_row(j, t, rb, rows)
            q = q_ref[pl.ds(q0, GRID_W), :]
            dyr = dy_ref[pl.ds(q0, GRID_W), :]
            ks = k_ref[pl.ds(k0, nkeys), :]
            vs = v_ref[pl.ds(k0, nkeys), :]
            dqs = []
            dk = jnp.zeros((nkeys, LANES), F32)
            dv = jnp.zeros((nkeys, LANES), F32)
            for hh in range(2):
                qm = _keep(halves[hh], q)
                dym = _keep(halves[hh], dyr)
                s = _dot_nt(qm, ks) * QK_SCALE + bias_ref[hh, ro0]
                p = jnp.exp(s - jnp.max(s, axis=-1, keepdims=True))
                p = p / jnp.sum(p, axis=-1, keepdims=True)
                dp = _dot_nt(dym, vs)
                ds = p * (dp - jnp.sum(p * dp, axis=-1, keepdims=True))
                dbias_ref[hh, ro0] += ds
                dsb = ds.astype(BF16)
                dqs.append(_dot(dsb, ks) * QK_SCALE)
                dk = dk + _dot_tn(dsb, qm) * QK_SCALE
                dv = dv + _dot_tn(p.astype(BF16), dym)
            dq_ref[pl.ds(q0, GRID_W), :] = jnp.where(left, dqs[0], dqs[1]).astype(BF16)
            dk_ref[pl.ds(k0, nkeys), :] += dk
            dv_ref[pl.ds(k0, nkeys), :] += dv
            return carry

        lax.fori_loop(0, rb, row, 0)

    qkv_specs, bias_spec = _na_specs(T, rb)
    width = NA_HEADS * HEAD_DIM
    kv_out = pl.BlockSpec((T, LANES), lambda p, j: (0, p))
    return pl.pallas_call(
        body, name="na_bwd", grid=(NA_HEADS // 2, rows // rb),
        in_specs=qkv_specs + [qkv_specs[0], bias_spec],
        out_specs=[qkv_specs[0], kv_out, kv_out, bias_spec],
        out_shape=[jax.ShapeDtypeStruct((T, width), BF16), jax.ShapeDtypeStruct((T, width), F32),
                   jax.ShapeDtypeStruct((T, width), F32), jax.ShapeDtypeStruct(bias.shape, F32)],
        compiler_params=_params(2),
    )(qkva, qkva, qkva, dy, bias)


def _rpb_fold(dslab):
    H = dslab.shape[0]
    nro = NA_KH * NA_KH
    ncell = GRID_W * GRID_W
    xs = dslab.reshape(H, NA_KH, GRID_W, NA_KH, GRID_W).transpose(0, 1, 3, 2, 4).reshape(H, nro, ncell)
    cell = jnp.arange(ncell)
    co = cell % GRID_W - cell // GRID_W + (NA_KW - 1)
    e_co = (co[:, None] == jnp.arange(LANES)[None, :]).astype(F32)
    pair = jnp.arange(nro)
    e_ro = ((pair // NA_KH + pair % NA_KH)[None, :] == jnp.arange(2 * NA_KH)[:, None]).astype(F32)

    def body(x_ref, eco_ref, ero_ref, o_ref):
        y = jnp.dot(x_ref[0], eco_ref[...], preferred_element_type=F32, precision=lax.Precision.HIGHEST)
        o_ref[0] = jnp.dot(ero_ref[...], y, preferred_element_type=F32, precision=lax.Precision.HIGHEST)

    return pl.pallas_call(
        body, name="rpb_fold", grid=(H,),
        in_specs=[pl.BlockSpec((1, nro, ncell), lambda h: (h, 0, 0)), pl.BlockSpec((ncell, LANES), lambda h: (0, 0)),
                  pl.BlockSpec((2 * NA_KH, nro), lambda h: (0, 0))],
        out_specs=pl.BlockSpec((1, 2 * NA_KH, LANES), lambda h: (h, 0, 0)),
        out_shape=jax.ShapeDtypeStruct((H, 2 * NA_KH, LANES), F32),
        compiler_params=_params(1),
    )(xs, e_co, e_ro)


SWA_KEYS = 3 * WIN


def _swa_block(j, t, qbn, T):
    blk = j * qbn + t
    start = jnp.clip((blk - 1) * WIN, 0, T - SWA_KEYS)
    qpos = blk * WIN + lax.broadcasted_iota(jnp.int32, (WIN, SWA_KEYS), 0)
    kpos = start + lax.broadcasted_iota(jnp.int32, (WIN, SWA_KEYS), 1)
    return pl.multiple_of(t * WIN, WIN), pl.multiple_of(start, WIN), jnp.abs(qpos - kpos) <= WIN


def _swa_probs(qm, ks, mask, sink):
    s = jnp.where(mask, _dot_nt(qm, ks) * QK_SCALE, NEG)
    m = jnp.maximum(jnp.max(s, axis=-1, keepdims=True), sink)
    e = jnp.exp(s - m)
    esink = jnp.exp(sink - m)
    den = jnp.sum(e, axis=-1, keepdims=True) + esink
    return e / den, esink / den


def _swa_specs(T, qbn):
    return [pl.BlockSpec(memory_space=pltpu.SMEM),
            pl.BlockSpec((WIN * qbn, LANES), lambda p, j: (j, p)),
            pl.BlockSpec((T, LANES), lambda p, j: (0, 0)),
            pl.BlockSpec((T, LANES), lambda p, j: (0, 1))]


def _swa_fwd(qb, kvb, sink, *, qbn=4):
    T = qb.shape[0]
    pairs = NB_HEADS // 2

    def body(sink_ref, q_ref, k_ref, v_ref, y_ref):
        p, j = pl.program_id(0), pl.program_id(1)
        left, halves = _half_masks(WIN)

        def block(t, carry):
            q0, k0, mask = _swa_block(j, t, qbn, T)
            q = q_ref[pl.ds(q0, WIN), :]
            ks = k_ref[pl.ds(k0, SWA_KEYS), :]
            vs = v_ref[pl.ds(k0, SWA_KEYS), :]
            outs = []
            for hh in range(2):
                pr, _ = _swa_probs(_keep(halves[hh], q), ks, mask, sink_ref[p + pairs * hh])
                outs.append(_dot(pr.astype(BF16), vs))
            y_ref[pl.ds(q0, WIN), :] = jnp.where(left, outs[0], outs[1]).astype(BF16)
            return carry

        lax.fori_loop(0, qbn, block, 0)

    specs = _swa_specs(T, qbn)
    return pl.pallas_call(
        body, name="swa_fwd", grid=(pairs, T // (WIN * qbn)),
        in_specs=specs, out_specs=specs[1],
        out_shape=jax.ShapeDtypeStruct((T, NB_HEADS * HEAD_DIM), BF16),
        compiler_params=_params(2),
    )(sink, qb, kvb, kvb)


def _swa_bwd(qb, kvb, dy, sink, *, qbn=4):
    T = qb.shape[0]
    pairs = NB_HEADS // 2

    def body(sink_ref, q_ref, k_ref, v_ref, dy_ref, dq_ref, dk_ref, dv_ref, dsink_ref):
        p, j = pl.program_id(0), pl.program_id(1)
        left, halves = _half_masks(WIN)

        @pl.when((p == 0) & (j == 0))
        def _():
            dk_ref[...] = jnp.zeros_like(dk_ref)
            dv_ref[...] = jnp.zeros_like(dv_ref)

        @pl.when(j == 0)
        def _():
            dsink_ref[...] = jnp.zeros_like(dsink_ref)

        def block(t, carry):
            q0, k0, mask = _swa_block(j, t, qbn, T)
            q = q_ref[pl.ds(q0, WIN), :]
            dyb = dy_ref[pl.ds(q0, WIN), :]
            ks = k_ref[pl.ds(k0, SWA_KEYS), :]
            vs = v_ref[pl.ds(k0, SWA_KEYS), :]
            dqs = []
            dk = jnp.zeros((SWA_KEYS, LANES), F32)
            dv = jnp.zeros((SWA_KEYS, LANES), F32)
            for hh in range(2):
                qm = _keep(halves[hh], q)
                dym = _keep(halves[hh], dyb)
                pr, psink = _swa_probs(qm, ks, mask, sink_ref[p + pairs * hh])
                dp = _dot_nt(dym, vs)
                delta = jnp.sum(pr * dp, axis=-1, keepdims=True)
                ds = pr * (dp - delta)
                dsink_ref[0, hh:hh + 1, :] += jnp.broadcast_to(-jnp.sum(psink * delta), (1, LANES))
                dsb = ds.astype(BF16)
                dqs.append(_dot(dsb, ks) * QK_SCALE)
                dk = dk + _dot_tn(dsb, qm) * QK_SCALE
                dv = dv + _dot_tn(pr.astype(BF16), dym)
            dq_ref[pl.ds(q0, WIN), :] = jnp.where(left, dqs[0], dqs[1])
            dk_ref[pl.ds(k0, SWA_KEYS), :] += dk
            dv_ref[pl.ds(k0, SWA_KEYS), :] += dv
            return carry

        lax.fori_loop(0, qbn, block, 0)

    specs = _swa_specs(T, qbn)
    kv_out = pl.BlockSpec((T, LANES), lambda p, j: (0, 0))
    return pl.pallas_call(
        body, name="swa_bwd", grid=(pairs, T // (WIN * qbn)),
        in_specs=specs + [specs[1]],
        out_specs=[specs[1], kv_out, kv_out, pl.BlockSpec((1, 8, LANES), lambda p, j: (p, 0, 0))],
        out_shape=[jax.ShapeDtypeStruct((T, NB_HEADS * HEAD_DIM), F32), jax.ShapeDtypeStruct((T, LANES), F32),
                   jax.ShapeDtypeStruct((T, LANES), F32), jax.ShapeDtypeStruct((pairs, 8, LANES), F32)],
        compiler_params=_params(2),
    )(sink, qb, kvb, kvb, dy)


def _merge_fwd(ya, yb, gates, wa, wb, wout, h, *, tm=512):
    T, D = h.shape
    W = ya.shape[1]

    def body(ya_ref, yb_ref, gt_ref, wa_ref, wb_ref, wo_ref, h_ref, h2_ref, mg_ref):
        pa = _dot(ya_ref[...], wa_ref[...])
        pb = _dot(yb_ref[...], wb_ref[...])
        mg = (jax.nn.sigmoid(gt_ref[:, 0:D]) * pa + jax.nn.sigmoid(gt_ref[:, D:2 * D]) * pb).astype(BF16)
        mg_ref[...] = mg
        h2_ref[...] = h_ref[...] + _dot(mg, wo_ref[...])

    def tok(n):
        return pl.BlockSpec((tm, n), lambda i: (i, 0))

    def full(r, c):
        return pl.BlockSpec((r, c), lambda i: (0, 0))

    return pl.pallas_call(
        body, name="merge_fwd", grid=(T // tm,),
        in_specs=[tok(W), tok(W), tok(2 * D), full(W, D), full(W, D), full(D, D), tok(D)],
        out_specs=[tok(D), tok(D)],
        out_shape=[jax.ShapeDtypeStruct((T, D), F32), jax.ShapeDtypeStruct((T, D), BF16)],
        compiler_params=_params(1),
    )(ya, yb, gates, wa, wb, wout, h)


def _merge_bwd(dh, ya, yb, gates, wa, wb, wout, *, tm=512):
    T, D = dh.shape
    W = ya.shape[1]

    def body(dh_ref, ya_ref, yb_ref, gt_ref, wa_ref, wb_ref, wo_ref, dya_ref, dyb_ref, dpa_ref, dpb_ref, dgt_ref):
        dmg = _dot_nt(dh_ref[...].astype(BF16), wo_ref[...])
        for y_ref, w_ref, dy_ref, dp_ref, lo in ((ya_ref, wa_ref, dya_ref, dpa_ref, 0), (yb_ref, wb_ref, dyb_ref, dpb_ref, D)):
            sg = jax.nn.sigmoid(gt_ref[:, lo:lo + D])
            dp = (dmg * sg).astype(BF16)
            dp_ref[...] = dp
            dgt_ref[:, lo:lo + D] = (dmg * _dot(y_ref[...], w_ref[...]) * (sg * (1.0 - sg))).astype(BF16)
            dy_ref[...] = _dot_nt(dp, w_ref[...]).astype(BF16)

    def tok(n):
        return pl.BlockSpec((tm, n), lambda i: (i, 0))

    def full(r, c):
        return pl.BlockSpec((r, c), lambda i: (0, 0))

    return pl.pallas_call(
        body, name="merge_bwd", grid=(T // tm,),
        in_specs=[tok(D), tok(W), tok(W), tok(2 * D), full(W, D), full(W, D), full(D, D)],
        out_specs=[tok(W), tok(W), tok(D), tok(D), tok(2 * D)],
        out_shape=[jax.ShapeDtypeStruct((T, W), BF16), jax.ShapeDtypeStruct((T, W), BF16),
                   jax.ShapeDtypeStruct((T, D), BF16), jax.ShapeDtypeStruct((T, D), BF16),
                   jax.ShapeDtypeStruct((T, 2 * D), BF16)],
        compiler_params=_params(1),
    )(dh, ya, yb, gates, wa, wb, wout)


def _final_loss(h, g, target, *, tm=512):
    T, D = h.shape

    def body(h_ref, g_ref, t_ref, dh_ref, loss_ref, dg_ref):
        @pl.when(pl.program_id(0) == 0)
        def _():
            loss_ref[...] = jnp.zeros_like(loss_ref)
            dg_ref[...] = jnp.zeros_like(dg_ref)

        hf = h_ref[...]
        r = _rstd(hf)
        gv = g_ref[...]
        err = (hf * r) * gv - t_ref[...]
        loss_ref[...] += jnp.broadcast_to(0.5 * jnp.sum(jnp.mean(err * err, axis=-1, keepdims=True)), loss_ref.shape)
        dx, dgr = _norm_bwd(err * (1.0 / D), hf, gv, r)
        dg_ref[...] += jnp.sum(dgr, axis=0, keepdims=True)
        dh_ref[...] = dx

    tok = pl.BlockSpec((tm, D), lambda i: (i, 0))
    vec = pl.BlockSpec((1, D), lambda i: (0, 0))
    return pl.pallas_call(
        body, name="final_loss", grid=(T // tm,),
        in_specs=[tok, vec, tok],
        out_specs=[tok, pl.BlockSpec((1, LANES), lambda i: (0, 0)), vec],
        out_shape=[jax.ShapeDtypeStruct((T, D), F32), jax.ShapeDtypeStruct((1, LANES), F32),
                   jax.ShapeDtypeStruct((1, D), F32)],
        compiler_params=_params(1),
    )(h, g, target)


def _pair_heads(a, axis):
    shp = a.shape
    a = a.reshape(shp[:axis] + (2, NB_HEADS // 2, HEAD_DIM) + shp[axis + 1:])
    return jnp.swapaxes(a, axis, axis + 1).reshape(shp)


def _unpair_heads(a, axis):
    shp = a.shape
    a = a.reshape(shp[:axis] + (NB_HEADS // 2, 2, HEAD_DIM) + shp[axis + 1:])
    return jnp.swapaxes(a, axis, axis + 1).reshape(shp)


def _layer_grads(x, target, g1, f1, gmix, w_in, rpb, sink, wa, wb, wout, g2, f2, gfin):
    T = x.shape[0]
    tables = _rope_tables(T)
    w_in_p = jnp.concatenate([w_in[:, :O_QB], _pair_heads(w_in[:, O_QB:O_KB], 1), w_in[:, O_KB:]], axis=1)
    wb_p = _pair_heads(wb, 0)
    bias = _na_bias_slabs(rpb)

    h1, n1, a1, b1 = _ffn_fwd(x, g1, *f1, name="ffn1_fwd")
    u, qkva, qb, kvb, gates = _mix_in_fwd(h1, gmix, w_in_p, tables)
    ya = _na_fwd(qkva, bias)
    yb = _swa_fwd(qb, kvb, sink)
    h2, merged = _merge_fwd(ya, yb, gates, wa, wb_p, wout, h1)
    h3, n2, a2, b2 = _ffn_fwd(h2, g2, *f2, name="ffn2_fwd")
    dh3, loss, dgfin = _final_loss(h3, gfin, target)

    dh2, da2, db2, hdn2, dg2 = _ffn_bwd(dh3, h2, g2, a2, b2, *f2, name="ffn2_bwd")
    df2 = (_wgrad_shard_b(n2, da2, name="ffn2_dwg"), _wgrad_shard_b(n2, db2, name="ffn2_dwu"),
           _wgrad_shard_a(hdn2, dh3, scale=0.5, name="ffn2_dwd"))
    dya, dyb, dpa, dpb, dgates = _merge_bwd(dh2, ya, yb, gates, wa, wb_p, wout)
    dwout = _wgrad_cols(merged, dh2, 1, name="dwout")[0]
    dwa = _wgrad_cols(ya, dpa, 1, name="dwa")[0]
    dwb = _unpair_heads(_wgrad_cols(yb, dpb, 1, name="dwb")[0], 0)
    dqa, dka, dva, dbias = _na_bwd(qkva, dya, bias)
    drpb = _rpb_fold(dbias)
    dqb, dkb, dvb, dsink = _swa_bwd(qb, kvb, dyb, sink)
    dz, dh1, dgmix = _mix_in_bwd(dqa, dka, dva, dqb, dkb, dvb, dgates, h1, gmix, dh2, w_in_p, tables)
    dwin_p = _wgrad_cols(u, dz, 2, name="dwin")
    dwin_p = dwin_p.transpose(1, 0, 2).reshape(D_MODEL, D_IN)
    dwin = jnp.concatenate([dwin_p[:, :O_QB], _unpair_heads(dwin_p[:, O_QB:O_KB], 1), dwin_p[:, O_KB:]], axis=1)
    dx, da1, db1, hdn1, dg1 = _ffn_bwd(dh1, x, g1, a1, b1, *f1, name="ffn1_bwd")
    df1 = (_wgrad_shard_b(n1, da1, name="ffn1_dwg"), _wgrad_shard_b(n1, db1, name="ffn1_dwu"),
           _wgrad_shard_a(hdn1, dh1, scale=0.5, name="ffn1_dwd"))
    dsink_v = dsink[:, 0:2, 0].T.reshape(NB_HEADS)
    return dict(loss=loss, dx=dx, g1=dg1, f1=df1, gmix=dgmix, w_in=dwin, rpb=drpb, sink=dsink_v,
                wa=dwa, wb=dwb, wout=dwout, g2=dg2, f2=df2, gfin=dgfin)


ANY = pl.BlockSpec(memory_space=pl.ANY)


def _place():
    x, y, c = lax.axis_index("x"), lax.axis_index("y"), lax.axis_index("c")
    chips = [(1 - x, y), (x, 1 - y), (1 - x, 1 - y)]
    return x, y, c, 2 * x + y, chips


def _remote(src, dst, send_sems, recv_sems, k, device):
    return pltpu.make_async_remote_copy(src_ref=src, dst_ref=dst, send_sem=send_sems.at[k], recv_sem=recv_sems.at[k],
                                        device_id=device, device_id_type=MESH)


def _all_gather(shards):
    n = len(shards)

    def body(*refs):
        ins, outs = refs[:n], refs[n:2 * n]
        send_sems, recv_sems, local_sems = refs[2 * n:]
        x, y, c, mine, chips = _place()
        sibling = (x, y, 1 - c)
        local, sends, passes = [], [], []
        for i in range(n):
            hr = shards[i].shape[0] // 2
            cp = pltpu.make_async_copy(ins[i], outs[i].at[mine], local_sems.at[i])
            cp.start()
            local.append(cp)
            for j, (cx, cy) in enumerate(chips):
                cp = _remote(ins[i].at[pl.ds(c * hr, hr)], outs[i].at[mine, pl.ds(c * hr, hr)], send_sems, recv_sems,
                             6 * i + j, (cx, cy, c))
                cp.start()
                sends.append(cp)
        for i in range(n):
            hr = shards[i].shape[0] // 2
            for j, (cx, cy) in enumerate(chips):
                landed = outs[i].at[2 * cx + cy, pl.ds(c * hr, hr)]
                _remote(landed, landed, send_sems, recv_sems, 6 * i + j, (cx, cy, c)).wait_recv()
                cp = _remote(landed, landed, send_sems, recv_sems, 6 * i + 3 + j, sibling)
                cp.start()
                passes.append(cp)
        for i in range(n):
            hr = shards[i].shape[0] // 2
            for j, (cx, cy) in enumerate(chips):
                other = outs[i].at[2 * cx + cy, pl.ds((1 - c) * hr, hr)]
                _remote(other, other, send_sems, recv_sems, 6 * i + 3 + j, sibling).wait_recv()
        for cp in sends + passes:
            cp.wait_send()
        for cp in local:
            cp.wait()

    return pl.pallas_call(
        body, name="all_gather_weights",
        in_specs=[ANY] * n, out_specs=[ANY] * n,
        out_shape=[jax.ShapeDtypeStruct((N_CHIPS,) + s.shape, s.dtype) for s in shards],
        scratch_shapes=[pltpu.SemaphoreType.DMA((6 * n,)), pltpu.SemaphoreType.DMA((6 * n,)), pltpu.SemaphoreType.DMA((n,))],
    )(*shards)


def _rs_sibling(grads):
    n = len(grads)

    def body(*refs):
        ins, outs = refs[:n], refs[n:2 * n]
        send_sems, recv_sems = refs[2 * n:]
        x, y, c, _, _ = _place()
        copies = []
        for i in range(n):
            hr = grads[i].shape[1] // 2
            cp = _remote(ins[i].at[:, pl.ds((1 - c) * hr, hr)], outs[i], send_sems, recv_sems, i, (x, y, 1 - c))
            cp.start()
            copies.append(cp)
        for cp in copies:
            cp.wait()

    return pl.pallas_call(
        body, name="rs_sibling",
        in_specs=[ANY] * n, out_specs=[ANY] * n,
        out_shape=[jax.ShapeDtypeStruct((g.shape[0], g.shape[1] // 2, g.shape[2]), g.dtype) for g in grads],
        scratch_shapes=[pltpu.SemaphoreType.DMA((n,)), pltpu.SemaphoreType.DMA((n,))],
    )(*grads)


def _rs_chips(parts):
    n = len(parts)

    def body(*refs):
        ins, outs = refs[:n], refs[n:2 * n]
        send_sems, recv_sems = refs[2 * n:]
        _, _, c, _, chips = _place()
        copies = []
        for i in range(n):
            for j, (cx, cy) in enumerate(chips):
                cp = _remote(ins[i].at[2 * cx + cy], outs[i].at[j], send_sems, recv_sems, 3 * i + j, (cx, cy, c))
                cp.start()
                copies.append(cp)
        for cp in copies:
            cp.wait()

    return pl.pallas_call(
        body, name="rs_chips",
        in_specs=[ANY] * n, out_specs=[ANY] * n,
        out_shape=[jax.ShapeDtypeStruct((N_CHIPS - 1,) + p.shape[1:], p.dtype) for p in parts],
        scratch_shapes=[pltpu.SemaphoreType.DMA((3 * n,)), pltpu.SemaphoreType.DMA((3 * n,))],
    )(*parts)


def _rs_share(halves):
    n = len(halves)

    def body(*refs):
        ins, outs = refs[:n], refs[n:2 * n]
        send_sems, recv_sems, local_sems = refs[2 * n:]
        x, y, c, _, _ = _place()
        copies = []
        for i in range(n):
            hr = halves[i].shape[0]
            rows = outs[i].at[pl.ds(c * hr, hr)]
            cp = pltpu.make_async_copy(ins[i], rows, local_sems.at[i])
            cp.start()
            copies.append(cp)
            cp = _remote(ins[i], rows, send_sems, recv_sems, i, (x, y, 1 - c))
            cp.start()
            copies.append(cp)
        for cp in copies:
            cp.wait()

    return pl.pallas_call(
        body, name="rs_share",
        in_specs=[ANY] * n, out_specs=[ANY] * n,
        out_shape=[jax.ShapeDtypeStruct((2 * h.shape[0], h.shape[1]), h.dtype) for h in halves],
        scratch_shapes=[pltpu.SemaphoreType.DMA((n,)), pltpu.SemaphoreType.DMA((n,)), pltpu.SemaphoreType.DMA((n,))],
    )(*halves)


N_DEV = 8


def _small_allreduce(vec):
    R = vec.shape[0]

    def body(v_ref, o_ref, buf, send_sems, recv_sems):
        x, y, c, _, _ = _place()
        me = 4 * x + 2 * y + c
        buf[me] = v_ref[...]
        copies = []
        for k in range(1, N_DEV):
            peer = (x ^ (k >> 2), y ^ ((k >> 1) & 1), c ^ (k & 1))
            cp = _remote(v_ref, buf.at[me], send_sems, recv_sems, k - 1, peer)
            cp.start()
            copies.append(cp)
        for k, cp in enumerate(copies, start=1):
            cp.wait_send()
            landed = buf.at[me ^ k]
            _remote(landed, landed, send_sems, recv_sems, k - 1, (x, y, c)).wait_recv()
        acc = buf[0]
        for d in range(1, N_DEV):
            acc = acc + buf[d]
        o_ref[...] = acc

    return pl.pallas_call(
        body, name="small_allreduce",
        in_specs=[pl.BlockSpec(memory_space=pltpu.VMEM)], out_specs=pl.BlockSpec(memory_space=pltpu.VMEM),
        out_shape=jax.ShapeDtypeStruct(vec.shape, vec.dtype),
        scratch_shapes=[pltpu.VMEM((N_DEV, R, LANES), F32), pltpu.SemaphoreType.DMA((N_DEV - 1,)),
                        pltpu.SemaphoreType.DMA((N_DEV - 1,))],
    )(vec)


ELEMWISE_BLOCK = 256 * 1024


def _row_tile(rows, cols):
    best = None
    for t in range(8, rows + 1, 8):
        if rows % t == 0 and t * cols <= ELEMWISE_BLOCK:
            best = t
    return best if best is not None else rows


def _add_sibling(g, r1, cidx, *, name):
    S, R, C = g.shape
    hr = R // 2
    tr = _row_tile(hr, C)
    nt = hr // tr

    def body(c_ref, g_ref, r_ref, o_ref):
        o_ref[...] = g_ref[...] + r_ref[...]

    blk = pl.BlockSpec((1, tr, C), lambda s, t, c: (s, t, 0))
    return pl.pallas_call(
        body, name=name,
        grid_spec=pltpu.PrefetchScalarGridSpec(
            num_scalar_prefetch=1, grid=(S, nt),
            in_specs=[pl.BlockSpec((1, tr, C), lambda s, t, c: (s, c[0] * nt + t, 0)), blk], out_specs=blk),
        out_shape=jax.ShapeDtypeStruct((S, hr, C), F32),
        compiler_params=_params(2),
    )(cidx, g, r1)


def _add_chips(p, r2, chip, *, name):
    _, hr, C = p.shape
    tr = _row_tile(hr, C)

    def body(chip_ref, p_ref, r_ref, o_ref):
        o_ref[...] = ((p_ref[0] + r_ref[0]) + r_ref[1]) + r_ref[2]

    return pl.pallas_call(
        body, name=name,
        grid_spec=pltpu.PrefetchScalarGridSpec(
            num_scalar_prefetch=1, grid=(hr // tr,),
            in_specs=[pl.BlockSpec((1, tr, C), lambda t, s: (s[0], t, 0)), pl.BlockSpec((N_CHIPS - 1, tr, C), lambda t, s: (0, t, 0))],
            out_specs=pl.BlockSpec((tr, C), lambda t, s: (t, 0))),
        out_shape=jax.ShapeDtypeStruct((hr, C), F32),
        compiler_params=_params(1),
    )(chip, p, r2)


def _adamw(w, g, m, v, *, name):
    R, C = w.shape
    tr = _row_tile(R, C)

    def body(w_ref, g_ref, m_ref, v_ref, d_ref, mo_ref, vo_ref):
        gv = g_ref[...]
        mn = ADAM_B1 * m_ref[...] + (1.0 - ADAM_B1) * gv
        vn = ADAM_B2 * v_ref[...] + (1.0 - ADAM_B2) * (gv * gv)
        mo_ref[...] = mn
        vo_ref[...] = vn
        m_hat = mn / (1.0 - ADAM_B1 ** ADAM_STEP)
        v_hat = vn / (1.0 - ADAM_B2 ** ADAM_STEP)
        d_ref[...] = -ADAM_LR * (m_hat / (jnp.sqrt(v_hat) + ADAM_EPS) + ADAM_WD * w_ref[...])

    blk = pl.BlockSpec((tr, C), lambda t: (t, 0))
    shape = jax.ShapeDtypeStruct((R, C), F32)
    return pl.pallas_call(
        body, name=name, grid=(R // tr,),
        in_specs=[blk] * 4, out_specs=[blk] * 3, out_shape=[shape] * 3,
        compiler_params=_params(1),
    )(w, g, m, v)


def _stack_cols(w):
    r, n = w.shape
    return w.reshape(r, N_CHIPS, n // N_CHIPS).transpose(1, 0, 2)


def _unstack_cols(w):
    s, r, c = w.shape
    return w.transpose(1, 0, 2).reshape(r, s * c)


def _pad_rows(a, rows):
    return jnp.pad(a, ((0, rows - a.shape[0]), (0, LANES - a.shape[1])))


BIG = ("ffn1_w_gate", "ffn1_w_up", "ffn1_w_down", "w_in", "w_branch_a", "w_branch_b", "w_out",
       "ffn2_w_gate", "ffn2_w_up", "ffn2_w_down")
SMALL = ("ffn1_norm", "mix_norm", "na_rpb", "sink_logit", "ffn2_norm", "final_norm")
WEIGHTS = ("ffn1_norm", "ffn1_w_gate", "ffn1_w_up", "ffn1_w_down", "mix_norm", "w_in", "na_rpb", "sink_logit",
           "w_branch_a", "w_branch_b", "w_out", "ffn2_norm", "ffn2_w_gate", "ffn2_w_up", "ffn2_w_down", "final_norm")


def kernel(x, ffn1_norm, ffn1_w_gate, ffn1_w_up, ffn1_w_down, mix_norm, w_in, na_rpb, sink_logit, w_branch_a, w_branch_b, w_out, ffn2_norm, ffn2_w_gate, ffn2_w_up, ffn2_w_down, final_norm, loss_target, m_ffn1_norm, m_ffn1_w_gate, m_ffn1_w_up, m_ffn1_w_down, m_mix_norm, m_w_in, m_na_rpb, m_sink_logit, m_w_branch_a, m_w_branch_b, m_w_out, m_ffn2_norm, m_ffn2_w_gate, m_ffn2_w_up, m_ffn2_w_down, m_final_norm, v_ffn1_norm, v_ffn1_w_gate, v_ffn1_w_up, v_ffn1_w_down, v_mix_norm, v_w_in, v_na_rpb, v_sink_logit, v_w_branch_a, v_w_branch_b, v_w_out, v_ffn2_norm, v_ffn2_w_gate, v_ffn2_w_up, v_ffn2_w_down, v_final_norm):
    args = dict(locals())
    w = {k: args[k] for k in WEIGHTS}
    mom = {k: args["m_" + k] for k in WEIGHTS}
    var = {k: args["v_" + k] for k in WEIGHTS}
    cidx = lax.axis_index("c").astype(jnp.int32).reshape(1)
    chip = (2 * lax.axis_index("x") + lax.axis_index("y")).astype(jnp.int32).reshape(1)

    full = dict(zip(BIG, _all_gather([w[k][0].astype(BF16) for k in BIG])))
    f1 = (full["ffn1_w_gate"], full["ffn1_w_up"], full["ffn1_w_down"])
    f2 = (full["ffn2_w_gate"], full["ffn2_w_up"], full["ffn2_w_down"])
    out = _layer_grads(
        x[0], loss_target[0], ffn1_norm, f1, mix_norm, _unstack_cols(full["w_in"]), na_rpb[0], sink_logit[0],
        _unstack_cols(full["w_branch_a"]), _unstack_cols(full["w_branch_b"]), full["w_out"].reshape(D_MODEL, D_MODEL),
        ffn2_norm, f2, final_norm.reshape(1, D_MODEL))

    by_chip = dict(zip(BIG, [out["f1"][0], out["f1"][1], out["f1"][2], _stack_cols(out["w_in"]), _stack_cols(out["wa"]),
                             _stack_cols(out["wb"]), out["wout"].reshape(N_CHIPS, D_MODEL // N_CHIPS, D_MODEL),
                             out["f2"][0], out["f2"][1], out["f2"][2]]))
    grads = [by_chip[k] for k in BIG]
    from_sibling = _rs_sibling(grads)
    parts = [_add_sibling(g, r, cidx, name="add_sibling_" + k) for k, g, r in zip(BIG, grads, from_sibling)]
    from_chips = _rs_chips(parts)
    halves = [_add_chips(p, r, chip, name="add_chips_" + k) for k, p, r in zip(BIG, parts, from_chips)]
    grad = dict(zip(BIG, _rs_share(halves)))

    rows = D_MODEL // LANES
    small = jnp.concatenate([
        out["g1"].reshape(rows, LANES), out["gmix"].reshape(rows, LANES), out["g2"].reshape(rows, LANES),
        out["gfin"].reshape(rows, LANES), out["rpb"].reshape(-1, LANES),
        _pad_rows(out["sink"].reshape(1, NB_HEADS), 8), _pad_rows(out["loss"], 8)], axis=0)
    total = _small_allreduce(small)
    n_rpb = NA_HEADS * 2 * NA_KH
    grad["ffn1_norm"] = total[0:rows].reshape(1, D_MODEL)
    grad["mix_norm"] = total[rows:2 * rows].reshape(1, D_MODEL)
    grad["ffn2_norm"] = total[2 * rows:3 * rows].reshape(1, D_MODEL)
    grad["final_norm"] = total[3 * rows:4 * rows].reshape(1, D_MODEL)
    grad["na_rpb"] = total[4 * rows:4 * rows + n_rpb].reshape(NA_HEADS, 2 * NA_KH, LANES)[:, :2 * NA_KH - 1, :2 * NA_KW - 1]
    grad["na_rpb"] = grad["na_rpb"].reshape(NA_HEADS, -1)
    grad["sink_logit"] = total[4 * rows + n_rpb:4 * rows + n_rpb + 1, 0:NB_HEADS]
    loss = total[4 * rows + n_rpb + 8, 0]

    deltas, new_m, new_v, grads_out = {}, {}, {}, {}
    for k in WEIGHTS:
        shape = w[k].shape
        g2d = grad[k]
        d, mn, vn = _adamw(w[k].reshape(g2d.shape), g2d, mom[k].reshape(g2d.shape), var[k].reshape(g2d.shape), name="adamw_" + k)
        grads_out[k], deltas[k], new_m[k], new_v[k] = (a.reshape(shape) for a in (g2d, d, mn, vn))
    return (loss, out["dx"].reshape(x.shape), *[grads_out[k] for k in WEIGHTS], *[deltas[k] for k in WEIGHTS],
            *[new_m[k] for k in WEIGHTS], *[new_v[k] for k in WEIGHTS])
```

```python
import functools
import math

import jax
import jax.numpy as jnp
from jax import lax
from jax.experimental import pallas as pl
from jax.experimental.pallas import tpu as pltpu

F32 = jnp.float32
BF16 = jnp.bfloat16

D_MODEL = 1024
HEAD_DIM = 64
NA_HEADS = 8
NB_HEADS = 8
GRID_W = 64
NA_KH = 8
NA_KW = 16
WIN = 128
ROPE_THETA = 10000.0
EPS = 1e-6
N_CHIPS = 4
QK_SCALE = HEAD_DIM ** -0.5
NEG = -1e30
LANES = 128
VMEM_LIMIT = 56 * 1024 * 1024

C_QKVA = 3 * NA_HEADS * HEAD_DIM
C_QB = NB_HEADS * HEAD_DIM
C_KB = 2 * HEAD_DIM
C_ROPE = C_QB + C_KB
C_GATES = 2 * D_MODEL
D_IN = C_QKVA + C_QB + 2 * C_KB + C_GATES
O_QB = C_QKVA
O_KB = O_QB + C_QB
O_VB = O_KB + C_KB
O_G = O_VB + C_KB

ADAM_LR = 0.001
ADAM_B1 = 0.9
ADAM_B2 = 0.999
ADAM_EPS = 1e-08
ADAM_WD = 0.01
ADAM_STEP = 10

MESH = pl.DeviceIdType.MESH


def _dot(a, b):
    return jnp.dot(a, b, preferred_element_type=F32)


def _dot_nt(a, b):
    return lax.dot_general(a, b, (((1,), (1,)), ((), ())), preferred_element_type=F32)


def _dot_tn(a, b):
    return lax.dot_general(a, b, (((0,), (0,)), ((), ())), preferred_element_type=F32)


def _params(n_axes):
    return pltpu.CompilerParams(dimension_semantics=("arbitrary",) * n_axes, vmem_limit_bytes=VMEM_LIMIT)


def _rstd(xf):
    return lax.rsqrt(jnp.mean(xf * xf, axis=-1, keepdims=True) + EPS)


def _norm_bwd(dn, xf, g, r):
    xhat = xf * r
    dxh = dn * g
    dx = r * (dxh - xhat * jnp.mean(dxh * xhat, axis=-1, keepdims=True))
    return dx, dn * xhat


def _ffn_fwd(x, g, wg, wu, wd, *, name, tm=512):
    T, D = x.shape
    F = wg.shape[2]

    def body(x_ref, g_ref, wg_ref, wu_ref, wd_ref, h_ref, n_ref, a_ref, b_ref):
        s = pl.program_id(1)

        @pl.when(s == 0)
        def _():
            xf = x_ref[...]
            n_ref[...] = ((xf * _rstd(xf)) * g_ref[...]).astype(BF16)
            h_ref[...] = xf

        n = n_ref[...]
        a = _dot(n, wg_ref[0])
        b = _dot(n, wu_ref[0])
        a_ref[0] = a.astype(BF16)
        b_ref[0] = b.astype(BF16)
        hdn = (a * jax.nn.sigmoid(a) * b).astype(BF16)
        h_ref[...] += 0.5 * _dot(hdn, wd_ref[0])

    tok = pl.BlockSpec((tm, D), lambda i, s: (i, 0))
    hid = pl.BlockSpec((1, tm, F), lambda i, s: (s, i, 0))
    return pl.pallas_call(
        body, name=name, grid=(T // tm, N_CHIPS),
        in_specs=[tok, pl.BlockSpec((1, D), lambda i, s: (0, 0)),
                  pl.BlockSpec((1, D, F), lambda i, s: (s, 0, 0)), pl.BlockSpec((1, D, F), lambda i, s: (s, 0, 0)),
                  pl.BlockSpec((1, F, D), lambda i, s: (s, 0, 0))],
        out_specs=[tok, tok, hid, hid],
        out_shape=[jax.ShapeDtypeStruct((T, D), F32), jax.ShapeDtypeStruct((T, D), BF16),
                   jax.ShapeDtypeStruct((N_CHIPS, T, F), BF16), jax.ShapeDtypeStruct((N_CHIPS, T, F), BF16)],
        compiler_params=_params(2),
    )(x, g, wg, wu, wd)


def _ffn_bwd(dh, x, g, a, b, wg, wu, wd, *, name, tm=512):
    T, D = x.shape
    F = wg.shape[2]

    def body(dh_ref, x_ref, g_ref, a_ref, b_ref, wg_ref, wu_ref, wd_ref, dx_ref, da_ref, db_ref, hdn_ref, dg_ref):
        i, s = pl.program_id(0), pl.program_id(1)

        @pl.when((i == 0) & (s == 0))
        def _():
            dg_ref[...] = jnp.zeros_like(dg_ref)

        @pl.when(s == 0)
        def _():
            dx_ref[...] = jnp.zeros_like(dx_ref)

        dhdn = _dot_nt((0.5 * dh_ref[...]).astype(BF16), wd_ref[0])
        af = a_ref[0].astype(F32)
        bf = b_ref[0].astype(F32)
        sg = jax.nn.sigmoid(af)
        silu = af * sg
        hdn_ref[0] = (silu * bf).astype(BF16)
        da = (dhdn * bf * (sg * (1.0 + af * (1.0 - sg)))).astype(BF16)
        db = (dhdn * silu).astype(BF16)
        da_ref[0] = da
        db_ref[0] = db
        dx_ref[...] += _dot_nt(da, wg_ref[0]) + _dot_nt(db, wu_ref[0])

        @pl.when(s == N_CHIPS - 1)
        def _():
            xf = x_ref[...]
            dx, dgr = _norm_bwd(dx_ref[...], xf, g_ref[...], _rstd(xf))
            dg_ref[...] += jnp.sum(dgr, axis=0, keepdims=True)
            dx_ref[...] = dh_ref[...] + dx

    tok = pl.BlockSpec((tm, D), lambda i, s: (i, 0))
    hid = pl.BlockSpec((1, tm, F), lambda i, s: (s, i, 0))
    vec = pl.BlockSpec((1, D), lambda i, s: (0, 0))
    hshape = jax.ShapeDtypeStruct((N_CHIPS, T, F), BF16)
    return pl.pallas_call(
        body, name=name, grid=(T // tm, N_CHIPS),
        in_specs=[tok, tok, vec, hid, hid,
                  pl.BlockSpec((1, D, F), lambda i, s: (s, 0, 0)), pl.BlockSpec((1, D, F), lambda i, s: (s, 0, 0)),
                  pl.BlockSpec((1, F, D), lambda i, s: (s, 0, 0))],
        out_specs=[tok, hid, hid, hid, vec],
        out_shape=[jax.ShapeDtypeStruct((T, D), F32), hshape, hshape, hshape, jax.ShapeDtypeStruct((1, D), F32)],
        compiler_params=_params(2),
    )(dh, x, g, a, b, wg, wu, wd)


def _wgrad(a, b, *, a_block, a_map, b_block, b_map, out_shape, o_block, o_map, grid, scale=1.0, name):
    def body(a_ref, b_ref, o_ref):
        @pl.when(pl.program_id(len(grid) - 1) == 0)
        def _():
            o_ref[...] = jnp.zeros_like(o_ref)

        av = a_ref[...]
        bv = b_ref[...]
        av = av.reshape(av.shape[-2:]).astype(BF16)
        bv = bv.reshape(bv.shape[-2:])
        if scale != 1.0:
            bv = scale * bv
        o_ref[...] += _dot_tn(av, bv.astype(BF16)).reshape(o_ref.shape)

    return pl.pallas_call(
        body, name=name, grid=grid,
        in_specs=[pl.BlockSpec(a_block, a_map), pl.BlockSpec(b_block, b_map)],
        out_specs=pl.BlockSpec(o_block, o_map),
        out_shape=jax.ShapeDtypeStruct(out_shape, F32),
        compiler_params=_params(len(grid)),
    )(a, b)


def _wgrad_shard_b(a, b, *, name, scale=1.0, tk=512):
    T, M = a.shape
    S, _, N = b.shape
    return _wgrad(a, b, a_block=(tk, M), a_map=lambda s, k: (k, 0), b_block=(1, tk, N), b_map=lambda s, k: (s, k, 0),
                  out_shape=(S, M, N), o_block=(1, M, N), o_map=lambda s, k: (s, 0, 0), grid=(S, T // tk), scale=scale, name=name)


def _wgrad_shard_a(a, b, *, name, scale=1.0, tk=512):
    S, T, M = a.shape
    N = b.shape[1]
    return _wgrad(a, b, a_block=(1, tk, M), a_map=lambda s, k: (s, k, 0), b_block=(tk, N), b_map=lambda s, k: (k, 0),
                  out_shape=(S, M, N), o_block=(1, M, N), o_map=lambda s, k: (s, 0, 0), grid=(S, T // tk), scale=scale, name=name)


def _wgrad_cols(a, b, n_blocks, *, name, tk=512):
    T, M = a.shape
    N = b.shape[1] // n_blocks
    return _wgrad(a, b, a_block=(tk, M), a_map=lambda s, k: (k, 0), b_block=(tk, N), b_map=lambda s, k: (k, s),
                  out_shape=(n_blocks, M, N), o_block=(1, M, N), o_map=lambda s, k: (s, 0, 0), grid=(n_blocks, T // tk), name=name)


def _rope_tables(T):
    half = HEAD_DIM // 2
    inv = ROPE_THETA ** (-jnp.arange(half, dtype=F32) / half)
    ang = jnp.arange(T, dtype=F32)[:, None] * inv[None, :]
    cos, sin, zero = jnp.cos(ang), jnp.sin(ang), jnp.zeros_like(ang)
    reps = LANES // HEAD_DIM
    return (jnp.tile(jnp.concatenate([cos, cos], axis=1), (1, reps)),
            jnp.tile(jnp.concatenate([-sin, zero], axis=1), (1, reps)),
            jnp.tile(jnp.concatenate([zero, sin], axis=1), (1, reps)))


def _rope(x, cos, sa, sb, sign):
    half = HEAD_DIM // 2
    return x * cos + sign * (pltpu.roll(x, LANES - half, 1) * sa + pltpu.roll(x, half, 1) * sb)


def _mix_in_fwd(h, g, w_in, tables, *, tm=256):
    T, D = h.shape

    def body(h_ref, g_ref, w_ref, cos_ref, sa_ref, sb_ref, u_ref, qkva_ref, qb_ref, kvb_ref, gates_ref):
        hf = h_ref[...]
        u = ((hf * _rstd(hf)) * g_ref[...]).astype(BF16)
        u_ref[...] = u
        qkva_ref[...] = _dot(u, w_ref[:, 0:C_QKVA]).astype(BF16)
        zr = _dot(u, w_ref[:, O_QB:O_QB + C_ROPE])
        cos, sa, sb = cos_ref[...], sa_ref[...], sb_ref[...]
        for j in range(C_ROPE // LANES):
            rj = _rope(zr[:, j * LANES:(j + 1) * LANES], cos, sa, sb, 1.0).astype(BF16)
            if j < C_QB // LANES:
                qb_ref[:, j * LANES:(j + 1) * LANES] = rj
            else:
                kvb_ref[:, 0:C_KB] = rj
        kvb_ref[:, C_KB:2 * C_KB] = _dot(u, w_ref[:, O_VB:O_VB + C_KB]).astype(BF16)
        gates_ref[...] = _dot(u, w_ref[:, O_G:O_G + C_GATES])

    def tok(n):
        return pl.BlockSpec((tm, n), lambda i: (i, 0))

    return pl.pallas_call(
        body, name="mix_in_fwd", grid=(T // tm,),
        in_specs=[tok(D), pl.BlockSpec((1, D), lambda i: (0, 0)), pl.BlockSpec((D, D_IN), lambda i: (0, 0)),
                  tok(LANES), tok(LANES), tok(LANES)],
        out_specs=[tok(D), tok(C_QKVA), tok(C_QB), tok(2 * C_KB), tok(C_GATES)],
        out_shape=[jax.ShapeDtypeStruct((T, D), BF16), jax.ShapeDtypeStruct((T, C_QKVA), BF16),
                   jax.ShapeDtypeStruct((T, C_QB), BF16), jax.ShapeDtypeStruct((T, 2 * C_KB), BF16),
                   jax.ShapeDtypeStruct((T, C_GATES), F32)],
        compiler_params=_params(1),
    )(h, g, w_in, *tables)


def _mix_in_bwd(dqa, dka, dva, dqb, dkb, dvb, dgates, h, g, dres, w_in, tables, *, tm=256):
    T, D = h.shape

    def body(dqa_ref, dka_ref, dva_ref, dqb_ref, dkb_ref, dvb_ref, dgt_ref, h_ref, g_ref, dres_ref, w_ref,
             cos_ref, sa_ref, sb_ref, dz_ref, dh_ref, dg_ref):
        @pl.when(pl.program_id(0) == 0)
        def _():
            dg_ref[...] = jnp.zeros_like(dg_ref)

        na = NA_HEADS * HEAD_DIM
        dz_ref[:, 0:na] = dqa_ref[...].astype(BF16)
        dz_ref[:, na:2 * na] = dka_ref[...].astype(BF16)
        dz_ref[:, 2 * na:3 * na] = dva_ref[...].astype(BF16)
        cos, sa, sb = cos_ref[...], sa_ref[...], sb_ref[...]
        for j in range(C_QB // LANES):
            dz_ref[:, O_QB + j * LANES:O_QB + (j + 1) * LANES] = _rope(
                dqb_ref[:, j * LANES:(j + 1) * LANES], cos, sa, sb, -1.0).astype(BF16)
        dz_ref[:, O_KB:O_KB + C_KB] = _rope(dkb_ref[...], cos, sa, sb, -1.0).astype(BF16)
        dz_ref[:, O_VB:O_VB + C_KB] = dvb_ref[...].astype(BF16)
        dz_ref[:, O_G:O_G + C_GATES] = dgt_ref[...].astype(BF16)
        du = _dot_nt(dz_ref[...], w_ref[...])
        hf = h_ref[...]
        dx, dgr = _norm_bwd(du, hf, g_ref[...], _rstd(hf))
        dg_ref[...] += jnp.sum(dgr, axis=0, keepdims=True)
        dh_ref[...] = dres_ref[...] + dx

    def tok(n):
        return pl.BlockSpec((tm, n), lambda i: (i, 0))

    vec = pl.BlockSpec((1, D), lambda i: (0, 0))
    na = NA_HEADS * HEAD_DIM
    return pl.pallas_call(
        body, name="mix_in_bwd", grid=(T // tm,),
        in_specs=[tok(na), tok(na), tok(na), tok(C_QB), tok(C_KB), tok(C_KB), tok(C_GATES), tok(D), vec, tok(D),
                  pl.BlockSpec((D, D_IN), lambda i: (0, 0)), tok(LANES), tok(LANES), tok(LANES)],
        out_specs=[tok(D_IN), tok(D), vec],
        out_shape=[jax.ShapeDtypeStruct((T, D_IN), BF16), jax.ShapeDtypeStruct((T, D), F32),
                   jax.ShapeDtypeStruct((1, D), F32)],
        compiler_params=_params(1),
    )(dqa, dka, dva, dqb, dkb, dvb, dgates, h, g, dres, w_in, *tables)


def _na_bias_slabs(rpb):
    H = rpb.shape[0]
    ncell = GRID_W * GRID_W
    cell = jnp.arange(ncell)
    co = cell % GRID_W - cell // GRID_W + (NA_KW - 1)
    e_co = (jnp.arange(LANES)[:, None] == co[None, :]).astype(F32)
    table = jnp.pad(rpb, ((0, 0), (0, 1), (0, LANES - rpb.shape[2]))).reshape(H * 2 * NA_KH, LANES)

    def body(t_ref, e_ref, o_ref):
        o_ref[...] = jnp.dot(t_ref[...], e_ref[...], preferred_element_type=F32, precision=lax.Precision.HIGHEST)

    toeplitz = pl.pallas_call(
        body, name="rpb_unfold", out_shape=jax.ShapeDtypeStruct((H * 2 * NA_KH, ncell), F32),
        compiler_params=_params(0),
    )(table, e_co).reshape(H, 2 * NA_KH, GRID_W, GRID_W)
    c = jnp.arange(GRID_W)
    cs = jnp.clip(c - NA_KW // 2, 0, GRID_W - NA_KW)
    inwin = (c[None, :] >= cs[:, None]) & (c[None, :] < cs[:, None] + NA_KW)
    toeplitz = jnp.where(inwin[None, None], toeplitz, NEG)
    slabs = jnp.stack([toeplitz[:, r:r + NA_KH] for r in range(NA_KH)], axis=1)
    return slabs.transpose(0, 1, 3, 2, 4).reshape(H, NA_KH, GRID_W, NA_KH * GRID_W)


def _half_masks(rows):
    lane = lax.broadcasted_iota(jnp.int32, (rows, LANES), 1)
    left = lane < HEAD_DIM
    return left, (left, jnp.logical_not(left))


def _keep(mask, x):
    return jnp.where(mask, x.astype(F32), 0.0).astype(BF16)


def _na_row(j, t, rb, rows):
    r = j * rb + t
    rs = jnp.clip(r - NA_KH // 2, 0, rows - NA_KH)
    return pl.multiple_of(t * GRID_W, GRID_W), pl.multiple_of(rs * GRID_W, GRID_W), rs - r + (NA_KH - 1)


def _na_specs(T, rb):
    qrows = GRID_W * rb
    pairs = NA_HEADS // 2
    return ([pl.BlockSpec((qrows, LANES), lambda p, j: (j, p)),
             pl.BlockSpec((T, LANES), lambda p, j: (0, pairs + p)),
             pl.BlockSpec((T, LANES), lambda p, j: (0, 2 * pairs + p))],
            pl.BlockSpec((2, NA_KH, GRID_W, NA_KH * GRID_W), lambda p, j: (p, 0, 0, 0)))


def _na_fwd(qkva, bias, *, rb=8):
    T = qkva.shape[0]
    rows = T // GRID_W
    nkeys = NA_KH * GRID_W

    def body(q_ref, k_ref, v_ref, bias_ref, y_ref):
        j = pl.program_id(1)
        left, halves = _half_masks(GRID_W)

        def row(t, carry):
            q0, k0, ro0 = _na_row(j, t, rb, rows)
            q = q_ref[pl.ds(q0, GRID_W), :]
            ks = k_ref[pl.ds(k0, nkeys), :]
            vs = v_ref[pl.ds(k0, nkeys), :]
            outs = []
            for hh in range(2):
                s = _dot_nt(_keep(halves[hh], q), ks) * QK_SCALE + bias_ref[hh, ro0]
                p = jnp.exp(s - jnp.max(s, axis=-1, keepdims=True))
                p = p / jnp.sum(p, axis=-1, keepdims=True)
                outs.append(_dot(p.astype(BF16), vs))
            y_ref[pl.ds(q0, GRID_W), :] = jnp.where(left, outs[0], outs[1]).astype(BF16)
            return carry

        lax.fori_loop(0, rb, row, 0)

    qkv_specs, bias_spec = _na_specs(T, rb)
    return pl.pallas_call(
        body, name="na_fwd", grid=(NA_HEADS // 2, rows // rb),
        in_specs=qkv_specs + [bias_spec],
        out_specs=qkv_specs[0],
        out_shape=jax.ShapeDtypeStruct((T, NA_HEADS * HEAD_DIM), BF16),
        compiler_params=_params(2),
    )(qkva, qkva, qkva, bias)


def _na_bwd(qkva, dy, bias, *, rb=8):
    T = qkva.shape[0]
    rows = T // GRID_W
    nkeys = NA_KH * GRID_W

    def body(q_ref, k_ref, v_ref, dy_ref, bias_ref, dq_ref, dk_ref, dv_ref, dbias_ref):
        j = pl.program_id(1)
        left, halves = _half_masks(GRID_W)

        @pl.when(j == 0)
        def _():
            dk_ref[...] = jnp.zeros_like(dk_ref)
            dv_ref[...] = jnp.zeros_like(dv_ref)
            dbias_ref[...] = jnp.zeros_like(dbias_ref)

        def row(t, carry):
            q0, k0, ro0 = _na_row(j, t, rb, rows)
            q = q_ref[pl.ds(q0, GRID_W), :]
            dyr = dy_ref[pl.ds(q0, GRID_W), :]
            ks = k_ref[pl.ds(k0, nkeys), :]
            vs = v_ref[pl.ds(k0, nkeys), :]
            dqs = []
            dk = jnp.zeros((nkeys, LANES), F32)
            dv = jnp.zeros((nkeys, LANES), F32)
            for hh in range(2):
                qm = _keep(halves[hh], q)
                dym = _keep(halves[hh], dyr)
                s = _dot_nt(qm, ks) * QK_SCALE + bias_ref[hh, ro0]
                p = jnp.exp(s - jnp.max(s, axis=-1, keepdims=True))
                p = p / jnp.sum(p, axis=-1, keepdims=True)
                dp = _dot_nt(dym, vs)
                ds = p * (dp - jnp.sum(p * dp, axis=-1, keepdims=True))
                dbias_ref[hh, ro0] += ds
                dsb = ds.astype(BF16)
                dqs.append(_dot(dsb, ks) * QK_SCALE)
                dk = dk + _dot_tn(dsb, qm) * QK_SCALE
                dv = dv + _dot_tn(p.astype(BF16), dym)
            dq_ref[pl.ds(q0, GRID_W), :] = jnp.where(left, dqs[0], dqs[1]).astype(BF16)
            dk_ref[pl.ds(k0, nkeys), :] += dk
            dv_ref[pl.ds(k0, nkeys), :] += dv
            return carry

        lax.fori_loop(0, rb, row, 0)

    qkv_specs, bias_spec = _na_specs(T, rb)
    width = NA_HEADS * HEAD_DIM
    kv_out = pl.BlockSpec((T, LANES), lambda p, j: (0, p))
    return pl.pallas_call(
        body, name="na_bwd", grid=(NA_HEADS // 2, rows // rb),
        in_specs=qkv_specs + [qkv_specs[0], bias_spec],
        out_specs=[qkv_specs[0], kv_out, kv_out, bias_spec],
        out_shape=[jax.ShapeDtypeStruct((T, width), BF16), jax.ShapeDtypeStruct((T, width), F32),
                   jax.ShapeDtypeStruct((T, width), F32), jax.ShapeDtypeStruct(bias.shape, F32)],
        compiler_params=_params(2),
    )(qkva, qkva, qkva, dy, bias)


def _rpb_fold(dslab):
    H = dslab.shape[0]
    nro = NA_KH * NA_KH
    ncell = GRID_W * GRID_W
    xs = dslab.reshape(H, NA_KH, GRID_W, NA_KH, GRID_W).transpose(0, 1, 3, 2, 4).reshape(H, nro, ncell)
    cell = jnp.arange(ncell)
    co = cell % GRID_W - cell // GRID_W + (NA_KW - 1)
    e_co = (co[:, None] == jnp.arange(LANES)[None, :]).astype(F32)
    pair = jnp.arange(nro)
    e_ro = ((pair // NA_KH + pair % NA_KH)[None, :] == jnp.arange(2 * NA_KH)[:, None]).astype(F32)

    def body(x_ref, eco_ref, ero_ref, o_ref):
        y = jnp.dot(x_ref[0], eco_ref[...], preferred_element_type=F32, precision=lax.Precision.HIGHEST)
        o_ref[0] = jnp.dot(ero_ref[...], y, preferred_element_type=F32, precision=lax.Precision.HIGHEST)

    return pl.pallas_call(
        body, name="rpb_fold", grid=(H,),
        in_specs=[pl.BlockSpec((1, nro, ncell), lambda h: (h, 0, 0)), pl.BlockSpec((ncell, LANES), lambda h: (0, 0)),
                  pl.BlockSpec((2 * NA_KH, nro), lambda h: (0, 0))],
        out_specs=pl.BlockSpec((1, 2 * NA_KH, LANES), lambda h: (h, 0, 0)),
        out_shape=jax.ShapeDtypeStruct((H, 2 * NA_KH, LANES), F32),
        compiler_params=_params(1),
    )(xs, e_co, e_ro)


SWA_KEYS = 3 * WIN


def _swa_block(j, t, qbn, T):
    blk = j * qbn + t
    start = jnp.clip((blk - 1) * WIN, 0, T - SWA_KEYS)
    qpos = blk * WIN + lax.broadcasted_iota(jnp.int32, (WIN, SWA_KEYS), 0)
    kpos = start + lax.broadcasted_iota(jnp.int32, (WIN, SWA_KEYS), 1)
    return pl.multiple_of(t * WIN, WIN), pl.multiple_of(start, WIN), jnp.abs(qpos - kpos) <= WIN


def _swa_probs(qm, ks, mask, sink):
    s = jnp.where(mask, _dot_nt(qm, ks) * QK_SCALE, NEG)
    m = jnp.maximum(jnp.max(s, axis=-1, keepdims=True), sink)
    e = jnp.exp(s - m)
    esink = jnp.exp(sink - m)
    den = jnp.sum(e, axis=-1, keepdims=True) + esink
    return e / den, esink / den


def _swa_specs(T, qbn):
    return [pl.BlockSpec(memory_space=pltpu.SMEM),
            pl.BlockSpec((WIN * qbn, LANES), lambda p, j: (j, p)),
            pl.BlockSpec((T, LANES), lambda p, j: (0, 0)),
            pl.BlockSpec((T, LANES), lambda p, j: (0, 1))]


def _swa_fwd(qb, kvb, sink, *, qbn=4):
    T = qb.shape[0]
    pairs = NB_HEADS // 2

    def body(sink_ref, q_ref, k_ref, v_ref, y_ref):
        p, j = pl.program_id(0), pl.program_id(1)
        left, halves = _half_masks(WIN)

        def block(t, carry):
            q0, k0, mask = _swa_block(j, t, qbn, T)
            q = q_ref[pl.ds(q0, WIN), :]
            ks = k_ref[pl.ds(k0, SWA_KEYS), :]
            vs = v_ref[pl.ds(k0, SWA_KEYS), :]
            outs = []
            for hh in range(2):
                pr, _ = _swa_probs(_keep(halves[hh], q), ks, mask, sink_ref[p + pairs * hh])
                outs.append(_dot(pr.astype(BF16), vs))
            y_ref[pl.ds(q0, WIN), :] = jnp.where(left, outs[0], outs[1]).astype(BF16)
            return carry

        lax.fori_loop(0, qbn, block, 0)

    specs = _swa_specs(T, qbn)
    return pl.pallas_call(
        body, name="swa_fwd", grid=(pairs, T // (WIN * qbn)),
        in_specs=specs, out_specs=specs[1],
        out_shape=jax.ShapeDtypeStruct((T, NB_HEADS * HEAD_DIM), BF16),
        compiler_params=_params(2),
    )(sink, qb, kvb, kvb)


def _swa_bwd(qb, kvb, dy, sink, *, qbn=4):
    T = qb.shape[0]
    pairs = NB_HEADS // 2

    def body(sink_ref, q_ref, k_ref, v_ref, dy_ref, dq_ref, dk_ref, dv_ref, dsink_ref):
        p, j = pl.program_id(0), pl.program_id(1)
        left, halves = _half_masks(WIN)

        @pl.when((p == 0) & (j == 0))
        def _():
            dk_ref[...] = jnp.zeros_like(dk_ref)
            dv_ref[...] = jnp.zeros_like(dv_ref)

        @pl.when(j == 0)
        def _():
            dsink_ref[...] = jnp.zeros_like(dsink_ref)

        def block(t, carry):
            q0, k0, mask = _swa_block(j, t, qbn, T)
            q = q_ref[pl.ds(q0, WIN), :]
            dyb = dy_ref[pl.ds(q0, WIN), :]
            ks = k_ref[pl.ds(k0, SWA_KEYS), :]
            vs = v_ref[pl.ds(k0, SWA_KEYS), :]
            dqs = []
            dk = jnp.zeros((SWA_KEYS, LANES), F32)
            dv = jnp.zeros((SWA_KEYS, LANES), F32)
            for hh in range(2):
                qm = _keep(halves[hh], q)
                dym = _keep(halves[hh], dyb)
                pr, psink = _swa_probs(qm, ks, mask, sink_ref[p + pairs * hh])
                dp = _dot_nt(dym, vs)
                delta = jnp.sum(pr * dp, axis=-1, keepdims=True)
                ds = pr * (dp - delta)
                dsink_ref[0, hh:hh + 1, :] += jnp.broadcast_to(-jnp.sum(psink * delta), (1, LANES))
                dsb = ds.astype(BF16)
                dqs.append(_dot(dsb, ks) * QK_SCALE)
                dk = dk + _dot_tn(dsb, qm) * QK_SCALE
                dv = dv + _dot_tn(pr.astype(BF16), dym)
            dq_ref[pl.ds(q0, WIN), :] = jnp.where(left, dqs[0], dqs[1])
            dk_ref[pl.ds(k0, SWA_KEYS), :] += dk
            dv_ref[pl.ds(k0, SWA_KEYS), :] += dv
            return carry

        lax.fori_loop(0, qbn, block, 0)

    specs = _swa_specs(T, qbn)
    kv_out = pl.BlockSpec((T, LANES), lambda p, j: (0, 0))
    return pl.pallas_call(
        body, name="swa_bwd", grid=(pairs, T // (WIN * qbn)),
        in_specs=specs + [specs[1]],
        out_specs=[specs[1], kv_out, kv_out, pl.BlockSpec((1, 8, LANES), lambda p, j: (p, 0, 0))],
        out_shape=[jax.ShapeDtypeStruct((T, NB_HEADS * HEAD_DIM), F32), jax.ShapeDtypeStruct((T, LANES), F32),
                   jax.ShapeDtypeStruct((T, LANES), F32), jax.ShapeDtypeStruct((pairs, 8, LANES), F32)],
        compiler_params=_params(2),
    )(sink, qb, kvb, kvb, dy)


def _merge_fwd(ya, yb, gates, wa, wb, wout, h, *, tm=512):
    T, D = h.shape
    W = ya.shape[1]

    def body(ya_ref, yb_ref, gt_ref, wa_ref, wb_ref, wo_ref, h_ref, h2_ref, mg_ref):
        pa = _dot(ya_ref[...], wa_ref[...])
        pb = _dot(yb_ref[...], wb_ref[...])
        mg = (jax.nn.sigmoid(gt_ref[:, 0:D]) * pa + jax.nn.sigmoid(gt_ref[:, D:2 * D]) * pb).astype(BF16)
        mg_ref[...] = mg
        h2_ref[...] = h_ref[...] + _dot(mg, wo_ref[...])

    def tok(n):
        return pl.BlockSpec((tm, n), lambda i: (i, 0))

    def full(r, c):
        return pl.BlockSpec((r, c), lambda i: (0, 0))

    return pl.pallas_call(
        body, name="merge_fwd", grid=(T // tm,),
        in_specs=[tok(W), tok(W), tok(2 * D), full(W, D), full(W, D), full(D, D), tok(D)],
        out_specs=[tok(D), tok(D)],
        out_shape=[jax.ShapeDtypeStruct((T, D), F32), jax.ShapeDtypeStruct((T, D), BF16)],
        compiler_params=_params(1),
    )(ya, yb, gates, wa, wb, wout, h)


def _merge_bwd(dh, ya, yb, gates, wa, wb, wout, *, tm=512):
    T, D = dh.shape
    W = ya.shape[1]

    def body(dh_ref, ya_ref, yb_ref, gt_ref, wa_ref, wb_ref, wo_ref, dya_ref, dyb_ref, dpa_ref, dpb_ref, dgt_ref):
        dmg = _dot_nt(dh_ref[...].astype(BF16), wo_ref[...])
        for y_ref, w_ref, dy_ref, dp_ref, lo in ((ya_ref, wa_ref, dya_ref, dpa_ref, 0), (yb_ref, wb_ref, dyb_ref, dpb_ref, D)):
            sg = jax.nn.sigmoid(gt_ref[:, lo:lo + D])
            dp = (dmg * sg).astype(BF16)
            dp_ref[...] = dp
            dgt_ref[:, lo:lo + D] = (dmg * _dot(y_ref[...], w_ref[...]) * (sg * (1.0 - sg))).astype(BF16)
            dy_ref[...] = _dot_nt(dp, w_ref[...]).astype(BF16)

    def tok(n):
        return pl.BlockSpec((tm, n), lambda i: (i, 0))

    def full(r, c):
        return pl.BlockSpec((r, c), lambda i: (0, 0))

    return pl.pallas_call(
        body, name="merge_bwd", grid=(T // tm,),
        in_specs=[tok(D), tok(W), tok(W), tok(2 * D), full(W, D), full(W, D), full(D, D)],
        out_specs=[tok(W), tok(W), tok(D), tok(D), tok(2 * D)],
        out_shape=[jax.ShapeDtypeStruct((T, W), BF16), jax.ShapeDtypeStruct((T, W), BF16),
                   jax.ShapeDtypeStruct((T, D), BF16), jax.ShapeDtypeStruct((T, D), BF16),
                   jax.ShapeDtypeStruct((T, 2 * D), BF16)],
        compiler_params=_params(1),
    )(dh, ya, yb, gates, wa, wb, wout)


def _final_loss(h, g, target, *, tm=512):
    T, D = h.shape

    def body(h_ref, g_ref, t_ref, dh_ref, loss_ref, dg_ref):
        @pl.when(pl.program_id(0) == 0)
        def _():
            loss_ref[...] = jnp.zeros_like(loss_ref)
            dg_ref[...] = jnp.zeros_like(dg_ref)

        hf = h_ref[...]
        r = _rstd(hf)
        gv = g_ref[...]
        err = (hf * r) * gv - t_ref[...]
        loss_ref[...] += jnp.broadcast_to(0.5 * jnp.sum(jnp.mean(err * err, axis=-1, keepdims=True)), loss_ref.shape)
        dx, dgr = _norm_bwd(err * (1.0 / D), hf, gv, r)
        dg_ref[...] += jnp.sum(dgr, axis=0, keepdims=True)
        dh_ref[...] = dx

    tok = pl.BlockSpec((tm, D), lambda i: (i, 0))
    vec = pl.BlockSpec((1, D), lambda i: (0, 0))
    return pl.pallas_call(
        body, name="final_loss", grid=(T // tm,),
        in_specs=[tok, vec, tok],
        out_specs=[tok, pl.BlockSpec((1, LANES), lambda i: (0, 0)), vec],
        out_shape=[jax.ShapeDtypeStruct((T, D), F32), jax.ShapeDtypeStruct((1, LANES), F32),
                   jax.ShapeDtypeStruct((1, D), F32)],
        compiler_params=_params(1),
    )(h, g, target)


def _pair_heads(a, axis):
    shp = a.shape
    a = a.reshape(shp[:axis] + (2, NB_HEADS // 2, HEAD_DIM) + shp[axis + 1:])
    return jnp.swapaxes(a, axis, axis + 1).reshape(shp)


def _unpair_heads(a, axis):
    shp = a.shape
    a = a.reshape(shp[:axis] + (NB_HEADS // 2, 2, HEAD_DIM) + shp[axis + 1:])
    return jnp.swapaxes(a, axis, axis + 1).reshape(shp)


def _layer_grads(x, target, g1, f1, gmix, w_in, rpb, sink, wa, wb, wout, g2, f2, gfin):
    T = x.shape[0]
    tables = _rope_tables(T)
    w_in_p = jnp.concatenate([w_in[:, :O_QB], _pair_heads(w_in[:, O_QB:O_KB], 1), w_in[:, O_KB:]], axis=1)
    wb_p = _pair_heads(wb, 0)
    bias = _na_bias_slabs(rpb)

    h1, n1, a1, b1 = _ffn_fwd(x, g1, *f1, name="ffn1_fwd")
    u, qkva, qb, kvb, gates = _mix_in_fwd(h1, gmix, w_in_p, tables)
    ya = _na_fwd(qkva, bias)
    yb = _swa_fwd(qb, kvb, sink)
    h2, merged = _merge_fwd(ya, yb, gates, wa, wb_p, wout, h1)
    h3, n2, a2, b2 = _ffn_fwd(h2, g2, *f2, name="ffn2_fwd")
    dh3, loss, dgfin = _final_loss(h3, gfin, target)

    dh2, da2, db2, hdn2, dg2 = _ffn_bwd(dh3, h2, g2, a2, b2, *f2, name="ffn2_bwd")
    df2 = (_wgrad_shard_b(n2, da2, name="ffn2_dwg"), _wgrad_shard_b(n2, db2, name="ffn2_dwu"),
           _wgrad_shard_a(hdn2, dh3, scale=0.5, name="ffn2_dwd"))
    dya, dyb, dpa, dpb, dgates = _merge_bwd(dh2, ya, yb, gates, wa, wb_p, wout)
    dwout = _wgrad_cols(merged, dh2, 1, name="dwout")[0]
    dwa = _wgrad_cols(ya, dpa, 1, name="dwa")[0]
    dwb = _unpair_heads(_wgrad_cols(yb, dpb, 1, name="dwb")[0], 0)
    dqa, dka, dva, dbias = _na_bwd(qkva, dya, bias)
    drpb = _rpb_fold(dbias)
    dqb, dkb, dvb, dsink = _swa_bwd(qb, kvb, dyb, sink)
    dz, dh1, dgmix = _mix_in_bwd(dqa, dka, dva, dqb, dkb, dvb, dgates, h1, gmix, dh2, w_in_p, tables)
    dwin_p = _wgrad_cols(u, dz, 2, name="dwin")
    dwin_p = dwin_p.transpose(1, 0, 2).reshape(D_MODEL, D_IN)
    dwin = jnp.concatenate([dwin_p[:, :O_QB], _unpair_heads(dwin_p[:, O_QB:O_KB], 1), dwin_p[:, O_KB:]], axis=1)
    dx, da1, db1, hdn1, dg1 = _ffn_bwd(dh1, x, g1, a1, b1, *f1, name="ffn1_bwd")
    df1 = (_wgrad_shard_b(n1, da1, name="ffn1_dwg"), _wgrad_shard_b(n1, db1, name="ffn1_dwu"),
           _wgrad_shard_a(hdn1, dh1, scale=0.5, name="ffn1_dwd"))
    dsink_v = dsink[:, 0:2, 0].T.reshape(NB_HEADS)
    return dict(loss=loss, dx=dx, g1=dg1, f1=df1, gmix=dgmix, w_in=dwin, rpb=drpb, sink=dsink_v,
                wa=dwa, wb=dwb, wout=dwout, g2=dg2, f2=df2, gfin=dgfin)


ANY = pl.BlockSpec(memory_space=pl.ANY)


def _place():
    x, y, c = lax.axis_index("x"), lax.axis_index("y"), lax.axis_index("c")
    chips = [(1 - x, y), (x, 1 - y), (1 - x, 1 - y)]
    return x, y, c, 2 * x + y, chips


def _remote(src, dst, send_sems, recv_sems, k, device):
    return pltpu.make_async_remote_copy(src_ref=src, dst_ref=dst, send_sem=send_sems.at[k], recv_sem=recv_sems.at[k],
                                        device_id=device, device_id_type=MESH)


def _all_gather(shards):
    n = len(shards)

    def body(*refs):
        ins, outs = refs[:n], refs[n:2 * n]
        send_sems, recv_sems, own_send_sems, own_recv_sems = refs[2 * n:]
        x, y, c, mine, chips = _place()
        sibling = (x, y, 1 - c)
        own, sends, passes = [], [], []
        for i in range(n):
            hr = shards[i].shape[0] // 2
            cp = _remote(ins[i], outs[i].at[mine], own_send_sems, own_recv_sems, i, sibling)
            cp.start()
            own.append(cp)
            for j, (cx, cy) in enumerate(chips):
                cp = _remote(ins[i].at[pl.ds(c * hr, hr)], outs[i].at[mine, pl.ds(c * hr, hr)], send_sems, recv_sems,
                             6 * i + j, (cx, cy, c))
                cp.start()
                sends.append(cp)
        for i in range(n):
            hr = shards[i].shape[0] // 2
            for j, (cx, cy) in enumerate(chips):
                landed = outs[i].at[2 * cx + cy, pl.ds(c * hr, hr)]
                _remote(landed, landed, send_sems, recv_sems, 6 * i + j, (cx, cy, c)).wait_recv()
                cp = _remote(landed, landed, send_sems, recv_sems, 6 * i + 3 + j, sibling)
                cp.start()
                passes.append(cp)
        for i in range(n):
            hr = shards[i].shape[0] // 2
            for j, (cx, cy) in enumerate(chips):
                other = outs[i].at[2 * cx + cy, pl.ds((1 - c) * hr, hr)]
                _remote(other, other, send_sems, recv_sems, 6 * i + 3 + j, sibling).wait_recv()
        for cp in sends + passes:
            cp.wait_send()
        for cp in own:
            cp.wait()

    return pl.pallas_call(
        body, name="all_gather_weights",
        in_specs=[ANY] * n, out_specs=[ANY] * n,
        out_shape=[jax.ShapeDtypeStruct((N_CHIPS,) + s.shape, s.dtype) for s in shards],
        scratch_shapes=[pltpu.SemaphoreType.DMA((6 * n,)), pltpu.SemaphoreType.DMA((6 * n,)),
                        pltpu.SemaphoreType.DMA((n,)), pltpu.SemaphoreType.DMA((n,))],
    )(*shards)


def _rs_sibling(grads):
    n = len(grads)

    def body(*refs):
        ins, outs = refs[:n], refs[n:2 * n]
        send_sems, recv_sems = refs[2 * n:]
        x, y, c, _, _ = _place()
        copies = []
        for i in range(n):
            hr = grads[i].shape[1] // 2
            cp = _remote(ins[i].at[:, pl.ds((1 - c) * hr, hr)], outs[i], send_sems, recv_sems, i, (x, y, 1 - c))
            cp.start()
            copies.append(cp)
        for cp in copies:
            cp.wait()

    return pl.pallas_call(
        body, name="rs_sibling",
        in_specs=[ANY] * n, out_specs=[ANY] * n,
        out_shape=[jax.ShapeDtypeStruct((g.shape[0], g.shape[1] // 2, g.shape[2]), g.dtype) for g in grads],
        scratch_shapes=[pltpu.SemaphoreType.DMA((n,)), pltpu.SemaphoreType.DMA((n,))],
    )(*grads)


def _rs_chips(parts):
    n = len(parts)

    def body(*refs):
        ins, outs = refs[:n], refs[n:2 * n]
        send_sems, recv_sems = refs[2 * n:]
        _, _, c, _, chips = _place()
        copies = []
        for i in range(n):
            for j, (cx, cy) in enumerate(chips):
                cp = _remote(ins[i].at[2 * cx + cy], outs[i].at[j], send_sems, recv_sems, 3 * i + j, (cx, cy, c))
                cp.start()
                copies.append(cp)
        for cp in copies:
            cp.wait()

    return pl.pallas_call(
        body, name="rs_chips",
        in_specs=[ANY] * n, out_specs=[ANY] * n,
        out_shape=[jax.ShapeDtypeStruct((N_CHIPS - 1,) + p.shape[1:], p.dtype) for p in parts],
        scratch_shapes=[pltpu.SemaphoreType.DMA((3 * n,)), pltpu.SemaphoreType.DMA((3 * n,))],
    )(*parts)


def _rs_share(halves):
    n = len(halves)

    def body(*refs):
        ins, outs = refs[:n], refs[n:2 * n]
        send_sems, recv_sems = refs[2 * n:]
        x, y, c, _, _ = _place()
        copies = []
        for i in range(n):
            cp = _remote(ins[i], outs[i], send_sems, recv_sems, i, (x, y, 1 - c))
            cp.start()
            copies.append(cp)
        for cp in copies:
            cp.wait()

    return pl.pallas_call(
        body, name="rs_share",
        in_specs=[ANY] * n, out_specs=[ANY] * n,
        out_shape=[jax.ShapeDtypeStruct(h.shape, h.dtype) for h in halves],
        scratch_shapes=[pltpu.SemaphoreType.DMA((n,)), pltpu.SemaphoreType.DMA((n,))],
    )(*halves)


N_DEV = 8


def _small_allreduce(vec):
    R = vec.shape[0]

    def body(v_ref, o_ref, buf, send_sems, recv_sems):
        x, y, c, _, _ = _place()
        me = 4 * x + 2 * y + c
        buf[me] = v_ref[...]
        copies = []
        for k in range(1, N_DEV):
            peer = (x ^ (k >> 2), y ^ ((k >> 1) & 1), c ^ (k & 1))
            cp = _remote(v_ref, buf.at[me], send_sems, recv_sems, k - 1, peer)
            cp.start()
            copies.append(cp)
        for k, cp in enumerate(copies, start=1):
            cp.wait_send()
            landed = buf.at[me ^ k]
            _remote(landed, landed, send_sems, recv_sems, k - 1, (x, y, c)).wait_recv()
        acc = buf[0]
        for d in range(1, N_DEV):
            acc = acc + buf[d]
        o_ref[...] = acc

    return pl.pallas_call(
        body, name="small_allreduce",
        in_specs=[pl.BlockSpec(memory_space=pltpu.VMEM)], out_specs=pl.BlockSpec(memory_space=pltpu.VMEM),
        out_shape=jax.ShapeDtypeStruct(vec.shape, vec.dtype),
        scratch_shapes=[pltpu.VMEM((N_DEV, R, LANES), F32), pltpu.SemaphoreType.DMA((N_DEV - 1,)),
                        pltpu.SemaphoreType.DMA((N_DEV - 1,))],
    )(vec)


ELEMWISE_BLOCK = 256 * 1024


def _row_tile(rows, cols):
    best = None
    for t in range(8, rows + 1, 8):
        if rows % t == 0 and t * cols <= ELEMWISE_BLOCK:
            best = t
    return best if best is not None else rows


def _add_sibling(g, r1, cidx, *, name):
    S, R, C = g.shape
    hr = R // 2
    tr = _row_tile(hr, C)
    nt = hr // tr

    def body(c_ref, g_ref, r_ref, o_ref, o16_ref):
        p = g_ref[...] + r_ref[...]
        o_ref[...] = p
        o16_ref[...] = p.astype(BF16)

    blk = pl.BlockSpec((1, tr, C), lambda s, t, c: (s, t, 0))
    return pl.pallas_call(
        body, name=name,
        grid_spec=pltpu.PrefetchScalarGridSpec(
            num_scalar_prefetch=1, grid=(S, nt),
            in_specs=[pl.BlockSpec((1, tr, C), lambda s, t, c: (s, c[0] * nt + t, 0)), blk], out_specs=[blk, blk]),
        out_shape=[jax.ShapeDtypeStruct((S, hr, C), F32), jax.ShapeDtypeStruct((S, hr, C), BF16)],
        compiler_params=_params(2),
    )(cidx, g, r1)


def _add_chips(p, r2, chip, *, name):
    _, hr, C = p.shape
    tr = _row_tile(hr, C)

    def body(chip_ref, p_ref, r_ref, o_ref):
        o_ref[...] = ((p_ref[0] + r_ref[0].astype(F32)) + r_ref[1].astype(F32)) + r_ref[2].astype(F32)

    return pl.pallas_call(
        body, name=name,
        grid_spec=pltpu.PrefetchScalarGridSpec(
            num_scalar_prefetch=1, grid=(hr // tr,),
            in_specs=[pl.BlockSpec((1, tr, C), lambda t, s: (s[0], t, 0)), pl.BlockSpec((N_CHIPS - 1, tr, C), lambda t, s: (0, t, 0))],
            out_specs=pl.BlockSpec((tr, C), lambda t, s: (t, 0))),
        out_shape=jax.ShapeDtypeStruct((hr, C), F32),
        compiler_params=_params(1),
    )(chip, p, r2)


def _adamw_math(w, g, m, v):
    mn = ADAM_B1 * m + (1.0 - ADAM_B1) * g
    vn = ADAM_B2 * v + (1.0 - ADAM_B2) * (g * g)
    m_hat = mn / (1.0 - ADAM_B1 ** ADAM_STEP)
    v_hat = vn / (1.0 - ADAM_B2 ** ADAM_STEP)
    return -ADAM_LR * (m_hat / (jnp.sqrt(v_hat) + ADAM_EPS) + ADAM_WD * w), mn, vn


def _adamw_halves(w, mine, other, m, v, cidx, *, name):
    R, C = w.shape
    hr = R // 2
    tr = _row_tile(hr, C)
    nt = hr // tr

    def body(c_ref, w_ref, a_ref, b_ref, m_ref, v_ref, g_ref, d_ref, mo_ref, vo_ref):
        gv = jnp.where(pl.program_id(0) == c_ref[0], a_ref[...], b_ref[...])
        g_ref[...] = gv
        d_ref[...], mo_ref[...], vo_ref[...] = _adamw_math(w_ref[...], gv, m_ref[...], v_ref[...])

    full = pl.BlockSpec((tr, C), lambda h, t, c: (h * nt + t, 0))
    half = pl.BlockSpec((tr, C), lambda h, t, c: (t, 0))
    shape = jax.ShapeDtypeStruct((R, C), F32)
    return pl.pallas_call(
        body, name=name,
        grid_spec=pltpu.PrefetchScalarGridSpec(
            num_scalar_prefetch=1, grid=(2, nt), in_specs=[full, half, half, full, full], out_specs=[full] * 4),
        out_shape=[shape] * 4,
        compiler_params=_params(2),
    )(cidx, w, mine, other, m, v)


def _adamw(w, g, m, v, *, name):
    R, C = w.shape
    tr = _row_tile(R, C)

    def body(w_ref, g_ref, m_ref, v_ref, d_ref, mo_ref, vo_ref):
        d_ref[...], mo_ref[...], vo_ref[...] = _adamw_math(w_ref[...], g_ref[...], m_ref[...], v_ref[...])

    blk = pl.BlockSpec((tr, C), lambda t: (t, 0))
    shape = jax.ShapeDtypeStruct((R, C), F32)
    return pl.pallas_call(
        body, name=name, grid=(R // tr,),
        in_specs=[blk] * 4, out_specs=[blk] * 3, out_shape=[shape] * 3,
        compiler_params=_params(1),
    )(w, g, m, v)


def _stack_cols(w):
    r, n = w.shape
    return w.reshape(r, N_CHIPS, n // N_CHIPS).transpose(1, 0, 2)


def _unstack_cols(w):
    s, r, c = w.shape
    return w.transpose(1, 0, 2).reshape(r, s * c)


def _pad_rows(a, rows):
    return jnp.pad(a, ((0, rows - a.shape[0]), (0, LANES - a.shape[1])))


BIG = ("ffn1_w_gate", "ffn1_w_up", "ffn1_w_down", "w_in", "w_branch_a", "w_branch_b", "w_out",
       "ffn2_w_gate", "ffn2_w_up", "ffn2_w_down")
WEIGHTS = ("ffn1_norm", "ffn1_w_gate", "ffn1_w_up", "ffn1_w_down", "mix_norm", "w_in", "na_rpb", "sink_logit",
           "w_branch_a", "w_branch_b", "w_out", "ffn2_norm", "ffn2_w_gate", "ffn2_w_up", "ffn2_w_down", "final_norm")


def kernel(x, ffn1_norm, ffn1_w_gate, ffn1_w_up, ffn1_w_down, mix_norm, w_in, na_rpb, sink_logit, w_branch_a, w_branch_b, w_out, ffn2_norm, ffn2_w_gate, ffn2_w_up, ffn2_w_down, final_norm, loss_target, m_ffn1_norm, m_ffn1_w_gate, m_ffn1_w_up, m_ffn1_w_down, m_mix_norm, m_w_in, m_na_rpb, m_sink_logit, m_w_branch_a, m_w_branch_b, m_w_out, m_ffn2_norm, m_ffn2_w_gate, m_ffn2_w_up, m_ffn2_w_down, m_final_norm, v_ffn1_norm, v_ffn1_w_gate, v_ffn1_w_up, v_ffn1_w_down, v_mix_norm, v_w_in, v_na_rpb, v_sink_logit, v_w_branch_a, v_w_branch_b, v_w_out, v_ffn2_norm, v_ffn2_w_gate, v_ffn2_w_up, v_ffn2_w_down, v_final_norm):
    args = dict(locals())
    w = {k: args[k] for k in WEIGHTS}
    mom = {k: args["m_" + k] for k in WEIGHTS}
    var = {k: args["v_" + k] for k in WEIGHTS}
    cidx = lax.axis_index("c").astype(jnp.int32).reshape(1)
    chip = (2 * lax.axis_index("x") + lax.axis_index("y")).astype(jnp.int32).reshape(1)

    full = dict(zip(BIG, _all_gather([w[k][0].astype(BF16) for k in BIG])))
    f1 = (full["ffn1_w_gate"], full["ffn1_w_up"], full["ffn1_w_down"])
    f2 = (full["ffn2_w_gate"], full["ffn2_w_up"], full["ffn2_w_down"])
    out = _layer_grads(
        x[0], loss_target[0], ffn1_norm, f1, mix_norm, _unstack_cols(full["w_in"]), na_rpb[0], sink_logit[0],
        _unstack_cols(full["w_branch_a"]), _unstack_cols(full["w_branch_b"]), full["w_out"].reshape(D_MODEL, D_MODEL),
        ffn2_norm, f2, final_norm.reshape(1, D_MODEL))

    by_chip = dict(zip(BIG, [out["f1"][0], out["f1"][1], out["f1"][2], _stack_cols(out["w_in"]), _stack_cols(out["wa"]),
                             _stack_cols(out["wb"]), out["wout"].reshape(N_CHIPS, D_MODEL // N_CHIPS, D_MODEL),
                             out["f2"][0], out["f2"][1], out["f2"][2]]))
    grads = [by_chip[k] for k in BIG]
    from_sibling = _rs_sibling(grads)
    parts = [_add_sibling(g, r, cidx, name="add_sibling_" + k) for k, g, r in zip(BIG, grads, from_sibling)]
    from_chips = _rs_chips([p16 for _, p16 in parts])
    halves = [_add_chips(p, r, chip, name="add_chips_" + k) for k, (p, _), r in zip(BIG, parts, from_chips)]
    mine = dict(zip(BIG, halves))
    other = dict(zip(BIG, _rs_share(halves)))
    grad = {}

    rows = D_MODEL // LANES
    small = jnp.concatenate([
        out["g1"].reshape(rows, LANES), out["gmix"].reshape(rows, LANES), out["g2"].reshape(rows, LANES),
        out["gfin"].reshape(rows, LANES), out["rpb"].reshape(-1, LANES),
        _pad_rows(out["sink"].reshape(1, NB_HEADS), 8), _pad_rows(out["loss"], 8)], axis=0)
    total = _small_allreduce(small)
    n_rpb = NA_HEADS * 2 * NA_KH
    grad["ffn1_norm"] = total[0:rows].reshape(1, D_MODEL)
    grad["mix_norm"] = total[rows:2 * rows].reshape(1, D_MODEL)
    grad["ffn2_norm"] = total[2 * rows:3 * rows].reshape(1, D_MODEL)
    grad["final_norm"] = total[3 * rows:4 * rows].reshape(1, D_MODEL)
    grad["na_rpb"] = total[4 * rows:4 * rows + n_rpb].reshape(NA_HEADS, 2 * NA_KH, LANES)[:, :2 * NA_KH - 1, :2 * NA_KW - 1]
    grad["na_rpb"] = grad["na_rpb"].reshape(NA_HEADS, -1)
    grad["sink_logit"] = total[4 * rows + n_rpb:4 * rows + n_rpb + 1, 0:NB_HEADS]
    loss = total[4 * rows + n_rpb + 8, 0]

    deltas, new_m, new_v, grads_out = {}, {}, {}, {}
    for k in WEIGHTS:
        shape = w[k].shape
        if k in mine:
            two_d = w[k].shape[1:]
            g2d, d, mn, vn = _adamw_halves(w[k].reshape(two_d), mine[k], other[k], mom[k].reshape(two_d), var[k].reshape(two_d),
                                           cidx, name="adamw_" + k)
        else:
            g2d = grad[k]
            d, mn, vn = _adamw(w[k].reshape(g2d.shape), g2d, mom[k].reshape(g2d.shape), var[k].reshape(g2d.shape), name="adamw_" + k)
        grads_out[k], deltas[k], new_m[k], new_v[k] = (a.reshape(shape) for a in (g2d, d, mn, vn))
    return (loss, out["dx"].reshape(x.shape), *[grads_out[k] for k in WEIGHTS], *[deltas[k] for k in WEIGHTS],
            *[new_m[k] for k in WEIGHTS], *[new_v[k] for k in WEIGHTS])
```

```python
import functools
import math

import jax
import jax.numpy as jnp
from jax import lax
from jax.experimental import pallas as pl
from jax.experimental.pallas import tpu as pltpu

F32 = jnp.float32
BF16 = jnp.bfloat16

D_MODEL = 1024
HEAD_DIM = 64
NA_HEADS = 8
NB_HEADS = 8
GRID_W = 64
NA_KH = 8
NA_KW = 16
WIN = 128
ROPE_THETA = 10000.0
EPS = 1e-6
N_CHIPS = 4
QK_SCALE = HEAD_DIM ** -0.5
NEG = -1e30
LANES = 128
VMEM_LIMIT = 56 * 1024 * 1024

C_QKVA = 3 * NA_HEADS * HEAD_DIM
C_QB = NB_HEADS * HEAD_DIM
C_KB = 2 * HEAD_DIM
C_ROPE = C_QB + C_KB
C_GATES = 2 * D_MODEL
D_IN = C_QKVA + C_QB + 2 * C_KB + C_GATES
O_QB = C_QKVA
O_KB = O_QB + C_QB
O_VB = O_KB + C_KB
O_G = O_VB + C_KB

ADAM_LR = 0.001
ADAM_B1 = 0.9
ADAM_B2 = 0.999
ADAM_EPS = 1e-08
ADAM_WD = 0.01
ADAM_STEP = 10

MESH = pl.DeviceIdType.MESH


def _dot(a, b):
    return jnp.dot(a, b, preferred_element_type=F32)


def _dot_nt(a, b):
    return lax.dot_general(a, b, (((1,), (1,)), ((), ())), preferred_element_type=F32)


def _dot_tn(a, b):
    return lax.dot_general(a, b, (((0,), (0,)), ((), ())), preferred_element_type=F32)


def _params(n_axes):
    return pltpu.CompilerParams(dimension_semantics=("arbitrary",) * n_axes, vmem_limit_bytes=VMEM_LIMIT)


def _rstd(xf):
    return lax.rsqrt(jnp.mean(xf * xf, axis=-1, keepdims=True) + EPS)


def _norm_bwd(dn, xf, g, r):
    xhat = xf * r
    dxh = dn * g
    dx = r * (dxh - xhat * jnp.mean(dxh * xhat, axis=-1, keepdims=True))
    return dx, dn * xhat


def _ffn_fwd(x, g, wg, wu, wd, *, name, tm=512):
    T, D = x.shape
    F = wg.shape[2]

    def body(x_ref, g_ref, wg_ref, wu_ref, wd_ref, h_ref, n_ref, a_ref, b_ref):
        s = pl.program_id(1)

        @pl.when(s == 0)
        def _():
            xf = x_ref[...]
            n_ref[...] = ((xf * _rstd(xf)) * g_ref[...]).astype(BF16)
            h_ref[...] = xf

        n = n_ref[...]
        a = _dot(n, wg_ref[0])
        b = _dot(n, wu_ref[0])
        a_ref[0] = a.astype(BF16)
        b_ref[0] = b.astype(BF16)
        hdn = (a * jax.nn.sigmoid(a) * b).astype(BF16)
        h_ref[...] += 0.5 * _dot(hdn, wd_ref[0])

    tok = pl.BlockSpec((tm, D), lambda i, s: (i, 0))
    hid = pl.BlockSpec((1, tm, F), lambda i, s: (s, i, 0))
    return pl.pallas_call(
        body, name=name, grid=(T // tm, N_CHIPS),
        in_specs=[tok, pl.BlockSpec((1, D), lambda i, s: (0, 0)),
                  pl.BlockSpec((1, D, F), lambda i, s: (s, 0, 0)), pl.BlockSpec((1, D, F), lambda i, s: (s, 0, 0)),
                  pl.BlockSpec((1, F, D), lambda i, s: (s, 0, 0))],
        out_specs=[tok, tok, hid, hid],
        out_shape=[jax.ShapeDtypeStruct((T, D), F32), jax.ShapeDtypeStruct((T, D), BF16),
                   jax.ShapeDtypeStruct((N_CHIPS, T, F), BF16), jax.ShapeDtypeStruct((N_CHIPS, T, F), BF16)],
        compiler_params=_params(2),
    )(x, g, wg, wu, wd)


def _ffn_bwd(dh, x, g, a, b, wg, wu, wd, *, name, tm=512):
    T, D = x.shape
    F = wg.shape[2]

    def body(dh_ref, x_ref, g_ref, a_ref, b_ref, wg_ref, wu_ref, wd_ref, dx_ref, da_ref, db_ref, hdn_ref, dg_ref):
        i, s = pl.program_id(0), pl.program_id(1)

        @pl.when((i == 0) & (s == 0))
        def _():
            dg_ref[...] = jnp.zeros_like(dg_ref)

        @pl.when(s == 0)
        def _():
            dx_ref[...] = jnp.zeros_like(dx_ref)

        dhdn = _dot_nt((0.5 * dh_ref[...]).astype(BF16), wd_ref[0])
        af = a_ref[0].astype(F32)
        bf = b_ref[0].astype(F32)
        sg = jax.nn.sigmoid(af)
        silu = af * sg
        hdn_ref[0] = (silu * bf).astype(BF16)
        da = (dhdn * bf * (sg * (1.0 + af * (1.0 - sg)))).astype(BF16)
        db = (dhdn * silu).astype(BF16)
        da_ref[0] = da
        db_ref[0] = db
        dx_ref[...] += _dot_nt(da, wg_ref[0]) + _dot_nt(db, wu_ref[0])

        @pl.when(s == N_CHIPS - 1)
        def _():
            xf = x_ref[...]
            dx, dgr = _norm_bwd(dx_ref[...], xf, g_ref[...], _rstd(xf))
            dg_ref[...] += jnp.sum(dgr, axis=0, keepdims=True)
            dx_ref[...] = dh_ref[...] + dx

    tok = pl.BlockSpec((tm, D), lambda i, s: (i, 0))
    hid = pl.BlockSpec((1, tm, F), lambda i, s: (s, i, 0))
    vec = pl.BlockSpec((1, D), lambda i, s: (0, 0))
    hshape = jax.ShapeDtypeStruct((N_CHIPS, T, F), BF16)
    return pl.pallas_call(
        body, name=name, grid=(T // tm, N_CHIPS),
        in_specs=[tok, tok, vec, hid, hid,
                  pl.BlockSpec((1, D, F), lambda i, s: (s, 0, 0)), pl.BlockSpec((1, D, F), lambda i, s: (s, 0, 0)),
                  pl.BlockSpec((1, F, D), lambda i, s: (s, 0, 0))],
        out_specs=[tok, hid, hid, hid, vec],
        out_shape=[jax.ShapeDtypeStruct((T, D), F32), hshape, hshape, hshape, jax.ShapeDtypeStruct((1, D), F32)],
        compiler_params=_params(2),
    )(dh, x, g, a, b, wg, wu, wd)


def _wgrad(a, b, *, a_block, a_map, b_block, b_map, out_shape, o_block, o_map, grid, scale=1.0, name):
    def body(a_ref, b_ref, o_ref):
        @pl.when(pl.program_id(len(grid) - 1) == 0)
        def _():
            o_ref[...] = jnp.zeros_like(o_ref)

        av = a_ref[...]
        bv = b_ref[...]
        av = av.reshape(av.shape[-2:]).astype(BF16)
        bv = bv.reshape(bv.shape[-2:])
        if scale != 1.0:
            bv = scale * bv
        o_ref[...] += _dot_tn(av, bv.astype(BF16)).reshape(o_ref.shape)

    return pl.pallas_call(
        body, name=name, grid=grid,
        in_specs=[pl.BlockSpec(a_block, a_map), pl.BlockSpec(b_block, b_map)],
        out_specs=pl.BlockSpec(o_block, o_map),
        out_shape=jax.ShapeDtypeStruct(out_shape, F32),
        compiler_params=_params(len(grid)),
    )(a, b)


def _wgrad_shard_b(a, b, *, name, scale=1.0, tk=512):
    T, M = a.shape
    S, _, N = b.shape
    return _wgrad(a, b, a_block=(tk, M), a_map=lambda s, k: (k, 0), b_block=(1, tk, N), b_map=lambda s, k: (s, k, 0),
                  out_shape=(S, M, N), o_block=(1, M, N), o_map=lambda s, k: (s, 0, 0), grid=(S, T // tk), scale=scale, name=name)


def _wgrad_shard_a(a, b, *, name, scale=1.0, tk=512):
    S, T, M = a.shape
    N = b.shape[1]
    return _wgrad(a, b, a_block=(1, tk, M), a_map=lambda s, k: (s, k, 0), b_block=(tk, N), b_map=lambda s, k: (k, 0),
                  out_shape=(S, M, N), o_block=(1, M, N), o_map=lambda s, k: (s, 0, 0), grid=(S, T // tk), scale=scale, name=name)


def _wgrad_cols(a, b, n_blocks, *, name, tk=512):
    T, M = a.shape
    N = b.shape[1] // n_blocks
    return _wgrad(a, b, a_block=(tk, M), a_map=lambda s, k: (k, 0), b_block=(tk, N), b_map=lambda s, k: (k, s),
                  out_shape=(n_blocks, M, N), o_block=(1, M, N), o_map=lambda s, k: (s, 0, 0), grid=(n_blocks, T // tk), name=name)


def _rope_tables(T):
    half = HEAD_DIM // 2
    inv = ROPE_THETA ** (-jnp.arange(half, dtype=F32) / half)
    ang = jnp.arange(T, dtype=F32)[:, None] * inv[None, :]
    cos, sin, zero = jnp.cos(ang), jnp.sin(ang), jnp.zeros_like(ang)
    reps = LANES // HEAD_DIM
    return (jnp.tile(jnp.concatenate([cos, cos], axis=1), (1, reps)),
            jnp.tile(jnp.concatenate([-sin, zero], axis=1), (1, reps)),
            jnp.tile(jnp.concatenate([zero, sin], axis=1), (1, reps)))


def _rope(x, cos, sa, sb, sign):
    half = HEAD_DIM // 2
    return x * cos + sign * (pltpu.roll(x, LANES - half, 1) * sa + pltpu.roll(x, half, 1) * sb)


def _mix_in_fwd(h, g, w_in, tables, *, tm=256):
    T, D = h.shape

    def body(h_ref, g_ref, w_ref, cos_ref, sa_ref, sb_ref, u_ref, qkva_ref, qb_ref, kvb_ref, gates_ref):
        hf = h_ref[...]
        u = ((hf * _rstd(hf)) * g_ref[...]).astype(BF16)
        u_ref[...] = u
        qkva_ref[...] = _dot(u, w_ref[:, 0:C_QKVA]).astype(BF16)
        zr = _dot(u, w_ref[:, O_QB:O_QB + C_ROPE])
        cos, sa, sb = cos_ref[...], sa_ref[...], sb_ref[...]
        for j in range(C_ROPE // LANES):
            rj = _rope(zr[:, j * LANES:(j + 1) * LANES], cos, sa, sb, 1.0).astype(BF16)
            if j < C_QB // LANES:
                qb_ref[:, j * LANES:(j + 1) * LANES] = rj
            else:
                kvb_ref[:, 0:C_KB] = rj
        kvb_ref[:, C_KB:2 * C_KB] = _dot(u, w_ref[:, O_VB:O_VB + C_KB]).astype(BF16)
        gates_ref[...] = _dot(u, w_ref[:, O_G:O_G + C_GATES])

    def tok(n):
        return pl.BlockSpec((tm, n), lambda i: (i, 0))

    return pl.pallas_call(
        body, name="mix_in_fwd", grid=(T // tm,),
        in_specs=[tok(D), pl.BlockSpec((1, D), lambda i: (0, 0)), pl.BlockSpec((D, D_IN), lambda i: (0, 0)),
                  tok(LANES), tok(LANES), tok(LANES)],
        out_specs=[tok(D), tok(C_QKVA), tok(C_QB), tok(2 * C_KB), tok(C_GATES)],
        out_shape=[jax.ShapeDtypeStruct((T, D), BF16), jax.ShapeDtypeStruct((T, C_QKVA), BF16),
                   jax.ShapeDtypeStruct((T, C_QB), BF16), jax.ShapeDtypeStruct((T, 2 * C_KB), BF16),
                   jax.ShapeDtypeStruct((T, C_GATES), F32)],
        compiler_params=_params(1),
    )(h, g, w_in, *tables)


def _mix_in_bwd(dqa, dka, dva, dqb, dkb, dvb, dgates, h, g, dres, w_in, tables, *, tm=256):
    T, D = h.shape

    def body(dqa_ref, dka_ref, dva_ref, dqb_ref, dkb_ref, dvb_ref, dgt_ref, h_ref, g_ref, dres_ref, w_ref,
             cos_ref, sa_ref, sb_ref, dz_ref, dh_ref, dg_ref):
        @pl.when(pl.program_id(0) == 0)
        def _():
            dg_ref[...] = jnp.zeros_like(dg_ref)

        na = NA_HEADS * HEAD_DIM
        dz_ref[:, 0:na] = dqa_ref[...].astype(BF16)
        dz_ref[:, na:2 * na] = dka_ref[...].astype(BF16)
        dz_ref[:, 2 * na:3 * na] = dva_ref[...].astype(BF16)
        cos, sa, sb = cos_ref[...], sa_ref[...], sb_ref[...]
        for j in range(C_QB // LANES):
            dz_ref[:, O_QB + j * LANES:O_QB + (j + 1) * LANES] = _rope(
                dqb_ref[:, j * LANES:(j + 1) * LANES], cos, sa, sb, -1.0).astype(BF16)
        dz_ref[:, O_KB:O_KB + C_KB] = _rope(dkb_ref[...], cos, sa, sb, -1.0).astype(BF16)
        dz_ref[:, O_VB:O_VB + C_KB] = dvb_ref[...].astype(BF16)
        dz_ref[:, O_G:O_G + C_GATES] = dgt_ref[...].astype(BF16)
        du = _dot_nt(dz_ref[...], w_ref[...])
        hf = h_ref[...]
        dx, dgr = _norm_bwd(du, hf, g_ref[...], _rstd(hf))
        dg_ref[...] += jnp.sum(dgr, axis=0, keepdims=True)
        dh_ref[...] = dres_ref[...] + dx

    def tok(n):
        return pl.BlockSpec((tm, n), lambda i: (i, 0))

    vec = pl.BlockSpec((1, D), lambda i: (0, 0))
    na = NA_HEADS * HEAD_DIM
    return pl.pallas_call(
        body, name="mix_in_bwd", grid=(T // tm,),
        in_specs=[tok(na), tok(na), tok(na), tok(C_QB), tok(C_KB), tok(C_KB), tok(C_GATES), tok(D), vec, tok(D),
                  pl.BlockSpec((D, D_IN), lambda i: (0, 0)), tok(LANES), tok(LANES), tok(LANES)],
        out_specs=[tok(D_IN), tok(D), vec],
        out_shape=[jax.ShapeDtypeStruct((T, D_IN), BF16), jax.ShapeDtypeStruct((T, D), F32),
                   jax.ShapeDtypeStruct((1, D), F32)],
        compiler_params=_params(1),
    )(dqa, dka, dva, dqb, dkb, dvb, dgates, h, g, dres, w_in, *tables)


def _na_bias_slabs(rpb):
    H = rpb.shape[0]
    ncell = GRID_W * GRID_W
    cell = jnp.arange(ncell)
    co = cell % GRID_W - cell // GRID_W + (NA_KW - 1)
    e_co = (jnp.arange(LANES)[:, None] == co[None, :]).astype(F32)
    table = jnp.pad(rpb, ((0, 0), (0, 1), (0, LANES - rpb.shape[2]))).reshape(H * 2 * NA_KH, LANES)

    def body(t_ref, e_ref, o_ref):
        o_ref[...] = jnp.dot(t_ref[...], e_ref[...], preferred_element_type=F32, precision=lax.Precision.HIGHEST)

    toeplitz = pl.pallas_call(
        body, name="rpb_unfold", out_shape=jax.ShapeDtypeStruct((H * 2 * NA_KH, ncell), F32),
        compiler_params=_params(0),
    )(table, e_co).reshape(H, 2 * NA_KH, GRID_W, GRID_W)
    c = jnp.arange(GRID_W)
    cs = jnp.clip(c - NA_KW // 2, 0, GRID_W - NA_KW)
    inwin = (c[None, :] >= cs[:, None]) & (c[None, :] < cs[:, None] + NA_KW)
    toeplitz = jnp.where(inwin[None, None], toeplitz, NEG)
    slabs = jnp.stack([toeplitz[:, r:r + NA_KH] for r in range(NA_KH)], axis=1)
    slabs = slabs.reshape(H // 2, 2, NA_KH, NA_KH, GRID_W, GRID_W).transpose(0, 2, 1, 4, 3, 5)
    return slabs.reshape(H // 2, NA_KH, 2 * GRID_W, NA_KH * GRID_W)


def _na_unstack_slabs(dslab):
    pairs = dslab.shape[0]
    d = dslab.reshape(pairs, NA_KH, 2, GRID_W, NA_KH * GRID_W).transpose(0, 2, 1, 3, 4)
    return d.reshape(2 * pairs, NA_KH, GRID_W, NA_KH * GRID_W)


def _half_masks(rows):
    lane = lax.broadcasted_iota(jnp.int32, (rows, LANES), 1)
    left = lane < HEAD_DIM
    return left, (left, jnp.logical_not(left))


def _stack_heads(x):
    left, halves = _half_masks(x.shape[0])
    xf = x.astype(F32)
    return jnp.concatenate([jnp.where(m, xf, 0.0).astype(BF16) for m in halves], axis=0)


def _unstack_heads(o):
    rows = o.shape[0] // 2
    left, _ = _half_masks(rows)
    return jnp.where(left, o[:rows], o[rows:])


def _na_row(j, t, rb, rows):
    r = j * rb + t
    rs = jnp.clip(r - NA_KH // 2, 0, rows - NA_KH)
    return pl.multiple_of(t * GRID_W, GRID_W), pl.multiple_of(rs * GRID_W, GRID_W), rs - r + (NA_KH - 1)


def _na_specs(T, rb):
    qrows = GRID_W * rb
    pairs = NA_HEADS // 2
    return ([pl.BlockSpec((qrows, LANES), lambda p, j: (j, p)),
             pl.BlockSpec((T, LANES), lambda p, j: (0, pairs + p)),
             pl.BlockSpec((T, LANES), lambda p, j: (0, 2 * pairs + p))],
            pl.BlockSpec((1, NA_KH, 2 * GRID_W, NA_KH * GRID_W), lambda p, j: (p, 0, 0, 0)))


def _softmax(s):
    p = jnp.exp(s - jnp.max(s, axis=-1, keepdims=True))
    return p / jnp.sum(p, axis=-1, keepdims=True)


def _na_probs(qs, ks, bias):
    return _softmax(_dot_nt(qs, ks) * QK_SCALE + bias)


def _na_fwd(qkva, bias, *, rb=8, group=4):
    T = qkva.shape[0]
    rows = T // GRID_W
    nkeys = NA_KH * GRID_W

    def body(q_ref, k_ref, v_ref, bias_ref, y_ref):
        j = pl.program_id(1)

        def rows_step(t, carry):
            at = [_na_row(j, t * group + u, rb, rows) for u in range(group)]
            s = [_dot_nt(_stack_heads(q_ref[pl.ds(q0, GRID_W), :]), k_ref[pl.ds(k0, nkeys), :]) for q0, k0, _ in at]
            p = [_softmax(su * QK_SCALE + bias_ref[0, ro0]) for su, (_, _, ro0) in zip(s, at)]
            o = [_dot(pu.astype(BF16), v_ref[pl.ds(k0, nkeys), :]) for pu, (_, k0, _) in zip(p, at)]
            for ou, (q0, _, _) in zip(o, at):
                y_ref[pl.ds(q0, GRID_W), :] = _unstack_heads(ou).astype(BF16)
            return carry

        lax.fori_loop(0, rb // group, rows_step, 0)

    qkv_specs, bias_spec = _na_specs(T, rb)
    return pl.pallas_call(
        body, name="na_fwd", grid=(NA_HEADS // 2, rows // rb),
        in_specs=qkv_specs + [bias_spec],
        out_specs=qkv_specs[0],
        out_shape=jax.ShapeDtypeStruct((T, NA_HEADS * HEAD_DIM), BF16),
        compiler_params=_params(2),
    )(qkva, qkva, qkva, bias)


def _na_bwd(qkva, dy, bias, *, rb=8, group=4):
    T = qkva.shape[0]
    rows = T // GRID_W
    nkeys = NA_KH * GRID_W

    def body(q_ref, k_ref, v_ref, dy_ref, bias_ref, dq_ref, dk_ref, dv_ref, dbias_ref):
        j = pl.program_id(1)

        @pl.when(j == 0)
        def _():
            dk_ref[...] = jnp.zeros_like(dk_ref)
            dv_ref[...] = jnp.zeros_like(dv_ref)
            dbias_ref[...] = jnp.zeros_like(dbias_ref)

        def rows_step(t, carry):
            at = [_na_row(j, t * group + u, rb, rows) for u in range(group)]
            qs = [_stack_heads(q_ref[pl.ds(q0, GRID_W), :]) for q0, _, _ in at]
            dys = [_stack_heads(dy_ref[pl.ds(q0, GRID_W), :]) for q0, _, _ in at]
            s = [_dot_nt(qu, k_ref[pl.ds(k0, nkeys), :]) for qu, (_, k0, _) in zip(qs, at)]
            dp = [_dot_nt(du, v_ref[pl.ds(k0, nkeys), :]) for du, (_, k0, _) in zip(dys, at)]
            p = [_softmax(su * QK_SCALE + bias_ref[0, ro0]) for su, (_, _, ro0) in zip(s, at)]
            ds = [pu * (du - jnp.sum(pu * du, axis=-1, keepdims=True)) for pu, du in zip(p, dp)]
            for u, (q0, k0, ro0) in enumerate(at):
                dbias_ref[0, ro0] += ds[u]
                dsb = ds[u].astype(BF16)
                dq_ref[pl.ds(q0, GRID_W), :] = (_unstack_heads(_dot(dsb, k_ref[pl.ds(k0, nkeys), :])) * QK_SCALE).astype(BF16)
                dk_ref[pl.ds(k0, nkeys), :] += _dot_tn(dsb, qs[u]) * QK_SCALE
                dv_ref[pl.ds(k0, nkeys), :] += _dot_tn(p[u].astype(BF16), dys[u])
            return carry

        lax.fori_loop(0, rb // group, rows_step, 0)

    qkv_specs, bias_spec = _na_specs(T, rb)
    width = NA_HEADS * HEAD_DIM
    kv_out = pl.BlockSpec((T, LANES), lambda p, j: (0, p))
    return pl.pallas_call(
        body, name="na_bwd", grid=(NA_HEADS // 2, rows // rb),
        in_specs=qkv_specs + [qkv_specs[0], bias_spec],
        out_specs=[qkv_specs[0], kv_out, kv_out, bias_spec],
        out_shape=[jax.ShapeDtypeStruct((T, width), BF16), jax.ShapeDtypeStruct((T, width), F32),
                   jax.ShapeDtypeStruct((T, width), F32), jax.ShapeDtypeStruct(bias.shape, F32)],
        compiler_params=_params(2),
    )(qkva, qkva, qkva, dy, bias)


def _rpb_fold(dslab):
    H = dslab.shape[0]
    nro = NA_KH * NA_KH
    ncell = GRID_W * GRID_W
    xs = dslab.reshape(H, NA_KH, GRID_W, NA_KH, GRID_W).transpose(0, 1, 3, 2, 4).reshape(H, nro, ncell)
    cell = jnp.arange(ncell)
    co = cell % GRID_W - cell // GRID_W + (NA_KW - 1)
    e_co = (co[:, None] == jnp.arange(LANES)[None, :]).astype(F32)
    pair = jnp.arange(nro)
    e_ro = ((pair // NA_KH + pair % NA_KH)[None, :] == jnp.arange(2 * NA_KH)[:, None]).astype(F32)

    def body(x_ref, eco_ref, ero_ref, o_ref):
        y = jnp.dot(x_ref[0], eco_ref[...], preferred_element_type=F32, precision=lax.Precision.HIGHEST)
        o_ref[0] = jnp.dot(ero_ref[...], y, preferred_element_type=F32, precision=lax.Precision.HIGHEST)

    return pl.pallas_call(
        body, name="rpb_fold", grid=(H,),
        in_specs=[pl.BlockSpec((1, nro, ncell), lambda h: (h, 0, 0)), pl.BlockSpec((ncell, LANES), lambda h: (0, 0)),
                  pl.BlockSpec((2 * NA_KH, nro), lambda h: (0, 0))],
        out_specs=pl.BlockSpec((1, 2 * NA_KH, LANES), lambda h: (h, 0, 0)),
        out_shape=jax.ShapeDtypeStruct((H, 2 * NA_KH, LANES), F32),
        compiler_params=_params(1),
    )(xs, e_co, e_ro)


SWA_KEYS = 3 * WIN


def _swa_block(j, t, qbn, T):
    blk = j * qbn + t
    start = jnp.clip((blk - 1) * WIN, 0, T - SWA_KEYS)
    row = lax.broadcasted_iota(jnp.int32, (2 * WIN, SWA_KEYS), 0)
    qpos = blk * WIN + jnp.where(row < WIN, row, row - WIN)
    kpos = start + lax.broadcasted_iota(jnp.int32, (2 * WIN, SWA_KEYS), 1)
    return pl.multiple_of(t * WIN, WIN), pl.multiple_of(start, WIN), jnp.abs(qpos - kpos) <= WIN


def _swa_sinks(sink_ref, p):
    row = lax.broadcasted_iota(jnp.int32, (2 * WIN, 1), 0)
    return jnp.where(row < WIN, sink_ref[p], sink_ref[p + NB_HEADS // 2])


def _swa_probs(s, mask, sink):
    s = jnp.where(mask, s * QK_SCALE, NEG)
    m = jnp.maximum(jnp.max(s, axis=-1, keepdims=True), sink)
    e = jnp.exp(s - m)
    esink = jnp.exp(sink - m)
    den = jnp.sum(e, axis=-1, keepdims=True) + esink
    return e / den, esink / den


def _swa_specs(T, qbn):
    return [pl.BlockSpec(memory_space=pltpu.SMEM),
            pl.BlockSpec((WIN * qbn, LANES), lambda p, j: (j, p)),
            pl.BlockSpec((T, LANES), lambda p, j: (0, 0)),
            pl.BlockSpec((T, LANES), lambda p, j: (0, 1))]


def _swa_fwd(qb, kvb, sink, *, qbn=4, group=4):
    T = qb.shape[0]
    pairs = NB_HEADS // 2

    def body(sink_ref, q_ref, k_ref, v_ref, y_ref):
        p, j = pl.program_id(0), pl.program_id(1)
        sinks = _swa_sinks(sink_ref, p)

        def blocks_step(t, carry):
            at = [_swa_block(j, t * group + u, qbn, T) for u in range(group)]
            s = [_dot_nt(_stack_heads(q_ref[pl.ds(q0, WIN), :]), k_ref[pl.ds(k0, SWA_KEYS), :]) for q0, k0, _ in at]
            pr = [_swa_probs(su, mask, sinks)[0] for su, (_, _, mask) in zip(s, at)]
            o = [_dot(pu.astype(BF16), v_ref[pl.ds(k0, SWA_KEYS), :]) for pu, (_, k0, _) in zip(pr, at)]
            for ou, (q0, _, _) in zip(o, at):
                y_ref[pl.ds(q0, WIN), :] = _unstack_heads(ou).astype(BF16)
            return carry

        lax.fori_loop(0, qbn // group, blocks_step, 0)

    specs = _swa_specs(T, qbn)
    return pl.pallas_call(
        body, name="swa_fwd", grid=(pairs, T // (WIN * qbn)),
        in_specs=specs, out_specs=specs[1],
        out_shape=jax.ShapeDtypeStruct((T, NB_HEADS * HEAD_DIM), BF16),
        compiler_params=_params(2),
    )(sink, qb, kvb, kvb)


def _swa_bwd(qb, kvb, dy, sink, *, qbn=4, group=4):
    T = qb.shape[0]
    pairs = NB_HEADS // 2

    def body(sink_ref, q_ref, k_ref, v_ref, dy_ref, dq_ref, dk_ref, dv_ref, dsink_ref):
        p, j = pl.program_id(0), pl.program_id(1)
        sinks = _swa_sinks(sink_ref, p)

        @pl.when((p == 0) & (j == 0))
        def _():
            dk_ref[...] = jnp.zeros_like(dk_ref)
            dv_ref[...] = jnp.zeros_like(dv_ref)

        @pl.when(j == 0)
        def _():
            dsink_ref[...] = jnp.zeros_like(dsink_ref)

        def blocks_step(t, carry):
            at = [_swa_block(j, t * group + u, qbn, T) for u in range(group)]
            qs = [_stack_heads(q_ref[pl.ds(q0, WIN), :]) for q0, _, _ in at]
            dys = [_stack_heads(dy_ref[pl.ds(q0, WIN), :]) for q0, _, _ in at]
            s = [_dot_nt(qu, k_ref[pl.ds(k0, SWA_KEYS), :]) for qu, (_, k0, _) in zip(qs, at)]
            dp = [_dot_nt(du, v_ref[pl.ds(k0, SWA_KEYS), :]) for du, (_, k0, _) in zip(dys, at)]
            probs = [_swa_probs(su, mask, sinks) for su, (_, _, mask) in zip(s, at)]
            for u, (q0, k0, _) in enumerate(at):
                pr, psink = probs[u]
                delta = jnp.sum(pr * dp[u], axis=-1, keepdims=True)
                dsb = (pr * (dp[u] - delta)).astype(BF16)
                dsk = psink * delta
                for hh in range(2):
                    dsink_ref[0, hh:hh + 1, :] += jnp.broadcast_to(-jnp.sum(dsk[hh * WIN:(hh + 1) * WIN]), (1, LANES))
                dq_ref[pl.ds(q0, WIN), :] = _unstack_heads(_dot(dsb, k_ref[pl.ds(k0, SWA_KEYS), :])) * QK_SCALE
                dk_ref[pl.ds(k0, SWA_KEYS), :] += _dot_tn(dsb, qs[u]) * QK_SCALE
                dv_ref[pl.ds(k0, SWA_KEYS), :] += _dot_tn(pr.astype(BF16), dys[u])
            return carry

        lax.fori_loop(0, qbn // group, blocks_step, 0)

    specs = _swa_specs(T, qbn)
    kv_out = pl.BlockSpec((T, LANES), lambda p, j: (0, 0))
    return pl.pallas_call(
        body, name="swa_bwd", grid=(pairs, T // (WIN * qbn)),
        in_specs=specs + [specs[1]],
        out_specs=[specs[1], kv_out, kv_out, pl.BlockSpec((1, 8, LANES), lambda p, j: (p, 0, 0))],
        out_shape=[jax.ShapeDtypeStruct((T, NB_HEADS * HEAD_DIM), F32), jax.ShapeDtypeStruct((T, LANES), F32),
                   jax.ShapeDtypeStruct((T, LANES), F32), jax.ShapeDtypeStruct((pairs, 8, LANES), F32)],
        compiler_params=_params(2),
    )(sink, qb, kvb, kvb, dy)


def _merge_fwd(ya, yb, gates, wa, wb, wout, h, *, tm=512):
    T, D = h.shape
    W = ya.shape[1]

    def body(ya_ref, yb_ref, gt_ref, wa_ref, wb_ref, wo_ref, h_ref, h2_ref, mg_ref):
        pa = _dot(ya_ref[...], wa_ref[...])
        pb = _dot(yb_ref[...], wb_ref[...])
        mg = (jax.nn.sigmoid(gt_ref[:, 0:D]) * pa + jax.nn.sigmoid(gt_ref[:, D:2 * D]) * pb).astype(BF16)
        mg_ref[...] = mg
        h2_ref[...] = h_ref[...] + _dot(mg, wo_ref[...])

    def tok(n):
        return pl.BlockSpec((tm, n), lambda i: (i, 0))

    def full(r, c):
        return pl.BlockSpec((r, c), lambda i: (0, 0))

    return pl.pallas_call(
        body, name="merge_fwd", grid=(T // tm,),
        in_specs=[tok(W), tok(W), tok(2 * D), full(W, D), full(W, D), full(D, D), tok(D)],
        out_specs=[tok(D), tok(D)],
        out_shape=[jax.ShapeDtypeStruct((T, D), F32), jax.ShapeDtypeStruct((T, D), BF16)],
        compiler_params=_params(1),
    )(ya, yb, gates, wa, wb, wout, h)


def _merge_bwd(dh, ya, yb, gates, wa, wb, wout, *, tm=512):
    T, D = dh.shape
    W = ya.shape[1]

    def body(dh_ref, ya_ref, yb_ref, gt_ref, wa_ref, wb_ref, wo_ref, dya_ref, dyb_ref, dpa_ref, dpb_ref, dgt_ref):
        dmg = _dot_nt(dh_ref[...].astype(BF16), wo_ref[...])
        for y_ref, w_ref, dy_ref, dp_ref, lo in ((ya_ref, wa_ref, dya_ref, dpa_ref, 0), (yb_ref, wb_ref, dyb_ref, dpb_ref, D)):
            sg = jax.nn.sigmoid(gt_ref[:, lo:lo + D])
            dp = (dmg * sg).astype(BF16)
            dp_ref[...] = dp
            dgt_ref[:, lo:lo + D] = (dmg * _dot(y_ref[...], w_ref[...]) * (sg * (1.0 - sg))).astype(BF16)
            dy_ref[...] = _dot_nt(dp, w_ref[...]).astype(BF16)

    def tok(n):
        return pl.BlockSpec((tm, n), lambda i: (i, 0))

    def full(r, c):
        return pl.BlockSpec((r, c), lambda i: (0, 0))

    return pl.pallas_call(
        body, name="merge_bwd", grid=(T // tm,),
        in_specs=[tok(D), tok(W), tok(W), tok(2 * D), full(W, D), full(W, D), full(D, D)],
        out_specs=[tok(W), tok(W), tok(D), tok(D), tok(2 * D)],
        out_shape=[jax.ShapeDtypeStruct((T, W), BF16), jax.ShapeDtypeStruct((T, W), BF16),
                   jax.ShapeDtypeStruct((T, D), BF16), jax.ShapeDtypeStruct((T, D), BF16),
                   jax.ShapeDtypeStruct((T, 2 * D), BF16)],
        compiler_params=_params(1),
    )(dh, ya, yb, gates, wa, wb, wout)


def _final_loss(h, g, target, *, tm=512):
    T, D = h.shape

    def body(h_ref, g_ref, t_ref, dh_ref, loss_ref, dg_ref):
        @pl.when(pl.program_id(0) == 0)
        def _():
            loss_ref[...] = jnp.zeros_like(loss_ref)
            dg_ref[...] = jnp.zeros_like(dg_ref)

        hf = h_ref[...]
        r = _rstd(hf)
        gv = g_ref[...]
        err = (hf * r) * gv - t_ref[...]
        loss_ref[...] += jnp.broadcast_to(0.5 * jnp.sum(jnp.mean(err * err, axis=-1, keepdims=True)), loss_ref.shape)
        dx, dgr = _norm_bwd(err * (1.0 / D), hf, gv, r)
        dg_ref[...] += jnp.sum(dgr, axis=0, keepdims=True)
        dh_ref[...] = dx

    tok = pl.BlockSpec((tm, D), lambda i: (i, 0))
    vec = pl.BlockSpec((1, D), lambda i: (0, 0))
    return pl.pallas_call(
        body, name="final_loss", grid=(T // tm,),
        in_specs=[tok, vec, tok],
        out_specs=[tok, pl.BlockSpec((1, LANES), lambda i: (0, 0)), vec],
        out_shape=[jax.ShapeDtypeStruct((T, D), F32), jax.ShapeDtypeStruct((1, LANES), F32),
                   jax.ShapeDtypeStruct((1, D), F32)],
        compiler_params=_params(1),
    )(h, g, target)


def _pair_heads(a, axis):
    shp = a.shape
    a = a.reshape(shp[:axis] + (2, NB_HEADS // 2, HEAD_DIM) + shp[axis + 1:])
    return jnp.swapaxes(a, axis, axis + 1).reshape(shp)


def _unpair_heads(a, axis):
    shp = a.shape
    a = a.reshape(shp[:axis] + (NB_HEADS // 2, 2, HEAD_DIM) + shp[axis + 1:])
    return jnp.swapaxes(a, axis, axis + 1).reshape(shp)


def _layer_grads(x, target, g1, f1, gmix, w_in, rpb, sink, wa, wb, wout, g2, f2, gfin):
    T = x.shape[0]
    tables = _rope_tables(T)
    w_in_p = jnp.concatenate([w_in[:, :O_QB], _pair_heads(w_in[:, O_QB:O_KB], 1), w_in[:, O_KB:]], axis=1)
    wb_p = _pair_heads(wb, 0)
    bias = _na_bias_slabs(rpb)

    h1, n1, a1, b1 = _ffn_fwd(x, g1, *f1, name="ffn1_fwd")
    u, qkva, qb, kvb, gates = _mix_in_fwd(h1, gmix, w_in_p, tables)
    ya = _na_fwd(qkva, bias)
    yb = _swa_fwd(qb, kvb, sink)
    h2, merged = _merge_fwd(ya, yb, gates, wa, wb_p, wout, h1)
    h3, n2, a2, b2 = _ffn_fwd(h2, g2, *f2, name="ffn2_fwd")
    dh3, loss, dgfin = _final_loss(h3, gfin, target)

    dh2, da2, db2, hdn2, dg2 = _ffn_bwd(dh3, h2, g2, a2, b2, *f2, name="ffn2_bwd")
    df2 = (_wgrad_shard_b(n2, da2, name="ffn2_dwg"), _wgrad_shard_b(n2, db2, name="ffn2_dwu"),
           _wgrad_shard_a(hdn2, dh3, scale=0.5, name="ffn2_dwd"))
    dya, dyb, dpa, dpb, dgates = _merge_bwd(dh2, ya, yb, gates, wa, wb_p, wout)
    dwout = _wgrad_cols(merged, dh2, 1, name="dwout")[0]
    dwa = _wgrad_cols(ya, dpa, 1, name="dwa")[0]
    dwb = _unpair_heads(_wgrad_cols(yb, dpb, 1, name="dwb")[0], 0)
    dqa, dka, dva, dbias = _na_bwd(qkva, dya, bias)
    drpb = _rpb_fold(_na_unstack_slabs(dbias))
    dqb, dkb, dvb, dsink = _swa_bwd(qb, kvb, dyb, sink)
    dz, dh1, dgmix = _mix_in_bwd(dqa, dka, dva, dqb, dkb, dvb, dgates, h1, gmix, dh2, w_in_p, tables)
    dwin_p = _wgrad_cols(u, dz, 2, name="dwin")
    dwin_p = dwin_p.transpose(1, 0, 2).reshape(D_MODEL, D_IN)
    dwin = jnp.concatenate([dwin_p[:, :O_QB], _unpair_heads(dwin_p[:, O_QB:O_KB], 1), dwin_p[:, O_KB:]], axis=1)
    dx, da1, db1, hdn1, dg1 = _ffn_bwd(dh1, x, g1, a1, b1, *f1, name="ffn1_bwd")
    df1 = (_wgrad_shard_b(n1, da1, name="ffn1_dwg"), _wgrad_shard_b(n1, db1, name="ffn1_dwu"),
           _wgrad_shard_a(hdn1, dh1, scale=0.5, name="ffn1_dwd"))
    dsink_v = dsink[:, 0:2, 0].T.reshape(NB_HEADS)
    return dict(loss=loss, dx=dx, g1=dg1, f1=df1, gmix=dgmix, w_in=dwin, rpb=drpb, sink=dsink_v,
                wa=dwa, wb=dwb, wout=dwout, g2=dg2, f2=df2, gfin=dgfin)


ANY = pl.BlockSpec(memory_space=pl.ANY)


def _place():
    x, y, c = lax.axis_index("x"), lax.axis_index("y"), lax.axis_index("c")
    chips = [(1 - x, y), (x, 1 - y), (1 - x, 1 - y)]
    return x, y, c, 2 * x + y, chips


def _remote(src, dst, send_sems, recv_sems, k, device):
    return pltpu.make_async_remote_copy(src_ref=src, dst_ref=dst, send_sem=send_sems.at[k], recv_sem=recv_sems.at[k],
                                        device_id=device, device_id_type=MESH)


def _all_gather(shards):
    n = len(shards)

    def body(*refs):
        ins, outs = refs[:n], refs[n:2 * n]
        send_sems, recv_sems, own_send_sems, own_recv_sems = refs[2 * n:]
        x, y, c, mine, chips = _place()
        sibling = (x, y, 1 - c)
        own, sends, passes = [], [], []
        for i in range(n):
            hr = shards[i].shape[0] // 2
            cp = _remote(ins[i], outs[i].at[mine], own_send_sems, own_recv_sems, i, sibling)
            cp.start()
            own.append(cp)
            for j, (cx, cy) in enumerate(chips):
                cp = _remote(ins[i].at[pl.ds(c * hr, hr)], outs[i].at[mine, pl.ds(c * hr, hr)], send_sems, recv_sems,
                             6 * i + j, (cx, cy, c))
                cp.start()
                sends.append(cp)
        for i in range(n):
            hr = shards[i].shape[0] // 2
            for j, (cx, cy) in enumerate(chips):
                landed = outs[i].at[2 * cx + cy, pl.ds(c * hr, hr)]
                _remote(landed, landed, send_sems, recv_sems, 6 * i + j, (cx, cy, c)).wait_recv()
                cp = _remote(landed, landed, send_sems, recv_sems, 6 * i + 3 + j, sibling)
                cp.start()
                passes.append(cp)
        for i in range(n):
            hr = shards[i].shape[0] // 2
            for j, (cx, cy) in enumerate(chips):
                other = outs[i].at[2 * cx + cy, pl.ds((1 - c) * hr, hr)]
                _remote(other, other, send_sems, recv_sems, 6 * i + 3 + j, sibling).wait_recv()
        for cp in sends + passes:
            cp.wait_send()
        for cp in own:
            cp.wait()

    return pl.pallas_call(
        body, name="all_gather_weights",
        in_specs=[ANY] * n, out_specs=[ANY] * n,
        out_shape=[jax.ShapeDtypeStruct((N_CHIPS,) + s.shape, s.dtype) for s in shards],
        scratch_shapes=[pltpu.SemaphoreType.DMA((6 * n,)), pltpu.SemaphoreType.DMA((6 * n,)),
                        pltpu.SemaphoreType.DMA((n,)), pltpu.SemaphoreType.DMA((n,))],
    )(*shards)


def _rs_sibling(grads):
    n = len(grads)

    def body(*refs):
        ins, outs = refs[:n], refs[n:2 * n]
        send_sems, recv_sems = refs[2 * n:]
        x, y, c, _, _ = _place()
        copies = []
        for i in range(n):
            hr = grads[i].shape[1] // 2
            cp = _remote(ins[i].at[:, pl.ds((1 - c) * hr, hr)], outs[i], send_sems, recv_sems, i, (x, y, 1 - c))
            cp.start()
            copies.append(cp)
        for cp in copies:
            cp.wait()

    return pl.pallas_call(
        body, name="rs_sibling",
        in_specs=[ANY] * n, out_specs=[ANY] * n,
        out_shape=[jax.ShapeDtypeStruct((g.shape[0], g.shape[1] // 2, g.shape[2]), g.dtype) for g in grads],
        scratch_shapes=[pltpu.SemaphoreType.DMA((n,)), pltpu.SemaphoreType.DMA((n,))],
    )(*grads)


def _rs_chips(parts):
    n = len(parts)

    def body(*refs):
        ins, outs = refs[:n], refs[n:2 * n]
        send_sems, recv_sems = refs[2 * n:]
        _, _, c, _, chips = _place()
        copies = []
        for i in range(n):
            for j, (cx, cy) in enumerate(chips):
                cp = _remote(ins[i].at[2 * cx + cy], outs[i].at[j], send_sems, recv_sems, 3 * i + j, (cx, cy, c))
                cp.start()
                copies.append(cp)
        for cp in copies:
            cp.wait()

    return pl.pallas_call(
        body, name="rs_chips",
        in_specs=[ANY] * n, out_specs=[ANY] * n,
        out_shape=[jax.ShapeDtypeStruct((N_CHIPS - 1,) + p.shape[1:], p.dtype) for p in parts],
        scratch_shapes=[pltpu.SemaphoreType.DMA((3 * n,)), pltpu.SemaphoreType.DMA((3 * n,))],
    )(*parts)


def _rs_share(halves):
    n = len(halves)

    def body(*refs):
        ins, outs = refs[:n], refs[n:2 * n]
        send_sems, recv_sems = refs[2 * n:]
        x, y, c, _, _ = _place()
        copies = []
        for i in range(n):
            cp = _remote(ins[i], outs[i], send_sems, recv_sems, i, (x, y, 1 - c))
            cp.start()
            copies.append(cp)
        for cp in copies:
            cp.wait()

    return pl.pallas_call(
        body, name="rs_share",
        in_specs=[ANY] * n, out_specs=[ANY] * n,
        out_shape=[jax.ShapeDtypeStruct(h.shape, h.dtype) for h in halves],
        scratch_shapes=[pltpu.SemaphoreType.DMA((n,)), pltpu.SemaphoreType.DMA((n,))],
    )(*halves)


N_DEV = 8


def _small_allreduce(vec):
    R = vec.shape[0]

    def body(v_ref, o_ref, buf, send_sems, recv_sems):
        x, y, c, _, _ = _place()
        me = 4 * x + 2 * y + c
        buf[me] = v_ref[...]
        copies = []
        for k in range(1, N_DEV):
            peer = (x ^ (k >> 2), y ^ ((k >> 1) & 1), c ^ (k & 1))
            cp = _remote(v_ref, buf.at[me], send_sems, recv_sems, k - 1, peer)
            cp.start()
            copies.append(cp)
        for k, cp in enumerate(copies, start=1):
            cp.wait_send()
            landed = buf.at[me ^ k]
            _remote(landed, landed, send_sems, recv_sems, k - 1, (x, y, c)).wait_recv()
        acc = buf[0]
        for d in range(1, N_DEV):
            acc = acc + buf[d]
        o_ref[...] = acc

    return pl.pallas_call(
        body, name="small_allreduce",
        in_specs=[pl.BlockSpec(memory_space=pltpu.VMEM)], out_specs=pl.BlockSpec(memory_space=pltpu.VMEM),
        out_shape=jax.ShapeDtypeStruct(vec.shape, vec.dtype),
        scratch_shapes=[pltpu.VMEM((N_DEV, R, LANES), F32), pltpu.SemaphoreType.DMA((N_DEV - 1,)),
                        pltpu.SemaphoreType.DMA((N_DEV - 1,))],
    )(vec)


ELEMWISE_BLOCK = 256 * 1024


def _row_tile(rows, cols):
    best = None
    for t in range(8, rows + 1, 8):
        if rows % t == 0 and t * cols <= ELEMWISE_BLOCK:
            best = t
    return best if best is not None else rows


def _add_sibling(g, r1, cidx, *, name):
    S, R, C = g.shape
    hr = R // 2
    tr = _row_tile(hr, C)
    nt = hr // tr

    def body(c_ref, g_ref, r_ref, o_ref, o16_ref):
        p = g_ref[...] + r_ref[...]
        o_ref[...] = p
        o16_ref[...] = p.astype(BF16)

    blk = pl.BlockSpec((1, tr, C), lambda s, t, c: (s, t, 0))
    return pl.pallas_call(
        body, name=name,
        grid_spec=pltpu.PrefetchScalarGridSpec(
            num_scalar_prefetch=1, grid=(S, nt),
            in_specs=[pl.BlockSpec((1, tr, C), lambda s, t, c: (s, c[0] * nt + t, 0)), blk], out_specs=[blk, blk]),
        out_shape=[jax.ShapeDtypeStruct((S, hr, C), F32), jax.ShapeDtypeStruct((S, hr, C), BF16)],
        compiler_params=_params(2),
    )(cidx, g, r1)


def _add_chips(p, r2, chip, *, name):
    _, hr, C = p.shape
    tr = _row_tile(hr, C)

    def body(chip_ref, p_ref, r_ref, o_ref):
        o_ref[...] = ((p_ref[0] + r_ref[0].astype(F32)) + r_ref[1].astype(F32)) + r_ref[2].astype(F32)

    return pl.pallas_call(
        body, name=name,
        grid_spec=pltpu.PrefetchScalarGridSpec(
            num_scalar_prefetch=1, grid=(hr // tr,),
            in_specs=[pl.BlockSpec((1, tr, C), lambda t, s: (s[0], t, 0)), pl.BlockSpec((N_CHIPS - 1, tr, C), lambda t, s: (0, t, 0))],
            out_specs=pl.BlockSpec((tr, C), lambda t, s: (t, 0))),
        out_shape=jax.ShapeDtypeStruct((hr, C), F32),
        compiler_params=_params(1),
    )(chip, p, r2)


def _adamw_math(w, g, m, v):
    mn = ADAM_B1 * m + (1.0 - ADAM_B1) * g
    vn = ADAM_B2 * v + (1.0 - ADAM_B2) * (g * g)
    m_hat = mn / (1.0 - ADAM_B1 ** ADAM_STEP)
    v_hat = vn / (1.0 - ADAM_B2 ** ADAM_STEP)
    return -ADAM_LR * (m_hat / (jnp.sqrt(v_hat) + ADAM_EPS) + ADAM_WD * w), mn, vn


def _adamw_halves(w, mine, other, m, v, cidx, *, name):
    R, C = w.shape
    hr = R // 2
    tr = _row_tile(hr, C)
    nt = hr // tr

    def body(c_ref, w_ref, a_ref, b_ref, m_ref, v_ref, g_ref, d_ref, mo_ref, vo_ref):
        gv = jnp.where(pl.program_id(0) == c_ref[0], a_ref[...], b_ref[...])
        g_ref[...] = gv
        d_ref[...], mo_ref[...], vo_ref[...] = _adamw_math(w_ref[...], gv, m_ref[...], v_ref[...])

    full = pl.BlockSpec((tr, C), lambda h, t, c: (h * nt + t, 0))
    half = pl.BlockSpec((tr, C), lambda h, t, c: (t, 0))
    shape = jax.ShapeDtypeStruct((R, C), F32)
    return pl.pallas_call(
        body, name=name,
        grid_spec=pltpu.PrefetchScalarGridSpec(
            num_scalar_prefetch=1, grid=(2, nt), in_specs=[full, half, half, full, full], out_specs=[full] * 4),
        out_shape=[shape] * 4,
        compiler_params=_params(2),
    )(cidx, w, mine, other, m, v)


def _adamw(w, g, m, v, *, name):
    R, C = w.shape
    tr = _row_tile(R, C)

    def body(w_ref, g_ref, m_ref, v_ref, d_ref, mo_ref, vo_ref):
        d_ref[...], mo_ref[...], vo_ref[...] = _adamw_math(w_ref[...], g_ref[...], m_ref[...], v_ref[...])

    blk = pl.BlockSpec((tr, C), lambda t: (t, 0))
    shape = jax.ShapeDtypeStruct((R, C), F32)
    return pl.pallas_call(
        body, name=name, grid=(R // tr,),
        in_specs=[blk] * 4, out_specs=[blk] * 3, out_shape=[shape] * 3,
        compiler_params=_params(1),
    )(w, g, m, v)


def _stack_cols(w):
    r, n = w.shape
    return w.reshape(r, N_CHIPS, n // N_CHIPS).transpose(1, 0, 2)


def _unstack_cols(w):
    s, r, c = w.shape
    return w.transpose(1, 0, 2).reshape(r, s * c)


def _pad_rows(a, rows):
    return jnp.pad(a, ((0, rows - a.shape[0]), (0, LANES - a.shape[1])))


BIG = ("ffn1_w_gate", "ffn1_w_up", "ffn1_w_down", "w_in", "w_branch_a", "w_branch_b", "w_out",
       "ffn2_w_gate", "ffn2_w_up", "ffn2_w_down")
WEIGHTS = ("ffn1_norm", "ffn1_w_gate", "ffn1_w_up", "ffn1_w_down", "mix_norm", "w_in", "na_rpb", "sink_logit",
           "w_branch_a", "w_branch_b", "w_out", "ffn2_norm", "ffn2_w_gate", "ffn2_w_up", "ffn2_w_down", "final_norm")


def kernel(x, ffn1_norm, ffn1_w_gate, ffn1_w_up, ffn1_w_down, mix_norm, w_in, na_rpb, sink_logit, w_branch_a, w_branch_b, w_out, ffn2_norm, ffn2_w_gate, ffn2_w_up, ffn2_w_down, final_norm, loss_target, m_ffn1_norm, m_ffn1_w_gate, m_ffn1_w_up, m_ffn1_w_down, m_mix_norm, m_w_in, m_na_rpb, m_sink_logit, m_w_branch_a, m_w_branch_b, m_w_out, m_ffn2_norm, m_ffn2_w_gate, m_ffn2_w_up, m_ffn2_w_down, m_final_norm, v_ffn1_norm, v_ffn1_w_gate, v_ffn1_w_up, v_ffn1_w_down, v_mix_norm, v_w_in, v_na_rpb, v_sink_logit, v_w_branch_a, v_w_branch_b, v_w_out, v_ffn2_norm, v_ffn2_w_gate, v_ffn2_w_up, v_ffn2_w_down, v_final_norm):
    args = dict(locals())
    w = {k: args[k] for k in WEIGHTS}
    mom = {k: args["m_" + k] for k in WEIGHTS}
    var = {k: args["v_" + k] for k in WEIGHTS}
    cidx = lax.axis_index("c").astype(jnp.int32).reshape(1)
    chip = (2 * lax.axis_index("x") + lax.axis_index("y")).astype(jnp.int32).reshape(1)

    full = dict(zip(BIG, _all_gather([w[k][0].astype(BF16) for k in BIG])))
    f1 = (full["ffn1_w_gate"], full["ffn1_w_up"], full["ffn1_w_down"])
    f2 = (full["ffn2_w_gate"], full["ffn2_w_up"], full["ffn2_w_down"])
    out = _layer_grads(
        x[0], loss_target[0], ffn1_norm, f1, mix_norm, _unstack_cols(full["w_in"]), na_rpb[0], sink_logit[0],
        _unstack_cols(full["w_branch_a"]), _unstack_cols(full["w_branch_b"]), full["w_out"].reshape(D_MODEL, D_MODEL),
        ffn2_norm, f2, final_norm.reshape(1, D_MODEL))

    by_chip = dict(zip(BIG, [out["f1"][0], out["f1"][1], out["f1"][2], _stack_cols(out["w_in"]), _stack_cols(out["wa"]),
                             _stack_cols(out["wb"]), out["wout"].reshape(N_CHIPS, D_MODEL // N_CHIPS, D_MODEL),
                             out["f2"][0], out["f2"][1], out["f2"][2]]))
    grads = [by_chip[k] for k in BIG]
    from_sibling = _rs_sibling(grads)
    parts = [_add_sibling(g, r, cidx, name="add_sibling_" + k) for k, g, r in zip(BIG, grads, from_sibling)]
    from_chips = _rs_chips([p16 for _, p16 in parts])
    halves = [_add_chips(p, r, chip, name="add_chips_" + k) for k, (p, _), r in zip(BIG, parts, from_chips)]
    mine = dict(zip(BIG, halves))
    other = dict(zip(BIG, _rs_share(halves)))
    grad = {}

    rows = D_MODEL // LANES
    small = jnp.concatenate([
        out["g1"].reshape(rows, LANES), out["gmix"].reshape(rows, LANES), out["g2"].reshape(rows, LANES),
        out["gfin"].reshape(rows, LANES), out["rpb"].reshape(-1, LANES),
        _pad_rows(out["sink"].reshape(1, NB_HEADS), 8), _pad_rows(out["loss"], 8)], axis=0)
    total = _small_allreduce(small)
    n_rpb = NA_HEADS * 2 * NA_KH
    grad["ffn1_norm"] = total[0:rows].reshape(1, D_MODEL)
    grad["mix_norm"] = total[rows:2 * rows].reshape(1, D_MODEL)
    grad["ffn2_norm"] = total[2 * rows:3 * rows].reshape(1, D_MODEL)
    grad["final_norm"] = total[3 * rows:4 * rows].reshape(1, D_MODEL)
    grad["na_rpb"] = total[4 * rows:4 * rows + n_rpb].reshape(NA_HEADS, 2 * NA_KH, LANES)[:, :2 * NA_KH - 1, :2 * NA_KW - 1]
    grad["na_rpb"] = grad["na_rpb"].reshape(NA_HEADS, -1)
    grad["sink_logit"] = total[4 * rows + n_rpb:4 * rows + n_rpb + 1, 0:NB_HEADS]
    loss = total[4 * rows + n_rpb + 8, 0]

    deltas, new_m, new_v, grads_out = {}, {}, {}, {}
    for k in WEIGHTS:
        shape = w[k].shape
        if k in mine:
            two_d = w[k].shape[1:]
            g2d, d, mn, vn = _adamw_halves(w[k].reshape(two_d), mine[k], other[k], mom[k].reshape(two_d), var[k].reshape(two_d),
                                           cidx, name="adamw_" + k)
        else:
            g2d = grad[k]
            d, mn, vn = _adamw(w[k].reshape(g2d.shape), g2d, mom[k].reshape(g2d.shape), var[k].reshape(g2d.shape), name="adamw_" + k)
        grads_out[k], deltas[k], new_m[k], new_v[k] = (a.reshape(shape) for a in (g2d, d, mn, vn))
    return (loss, out["dx"].reshape(x.shape), *[grads_out[k] for k in WEIGHTS], *[deltas[k] for k in WEIGHTS],
            *[new_m[k] for k in WEIGHTS], *[new_v[k] for k in WEIGHTS])
```

```python
import functools
import math

import jax
import jax.numpy as jnp
from jax import lax
from jax.experimental import pallas as pl
from jax.experimental.pallas import tpu as pltpu

F32 = jnp.float32
BF16 = jnp.bfloat16

D_MODEL = 1024
HEAD_DIM = 64
NA_HEADS = 8
NB_HEADS = 8
GRID_W = 64
NA_KH = 8
NA_KW = 16
WIN = 128
ROPE_THETA = 10000.0
EPS = 1e-6
N_CHIPS = 4
QK_SCALE = HEAD_DIM ** -0.5
NEG = -1e30
LANES = 128
VMEM_LIMIT = 56 * 1024 * 1024

C_QKVA = 3 * NA_HEADS * HEAD_DIM
C_QB = NB_HEADS * HEAD_DIM
C_KB = 2 * HEAD_DIM
C_ROPE = C_QB + C_KB
C_GATES = 2 * D_MODEL
D_IN = C_QKVA + C_QB + 2 * C_KB + C_GATES
O_QB = C_QKVA
O_KB = O_QB + C_QB
O_VB = O_KB + C_KB
O_G = O_VB + C_KB

ADAM_LR = 0.001
ADAM_B1 = 0.9
ADAM_B2 = 0.999
ADAM_EPS = 1e-08
ADAM_WD = 0.01
ADAM_STEP = 10

MESH = pl.DeviceIdType.MESH


def _dot(a, b):
    return jnp.dot(a, b, preferred_element_type=F32)


def _dot_nt(a, b):
    return lax.dot_general(a, b, (((1,), (1,)), ((), ())), preferred_element_type=F32)


def _dot_tn(a, b):
    return lax.dot_general(a, b, (((0,), (0,)), ((), ())), preferred_element_type=F32)


def _params(n_axes):
    return pltpu.CompilerParams(dimension_semantics=("arbitrary",) * n_axes, vmem_limit_bytes=VMEM_LIMIT)


def _rstd(xf):
    return lax.rsqrt(jnp.mean(xf * xf, axis=-1, keepdims=True) + EPS)


def _norm_bwd(dn, xf, g, r):
    xhat = xf * r
    dxh = dn * g
    dx = r * (dxh - xhat * jnp.mean(dxh * xhat, axis=-1, keepdims=True))
    return dx, dn * xhat


def _ffn_fwd(x, g, wg, wu, wd, *, name, tm=512):
    T, D = x.shape
    F = wg.shape[1]

    def body(x_ref, g_ref, wg_ref, wu_ref, wd_ref, h_ref, n_ref, a_ref, b_ref):
        s = pl.program_id(1)

        @pl.when(s == 0)
        def _():
            xf = x_ref[...]
            n_ref[...] = ((xf * _rstd(xf)) * g_ref[...]).astype(BF16)
            h_ref[...] = xf

        n = n_ref[...]
        a = _dot_nt(n, wg_ref[0])
        b = _dot_nt(n, wu_ref[0])
        a_ref[0] = a.astype(BF16)
        b_ref[0] = b.astype(BF16)
        hdn = (a * jax.nn.sigmoid(a) * b).astype(BF16)
        h_ref[...] += 0.5 * _dot(hdn, wd_ref[0])

    tok = pl.BlockSpec((tm, D), lambda i, s: (i, 0))
    hid = pl.BlockSpec((1, tm, F), lambda i, s: (s, i, 0))
    return pl.pallas_call(
        body, name=name, grid=(T // tm, N_CHIPS),
        in_specs=[tok, pl.BlockSpec((1, D), lambda i, s: (0, 0)),
                  pl.BlockSpec((1, F, D), lambda i, s: (s, 0, 0)), pl.BlockSpec((1, F, D), lambda i, s: (s, 0, 0)),
                  pl.BlockSpec((1, F, D), lambda i, s: (s, 0, 0))],
        out_specs=[tok, tok, hid, hid],
        out_shape=[jax.ShapeDtypeStruct((T, D), F32), jax.ShapeDtypeStruct((T, D), BF16),
                   jax.ShapeDtypeStruct((N_CHIPS, T, F), BF16), jax.ShapeDtypeStruct((N_CHIPS, T, F), BF16)],
        compiler_params=_params(2),
    )(x, g, wg, wu, wd)


def _ffn_bwd(dh, x, g, a, b, wg, wu, wd, *, name, tm=512):
    T, D = x.shape
    F = wg.shape[1]

    def body(dh_ref, x_ref, g_ref, a_ref, b_ref, wg_ref, wu_ref, wd_ref, dx_ref, da_ref, db_ref, hdn_ref, dg_ref):
        i, s = pl.program_id(0), pl.program_id(1)

        @pl.when((i == 0) & (s == 0))
        def _():
            dg_ref[...] = jnp.zeros_like(dg_ref)

        @pl.when(s == 0)
        def _():
            dx_ref[...] = jnp.zeros_like(dx_ref)

        dhdn = _dot_nt((0.5 * dh_ref[...]).astype(BF16), wd_ref[0])
        af = a_ref[0].astype(F32)
        bf = b_ref[0].astype(F32)
        sg = jax.nn.sigmoid(af)
        silu = af * sg
        hdn_ref[0] = (silu * bf).astype(BF16)
        da = (dhdn * bf * (sg * (1.0 + af * (1.0 - sg)))).astype(BF16)
        db = (dhdn * silu).astype(BF16)
        da_ref[0] = da
        db_ref[0] = db
        dx_ref[...] += _dot(da, wg_ref[0]) + _dot(db, wu_ref[0])

        @pl.when(s == N_CHIPS - 1)
        def _():
            xf = x_ref[...]
            dx, dgr = _norm_bwd(dx_ref[...], xf, g_ref[...], _rstd(xf))
            dg_ref[...] += jnp.sum(dgr, axis=0, keepdims=True)
            dx_ref[...] = dh_ref[...] + dx

    tok = pl.BlockSpec((tm, D), lambda i, s: (i, 0))
    hid = pl.BlockSpec((1, tm, F), lambda i, s: (s, i, 0))
    vec = pl.BlockSpec((1, D), lambda i, s: (0, 0))
    hshape = jax.ShapeDtypeStruct((N_CHIPS, T, F), BF16)
    return pl.pallas_call(
        body, name=name, grid=(T // tm, N_CHIPS),
        in_specs=[tok, tok, vec, hid, hid,
                  pl.BlockSpec((1, F, D), lambda i, s: (s, 0, 0)), pl.BlockSpec((1, F, D), lambda i, s: (s, 0, 0)),
                  pl.BlockSpec((1, F, D), lambda i, s: (s, 0, 0))],
        out_specs=[tok, hid, hid, hid, vec],
        out_shape=[jax.ShapeDtypeStruct((T, D), F32), hshape, hshape, hshape, jax.ShapeDtypeStruct((1, D), F32)],
        compiler_params=_params(2),
    )(dh, x, g, a, b, wg, wu, wd)


def _wgrad(a, b, *, a_block, a_map, b_block, b_map, out_shape, o_block, o_map, grid, scale=1.0, name):
    def body(a_ref, b_ref, o_ref):
        @pl.when(pl.program_id(len(grid) - 1) == 0)
        def _():
            o_ref[...] = jnp.zeros_like(o_ref)

        av = a_ref[...]
        bv = b_ref[...]
        av = av.reshape(av.shape[-2:]).astype(BF16)
        bv = bv.reshape(bv.shape[-2:])
        if scale != 1.0:
            bv = scale * bv
        o_ref[...] += _dot_tn(av, bv.astype(BF16)).reshape(o_ref.shape)

    return pl.pallas_call(
        body, name=name, grid=grid,
        in_specs=[pl.BlockSpec(a_block, a_map), pl.BlockSpec(b_block, b_map)],
        out_specs=pl.BlockSpec(o_block, o_map),
        out_shape=jax.ShapeDtypeStruct(out_shape, F32),
        compiler_params=_params(len(grid)),
    )(a, b)


def _wgrad_rows(a, b, n_blocks, *, name, tk=512):
    T, N = b.shape
    M = a.shape[1] // n_blocks
    return _wgrad(a, b, a_block=(tk, M), a_map=lambda s, k: (k, s), b_block=(tk, N), b_map=lambda s, k: (k, 0),
                  out_shape=(n_blocks, M, N), o_block=(1, M, N), o_map=lambda s, k: (s, 0, 0), grid=(n_blocks, T // tk), name=name)


def _wgrad_shard_a(a, b, *, name, scale=1.0, tk=512):
    S, T, M = a.shape
    N = b.shape[1]
    return _wgrad(a, b, a_block=(1, tk, M), a_map=lambda s, k: (s, k, 0), b_block=(tk, N), b_map=lambda s, k: (k, 0),
                  out_shape=(S, M, N), o_block=(1, M, N), o_map=lambda s, k: (s, 0, 0), grid=(S, T // tk), scale=scale, name=name)


def _wgrad_cols(a, b, n_blocks, *, name, tk=512):
    T, M = a.shape
    N = b.shape[1] // n_blocks
    return _wgrad(a, b, a_block=(tk, M), a_map=lambda s, k: (k, 0), b_block=(tk, N), b_map=lambda s, k: (k, s),
                  out_shape=(n_blocks, M, N), o_block=(1, M, N), o_map=lambda s, k: (s, 0, 0), grid=(n_blocks, T // tk), name=name)


def _rope_tables(T):
    half = HEAD_DIM // 2
    inv = ROPE_THETA ** (-jnp.arange(half, dtype=F32) / half)
    ang = jnp.arange(T, dtype=F32)[:, None] * inv[None, :]
    cos, sin, zero = jnp.cos(ang), jnp.sin(ang), jnp.zeros_like(ang)
    reps = LANES // HEAD_DIM
    return (jnp.tile(jnp.concatenate([cos, cos], axis=1), (1, reps)),
            jnp.tile(jnp.concatenate([-sin, zero], axis=1), (1, reps)),
            jnp.tile(jnp.concatenate([zero, sin], axis=1), (1, reps)))


def _rope(x, cos, sa, sb, sign):
    half = HEAD_DIM // 2
    return x * cos + sign * (pltpu.roll(x, LANES - half, 1) * sa + pltpu.roll(x, half, 1) * sb)


def _mix_in_fwd(h, g, w_in, tables, *, tm=256):
    T, D = h.shape

    def body(h_ref, g_ref, w_ref, cos_ref, sa_ref, sb_ref, u_ref, qkva_ref, qb_ref, kvb_ref, gates_ref):
        hf = h_ref[...]
        u = ((hf * _rstd(hf)) * g_ref[...]).astype(BF16)
        u_ref[...] = u
        qkva_ref[...] = _dot_nt(u, w_ref[0:C_QKVA, :]).astype(BF16)
        zr = _dot_nt(u, w_ref[O_QB:O_QB + C_ROPE, :])
        cos, sa, sb = cos_ref[...], sa_ref[...], sb_ref[...]
        for j in range(C_ROPE // LANES):
            rj = _rope(zr[:, j * LANES:(j + 1) * LANES], cos, sa, sb, 1.0).astype(BF16)
            if j < C_QB // LANES:
                qb_ref[:, j * LANES:(j + 1) * LANES] = rj
            else:
                kvb_ref[:, 0:C_KB] = rj
        kvb_ref[:, C_KB:2 * C_KB] = _dot_nt(u, w_ref[O_VB:O_VB + C_KB, :]).astype(BF16)
        gates_ref[...] = _dot_nt(u, w_ref[O_G:O_G + C_GATES, :])

    def tok(n):
        return pl.BlockSpec((tm, n), lambda i: (i, 0))

    return pl.pallas_call(
        body, name="mix_in_fwd", grid=(T // tm,),
        in_specs=[tok(D), pl.BlockSpec((1, D), lambda i: (0, 0)), pl.BlockSpec((D_IN, D), lambda i: (0, 0)),
                  tok(LANES), tok(LANES), tok(LANES)],
        out_specs=[tok(D), tok(C_QKVA), tok(C_QB), tok(2 * C_KB), tok(C_GATES)],
        out_shape=[jax.ShapeDtypeStruct((T, D), BF16), jax.ShapeDtypeStruct((T, C_QKVA), BF16),
                   jax.ShapeDtypeStruct((T, C_QB), BF16), jax.ShapeDtypeStruct((T, 2 * C_KB), BF16),
                   jax.ShapeDtypeStruct((T, C_GATES), F32)],
        compiler_params=_params(1),
    )(h, g, w_in, *tables)


def _mix_in_bwd(dqa, dka, dva, dqb, dkb, dvb, dgates, h, g, dres, w_in, tables, *, tm=256):
    T, D = h.shape

    def body(dqa_ref, dka_ref, dva_ref, dqb_ref, dkb_ref, dvb_ref, dgt_ref, h_ref, g_ref, dres_ref, w_ref,
             cos_ref, sa_ref, sb_ref, dz_ref, dh_ref, dg_ref):
        @pl.when(pl.program_id(0) == 0)
        def _():
            dg_ref[...] = jnp.zeros_like(dg_ref)

        na = NA_HEADS * HEAD_DIM
        dz_ref[:, 0:na] = dqa_ref[...].astype(BF16)
        dz_ref[:, na:2 * na] = dka_ref[...].astype(BF16)
        dz_ref[:, 2 * na:3 * na] = dva_ref[...].astype(BF16)
        cos, sa, sb = cos_ref[...], sa_ref[...], sb_ref[...]
        for j in range(C_QB // LANES):
            dz_ref[:, O_QB + j * LANES:O_QB + (j + 1) * LANES] = _rope(
                dqb_ref[:, j * LANES:(j + 1) * LANES], cos, sa, sb, -1.0).astype(BF16)
        dz_ref[:, O_KB:O_KB + C_KB] = _rope(dkb_ref[...], cos, sa, sb, -1.0).astype(BF16)
        dz_ref[:, O_VB:O_VB + C_KB] = dvb_ref[...].astype(BF16)
        dz_ref[:, O_G:O_G + C_GATES] = dgt_ref[...].astype(BF16)
        du = _dot(dz_ref[...], w_ref[...])
        hf = h_ref[...]
        dx, dgr = _norm_bwd(du, hf, g_ref[...], _rstd(hf))
        dg_ref[...] += jnp.sum(dgr, axis=0, keepdims=True)
        dh_ref[...] = dres_ref[...] + dx

    def tok(n):
        return pl.BlockSpec((tm, n), lambda i: (i, 0))

    vec = pl.BlockSpec((1, D), lambda i: (0, 0))
    na = NA_HEADS * HEAD_DIM
    return pl.pallas_call(
        body, name="mix_in_bwd", grid=(T // tm,),
        in_specs=[tok(na), tok(na), tok(na), tok(C_QB), tok(C_KB), tok(C_KB), tok(C_GATES), tok(D), vec, tok(D),
                  pl.BlockSpec((D_IN, D), lambda i: (0, 0)), tok(LANES), tok(LANES), tok(LANES)],
        out_specs=[tok(D_IN), tok(D), vec],
        out_shape=[jax.ShapeDtypeStruct((T, D_IN), BF16), jax.ShapeDtypeStruct((T, D), F32),
                   jax.ShapeDtypeStruct((1, D), F32)],
        compiler_params=_params(1),
    )(dqa, dka, dva, dqb, dkb, dvb, dgates, h, g, dres, w_in, *tables)


def _na_bias_slabs(rpb):
    H = rpb.shape[0]
    ncell = GRID_W * GRID_W
    cell = jnp.arange(ncell)
    co = cell % GRID_W - cell // GRID_W + (NA_KW - 1)
    e_co = (jnp.arange(LANES)[:, None] == co[None, :]).astype(F32)
    table = jnp.pad(rpb, ((0, 0), (0, 1), (0, LANES - rpb.shape[2]))).reshape(H * 2 * NA_KH, LANES)

    def body(t_ref, e_ref, o_ref):
        o_ref[...] = jnp.dot(t_ref[...], e_ref[...], preferred_element_type=F32, precision=lax.Precision.HIGHEST)

    toeplitz = pl.pallas_call(
        body, name="rpb_unfold", out_shape=jax.ShapeDtypeStruct((H * 2 * NA_KH, ncell), F32),
        compiler_params=_params(0),
    )(table, e_co).reshape(H, 2 * NA_KH, GRID_W, GRID_W)
    c = jnp.arange(GRID_W)
    cs = jnp.clip(c - NA_KW // 2, 0, GRID_W - NA_KW)
    inwin = (c[None, :] >= cs[:, None]) & (c[None, :] < cs[:, None] + NA_KW)
    toeplitz = jnp.where(inwin[None, None], toeplitz, NEG)
    slabs = jnp.stack([toeplitz[:, r:r + NA_KH] for r in range(NA_KH)], axis=1)
    slabs = slabs.reshape(H // 2, 2, NA_KH, NA_KH, GRID_W, GRID_W).transpose(0, 2, 1, 4, 3, 5)
    return slabs.reshape(H // 2, NA_KH, 2 * GRID_W, NA_KH * GRID_W)


def _na_unstack_slabs(dslab):
    pairs = dslab.shape[0]
    d = dslab.reshape(pairs, NA_KH, 2, GRID_W, NA_KH * GRID_W).transpose(0, 2, 1, 3, 4)
    return d.reshape(2 * pairs, NA_KH, GRID_W, NA_KH * GRID_W)


def _half_masks(rows):
    lane = lax.broadcasted_iota(jnp.int32, (rows, LANES), 1)
    left = lane < HEAD_DIM
    return left, (left, jnp.logical_not(left))


def _stack_heads(x):
    left, halves = _half_masks(x.shape[0])
    xf = x.astype(F32)
    return jnp.concatenate([jnp.where(m, xf, 0.0).astype(BF16) for m in halves], axis=0)


def _unstack_heads(o):
    rows = o.shape[0] // 2
    left, _ = _half_masks(rows)
    return jnp.where(left, o[:rows], o[rows:])


def _na_row(j, t, rb, rows):
    r = j * rb + t
    rs = jnp.clip(r - NA_KH // 2, 0, rows - NA_KH)
    return pl.multiple_of(t * GRID_W, GRID_W), pl.multiple_of(rs * GRID_W, GRID_W), rs - r + (NA_KH - 1)


def _na_specs(T, rb):
    qrows = GRID_W * rb
    pairs = NA_HEADS // 2
    return ([pl.BlockSpec((qrows, LANES), lambda p, j: (j, p)),
             pl.BlockSpec((T, LANES), lambda p, j: (0, pairs + p)),
             pl.BlockSpec((T, LANES), lambda p, j: (0, 2 * pairs + p))],
            pl.BlockSpec((1, NA_KH, 2 * GRID_W, NA_KH * GRID_W), lambda p, j: (p, 0, 0, 0)))


def _softmax(s):
    p = jnp.exp(s - jnp.max(s, axis=-1, keepdims=True))
    return p / jnp.sum(p, axis=-1, keepdims=True)


def _na_probs(qs, ks, bias):
    return _softmax(_dot_nt(qs, ks) * QK_SCALE + bias)


def _na_fwd(qkva, bias, *, rb=8, group=4):
    T = qkva.shape[0]
    rows = T // GRID_W
    nkeys = NA_KH * GRID_W

    def body(q_ref, k_ref, v_ref, bias_ref, y_ref):
        j = pl.program_id(1)

        def rows_step(t, carry):
            at = [_na_row(j, t * group + u, rb, rows) for u in range(group)]
            s = [_dot_nt(_stack_heads(q_ref[pl.ds(q0, GRID_W), :]), k_ref[pl.ds(k0, nkeys), :]) for q0, k0, _ in at]
            p = [_softmax(su * QK_SCALE + bias_ref[0, ro0]) for su, (_, _, ro0) in zip(s, at)]
            o = [_dot(pu.astype(BF16), v_ref[pl.ds(k0, nkeys), :]) for pu, (_, k0, _) in zip(p, at)]
            for ou, (q0, _, _) in zip(o, at):
                y_ref[pl.ds(q0, GRID_W), :] = _unstack_heads(ou).astype(BF16)
            return carry

        lax.fori_loop(0, rb // group, rows_step, 0)

    qkv_specs, bias_spec = _na_specs(T, rb)
    return pl.pallas_call(
        body, name="na_fwd", grid=(NA_HEADS // 2, rows // rb),
        in_specs=qkv_specs + [bias_spec],
        out_specs=qkv_specs[0],
        out_shape=jax.ShapeDtypeStruct((T, NA_HEADS * HEAD_DIM), BF16),
        compiler_params=_params(2),
    )(qkva, qkva, qkva, bias)


def _na_bwd(qkva, dy, bias, *, rb=8, group=4):
    T = qkva.shape[0]
    rows = T // GRID_W
    nkeys = NA_KH * GRID_W

    def body(q_ref, k_ref, v_ref, dy_ref, bias_ref, dq_ref, dk_ref, dv_ref, dbias_ref):
        j = pl.program_id(1)

        @pl.when(j == 0)
        def _():
            dk_ref[...] = jnp.zeros_like(dk_ref)
            dv_ref[...] = jnp.zeros_like(dv_ref)
            dbias_ref[...] = jnp.zeros_like(dbias_ref)

        def rows_step(t, carry):
            at = [_na_row(j, t * group + u, rb, rows) for u in range(group)]
            qs = [_stack_heads(q_ref[pl.ds(q0, GRID_W), :]) for q0, _, _ in at]
            dys = [_stack_heads(dy_ref[pl.ds(q0, GRID_W), :]) for q0, _, _ in at]
            s = [_dot_nt(qu, k_ref[pl.ds(k0, nkeys), :]) for qu, (_, k0, _) in zip(qs, at)]
            dp = [_dot_nt(du, v_ref[pl.ds(k0, nkeys), :]) for du, (_, k0, _) in zip(dys, at)]
            p = [_softmax(su * QK_SCALE + bias_ref[0, ro0]) for su, (_, _, ro0) in zip(s, at)]
            ds = [pu * (du - jnp.sum(pu * du, axis=-1, keepdims=True)) for pu, du in zip(p, dp)]
            for u, (q0, k0, ro0) in enumerate(at):
                dbias_ref[0, ro0] += ds[u]
                dsb = ds[u].astype(BF16)
                dq_ref[pl.ds(q0, GRID_W), :] = (_unstack_heads(_dot(dsb, k_ref[pl.ds(k0, nkeys), :])) * QK_SCALE).astype(BF16)
                dk_ref[pl.ds(k0, nkeys), :] += _dot_tn(dsb, qs[u]) * QK_SCALE
                dv_ref[pl.ds(k0, nkeys), :] += _dot_tn(p[u].astype(BF16), dys[u])
            return carry

        lax.fori_loop(0, rb // group, rows_step, 0)

    qkv_specs, bias_spec = _na_specs(T, rb)
    width = NA_HEADS * HEAD_DIM
    kv_out = pl.BlockSpec((T, LANES), lambda p, j: (0, p))
    return pl.pallas_call(
        body, name="na_bwd", grid=(NA_HEADS // 2, rows // rb),
        in_specs=qkv_specs + [qkv_specs[0], bias_spec],
        out_specs=[qkv_specs[0], kv_out, kv_out, bias_spec],
        out_shape=[jax.ShapeDtypeStruct((T, width), BF16), jax.ShapeDtypeStruct((T, width), F32),
                   jax.ShapeDtypeStruct((T, width), F32), jax.ShapeDtypeStruct(bias.shape, F32)],
        compiler_params=_params(2),
    )(qkva, qkva, qkva, dy, bias)


def _rpb_fold(dslab):
    H = dslab.shape[0]
    nro = NA_KH * NA_KH
    ncell = GRID_W * GRID_W
    xs = dslab.reshape(H, NA_KH, GRID_W, NA_KH, GRID_W).transpose(0, 1, 3, 2, 4).reshape(H, nro, ncell)
    cell = jnp.arange(ncell)
    co = cell % GRID_W - cell // GRID_W + (NA_KW - 1)
    e_co = (co[:, None] == jnp.arange(LANES)[None, :]).astype(F32)
    pair = jnp.arange(nro)
    e_ro = ((pair // NA_KH + pair % NA_KH)[None, :] == jnp.arange(2 * NA_KH)[:, None]).astype(F32)

    def body(x_ref, eco_ref, ero_ref, o_ref):
        y = jnp.dot(x_ref[0], eco_ref[...], preferred_element_type=F32, precision=lax.Precision.HIGHEST)
        o_ref[0] = jnp.dot(ero_ref[...], y, preferred_element_type=F32, precision=lax.Precision.HIGHEST)

    return pl.pallas_call(
        body, name="rpb_fold", grid=(H,),
        in_specs=[pl.BlockSpec((1, nro, ncell), lambda h: (h, 0, 0)), pl.BlockSpec((ncell, LANES), lambda h: (0, 0)),
                  pl.BlockSpec((2 * NA_KH, nro), lambda h: (0, 0))],
        out_specs=pl.BlockSpec((1, 2 * NA_KH, LANES), lambda h: (h, 0, 0)),
        out_shape=jax.ShapeDtypeStruct((H, 2 * NA_KH, LANES), F32),
        compiler_params=_params(1),
    )(xs, e_co, e_ro)


SWA_KEYS = 3 * WIN


def _swa_block(j, t, qbn, T):
    blk = j * qbn + t
    start = jnp.clip((blk - 1) * WIN, 0, T - SWA_KEYS)
    row = lax.broadcasted_iota(jnp.int32, (2 * WIN, SWA_KEYS), 0)
    qpos = blk * WIN + jnp.where(row < WIN, row, row - WIN)
    kpos = start + lax.broadcasted_iota(jnp.int32, (2 * WIN, SWA_KEYS), 1)
    return pl.multiple_of(t * WIN, WIN), pl.multiple_of(start, WIN), jnp.abs(qpos - kpos) <= WIN


def _swa_sinks(sink_ref, p):
    row = lax.broadcasted_iota(jnp.int32, (2 * WIN, 1), 0)
    return jnp.where(row < WIN, sink_ref[p], sink_ref[p + NB_HEADS // 2])


def _swa_probs(s, mask, sink):
    s = jnp.where(mask, s * QK_SCALE, NEG)
    m = jnp.maximum(jnp.max(s, axis=-1, keepdims=True), sink)
    e = jnp.exp(s - m)
    esink = jnp.exp(sink - m)
    den = jnp.sum(e, axis=-1, keepdims=True) + esink
    return e / den, esink / den


def _swa_specs(T, qbn):
    return [pl.BlockSpec(memory_space=pltpu.SMEM),
            pl.BlockSpec((WIN * qbn, LANES), lambda p, j: (j, p)),
            pl.BlockSpec((T, LANES), lambda p, j: (0, 0)),
            pl.BlockSpec((T, LANES), lambda p, j: (0, 1))]


def _swa_fwd(qb, kvb, sink, *, qbn=4, group=4):
    T = qb.shape[0]
    pairs = NB_HEADS // 2

    def body(sink_ref, q_ref, k_ref, v_ref, y_ref):
        p, j = pl.program_id(0), pl.program_id(1)
        sinks = _swa_sinks(sink_ref, p)

        def blocks_step(t, carry):
            at = [_swa_block(j, t * group + u, qbn, T) for u in range(group)]
            s = [_dot_nt(_stack_heads(q_ref[pl.ds(q0, WIN), :]), k_ref[pl.ds(k0, SWA_KEYS), :]) for q0, k0, _ in at]
            pr = [_swa_probs(su, mask, sinks)[0] for su, (_, _, mask) in zip(s, at)]
            o = [_dot(pu.astype(BF16), v_ref[pl.ds(k0, SWA_KEYS), :]) for pu, (_, k0, _) in zip(pr, at)]
            for ou, (q0, _, _) in zip(o, at):
                y_ref[pl.ds(q0, WIN), :] = _unstack_heads(ou).astype(BF16)
            return carry

        lax.fori_loop(0, qbn // group, blocks_step, 0)

    specs = _swa_specs(T, qbn)
    return pl.pallas_call(
        body, name="swa_fwd", grid=(pairs, T // (WIN * qbn)),
        in_specs=specs, out_specs=specs[1],
        out_shape=jax.ShapeDtypeStruct((T, NB_HEADS * HEAD_DIM), BF16),
        compiler_params=_params(2),
    )(sink, qb, kvb, kvb)


def _swa_bwd(qb, kvb, dy, sink, *, qbn=4, group=4):
    T = qb.shape[0]
    pairs = NB_HEADS // 2

    def body(sink_ref, q_ref, k_ref, v_ref, dy_ref, dq_ref, dk_ref, dv_ref, dsink_ref):
        p, j = pl.program_id(0), pl.program_id(1)
        sinks = _swa_sinks(sink_ref, p)

        @pl.when((p == 0) & (j == 0))
        def _():
            dk_ref[...] = jnp.zeros_like(dk_ref)
            dv_ref[...] = jnp.zeros_like(dv_ref)

        @pl.when(j == 0)
        def _():
            dsink_ref[...] = jnp.zeros_like(dsink_ref)

        def blocks_step(t, carry):
            at = [_swa_block(j, t * group + u, qbn, T) for u in range(group)]
            qs = [_stack_heads(q_ref[pl.ds(q0, WIN), :]) for q0, _, _ in at]
            dys = [_stack_heads(dy_ref[pl.ds(q0, WIN), :]) for q0, _, _ in at]
            s = [_dot_nt(qu, k_ref[pl.ds(k0, SWA_KEYS), :]) for qu, (_, k0, _) in zip(qs, at)]
            dp = [_dot_nt(du, v_ref[pl.ds(k0, SWA_KEYS), :]) for du, (_, k0, _) in zip(dys, at)]
            probs = [_swa_probs(su, mask, sinks) for su, (_, _, mask) in zip(s, at)]
            for u, (q0, k0, _) in enumerate(at):
                pr, psink = probs[u]
                delta = jnp.sum(pr * dp[u], axis=-1, keepdims=True)
                dsb = (pr * (dp[u] - delta)).astype(BF16)
                dsk = psink * delta
                for hh in range(2):
                    dsink_ref[0, hh:hh + 1, :] += jnp.broadcast_to(-jnp.sum(dsk[hh * WIN:(hh + 1) * WIN]), (1, LANES))
                dq_ref[pl.ds(q0, WIN), :] = _unstack_heads(_dot(dsb, k_ref[pl.ds(k0, SWA_KEYS), :])) * QK_SCALE
                dk_ref[pl.ds(k0, SWA_KEYS), :] += _dot_tn(dsb, qs[u]) * QK_SCALE
                dv_ref[pl.ds(k0, SWA_KEYS), :] += _dot_tn(pr.astype(BF16), dys[u])
            return carry

        lax.fori_loop(0, qbn // group, blocks_step, 0)

    specs = _swa_specs(T, qbn)
    kv_out = pl.BlockSpec((T, LANES), lambda p, j: (0, 0))
    return pl.pallas_call(
        body, name="swa_bwd", grid=(pairs, T // (WIN * qbn)),
        in_specs=specs + [specs[1]],
        out_specs=[specs[1], kv_out, kv_out, pl.BlockSpec((1, 8, LANES), lambda p, j: (p, 0, 0))],
        out_shape=[jax.ShapeDtypeStruct((T, NB_HEADS * HEAD_DIM), F32), jax.ShapeDtypeStruct((T, LANES), F32),
                   jax.ShapeDtypeStruct((T, LANES), F32), jax.ShapeDtypeStruct((pairs, 8, LANES), F32)],
        compiler_params=_params(2),
    )(sink, qb, kvb, kvb, dy)


def _merge_fwd(ya, yb, gates, wa, wb, wout, h, *, tm=512):
    T, D = h.shape
    W = ya.shape[1]

    def body(ya_ref, yb_ref, gt_ref, wa_ref, wb_ref, wo_ref, h_ref, h2_ref, mg_ref):
        pa = _dot(ya_ref[...], wa_ref[...])
        pb = _dot(yb_ref[...], wb_ref[...])
        mg = (jax.nn.sigmoid(gt_ref[:, 0:D]) * pa + jax.nn.sigmoid(gt_ref[:, D:2 * D]) * pb).astype(BF16)
        mg_ref[...] = mg
        h2_ref[...] = h_ref[...] + _dot(mg, wo_ref[...])

    def tok(n):
        return pl.BlockSpec((tm, n), lambda i: (i, 0))

    def full(r, c):
        return pl.BlockSpec((r, c), lambda i: (0, 0))

    return pl.pallas_call(
        body, name="merge_fwd", grid=(T // tm,),
        in_specs=[tok(W), tok(W), tok(2 * D), full(W, D), full(W, D), full(D, D), tok(D)],
        out_specs=[tok(D), tok(D)],
        out_shape=[jax.ShapeDtypeStruct((T, D), F32), jax.ShapeDtypeStruct((T, D), BF16)],
        compiler_params=_params(1),
    )(ya, yb, gates, wa, wb, wout, h)


def _merge_bwd(dh, ya, yb, gates, wa, wb, wout, *, tm=512):
    T, D = dh.shape
    W = ya.shape[1]

    def body(dh_ref, ya_ref, yb_ref, gt_ref, wa_ref, wb_ref, wo_ref, dya_ref, dyb_ref, dpa_ref, dpb_ref, dgt_ref):
        dmg = _dot_nt(dh_ref[...].astype(BF16), wo_ref[...])
        for y_ref, w_ref, dy_ref, dp_ref, lo in ((ya_ref, wa_ref, dya_ref, dpa_ref, 0), (yb_ref, wb_ref, dyb_ref, dpb_ref, D)):
            sg = jax.nn.sigmoid(gt_ref[:, lo:lo + D])
            dp = (dmg * sg).astype(BF16)
            dp_ref[...] = dp
            dgt_ref[:, lo:lo + D] = (dmg * _dot(y_ref[...], w_ref[...]) * (sg * (1.0 - sg))).astype(BF16)
            dy_ref[...] = _dot_nt(dp, w_ref[...]).astype(BF16)

    def tok(n):
        return pl.BlockSpec((tm, n), lambda i: (i, 0))

    def full(r, c):
        return pl.BlockSpec((r, c), lambda i: (0, 0))

    return pl.pallas_call(
        body, name="merge_bwd", grid=(T // tm,),
        in_specs=[tok(D), tok(W), tok(W), tok(2 * D), full(W, D), full(W, D), full(D, D)],
        out_specs=[tok(W), tok(W), tok(D), tok(D), tok(2 * D)],
        out_shape=[jax.ShapeDtypeStruct((T, W), BF16), jax.ShapeDtypeStruct((T, W), BF16),
                   jax.ShapeDtypeStruct((T, D), BF16), jax.ShapeDtypeStruct((T, D), BF16),
                   jax.ShapeDtypeStruct((T, 2 * D), BF16)],
        compiler_params=_params(1),
    )(dh, ya, yb, gates, wa, wb, wout)


def _final_loss(h, g, target, *, tm=512):
    T, D = h.shape

    def body(h_ref, g_ref, t_ref, dh_ref, loss_ref, dg_ref):
        @pl.when(pl.program_id(0) == 0)
        def _():
            loss_ref[...] = jnp.zeros_like(loss_ref)
            dg_ref[...] = jnp.zeros_like(dg_ref)

        hf = h_ref[...]
        r = _rstd(hf)
        gv = g_ref[...]
        err = (hf * r) * gv - t_ref[...]
        loss_ref[...] += jnp.broadcast_to(0.5 * jnp.sum(jnp.mean(err * err, axis=-1, keepdims=True)), loss_ref.shape)
        dx, dgr = _norm_bwd(err * (1.0 / D), hf, gv, r)
        dg_ref[...] += jnp.sum(dgr, axis=0, keepdims=True)
        dh_ref[...] = dx

    tok = pl.BlockSpec((tm, D), lambda i: (i, 0))
    vec = pl.BlockSpec((1, D), lambda i: (0, 0))
    return pl.pallas_call(
        body, name="final_loss", grid=(T // tm,),
        in_specs=[tok, vec, tok],
        out_specs=[tok, pl.BlockSpec((1, LANES), lambda i: (0, 0)), vec],
        out_shape=[jax.ShapeDtypeStruct((T, D), F32), jax.ShapeDtypeStruct((1, LANES), F32),
                   jax.ShapeDtypeStruct((1, D), F32)],
        compiler_params=_params(1),
    )(h, g, target)


def _pair_heads(a, axis):
    shp = a.shape
    a = a.reshape(shp[:axis] + (2, NB_HEADS // 2, HEAD_DIM) + shp[axis + 1:])
    return jnp.swapaxes(a, axis, axis + 1).reshape(shp)


def _unpair_heads(a, axis):
    shp = a.shape
    a = a.reshape(shp[:axis] + (NB_HEADS // 2, 2, HEAD_DIM) + shp[axis + 1:])
    return jnp.swapaxes(a, axis, axis + 1).reshape(shp)


def _layer_grads(x, target, g1, f1, gmix, w_in_t, rpb, sink, wa, wb, wout, g2, f2, gfin):
    T = x.shape[0]
    tables = _rope_tables(T)
    w_in_p = jnp.concatenate([w_in_t[:O_QB], _pair_heads(w_in_t[O_QB:O_KB], 0), w_in_t[O_KB:]], axis=0)
    wb_p = _pair_heads(wb, 0)
    bias = _na_bias_slabs(rpb)

    h1, n1, a1, b1 = _ffn_fwd(x, g1, *f1, name="ffn1_fwd")
    u, qkva, qb, kvb, gates = _mix_in_fwd(h1, gmix, w_in_p, tables)
    ya = _na_fwd(qkva, bias)
    yb = _swa_fwd(qb, kvb, sink)
    h2, merged = _merge_fwd(ya, yb, gates, wa, wb_p, wout, h1)
    h3, n2, a2, b2 = _ffn_fwd(h2, g2, *f2, name="ffn2_fwd")
    dh3, loss, dgfin = _final_loss(h3, gfin, target)

    dh2, da2, db2, hdn2, dg2 = _ffn_bwd(dh3, h2, g2, a2, b2, *f2, name="ffn2_bwd")
    df2 = (_wgrad_shard_a(da2, n2, name="ffn2_dwg"), _wgrad_shard_a(db2, n2, name="ffn2_dwu"),
           _wgrad_shard_a(hdn2, dh3, scale=0.5, name="ffn2_dwd"))
    dya, dyb, dpa, dpb, dgates = _merge_bwd(dh2, ya, yb, gates, wa, wb_p, wout)
    dwout = _wgrad_cols(merged, dh2, 1, name="dwout")[0]
    dwa = _wgrad_cols(ya, dpa, N_CHIPS, name="dwa")
    dwb = _unpair_heads(_wgrad_cols(yb, dpb, N_CHIPS, name="dwb"), 1)
    dqa, dka, dva, dbias = _na_bwd(qkva, dya, bias)
    drpb = _rpb_fold(_na_unstack_slabs(dbias))
    dqb, dkb, dvb, dsink = _swa_bwd(qb, kvb, dyb, sink)
    dz, dh1, dgmix = _mix_in_bwd(dqa, dka, dva, dqb, dkb, dvb, dgates, h1, gmix, dh2, w_in_p, tables)
    dwin_p = _wgrad_rows(dz, u, 2, name="dwin").reshape(D_IN, D_MODEL)
    dwin = jnp.concatenate([dwin_p[:O_QB], _unpair_heads(dwin_p[O_QB:O_KB], 0), dwin_p[O_KB:]], axis=0)
    dx, da1, db1, hdn1, dg1 = _ffn_bwd(dh1, x, g1, a1, b1, *f1, name="ffn1_bwd")
    df1 = (_wgrad_shard_a(da1, n1, name="ffn1_dwg"), _wgrad_shard_a(db1, n1, name="ffn1_dwu"),
           _wgrad_shard_a(hdn1, dh1, scale=0.5, name="ffn1_dwd"))
    dsink_v = dsink[:, 0:2, 0].T.reshape(NB_HEADS)
    return dict(loss=loss, dx=dx, g1=dg1, f1=df1, gmix=dgmix, w_in=dwin, rpb=drpb, sink=dsink_v,
                wa=dwa, wb=dwb, wout=dwout, g2=dg2, f2=df2, gfin=dgfin)


ANY = pl.BlockSpec(memory_space=pl.ANY)


def _place():
    x, y, c = lax.axis_index("x"), lax.axis_index("y"), lax.axis_index("c")
    chips = [(1 - x, y), (x, 1 - y), (1 - x, 1 - y)]
    return x, y, c, 2 * x + y, chips


def _remote(src, dst, send_sems, recv_sems, k, device):
    return pltpu.make_async_remote_copy(src_ref=src, dst_ref=dst, send_sem=send_sems.at[k], recv_sem=recv_sems.at[k],
                                        device_id=device, device_id_type=MESH)


def _all_gather(shards):
    n = len(shards)

    def body(*refs):
        ins, outs = refs[:n], refs[n:2 * n]
        send_sems, recv_sems, own_send_sems, own_recv_sems = refs[2 * n:]
        x, y, c, mine, chips = _place()
        sibling = (x, y, 1 - c)
        own, sends, passes = [], [], []
        for i in range(n):
            hr = shards[i].shape[0] // 2
            cp = _remote(ins[i], outs[i].at[mine], own_send_sems, own_recv_sems, i, sibling)
            cp.start()
            own.append(cp)
            for j, (cx, cy) in enumerate(chips):
                cp = _remote(ins[i].at[pl.ds(c * hr, hr)], outs[i].at[mine, pl.ds(c * hr, hr)], send_sems, recv_sems,
                             6 * i + j, (cx, cy, c))
                cp.start()
                sends.append(cp)
        for i in range(n):
            hr = shards[i].shape[0] // 2
            for j, (cx, cy) in enumerate(chips):
                landed = outs[i].at[2 * cx + cy, pl.ds(c * hr, hr)]
                _remote(landed, landed, send_sems, recv_sems, 6 * i + j, (cx, cy, c)).wait_recv()
                cp = _remote(landed, landed, send_sems, recv_sems, 6 * i + 3 + j, sibling)
                cp.start()
                passes.append(cp)
        for i in range(n):
            hr = shards[i].shape[0] // 2
            for j, (cx, cy) in enumerate(chips):
                other = outs[i].at[2 * cx + cy, pl.ds((1 - c) * hr, hr)]
                _remote(other, other, send_sems, recv_sems, 6 * i + 3 + j, sibling).wait_recv()
        for cp in sends + passes:
            cp.wait_send()
        for cp in own:
            cp.wait()

    return pl.pallas_call(
        body, name="all_gather_weights",
        in_specs=[ANY] * n, out_specs=[ANY] * n,
        out_shape=[jax.ShapeDtypeStruct((N_CHIPS,) + s.shape, s.dtype) for s in shards],
        scratch_shapes=[pltpu.SemaphoreType.DMA((6 * n,)), pltpu.SemaphoreType.DMA((6 * n,)),
                        pltpu.SemaphoreType.DMA((n,)), pltpu.SemaphoreType.DMA((n,))],
    )(*shards)


def _rs_sibling(grads):
    n = len(grads)

    def body(*refs):
        ins, outs = refs[:n], refs[n:2 * n]
        send_sems, recv_sems = refs[2 * n:]
        x, y, c, _, _ = _place()
        copies = []
        for i in range(n):
            hr = grads[i].shape[1] // 2
            cp = _remote(ins[i].at[:, pl.ds((1 - c) * hr, hr)], outs[i], send_sems, recv_sems, i, (x, y, 1 - c))
            cp.start()
            copies.append(cp)
        for cp in copies:
            cp.wait()

    return pl.pallas_call(
        body, name="rs_sibling",
        in_specs=[ANY] * n, out_specs=[ANY] * n,
        out_shape=[jax.ShapeDtypeStruct((g.shape[0], g.shape[1] // 2, g.shape[2]), g.dtype) for g in grads],
        scratch_shapes=[pltpu.SemaphoreType.DMA((n,)), pltpu.SemaphoreType.DMA((n,))],
    )(*grads)


def _rs_chips(parts):
    n = len(parts)

    def body(*refs):
        ins, outs = refs[:n], refs[n:2 * n]
        send_sems, recv_sems = refs[2 * n:]
        _, _, c, _, chips = _place()
        copies = []
        for i in range(n):
            for j, (cx, cy) in enumerate(chips):
                cp = _remote(ins[i].at[2 * cx + cy], outs[i].at[j], send_sems, recv_sems, 3 * i + j, (cx, cy, c))
                cp.start()
                copies.append(cp)
        for cp in copies:
            cp.wait()

    return pl.pallas_call(
        body, name="rs_chips",
        in_specs=[ANY] * n, out_specs=[ANY] * n,
        out_shape=[jax.ShapeDtypeStruct((N_CHIPS - 1,) + p.shape[1:], p.dtype) for p in parts],
        scratch_shapes=[pltpu.SemaphoreType.DMA((3 * n,)), pltpu.SemaphoreType.DMA((3 * n,))],
    )(*parts)


def _rs_share(halves):
    n = len(halves)

    def body(*refs):
        ins, outs = refs[:n], refs[n:2 * n]
        send_sems, recv_sems = refs[2 * n:]
        x, y, c, _, _ = _place()
        copies = []
        for i in range(n):
            cp = _remote(ins[i], outs[i], send_sems, recv_sems, i, (x, y, 1 - c))
            cp.start()
            copies.append(cp)
        for cp in copies:
            cp.wait()

    return pl.pallas_call(
        body, name="rs_share",
        in_specs=[ANY] * n, out_specs=[ANY] * n,
        out_shape=[jax.ShapeDtypeStruct(h.shape, h.dtype) for h in halves],
        scratch_shapes=[pltpu.SemaphoreType.DMA((n,)), pltpu.SemaphoreType.DMA((n,))],
    )(*halves)


N_DEV = 8


def _small_allreduce(vec):
    R = vec.shape[0]

    def body(v_ref, o_ref, buf, send_sems, recv_sems):
        x, y, c, _, _ = _place()
        me = 4 * x + 2 * y + c
        buf[me] = v_ref[...]
        copies = []
        for k in range(1, N_DEV):
            peer = (x ^ (k >> 2), y ^ ((k >> 1) & 1), c ^ (k & 1))
            cp = _remote(v_ref, buf.at[me], send_sems, recv_sems, k - 1, peer)
            cp.start()
            copies.append(cp)
        for k, cp in enumerate(copies, start=1):
            cp.wait_send()
            landed = buf.at[me ^ k]
            _remote(landed, landed, send_sems, recv_sems, k - 1, (x, y, c)).wait_recv()
        acc = buf[0]
        for d in range(1, N_DEV):
            acc = acc + buf[d]
        o_ref[...] = acc

    return pl.pallas_call(
        body, name="small_allreduce",
        in_specs=[pl.BlockSpec(memory_space=pltpu.VMEM)], out_specs=pl.BlockSpec(memory_space=pltpu.VMEM),
        out_shape=jax.ShapeDtypeStruct(vec.shape, vec.dtype),
        scratch_shapes=[pltpu.VMEM((N_DEV, R, LANES), F32), pltpu.SemaphoreType.DMA((N_DEV - 1,)),
                        pltpu.SemaphoreType.DMA((N_DEV - 1,))],
    )(vec)


ELEMWISE_BLOCK = 256 * 1024


def _row_tile(rows, cols):
    best = None
    for t in range(8, rows + 1, 8):
        if rows % t == 0 and t * cols <= ELEMWISE_BLOCK:
            best = t
    return best if best is not None else rows


def _add_sibling(g, r1, cidx, *, name):
    S, R, C = g.shape
    hr = R // 2
    tr = _row_tile(hr, C)
    nt = hr // tr

    def body(c_ref, g_ref, r_ref, o_ref, o16_ref):
        p = g_ref[...] + r_ref[...]
        o_ref[...] = p
        o16_ref[...] = p.astype(BF16)

    blk = pl.BlockSpec((1, tr, C), lambda s, t, c: (s, t, 0))
    return pl.pallas_call(
        body, name=name,
        grid_spec=pltpu.PrefetchScalarGridSpec(
            num_scalar_prefetch=1, grid=(S, nt),
            in_specs=[pl.BlockSpec((1, tr, C), lambda s, t, c: (s, c[0] * nt + t, 0)), blk], out_specs=[blk, blk]),
        out_shape=[jax.ShapeDtypeStruct((S, hr, C), F32), jax.ShapeDtypeStruct((S, hr, C), BF16)],
        compiler_params=_params(2),
    )(cidx, g, r1)


def _add_chips(p, r2, chip, *, name):
    _, hr, C = p.shape
    tr = _row_tile(hr, C)

    def body(chip_ref, p_ref, r_ref, o_ref):
        o_ref[...] = ((p_ref[0] + r_ref[0].astype(F32)) + r_ref[1].astype(F32)) + r_ref[2].astype(F32)

    return pl.pallas_call(
        body, name=name,
        grid_spec=pltpu.PrefetchScalarGridSpec(
            num_scalar_prefetch=1, grid=(hr // tr,),
            in_specs=[pl.BlockSpec((1, tr, C), lambda t, s: (s[0], t, 0)), pl.BlockSpec((N_CHIPS - 1, tr, C), lambda t, s: (0, t, 0))],
            out_specs=pl.BlockSpec((tr, C), lambda t, s: (t, 0))),
        out_shape=jax.ShapeDtypeStruct((hr, C), F32),
        compiler_params=_params(1),
    )(chip, p, r2)


def _adamw_math(w, g, m, v):
    mn = ADAM_B1 * m + (1.0 - ADAM_B1) * g
    vn = ADAM_B2 * v + (1.0 - ADAM_B2) * (g * g)
    m_hat = mn / (1.0 - ADAM_B1 ** ADAM_STEP)
    v_hat = vn / (1.0 - ADAM_B2 ** ADAM_STEP)
    return -ADAM_LR * (m_hat / (jnp.sqrt(v_hat) + ADAM_EPS) + ADAM_WD * w), mn, vn


def _adamw_halves(w, mine, other, m, v, cidx, *, name):
    R, C = w.shape
    hr = R // 2
    tr = _row_tile(hr, C)
    nt = hr // tr

    def body(c_ref, w_ref, a_ref, b_ref, m_ref, v_ref, g_ref, d_ref, mo_ref, vo_ref):
        gv = jnp.where(pl.program_id(0) == c_ref[0], a_ref[...], b_ref[...])
        g_ref[...] = gv
        d_ref[...], mo_ref[...], vo_ref[...] = _adamw_math(w_ref[...], gv, m_ref[...], v_ref[...])

    full = pl.BlockSpec((tr, C), lambda h, t, c: (h * nt + t, 0))
    half = pl.BlockSpec((tr, C), lambda h, t, c: (t, 0))
    shape = jax.ShapeDtypeStruct((R, C), F32)
    return pl.pallas_call(
        body, name=name,
        grid_spec=pltpu.PrefetchScalarGridSpec(
            num_scalar_prefetch=1, grid=(2, nt), in_specs=[full, half, half, full, full], out_specs=[full] * 4),
        out_shape=[shape] * 4,
        compiler_params=_params(2),
    )(cidx, w, mine, other, m, v)


def _adamw(w, g, m, v, *, name):
    R, C = w.shape
    tr = _row_tile(R, C)

    def body(w_ref, g_ref, m_ref, v_ref, d_ref, mo_ref, vo_ref):
        d_ref[...], mo_ref[...], vo_ref[...] = _adamw_math(w_ref[...], g_ref[...], m_ref[...], v_ref[...])

    blk = pl.BlockSpec((tr, C), lambda t: (t, 0))
    shape = jax.ShapeDtypeStruct((R, C), F32)
    return pl.pallas_call(
        body, name=name, grid=(R // tr,),
        in_specs=[blk] * 4, out_specs=[blk] * 3, out_shape=[shape] * 3,
        compiler_params=_params(1),
    )(w, g, m, v)


def _unstack_cols(w):
    s, r, c = w.shape
    return w.transpose(1, 0, 2).reshape(r, s * c)


def _pad_rows(a, rows):
    return jnp.pad(a, ((0, rows - a.shape[0]), (0, LANES - a.shape[1])))


BIG = ("ffn1_w_gate", "ffn1_w_up", "ffn1_w_down", "w_in", "w_branch_a", "w_branch_b", "w_out",
       "ffn2_w_gate", "ffn2_w_up", "ffn2_w_down")
TRANSPOSED = ("ffn1_w_gate", "ffn1_w_up", "w_in", "ffn2_w_gate", "ffn2_w_up")
WEIGHTS = ("ffn1_norm", "ffn1_w_gate", "ffn1_w_up", "ffn1_w_down", "mix_norm", "w_in", "na_rpb", "sink_logit",
           "w_branch_a", "w_branch_b", "w_out", "ffn2_norm", "ffn2_w_gate", "ffn2_w_up", "ffn2_w_down", "final_norm")


def kernel(x, ffn1_norm, ffn1_w_gate, ffn1_w_up, ffn1_w_down, mix_norm, w_in, na_rpb, sink_logit, w_branch_a, w_branch_b, w_out, ffn2_norm, ffn2_w_gate, ffn2_w_up, ffn2_w_down, final_norm, loss_target, m_ffn1_norm, m_ffn1_w_gate, m_ffn1_w_up, m_ffn1_w_down, m_mix_norm, m_w_in, m_na_rpb, m_sink_logit, m_w_branch_a, m_w_branch_b, m_w_out, m_ffn2_norm, m_ffn2_w_gate, m_ffn2_w_up, m_ffn2_w_down, m_final_norm, v_ffn1_norm, v_ffn1_w_gate, v_ffn1_w_up, v_ffn1_w_down, v_mix_norm, v_w_in, v_na_rpb, v_sink_logit, v_w_branch_a, v_w_branch_b, v_w_out, v_ffn2_norm, v_ffn2_w_gate, v_ffn2_w_up, v_ffn2_w_down, v_final_norm):
    args = dict(locals())
    w = {k: args[k] for k in WEIGHTS}
    mom = {k: args["m_" + k] for k in WEIGHTS}
    var = {k: args["v_" + k] for k in WEIGHTS}
    cidx = lax.axis_index("c").astype(jnp.int32).reshape(1)
    chip = (2 * lax.axis_index("x") + lax.axis_index("y")).astype(jnp.int32).reshape(1)

    def shard(a, k):
        return jnp.swapaxes(a[0], 0, 1) if k in TRANSPOSED else a[0]

    def unshard(a, k):
        return (jnp.swapaxes(a, 0, 1) if k in TRANSPOSED else a)[None]

    full = dict(zip(BIG, _all_gather([shard(w[k], k).astype(BF16) for k in BIG])))
    f1 = (full["ffn1_w_gate"], full["ffn1_w_up"], full["ffn1_w_down"])
    f2 = (full["ffn2_w_gate"], full["ffn2_w_up"], full["ffn2_w_down"])
    out = _layer_grads(
        x[0], loss_target[0], ffn1_norm, f1, mix_norm, full["w_in"].reshape(D_IN, D_MODEL), na_rpb[0], sink_logit[0],
        _unstack_cols(full["w_branch_a"]), _unstack_cols(full["w_branch_b"]), full["w_out"].reshape(D_MODEL, D_MODEL),
        ffn2_norm, f2, final_norm.reshape(1, D_MODEL))

    by_chip = dict(zip(BIG, [out["f1"][0], out["f1"][1], out["f1"][2], out["w_in"].reshape(N_CHIPS, D_IN // N_CHIPS, D_MODEL),
                             out["wa"], out["wb"], out["wout"].reshape(N_CHIPS, D_MODEL // N_CHIPS, D_MODEL),
                             out["f2"][0], out["f2"][1], out["f2"][2]]))
    grads = [by_chip[k] for k in BIG]
    from_sibling = _rs_sibling(grads)
    parts = [_add_sibling(g, r, cidx, name="add_sibling_" + k) for k, g, r in zip(BIG, grads, from_sibling)]
    from_chips = _rs_chips([p16 for _, p16 in parts])
    halves = [_add_chips(p, r, chip, name="add_chips_" + k) for k, (p, _), r in zip(BIG, parts, from_chips)]
    mine = dict(zip(BIG, halves))
    other = dict(zip(BIG, _rs_share(halves)))
    grad = {}

    rows = D_MODEL // LANES
    small = jnp.concatenate([
        out["g1"].reshape(rows, LANES), out["gmix"].reshape(rows, LANES), out["g2"].reshape(rows, LANES),
        out["gfin"].reshape(rows, LANES), out["rpb"].reshape(-1, LANES),
        _pad_rows(out["sink"].reshape(1, NB_HEADS), 8), _pad_rows(out["loss"], 8)], axis=0)
    total = _small_allreduce(small)
    n_rpb = NA_HEADS * 2 * NA_KH
    grad["ffn1_norm"] = total[0:rows].reshape(1, D_MODEL)
    grad["mix_norm"] = total[rows:2 * rows].reshape(1, D_MODEL)
    grad["ffn2_norm"] = total[2 * rows:3 * rows].reshape(1, D_MODEL)
    grad["final_norm"] = total[3 * rows:4 * rows].reshape(1, D_MODEL)
    grad["na_rpb"] = total[4 * rows:4 * rows + n_rpb].reshape(NA_HEADS, 2 * NA_KH, LANES)[:, :2 * NA_KH - 1, :2 * NA_KW - 1]
    grad["na_rpb"] = grad["na_rpb"].reshape(NA_HEADS, -1)
    grad["sink_logit"] = total[4 * rows + n_rpb:4 * rows + n_rpb + 1, 0:NB_HEADS]
    loss = total[4 * rows + n_rpb + 8, 0]

    deltas, new_m, new_v, grads_out = {}, {}, {}, {}
    for k in WEIGHTS:
        shape = w[k].shape
        if k in mine:
            res = _adamw_halves(shard(w[k], k), mine[k], other[k], shard(mom[k], k), shard(var[k], k), cidx, name="adamw_" + k)
            grads_out[k], deltas[k], new_m[k], new_v[k] = (unshard(a, k) for a in res)
        else:
            g2d = grad[k]
            d, mn, vn = _adamw(w[k].reshape(g2d.shape), g2d, mom[k].reshape(g2d.shape), var[k].reshape(g2d.shape), name="adamw_" + k)
            grads_out[k], deltas[k], new_m[k], new_v[k] = (a.reshape(shape) for a in (g2d, d, mn, vn))
    return (loss, out["dx"].reshape(x.shape), *[grads_out[k] for k in WEIGHTS], *[deltas[k] for k in WEIGHTS],
            *[new_m[k] for k in WEIGHTS], *[new_v[k] for k in WEIGHTS])
```

```python
import functools
import math

import jax
import jax.numpy as jnp
from jax import lax
from jax.experimental import pallas as pl
from jax.experimental.pallas import tpu as pltpu

F32 = jnp.float32
BF16 = jnp.bfloat16

D_MODEL = 1024
HEAD_DIM = 64
NA_HEADS = 8
NB_HEADS = 8
GRID_W = 64
NA_KH = 8
NA_KW = 16
WIN = 128
ROPE_THETA = 10000.0
EPS = 1e-6
N_CHIPS = 4
QK_SCALE = HEAD_DIM ** -0.5
NEG = -1e30
LANES = 128
VMEM_LIMIT = 56 * 1024 * 1024

C_QKVA = 3 * NA_HEADS * HEAD_DIM
C_QB = NB_HEADS * HEAD_DIM
C_KB = 2 * HEAD_DIM
C_ROPE = C_QB + C_KB
C_GATES = 2 * D_MODEL
D_IN = C_QKVA + C_QB + 2 * C_KB + C_GATES
O_QB = C_QKVA
O_KB = O_QB + C_QB
O_VB = O_KB + C_KB
O_G = O_VB + C_KB

ADAM_LR = 0.001
ADAM_B1 = 0.9
ADAM_B2 = 0.999
ADAM_EPS = 1e-08
ADAM_WD = 0.01
ADAM_STEP = 10

MESH = pl.DeviceIdType.MESH


def _dot(a, b):
    return jnp.dot(a, b, preferred_element_type=F32)


def _dot_nt(a, b):
    return lax.dot_general(a, b, (((1,), (1,)), ((), ())), preferred_element_type=F32)


def _dot_tn(a, b):
    return lax.dot_general(a, b, (((0,), (0,)), ((), ())), preferred_element_type=F32)


def _params(n_axes):
    return pltpu.CompilerParams(dimension_semantics=("arbitrary",) * n_axes, vmem_limit_bytes=VMEM_LIMIT)


def _rstd(xf):
    return lax.rsqrt(jnp.mean(xf * xf, axis=-1, keepdims=True) + EPS)


def _norm_bwd(dn, xf, g, r):
    xhat = xf * r
    dxh = dn * g
    dx = r * (dxh - xhat * jnp.mean(dxh * xhat, axis=-1, keepdims=True))
    return dx, dn * xhat


def _sigmoid(x):
    return 0.5 * jnp.tanh(0.5 * x) + 0.5


def _ffn_fwd(x, g, wg, wu, wd, *, name, tm=1024, sub=512):
    T, D = x.shape
    F = wg.shape[1]

    def body(x_ref, g_ref, wg_ref, wu_ref, wd_ref, h_ref, n_ref, hdn_ref, p_ref, q_ref):
        s = pl.program_id(1)

        @pl.when(s == 0)
        def _():
            xf = x_ref[...]
            n_ref[...] = ((xf * _rstd(xf)) * g_ref[...]).astype(BF16)
            h_ref[...] = xf

        rows = [pl.ds(u * sub, sub) for u in range(tm // sub)]
        ab = [(_dot_nt(n_ref[r, :], wg_ref[0]), _dot_nt(n_ref[r, :], wu_ref[0])) for r in rows]
        hdns = []
        for r, (a, b) in zip(rows, ab):
            sg = _sigmoid(a)
            silu = a * sg
            hdn = (silu * b).astype(BF16)
            hdn_ref[0, r, :] = hdn
            p_ref[0, r, :] = (b * (sg + silu * (1.0 - sg))).astype(BF16)
            q_ref[0, r, :] = silu.astype(BF16)
            hdns.append(hdn)
        for r, hdn in zip(rows, hdns):
            h_ref[r, :] += 0.5 * _dot(hdn, wd_ref[0])

    tok = pl.BlockSpec((tm, D), lambda i, s: (i, 0))
    hid = pl.BlockSpec((1, tm, F), lambda i, s: (s, i, 0))
    wspec = pl.BlockSpec((1, F, D), lambda i, s: (s, 0, 0))
    hshape = jax.ShapeDtypeStruct((N_CHIPS, T, F), BF16)
    return pl.pallas_call(
        body, name=name, grid=(T // tm, N_CHIPS),
        in_specs=[tok, pl.BlockSpec((1, D), lambda i, s: (0, 0)), wspec, wspec, wspec],
        out_specs=[tok, tok, hid, hid, hid],
        out_shape=[jax.ShapeDtypeStruct((T, D), F32), jax.ShapeDtypeStruct((T, D), BF16), hshape, hshape, hshape],
        compiler_params=_params(2),
    )(x, g, wg, wu, wd)


def _ffn_bwd(dh, x, g, p, q, wg, wu, wd, *, name, tm=512, sub=256):
    T, D = x.shape
    F = wg.shape[1]

    def body(dh_ref, x_ref, g_ref, p_ref, q_ref, wg_ref, wu_ref, wd_ref, dx_ref, da_ref, db_ref, dg_ref):
        i, s = pl.program_id(0), pl.program_id(1)

        @pl.when((i == 0) & (s == 0))
        def _():
            dg_ref[...] = jnp.zeros_like(dg_ref)

        @pl.when(s == 0)
        def _():
            dx_ref[...] = jnp.zeros_like(dx_ref)

        rows = [pl.ds(u * sub, sub) for u in range(tm // sub)]
        dhdn = [_dot_nt((0.5 * dh_ref[r, :]).astype(BF16), wd_ref[0]) for r in rows]
        das, dbs = [], []
        for r, dd in zip(rows, dhdn):
            da = (dd * p_ref[0, r, :].astype(F32)).astype(BF16)
            db = (dd * q_ref[0, r, :].astype(F32)).astype(BF16)
            da_ref[0, r, :] = da
            db_ref[0, r, :] = db
            das.append(da)
            dbs.append(db)
        for r, da, db in zip(rows, das, dbs):
            dx_ref[r, :] += _dot(da, wg_ref[0]) + _dot(db, wu_ref[0])

        @pl.when(s == N_CHIPS - 1)
        def _():
            xf = x_ref[...]
            dx, dgr = _norm_bwd(dx_ref[...], xf, g_ref[...], _rstd(xf))
            dg_ref[...] += jnp.sum(dgr, axis=0, keepdims=True)
            dx_ref[...] = dh_ref[...] + dx

    tok = pl.BlockSpec((tm, D), lambda i, s: (i, 0))
    hid = pl.BlockSpec((1, tm, F), lambda i, s: (s, i, 0))
    vec = pl.BlockSpec((1, D), lambda i, s: (0, 0))
    hshape = jax.ShapeDtypeStruct((N_CHIPS, T, F), BF16)
    return pl.pallas_call(
        body, name=name, grid=(T // tm, N_CHIPS),
        in_specs=[tok, tok, vec, hid, hid,
                  pl.BlockSpec((1, F, D), lambda i, s: (s, 0, 0)), pl.BlockSpec((1, F, D), lambda i, s: (s, 0, 0)),
                  pl.BlockSpec((1, F, D), lambda i, s: (s, 0, 0))],
        out_specs=[tok, hid, hid, vec],
        out_shape=[jax.ShapeDtypeStruct((T, D), F32), hshape, hshape, jax.ShapeDtypeStruct((1, D), F32)],
        compiler_params=_params(2),
    )(dh, x, g, p, q, wg, wu, wd)


def _wgrad(a, b, *, a_block, a_map, b_block, b_map, out_shape, o_block, o_map, grid, scale=1.0, name):
    def body(a_ref, b_ref, o_ref):
        @pl.when(pl.program_id(len(grid) - 1) == 0)
        def _():
            o_ref[...] = jnp.zeros_like(o_ref)

        av = a_ref[...]
        bv = b_ref[...]
        av = av.reshape(av.shape[-2:]).astype(BF16)
        bv = bv.reshape(bv.shape[-2:])
        if scale != 1.0:
            bv = scale * bv
        o_ref[...] += _dot_tn(av, bv.astype(BF16)).reshape(o_ref.shape)

    return pl.pallas_call(
        body, name=name, grid=grid,
        in_specs=[pl.BlockSpec(a_block, a_map), pl.BlockSpec(b_block, b_map)],
        out_specs=pl.BlockSpec(o_block, o_map),
        out_shape=jax.ShapeDtypeStruct(out_shape, F32),
        compiler_params=_params(len(grid)),
    )(a, b)


def _wgrad_rows(a, b, n_blocks, *, name, tk=1024):
    T, N = b.shape
    M = a.shape[1] // n_blocks
    tk = min(tk, T)
    return _wgrad(a, b, a_block=(tk, M), a_map=lambda s, k: (k, s), b_block=(tk, N), b_map=lambda s, k: (k, 0),
                  out_shape=(n_blocks, M, N), o_block=(1, M, N), o_map=lambda s, k: (s, 0, 0), grid=(n_blocks, T // tk), name=name)


def _wgrad_shard_a(a, b, *, name, scale=1.0, tk=2048):
    S, T, M = a.shape
    N = b.shape[1]
    tk = min(tk, T)
    return _wgrad(a, b, a_block=(1, tk, M), a_map=lambda s, k: (s, k, 0), b_block=(tk, N), b_map=lambda s, k: (k, 0),
                  out_shape=(S, M, N), o_block=(1, M, N), o_map=lambda s, k: (s, 0, 0), grid=(S, T // tk), scale=scale, name=name)


def _wgrad_cols(a, b, n_blocks, *, name, tk=1024):
    T, M = a.shape
    N = b.shape[1] // n_blocks
    tk = min(tk, T)

    def body(a_ref, b_ref, o_ref):
        @pl.when(pl.program_id(0) == 0)
        def _():
            o_ref[...] = jnp.zeros_like(o_ref)

        r = _dot_tn(a_ref[...].astype(BF16), b_ref[...].astype(BF16))
        for s in range(n_blocks):
            o_ref[s] += r[:, s * N:(s + 1) * N]

    return pl.pallas_call(
        body, name=name, grid=(T // tk,),
        in_specs=[pl.BlockSpec((tk, M), lambda k: (k, 0)), pl.BlockSpec((tk, n_blocks * N), lambda k: (k, 0))],
        out_specs=pl.BlockSpec((n_blocks, M, N), lambda k: (0, 0, 0)),
        out_shape=jax.ShapeDtypeStruct((n_blocks, M, N), F32),
        compiler_params=_params(1),
    )(a, b)


def _rope_tables(T):
    half = HEAD_DIM // 2
    inv = ROPE_THETA ** (-jnp.arange(half, dtype=F32) / half)
    ang = jnp.arange(T, dtype=F32)[:, None] * inv[None, :]
    cos, sin, zero = jnp.cos(ang), jnp.sin(ang), jnp.zeros_like(ang)
    reps = LANES // HEAD_DIM
    return (jnp.tile(jnp.concatenate([cos, cos], axis=1), (1, reps)),
            jnp.tile(jnp.concatenate([-sin, zero], axis=1), (1, reps)),
            jnp.tile(jnp.concatenate([zero, sin], axis=1), (1, reps)))


def _rope(x, cos, sa, sb, sign):
    half = HEAD_DIM // 2
    return x * cos + sign * (pltpu.roll(x, LANES - half, 1) * sa + pltpu.roll(x, half, 1) * sb)


def _mix_in_fwd(h, g, w_in, tables, *, tm=256):
    T, D = h.shape

    def body(h_ref, g_ref, w_ref, cos_ref, sa_ref, sb_ref, u_ref, qkva_ref, qb_ref, kvb_ref, gates_ref):
        hf = h_ref[...]
        u = ((hf * _rstd(hf)) * g_ref[...]).astype(BF16)
        u_ref[...] = u
        qkva_ref[...] = _dot_nt(u, w_ref[0:C_QKVA, :]).astype(BF16)
        zr = _dot_nt(u, w_ref[O_QB:O_QB + C_ROPE, :])
        cos, sa, sb = cos_ref[...], sa_ref[...], sb_ref[...]
        for j in range(C_ROPE // LANES):
            rj = _rope(zr[:, j * LANES:(j + 1) * LANES], cos, sa, sb, 1.0).astype(BF16)
            if j < C_QB // LANES:
                qb_ref[:, j * LANES:(j + 1) * LANES] = rj
            else:
                kvb_ref[:, 0:C_KB] = rj
        kvb_ref[:, C_KB:2 * C_KB] = _dot_nt(u, w_ref[O_VB:O_VB + C_KB, :]).astype(BF16)
        gates_ref[...] = _dot_nt(u, w_ref[O_G:O_G + C_GATES, :])

    def tok(n):
        return pl.BlockSpec((tm, n), lambda i: (i, 0))

    return pl.pallas_call(
        body, name="mix_in_fwd", grid=(T // tm,),
        in_specs=[tok(D), pl.BlockSpec((1, D), lambda i: (0, 0)), pl.BlockSpec((D_IN, D), lambda i: (0, 0)),
                  tok(LANES), tok(LANES), tok(LANES)],
        out_specs=[tok(D), tok(C_QKVA), tok(C_QB), tok(2 * C_KB), tok(C_GATES)],
        out_shape=[jax.ShapeDtypeStruct((T, D), BF16), jax.ShapeDtypeStruct((T, C_QKVA), BF16),
                   jax.ShapeDtypeStruct((T, C_QB), BF16), jax.ShapeDtypeStruct((T, 2 * C_KB), BF16),
                   jax.ShapeDtypeStruct((T, C_GATES), F32)],
        compiler_params=_params(1),
    )(h, g, w_in, *tables)


def _mix_in_bwd(dqa, dka, dva, dqb, dkb, dvb, dgates, h, g, dres, w_in, tables, *, tm=256):
    T, D = h.shape

    def body(dqa_ref, dka_ref, dva_ref, dqb_ref, dkb_ref, dvb_ref, dgt_ref, h_ref, g_ref, dres_ref, w_ref,
             cos_ref, sa_ref, sb_ref, dz_ref, dh_ref, dg_ref):
        @pl.when(pl.program_id(0) == 0)
        def _():
            dg_ref[...] = jnp.zeros_like(dg_ref)

        na = NA_HEADS * HEAD_DIM
        dz_ref[:, 0:na] = dqa_ref[...].astype(BF16)
        dz_ref[:, na:2 * na] = dka_ref[...].astype(BF16)
        dz_ref[:, 2 * na:3 * na] = dva_ref[...].astype(BF16)
        cos, sa, sb = cos_ref[...], sa_ref[...], sb_ref[...]
        for j in range(C_QB // LANES):
            dz_ref[:, O_QB + j * LANES:O_QB + (j + 1) * LANES] = _rope(
                dqb_ref[:, j * LANES:(j + 1) * LANES], cos, sa, sb, -1.0).astype(BF16)
        dz_ref[:, O_KB:O_KB + C_KB] = _rope(dkb_ref[...], cos, sa, sb, -1.0).astype(BF16)
        dz_ref[:, O_VB:O_VB + C_KB] = dvb_ref[...].astype(BF16)
        dz_ref[:, O_G:O_G + C_GATES] = dgt_ref[...].astype(BF16)
        du = _dot(dz_ref[...], w_ref[...])
        hf = h_ref[...]
        dx, dgr = _norm_bwd(du, hf, g_ref[...], _rstd(hf))
        dg_ref[...] += jnp.sum(dgr, axis=0, keepdims=True)
        dh_ref[...] = dres_ref[...] + dx

    def tok(n):
        return pl.BlockSpec((tm, n), lambda i: (i, 0))

    vec = pl.BlockSpec((1, D), lambda i: (0, 0))
    na = NA_HEADS * HEAD_DIM
    return pl.pallas_call(
        body, name="mix_in_bwd", grid=(T // tm,),
        in_specs=[tok(na), tok(na), tok(na), tok(C_QB), tok(C_KB), tok(C_KB), tok(C_GATES), tok(D), vec, tok(D),
                  pl.BlockSpec((D_IN, D), lambda i: (0, 0)), tok(LANES), tok(LANES), tok(LANES)],
        out_specs=[tok(D_IN), tok(D), vec],
        out_shape=[jax.ShapeDtypeStruct((T, D_IN), BF16), jax.ShapeDtypeStruct((T, D), F32),
                   jax.ShapeDtypeStruct((1, D), F32)],
        compiler_params=_params(1),
    )(dqa, dka, dva, dqb, dkb, dvb, dgates, h, g, dres, w_in, *tables)


def _na_bias_slabs(rpb):
    H = rpb.shape[0]
    ncell = GRID_W * GRID_W
    cell = jnp.arange(ncell)
    co = cell % GRID_W - cell // GRID_W + (NA_KW - 1)
    e_co = (jnp.arange(LANES)[:, None] == co[None, :]).astype(F32)
    table = jnp.pad(rpb, ((0, 0), (0, 1), (0, LANES - rpb.shape[2]))).reshape(H * 2 * NA_KH, LANES)

    def body(t_ref, e_ref, o_ref):
        o_ref[...] = jnp.dot(t_ref[...], e_ref[...], preferred_element_type=F32, precision=lax.Precision.HIGHEST)

    toeplitz = pl.pallas_call(
        body, name="rpb_unfold", out_shape=jax.ShapeDtypeStruct((H * 2 * NA_KH, ncell), F32),
        compiler_params=_params(0),
    )(table, e_co).reshape(H, 2 * NA_KH, GRID_W, GRID_W)
    c = jnp.arange(GRID_W)
    cs = jnp.clip(c - NA_KW // 2, 0, GRID_W - NA_KW)
    inwin = (c[None, :] >= cs[:, None]) & (c[None, :] < cs[:, None] + NA_KW)
    toeplitz = jnp.where(inwin[None, None], toeplitz, NEG)
    slabs = jnp.stack([toeplitz[:, r:r + NA_KH] for r in range(NA_KH)], axis=1)
    slabs = slabs.reshape(H // 2, 2, NA_KH, NA_KH, GRID_W, GRID_W).transpose(0, 2, 1, 4, 3, 5)
    return slabs.reshape(H // 2, NA_KH, 2 * GRID_W, NA_KH * GRID_W)


def _na_unstack_slabs(dslab):
    pairs = dslab.shape[0]
    d = dslab.reshape(pairs, NA_KH, 2, GRID_W, NA_KH * GRID_W).transpose(0, 2, 1, 3, 4)
    return d.reshape(2 * pairs, NA_KH, GRID_W, NA_KH * GRID_W)


def _half_masks(rows):
    lane = lax.broadcasted_iota(jnp.int32, (rows, LANES), 1)
    left = lane < HEAD_DIM
    return left, (left, jnp.logical_not(left))


def _stack_heads(x):
    left, halves = _half_masks(x.shape[0])
    xf = x.astype(F32)
    return jnp.concatenate([jnp.where(m, xf, 0.0).astype(BF16) for m in halves], axis=0)


def _unstack_heads(o):
    rows = o.shape[0] // 2
    left, _ = _half_masks(rows)
    return jnp.where(left, o[:rows], o[rows:])


def _na_row(j, t, rb, rows):
    r = j * rb + t
    rs = jnp.clip(r - NA_KH // 2, 0, rows - NA_KH)
    return pl.multiple_of(t * GRID_W, GRID_W), pl.multiple_of(rs * GRID_W, GRID_W), rs - r + (NA_KH - 1)


def _na_specs(T, rb):
    qrows = GRID_W * rb
    pairs = NA_HEADS // 2
    return ([pl.BlockSpec((qrows, LANES), lambda p, j: (j, p)),
             pl.BlockSpec((T, LANES), lambda p, j: (0, pairs + p)),
             pl.BlockSpec((T, LANES), lambda p, j: (0, 2 * pairs + p))],
            pl.BlockSpec((1, NA_KH, 2 * GRID_W, NA_KH * GRID_W), lambda p, j: (p, 0, 0, 0)))


def _softmax(s):
    p = jnp.exp(s - jnp.max(s, axis=-1, keepdims=True))
    return p / jnp.sum(p, axis=-1, keepdims=True)


def _na_probs(qs, ks, bias):
    return _softmax(_dot_nt(qs, ks) * QK_SCALE + bias)


def _na_fwd(qkva, bias, *, rb=8, group=4):
    T = qkva.shape[0]
    rows = T // GRID_W
    nkeys = NA_KH * GRID_W

    def body(q_ref, k_ref, v_ref, bias_ref, y_ref):
        j = pl.program_id(1)

        def rows_step(t, carry):
            at = [_na_row(j, t * group + u, rb, rows) for u in range(group)]
            s = [_dot_nt(_stack_heads(q_ref[pl.ds(q0, GRID_W), :]), k_ref[pl.ds(k0, nkeys), :]) for q0, k0, _ in at]
            p = [_softmax(su * QK_SCALE + bias_ref[0, ro0]) for su, (_, _, ro0) in zip(s, at)]
            o = [_dot(pu.astype(BF16), v_ref[pl.ds(k0, nkeys), :]) for pu, (_, k0, _) in zip(p, at)]
            for ou, (q0, _, _) in zip(o, at):
                y_ref[pl.ds(q0, GRID_W), :] = _unstack_heads(ou).astype(BF16)
            return carry

        lax.fori_loop(0, rb // group, rows_step, 0)

    qkv_specs, bias_spec = _na_specs(T, rb)
    return pl.pallas_call(
        body, name="na_fwd", grid=(NA_HEADS // 2, rows // rb),
        in_specs=qkv_specs + [bias_spec],
        out_specs=qkv_specs[0],
        out_shape=jax.ShapeDtypeStruct((T, NA_HEADS * HEAD_DIM), BF16),
        compiler_params=_params(2),
    )(qkva, qkva, qkva, bias)


def _na_bwd(qkva, dy, bias, *, rb=8, group=4):
    T = qkva.shape[0]
    rows = T // GRID_W
    nkeys = NA_KH * GRID_W

    def body(q_ref, k_ref, v_ref, dy_ref, bias_ref, dq_ref, dk_ref, dv_ref, dbias_ref):
        j = pl.program_id(1)

        @pl.when(j == 0)
        def _():
            dk_ref[...] = jnp.zeros_like(dk_ref)
            dv_ref[...] = jnp.zeros_like(dv_ref)
            dbias_ref[...] = jnp.zeros_like(dbias_ref)

        def rows_step(t, carry):
            at = [_na_row(j, t * group + u, rb, rows) for u in range(group)]
            qs = [_stack_heads(q_ref[pl.ds(q0, GRID_W), :]) for q0, _, _ in at]
            dys = [_stack_heads(dy_ref[pl.ds(q0, GRID_W), :]) for q0, _, _ in at]
            s = [_dot_nt(qu, k_ref[pl.ds(k0, nkeys), :]) for qu, (_, k0, _) in zip(qs, at)]
            dp = [_dot_nt(du, v_ref[pl.ds(k0, nkeys), :]) for du, (_, k0, _) in zip(dys, at)]
            p = [_softmax(su * QK_SCALE + bias_ref[0, ro0]) for su, (_, _, ro0) in zip(s, at)]
            ds = [pu * (du - jnp.sum(pu * du, axis=-1, keepdims=True)) for pu, du in zip(p, dp)]
            for u, (q0, k0, ro0) in enumerate(at):
                dbias_ref[0, ro0] += ds[u]
                dsb = ds[u].astype(BF16)
                dq_ref[pl.ds(q0, GRID_W), :] = (_unstack_heads(_dot(dsb, k_ref[pl.ds(k0, nkeys), :])) * QK_SCALE).astype(BF16)
                dk_ref[pl.ds(k0, nkeys), :] += _dot_tn(dsb, qs[u]) * QK_SCALE
                dv_ref[pl.ds(k0, nkeys), :] += _dot_tn(p[u].astype(BF16), dys[u])
            return carry

        lax.fori_loop(0, rb // group, rows_step, 0)

    qkv_specs, bias_spec = _na_specs(T, rb)
    width = NA_HEADS * HEAD_DIM
    kv_out = pl.BlockSpec((T, LANES), lambda p, j: (0, p))
    return pl.pallas_call(
        body, name="na_bwd", grid=(NA_HEADS // 2, rows // rb),
        in_specs=qkv_specs + [qkv_specs[0], bias_spec],
        out_specs=[qkv_specs[0], kv_out, kv_out, bias_spec],
        out_shape=[jax.ShapeDtypeStruct((T, width), BF16), jax.ShapeDtypeStruct((T, width), F32),
                   jax.ShapeDtypeStruct((T, width), F32), jax.ShapeDtypeStruct(bias.shape, F32)],
        compiler_params=_params(2),
    )(qkva, qkva, qkva, dy, bias)


def _rpb_fold(dslab):
    H = dslab.shape[0]
    nro = NA_KH * NA_KH
    ncell = GRID_W * GRID_W
    xs = dslab.reshape(H, NA_KH, GRID_W, NA_KH, GRID_W).transpose(0, 1, 3, 2, 4).reshape(H, nro, ncell)
    cell = jnp.arange(ncell)
    co = cell % GRID_W - cell // GRID_W + (NA_KW - 1)
    e_co = (co[:, None] == jnp.arange(LANES)[None, :]).astype(F32)
    pair = jnp.arange(nro)
    e_ro = ((pair // NA_KH + pair % NA_KH)[None, :] == jnp.arange(2 * NA_KH)[:, None]).astype(F32)

    def body(x_ref, eco_ref, ero_ref, o_ref):
        y = jnp.dot(x_ref[0], eco_ref[...], preferred_element_type=F32, precision=lax.Precision.HIGHEST)
        o_ref[0] = jnp.dot(ero_ref[...], y, preferred_element_type=F32, precision=lax.Precision.HIGHEST)

    return pl.pallas_call(
        body, name="rpb_fold", grid=(H,),
        in_specs=[pl.BlockSpec((1, nro, ncell), lambda h: (h, 0, 0)), pl.BlockSpec((ncell, LANES), lambda h: (0, 0)),
                  pl.BlockSpec((2 * NA_KH, nro), lambda h: (0, 0))],
        out_specs=pl.BlockSpec((1, 2 * NA_KH, LANES), lambda h: (h, 0, 0)),
        out_shape=jax.ShapeDtypeStruct((H, 2 * NA_KH, LANES), F32),
        compiler_params=_params(1),
    )(xs, e_co, e_ro)


SWA_KEYS = 3 * WIN


def _swa_block(j, t, qbn, T):
    blk = j * qbn + t
    start = jnp.clip((blk - 1) * WIN, 0, T - SWA_KEYS)
    row = lax.broadcasted_iota(jnp.int32, (2 * WIN, SWA_KEYS), 0)
    qpos = blk * WIN + jnp.where(row < WIN, row, row - WIN)
    kpos = start + lax.broadcasted_iota(jnp.int32, (2 * WIN, SWA_KEYS), 1)
    return pl.multiple_of(t * WIN, WIN), pl.multiple_of(start, WIN), jnp.abs(qpos - kpos) <= WIN


def _swa_sinks(sink_ref, p):
    row = lax.broadcasted_iota(jnp.int32, (2 * WIN, 1), 0)
    return jnp.where(row < WIN, sink_ref[p], sink_ref[p + NB_HEADS // 2])


def _swa_probs(s, mask, sink):
    s = jnp.where(mask, s * QK_SCALE, NEG)
    m = jnp.maximum(jnp.max(s, axis=-1, keepdims=True), sink)
    e = jnp.exp(s - m)
    esink = jnp.exp(sink - m)
    den = jnp.sum(e, axis=-1, keepdims=True) + esink
    return e / den, esink / den


def _swa_specs(T, qbn):
    return [pl.BlockSpec(memory_space=pltpu.SMEM),
            pl.BlockSpec((WIN * qbn, LANES), lambda p, j: (j, p)),
            pl.BlockSpec((T, LANES), lambda p, j: (0, 0)),
            pl.BlockSpec((T, LANES), lambda p, j: (0, 1))]


def _swa_fwd(qb, kvb, sink, *, qbn=4, group=4):
    T = qb.shape[0]
    pairs = NB_HEADS // 2

    def body(sink_ref, q_ref, k_ref, v_ref, y_ref):
        p, j = pl.program_id(0), pl.program_id(1)
        sinks = _swa_sinks(sink_ref, p)

        def blocks_step(t, carry):
            at = [_swa_block(j, t * group + u, qbn, T) for u in range(group)]
            s = [_dot_nt(_stack_heads(q_ref[pl.ds(q0, WIN), :]), k_ref[pl.ds(k0, SWA_KEYS), :]) for q0, k0, _ in at]
            pr = [_swa_probs(su, mask, sinks)[0] for su, (_, _, mask) in zip(s, at)]
            o = [_dot(pu.astype(BF16), v_ref[pl.ds(k0, SWA_KEYS), :]) for pu, (_, k0, _) in zip(pr, at)]
            for ou, (q0, _, _) in zip(o, at):
                y_ref[pl.ds(q0, WIN), :] = _unstack_heads(ou).astype(BF16)
            return carry

        lax.fori_loop(0, qbn // group, blocks_step, 0)

    specs = _swa_specs(T, qbn)
    return pl.pallas_call(
        body, name="swa_fwd", grid=(pairs, T // (WIN * qbn)),
        in_specs=specs, out_specs=specs[1],
        out_shape=jax.ShapeDtypeStruct((T, NB_HEADS * HEAD_DIM), BF16),
        compiler_params=_params(2),
    )(sink, qb, kvb, kvb)


def _swa_bwd(qb, kvb, dy, sink, *, qbn=4, group=4):
    T = qb.shape[0]
    pairs = NB_HEADS // 2

    def body(sink_ref, q_ref, k_ref, v_ref, dy_ref, dq_ref, dk_ref, dv_ref, dsink_ref):
        p, j = pl.program_id(0), pl.program_id(1)
        sinks = _swa_sinks(sink_ref, p)

        @pl.when((p == 0) & (j == 0))
        def _():
            dk_ref[...] = jnp.zeros_like(dk_ref)
            dv_ref[...] = jnp.zeros_like(dv_ref)

        @pl.when(j == 0)
        def _():
            dsink_ref[...] = jnp.zeros_like(dsink_ref)

        def blocks_step(t, carry):
            at = [_swa_block(j, t * group + u, qbn, T) for u in range(group)]
            qs = [_stack_heads(q_ref[pl.ds(q0, WIN), :]) for q0, _, _ in at]
            dys = [_stack_heads(dy_ref[pl.ds(q0, WIN), :]) for q0, _, _ in at]
            s = [_dot_nt(qu, k_ref[pl.ds(k0, SWA_KEYS), :]) for qu, (_, k0, _) in zip(qs, at)]
            dp = [_dot_nt(du, v_ref[pl.ds(k0, SWA_KEYS), :]) for du, (_, k0, _) in zip(dys, at)]
            probs = [_swa_probs(su, mask, sinks) for su, (_, _, mask) in zip(s, at)]
            for u, (q0, k0, _) in enumerate(at):
                pr, psink = probs[u]
                delta = jnp.sum(pr * dp[u], axis=-1, keepdims=True)
                dsb = (pr * (dp[u] - delta)).astype(BF16)
                dsk = psink * delta
                for hh in range(2):
                    dsink_ref[0, hh:hh + 1, :] += jnp.broadcast_to(-jnp.sum(dsk[hh * WIN:(hh + 1) * WIN]), (1, LANES))
                dq_ref[pl.ds(q0, WIN), :] = _unstack_heads(_dot(dsb, k_ref[pl.ds(k0, SWA_KEYS), :])) * QK_SCALE
                dk_ref[pl.ds(k0, SWA_KEYS), :] += _dot_tn(dsb, qs[u]) * QK_SCALE
                dv_ref[pl.ds(k0, SWA_KEYS), :] += _dot_tn(pr.astype(BF16), dys[u])
            return carry

        lax.fori_loop(0, qbn // group, blocks_step, 0)

    specs = _swa_specs(T, qbn)
    kv_out = pl.BlockSpec((T, LANES), lambda p, j: (0, 0))
    return pl.pallas_call(
        body, name="swa_bwd", grid=(pairs, T // (WIN * qbn)),
        in_specs=specs + [specs[1]],
        out_specs=[specs[1], kv_out, kv_out, pl.BlockSpec((1, 8, LANES), lambda p, j: (p, 0, 0))],
        out_shape=[jax.ShapeDtypeStruct((T, NB_HEADS * HEAD_DIM), F32), jax.ShapeDtypeStruct((T, LANES), F32),
                   jax.ShapeDtypeStruct((T, LANES), F32), jax.ShapeDtypeStruct((pairs, 8, LANES), F32)],
        compiler_params=_params(2),
    )(sink, qb, kvb, kvb, dy)


def _merge_fwd(ya, yb, gates, wa, wb, wout, h, *, tm=512):
    T, D = h.shape
    W = ya.shape[1]

    def body(ya_ref, yb_ref, gt_ref, wa_ref, wb_ref, wo_ref, h_ref, h2_ref, mg_ref):
        pa = _dot(ya_ref[...], wa_ref[...])
        pb = _dot(yb_ref[...], wb_ref[...])
        mg = (jax.nn.sigmoid(gt_ref[:, 0:D]) * pa + jax.nn.sigmoid(gt_ref[:, D:2 * D]) * pb).astype(BF16)
        mg_ref[...] = mg
        h2_ref[...] = h_ref[...] + _dot(mg, wo_ref[...])

    def tok(n):
        return pl.BlockSpec((tm, n), lambda i: (i, 0))

    def full(r, c):
        return pl.BlockSpec((r, c), lambda i: (0, 0))

    return pl.pallas_call(
        body, name="merge_fwd", grid=(T // tm,),
        in_specs=[tok(W), tok(W), tok(2 * D), full(W, D), full(W, D), full(D, D), tok(D)],
        out_specs=[tok(D), tok(D)],
        out_shape=[jax.ShapeDtypeStruct((T, D), F32), jax.ShapeDtypeStruct((T, D), BF16)],
        compiler_params=_params(1),
    )(ya, yb, gates, wa, wb, wout, h)


def _merge_bwd(dh, ya, yb, gates, wa, wb, wout, *, tm=512):
    T, D = dh.shape
    W = ya.shape[1]

    def body(dh_ref, ya_ref, yb_ref, gt_ref, wa_ref, wb_ref, wo_ref, dya_ref, dyb_ref, dpa_ref, dpb_ref, dgt_ref):
        dmg = _dot_nt(dh_ref[...].astype(BF16), wo_ref[...])
        for y_ref, w_ref, dy_ref, dp_ref, lo in ((ya_ref, wa_ref, dya_ref, dpa_ref, 0), (yb_ref, wb_ref, dyb_ref, dpb_ref, D)):
            sg = jax.nn.sigmoid(gt_ref[:, lo:lo + D])
            dp = (dmg * sg).astype(BF16)
            dp_ref[...] = dp
            dgt_ref[:, lo:lo + D] = (dmg * _dot(y_ref[...], w_ref[...]) * (sg * (1.0 - sg))).astype(BF16)
            dy_ref[...] = _dot_nt(dp, w_ref[...]).astype(BF16)

    def tok(n):
        return pl.BlockSpec((tm, n), lambda i: (i, 0))

    def full(r, c):
        return pl.BlockSpec((r, c), lambda i: (0, 0))

    return pl.pallas_call(
        body, name="merge_bwd", grid=(T // tm,),
        in_specs=[tok(D), tok(W), tok(W), tok(2 * D), full(W, D), full(W, D), full(D, D)],
        out_specs=[tok(W), tok(W), tok(D), tok(D), tok(2 * D)],
        out_shape=[jax.ShapeDtypeStruct((T, W), BF16), jax.ShapeDtypeStruct((T, W), BF16),
                   jax.ShapeDtypeStruct((T, D), BF16), jax.ShapeDtypeStruct((T, D), BF16),
                   jax.ShapeDtypeStruct((T, 2 * D), BF16)],
        compiler_params=_params(1),
    )(dh, ya, yb, gates, wa, wb, wout)


def _final_loss(h, g, target, *, tm=512):
    T, D = h.shape

    def body(h_ref, g_ref, t_ref, dh_ref, loss_ref, dg_ref):
        @pl.when(pl.program_id(0) == 0)
        def _():
            loss_ref[...] = jnp.zeros_like(loss_ref)
            dg_ref[...] = jnp.zeros_like(dg_ref)

        hf = h_ref[...]
        r = _rstd(hf)
        gv = g_ref[...]
        err = (hf * r) * gv - t_ref[...]
        loss_ref[...] += jnp.broadcast_to(0.5 * jnp.sum(jnp.mean(err * err, axis=-1, keepdims=True)), loss_ref.shape)
        dx, dgr = _norm_bwd(err * (1.0 / D), hf, gv, r)
        dg_ref[...] += jnp.sum(dgr, axis=0, keepdims=True)
        dh_ref[...] = dx

    tok = pl.BlockSpec((tm, D), lambda i: (i, 0))
    vec = pl.BlockSpec((1, D), lambda i: (0, 0))
    return pl.pallas_call(
        body, name="final_loss", grid=(T // tm,),
        in_specs=[tok, vec, tok],
        out_specs=[tok, pl.BlockSpec((1, LANES), lambda i: (0, 0)), vec],
        out_shape=[jax.ShapeDtypeStruct((T, D), F32), jax.ShapeDtypeStruct((1, LANES), F32),
                   jax.ShapeDtypeStruct((1, D), F32)],
        compiler_params=_params(1),
    )(h, g, target)


def _pair_heads(a, axis):
    shp = a.shape
    a = a.reshape(shp[:axis] + (2, NB_HEADS // 2, HEAD_DIM) + shp[axis + 1:])
    return jnp.swapaxes(a, axis, axis + 1).reshape(shp)


def _unpair_heads(a, axis):
    shp = a.shape
    a = a.reshape(shp[:axis] + (NB_HEADS // 2, 2, HEAD_DIM) + shp[axis + 1:])
    return jnp.swapaxes(a, axis, axis + 1).reshape(shp)


def _layer_grads(x, target, g1, f1, gmix, w_in_t, rpb, sink, wa, wb, wout, g2, f2, gfin):
    T = x.shape[0]
    tables = _rope_tables(T)
    w_in_p = jnp.concatenate([w_in_t[:O_QB], _pair_heads(w_in_t[O_QB:O_KB], 0), w_in_t[O_KB:]], axis=0)
    wb_p = _pair_heads(wb, 0)
    bias = _na_bias_slabs(rpb)

    h1, n1, hdn1, p1, q1 = _ffn_fwd(x, g1, *f1, name="ffn1_fwd")
    u, qkva, qb, kvb, gates = _mix_in_fwd(h1, gmix, w_in_p, tables)
    ya = _na_fwd(qkva, bias)
    yb = _swa_fwd(qb, kvb, sink)
    h2, merged = _merge_fwd(ya, yb, gates, wa, wb_p, wout, h1)
    h3, n2, hdn2, p2, q2 = _ffn_fwd(h2, g2, *f2, name="ffn2_fwd")
    dh3, loss, dgfin = _final_loss(h3, gfin, target)

    dh2, da2, db2, dg2 = _ffn_bwd(dh3, h2, g2, p2, q2, *f2, name="ffn2_bwd")
    df2 = (_wgrad_shard_a(da2, n2, name="ffn2_dwg"), _wgrad_shard_a(db2, n2, name="ffn2_dwu"),
           _wgrad_shard_a(hdn2, dh3, scale=0.5, name="ffn2_dwd"))
    dya, dyb, dpa, dpb, dgates = _merge_bwd(dh2, ya, yb, gates, wa, wb_p, wout)
    dwout = _wgrad_cols(merged, dh2, 1, name="dwout")[0]
    dwa = _wgrad_cols(ya, dpa, N_CHIPS, name="dwa")
    dwb = _unpair_heads(_wgrad_cols(yb, dpb, N_CHIPS, name="dwb"), 1)
    dqa, dka, dva, dbias = _na_bwd(qkva, dya, bias)
    drpb = _rpb_fold(_na_unstack_slabs(dbias))
    dqb, dkb, dvb, dsink = _swa_bwd(qb, kvb, dyb, sink)
    dz, dh1, dgmix = _mix_in_bwd(dqa, dka, dva, dqb, dkb, dvb, dgates, h1, gmix, dh2, w_in_p, tables)
    dwin_p = _wgrad_rows(dz, u, 2, name="dwin").reshape(D_IN, D_MODEL)
    dwin = jnp.concatenate([dwin_p[:O_QB], _unpair_heads(dwin_p[O_QB:O_KB], 0), dwin_p[O_KB:]], axis=0)
    dx, da1, db1, dg1 = _ffn_bwd(dh1, x, g1, p1, q1, *f1, name="ffn1_bwd")
    df1 = (_wgrad_shard_a(da1, n1, name="ffn1_dwg"), _wgrad_shard_a(db1, n1, name="ffn1_dwu"),
           _wgrad_shard_a(hdn1, dh1, scale=0.5, name="ffn1_dwd"))
    dsink_v = dsink[:, 0:2, 0].T.reshape(NB_HEADS)
    return dict(loss=loss, dx=dx, g1=dg1, f1=df1, gmix=dgmix, w_in=dwin, rpb=drpb, sink=dsink_v,
                wa=dwa, wb=dwb, wout=dwout, g2=dg2, f2=df2, gfin=dgfin)


ANY = pl.BlockSpec(memory_space=pl.ANY)


def _place():
    x, y, c = lax.axis_index("x"), lax.axis_index("y"), lax.axis_index("c")
    chips = [(1 - x, y), (x, 1 - y), (1 - x, 1 - y)]
    return x, y, c, 2 * x + y, chips


def _remote(src, dst, send_sems, recv_sems, k, device):
    return pltpu.make_async_remote_copy(src_ref=src, dst_ref=dst, send_sem=send_sems.at[k], recv_sem=recv_sems.at[k],
                                        device_id=device, device_id_type=MESH)


def _all_gather(shards):
    n = len(shards)

    def body(*refs):
        ins, outs = refs[:n], refs[n:2 * n]
        send_sems, recv_sems, own_send_sems, own_recv_sems = refs[2 * n:]
        x, y, c, mine, chips = _place()
        sibling = (x, y, 1 - c)
        own, sends, passes = [], [], []
        for i in range(n):
            hr = shards[i].shape[0] // 2
            cp = _remote(ins[i], outs[i].at[mine], own_send_sems, own_recv_sems, i, sibling)
            cp.start()
            own.append(cp)
            for j, (cx, cy) in enumerate(chips):
                cp = _remote(ins[i].at[pl.ds(c * hr, hr)], outs[i].at[mine, pl.ds(c * hr, hr)], send_sems, recv_sems,
                             6 * i + j, (cx, cy, c))
                cp.start()
                sends.append(cp)
        for i in range(n):
            hr = shards[i].shape[0] // 2
            for j, (cx, cy) in enumerate(chips):
                landed = outs[i].at[2 * cx + cy, pl.ds(c * hr, hr)]
                _remote(landed, landed, send_sems, recv_sems, 6 * i + j, (cx, cy, c)).wait_recv()
                cp = _remote(landed, landed, send_sems, recv_sems, 6 * i + 3 + j, sibling)
                cp.start()
                passes.append(cp)
        for i in range(n):
            hr = shards[i].shape[0] // 2
            for j, (cx, cy) in enumerate(chips):
                other = outs[i].at[2 * cx + cy, pl.ds((1 - c) * hr, hr)]
                _remote(other, other, send_sems, recv_sems, 6 * i + 3 + j, sibling).wait_recv()
        for cp in sends + passes:
            cp.wait_send()
        for cp in own:
            cp.wait()

    return pl.pallas_call(
        body, name="all_gather_weights",
        in_specs=[ANY] * n, out_specs=[ANY] * n,
        out_shape=[jax.ShapeDtypeStruct((N_CHIPS,) + s.shape, s.dtype) for s in shards],
        scratch_shapes=[pltpu.SemaphoreType.DMA((6 * n,)), pltpu.SemaphoreType.DMA((6 * n,)),
                        pltpu.SemaphoreType.DMA((n,)), pltpu.SemaphoreType.DMA((n,))],
    )(*shards)


def _rs_sibling(grads):
    n = len(grads)

    def body(*refs):
        ins, outs = refs[:n], refs[n:2 * n]
        send_sems, recv_sems = refs[2 * n:]
        x, y, c, _, _ = _place()
        copies = []
        for i in range(n):
            hr = grads[i].shape[1] // 2
            cp = _remote(ins[i].at[:, pl.ds((1 - c) * hr, hr)], outs[i], send_sems, recv_sems, i, (x, y, 1 - c))
            cp.start()
            copies.append(cp)
        for cp in copies:
            cp.wait()

    return pl.pallas_call(
        body, name="rs_sibling",
        in_specs=[ANY] * n, out_specs=[ANY] * n,
        out_shape=[jax.ShapeDtypeStruct((g.shape[0], g.shape[1] // 2, g.shape[2]), g.dtype) for g in grads],
        scratch_shapes=[pltpu.SemaphoreType.DMA((n,)), pltpu.SemaphoreType.DMA((n,))],
    )(*grads)


def _rs_chips(parts):
    n = len(parts)

    def body(*refs):
        ins, outs = refs[:n], refs[n:2 * n]
        send_sems, recv_sems = refs[2 * n:]
        _, _, c, _, chips = _place()
        copies = []
        for i in range(n):
            for j, (cx, cy) in enumerate(chips):
                cp = _remote(ins[i].at[2 * cx + cy], outs[i].at[j], send_sems, recv_sems, 3 * i + j, (cx, cy, c))
                cp.start()
                copies.append(cp)
        for cp in copies:
            cp.wait()

    return pl.pallas_call(
        body, name="rs_chips",
        in_specs=[ANY] * n, out_specs=[ANY] * n,
        out_shape=[jax.ShapeDtypeStruct((N_CHIPS - 1,) + p.shape[1:], p.dtype) for p in parts],
        scratch_shapes=[pltpu.SemaphoreType.DMA((3 * n,)), pltpu.SemaphoreType.DMA((3 * n,))],
    )(*parts)


def _rs_share(halves):
    n = len(halves)

    def body(*refs):
        ins, outs = refs[:n], refs[n:2 * n]
        send_sems, recv_sems = refs[2 * n:]
        x, y, c, _, _ = _place()
        copies = []
        for i in range(n):
            cp = _remote(ins[i], outs[i], send_sems, recv_sems, i, (x, y, 1 - c))
            cp.start()
            copies.append(cp)
        for cp in copies:
            cp.wait()

    return pl.pallas_call(
        body, name="rs_share",
        in_specs=[ANY] * n, out_specs=[ANY] * n,
        out_shape=[jax.ShapeDtypeStruct(h.shape, h.dtype) for h in halves],
        scratch_shapes=[pltpu.SemaphoreType.DMA((n,)), pltpu.SemaphoreType.DMA((n,))],
    )(*halves)


N_DEV = 8


def _small_allreduce(vec):
    R = vec.shape[0]

    def body(v_ref, o_ref, buf, send_sems, recv_sems):
        x, y, c, _, _ = _place()
        me = 4 * x + 2 * y + c
        buf[me] = v_ref[...]
        copies = []
        for k in range(1, N_DEV):
            peer = (x ^ (k >> 2), y ^ ((k >> 1) & 1), c ^ (k & 1))
            cp = _remote(v_ref, buf.at[me], send_sems, recv_sems, k - 1, peer)
            cp.start()
            copies.append(cp)
        for k, cp in enumerate(copies, start=1):
            cp.wait_send()
            landed = buf.at[me ^ k]
            _remote(landed, landed, send_sems, recv_sems, k - 1, (x, y, c)).wait_recv()
        acc = buf[0]
        for d in range(1, N_DEV):
            acc = acc + buf[d]
        o_ref[...] = acc

    return pl.pallas_call(
        body, name="small_allreduce",
        in_specs=[pl.BlockSpec(memory_space=pltpu.VMEM)], out_specs=pl.BlockSpec(memory_space=pltpu.VMEM),
        out_shape=jax.ShapeDtypeStruct(vec.shape, vec.dtype),
        scratch_shapes=[pltpu.VMEM((N_DEV, R, LANES), F32), pltpu.SemaphoreType.DMA((N_DEV - 1,)),
                        pltpu.SemaphoreType.DMA((N_DEV - 1,))],
    )(vec)


ELEMWISE_BLOCK = 256 * 1024


def _row_tile(rows, cols):
    best = None
    for t in range(8, rows + 1, 8):
        if rows % t == 0 and t * cols <= ELEMWISE_BLOCK:
            best = t
    return best if best is not None else rows


def _add_sibling(g, r1, cidx, *, name):
    S, R, C = g.shape
    hr = R // 2
    tr = _row_tile(hr, C)
    nt = hr // tr

    def body(c_ref, g_ref, r_ref, o_ref, o16_ref):
        p = g_ref[...] + r_ref[...]
        o_ref[...] = p
        o16_ref[...] = p.astype(BF16)

    blk = pl.BlockSpec((1, tr, C), lambda s, t, c: (s, t, 0))
    return pl.pallas_call(
        body, name=name,
        grid_spec=pltpu.PrefetchScalarGridSpec(
            num_scalar_prefetch=1, grid=(S, nt),
            in_specs=[pl.BlockSpec((1, tr, C), lambda s, t, c: (s, c[0] * nt + t, 0)), blk], out_specs=[blk, blk]),
        out_shape=[jax.ShapeDtypeStruct((S, hr, C), F32), jax.ShapeDtypeStruct((S, hr, C), BF16)],
        compiler_params=_params(2),
    )(cidx, g, r1)


def _add_chips(p, r2, chip, *, name):
    _, hr, C = p.shape
    tr = _row_tile(hr, C)

    def body(chip_ref, p_ref, r_ref, o_ref):
        o_ref[...] = ((p_ref[0] + r_ref[0].astype(F32)) + r_ref[1].astype(F32)) + r_ref[2].astype(F32)

    return pl.pallas_call(
        body, name=name,
        grid_spec=pltpu.PrefetchScalarGridSpec(
            num_scalar_prefetch=1, grid=(hr // tr,),
            in_specs=[pl.BlockSpec((1, tr, C), lambda t, s: (s[0], t, 0)), pl.BlockSpec((N_CHIPS - 1, tr, C), lambda t, s: (0, t, 0))],
            out_specs=pl.BlockSpec((tr, C), lambda t, s: (t, 0))),
        out_shape=jax.ShapeDtypeStruct((hr, C), F32),
        compiler_params=_params(1),
    )(chip, p, r2)


def _adamw_math(w, g, m, v):
    mn = ADAM_B1 * m + (1.0 - ADAM_B1) * g
    vn = ADAM_B2 * v + (1.0 - ADAM_B2) * (g * g)
    m_hat = mn / (1.0 - ADAM_B1 ** ADAM_STEP)
    v_hat = vn / (1.0 - ADAM_B2 ** ADAM_STEP)
    return -ADAM_LR * (m_hat / (jnp.sqrt(v_hat) + ADAM_EPS) + ADAM_WD * w), mn, vn


def _adamw_halves(w, mine, other, m, v, cidx, *, name):
    R, C = w.shape
    hr = R // 2
    tr = _row_tile(hr, C)
    nt = hr // tr

    def body(c_ref, w_ref, a_ref, b_ref, m_ref, v_ref, g_ref, d_ref, mo_ref, vo_ref):
        gv = jnp.where(pl.program_id(0) == c_ref[0], a_ref[...], b_ref[...])
        g_ref[...] = gv
        d_ref[...], mo_ref[...], vo_ref[...] = _adamw_math(w_ref[...], gv, m_ref[...], v_ref[...])

    full = pl.BlockSpec((tr, C), lambda h, t, c: (h * nt + t, 0))
    half = pl.BlockSpec((tr, C), lambda h, t, c: (t, 0))
    shape = jax.ShapeDtypeStruct((R, C), F32)
    return pl.pallas_call(
        body, name=name,
        grid_spec=pltpu.PrefetchScalarGridSpec(
            num_scalar_prefetch=1, grid=(2, nt), in_specs=[full, half, half, full, full], out_specs=[full] * 4),
        out_shape=[shape] * 4,
        compiler_params=_params(2),
    )(cidx, w, mine, other, m, v)


def _adamw(w, g, m, v, *, name):
    R, C = w.shape
    tr = _row_tile(R, C)

    def body(w_ref, g_ref, m_ref, v_ref, d_ref, mo_ref, vo_ref):
        d_ref[...], mo_ref[...], vo_ref[...] = _adamw_math(w_ref[...], g_ref[...], m_ref[...], v_ref[...])

    blk = pl.BlockSpec((tr, C), lambda t: (t, 0))
    shape = jax.ShapeDtypeStruct((R, C), F32)
    return pl.pallas_call(
        body, name=name, grid=(R // tr,),
        in_specs=[blk] * 4, out_specs=[blk] * 3, out_shape=[shape] * 3,
        compiler_params=_params(1),
    )(w, g, m, v)


def _unstack_cols(w):
    s, r, c = w.shape
    return w.transpose(1, 0, 2).reshape(r, s * c)


def _pad_rows(a, rows):
    return jnp.pad(a, ((0, rows - a.shape[0]), (0, LANES - a.shape[1])))


BIG = ("ffn1_w_gate", "ffn1_w_up", "ffn1_w_down", "w_in", "w_branch_a", "w_branch_b", "w_out",
       "ffn2_w_gate", "ffn2_w_up", "ffn2_w_down")
TRANSPOSED = ("ffn1_w_gate", "ffn1_w_up", "w_in", "ffn2_w_gate", "ffn2_w_up")
WEIGHTS = ("ffn1_norm", "ffn1_w_gate", "ffn1_w_up", "ffn1_w_down", "mix_norm", "w_in", "na_rpb", "sink_logit",
           "w_branch_a", "w_branch_b", "w_out", "ffn2_norm", "ffn2_w_gate", "ffn2_w_up", "ffn2_w_down", "final_norm")


def kernel(x, ffn1_norm, ffn1_w_gate, ffn1_w_up, ffn1_w_down, mix_norm, w_in, na_rpb, sink_logit, w_branch_a, w_branch_b, w_out, ffn2_norm, ffn2_w_gate, ffn2_w_up, ffn2_w_down, final_norm, loss_target, m_ffn1_norm, m_ffn1_w_gate, m_ffn1_w_up, m_ffn1_w_down, m_mix_norm, m_w_in, m_na_rpb, m_sink_logit, m_w_branch_a, m_w_branch_b, m_w_out, m_ffn2_norm, m_ffn2_w_gate, m_ffn2_w_up, m_ffn2_w_down, m_final_norm, v_ffn1_norm, v_ffn1_w_gate, v_ffn1_w_up, v_ffn1_w_down, v_mix_norm, v_w_in, v_na_rpb, v_sink_logit, v_w_branch_a, v_w_branch_b, v_w_out, v_ffn2_norm, v_ffn2_w_gate, v_ffn2_w_up, v_ffn2_w_down, v_final_norm):
    args = dict(locals())
    w = {k: args[k] for k in WEIGHTS}
    mom = {k: args["m_" + k] for k in WEIGHTS}
    var = {k: args["v_" + k] for k in WEIGHTS}
    cidx = lax.axis_index("c").astype(jnp.int32).reshape(1)
    chip = (2 * lax.axis_index("x") + lax.axis_index("y")).astype(jnp.int32).reshape(1)

    def shard(a, k):
        return jnp.swapaxes(a[0], 0, 1) if k in TRANSPOSED else a[0]

    def unshard(a, k):
        return (jnp.swapaxes(a, 0, 1) if k in TRANSPOSED else a)[None]

    full = dict(zip(BIG, _all_gather([shard(w[k], k).astype(BF16) for k in BIG])))
    f1 = (full["ffn1_w_gate"], full["ffn1_w_up"], full["ffn1_w_down"])
    f2 = (full["ffn2_w_gate"], full["ffn2_w_up"], full["ffn2_w_down"])
    out = _layer_grads(
        x[0], loss_target[0], ffn1_norm, f1, mix_norm, full["w_in"].reshape(D_IN, D_MODEL), na_rpb[0], sink_logit[0],
        _unstack_cols(full["w_branch_a"]), _unstack_cols(full["w_branch_b"]), full["w_out"].reshape(D_MODEL, D_MODEL),
        ffn2_norm, f2, final_norm.reshape(1, D_MODEL))

    by_chip = dict(zip(BIG, [out["f1"][0], out["f1"][1], out["f1"][2], out["w_in"].reshape(N_CHIPS, D_IN // N_CHIPS, D_MODEL),
                             out["wa"], out["wb"], out["wout"].reshape(N_CHIPS, D_MODEL // N_CHIPS, D_MODEL),
                             out["f2"][0], out["f2"][1], out["f2"][2]]))
    grads = [by_chip[k] for k in BIG]
    from_sibling = _rs_sibling(grads)
    parts = [_add_sibling(g, r, cidx, name="add_sibling_" + k) for k, g, r in zip(BIG, grads, from_sibling)]
    from_chips = _rs_chips([p16 for _, p16 in parts])
    halves = [_add_chips(p, r, chip, name="add_chips_" + k) for k, (p, _), r in zip(BIG, parts, from_chips)]
    mine = dict(zip(BIG, halves))
    other = dict(zip(BIG, _rs_share(halves)))
    grad = {}

    rows = D_MODEL // LANES
    small = jnp.concatenate([
        out["g1"].reshape(rows, LANES), out["gmix"].reshape(rows, LANES), out["g2"].reshape(rows, LANES),
        out["gfin"].reshape(rows, LANES), out["rpb"].reshape(-1, LANES),
        _pad_rows(out["sink"].reshape(1, NB_HEADS), 8), _pad_rows(out["loss"], 8)], axis=0)
    total = _small_allreduce(small)
    n_rpb = NA_HEADS * 2 * NA_KH
    grad["ffn1_norm"] = total[0:rows].reshape(1, D_MODEL)
    grad["mix_norm"] = total[rows:2 * rows].reshape(1, D_MODEL)
    grad["ffn2_norm"] = total[2 * rows:3 * rows].reshape(1, D_MODEL)
    grad["final_norm"] = total[3 * rows:4 * rows].reshape(1, D_MODEL)
    grad["na_rpb"] = total[4 * rows:4 * rows + n_rpb].reshape(NA_HEADS, 2 * NA_KH, LANES)[:, :2 * NA_KH - 1, :2 * NA_KW - 1]
    grad["na_rpb"] = grad["na_rpb"].reshape(NA_HEADS, -1)
    grad["sink_logit"] = total[4 * rows + n_rpb:4 * rows + n_rpb + 1, 0:NB_HEADS]
    loss = total[4 * rows + n_rpb + 8, 0]

    deltas, new_m, new_v, grads_out = {}, {}, {}, {}
    for k in WEIGHTS:
        shape = w[k].shape
        if k in mine:
            res = _adamw_halves(shard(w[k], k), mine[k], other[k], shard(mom[k], k), shard(var[k], k), cidx, name="adamw_" + k)
            grads_out[k], deltas[k], new_m[k], new_v[k] = (unshard(a, k) for a in res)
        else:
            g2d = grad[k]
            d, mn, vn = _adamw(w[k].reshape(g2d.shape), g2d, mom[k].reshape(g2d.shape), var[k].reshape(g2d.shape), name="adamw_" + k)
            grads_out[k], deltas[k], new_m[k], new_v[k] = (a.reshape(shape) for a in (g2d, d, mn, vn))
    return (loss, out["dx"].reshape(x.shape), *[grads_out[k] for k in WEIGHTS], *[deltas[k] for k in WEIGHTS],
            *[new_m[k] for k in WEIGHTS], *[new_v[k] for k in WEIGHTS])
```

```python
import functools
import math

import jax
import jax.numpy as jnp
from jax import lax
from jax.experimental import pallas as pl
from jax.experimental.pallas import tpu as pltpu

F32 = jnp.float32
BF16 = jnp.bfloat16

D_MODEL = 1024
HEAD_DIM = 64
NA_HEADS = 8
NB_HEADS = 8
GRID_W = 64
NA_KH = 8
NA_KW = 16
WIN = 128
ROPE_THETA = 10000.0
EPS = 1e-6
N_CHIPS = 4
QK_SCALE = HEAD_DIM ** -0.5
NEG = -1e30
LANES = 128
VMEM_LIMIT = 56 * 1024 * 1024

C_QKVA = 3 * NA_HEADS * HEAD_DIM
C_QB = NB_HEADS * HEAD_DIM
C_KB = 2 * HEAD_DIM
C_ROPE = C_QB + C_KB
C_GATES = 2 * D_MODEL
D_IN = C_QKVA + C_QB + 2 * C_KB + C_GATES
O_QB = C_QKVA
O_KB = O_QB + C_QB
O_VB = O_KB + C_KB
O_G = O_VB + C_KB

ADAM_LR = 0.001
ADAM_B1 = 0.9
ADAM_B2 = 0.999
ADAM_EPS = 1e-08
ADAM_WD = 0.01
ADAM_STEP = 10

MESH = pl.DeviceIdType.MESH


def _dot(a, b):
    return jnp.dot(a, b, preferred_element_type=F32)


def _dot_nt(a, b):
    return lax.dot_general(a, b, (((1,), (1,)), ((), ())), preferred_element_type=F32)


def _dot_tn(a, b):
    return lax.dot_general(a, b, (((0,), (0,)), ((), ())), preferred_element_type=F32)


def _params(n_axes):
    return pltpu.CompilerParams(dimension_semantics=("arbitrary",) * n_axes, vmem_limit_bytes=VMEM_LIMIT)


def _rstd(xf):
    return lax.rsqrt(jnp.mean(xf * xf, axis=-1, keepdims=True) + EPS)


def _norm_bwd(dn, xf, g, r):
    xhat = xf * r
    dxh = dn * g
    dx = r * (dxh - xhat * jnp.mean(dxh * xhat, axis=-1, keepdims=True))
    return dx, dn * xhat


def _sigmoid(x):
    return 0.5 * jnp.tanh(0.5 * x) + 0.5


def _ffn_fwd(x, g, wg, wu, wd, *, name, tm=1024, sub=512, rider=None):
    T, D = x.shape
    F = wg.shape[1]

    def body(x_ref, g_ref, wg_ref, wu_ref, wd_ref, h_ref, n_ref, hdn_ref, p_ref, q_ref):
        s = pl.program_id(1)

        @pl.when(s == 0)
        def _():
            xf = x_ref[...]
            n_ref[...] = ((xf * _rstd(xf)) * g_ref[...]).astype(BF16)
            h_ref[...] = xf

        rows = [pl.ds(u * sub, sub) for u in range(tm // sub)]
        ab = [(_dot_nt(n_ref[r, :], wg_ref[0]), _dot_nt(n_ref[r, :], wu_ref[0])) for r in rows]
        hdns = []
        for r, (a, b) in zip(rows, ab):
            sg = _sigmoid(a)
            silu = a * sg
            hdn = (silu * b).astype(BF16)
            hdn_ref[0, r, :] = hdn
            p_ref[0, r, :] = (b * (sg + silu * (1.0 - sg))).astype(BF16)
            q_ref[0, r, :] = silu.astype(BF16)
            hdns.append(hdn)
        for r, hdn in zip(rows, hdns):
            h_ref[r, :] += 0.5 * _dot(hdn, wd_ref[0])

    tok = pl.BlockSpec((tm, D), lambda i, s: (i, 0))
    hid = pl.BlockSpec((1, tm, F), lambda i, s: (s, i, 0))
    wspec = pl.BlockSpec((1, F, D), lambda i, s: (s, 0, 0))
    hshape = jax.ShapeDtypeStruct((N_CHIPS, T, F), BF16)
    grid = (T // tm, N_CHIPS)
    body, r_in, r_in_specs, r_out, r_out_specs, scratch = _ride(body, 5, 5, rider, grid, (grid[0] * grid[1] * 7) // 8)
    outs = pl.pallas_call(
        body, name=name, grid=grid,
        in_specs=[tok, pl.BlockSpec((1, D), lambda i, s: (0, 0)), wspec, wspec, wspec] + r_in_specs,
        out_specs=[tok, tok, hid, hid, hid] + r_out_specs,
        out_shape=[jax.ShapeDtypeStruct((T, D), F32), jax.ShapeDtypeStruct((T, D), BF16), hshape, hshape, hshape] + r_out,
        scratch_shapes=scratch,
        compiler_params=_params(2),
    )(x, g, wg, wu, wd, *r_in)
    return (*outs[:5], list(outs[5:]))


def _ffn_bwd(dh, x, g, p, q, wg, wu, wd, *, name, tm=512, sub=256, rider=None):
    T, D = x.shape
    F = wg.shape[1]

    def body(dh_ref, x_ref, g_ref, p_ref, q_ref, wg_ref, wu_ref, wd_ref, dx_ref, da_ref, db_ref, dg_ref):
        i, s = pl.program_id(0), pl.program_id(1)

        @pl.when((i == 0) & (s == 0))
        def _():
            dg_ref[...] = jnp.zeros_like(dg_ref)

        @pl.when(s == 0)
        def _():
            dx_ref[...] = jnp.zeros_like(dx_ref)

        rows = [pl.ds(u * sub, sub) for u in range(tm // sub)]
        dhdn = [_dot_nt((0.5 * dh_ref[r, :]).astype(BF16), wd_ref[0]) for r in rows]
        das, dbs = [], []
        for r, dd in zip(rows, dhdn):
            da = (dd * p_ref[0, r, :].astype(F32)).astype(BF16)
            db = (dd * q_ref[0, r, :].astype(F32)).astype(BF16)
            da_ref[0, r, :] = da
            db_ref[0, r, :] = db
            das.append(da)
            dbs.append(db)
        for r, da, db in zip(rows, das, dbs):
            dx_ref[r, :] += _dot(da, wg_ref[0]) + _dot(db, wu_ref[0])

        @pl.when(s == N_CHIPS - 1)
        def _():
            xf = x_ref[...]
            dx, dgr = _norm_bwd(dx_ref[...], xf, g_ref[...], _rstd(xf))
            dg_ref[...] += jnp.sum(dgr, axis=0, keepdims=True)
            dx_ref[...] = dh_ref[...] + dx

    tok = pl.BlockSpec((tm, D), lambda i, s: (i, 0))
    hid = pl.BlockSpec((1, tm, F), lambda i, s: (s, i, 0))
    vec = pl.BlockSpec((1, D), lambda i, s: (0, 0))
    hshape = jax.ShapeDtypeStruct((N_CHIPS, T, F), BF16)
    wspec = pl.BlockSpec((1, F, D), lambda i, s: (s, 0, 0))
    grid = (T // tm, N_CHIPS)
    body, r_in, r_in_specs, r_out, r_out_specs, scratch = _ride(body, 8, 4, rider, grid, None)
    outs = pl.pallas_call(
        body, name=name, grid=grid,
        in_specs=[tok, tok, vec, hid, hid, wspec, wspec, wspec] + r_in_specs,
        out_specs=[tok, hid, hid, vec] + r_out_specs,
        out_shape=[jax.ShapeDtypeStruct((T, D), F32), hshape, hshape, jax.ShapeDtypeStruct((1, D), F32)] + r_out,
        scratch_shapes=scratch,
        compiler_params=_params(2),
    )(dh, x, g, p, q, wg, wu, wd, *r_in)
    return (*outs[:4], list(outs[4:]))


def _wgrad(a, b, *, a_block, a_map, b_block, b_map, out_shape, o_block, o_map, grid, scale=1.0, name):
    def body(a_ref, b_ref, o_ref):
        @pl.when(pl.program_id(len(grid) - 1) == 0)
        def _():
            o_ref[...] = jnp.zeros_like(o_ref)

        av = a_ref[...]
        bv = b_ref[...]
        av = av.reshape(av.shape[-2:]).astype(BF16)
        bv = bv.reshape(bv.shape[-2:])
        if scale != 1.0:
            bv = scale * bv
        o_ref[...] += _dot_tn(av, bv.astype(BF16)).reshape(o_ref.shape)

    return pl.pallas_call(
        body, name=name, grid=grid,
        in_specs=[pl.BlockSpec(a_block, a_map), pl.BlockSpec(b_block, b_map)],
        out_specs=pl.BlockSpec(o_block, o_map),
        out_shape=jax.ShapeDtypeStruct(out_shape, F32),
        compiler_params=_params(len(grid)),
    )(a, b)


def _wgrad_rows(a, b, n_blocks, *, name, tk=1024):
    T, N = b.shape
    M = a.shape[1] // n_blocks
    tk = min(tk, T)
    return _wgrad(a, b, a_block=(tk, M), a_map=lambda s, k: (k, s), b_block=(tk, N), b_map=lambda s, k: (k, 0),
                  out_shape=(n_blocks, M, N), o_block=(1, M, N), o_map=lambda s, k: (s, 0, 0), grid=(n_blocks, T // tk), name=name)


def _wgrad_shard_a(a, b, *, name, scale=1.0, tk=2048):
    S, T, M = a.shape
    N = b.shape[1]
    tk = min(tk, T)
    return _wgrad(a, b, a_block=(1, tk, M), a_map=lambda s, k: (s, k, 0), b_block=(tk, N), b_map=lambda s, k: (k, 0),
                  out_shape=(S, M, N), o_block=(1, M, N), o_map=lambda s, k: (s, 0, 0), grid=(S, T // tk), scale=scale, name=name)


def _wgrad_cols(a, b, n_blocks, *, name, tk=1024):
    T, M = a.shape
    N = b.shape[1] // n_blocks
    tk = min(tk, T)

    def body(a_ref, b_ref, o_ref):
        @pl.when(pl.program_id(0) == 0)
        def _():
            o_ref[...] = jnp.zeros_like(o_ref)

        r = _dot_tn(a_ref[...].astype(BF16), b_ref[...].astype(BF16))
        for s in range(n_blocks):
            o_ref[s] += r[:, s * N:(s + 1) * N]

    return pl.pallas_call(
        body, name=name, grid=(T // tk,),
        in_specs=[pl.BlockSpec((tk, M), lambda k: (k, 0)), pl.BlockSpec((tk, n_blocks * N), lambda k: (k, 0))],
        out_specs=pl.BlockSpec((n_blocks, M, N), lambda k: (0, 0, 0)),
        out_shape=jax.ShapeDtypeStruct((n_blocks, M, N), F32),
        compiler_params=_params(1),
    )(a, b)


def _rope_tables(T):
    half = HEAD_DIM // 2
    inv = ROPE_THETA ** (-jnp.arange(half, dtype=F32) / half)
    ang = jnp.arange(T, dtype=F32)[:, None] * inv[None, :]
    cos, sin, zero = jnp.cos(ang), jnp.sin(ang), jnp.zeros_like(ang)
    reps = LANES // HEAD_DIM
    return (jnp.tile(jnp.concatenate([cos, cos], axis=1), (1, reps)),
            jnp.tile(jnp.concatenate([-sin, zero], axis=1), (1, reps)),
            jnp.tile(jnp.concatenate([zero, sin], axis=1), (1, reps)))


def _rope(x, cos, sa, sb, sign):
    half = HEAD_DIM // 2
    return x * cos + sign * (pltpu.roll(x, LANES - half, 1) * sa + pltpu.roll(x, half, 1) * sb)


def _mix_in_fwd(h, g, w_in, tables, *, tm=256):
    T, D = h.shape

    def body(h_ref, g_ref, w_ref, cos_ref, sa_ref, sb_ref, u_ref, qkva_ref, qb_ref, kvb_ref, gates_ref):
        hf = h_ref[...]
        u = ((hf * _rstd(hf)) * g_ref[...]).astype(BF16)
        u_ref[...] = u
        qkva_ref[...] = _dot_nt(u, w_ref[0:C_QKVA, :]).astype(BF16)
        zr = _dot_nt(u, w_ref[O_QB:O_QB + C_ROPE, :])
        cos, sa, sb = cos_ref[...], sa_ref[...], sb_ref[...]
        for j in range(C_ROPE // LANES):
            rj = _rope(zr[:, j * LANES:(j + 1) * LANES], cos, sa, sb, 1.0).astype(BF16)
            if j < C_QB // LANES:
                qb_ref[:, j * LANES:(j + 1) * LANES] = rj
            else:
                kvb_ref[:, 0:C_KB] = rj
        kvb_ref[:, C_KB:2 * C_KB] = _dot_nt(u, w_ref[O_VB:O_VB + C_KB, :]).astype(BF16)
        gates_ref[...] = _dot_nt(u, w_ref[O_G:O_G + C_GATES, :])

    def tok(n):
        return pl.BlockSpec((tm, n), lambda i: (i, 0))

    return pl.pallas_call(
        body, name="mix_in_fwd", grid=(T // tm,),
        in_specs=[tok(D), pl.BlockSpec((1, D), lambda i: (0, 0)), pl.BlockSpec((D_IN, D), lambda i: (0, 0)),
                  tok(LANES), tok(LANES), tok(LANES)],
        out_specs=[tok(D), tok(C_QKVA), tok(C_QB), tok(2 * C_KB), tok(C_GATES)],
        out_shape=[jax.ShapeDtypeStruct((T, D), BF16), jax.ShapeDtypeStruct((T, C_QKVA), BF16),
                   jax.ShapeDtypeStruct((T, C_QB), BF16), jax.ShapeDtypeStruct((T, 2 * C_KB), BF16),
                   jax.ShapeDtypeStruct((T, C_GATES), F32)],
        compiler_params=_params(1),
    )(h, g, w_in, *tables)


def _mix_in_bwd(dqa, dka, dva, dqb, dkb, dvb, dgates, h, g, dres, w_in, tables, *, tm=256):
    T, D = h.shape

    def body(dqa_ref, dka_ref, dva_ref, dqb_ref, dkb_ref, dvb_ref, dgt_ref, h_ref, g_ref, dres_ref, w_ref,
             cos_ref, sa_ref, sb_ref, dz_ref, dh_ref, dg_ref):
        @pl.when(pl.program_id(0) == 0)
        def _():
            dg_ref[...] = jnp.zeros_like(dg_ref)

        na = NA_HEADS * HEAD_DIM
        dz_ref[:, 0:na] = dqa_ref[...].astype(BF16)
        dz_ref[:, na:2 * na] = dka_ref[...].astype(BF16)
        dz_ref[:, 2 * na:3 * na] = dva_ref[...].astype(BF16)
        cos, sa, sb = cos_ref[...], sa_ref[...], sb_ref[...]
        for j in range(C_QB // LANES):
            dz_ref[:, O_QB + j * LANES:O_QB + (j + 1) * LANES] = _rope(
                dqb_ref[:, j * LANES:(j + 1) * LANES], cos, sa, sb, -1.0).astype(BF16)
        dz_ref[:, O_KB:O_KB + C_KB] = _rope(dkb_ref[...], cos, sa, sb, -1.0).astype(BF16)
        dz_ref[:, O_VB:O_VB + C_KB] = dvb_ref[...].astype(BF16)
        dz_ref[:, O_G:O_G + C_GATES] = dgt_ref[...].astype(BF16)
        du = _dot(dz_ref[...], w_ref[...])
        hf = h_ref[...]
        dx, dgr = _norm_bwd(du, hf, g_ref[...], _rstd(hf))
        dg_ref[...] += jnp.sum(dgr, axis=0, keepdims=True)
        dh_ref[...] = dres_ref[...] + dx

    def tok(n):
        return pl.BlockSpec((tm, n), lambda i: (i, 0))

    vec = pl.BlockSpec((1, D), lambda i: (0, 0))
    na = NA_HEADS * HEAD_DIM
    return pl.pallas_call(
        body, name="mix_in_bwd", grid=(T // tm,),
        in_specs=[tok(na), tok(na), tok(na), tok(C_QB), tok(C_KB), tok(C_KB), tok(C_GATES), tok(D), vec, tok(D),
                  pl.BlockSpec((D_IN, D), lambda i: (0, 0)), tok(LANES), tok(LANES), tok(LANES)],
        out_specs=[tok(D_IN), tok(D), vec],
        out_shape=[jax.ShapeDtypeStruct((T, D_IN), BF16), jax.ShapeDtypeStruct((T, D), F32),
                   jax.ShapeDtypeStruct((1, D), F32)],
        compiler_params=_params(1),
    )(dqa, dka, dva, dqb, dkb, dvb, dgates, h, g, dres, w_in, *tables)


def _na_bias_slabs(rpb):
    H = rpb.shape[0]
    ncell = GRID_W * GRID_W
    cell = jnp.arange(ncell)
    co = cell % GRID_W - cell // GRID_W + (NA_KW - 1)
    e_co = (jnp.arange(LANES)[:, None] == co[None, :]).astype(F32)
    table = jnp.pad(rpb, ((0, 0), (0, 1), (0, LANES - rpb.shape[2]))).reshape(H * 2 * NA_KH, LANES)

    def body(t_ref, e_ref, o_ref):
        o_ref[...] = jnp.dot(t_ref[...], e_ref[...], preferred_element_type=F32, precision=lax.Precision.HIGHEST)

    toeplitz = pl.pallas_call(
        body, name="rpb_unfold", out_shape=jax.ShapeDtypeStruct((H * 2 * NA_KH, ncell), F32),
        compiler_params=_params(0),
    )(table, e_co).reshape(H, 2 * NA_KH, GRID_W, GRID_W)
    c = jnp.arange(GRID_W)
    cs = jnp.clip(c - NA_KW // 2, 0, GRID_W - NA_KW)
    inwin = (c[None, :] >= cs[:, None]) & (c[None, :] < cs[:, None] + NA_KW)
    toeplitz = jnp.where(inwin[None, None], toeplitz, NEG)
    slabs = jnp.stack([toeplitz[:, r:r + NA_KH] for r in range(NA_KH)], axis=1)
    slabs = slabs.reshape(H // 2, 2, NA_KH, NA_KH, GRID_W, GRID_W).transpose(0, 2, 1, 4, 3, 5)
    return slabs.reshape(H // 2, NA_KH, 2 * GRID_W, NA_KH * GRID_W)


def _na_unstack_slabs(dslab):
    pairs = dslab.shape[0]
    d = dslab.reshape(pairs, NA_KH, 2, GRID_W, NA_KH * GRID_W).transpose(0, 2, 1, 3, 4)
    return d.reshape(2 * pairs, NA_KH, GRID_W, NA_KH * GRID_W)


def _half_masks(rows):
    lane = lax.broadcasted_iota(jnp.int32, (rows, LANES), 1)
    left = lane < HEAD_DIM
    return left, (left, jnp.logical_not(left))


def _stack_heads(x):
    left, halves = _half_masks(x.shape[0])
    xf = x.astype(F32)
    return jnp.concatenate([jnp.where(m, xf, 0.0).astype(BF16) for m in halves], axis=0)


def _unstack_heads(o):
    rows = o.shape[0] // 2
    left, _ = _half_masks(rows)
    return jnp.where(left, o[:rows], o[rows:])


def _na_row(j, t, rb, rows):
    r = j * rb + t
    rs = jnp.clip(r - NA_KH // 2, 0, rows - NA_KH)
    return pl.multiple_of(t * GRID_W, GRID_W), pl.multiple_of(rs * GRID_W, GRID_W), rs - r + (NA_KH - 1)


def _na_specs(T, rb):
    qrows = GRID_W * rb
    pairs = NA_HEADS // 2
    return ([pl.BlockSpec((qrows, LANES), lambda p, j: (j, p)),
             pl.BlockSpec((T, LANES), lambda p, j: (0, pairs + p)),
             pl.BlockSpec((T, LANES), lambda p, j: (0, 2 * pairs + p))],
            pl.BlockSpec((1, NA_KH, 2 * GRID_W, NA_KH * GRID_W), lambda p, j: (p, 0, 0, 0)))


def _softmax(s):
    p = jnp.exp(s - jnp.max(s, axis=-1, keepdims=True))
    return p / jnp.sum(p, axis=-1, keepdims=True)


def _na_probs(qs, ks, bias):
    return _softmax(_dot_nt(qs, ks) * QK_SCALE + bias)


def _na_fwd(qkva, bias, *, rb=8, group=4):
    T = qkva.shape[0]
    rows = T // GRID_W
    nkeys = NA_KH * GRID_W

    def body(q_ref, k_ref, v_ref, bias_ref, y_ref):
        j = pl.program_id(1)

        def rows_step(t, carry):
            at = [_na_row(j, t * group + u, rb, rows) for u in range(group)]
            s = [_dot_nt(_stack_heads(q_ref[pl.ds(q0, GRID_W), :]), k_ref[pl.ds(k0, nkeys), :]) for q0, k0, _ in at]
            p = [_softmax(su * QK_SCALE + bias_ref[0, ro0]) for su, (_, _, ro0) in zip(s, at)]
            o = [_dot(pu.astype(BF16), v_ref[pl.ds(k0, nkeys), :]) for pu, (_, k0, _) in zip(p, at)]
            for ou, (q0, _, _) in zip(o, at):
                y_ref[pl.ds(q0, GRID_W), :] = _unstack_heads(ou).astype(BF16)
            return carry

        lax.fori_loop(0, rb // group, rows_step, 0)

    qkv_specs, bias_spec = _na_specs(T, rb)
    return pl.pallas_call(
        body, name="na_fwd", grid=(NA_HEADS // 2, rows // rb),
        in_specs=qkv_specs + [bias_spec],
        out_specs=qkv_specs[0],
        out_shape=jax.ShapeDtypeStruct((T, NA_HEADS * HEAD_DIM), BF16),
        compiler_params=_params(2),
    )(qkva, qkva, qkva, bias)


def _na_bwd(qkva, dy, bias, *, rb=8, group=4, rider=None):
    T = qkva.shape[0]
    rows = T // GRID_W
    nkeys = NA_KH * GRID_W

    def body(q_ref, k_ref, v_ref, dy_ref, bias_ref, dq_ref, dk_ref, dv_ref, dbias_ref):
        j = pl.program_id(1)

        @pl.when(j == 0)
        def _():
            dk_ref[...] = jnp.zeros_like(dk_ref)
            dv_ref[...] = jnp.zeros_like(dv_ref)
            dbias_ref[...] = jnp.zeros_like(dbias_ref)

        def rows_step(t, carry):
            at = [_na_row(j, t * group + u, rb, rows) for u in range(group)]
            qs = [_stack_heads(q_ref[pl.ds(q0, GRID_W), :]) for q0, _, _ in at]
            dys = [_stack_heads(dy_ref[pl.ds(q0, GRID_W), :]) for q0, _, _ in at]
            s = [_dot_nt(qu, k_ref[pl.ds(k0, nkeys), :]) for qu, (_, k0, _) in zip(qs, at)]
            dp = [_dot_nt(du, v_ref[pl.ds(k0, nkeys), :]) for du, (_, k0, _) in zip(dys, at)]
            p = [_softmax(su * QK_SCALE + bias_ref[0, ro0]) for su, (_, _, ro0) in zip(s, at)]
            ds = [pu * (du - jnp.sum(pu * du, axis=-1, keepdims=True)) for pu, du in zip(p, dp)]
            for u, (q0, k0, ro0) in enumerate(at):
                dbias_ref[0, ro0] += ds[u]
                dsb = ds[u].astype(BF16)
                dq_ref[pl.ds(q0, GRID_W), :] = (_unstack_heads(_dot(dsb, k_ref[pl.ds(k0, nkeys), :])) * QK_SCALE).astype(BF16)
                dk_ref[pl.ds(k0, nkeys), :] += _dot_tn(dsb, qs[u]) * QK_SCALE
                dv_ref[pl.ds(k0, nkeys), :] += _dot_tn(p[u].astype(BF16), dys[u])
            return carry

        lax.fori_loop(0, rb // group, rows_step, 0)

    qkv_specs, bias_spec = _na_specs(T, rb)
    width = NA_HEADS * HEAD_DIM
    kv_out = pl.BlockSpec((T, LANES), lambda p, j: (0, p))
    grid = (NA_HEADS // 2, rows // rb)
    body, r_in, r_in_specs, r_out, r_out_specs, scratch = _ride(body, 5, 4, rider, grid, None)
    outs = pl.pallas_call(
        body, name="na_bwd", grid=grid,
        in_specs=qkv_specs + [qkv_specs[0], bias_spec] + r_in_specs,
        out_specs=[qkv_specs[0], kv_out, kv_out, bias_spec] + r_out_specs,
        out_shape=[jax.ShapeDtypeStruct((T, width), BF16), jax.ShapeDtypeStruct((T, width), F32),
                   jax.ShapeDtypeStruct((T, width), F32), jax.ShapeDtypeStruct(bias.shape, F32)] + r_out,
        scratch_shapes=scratch,
        compiler_params=_params(2),
    )(qkva, qkva, qkva, dy, bias, *r_in)
    return (*outs[:4], list(outs[4:]))


def _rpb_fold(dslab):
    H = dslab.shape[0]
    nro = NA_KH * NA_KH
    ncell = GRID_W * GRID_W
    xs = dslab.reshape(H, NA_KH, GRID_W, NA_KH, GRID_W).transpose(0, 1, 3, 2, 4).reshape(H, nro, ncell)
    cell = jnp.arange(ncell)
    co = cell % GRID_W - cell // GRID_W + (NA_KW - 1)
    e_co = (co[:, None] == jnp.arange(LANES)[None, :]).astype(F32)
    pair = jnp.arange(nro)
    e_ro = ((pair // NA_KH + pair % NA_KH)[None, :] == jnp.arange(2 * NA_KH)[:, None]).astype(F32)

    def body(x_ref, eco_ref, ero_ref, o_ref):
        y = jnp.dot(x_ref[0], eco_ref[...], preferred_element_type=F32, precision=lax.Precision.HIGHEST)
        o_ref[0] = jnp.dot(ero_ref[...], y, preferred_element_type=F32, precision=lax.Precision.HIGHEST)

    return pl.pallas_call(
        body, name="rpb_fold", grid=(H,),
        in_specs=[pl.BlockSpec((1, nro, ncell), lambda h: (h, 0, 0)), pl.BlockSpec((ncell, LANES), lambda h: (0, 0)),
                  pl.BlockSpec((2 * NA_KH, nro), lambda h: (0, 0))],
        out_specs=pl.BlockSpec((1, 2 * NA_KH, LANES), lambda h: (h, 0, 0)),
        out_shape=jax.ShapeDtypeStruct((H, 2 * NA_KH, LANES), F32),
        compiler_params=_params(1),
    )(xs, e_co, e_ro)


SWA_KEYS = 3 * WIN


def _swa_block(j, t, qbn, T):
    blk = j * qbn + t
    start = jnp.clip((blk - 1) * WIN, 0, T - SWA_KEYS)
    row = lax.broadcasted_iota(jnp.int32, (2 * WIN, SWA_KEYS), 0)
    qpos = blk * WIN + jnp.where(row < WIN, row, row - WIN)
    kpos = start + lax.broadcasted_iota(jnp.int32, (2 * WIN, SWA_KEYS), 1)
    return pl.multiple_of(t * WIN, WIN), pl.multiple_of(start, WIN), jnp.abs(qpos - kpos) <= WIN


def _swa_sinks(sink_ref, p):
    row = lax.broadcasted_iota(jnp.int32, (2 * WIN, 1), 0)
    return jnp.where(row < WIN, sink_ref[p], sink_ref[p + NB_HEADS // 2])


def _swa_probs(s, mask, sink):
    s = jnp.where(mask, s * QK_SCALE, NEG)
    m = jnp.maximum(jnp.max(s, axis=-1, keepdims=True), sink)
    e = jnp.exp(s - m)
    esink = jnp.exp(sink - m)
    den = jnp.sum(e, axis=-1, keepdims=True) + esink
    return e / den, esink / den


def _swa_specs(T, qbn):
    return [pl.BlockSpec(memory_space=pltpu.SMEM),
            pl.BlockSpec((WIN * qbn, LANES), lambda p, j: (j, p)),
            pl.BlockSpec((T, LANES), lambda p, j: (0, 0)),
            pl.BlockSpec((T, LANES), lambda p, j: (0, 1))]


def _swa_fwd(qb, kvb, sink, *, qbn=4, group=4):
    T = qb.shape[0]
    pairs = NB_HEADS // 2

    def body(sink_ref, q_ref, k_ref, v_ref, y_ref):
        p, j = pl.program_id(0), pl.program_id(1)
        sinks = _swa_sinks(sink_ref, p)

        def blocks_step(t, carry):
            at = [_swa_block(j, t * group + u, qbn, T) for u in range(group)]
            s = [_dot_nt(_stack_heads(q_ref[pl.ds(q0, WIN), :]), k_ref[pl.ds(k0, SWA_KEYS), :]) for q0, k0, _ in at]
            pr = [_swa_probs(su, mask, sinks)[0] for su, (_, _, mask) in zip(s, at)]
            o = [_dot(pu.astype(BF16), v_ref[pl.ds(k0, SWA_KEYS), :]) for pu, (_, k0, _) in zip(pr, at)]
            for ou, (q0, _, _) in zip(o, at):
                y_ref[pl.ds(q0, WIN), :] = _unstack_heads(ou).astype(BF16)
            return carry

        lax.fori_loop(0, qbn // group, blocks_step, 0)

    specs = _swa_specs(T, qbn)
    return pl.pallas_call(
        body, name="swa_fwd", grid=(pairs, T // (WIN * qbn)),
        in_specs=specs, out_specs=specs[1],
        out_shape=jax.ShapeDtypeStruct((T, NB_HEADS * HEAD_DIM), BF16),
        compiler_params=_params(2),
    )(sink, qb, kvb, kvb)


def _swa_bwd(qb, kvb, dy, sink, *, qbn=4, group=4):
    T = qb.shape[0]
    pairs = NB_HEADS // 2

    def body(sink_ref, q_ref, k_ref, v_ref, dy_ref, dq_ref, dk_ref, dv_ref, dsink_ref):
        p, j = pl.program_id(0), pl.program_id(1)
        sinks = _swa_sinks(sink_ref, p)

        @pl.when((p == 0) & (j == 0))
        def _():
            dk_ref[...] = jnp.zeros_like(dk_ref)
            dv_ref[...] = jnp.zeros_like(dv_ref)

        @pl.when(j == 0)
        def _():
            dsink_ref[...] = jnp.zeros_like(dsink_ref)

        def blocks_step(t, carry):
            at = [_swa_block(j, t * group + u, qbn, T) for u in range(group)]
            qs = [_stack_heads(q_ref[pl.ds(q0, WIN), :]) for q0, _, _ in at]
            dys = [_stack_heads(dy_ref[pl.ds(q0, WIN), :]) for q0, _, _ in at]
            s = [_dot_nt(qu, k_ref[pl.ds(k0, SWA_KEYS), :]) for qu, (_, k0, _) in zip(qs, at)]
            dp = [_dot_nt(du, v_ref[pl.ds(k0, SWA_KEYS), :]) for du, (_, k0, _) in zip(dys, at)]
            probs = [_swa_probs(su, mask, sinks) for su, (_, _, mask) in zip(s, at)]
            for u, (q0, k0, _) in enumerate(at):
                pr, psink = probs[u]
                delta = jnp.sum(pr * dp[u], axis=-1, keepdims=True)
                dsb = (pr * (dp[u] - delta)).astype(BF16)
                dsk = psink * delta
                for hh in range(2):
                    dsink_ref[0, hh:hh + 1, :] += jnp.broadcast_to(-jnp.sum(dsk[hh * WIN:(hh + 1) * WIN]), (1, LANES))
                dq_ref[pl.ds(q0, WIN), :] = _unstack_heads(_dot(dsb, k_ref[pl.ds(k0, SWA_KEYS), :])) * QK_SCALE
                dk_ref[pl.ds(k0, SWA_KEYS), :] += _dot_tn(dsb, qs[u]) * QK_SCALE
                dv_ref[pl.ds(k0, SWA_KEYS), :] += _dot_tn(pr.astype(BF16), dys[u])
            return carry

        lax.fori_loop(0, qbn // group, blocks_step, 0)

    specs = _swa_specs(T, qbn)
    kv_out = pl.BlockSpec((T, LANES), lambda p, j: (0, 0))
    return pl.pallas_call(
        body, name="swa_bwd", grid=(pairs, T // (WIN * qbn)),
        in_specs=specs + [specs[1]],
        out_specs=[specs[1], kv_out, kv_out, pl.BlockSpec((1, 8, LANES), lambda p, j: (p, 0, 0))],
        out_shape=[jax.ShapeDtypeStruct((T, NB_HEADS * HEAD_DIM), F32), jax.ShapeDtypeStruct((T, LANES), F32),
                   jax.ShapeDtypeStruct((T, LANES), F32), jax.ShapeDtypeStruct((pairs, 8, LANES), F32)],
        compiler_params=_params(2),
    )(sink, qb, kvb, kvb, dy)


def _merge_fwd(ya, yb, gates, wa, wb, wout, h, *, tm=512):
    T, D = h.shape
    W = ya.shape[1]

    def body(ya_ref, yb_ref, gt_ref, wa_ref, wb_ref, wo_ref, h_ref, h2_ref, mg_ref):
        pa = _dot(ya_ref[...], wa_ref[...])
        pb = _dot(yb_ref[...], wb_ref[...])
        mg = (jax.nn.sigmoid(gt_ref[:, 0:D]) * pa + jax.nn.sigmoid(gt_ref[:, D:2 * D]) * pb).astype(BF16)
        mg_ref[...] = mg
        h2_ref[...] = h_ref[...] + _dot(mg, wo_ref[...])

    def tok(n):
        return pl.BlockSpec((tm, n), lambda i: (i, 0))

    def full(r, c):
        return pl.BlockSpec((r, c), lambda i: (0, 0))

    return pl.pallas_call(
        body, name="merge_fwd", grid=(T // tm,),
        in_specs=[tok(W), tok(W), tok(2 * D), full(W, D), full(W, D), full(D, D), tok(D)],
        out_specs=[tok(D), tok(D)],
        out_shape=[jax.ShapeDtypeStruct((T, D), F32), jax.ShapeDtypeStruct((T, D), BF16)],
        compiler_params=_params(1),
    )(ya, yb, gates, wa, wb, wout, h)


def _merge_bwd(dh, ya, yb, gates, wa, wb, wout, *, tm=512):
    T, D = dh.shape
    W = ya.shape[1]

    def body(dh_ref, ya_ref, yb_ref, gt_ref, wa_ref, wb_ref, wo_ref, dya_ref, dyb_ref, dpa_ref, dpb_ref, dgt_ref):
        dmg = _dot_nt(dh_ref[...].astype(BF16), wo_ref[...])
        for y_ref, w_ref, dy_ref, dp_ref, lo in ((ya_ref, wa_ref, dya_ref, dpa_ref, 0), (yb_ref, wb_ref, dyb_ref, dpb_ref, D)):
            sg = jax.nn.sigmoid(gt_ref[:, lo:lo + D])
            dp = (dmg * sg).astype(BF16)
            dp_ref[...] = dp
            dgt_ref[:, lo:lo + D] = (dmg * _dot(y_ref[...], w_ref[...]) * (sg * (1.0 - sg))).astype(BF16)
            dy_ref[...] = _dot_nt(dp, w_ref[...]).astype(BF16)

    def tok(n):
        return pl.BlockSpec((tm, n), lambda i: (i, 0))

    def full(r, c):
        return pl.BlockSpec((r, c), lambda i: (0, 0))

    return pl.pallas_call(
        body, name="merge_bwd", grid=(T // tm,),
        in_specs=[tok(D), tok(W), tok(W), tok(2 * D), full(W, D), full(W, D), full(D, D)],
        out_specs=[tok(W), tok(W), tok(D), tok(D), tok(2 * D)],
        out_shape=[jax.ShapeDtypeStruct((T, W), BF16), jax.ShapeDtypeStruct((T, W), BF16),
                   jax.ShapeDtypeStruct((T, D), BF16), jax.ShapeDtypeStruct((T, D), BF16),
                   jax.ShapeDtypeStruct((T, 2 * D), BF16)],
        compiler_params=_params(1),
    )(dh, ya, yb, gates, wa, wb, wout)


def _final_loss(h, g, target, *, tm=512):
    T, D = h.shape

    def body(h_ref, g_ref, t_ref, dh_ref, loss_ref, dg_ref):
        @pl.when(pl.program_id(0) == 0)
        def _():
            loss_ref[...] = jnp.zeros_like(loss_ref)
            dg_ref[...] = jnp.zeros_like(dg_ref)

        hf = h_ref[...]
        r = _rstd(hf)
        gv = g_ref[...]
        err = (hf * r) * gv - t_ref[...]
        loss_ref[...] += jnp.broadcast_to(0.5 * jnp.sum(jnp.mean(err * err, axis=-1, keepdims=True)), loss_ref.shape)
        dx, dgr = _norm_bwd(err * (1.0 / D), hf, gv, r)
        dg_ref[...] += jnp.sum(dgr, axis=0, keepdims=True)
        dh_ref[...] = dx

    tok = pl.BlockSpec((tm, D), lambda i: (i, 0))
    vec = pl.BlockSpec((1, D), lambda i: (0, 0))
    return pl.pallas_call(
        body, name="final_loss", grid=(T // tm,),
        in_specs=[tok, vec, tok],
        out_specs=[tok, pl.BlockSpec((1, LANES), lambda i: (0, 0)), vec],
        out_shape=[jax.ShapeDtypeStruct((T, D), F32), jax.ShapeDtypeStruct((1, LANES), F32),
                   jax.ShapeDtypeStruct((1, D), F32)],
        compiler_params=_params(1),
    )(h, g, target)


def _pair_heads(a, axis):
    shp = a.shape
    a = a.reshape(shp[:axis] + (2, NB_HEADS // 2, HEAD_DIM) + shp[axis + 1:])
    return jnp.swapaxes(a, axis, axis + 1).reshape(shp)


def _unpair_heads(a, axis):
    shp = a.shape
    a = a.reshape(shp[:axis] + (NB_HEADS // 2, 2, HEAD_DIM) + shp[axis + 1:])
    return jnp.swapaxes(a, axis, axis + 1).reshape(shp)


FFN1 = ("ffn1_w_gate", "ffn1_w_up", "ffn1_w_down")
FFN2 = ("ffn2_w_gate", "ffn2_w_up", "ffn2_w_down")
MIXER = ("w_in", "w_branch_a", "w_branch_b", "w_out")


def _layer_grads(x, target, g1, f1, gmix, late, rpb, sink, g2, gfin, comm=None):
    T = x.shape[0]
    tables = _rope_tables(T)
    bias = _na_bias_slabs(rpb)

    h1, n1, hdn1, p1, q1, gathered = _ffn_fwd(x, g1, *f1, name="ffn1_fwd", rider=comm.late_rider if comm else None)
    w_in_t, wa, wb, wout, f2 = comm.late(gathered) if comm else late
    w_in_p = jnp.concatenate([w_in_t[:O_QB], _pair_heads(w_in_t[O_QB:O_KB], 0), w_in_t[O_KB:]], axis=0)
    wb_p = _pair_heads(wb, 0)
    u, qkva, qb, kvb, gates = _mix_in_fwd(h1, gmix, w_in_p, tables)
    ya = _na_fwd(qkva, bias)
    yb = _swa_fwd(qb, kvb, sink)
    h2, merged = _merge_fwd(ya, yb, gates, wa, wb_p, wout, h1)
    h3, n2, hdn2, p2, q2, _ = _ffn_fwd(h2, g2, *f2, name="ffn2_fwd")
    dh3, loss, dgfin = _final_loss(h3, gfin, target)

    dh2, da2, db2, dg2, _ = _ffn_bwd(dh3, h2, g2, p2, q2, *f2, name="ffn2_bwd")
    df2 = [_wgrad_shard_a(da2, n2, name="ffn2_dwg"), _wgrad_shard_a(db2, n2, name="ffn2_dwu"),
           _wgrad_shard_a(hdn2, dh3, scale=0.5, name="ffn2_dwd")]
    red2 = comm.reduce(FFN2, df2, tag="ffn2") if comm else None
    dya, dyb, dpa, dpb, dgates = _merge_bwd(dh2, ya, yb, gates, wa, wb_p, wout)
    dwout = _wgrad_cols(merged, dh2, 1, name="dwout").reshape(N_CHIPS, D_MODEL // N_CHIPS, D_MODEL)
    dwa = _wgrad_cols(ya, dpa, N_CHIPS, name="dwa")
    dwb = _unpair_heads(_wgrad_cols(yb, dpb, N_CHIPS, name="dwb"), 1)
    dqa, dka, dva, dbias, got2 = _na_bwd(qkva, dya, bias, rider=red2.rider if comm else None)
    drpb = _rpb_fold(_na_unstack_slabs(dbias))
    dqb, dkb, dvb, dsink = _swa_bwd(qb, kvb, dyb, sink)
    dz, dh1, dgmix = _mix_in_bwd(dqa, dka, dva, dqb, dkb, dvb, dgates, h1, gmix, dh2, w_in_p, tables)
    dwin_p = _wgrad_rows(dz, u, 2, name="dwin").reshape(D_IN, D_MODEL)
    dwin = jnp.concatenate([dwin_p[:O_QB], _unpair_heads(dwin_p[O_QB:O_KB], 0), dwin_p[O_KB:]], axis=0)
    dmix = [dwin.reshape(N_CHIPS, D_IN // N_CHIPS, D_MODEL), dwa, dwb, dwout]
    redm = comm.reduce(MIXER, dmix, tag="mixer") if comm else None
    dx, da1, db1, dg1, gotm = _ffn_bwd(dh1, x, g1, p1, q1, *f1, name="ffn1_bwd", rider=redm.rider if comm else None)
    df1 = [_wgrad_shard_a(da1, n1, name="ffn1_dwg"), _wgrad_shard_a(db1, n1, name="ffn1_dwu"),
           _wgrad_shard_a(hdn1, dh1, scale=0.5, name="ffn1_dwd")]
    out = dict(loss=loss, dx=dx, ffn1_norm=dg1, mix_norm=dgmix, ffn2_norm=dg2, final_norm=dgfin, na_rpb=drpb,
               sink_logit=dsink[:, 0:2, 0].T.reshape(NB_HEADS))
    if comm:
        red1 = comm.reduce(FFN1, df1, tag="ffn1")
        got1 = _run_rider(red1.rider, name="rs_chips_ffn1")
        out.update(red2.finish(got2))
        out.update(redm.finish(gotm))
        out.update(red1.finish(got1))
    else:
        out.update(zip(FFN1 + MIXER + FFN2, df1 + dmix + df2))
    return out


ANY = pl.BlockSpec(memory_space=pl.ANY)


def _place():
    x, y, c = lax.axis_index("x"), lax.axis_index("y"), lax.axis_index("c")
    chips = [(1 - x, y), (x, 1 - y), (1 - x, 1 - y)]
    return x, y, c, 2 * x + y, chips


def _remote(src, dst, send_sems, recv_sems, k, device):
    return pltpu.make_async_remote_copy(src_ref=src, dst_ref=dst, send_sem=send_sems.at[k], recv_sem=recv_sems.at[k],
                                        device_id=device, device_id_type=MESH)


class _Rider:
    def __init__(self, inputs, out_shape, scratch, start, middle, finish):
        self.inputs, self.out_shape, self.scratch = list(inputs), list(out_shape), list(scratch)
        self.start, self.middle, self.finish = start, middle, finish


def _run_rider(rider, *, name):
    n_in, n_out = len(rider.inputs), len(rider.out_shape)

    def body(*refs):
        ins, outs, sems = refs[:n_in], refs[n_in:n_in + n_out], refs[n_in + n_out:]
        rider.start(ins, outs, sems)
        if rider.middle is not None:
            rider.middle(ins, outs, sems)
        rider.finish(ins, outs, sems)

    return pl.pallas_call(body, name=name, in_specs=[ANY] * n_in, out_specs=[ANY] * n_out, out_shape=rider.out_shape,
                          scratch_shapes=rider.scratch)(*rider.inputs)


def _ride(body, n_in, n_out, rider, grid, middle_step):
    if rider is None:
        return body, [], [], [], [], []
    r_in, r_out = len(rider.inputs), len(rider.out_shape)
    steps = grid[0] * grid[1]

    def riding(*refs):
        ins, r_ins = refs[:n_in], refs[n_in:n_in + r_in]
        outs = refs[n_in + r_in:n_in + r_in + n_out]
        r_outs = refs[n_in + r_in + n_out:n_in + r_in + n_out + r_out]
        sems = refs[n_in + r_in + n_out + r_out:]
        step = pl.program_id(0) * grid[1] + pl.program_id(1)

        @pl.when(step == 0)
        def _():
            rider.start(r_ins, r_outs, sems)

        body(*ins, *outs)

        if rider.middle is not None:
            @pl.when(step == middle_step)
            def _():
                rider.middle(r_ins, r_outs, sems)

        @pl.when(step == steps - 1)
        def _():
            rider.finish(r_ins, r_outs, sems)

    return riding, rider.inputs, [ANY] * r_in, rider.out_shape, [ANY] * r_out, rider.scratch


def _gather_rider(shards):
    n = len(shards)

    def plan(ins, outs, sems):
        send_sems, recv_sems, own_send_sems, own_recv_sems = sems
        x, y, c, mine, chips = _place()
        sibling = (x, y, 1 - c)
        own, sends, landed, passes, others = [], [], [], [], []
        for i in range(n):
            hr = shards[i].shape[0] // 2
            own.append(_remote(ins[i], outs[i].at[mine], own_send_sems, own_recv_sems, i, sibling))
            for j, (cx, cy) in enumerate(chips):
                sends.append(_remote(ins[i].at[pl.ds(c * hr, hr)], outs[i].at[mine, pl.ds(c * hr, hr)], send_sems, recv_sems,
                                     6 * i + j, (cx, cy, c)))
                here = outs[i].at[2 * cx + cy, pl.ds(c * hr, hr)]
                landed.append(_remote(here, here, send_sems, recv_sems, 6 * i + j, (cx, cy, c)))
                passes.append(_remote(here, here, send_sems, recv_sems, 6 * i + 3 + j, sibling))
                there = outs[i].at[2 * cx + cy, pl.ds((1 - c) * hr, hr)]
                others.append(_remote(there, there, send_sems, recv_sems, 6 * i + 3 + j, sibling))
        return own, sends, landed, passes, others

    def start(ins, outs, sems):
        own, sends, _, _, _ = plan(ins, outs, sems)
        for cp in own + sends:
            cp.start()

    def middle(ins, outs, sems):
        _, _, landed, passes, _ = plan(ins, outs, sems)
        for arrived, cp in zip(landed, passes):
            arrived.wait_recv()
            cp.start()

    def finish(ins, outs, sems):
        own, sends, _, passes, others = plan(ins, outs, sems)
        for arrived in others:
            arrived.wait_recv()
        for cp in sends + passes:
            cp.wait_send()
        for cp in own:
            cp.wait()

    return _Rider(shards, [jax.ShapeDtypeStruct((N_CHIPS,) + s.shape, s.dtype) for s in shards],
                  [pltpu.SemaphoreType.DMA((6 * n,)), pltpu.SemaphoreType.DMA((6 * n,)),
                   pltpu.SemaphoreType.DMA((n,)), pltpu.SemaphoreType.DMA((n,))], start, middle, finish)


def _rs_sibling(grads, *, name):
    n = len(grads)

    def body(*refs):
        ins, outs = refs[:n], refs[n:2 * n]
        send_sems, recv_sems = refs[2 * n:]
        x, y, c, _, _ = _place()
        copies = []
        for i in range(n):
            hr = grads[i].shape[1] // 2
            cp = _remote(ins[i].at[:, pl.ds((1 - c) * hr, hr)], outs[i], send_sems, recv_sems, i, (x, y, 1 - c))
            cp.start()
            copies.append(cp)
        for cp in copies:
            cp.wait()

    return pl.pallas_call(
        body, name=name,
        in_specs=[ANY] * n, out_specs=[ANY] * n,
        out_shape=[jax.ShapeDtypeStruct((g.shape[0], g.shape[1] // 2, g.shape[2]), g.dtype) for g in grads],
        scratch_shapes=[pltpu.SemaphoreType.DMA((n,)), pltpu.SemaphoreType.DMA((n,))],
    )(*grads)


def _chips_rider(parts):
    n = len(parts)

    def plan(ins, outs, sems):
        send_sems, recv_sems = sems
        _, _, c, _, chips = _place()
        return [_remote(ins[i].at[2 * cx + cy], outs[i].at[j], send_sems, recv_sems, 3 * i + j, (cx, cy, c))
                for i in range(n) for j, (cx, cy) in enumerate(chips)]

    def start(ins, outs, sems):
        for cp in plan(ins, outs, sems):
            cp.start()

    def finish(ins, outs, sems):
        for cp in plan(ins, outs, sems):
            cp.wait()

    return _Rider(parts, [jax.ShapeDtypeStruct((N_CHIPS - 1,) + p.shape[1:], p.dtype) for p in parts],
                  [pltpu.SemaphoreType.DMA((3 * n,)), pltpu.SemaphoreType.DMA((3 * n,))], start, None, finish)


class _Reduce:
    def __init__(self, names, grads, cidx, chip, *, tag):
        self.names, self.chip, self.tag = names, chip, tag
        from_sibling = _rs_sibling(grads, name="rs_sibling_" + tag)
        self.parts = [_add_sibling(g, r, cidx, name="add_sibling_" + k) for k, g, r in zip(names, grads, from_sibling)]
        self.rider = _chips_rider([p16 for _, p16 in self.parts])

    def finish(self, from_chips):
        halves = [_add_chips(p, r, self.chip, name="add_chips_" + k) for k, (p, _), r in zip(self.names, self.parts, from_chips)]
        others = _rs_share(halves, name="rs_share_" + self.tag)
        return dict(zip(self.names, zip(halves, others)))


def _rs_share(halves, *, name):
    n = len(halves)

    def body(*refs):
        ins, outs = refs[:n], refs[n:2 * n]
        send_sems, recv_sems = refs[2 * n:]
        x, y, c, _, _ = _place()
        copies = []
        for i in range(n):
            cp = _remote(ins[i], outs[i], send_sems, recv_sems, i, (x, y, 1 - c))
            cp.start()
            copies.append(cp)
        for cp in copies:
            cp.wait()

    return pl.pallas_call(
        body, name=name,
        in_specs=[ANY] * n, out_specs=[ANY] * n,
        out_shape=[jax.ShapeDtypeStruct(h.shape, h.dtype) for h in halves],
        scratch_shapes=[pltpu.SemaphoreType.DMA((n,)), pltpu.SemaphoreType.DMA((n,))],
    )(*halves)


N_DEV = 8


def _small_allreduce(vec):
    R = vec.shape[0]

    def body(v_ref, o_ref, buf, send_sems, recv_sems):
        x, y, c, _, _ = _place()
        me = 4 * x + 2 * y + c
        buf[me] = v_ref[...]
        copies = []
        for k in range(1, N_DEV):
            peer = (x ^ (k >> 2), y ^ ((k >> 1) & 1), c ^ (k & 1))
            cp = _remote(v_ref, buf.at[me], send_sems, recv_sems, k - 1, peer)
            cp.start()
            copies.append(cp)
        for k, cp in enumerate(copies, start=1):
            cp.wait_send()
            landed = buf.at[me ^ k]
            _remote(landed, landed, send_sems, recv_sems, k - 1, (x, y, c)).wait_recv()
        acc = buf[0]
        for d in range(1, N_DEV):
            acc = acc + buf[d]
        o_ref[...] = acc

    return pl.pallas_call(
        body, name="small_allreduce",
        in_specs=[pl.BlockSpec(memory_space=pltpu.VMEM)], out_specs=pl.BlockSpec(memory_space=pltpu.VMEM),
        out_shape=jax.ShapeDtypeStruct(vec.shape, vec.dtype),
        scratch_shapes=[pltpu.VMEM((N_DEV, R, LANES), F32), pltpu.SemaphoreType.DMA((N_DEV - 1,)),
                        pltpu.SemaphoreType.DMA((N_DEV - 1,))],
    )(vec)


ELEMWISE_BLOCK = 256 * 1024


def _row_tile(rows, cols):
    best = None
    for t in range(8, rows + 1, 8):
        if rows % t == 0 and t * cols <= ELEMWISE_BLOCK:
            best = t
    return best if best is not None else rows


def _add_sibling(g, r1, cidx, *, name):
    S, R, C = g.shape
    hr = R // 2
    tr = _row_tile(hr, C)
    nt = hr // tr

    def body(c_ref, g_ref, r_ref, o_ref, o16_ref):
        p = g_ref[...] + r_ref[...]
        o_ref[...] = p
        o16_ref[...] = p.astype(BF16)

    blk = pl.BlockSpec((1, tr, C), lambda s, t, c: (s, t, 0))
    return pl.pallas_call(
        body, name=name,
        grid_spec=pltpu.PrefetchScalarGridSpec(
            num_scalar_prefetch=1, grid=(S, nt),
            in_specs=[pl.BlockSpec((1, tr, C), lambda s, t, c: (s, c[0] * nt + t, 0)), blk], out_specs=[blk, blk]),
        out_shape=[jax.ShapeDtypeStruct((S, hr, C), F32), jax.ShapeDtypeStruct((S, hr, C), BF16)],
        compiler_params=_params(2),
    )(cidx, g, r1)


def _add_chips(p, r2, chip, *, name):
    _, hr, C = p.shape
    tr = _row_tile(hr, C)

    def body(chip_ref, p_ref, r_ref, o_ref):
        o_ref[...] = ((p_ref[0] + r_ref[0].astype(F32)) + r_ref[1].astype(F32)) + r_ref[2].astype(F32)

    return pl.pallas_call(
        body, name=name,
        grid_spec=pltpu.PrefetchScalarGridSpec(
            num_scalar_prefetch=1, grid=(hr // tr,),
            in_specs=[pl.BlockSpec((1, tr, C), lambda t, s: (s[0], t, 0)), pl.BlockSpec((N_CHIPS - 1, tr, C), lambda t, s: (0, t, 0))],
            out_specs=pl.BlockSpec((tr, C), lambda t, s: (t, 0))),
        out_shape=jax.ShapeDtypeStruct((hr, C), F32),
        compiler_params=_params(1),
    )(chip, p, r2)


def _adamw_math(w, g, m, v):
    mn = ADAM_B1 * m + (1.0 - ADAM_B1) * g
    vn = ADAM_B2 * v + (1.0 - ADAM_B2) * (g * g)
    m_hat = mn / (1.0 - ADAM_B1 ** ADAM_STEP)
    v_hat = vn / (1.0 - ADAM_B2 ** ADAM_STEP)
    return -ADAM_LR * (m_hat / (jnp.sqrt(v_hat) + ADAM_EPS) + ADAM_WD * w), mn, vn


def _adamw_halves(w, mine, other, m, v, cidx, *, name):
    R, C = w.shape
    hr = R // 2
    tr = _row_tile(hr, C)
    nt = hr // tr

    def body(c_ref, w_ref, a_ref, b_ref, m_ref, v_ref, g_ref, d_ref, mo_ref, vo_ref):
        gv = jnp.where(pl.program_id(0) == c_ref[0], a_ref[...], b_ref[...])
        g_ref[...] = gv
        d_ref[...], mo_ref[...], vo_ref[...] = _adamw_math(w_ref[...], gv, m_ref[...], v_ref[...])

    full = pl.BlockSpec((tr, C), lambda h, t, c: (h * nt + t, 0))
    half = pl.BlockSpec((tr, C), lambda h, t, c: (t, 0))
    shape = jax.ShapeDtypeStruct((R, C), F32)
    return pl.pallas_call(
        body, name=name,
        grid_spec=pltpu.PrefetchScalarGridSpec(
            num_scalar_prefetch=1, grid=(2, nt), in_specs=[full, half, half, full, full], out_specs=[full] * 4),
        out_shape=[shape] * 4,
        compiler_params=_params(2),
    )(cidx, w, mine, other, m, v)


def _adamw(w, g, m, v, *, name):
    R, C = w.shape
    tr = _row_tile(R, C)

    def body(w_ref, g_ref, m_ref, v_ref, d_ref, mo_ref, vo_ref):
        d_ref[...], mo_ref[...], vo_ref[...] = _adamw_math(w_ref[...], g_ref[...], m_ref[...], v_ref[...])

    blk = pl.BlockSpec((tr, C), lambda t: (t, 0))
    shape = jax.ShapeDtypeStruct((R, C), F32)
    return pl.pallas_call(
        body, name=name, grid=(R // tr,),
        in_specs=[blk] * 4, out_specs=[blk] * 3, out_shape=[shape] * 3,
        compiler_params=_params(1),
    )(w, g, m, v)


def _unstack_cols(w):
    s, r, c = w.shape
    return w.transpose(1, 0, 2).reshape(r, s * c)


def _pad_rows(a, rows):
    return jnp.pad(a, ((0, rows - a.shape[0]), (0, LANES - a.shape[1])))


BIG = ("ffn1_w_gate", "ffn1_w_up", "ffn1_w_down", "w_in", "w_branch_a", "w_branch_b", "w_out",
       "ffn2_w_gate", "ffn2_w_up", "ffn2_w_down")
TRANSPOSED = ("ffn1_w_gate", "ffn1_w_up", "w_in", "ffn2_w_gate", "ffn2_w_up")
WEIGHTS = ("ffn1_norm", "ffn1_w_gate", "ffn1_w_up", "ffn1_w_down", "mix_norm", "w_in", "na_rpb", "sink_logit",
           "w_branch_a", "w_branch_b", "w_out", "ffn2_norm", "ffn2_w_gate", "ffn2_w_up", "ffn2_w_down", "final_norm")


def kernel(x, ffn1_norm, ffn1_w_gate, ffn1_w_up, ffn1_w_down, mix_norm, w_in, na_rpb, sink_logit, w_branch_a, w_branch_b, w_out, ffn2_norm, ffn2_w_gate, ffn2_w_up, ffn2_w_down, final_norm, loss_target, m_ffn1_norm, m_ffn1_w_gate, m_ffn1_w_up, m_ffn1_w_down, m_mix_norm, m_w_in, m_na_rpb, m_sink_logit, m_w_branch_a, m_w_branch_b, m_w_out, m_ffn2_norm, m_ffn2_w_gate, m_ffn2_w_up, m_ffn2_w_down, m_final_norm, v_ffn1_norm, v_ffn1_w_gate, v_ffn1_w_up, v_ffn1_w_down, v_mix_norm, v_w_in, v_na_rpb, v_sink_logit, v_w_branch_a, v_w_branch_b, v_w_out, v_ffn2_norm, v_ffn2_w_gate, v_ffn2_w_up, v_ffn2_w_down, v_final_norm):
    args = dict(locals())
    w = {k: args[k] for k in WEIGHTS}
    mom = {k: args["m_" + k] for k in WEIGHTS}
    var = {k: args["v_" + k] for k in WEIGHTS}
    cidx = lax.axis_index("c").astype(jnp.int32).reshape(1)
    chip = (2 * lax.axis_index("x") + lax.axis_index("y")).astype(jnp.int32).reshape(1)

    def shard(a, k):
        return jnp.swapaxes(a[0], 0, 1) if k in TRANSPOSED else a[0]

    def unshard(a, k):
        return (jnp.swapaxes(a, 0, 1) if k in TRANSPOSED else a)[None]

    def bf16_shards(names):
        return [shard(w[k], k).astype(BF16) for k in names]

    class comm:
        late_rider = _gather_rider(bf16_shards(MIXER + FFN2))

        @staticmethod
        def late(gathered):
            full = dict(zip(MIXER + FFN2, gathered))
            return (full["w_in"].reshape(D_IN, D_MODEL), _unstack_cols(full["w_branch_a"]), _unstack_cols(full["w_branch_b"]),
                    full["w_out"].reshape(D_MODEL, D_MODEL), tuple(full[k] for k in FFN2))

        @staticmethod
        def reduce(names, grads, *, tag):
            return _Reduce(names, grads, cidx, chip, tag=tag)

    f1 = _run_rider(_gather_rider(bf16_shards(FFN1)), name="all_gather_ffn1")
    out = _layer_grads(x[0], loss_target[0], ffn1_norm, f1, mix_norm, None, na_rpb[0], sink_logit[0], ffn2_norm,
                       final_norm.reshape(1, D_MODEL), comm=comm)
    mine = {k: out[k][0] for k in BIG}
    other = {k: out[k][1] for k in BIG}
    grad = {}

    rows = D_MODEL // LANES
    small = jnp.concatenate([
        out["ffn1_norm"].reshape(rows, LANES), out["mix_norm"].reshape(rows, LANES), out["ffn2_norm"].reshape(rows, LANES),
        out["final_norm"].reshape(rows, LANES), out["na_rpb"].reshape(-1, LANES),
        _pad_rows(out["sink_logit"].reshape(1, NB_HEADS), 8), _pad_rows(out["loss"], 8)], axis=0)
    total = _small_allreduce(small)
    n_rpb = NA_HEADS * 2 * NA_KH
    grad["ffn1_norm"] = total[0:rows].reshape(1, D_MODEL)
    grad["mix_norm"] = total[rows:2 * rows].reshape(1, D_MODEL)
    grad["ffn2_norm"] = total[2 * rows:3 * rows].reshape(1, D_MODEL)
    grad["final_norm"] = total[3 * rows:4 * rows].reshape(1, D_MODEL)
    grad["na_rpb"] = total[4 * rows:4 * rows + n_rpb].reshape(NA_HEADS, 2 * NA_KH, LANES)[:, :2 * NA_KH - 1, :2 * NA_KW - 1]
    grad["na_rpb"] = grad["na_rpb"].reshape(NA_HEADS, -1)
    grad["sink_logit"] = total[4 * rows + n_rpb:4 * rows + n_rpb + 1, 0:NB_HEADS]
    loss = total[4 * rows + n_rpb + 8, 0]

    deltas, new_m, new_v, grads_out = {}, {}, {}, {}
    for k in WEIGHTS:
        shape = w[k].shape
        if k in mine:
            res = _adamw_halves(shard(w[k], k), mine[k], other[k], shard(mom[k], k), shard(var[k], k), cidx, name="adamw_" + k)
            grads_out[k], deltas[k], new_m[k], new_v[k] = (unshard(a, k) for a in res)
        else:
            g2d = grad[k]
            d, mn, vn = _adamw(w[k].reshape(g2d.shape), g2d, mom[k].reshape(g2d.shape), var[k].reshape(g2d.shape), name="adamw_" + k)
            grads_out[k], deltas[k], new_m[k], new_v[k] = (a.reshape(shape) for a in (g2d, d, mn, vn))
    return (loss, out["dx"].reshape(x.shape), *[grads_out[k] for k in WEIGHTS], *[deltas[k] for k in WEIGHTS],
            *[new_m[k] for k in WEIGHTS], *[new_v[k] for k in WEIGHTS])
```

```python
import functools
import math

import jax
import jax.numpy as jnp
from jax import lax
from jax.experimental import pallas as pl
from jax.experimental.pallas import tpu as pltpu

F32 = jnp.float32
BF16 = jnp.bfloat16

D_MODEL = 1024
HEAD_DIM = 64
NA_HEADS = 8
NB_HEADS = 8
GRID_W = 64
NA_KH = 8
NA_KW = 16
WIN = 128
ROPE_THETA = 10000.0
EPS = 1e-6
N_CHIPS = 4
QK_SCALE = HEAD_DIM ** -0.5
NEG = -1e30
LANES = 128
VMEM_LIMIT = 56 * 1024 * 1024

C_QKVA = 3 * NA_HEADS * HEAD_DIM
C_QB = NB_HEADS * HEAD_DIM
C_KB = 2 * HEAD_DIM
C_ROPE = C_QB + C_KB
C_GATES = 2 * D_MODEL
D_IN = C_QKVA + C_QB + 2 * C_KB + C_GATES
O_QB = C_QKVA
O_KB = O_QB + C_QB
O_VB = O_KB + C_KB
O_G = O_VB + C_KB

ADAM_LR = 0.001
ADAM_B1 = 0.9
ADAM_B2 = 0.999
ADAM_EPS = 1e-08
ADAM_WD = 0.01
ADAM_STEP = 10

MESH = pl.DeviceIdType.MESH


def _dot(a, b):
    return jnp.dot(a, b, preferred_element_type=F32)


def _dot_nt(a, b):
    return lax.dot_general(a, b, (((1,), (1,)), ((), ())), preferred_element_type=F32)


def _dot_tn(a, b):
    return lax.dot_general(a, b, (((0,), (0,)), ((), ())), preferred_element_type=F32)


def _params(n_axes):
    return pltpu.CompilerParams(dimension_semantics=("arbitrary",) * n_axes, vmem_limit_bytes=VMEM_LIMIT)


def _rstd(xf):
    return lax.rsqrt(jnp.mean(xf * xf, axis=-1, keepdims=True) + EPS)


def _norm_bwd(dn, xf, g, r):
    xhat = xf * r
    dxh = dn * g
    dx = r * (dxh - xhat * jnp.mean(dxh * xhat, axis=-1, keepdims=True))
    return dx, dn * xhat


def _sigmoid(x):
    return 0.5 * jnp.tanh(0.5 * x) + 0.5


def _ffn_fwd(x, g, wg, wu, wd, *, name, tm=1024, sub=512, rider=None):
    T, D = x.shape
    F = wg.shape[1]

    def body(x_ref, g_ref, wg_ref, wu_ref, wd_ref, h_ref, n_ref, hdn_ref, p_ref, q_ref):
        s = pl.program_id(1)

        @pl.when(s == 0)
        def _():
            xf = x_ref[...]
            n_ref[...] = ((xf * _rstd(xf)) * g_ref[...]).astype(BF16)
            h_ref[...] = xf

        rows = [pl.ds(u * sub, sub) for u in range(tm // sub)]
        ab = [(_dot_nt(n_ref[r, :], wg_ref[0]), _dot_nt(n_ref[r, :], wu_ref[0])) for r in rows]
        hdns = []
        for r, (a, b) in zip(rows, ab):
            sg = _sigmoid(a)
            silu = a * sg
            hdn = (silu * b).astype(BF16)
            hdn_ref[0, r, :] = hdn
            p_ref[0, r, :] = (b * (sg + silu * (1.0 - sg))).astype(BF16)
            q_ref[0, r, :] = silu.astype(BF16)
            hdns.append(hdn)
        for r, hdn in zip(rows, hdns):
            h_ref[r, :] += 0.5 * _dot(hdn, wd_ref[0])

    tok = pl.BlockSpec((tm, D), lambda i, s: (i, 0))
    hid = pl.BlockSpec((1, tm, F), lambda i, s: (s, i, 0))
    wspec = pl.BlockSpec((1, F, D), lambda i, s: (s, 0, 0))
    hshape = jax.ShapeDtypeStruct((N_CHIPS, T, F), BF16)
    grid = (T // tm, N_CHIPS)
    body, r_in, r_in_specs, r_out, r_out_specs, scratch = _ride(body, 5, 5, rider, grid, (grid[0] * grid[1] * 7) // 8)
    outs = pl.pallas_call(
        body, name=name, grid=grid,
        in_specs=[tok, pl.BlockSpec((1, D), lambda i, s: (0, 0)), wspec, wspec, wspec] + r_in_specs,
        out_specs=[tok, tok, hid, hid, hid] + r_out_specs,
        out_shape=[jax.ShapeDtypeStruct((T, D), F32), jax.ShapeDtypeStruct((T, D), BF16), hshape, hshape, hshape] + r_out,
        scratch_shapes=scratch,
        compiler_params=_params(2),
    )(x, g, wg, wu, wd, *r_in)
    return (*outs[:5], list(outs[5:]))


def _ffn_bwd(dh, x, g, p, q, wg, wu, wd, *, name, tm=512, sub=256, rider=None):
    T, D = x.shape
    F = wg.shape[1]

    def body(dh_ref, x_ref, g_ref, p_ref, q_ref, wg_ref, wu_ref, wd_ref, dx_ref, da_ref, db_ref, dg_ref):
        i, s = pl.program_id(0), pl.program_id(1)

        @pl.when((i == 0) & (s == 0))
        def _():
            dg_ref[...] = jnp.zeros_like(dg_ref)

        @pl.when(s == 0)
        def _():
            dx_ref[...] = jnp.zeros_like(dx_ref)

        rows = [pl.ds(u * sub, sub) for u in range(tm // sub)]
        dhdn = [_dot_nt((0.5 * dh_ref[r, :]).astype(BF16), wd_ref[0]) for r in rows]
        das, dbs = [], []
        for r, dd in zip(rows, dhdn):
            da = (dd * p_ref[0, r, :].astype(F32)).astype(BF16)
            db = (dd * q_ref[0, r, :].astype(F32)).astype(BF16)
            da_ref[0, r, :] = da
            db_ref[0, r, :] = db
            das.append(da)
            dbs.append(db)
        for r, da, db in zip(rows, das, dbs):
            dx_ref[r, :] += _dot(da, wg_ref[0]) + _dot(db, wu_ref[0])

        @pl.when(s == N_CHIPS - 1)
        def _():
            xf = x_ref[...]
            dx, dgr = _norm_bwd(dx_ref[...], xf, g_ref[...], _rstd(xf))
            dg_ref[...] += jnp.sum(dgr, axis=0, keepdims=True)
            dx_ref[...] = dh_ref[...] + dx

    tok = pl.BlockSpec((tm, D), lambda i, s: (i, 0))
    hid = pl.BlockSpec((1, tm, F), lambda i, s: (s, i, 0))
    vec = pl.BlockSpec((1, D), lambda i, s: (0, 0))
    hshape = jax.ShapeDtypeStruct((N_CHIPS, T, F), BF16)
    wspec = pl.BlockSpec((1, F, D), lambda i, s: (s, 0, 0))
    grid = (T // tm, N_CHIPS)
    body, r_in, r_in_specs, r_out, r_out_specs, scratch = _ride(body, 8, 4, rider, grid, None)
    outs = pl.pallas_call(
        body, name=name, grid=grid,
        in_specs=[tok, tok, vec, hid, hid, wspec, wspec, wspec] + r_in_specs,
        out_specs=[tok, hid, hid, vec] + r_out_specs,
        out_shape=[jax.ShapeDtypeStruct((T, D), F32), hshape, hshape, jax.ShapeDtypeStruct((1, D), F32)] + r_out,
        scratch_shapes=scratch,
        compiler_params=_params(2),
    )(dh, x, g, p, q, wg, wu, wd, *r_in)
    return (*outs[:4], list(outs[4:]))


def _wgrad(a, b, *, a_block, a_map, b_block, b_map, out_shape, o_block, o_map, grid, scale=1.0, name, rider=None):
    def body(a_ref, b_ref, o_ref):
        @pl.when(pl.program_id(len(grid) - 1) == 0)
        def _():
            o_ref[...] = jnp.zeros_like(o_ref)

        av = a_ref[...]
        bv = b_ref[...]
        av = av.reshape(av.shape[-2:]).astype(BF16)
        bv = bv.reshape(bv.shape[-2:])
        if scale != 1.0:
            bv = scale * bv
        o_ref[...] += _dot_tn(av, bv.astype(BF16)).reshape(o_ref.shape)

    body, r_in, r_in_specs, r_out, r_out_specs, scratch = _ride(body, 2, 1, rider, grid, None)
    outs = pl.pallas_call(
        body, name=name, grid=grid,
        in_specs=[pl.BlockSpec(a_block, a_map), pl.BlockSpec(b_block, b_map)] + r_in_specs,
        out_specs=[pl.BlockSpec(o_block, o_map)] + r_out_specs,
        out_shape=[jax.ShapeDtypeStruct(out_shape, F32)] + r_out,
        scratch_shapes=scratch,
        compiler_params=_params(len(grid)),
    )(a, b, *r_in)
    return outs[0] if rider is None else (outs[0], list(outs[1:]))


def _wgrad_rows(a, b, n_blocks, *, name, tk=1024):
    T, N = b.shape
    M = a.shape[1] // n_blocks
    tk = min(tk, T)
    return _wgrad(a, b, a_block=(tk, M), a_map=lambda s, k: (k, s), b_block=(tk, N), b_map=lambda s, k: (k, 0),
                  out_shape=(n_blocks, M, N), o_block=(1, M, N), o_map=lambda s, k: (s, 0, 0), grid=(n_blocks, T // tk), name=name)


def _wgrad_shard_a(a, b, *, name, scale=1.0, tk=2048, rider=None):
    S, T, M = a.shape
    N = b.shape[1]
    tk = min(tk, T)
    return _wgrad(a, b, a_block=(1, tk, M), a_map=lambda s, k: (s, k, 0), b_block=(tk, N), b_map=lambda s, k: (k, 0),
                  out_shape=(S, M, N), o_block=(1, M, N), o_map=lambda s, k: (s, 0, 0), grid=(S, T // tk), scale=scale,
                  name=name, rider=rider)


def _wgrad_cols(a, b, n_blocks, *, name, tk=1024):
    T, M = a.shape
    N = b.shape[1] // n_blocks
    tk = min(tk, T)

    def body(a_ref, b_ref, o_ref):
        @pl.when(pl.program_id(0) == 0)
        def _():
            o_ref[...] = jnp.zeros_like(o_ref)

        r = _dot_tn(a_ref[...].astype(BF16), b_ref[...].astype(BF16))
        for s in range(n_blocks):
            o_ref[s] += r[:, s * N:(s + 1) * N]

    return pl.pallas_call(
        body, name=name, grid=(T // tk,),
        in_specs=[pl.BlockSpec((tk, M), lambda k: (k, 0)), pl.BlockSpec((tk, n_blocks * N), lambda k: (k, 0))],
        out_specs=pl.BlockSpec((n_blocks, M, N), lambda k: (0, 0, 0)),
        out_shape=jax.ShapeDtypeStruct((n_blocks, M, N), F32),
        compiler_params=_params(1),
    )(a, b)


def _rope_tables(T):
    half = HEAD_DIM // 2
    inv = ROPE_THETA ** (-jnp.arange(half, dtype=F32) / half)
    ang = jnp.arange(T, dtype=F32)[:, None] * inv[None, :]
    cos, sin, zero = jnp.cos(ang), jnp.sin(ang), jnp.zeros_like(ang)
    reps = LANES // HEAD_DIM
    return (jnp.tile(jnp.concatenate([cos, cos], axis=1), (1, reps)),
            jnp.tile(jnp.concatenate([-sin, zero], axis=1), (1, reps)),
            jnp.tile(jnp.concatenate([zero, sin], axis=1), (1, reps)))


def _rope(x, cos, sa, sb, sign):
    half = HEAD_DIM // 2
    return x * cos + sign * (pltpu.roll(x, LANES - half, 1) * sa + pltpu.roll(x, half, 1) * sb)


def _mix_in_fwd(h, g, w_in, tables, *, tm=256):
    T, D = h.shape

    def body(h_ref, g_ref, w_ref, cos_ref, sa_ref, sb_ref, u_ref, qkva_ref, qb_ref, kvb_ref, gates_ref):
        hf = h_ref[...]
        u = ((hf * _rstd(hf)) * g_ref[...]).astype(BF16)
        u_ref[...] = u
        qkva_ref[...] = _dot_nt(u, w_ref[0:C_QKVA, :]).astype(BF16)
        zr = _dot_nt(u, w_ref[O_QB:O_QB + C_ROPE, :])
        cos, sa, sb = cos_ref[...], sa_ref[...], sb_ref[...]
        for j in range(C_ROPE // LANES):
            rj = _rope(zr[:, j * LANES:(j + 1) * LANES], cos, sa, sb, 1.0).astype(BF16)
            if j < C_QB // LANES:
                qb_ref[:, j * LANES:(j + 1) * LANES] = rj
            else:
                kvb_ref[:, 0:C_KB] = rj
        kvb_ref[:, C_KB:2 * C_KB] = _dot_nt(u, w_ref[O_VB:O_VB + C_KB, :]).astype(BF16)
        gates_ref[...] = _dot_nt(u, w_ref[O_G:O_G + C_GATES, :])

    def tok(n):
        return pl.BlockSpec((tm, n), lambda i: (i, 0))

    return pl.pallas_call(
        body, name="mix_in_fwd", grid=(T // tm,),
        in_specs=[tok(D), pl.BlockSpec((1, D), lambda i: (0, 0)), pl.BlockSpec((D_IN, D), lambda i: (0, 0)),
                  tok(LANES), tok(LANES), tok(LANES)],
        out_specs=[tok(D), tok(C_QKVA), tok(C_QB), tok(2 * C_KB), tok(C_GATES)],
        out_shape=[jax.ShapeDtypeStruct((T, D), BF16), jax.ShapeDtypeStruct((T, C_QKVA), BF16),
                   jax.ShapeDtypeStruct((T, C_QB), BF16), jax.ShapeDtypeStruct((T, 2 * C_KB), BF16),
                   jax.ShapeDtypeStruct((T, C_GATES), F32)],
        compiler_params=_params(1),
    )(h, g, w_in, *tables)


def _mix_in_bwd(dqa, dka, dva, dqb, dkb, dvb, dgates, h, g, dres, w_in, tables, *, tm=256):
    T, D = h.shape

    def body(dqa_ref, dka_ref, dva_ref, dqb_ref, dkb_ref, dvb_ref, dgt_ref, h_ref, g_ref, dres_ref, w_ref,
             cos_ref, sa_ref, sb_ref, dz_ref, dh_ref, dg_ref):
        @pl.when(pl.program_id(0) == 0)
        def _():
            dg_ref[...] = jnp.zeros_like(dg_ref)

        na = NA_HEADS * HEAD_DIM
        dz_ref[:, 0:na] = dqa_ref[...].astype(BF16)
        dz_ref[:, na:2 * na] = dka_ref[...].astype(BF16)
        dz_ref[:, 2 * na:3 * na] = dva_ref[...].astype(BF16)
        cos, sa, sb = cos_ref[...], sa_ref[...], sb_ref[...]
        for j in range(C_QB // LANES):
            dz_ref[:, O_QB + j * LANES:O_QB + (j + 1) * LANES] = _rope(
                dqb_ref[:, j * LANES:(j + 1) * LANES], cos, sa, sb, -1.0).astype(BF16)
        dz_ref[:, O_KB:O_KB + C_KB] = _rope(dkb_ref[...], cos, sa, sb, -1.0).astype(BF16)
        dz_ref[:, O_VB:O_VB + C_KB] = dvb_ref[...].astype(BF16)
        dz_ref[:, O_G:O_G + C_GATES] = dgt_ref[...].astype(BF16)
        du = _dot(dz_ref[...], w_ref[...])
        hf = h_ref[...]
        dx, dgr = _norm_bwd(du, hf, g_ref[...], _rstd(hf))
        dg_ref[...] += jnp.sum(dgr, axis=0, keepdims=True)
        dh_ref[...] = dres_ref[...] + dx

    def tok(n):
        return pl.BlockSpec((tm, n), lambda i: (i, 0))

    vec = pl.BlockSpec((1, D), lambda i: (0, 0))
    na = NA_HEADS * HEAD_DIM
    return pl.pallas_call(
        body, name="mix_in_bwd", grid=(T // tm,),
        in_specs=[tok(na), tok(na), tok(na), tok(C_QB), tok(C_KB), tok(C_KB), tok(C_GATES), tok(D), vec, tok(D),
                  pl.BlockSpec((D_IN, D), lambda i: (0, 0)), tok(LANES), tok(LANES), tok(LANES)],
        out_specs=[tok(D_IN), tok(D), vec],
        out_shape=[jax.ShapeDtypeStruct((T, D_IN), BF16), jax.ShapeDtypeStruct((T, D), F32),
                   jax.ShapeDtypeStruct((1, D), F32)],
        compiler_params=_params(1),
    )(dqa, dka, dva, dqb, dkb, dvb, dgates, h, g, dres, w_in, *tables)


def _na_bias_slabs(rpb):
    H = rpb.shape[0]
    ncell = GRID_W * GRID_W
    cell = jnp.arange(ncell)
    co = cell % GRID_W - cell // GRID_W + (NA_KW - 1)
    e_co = (jnp.arange(LANES)[:, None] == co[None, :]).astype(F32)
    table = jnp.pad(rpb, ((0, 0), (0, 1), (0, LANES - rpb.shape[2]))).reshape(H * 2 * NA_KH, LANES)

    def body(t_ref, e_ref, o_ref):
        o_ref[...] = jnp.dot(t_ref[...], e_ref[...], preferred_element_type=F32, precision=lax.Precision.HIGHEST)

    toeplitz = pl.pallas_call(
        body, name="rpb_unfold", out_shape=jax.ShapeDtypeStruct((H * 2 * NA_KH, ncell), F32),
        compiler_params=_params(0),
    )(table, e_co).reshape(H, 2 * NA_KH, GRID_W, GRID_W)
    c = jnp.arange(GRID_W)
    cs = jnp.clip(c - NA_KW // 2, 0, GRID_W - NA_KW)
    inwin = (c[None, :] >= cs[:, None]) & (c[None, :] < cs[:, None] + NA_KW)
    toeplitz = jnp.where(inwin[None, None], toeplitz, NEG)
    slabs = jnp.stack([toeplitz[:, r:r + NA_KH] for r in range(NA_KH)], axis=1)
    slabs = slabs.reshape(H // 2, 2, NA_KH, NA_KH, GRID_W, GRID_W).transpose(0, 2, 1, 4, 3, 5)
    return slabs.reshape(H // 2, NA_KH, 2 * GRID_W, NA_KH * GRID_W)


def _na_unstack_slabs(dslab):
    pairs = dslab.shape[0]
    d = dslab.reshape(pairs, NA_KH, 2, GRID_W, NA_KH * GRID_W).transpose(0, 2, 1, 3, 4)
    return d.reshape(2 * pairs, NA_KH, GRID_W, NA_KH * GRID_W)


def _half_masks(rows):
    lane = lax.broadcasted_iota(jnp.int32, (rows, LANES), 1)
    left = lane < HEAD_DIM
    return left, (left, jnp.logical_not(left))


def _stack_heads(x):
    left, halves = _half_masks(x.shape[0])
    xf = x.astype(F32)
    return jnp.concatenate([jnp.where(m, xf, 0.0).astype(BF16) for m in halves], axis=0)


def _unstack_heads(o):
    rows = o.shape[0] // 2
    left, _ = _half_masks(rows)
    return jnp.where(left, o[:rows], o[rows:])


def _na_row(j, t, rb, rows):
    r = j * rb + t
    rs = jnp.clip(r - NA_KH // 2, 0, rows - NA_KH)
    return pl.multiple_of(t * GRID_W, GRID_W), pl.multiple_of(rs * GRID_W, GRID_W), rs - r + (NA_KH - 1)


def _na_specs(T, rb):
    qrows = GRID_W * rb
    pairs = NA_HEADS // 2
    return ([pl.BlockSpec((qrows, LANES), lambda p, j: (j, p)),
             pl.BlockSpec((T, LANES), lambda p, j: (0, pairs + p)),
             pl.BlockSpec((T, LANES), lambda p, j: (0, 2 * pairs + p))],
            pl.BlockSpec((1, NA_KH, 2 * GRID_W, NA_KH * GRID_W), lambda p, j: (p, 0, 0, 0)))


def _softmax(s):
    p = jnp.exp(s - jnp.max(s, axis=-1, keepdims=True))
    return p / jnp.sum(p, axis=-1, keepdims=True)


def _na_probs(qs, ks, bias):
    return _softmax(_dot_nt(qs, ks) * QK_SCALE + bias)


def _na_fwd(qkva, bias, *, rb=8, group=4):
    T = qkva.shape[0]
    rows = T // GRID_W
    nkeys = NA_KH * GRID_W

    def body(q_ref, k_ref, v_ref, bias_ref, y_ref):
        j = pl.program_id(1)

        def rows_step(t, carry):
            at = [_na_row(j, t * group + u, rb, rows) for u in range(group)]
            s = [_dot_nt(_stack_heads(q_ref[pl.ds(q0, GRID_W), :]), k_ref[pl.ds(k0, nkeys), :]) for q0, k0, _ in at]
            p = [_softmax(su * QK_SCALE + bias_ref[0, ro0]) for su, (_, _, ro0) in zip(s, at)]
            o = [_dot(pu.astype(BF16), v_ref[pl.ds(k0, nkeys), :]) for pu, (_, k0, _) in zip(p, at)]
            for ou, (q0, _, _) in zip(o, at):
                y_ref[pl.ds(q0, GRID_W), :] = _unstack_heads(ou).astype(BF16)
            return carry

        lax.fori_loop(0, rb // group, rows_step, 0)

    qkv_specs, bias_spec = _na_specs(T, rb)
    return pl.pallas_call(
        body, name="na_fwd", grid=(NA_HEADS // 2, rows // rb),
        in_specs=qkv_specs + [bias_spec],
        out_specs=qkv_specs[0],
        out_shape=jax.ShapeDtypeStruct((T, NA_HEADS * HEAD_DIM), BF16),
        compiler_params=_params(2),
    )(qkva, qkva, qkva, bias)


def _na_bwd(qkva, dy, bias, *, rb=8, group=4, rider=None):
    T = qkva.shape[0]
    rows = T // GRID_W
    nkeys = NA_KH * GRID_W

    def body(q_ref, k_ref, v_ref, dy_ref, bias_ref, dq_ref, dk_ref, dv_ref, dbias_ref):
        j = pl.program_id(1)

        @pl.when(j == 0)
        def _():
            dk_ref[...] = jnp.zeros_like(dk_ref)
            dv_ref[...] = jnp.zeros_like(dv_ref)
            dbias_ref[...] = jnp.zeros_like(dbias_ref)

        def rows_step(t, carry):
            at = [_na_row(j, t * group + u, rb, rows) for u in range(group)]
            qs = [_stack_heads(q_ref[pl.ds(q0, GRID_W), :]) for q0, _, _ in at]
            dys = [_stack_heads(dy_ref[pl.ds(q0, GRID_W), :]) for q0, _, _ in at]
            s = [_dot_nt(qu, k_ref[pl.ds(k0, nkeys), :]) for qu, (_, k0, _) in zip(qs, at)]
            dp = [_dot_nt(du, v_ref[pl.ds(k0, nkeys), :]) for du, (_, k0, _) in zip(dys, at)]
            p = [_softmax(su * QK_SCALE + bias_ref[0, ro0]) for su, (_, _, ro0) in zip(s, at)]
            ds = [pu * (du - jnp.sum(pu * du, axis=-1, keepdims=True)) for pu, du in zip(p, dp)]
            for u, (q0, k0, ro0) in enumerate(at):
                dbias_ref[0, ro0] += ds[u]
                dsb = ds[u].astype(BF16)
                dq_ref[pl.ds(q0, GRID_W), :] = (_unstack_heads(_dot(dsb, k_ref[pl.ds(k0, nkeys), :])) * QK_SCALE).astype(BF16)
                dk_ref[pl.ds(k0, nkeys), :] += _dot_tn(dsb, qs[u]) * QK_SCALE
                dv_ref[pl.ds(k0, nkeys), :] += _dot_tn(p[u].astype(BF16), dys[u])
            return carry

        lax.fori_loop(0, rb // group, rows_step, 0)

    qkv_specs, bias_spec = _na_specs(T, rb)
    width = NA_HEADS * HEAD_DIM
    kv_out = pl.BlockSpec((T, LANES), lambda p, j: (0, p))
    grid = (NA_HEADS // 2, rows // rb)
    body, r_in, r_in_specs, r_out, r_out_specs, scratch = _ride(body, 5, 4, rider, grid, None)
    outs = pl.pallas_call(
        body, name="na_bwd", grid=grid,
        in_specs=qkv_specs + [qkv_specs[0], bias_spec] + r_in_specs,
        out_specs=[qkv_specs[0], kv_out, kv_out, bias_spec] + r_out_specs,
        out_shape=[jax.ShapeDtypeStruct((T, width), BF16), jax.ShapeDtypeStruct((T, width), F32),
                   jax.ShapeDtypeStruct((T, width), F32), jax.ShapeDtypeStruct(bias.shape, F32)] + r_out,
        scratch_shapes=scratch,
        compiler_params=_params(2),
    )(qkva, qkva, qkva, dy, bias, *r_in)
    return (*outs[:4], list(outs[4:]))


def _rpb_fold(dslab):
    H = dslab.shape[0]
    nro = NA_KH * NA_KH
    ncell = GRID_W * GRID_W
    xs = dslab.reshape(H, NA_KH, GRID_W, NA_KH, GRID_W).transpose(0, 1, 3, 2, 4).reshape(H, nro, ncell)
    cell = jnp.arange(ncell)
    co = cell % GRID_W - cell // GRID_W + (NA_KW - 1)
    e_co = (co[:, None] == jnp.arange(LANES)[None, :]).astype(F32)
    pair = jnp.arange(nro)
    e_ro = ((pair // NA_KH + pair % NA_KH)[None, :] == jnp.arange(2 * NA_KH)[:, None]).astype(F32)

    def body(x_ref, eco_ref, ero_ref, o_ref):
        y = jnp.dot(x_ref[0], eco_ref[...], preferred_element_type=F32, precision=lax.Precision.HIGHEST)
        o_ref[0] = jnp.dot(ero_ref[...], y, preferred_element_type=F32, precision=lax.Precision.HIGHEST)

    return pl.pallas_call(
        body, name="rpb_fold", grid=(H,),
        in_specs=[pl.BlockSpec((1, nro, ncell), lambda h: (h, 0, 0)), pl.BlockSpec((ncell, LANES), lambda h: (0, 0)),
                  pl.BlockSpec((2 * NA_KH, nro), lambda h: (0, 0))],
        out_specs=pl.BlockSpec((1, 2 * NA_KH, LANES), lambda h: (h, 0, 0)),
        out_shape=jax.ShapeDtypeStruct((H, 2 * NA_KH, LANES), F32),
        compiler_params=_params(1),
    )(xs, e_co, e_ro)


SWA_KEYS = 3 * WIN


def _swa_block(j, t, qbn, T):
    blk = j * qbn + t
    start = jnp.clip((blk - 1) * WIN, 0, T - SWA_KEYS)
    row = lax.broadcasted_iota(jnp.int32, (2 * WIN, SWA_KEYS), 0)
    qpos = blk * WIN + jnp.where(row < WIN, row, row - WIN)
    kpos = start + lax.broadcasted_iota(jnp.int32, (2 * WIN, SWA_KEYS), 1)
    return pl.multiple_of(t * WIN, WIN), pl.multiple_of(start, WIN), jnp.abs(qpos - kpos) <= WIN


def _swa_sinks(sink_ref, p):
    row = lax.broadcasted_iota(jnp.int32, (2 * WIN, 1), 0)
    return jnp.where(row < WIN, sink_ref[p], sink_ref[p + NB_HEADS // 2])


def _swa_probs(s, mask, sink):
    s = jnp.where(mask, s * QK_SCALE, NEG)
    m = jnp.maximum(jnp.max(s, axis=-1, keepdims=True), sink)
    e = jnp.exp(s - m)
    esink = jnp.exp(sink - m)
    den = jnp.sum(e, axis=-1, keepdims=True) + esink
    return e / den, esink / den


def _swa_specs(T, qbn):
    return [pl.BlockSpec(memory_space=pltpu.SMEM),
            pl.BlockSpec((WIN * qbn, LANES), lambda p, j: (j, p)),
            pl.BlockSpec((T, LANES), lambda p, j: (0, 0)),
            pl.BlockSpec((T, LANES), lambda p, j: (0, 1))]


def _swa_fwd(qb, kvb, sink, *, qbn=4, group=4):
    T = qb.shape[0]
    pairs = NB_HEADS // 2

    def body(sink_ref, q_ref, k_ref, v_ref, y_ref):
        p, j = pl.program_id(0), pl.program_id(1)
        sinks = _swa_sinks(sink_ref, p)

        def blocks_step(t, carry):
            at = [_swa_block(j, t * group + u, qbn, T) for u in range(group)]
            s = [_dot_nt(_stack_heads(q_ref[pl.ds(q0, WIN), :]), k_ref[pl.ds(k0, SWA_KEYS), :]) for q0, k0, _ in at]
            pr = [_swa_probs(su, mask, sinks)[0] for su, (_, _, mask) in zip(s, at)]
            o = [_dot(pu.astype(BF16), v_ref[pl.ds(k0, SWA_KEYS), :]) for pu, (_, k0, _) in zip(pr, at)]
            for ou, (q0, _, _) in zip(o, at):
                y_ref[pl.ds(q0, WIN), :] = _unstack_heads(ou).astype(BF16)
            return carry

        lax.fori_loop(0, qbn // group, blocks_step, 0)

    specs = _swa_specs(T, qbn)
    return pl.pallas_call(
        body, name="swa_fwd", grid=(pairs, T // (WIN * qbn)),
        in_specs=specs, out_specs=specs[1],
        out_shape=jax.ShapeDtypeStruct((T, NB_HEADS * HEAD_DIM), BF16),
        compiler_params=_params(2),
    )(sink, qb, kvb, kvb)


def _swa_bwd(qb, kvb, dy, sink, *, qbn=4, group=4):
    T = qb.shape[0]
    pairs = NB_HEADS // 2

    def body(sink_ref, q_ref, k_ref, v_ref, dy_ref, dq_ref, dk_ref, dv_ref, dsink_ref):
        p, j = pl.program_id(0), pl.program_id(1)
        sinks = _swa_sinks(sink_ref, p)

        @pl.when((p == 0) & (j == 0))
        def _():
            dk_ref[...] = jnp.zeros_like(dk_ref)
            dv_ref[...] = jnp.zeros_like(dv_ref)

        @pl.when(j == 0)
        def _():
            dsink_ref[...] = jnp.zeros_like(dsink_ref)

        def blocks_step(t, carry):
            at = [_swa_block(j, t * group + u, qbn, T) for u in range(group)]
            qs = [_stack_heads(q_ref[pl.ds(q0, WIN), :]) for q0, _, _ in at]
            dys = [_stack_heads(dy_ref[pl.ds(q0, WIN), :]) for q0, _, _ in at]
            s = [_dot_nt(qu, k_ref[pl.ds(k0, SWA_KEYS), :]) for qu, (_, k0, _) in zip(qs, at)]
            dp = [_dot_nt(du, v_ref[pl.ds(k0, SWA_KEYS), :]) for du, (_, k0, _) in zip(dys, at)]
            probs = [_swa_probs(su, mask, sinks) for su, (_, _, mask) in zip(s, at)]
            for u, (q0, k0, _) in enumerate(at):
                pr, psink = probs[u]
                delta = jnp.sum(pr * dp[u], axis=-1, keepdims=True)
                dsb = (pr * (dp[u] - delta)).astype(BF16)
                dsk = psink * delta
                for hh in range(2):
                    dsink_ref[0, hh:hh + 1, :] += jnp.broadcast_to(-jnp.sum(dsk[hh * WIN:(hh + 1) * WIN]), (1, LANES))
                dq_ref[pl.ds(q0, WIN), :] = _unstack_heads(_dot(dsb, k_ref[pl.ds(k0, SWA_KEYS), :])) * QK_SCALE
                dk_ref[pl.ds(k0, SWA_KEYS), :] += _dot_tn(dsb, qs[u]) * QK_SCALE
                dv_ref[pl.ds(k0, SWA_KEYS), :] += _dot_tn(pr.astype(BF16), dys[u])
            return carry

        lax.fori_loop(0, qbn // group, blocks_step, 0)

    specs = _swa_specs(T, qbn)
    kv_out = pl.BlockSpec((T, LANES), lambda p, j: (0, 0))
    return pl.pallas_call(
        body, name="swa_bwd", grid=(pairs, T // (WIN * qbn)),
        in_specs=specs + [specs[1]],
        out_specs=[specs[1], kv_out, kv_out, pl.BlockSpec((1, 8, LANES), lambda p, j: (p, 0, 0))],
        out_shape=[jax.ShapeDtypeStruct((T, NB_HEADS * HEAD_DIM), F32), jax.ShapeDtypeStruct((T, LANES), F32),
                   jax.ShapeDtypeStruct((T, LANES), F32), jax.ShapeDtypeStruct((pairs, 8, LANES), F32)],
        compiler_params=_params(2),
    )(sink, qb, kvb, kvb, dy)


def _merge_fwd(ya, yb, gates, wa, wb, wout, h, *, tm=512):
    T, D = h.shape
    W = ya.shape[1]

    def body(ya_ref, yb_ref, gt_ref, wa_ref, wb_ref, wo_ref, h_ref, h2_ref, mg_ref):
        pa = _dot(ya_ref[...], wa_ref[...])
        pb = _dot(yb_ref[...], wb_ref[...])
        mg = (jax.nn.sigmoid(gt_ref[:, 0:D]) * pa + jax.nn.sigmoid(gt_ref[:, D:2 * D]) * pb).astype(BF16)
        mg_ref[...] = mg
        h2_ref[...] = h_ref[...] + _dot(mg, wo_ref[...])

    def tok(n):
        return pl.BlockSpec((tm, n), lambda i: (i, 0))

    def full(r, c):
        return pl.BlockSpec((r, c), lambda i: (0, 0))

    return pl.pallas_call(
        body, name="merge_fwd", grid=(T // tm,),
        in_specs=[tok(W), tok(W), tok(2 * D), full(W, D), full(W, D), full(D, D), tok(D)],
        out_specs=[tok(D), tok(D)],
        out_shape=[jax.ShapeDtypeStruct((T, D), F32), jax.ShapeDtypeStruct((T, D), BF16)],
        compiler_params=_params(1),
    )(ya, yb, gates, wa, wb, wout, h)


def _merge_bwd(dh, ya, yb, gates, wa, wb, wout, *, tm=512):
    T, D = dh.shape
    W = ya.shape[1]

    def body(dh_ref, ya_ref, yb_ref, gt_ref, wa_ref, wb_ref, wo_ref, dya_ref, dyb_ref, dpa_ref, dpb_ref, dgt_ref):
        dmg = _dot_nt(dh_ref[...].astype(BF16), wo_ref[...])
        for y_ref, w_ref, dy_ref, dp_ref, lo in ((ya_ref, wa_ref, dya_ref, dpa_ref, 0), (yb_ref, wb_ref, dyb_ref, dpb_ref, D)):
            sg = jax.nn.sigmoid(gt_ref[:, lo:lo + D])
            dp = (dmg * sg).astype(BF16)
            dp_ref[...] = dp
            dgt_ref[:, lo:lo + D] = (dmg * _dot(y_ref[...], w_ref[...]) * (sg * (1.0 - sg))).astype(BF16)
            dy_ref[...] = _dot_nt(dp, w_ref[...]).astype(BF16)

    def tok(n):
        return pl.BlockSpec((tm, n), lambda i: (i, 0))

    def full(r, c):
        return pl.BlockSpec((r, c), lambda i: (0, 0))

    return pl.pallas_call(
        body, name="merge_bwd", grid=(T // tm,),
        in_specs=[tok(D), tok(W), tok(W), tok(2 * D), full(W, D), full(W, D), full(D, D)],
        out_specs=[tok(W), tok(W), tok(D), tok(D), tok(2 * D)],
        out_shape=[jax.ShapeDtypeStruct((T, W), BF16), jax.ShapeDtypeStruct((T, W), BF16),
                   jax.ShapeDtypeStruct((T, D), BF16), jax.ShapeDtypeStruct((T, D), BF16),
                   jax.ShapeDtypeStruct((T, 2 * D), BF16)],
        compiler_params=_params(1),
    )(dh, ya, yb, gates, wa, wb, wout)


def _final_loss(h, g, target, *, tm=512):
    T, D = h.shape

    def body(h_ref, g_ref, t_ref, dh_ref, loss_ref, dg_ref):
        @pl.when(pl.program_id(0) == 0)
        def _():
            loss_ref[...] = jnp.zeros_like(loss_ref)
            dg_ref[...] = jnp.zeros_like(dg_ref)

        hf = h_ref[...]
        r = _rstd(hf)
        gv = g_ref[...]
        err = (hf * r) * gv - t_ref[...]
        loss_ref[...] += jnp.broadcast_to(0.5 * jnp.sum(jnp.mean(err * err, axis=-1, keepdims=True)), loss_ref.shape)
        dx, dgr = _norm_bwd(err * (1.0 / D), hf, gv, r)
        dg_ref[...] += jnp.sum(dgr, axis=0, keepdims=True)
        dh_ref[...] = dx

    tok = pl.BlockSpec((tm, D), lambda i: (i, 0))
    vec = pl.BlockSpec((1, D), lambda i: (0, 0))
    return pl.pallas_call(
        body, name="final_loss", grid=(T // tm,),
        in_specs=[tok, vec, tok],
        out_specs=[tok, pl.BlockSpec((1, LANES), lambda i: (0, 0)), vec],
        out_shape=[jax.ShapeDtypeStruct((T, D), F32), jax.ShapeDtypeStruct((1, LANES), F32),
                   jax.ShapeDtypeStruct((1, D), F32)],
        compiler_params=_params(1),
    )(h, g, target)


def _pair_heads(a, axis):
    shp = a.shape
    a = a.reshape(shp[:axis] + (2, NB_HEADS // 2, HEAD_DIM) + shp[axis + 1:])
    return jnp.swapaxes(a, axis, axis + 1).reshape(shp)


def _unpair_heads(a, axis):
    shp = a.shape
    a = a.reshape(shp[:axis] + (NB_HEADS // 2, 2, HEAD_DIM) + shp[axis + 1:])
    return jnp.swapaxes(a, axis, axis + 1).reshape(shp)


FFN1 = ("ffn1_w_gate", "ffn1_w_up", "ffn1_w_down")
FFN2 = ("ffn2_w_gate", "ffn2_w_up", "ffn2_w_down")
MIXER = ("w_in", "w_branch_a", "w_branch_b", "w_out")


def _layer_grads(x, target, g1, f1, gmix, late, rpb, sink, g2, gfin, comm=None):
    T = x.shape[0]
    tables = _rope_tables(T)
    bias = _na_bias_slabs(rpb)

    h1, n1, hdn1, p1, q1, gathered = _ffn_fwd(x, g1, *f1, name="ffn1_fwd", rider=comm.late_rider if comm else None)
    w_in_t, wa, wb, wout, f2 = comm.late(gathered) if comm else late
    w_in_p = jnp.concatenate([w_in_t[:O_QB], _pair_heads(w_in_t[O_QB:O_KB], 0), w_in_t[O_KB:]], axis=0)
    wb_p = _pair_heads(wb, 0)
    u, qkva, qb, kvb, gates = _mix_in_fwd(h1, gmix, w_in_p, tables)
    ya = _na_fwd(qkva, bias)
    yb = _swa_fwd(qb, kvb, sink)
    h2, merged = _merge_fwd(ya, yb, gates, wa, wb_p, wout, h1)
    h3, n2, hdn2, p2, q2, _ = _ffn_fwd(h2, g2, *f2, name="ffn2_fwd")
    dh3, loss, dgfin = _final_loss(h3, gfin, target)

    dh2, da2, db2, dg2, _ = _ffn_bwd(dh3, h2, g2, p2, q2, *f2, name="ffn2_bwd")
    df2 = [_wgrad_shard_a(da2, n2, name="ffn2_dwg"), _wgrad_shard_a(db2, n2, name="ffn2_dwu"),
           _wgrad_shard_a(hdn2, dh3, scale=0.5, name="ffn2_dwd")]
    red2 = comm.reduce(FFN2, df2, tag="ffn2") if comm else None
    dya, dyb, dpa, dpb, dgates = _merge_bwd(dh2, ya, yb, gates, wa, wb_p, wout)
    dwout = _wgrad_cols(merged, dh2, 1, name="dwout").reshape(N_CHIPS, D_MODEL // N_CHIPS, D_MODEL)
    dwa = _wgrad_cols(ya, dpa, N_CHIPS, name="dwa")
    dwb = _unpair_heads(_wgrad_cols(yb, dpb, N_CHIPS, name="dwb"), 1)
    dqa, dka, dva, dbias, got2 = _na_bwd(qkva, dya, bias, rider=red2.rider if comm else None)
    drpb = _rpb_fold(_na_unstack_slabs(dbias))
    dqb, dkb, dvb, dsink = _swa_bwd(qb, kvb, dyb, sink)
    dz, dh1, dgmix = _mix_in_bwd(dqa, dka, dva, dqb, dkb, dvb, dgates, h1, gmix, dh2, w_in_p, tables)
    dwin_p = _wgrad_rows(dz, u, 2, name="dwin").reshape(D_IN, D_MODEL)
    dwin = jnp.concatenate([dwin_p[:O_QB], _unpair_heads(dwin_p[O_QB:O_KB], 0), dwin_p[O_KB:]], axis=0)
    dmix = [dwin.reshape(N_CHIPS, D_IN // N_CHIPS, D_MODEL), dwa, dwb, dwout]
    dx, da1, db1, dg1, _ = _ffn_bwd(dh1, x, g1, p1, q1, *f1, name="ffn1_bwd")
    out = dict(loss=loss, dx=dx, ffn1_norm=dg1, mix_norm=dgmix, ffn2_norm=dg2, final_norm=dgfin, na_rpb=drpb,
               sink_logit=dsink[:, 0:2, 0].T.reshape(NB_HEADS))
    if comm:
        out.update(red2.finish(got2))
        redm = comm.reduce(MIXER, dmix, tag="mixer")
        dwg1, gotm = _wgrad_shard_a(da1, n1, name="ffn1_dwg", rider=redm.rider)
        out.update(redm.finish(gotm))
        dwu1 = _wgrad_shard_a(db1, n1, name="ffn1_dwu")
        red1 = comm.reduce(FFN1[:2], [dwg1, dwu1], tag="ffn1_gate_up")
        dwd1, got1 = _wgrad_shard_a(hdn1, dh1, scale=0.5, name="ffn1_dwd", rider=red1.rider)
        out.update(red1.finish(got1))
        red1 = comm.reduce(FFN1[2:], [dwd1], tag="ffn1_down")
        out.update(red1.finish(_run_rider(red1.rider, name="rs_chips_ffn1_down")))
    else:
        df1 = [_wgrad_shard_a(da1, n1, name="ffn1_dwg"), _wgrad_shard_a(db1, n1, name="ffn1_dwu"),
               _wgrad_shard_a(hdn1, dh1, scale=0.5, name="ffn1_dwd")]
        out.update(zip(FFN1 + MIXER + FFN2, df1 + dmix + df2))
    return out


ANY = pl.BlockSpec(memory_space=pl.ANY)


def _place():
    x, y, c = lax.axis_index("x"), lax.axis_index("y"), lax.axis_index("c")
    chips = [(1 - x, y), (x, 1 - y), (1 - x, 1 - y)]
    return x, y, c, 2 * x + y, chips


def _remote(src, dst, send_sems, recv_sems, k, device):
    return pltpu.make_async_remote_copy(src_ref=src, dst_ref=dst, send_sem=send_sems.at[k], recv_sem=recv_sems.at[k],
                                        device_id=device, device_id_type=MESH)


class _Rider:
    def __init__(self, inputs, out_shape, scratch, start, middle, finish):
        self.inputs, self.out_shape, self.scratch = list(inputs), list(out_shape), list(scratch)
        self.start, self.middle, self.finish = start, middle, finish


def _run_rider(rider, *, name):
    n_in, n_out = len(rider.inputs), len(rider.out_shape)

    def body(*refs):
        ins, outs, sems = refs[:n_in], refs[n_in:n_in + n_out], refs[n_in + n_out:]
        rider.start(ins, outs, sems)
        if rider.middle is not None:
            rider.middle(ins, outs, sems)
        rider.finish(ins, outs, sems)

    return pl.pallas_call(body, name=name, in_specs=[ANY] * n_in, out_specs=[ANY] * n_out, out_shape=rider.out_shape,
                          scratch_shapes=rider.scratch)(*rider.inputs)


def _ride(body, n_in, n_out, rider, grid, middle_step):
    if rider is None:
        return body, [], [], [], [], []
    r_in, r_out = len(rider.inputs), len(rider.out_shape)
    steps = grid[0] * grid[1]

    def riding(*refs):
        ins, r_ins = refs[:n_in], refs[n_in:n_in + r_in]
        outs = refs[n_in + r_in:n_in + r_in + n_out]
        r_outs = refs[n_in + r_in + n_out:n_in + r_in + n_out + r_out]
        sems = refs[n_in + r_in + n_out + r_out:]
        step = pl.program_id(0) * grid[1] + pl.program_id(1)

        @pl.when(step == 0)
        def _():
            rider.start(r_ins, r_outs, sems)

        body(*ins, *outs)

        if rider.middle is not None:
            @pl.when(step == middle_step)
            def _():
                rider.middle(r_ins, r_outs, sems)

        @pl.when(step == steps - 1)
        def _():
            rider.finish(r_ins, r_outs, sems)

    return riding, rider.inputs, [ANY] * r_in, rider.out_shape, [ANY] * r_out, rider.scratch


def _gather_rider(shards):
    n = len(shards)

    def plan(ins, outs, sems, kinds):
        send_sems, recv_sems, own_send_sems, own_recv_sems = sems
        x, y, c, mine, chips = _place()
        sibling = (x, y, 1 - c)
        made = {k: [] for k in kinds}
        for i in range(n):
            hr = shards[i].shape[0] // 2
            if "own" in made:
                made["own"].append(_remote(ins[i], outs[i].at[mine], own_send_sems, own_recv_sems, i, sibling))
            for j, (cx, cy) in enumerate(chips):
                here = outs[i].at[2 * cx + cy, pl.ds(c * hr, hr)]
                there = outs[i].at[2 * cx + cy, pl.ds((1 - c) * hr, hr)]
                if "sends" in made:
                    made["sends"].append(_remote(ins[i].at[pl.ds(c * hr, hr)], outs[i].at[mine, pl.ds(c * hr, hr)],
                                                 send_sems, recv_sems, 6 * i + j, (cx, cy, c)))
                if "landed" in made:
                    made["landed"].append(_remote(here, here, send_sems, recv_sems, 6 * i + j, (cx, cy, c)))
                if "passes" in made:
                    made["passes"].append(_remote(here, here, send_sems, recv_sems, 6 * i + 3 + j, sibling))
                if "others" in made:
                    made["others"].append(_remote(there, there, send_sems, recv_sems, 6 * i + 3 + j, sibling))
        return [made[k] for k in kinds]

    def start(ins, outs, sems):
        own, sends = plan(ins, outs, sems, ("own", "sends"))
        for cp in own + sends:
            cp.start()

    def middle(ins, outs, sems):
        landed, passes = plan(ins, outs, sems, ("landed", "passes"))
        for arrived, cp in zip(landed, passes):
            arrived.wait_recv()
            cp.start()

    def finish(ins, outs, sems):
        own, sends, passes, others = plan(ins, outs, sems, ("own", "sends", "passes", "others"))
        for arrived in others:
            arrived.wait_recv()
        for cp in sends + passes:
            cp.wait_send()
        for cp in own:
            cp.wait()

    return _Rider(shards, [jax.ShapeDtypeStruct((N_CHIPS,) + s.shape, s.dtype) for s in shards],
                  [pltpu.SemaphoreType.DMA((6 * n,)), pltpu.SemaphoreType.DMA((6 * n,)),
                   pltpu.SemaphoreType.DMA((n,)), pltpu.SemaphoreType.DMA((n,))], start, middle, finish)


def _rs_sibling(grads, *, name):
    n = len(grads)

    def body(*refs):
        ins, outs = refs[:n], refs[n:2 * n]
        send_sems, recv_sems = refs[2 * n:]
        x, y, c, _, _ = _place()
        copies = []
        for i in range(n):
            hr = grads[i].shape[1] // 2
            cp = _remote(ins[i].at[:, pl.ds((1 - c) * hr, hr)], outs[i], send_sems, recv_sems, i, (x, y, 1 - c))
            cp.start()
            copies.append(cp)
        for cp in copies:
            cp.wait()

    return pl.pallas_call(
        body, name=name,
        in_specs=[ANY] * n, out_specs=[ANY] * n,
        out_shape=[jax.ShapeDtypeStruct((g.shape[0], g.shape[1] // 2, g.shape[2]), g.dtype) for g in grads],
        scratch_shapes=[pltpu.SemaphoreType.DMA((n,)), pltpu.SemaphoreType.DMA((n,))],
    )(*grads)


def _chips_rider(parts):
    n = len(parts)

    def plan(ins, outs, sems):
        send_sems, recv_sems = sems
        _, _, c, _, chips = _place()
        return [_remote(ins[i].at[2 * cx + cy], outs[i].at[j], send_sems, recv_sems, 3 * i + j, (cx, cy, c))
                for i in range(n) for j, (cx, cy) in enumerate(chips)]

    def start(ins, outs, sems):
        for cp in plan(ins, outs, sems):
            cp.start()

    def finish(ins, outs, sems):
        for cp in plan(ins, outs, sems):
            cp.wait()

    return _Rider(parts, [jax.ShapeDtypeStruct((N_CHIPS - 1,) + p.shape[1:], p.dtype) for p in parts],
                  [pltpu.SemaphoreType.DMA((3 * n,)), pltpu.SemaphoreType.DMA((3 * n,))], start, None, finish)


class _Reduce:
    def __init__(self, names, grads, cidx, chip, *, tag):
        self.names, self.chip, self.tag = names, chip, tag
        from_sibling = _rs_sibling(grads, name="rs_sibling_" + tag)
        self.parts = [_add_sibling(g, r, cidx, name="add_sibling_" + k) for k, g, r in zip(names, grads, from_sibling)]
        self.rider = _chips_rider([p16 for _, p16 in self.parts])

    def finish(self, from_chips):
        halves = [_add_chips(p, r, self.chip, name="add_chips_" + k) for k, (p, _), r in zip(self.names, self.parts, from_chips)]
        others = _rs_share(halves, name="rs_share_" + self.tag)
        return dict(zip(self.names, zip(halves, others)))


def _rs_share(halves, *, name):
    n = len(halves)

    def body(*refs):
        ins, outs = refs[:n], refs[n:2 * n]
        send_sems, recv_sems = refs[2 * n:]
        x, y, c, _, _ = _place()
        copies = []
        for i in range(n):
            cp = _remote(ins[i], outs[i], send_sems, recv_sems, i, (x, y, 1 - c))
            cp.start()
            copies.append(cp)
        for cp in copies:
            cp.wait()

    return pl.pallas_call(
        body, name=name,
        in_specs=[ANY] * n, out_specs=[ANY] * n,
        out_shape=[jax.ShapeDtypeStruct(h.shape, h.dtype) for h in halves],
        scratch_shapes=[pltpu.SemaphoreType.DMA((n,)), pltpu.SemaphoreType.DMA((n,))],
    )(*halves)


N_DEV = 8


def _small_allreduce(vec):
    R = vec.shape[0]

    def body(v_ref, o_ref, buf, send_sems, recv_sems):
        x, y, c, _, _ = _place()
        me = 4 * x + 2 * y + c
        buf[me] = v_ref[...]
        copies = []
        for k in range(1, N_DEV):
            peer = (x ^ (k >> 2), y ^ ((k >> 1) & 1), c ^ (k & 1))
            cp = _remote(v_ref, buf.at[me], send_sems, recv_sems, k - 1, peer)
            cp.start()
            copies.append(cp)
        for k, cp in enumerate(copies, start=1):
            cp.wait_send()
            landed = buf.at[me ^ k]
            _remote(landed, landed, send_sems, recv_sems, k - 1, (x, y, c)).wait_recv()
        acc = buf[0]
        for d in range(1, N_DEV):
            acc = acc + buf[d]
        o_ref[...] = acc

    return pl.pallas_call(
        body, name="small_allreduce",
        in_specs=[pl.BlockSpec(memory_space=pltpu.VMEM)], out_specs=pl.BlockSpec(memory_space=pltpu.VMEM),
        out_shape=jax.ShapeDtypeStruct(vec.shape, vec.dtype),
        scratch_shapes=[pltpu.VMEM((N_DEV, R, LANES), F32), pltpu.SemaphoreType.DMA((N_DEV - 1,)),
                        pltpu.SemaphoreType.DMA((N_DEV - 1,))],
    )(vec)


ELEMWISE_BLOCK = 256 * 1024


def _row_tile(rows, cols):
    best = None
    for t in range(8, rows + 1, 8):
        if rows % t == 0 and t * cols <= ELEMWISE_BLOCK:
            best = t
    return best if best is not None else rows


def _add_sibling(g, r1, cidx, *, name):
    S, R, C = g.shape
    hr = R // 2
    tr = _row_tile(hr, C)
    nt = hr // tr

    def body(c_ref, g_ref, r_ref, o_ref, o16_ref):
        p = g_ref[...] + r_ref[...]
        o_ref[...] = p
        o16_ref[...] = p.astype(BF16)

    blk = pl.BlockSpec((1, tr, C), lambda s, t, c: (s, t, 0))
    return pl.pallas_call(
        body, name=name,
        grid_spec=pltpu.PrefetchScalarGridSpec(
            num_scalar_prefetch=1, grid=(S, nt),
            in_specs=[pl.BlockSpec((1, tr, C), lambda s, t, c: (s, c[0] * nt + t, 0)), blk], out_specs=[blk, blk]),
        out_shape=[jax.ShapeDtypeStruct((S, hr, C), F32), jax.ShapeDtypeStruct((S, hr, C), BF16)],
        compiler_params=_params(2),
    )(cidx, g, r1)


def _add_chips(p, r2, chip, *, name):
    _, hr, C = p.shape
    tr = _row_tile(hr, C)

    def body(chip_ref, p_ref, r_ref, o_ref):
        o_ref[...] = ((p_ref[0] + r_ref[0].astype(F32)) + r_ref[1].astype(F32)) + r_ref[2].astype(F32)

    return pl.pallas_call(
        body, name=name,
        grid_spec=pltpu.PrefetchScalarGridSpec(
            num_scalar_prefetch=1, grid=(hr // tr,),
            in_specs=[pl.BlockSpec((1, tr, C), lambda t, s: (s[0], t, 0)), pl.BlockSpec((N_CHIPS - 1, tr, C), lambda t, s: (0, t, 0))],
            out_specs=pl.BlockSpec((tr, C), lambda t, s: (t, 0))),
        out_shape=jax.ShapeDtypeStruct((hr, C), F32),
        compiler_params=_params(1),
    )(chip, p, r2)


def _adamw_math(w, g, m, v):
    mn = ADAM_B1 * m + (1.0 - ADAM_B1) * g
    vn = ADAM_B2 * v + (1.0 - ADAM_B2) * (g * g)
    m_hat = mn / (1.0 - ADAM_B1 ** ADAM_STEP)
    v_hat = vn / (1.0 - ADAM_B2 ** ADAM_STEP)
    return -ADAM_LR * (m_hat / (jnp.sqrt(v_hat) + ADAM_EPS) + ADAM_WD * w), mn, vn


def _adamw_halves(w, mine, other, m, v, cidx, *, name):
    R, C = w.shape
    hr = R // 2
    tr = _row_tile(hr, C)
    nt = hr // tr

    def body(c_ref, w_ref, a_ref, b_ref, m_ref, v_ref, g_ref, d_ref, mo_ref, vo_ref):
        gv = jnp.where(pl.program_id(0) == c_ref[0], a_ref[...], b_ref[...])
        g_ref[...] = gv
        d_ref[...], mo_ref[...], vo_ref[...] = _adamw_math(w_ref[...], gv, m_ref[...], v_ref[...])

    full = pl.BlockSpec((tr, C), lambda h, t, c: (h * nt + t, 0))
    half = pl.BlockSpec((tr, C), lambda h, t, c: (t, 0))
    shape = jax.ShapeDtypeStruct((R, C), F32)
    return pl.pallas_call(
        body, name=name,
        grid_spec=pltpu.PrefetchScalarGridSpec(
            num_scalar_prefetch=1, grid=(2, nt), in_specs=[full, half, half, full, full], out_specs=[full] * 4),
        out_shape=[shape] * 4,
        compiler_params=_params(2),
    )(cidx, w, mine, other, m, v)


def _adamw(w, g, m, v, *, name):
    R, C = w.shape
    tr = _row_tile(R, C)

    def body(w_ref, g_ref, m_ref, v_ref, d_ref, mo_ref, vo_ref):
        d_ref[...], mo_ref[...], vo_ref[...] = _adamw_math(w_ref[...], g_ref[...], m_ref[...], v_ref[...])

    blk = pl.BlockSpec((tr, C), lambda t: (t, 0))
    shape = jax.ShapeDtypeStruct((R, C), F32)
    return pl.pallas_call(
        body, name=name, grid=(R // tr,),
        in_specs=[blk] * 4, out_specs=[blk] * 3, out_shape=[shape] * 3,
        compiler_params=_params(1),
    )(w, g, m, v)


def _unstack_cols(w):
    s, r, c = w.shape
    return w.transpose(1, 0, 2).reshape(r, s * c)


def _pad_rows(a, rows):
    return jnp.pad(a, ((0, rows - a.shape[0]), (0, LANES - a.shape[1])))


BIG = ("ffn1_w_gate", "ffn1_w_up", "ffn1_w_down", "w_in", "w_branch_a", "w_branch_b", "w_out",
       "ffn2_w_gate", "ffn2_w_up", "ffn2_w_down")
TRANSPOSED = ("ffn1_w_gate", "ffn1_w_up", "w_in", "ffn2_w_gate", "ffn2_w_up")
WEIGHTS = ("ffn1_norm", "ffn1_w_gate", "ffn1_w_up", "ffn1_w_down", "mix_norm", "w_in", "na_rpb", "sink_logit",
           "w_branch_a", "w_branch_b", "w_out", "ffn2_norm", "ffn2_w_gate", "ffn2_w_up", "ffn2_w_down", "final_norm")


def kernel(x, ffn1_norm, ffn1_w_gate, ffn1_w_up, ffn1_w_down, mix_norm, w_in, na_rpb, sink_logit, w_branch_a, w_branch_b, w_out, ffn2_norm, ffn2_w_gate, ffn2_w_up, ffn2_w_down, final_norm, loss_target, m_ffn1_norm, m_ffn1_w_gate, m_ffn1_w_up, m_ffn1_w_down, m_mix_norm, m_w_in, m_na_rpb, m_sink_logit, m_w_branch_a, m_w_branch_b, m_w_out, m_ffn2_norm, m_ffn2_w_gate, m_ffn2_w_up, m_ffn2_w_down, m_final_norm, v_ffn1_norm, v_ffn1_w_gate, v_ffn1_w_up, v_ffn1_w_down, v_mix_norm, v_w_in, v_na_rpb, v_sink_logit, v_w_branch_a, v_w_branch_b, v_w_out, v_ffn2_norm, v_ffn2_w_gate, v_ffn2_w_up, v_ffn2_w_down, v_final_norm):
    args = dict(locals())
    w = {k: args[k] for k in WEIGHTS}
    mom = {k: args["m_" + k] for k in WEIGHTS}
    var = {k: args["v_" + k] for k in WEIGHTS}
    cidx = lax.axis_index("c").astype(jnp.int32).reshape(1)
    chip = (2 * lax.axis_index("x") + lax.axis_index("y")).astype(jnp.int32).reshape(1)

    def shard(a, k):
        return jnp.swapaxes(a[0], 0, 1) if k in TRANSPOSED else a[0]

    def unshard(a, k):
        return (jnp.swapaxes(a, 0, 1) if k in TRANSPOSED else a)[None]

    def bf16_shards(names):
        return [shard(w[k], k).astype(BF16) for k in names]

    class comm:
        late_rider = _gather_rider(bf16_shards(MIXER + FFN2))

        @staticmethod
        def late(gathered):
            full = dict(zip(MIXER + FFN2, gathered))
            return (full["w_in"].reshape(D_IN, D_MODEL), _unstack_cols(full["w_branch_a"]), _unstack_cols(full["w_branch_b"]),
                    full["w_out"].reshape(D_MODEL, D_MODEL), tuple(full[k] for k in FFN2))

        @staticmethod
        def reduce(names, grads, *, tag):
            return _Reduce(names, grads, cidx, chip, tag=tag)

    f1 = _run_rider(_gather_rider(bf16_shards(FFN1)), name="all_gather_ffn1")
    out = _layer_grads(x[0], loss_target[0], ffn1_norm, f1, mix_norm, None, na_rpb[0], sink_logit[0], ffn2_norm,
                       final_norm.reshape(1, D_MODEL), comm=comm)
    mine = {k: out[k][0] for k in BIG}
    other = {k: out[k][1] for k in BIG}
    grad = {}

    rows = D_MODEL // LANES
    small = jnp.concatenate([
        out["ffn1_norm"].reshape(rows, LANES), out["mix_norm"].reshape(rows, LANES), out["ffn2_norm"].reshape(rows, LANES),
        out["final_norm"].reshape(rows, LANES), out["na_rpb"].reshape(-1, LANES),
        _pad_rows(out["sink_logit"].reshape(1, NB_HEADS), 8), _pad_rows(out["loss"], 8)], axis=0)
    total = _small_allreduce(small)
    n_rpb = NA_HEADS * 2 * NA_KH
    grad["ffn1_norm"] = total[0:rows].reshape(1, D_MODEL)
    grad["mix_norm"] = total[rows:2 * rows].reshape(1, D_MODEL)
    grad["ffn2_norm"] = total[2 * rows:3 * rows].reshape(1, D_MODEL)
    grad["final_norm"] = total[3 * rows:4 * rows].reshape(1, D_MODEL)
    grad["na_rpb"] = total[4 * rows:4 * rows + n_rpb].reshape(NA_HEADS, 2 * NA_KH, LANES)[:, :2 * NA_KH - 1, :2 * NA_KW - 1]
    grad["na_rpb"] = grad["na_rpb"].reshape(NA_HEADS, -1)
    grad["sink_logit"] = total[4 * rows + n_rpb:4 * rows + n_rpb + 1, 0:NB_HEADS]
    loss = total[4 * rows + n_rpb + 8, 0]

    deltas, new_m, new_v, grads_out = {}, {}, {}, {}
    for k in WEIGHTS:
        shape = w[k].shape
        if k in mine:
            res = _adamw_halves(shard(w[k], k), mine[k], other[k], shard(mom[k], k), shard(var[k], k), cidx, name="adamw_" + k)
            grads_out[k], deltas[k], new_m[k], new_v[k] = (unshard(a, k) for a in res)
        else:
            g2d = grad[k]
            d, mn, vn = _adamw(w[k].reshape(g2d.shape), g2d, mom[k].reshape(g2d.shape), var[k].reshape(g2d.shape), name="adamw_" + k)
            grads_out[k], deltas[k], new_m[k], new_v[k] = (a.reshape(shape) for a in (g2d, d, mn, vn))
    return (loss, out["dx"].reshape(x.shape), *[grads_out[k] for k in WEIGHTS], *[deltas[k] for k in WEIGHTS],
            *[new_m[k] for k in WEIGHTS], *[new_v[k] for k in WEIGHTS])
```

```python
import functools
import math

import jax
import jax.numpy as jnp
import numpy as np
from jax import lax
from jax.experimental import pallas as pl
from jax.experimental.pallas import tpu as pltpu

F32 = jnp.float32
BF16 = jnp.bfloat16

D_MODEL = 1024
HEAD_DIM = 64
NA_HEADS = 8
NB_HEADS = 8
GRID_W = 64
NA_KH = 8
NA_KW = 16
WIN = 128
ROPE_THETA = 10000.0
EPS = 1e-6
N_CHIPS = 4
QK_SCALE = HEAD_DIM ** -0.5
NEG = -1e30
LANES = 128
VMEM_LIMIT = 56 * 1024 * 1024
HEAD_ROWS = 256

C_QKVA = 3 * NA_HEADS * HEAD_DIM
C_QB = NB_HEADS * HEAD_DIM
C_KB = 2 * HEAD_DIM
C_ROPE = C_QB + C_KB
C_GATES = 2 * D_MODEL
D_IN = C_QKVA + C_QB + 2 * C_KB + C_GATES
O_QB = C_QKVA
O_KB = O_QB + C_QB
O_VB = O_KB + C_KB
O_G = O_VB + C_KB

ADAM_LR = 0.001
ADAM_B1 = 0.9
ADAM_B2 = 0.999
ADAM_EPS = 1e-08
ADAM_WD = 0.01
ADAM_STEP = 10

MESH = pl.DeviceIdType.MESH


def _dot(a, b):
    return jnp.dot(a, b, preferred_element_type=F32)


def _dot_nt(a, b):
    return lax.dot_general(a, b, (((1,), (1,)), ((), ())), preferred_element_type=F32)


def _dot_tn(a, b):
    return lax.dot_general(a, b, (((0,), (0,)), ((), ())), preferred_element_type=F32)


def _params(n_axes):
    return pltpu.CompilerParams(dimension_semantics=("arbitrary",) * n_axes, vmem_limit_bytes=VMEM_LIMIT)


def _rstd(xf):
    return lax.rsqrt(jnp.mean(xf * xf, axis=-1, keepdims=True) + EPS)


def _norm_bwd(dn, xf, g, r):
    xhat = xf * r
    dxh = dn * g
    dx = r * (dxh - xhat * jnp.mean(dxh * xhat, axis=-1, keepdims=True))
    return dx, dn * xhat


def _sigmoid(x):
    return 0.5 * jnp.tanh(0.5 * x) + 0.5


def _loss_head(hf, gv, tgt):
    r = _rstd(hf)
    err = (hf * r) * gv - tgt
    dx, dgr = _norm_bwd(err * (1.0 / hf.shape[-1]), hf, gv, r)
    return 0.5 * jnp.mean(err * err, axis=-1, keepdims=True), dx, dgr


def _ffn_fwd(x, g, wg, wu, wd, *, name, tm=1024, sub=512, rider=None, head=None):
    T, D = x.shape
    F = wg.shape[1]
    n_head = 0 if head is None else 2

    def body(*refs):
        x_ref, g_ref, wg_ref, wu_ref, wd_ref = refs[:5]
        h_ref, n_ref, hdn_ref, p_ref, q_ref = refs[5 + n_head:10 + n_head]
        i, s = pl.program_id(0), pl.program_id(1)
        _ffn_fwd_step(x_ref, g_ref, wg_ref, wu_ref, wd_ref, h_ref, n_ref, hdn_ref, p_ref, q_ref, s)
        if head is not None:
            gf_ref, t_ref = refs[5:7]
            loss_ref, dgf_ref = refs[10 + n_head:]

            @pl.when((i == 0) & (s == 0))
            def _():
                loss_ref[...] = jnp.zeros_like(loss_ref)
                dgf_ref[...] = jnp.zeros_like(dgf_ref)

            @pl.when(s == N_CHIPS - 1)
            def _():
                for u in range(tm // HEAD_ROWS):
                    r = pl.ds(u * HEAD_ROWS, HEAD_ROWS)
                    terms, dh, dgr = _loss_head(h_ref[r, :], gf_ref[...], t_ref[r, :])
                    loss_ref[...] += jnp.broadcast_to(jnp.sum(terms), loss_ref.shape)
                    dgf_ref[...] += jnp.sum(dgr, axis=0, keepdims=True)
                    h_ref[r, :] = dh

    def _ffn_fwd_step(x_ref, g_ref, wg_ref, wu_ref, wd_ref, h_ref, n_ref, hdn_ref, p_ref, q_ref, s):

        @pl.when(s == 0)
        def _():
            xf = x_ref[...]
            n_ref[...] = ((xf * _rstd(xf)) * g_ref[...]).astype(BF16)
            h_ref[...] = xf

        rows = [pl.ds(u * sub, sub) for u in range(tm // sub)]
        ab = [(_dot_nt(n_ref[r, :], wg_ref[0]), _dot_nt(n_ref[r, :], wu_ref[0])) for r in rows]
        hdns = []
        for r, (a, b) in zip(rows, ab):
            sg = _sigmoid(a)
            silu = a * sg
            hdn = (silu * b).astype(BF16)
            hdn_ref[0, r, :] = hdn
            p_ref[0, r, :] = (b * (sg + silu * (1.0 - sg))).astype(BF16)
            q_ref[0, r, :] = silu.astype(BF16)
            hdns.append(hdn)
        for r, hdn in zip(rows, hdns):
            h_ref[r, :] += 0.5 * _dot(hdn, wd_ref[0])

    tok = pl.BlockSpec((tm, D), lambda i, s: (i, 0))
    hid = pl.BlockSpec((1, tm, F), lambda i, s: (s, i, 0))
    wspec = pl.BlockSpec((1, F, D), lambda i, s: (s, 0, 0))
    hshape = jax.ShapeDtypeStruct((N_CHIPS, T, F), BF16)
    grid = (T // tm, N_CHIPS)
    vec = pl.BlockSpec((1, D), lambda i, s: (0, 0))
    head_in, head_in_specs, head_out, head_out_specs = [], [], [], []
    if head is not None:
        head_in, head_in_specs = list(head), [vec, tok]
        head_out = [jax.ShapeDtypeStruct((1, LANES), F32), jax.ShapeDtypeStruct((1, D), F32)]
        head_out_specs = [pl.BlockSpec((1, LANES), lambda i, s: (0, 0)), vec]
    n_main = 5 + n_head
    body, r_in, r_in_specs, r_out, r_out_specs, scratch = _ride(body, n_main, n_main, rider, grid, (grid[0] * grid[1] * 7) // 8)
    outs = pl.pallas_call(
        body, name=name, grid=grid,
        in_specs=[tok, vec, wspec, wspec, wspec] + head_in_specs + r_in_specs,
        out_specs=[tok, tok, hid, hid, hid] + head_out_specs + r_out_specs,
        out_shape=[jax.ShapeDtypeStruct((T, D), F32), jax.ShapeDtypeStruct((T, D), BF16), hshape, hshape, hshape]
        + head_out + r_out,
        scratch_shapes=scratch,
        compiler_params=_params(2),
    )(x, g, wg, wu, wd, *head_in, *r_in)
    return (*outs[:n_main], list(outs[n_main:]))


def _ffn_bwd(dh, x, g, p, q, wg, wu, wd, *, name, tm=512, sub=256, rider=None):
    T, D = x.shape
    F = wg.shape[1]

    def body(dh_ref, x_ref, g_ref, p_ref, q_ref, wg_ref, wu_ref, wd_ref, dx_ref, da_ref, db_ref, dg_ref):
        i, s = pl.program_id(0), pl.program_id(1)

        @pl.when((i == 0) & (s == 0))
        def _():
            dg_ref[...] = jnp.zeros_like(dg_ref)

        @pl.when(s == 0)
        def _():
            dx_ref[...] = jnp.zeros_like(dx_ref)

        rows = [pl.ds(u * sub, sub) for u in range(tm // sub)]
        dhdn = [_dot_nt((0.5 * dh_ref[r, :]).astype(BF16), wd_ref[0]) for r in rows]
        das, dbs = [], []
        for r, dd in zip(rows, dhdn):
            da = (dd * p_ref[0, r, :].astype(F32)).astype(BF16)
            db = (dd * q_ref[0, r, :].astype(F32)).astype(BF16)
            da_ref[0, r, :] = da
            db_ref[0, r, :] = db
            das.append(da)
            dbs.append(db)
        for r, da, db in zip(rows, das, dbs):
            dx_ref[r, :] += _dot(da, wg_ref[0]) + _dot(db, wu_ref[0])

        @pl.when(s == N_CHIPS - 1)
        def _():
            xf = x_ref[...]
            dx, dgr = _norm_bwd(dx_ref[...], xf, g_ref[...], _rstd(xf))
            dg_ref[...] += jnp.sum(dgr, axis=0, keepdims=True)
            dx_ref[...] = dh_ref[...] + dx

    tok = pl.BlockSpec((tm, D), lambda i, s: (i, 0))
    hid = pl.BlockSpec((1, tm, F), lambda i, s: (s, i, 0))
    vec = pl.BlockSpec((1, D), lambda i, s: (0, 0))
    hshape = jax.ShapeDtypeStruct((N_CHIPS, T, F), BF16)
    wspec = pl.BlockSpec((1, F, D), lambda i, s: (s, 0, 0))
    grid = (T // tm, N_CHIPS)
    body, r_in, r_in_specs, r_out, r_out_specs, scratch = _ride(body, 8, 4, rider, grid, None)
    outs = pl.pallas_call(
        body, name=name, grid=grid,
        in_specs=[tok, tok, vec, hid, hid, wspec, wspec, wspec] + r_in_specs,
        out_specs=[tok, hid, hid, vec] + r_out_specs,
        out_shape=[jax.ShapeDtypeStruct((T, D), F32), hshape, hshape, jax.ShapeDtypeStruct((1, D), F32)] + r_out,
        scratch_shapes=scratch,
        compiler_params=_params(2),
    )(dh, x, g, p, q, wg, wu, wd, *r_in)
    return (*outs[:4], list(outs[4:]))


def _wgrad(a, b, *, a_block, a_map, b_block, b_map, out_shape, o_block, o_map, grid, scale=1.0, name, rider=None):
    def body(a_ref, b_ref, o_ref):
        @pl.when(pl.program_id(len(grid) - 1) == 0)
        def _():
            o_ref[...] = jnp.zeros_like(o_ref)

        av = a_ref[...]
        bv = b_ref[...]
        av = av.reshape(av.shape[-2:]).astype(BF16)
        bv = bv.reshape(bv.shape[-2:])
        if scale != 1.0:
            bv = scale * bv
        o_ref[...] += _dot_tn(av, bv.astype(BF16)).reshape(o_ref.shape)

    body, r_in, r_in_specs, r_out, r_out_specs, scratch = _ride(body, 2, 1, rider, grid, None)
    outs = pl.pallas_call(
        body, name=name, grid=grid,
        in_specs=[pl.BlockSpec(a_block, a_map), pl.BlockSpec(b_block, b_map)] + r_in_specs,
        out_specs=[pl.BlockSpec(o_block, o_map)] + r_out_specs,
        out_shape=[jax.ShapeDtypeStruct(out_shape, F32)] + r_out,
        scratch_shapes=scratch,
        compiler_params=_params(len(grid)),
    )(a, b, *r_in)
    return outs[0] if rider is None else (outs[0], list(outs[1:]))


def _wgrad_rows(a, b, n_blocks, *, name, tk=1024):
    T, N = b.shape
    M = a.shape[1] // n_blocks
    tk = min(tk, T)
    return _wgrad(a, b, a_block=(tk, M), a_map=lambda s, k: (k, s), b_block=(tk, N), b_map=lambda s, k: (k, 0),
                  out_shape=(n_blocks, M, N), o_block=(1, M, N), o_map=lambda s, k: (s, 0, 0), grid=(n_blocks, T // tk), name=name)


def _wgrad_shard_a(a, b, *, name, scale=1.0, tk=2048, rider=None):
    S, T, M = a.shape
    N = b.shape[1]
    tk = min(tk, T)
    return _wgrad(a, b, a_block=(1, tk, M), a_map=lambda s, k: (s, k, 0), b_block=(tk, N), b_map=lambda s, k: (k, 0),
                  out_shape=(S, M, N), o_block=(1, M, N), o_map=lambda s, k: (s, 0, 0), grid=(S, T // tk), scale=scale,
                  name=name, rider=rider)


def _wgrad_cols(a, b, n_blocks, *, name, tk=1024):
    T, M = a.shape
    N = b.shape[1] // n_blocks
    tk = min(tk, T)

    def body(a_ref, b_ref, o_ref):
        @pl.when(pl.program_id(0) == 0)
        def _():
            o_ref[...] = jnp.zeros_like(o_ref)

        r = _dot_tn(a_ref[...].astype(BF16), b_ref[...].astype(BF16))
        for s in range(n_blocks):
            o_ref[s] += r[:, s * N:(s + 1) * N]

    return pl.pallas_call(
        body, name=name, grid=(T // tk,),
        in_specs=[pl.BlockSpec((tk, M), lambda k: (k, 0)), pl.BlockSpec((tk, n_blocks * N), lambda k: (k, 0))],
        out_specs=pl.BlockSpec((n_blocks, M, N), lambda k: (0, 0, 0)),
        out_shape=jax.ShapeDtypeStruct((n_blocks, M, N), F32),
        compiler_params=_params(1),
    )(a, b)


def _rope_tables(T):
    half = HEAD_DIM // 2
    inv = np.float32(ROPE_THETA) ** (-np.arange(half, dtype=np.float32) / np.float32(half))
    ang = np.arange(T, dtype=np.float32)[:, None] * inv[None, :]
    cos, sin, zero = np.cos(ang), np.sin(ang), np.zeros_like(ang)
    reps = LANES // HEAD_DIM
    return (jnp.asarray(np.tile(np.concatenate([cos, cos], axis=1), (1, reps))),
            jnp.asarray(np.tile(np.concatenate([-sin, zero], axis=1), (1, reps))),
            jnp.asarray(np.tile(np.concatenate([zero, sin], axis=1), (1, reps))))


def _rope(x, cos, sa, sb, sign):
    half = HEAD_DIM // 2
    return x * cos + sign * (pltpu.roll(x, LANES - half, 1) * sa + pltpu.roll(x, half, 1) * sb)


def _mix_in_fwd(h, g, w_in, tables, *, tm=256):
    T, D = h.shape

    def body(h_ref, g_ref, w_ref, cos_ref, sa_ref, sb_ref, u_ref, qkva_ref, qb_ref, kvb_ref, gates_ref):
        hf = h_ref[...]
        u = ((hf * _rstd(hf)) * g_ref[...]).astype(BF16)
        u_ref[...] = u
        qkva_ref[...] = _dot_nt(u, w_ref[0:C_QKVA, :]).astype(BF16)
        zr = _dot_nt(u, w_ref[O_QB:O_QB + C_ROPE, :])
        cos, sa, sb = cos_ref[...], sa_ref[...], sb_ref[...]
        for j in range(C_ROPE // LANES):
            rj = _rope(zr[:, j * LANES:(j + 1) * LANES], cos, sa, sb, 1.0).astype(BF16)
            if j < C_QB // LANES:
                qb_ref[:, j * LANES:(j + 1) * LANES] = rj
            else:
                kvb_ref[:, 0:C_KB] = rj
        kvb_ref[:, C_KB:2 * C_KB] = _dot_nt(u, w_ref[O_VB:O_VB + C_KB, :]).astype(BF16)
        gates_ref[...] = _dot_nt(u, w_ref[O_G:O_G + C_GATES, :])

    def tok(n):
        return pl.BlockSpec((tm, n), lambda i: (i, 0))

    return pl.pallas_call(
        body, name="mix_in_fwd", grid=(T // tm,),
        in_specs=[tok(D), pl.BlockSpec((1, D), lambda i: (0, 0)), pl.BlockSpec((D_IN, D), lambda i: (0, 0)),
                  tok(LANES), tok(LANES), tok(LANES)],
        out_specs=[tok(D), tok(C_QKVA), tok(C_QB), tok(2 * C_KB), tok(C_GATES)],
        out_shape=[jax.ShapeDtypeStruct((T, D), BF16), jax.ShapeDtypeStruct((T, C_QKVA), BF16),
                   jax.ShapeDtypeStruct((T, C_QB), BF16), jax.ShapeDtypeStruct((T, 2 * C_KB), BF16),
                   jax.ShapeDtypeStruct((T, C_GATES), F32)],
        compiler_params=_params(1),
    )(h, g, w_in, *tables)


def _mix_in_bwd(dqa, dka, dva, dqb, dkb, dvb, dgates, h, g, dres, w_in, tables, *, tm=256):
    T, D = h.shape

    def body(dqa_ref, dka_ref, dva_ref, dqb_ref, dkb_ref, dvb_ref, dgt_ref, h_ref, g_ref, dres_ref, w_ref,
             cos_ref, sa_ref, sb_ref, dz_ref, dh_ref, dg_ref):
        @pl.when(pl.program_id(0) == 0)
        def _():
            dg_ref[...] = jnp.zeros_like(dg_ref)

        na = NA_HEADS * HEAD_DIM
        dz_ref[:, 0:na] = dqa_ref[...].astype(BF16)
        dz_ref[:, na:2 * na] = dka_ref[...].astype(BF16)
        dz_ref[:, 2 * na:3 * na] = dva_ref[...].astype(BF16)
        cos, sa, sb = cos_ref[...], sa_ref[...], sb_ref[...]
        for j in range(C_QB // LANES):
            dz_ref[:, O_QB + j * LANES:O_QB + (j + 1) * LANES] = _rope(
                dqb_ref[:, j * LANES:(j + 1) * LANES], cos, sa, sb, -1.0).astype(BF16)
        dz_ref[:, O_KB:O_KB + C_KB] = _rope(dkb_ref[...], cos, sa, sb, -1.0).astype(BF16)
        dz_ref[:, O_VB:O_VB + C_KB] = dvb_ref[...].astype(BF16)
        dz_ref[:, O_G:O_G + C_GATES] = dgt_ref[...].astype(BF16)
        du = _dot(dz_ref[...], w_ref[...])
        hf = h_ref[...]
        dx, dgr = _norm_bwd(du, hf, g_ref[...], _rstd(hf))
        dg_ref[...] += jnp.sum(dgr, axis=0, keepdims=True)
        dh_ref[...] = dres_ref[...] + dx

    def tok(n):
        return pl.BlockSpec((tm, n), lambda i: (i, 0))

    vec = pl.BlockSpec((1, D), lambda i: (0, 0))
    na = NA_HEADS * HEAD_DIM
    return pl.pallas_call(
        body, name="mix_in_bwd", grid=(T // tm,),
        in_specs=[tok(na), tok(na), tok(na), tok(C_QB), tok(C_KB), tok(C_KB), tok(C_GATES), tok(D), vec, tok(D),
                  pl.BlockSpec((D_IN, D), lambda i: (0, 0)), tok(LANES), tok(LANES), tok(LANES)],
        out_specs=[tok(D_IN), tok(D), vec],
        out_shape=[jax.ShapeDtypeStruct((T, D_IN), BF16), jax.ShapeDtypeStruct((T, D), F32),
                   jax.ShapeDtypeStruct((1, D), F32)],
        compiler_params=_params(1),
    )(dqa, dka, dva, dqb, dkb, dvb, dgates, h, g, dres, w_in, *tables)


def _na_bias_slabs(rpb):
    H = rpb.shape[0]
    ncell = GRID_W * GRID_W
    cell = jnp.arange(ncell)
    co = cell % GRID_W - cell // GRID_W + (NA_KW - 1)
    e_co = (jnp.arange(LANES)[:, None] == co[None, :]).astype(F32)
    table = jnp.pad(rpb, ((0, 0), (0, 1), (0, LANES - rpb.shape[2]))).reshape(H * 2 * NA_KH, LANES)

    def body(t_ref, e_ref, o_ref):
        o_ref[...] = jnp.dot(t_ref[...], e_ref[...], preferred_element_type=F32, precision=lax.Precision.HIGHEST)

    toeplitz = pl.pallas_call(
        body, name="rpb_unfold", out_shape=jax.ShapeDtypeStruct((H * 2 * NA_KH, ncell), F32),
        compiler_params=_params(0),
    )(table, e_co).reshape(H, 2 * NA_KH, GRID_W, GRID_W)

    def assemble(tz_ref, o_ref):
        c = lax.broadcasted_iota(jnp.int32, (GRID_W, GRID_W), 0)
        k = lax.broadcasted_iota(jnp.int32, (GRID_W, GRID_W), 1)
        cs = jnp.clip(c - NA_KW // 2, 0, GRID_W - NA_KW)
        inwin = (k >= cs) & (k < cs + NA_KW)
        for ro0 in range(NA_KH):
            for hh in range(2):
                for i in range(NA_KH):
                    o_ref[0, ro0, hh * GRID_W:(hh + 1) * GRID_W, i * GRID_W:(i + 1) * GRID_W] = jnp.where(
                        inwin, tz_ref[hh, ro0 + i], NEG)

    return pl.pallas_call(
        assemble, name="na_bias_slabs", grid=(H // 2,),
        in_specs=[pl.BlockSpec((2, 2 * NA_KH, GRID_W, GRID_W), lambda p: (p, 0, 0, 0))],
        out_specs=pl.BlockSpec((1, NA_KH, 2 * GRID_W, NA_KH * GRID_W), lambda p: (p, 0, 0, 0)),
        out_shape=jax.ShapeDtypeStruct((H // 2, NA_KH, 2 * GRID_W, NA_KH * GRID_W), F32),
        compiler_params=_params(1),
    )(toeplitz)


def _na_unstack_slabs(dslab):
    pairs = dslab.shape[0]
    d = dslab.reshape(pairs, NA_KH, 2, GRID_W, NA_KH * GRID_W).transpose(0, 2, 1, 3, 4)
    return d.reshape(2 * pairs, NA_KH, GRID_W, NA_KH * GRID_W)


def _half_masks(rows):
    lane = lax.broadcasted_iota(jnp.int32, (rows, LANES), 1)
    left = lane < HEAD_DIM
    return left, (left, jnp.logical_not(left))


def _stack_heads(x):
    left, halves = _half_masks(x.shape[0])
    xf = x.astype(F32)
    return jnp.concatenate([jnp.where(m, xf, 0.0).astype(BF16) for m in halves], axis=0)


def _unstack_heads(o):
    rows = o.shape[0] // 2
    left, _ = _half_masks(rows)
    return jnp.where(left, o[:rows], o[rows:])


def _na_row(j, t, rb, rows):
    r = j * rb + t
    rs = jnp.clip(r - NA_KH // 2, 0, rows - NA_KH)
    return pl.multiple_of(t * GRID_W, GRID_W), pl.multiple_of(rs * GRID_W, GRID_W), rs - r + (NA_KH - 1)


def _na_specs(T, rb):
    qrows = GRID_W * rb
    pairs = NA_HEADS // 2
    return ([pl.BlockSpec((qrows, LANES), lambda p, j: (j, p)),
             pl.BlockSpec((T, LANES), lambda p, j: (0, pairs + p)),
             pl.BlockSpec((T, LANES), lambda p, j: (0, 2 * pairs + p))],
            pl.BlockSpec((1, NA_KH, 2 * GRID_W, NA_KH * GRID_W), lambda p, j: (p, 0, 0, 0)))


def _softmax(s):
    p = jnp.exp(s - jnp.max(s, axis=-1, keepdims=True))
    return p / jnp.sum(p, axis=-1, keepdims=True)


def _na_probs(qs, ks, bias):
    return _softmax(_dot_nt(qs, ks) * QK_SCALE + bias)


def _na_fwd(qkva, bias, *, rb=8, group=8):
    T = qkva.shape[0]
    rows = T // GRID_W
    nkeys = NA_KH * GRID_W

    def body(q_ref, k_ref, v_ref, bias_ref, y_ref):
        j = pl.program_id(1)

        def rows_step(t, carry):
            at = [_na_row(j, t * group + u, rb, rows) for u in range(group)]
            s = [_dot_nt(_stack_heads(q_ref[pl.ds(q0, GRID_W), :]), k_ref[pl.ds(k0, nkeys), :]) for q0, k0, _ in at]
            p = [_softmax(su * QK_SCALE + bias_ref[0, ro0]) for su, (_, _, ro0) in zip(s, at)]
            o = [_dot(pu.astype(BF16), v_ref[pl.ds(k0, nkeys), :]) for pu, (_, k0, _) in zip(p, at)]
            for ou, (q0, _, _) in zip(o, at):
                y_ref[pl.ds(q0, GRID_W), :] = _unstack_heads(ou).astype(BF16)
            return carry

        lax.fori_loop(0, rb // group, rows_step, 0)

    qkv_specs, bias_spec = _na_specs(T, rb)
    return pl.pallas_call(
        body, name="na_fwd", grid=(NA_HEADS // 2, rows // rb),
        in_specs=qkv_specs + [bias_spec],
        out_specs=qkv_specs[0],
        out_shape=jax.ShapeDtypeStruct((T, NA_HEADS * HEAD_DIM), BF16),
        compiler_params=_params(2),
    )(qkva, qkva, qkva, bias)


def _na_bwd(qkva, dy, bias, *, rb=8, group=8, rider=None):
    T = qkva.shape[0]
    rows = T // GRID_W
    nkeys = NA_KH * GRID_W

    def body(q_ref, k_ref, v_ref, dy_ref, bias_ref, dq_ref, dk_ref, dv_ref, dbias_ref):
        j = pl.program_id(1)

        @pl.when(j == 0)
        def _():
            dk_ref[...] = jnp.zeros_like(dk_ref)
            dv_ref[...] = jnp.zeros_like(dv_ref)
            dbias_ref[...] = jnp.zeros_like(dbias_ref)

        def rows_step(t, carry):
            at = [_na_row(j, t * group + u, rb, rows) for u in range(group)]
            qs = [_stack_heads(q_ref[pl.ds(q0, GRID_W), :]) for q0, _, _ in at]
            dys = [_stack_heads(dy_ref[pl.ds(q0, GRID_W), :]) for q0, _, _ in at]
            s = [_dot_nt(qu, k_ref[pl.ds(k0, nkeys), :]) for qu, (_, k0, _) in zip(qs, at)]
            dp = [_dot_nt(du, v_ref[pl.ds(k0, nkeys), :]) for du, (_, k0, _) in zip(dys, at)]
            p = [_softmax(su * QK_SCALE + bias_ref[0, ro0]) for su, (_, _, ro0) in zip(s, at)]
            ds = [pu * (du - jnp.sum(pu * du, axis=-1, keepdims=True)) for pu, du in zip(p, dp)]
            for u, (q0, k0, ro0) in enumerate(at):
                dbias_ref[0, ro0] += ds[u]
                dsb = ds[u].astype(BF16)
                dq_ref[pl.ds(q0, GRID_W), :] = (_unstack_heads(_dot(dsb, k_ref[pl.ds(k0, nkeys), :])) * QK_SCALE).astype(BF16)
                dk_ref[pl.ds(k0, nkeys), :] += _dot_tn(dsb, qs[u]) * QK_SCALE
                dv_ref[pl.ds(k0, nkeys), :] += _dot_tn(p[u].astype(BF16), dys[u])
            return carry

        lax.fori_loop(0, rb // group, rows_step, 0)

    qkv_specs, bias_spec = _na_specs(T, rb)
    width = NA_HEADS * HEAD_DIM
    kv_out = pl.BlockSpec((T, LANES), lambda p, j: (0, p))
    grid = (NA_HEADS // 2, rows // rb)
    body, r_in, r_in_specs, r_out, r_out_specs, scratch = _ride(body, 5, 4, rider, grid, None)
    outs = pl.pallas_call(
        body, name="na_bwd", grid=grid,
        in_specs=qkv_specs + [qkv_specs[0], bias_spec] + r_in_specs,
        out_specs=[qkv_specs[0], kv_out, kv_out, bias_spec] + r_out_specs,
        out_shape=[jax.ShapeDtypeStruct((T, width), BF16), jax.ShapeDtypeStruct((T, width), F32),
                   jax.ShapeDtypeStruct((T, width), F32), jax.ShapeDtypeStruct(bias.shape, F32)] + r_out,
        scratch_shapes=scratch,
        compiler_params=_params(2),
    )(qkva, qkva, qkva, dy, bias, *r_in)
    return (*outs[:4], list(outs[4:]))


def _rpb_fold(dslab):
    H = dslab.shape[0]
    nro = NA_KH * NA_KH
    ncell = GRID_W * GRID_W
    xs = dslab.reshape(H, NA_KH, GRID_W, NA_KH, GRID_W).transpose(0, 1, 3, 2, 4).reshape(H, nro, ncell)
    cell = jnp.arange(ncell)
    co = cell % GRID_W - cell // GRID_W + (NA_KW - 1)
    e_co = (co[:, None] == jnp.arange(LANES)[None, :]).astype(F32)
    pair = jnp.arange(nro)
    e_ro = ((pair // NA_KH + pair % NA_KH)[None, :] == jnp.arange(2 * NA_KH)[:, None]).astype(F32)

    def body(x_ref, eco_ref, ero_ref, o_ref):
        y = jnp.dot(x_ref[0], eco_ref[...], preferred_element_type=F32, precision=lax.Precision.HIGHEST)
        o_ref[0] = jnp.dot(ero_ref[...], y, preferred_element_type=F32, precision=lax.Precision.HIGHEST)

    return pl.pallas_call(
        body, name="rpb_fold", grid=(H,),
        in_specs=[pl.BlockSpec((1, nro, ncell), lambda h: (h, 0, 0)), pl.BlockSpec((ncell, LANES), lambda h: (0, 0)),
                  pl.BlockSpec((2 * NA_KH, nro), lambda h: (0, 0))],
        out_specs=pl.BlockSpec((1, 2 * NA_KH, LANES), lambda h: (h, 0, 0)),
        out_shape=jax.ShapeDtypeStruct((H, 2 * NA_KH, LANES), F32),
        compiler_params=_params(1),
    )(xs, e_co, e_ro)


SWA_KEYS = 3 * WIN


def _swa_block(j, t, qbn, T):
    blk = j * qbn + t
    start = jnp.clip((blk - 1) * WIN, 0, T - SWA_KEYS)
    row = lax.broadcasted_iota(jnp.int32, (2 * WIN, SWA_KEYS), 0)
    qpos = blk * WIN + jnp.where(row < WIN, row, row - WIN)
    kpos = start + lax.broadcasted_iota(jnp.int32, (2 * WIN, SWA_KEYS), 1)
    return pl.multiple_of(t * WIN, WIN), pl.multiple_of(start, WIN), jnp.abs(qpos - kpos) <= WIN


def _swa_sinks(sink_ref, p):
    row = lax.broadcasted_iota(jnp.int32, (2 * WIN, 1), 0)
    return jnp.where(row < WIN, sink_ref[p], sink_ref[p + NB_HEADS // 2])


def _swa_probs(s, mask, sink):
    s = jnp.where(mask, s * QK_SCALE, NEG)
    m = jnp.maximum(jnp.max(s, axis=-1, keepdims=True), sink)
    e = jnp.exp(s - m)
    esink = jnp.exp(sink - m)
    den = jnp.sum(e, axis=-1, keepdims=True) + esink
    return e / den, esink / den


def _swa_specs(T, qbn):
    return [pl.BlockSpec(memory_space=pltpu.SMEM),
            pl.BlockSpec((WIN * qbn, LANES), lambda p, j: (j, p)),
            pl.BlockSpec((T, LANES), lambda p, j: (0, 0)),
            pl.BlockSpec((T, LANES), lambda p, j: (0, 1))]


def _swa_fwd(qb, kvb, sink, *, qbn=8, group=8):
    T = qb.shape[0]
    pairs = NB_HEADS // 2
    qbn = min(qbn, T // WIN)
    group = min(group, qbn)

    def body(sink_ref, q_ref, k_ref, v_ref, y_ref):
        p, j = pl.program_id(0), pl.program_id(1)
        sinks = _swa_sinks(sink_ref, p)

        def blocks_step(t, carry):
            at = [_swa_block(j, t * group + u, qbn, T) for u in range(group)]
            s = [_dot_nt(_stack_heads(q_ref[pl.ds(q0, WIN), :]), k_ref[pl.ds(k0, SWA_KEYS), :]) for q0, k0, _ in at]
            pr = [_swa_probs(su, mask, sinks)[0] for su, (_, _, mask) in zip(s, at)]
            o = [_dot(pu.astype(BF16), v_ref[pl.ds(k0, SWA_KEYS), :]) for pu, (_, k0, _) in zip(pr, at)]
            for ou, (q0, _, _) in zip(o, at):
                y_ref[pl.ds(q0, WIN), :] = _unstack_heads(ou).astype(BF16)
            return carry

        lax.fori_loop(0, qbn // group, blocks_step, 0)

    specs = _swa_specs(T, qbn)
    return pl.pallas_call(
        body, name="swa_fwd", grid=(pairs, T // (WIN * qbn)),
        in_specs=specs, out_specs=specs[1],
        out_shape=jax.ShapeDtypeStruct((T, NB_HEADS * HEAD_DIM), BF16),
        compiler_params=_params(2),
    )(sink, qb, kvb, kvb)


def _swa_bwd(qb, kvb, dy, sink, *, qbn=8, group=8):
    T = qb.shape[0]
    pairs = NB_HEADS // 2
    qbn = min(qbn, T // WIN)
    group = min(group, qbn)

    def body(sink_ref, q_ref, k_ref, v_ref, dy_ref, dq_ref, dk_ref, dv_ref, dsink_ref):
        p, j = pl.program_id(0), pl.program_id(1)
        sinks = _swa_sinks(sink_ref, p)

        @pl.when((p == 0) & (j == 0))
        def _():
            dk_ref[...] = jnp.zeros_like(dk_ref)
            dv_ref[...] = jnp.zeros_like(dv_ref)

        @pl.when(j == 0)
        def _():
            dsink_ref[...] = jnp.zeros_like(dsink_ref)

        def blocks_step(t, carry):
            at = [_swa_block(j, t * group + u, qbn, T) for u in range(group)]
            qs = [_stack_heads(q_ref[pl.ds(q0, WIN), :]) for q0, _, _ in at]
            dys = [_stack_heads(dy_ref[pl.ds(q0, WIN), :]) for q0, _, _ in at]
            s = [_dot_nt(qu, k_ref[pl.ds(k0, SWA_KEYS), :]) for qu, (_, k0, _) in zip(qs, at)]
            dp = [_dot_nt(du, v_ref[pl.ds(k0, SWA_KEYS), :]) for du, (_, k0, _) in zip(dys, at)]
            probs = [_swa_probs(su, mask, sinks) for su, (_, _, mask) in zip(s, at)]
            for u, (q0, k0, _) in enumerate(at):
                pr, psink = probs[u]
                delta = jnp.sum(pr * dp[u], axis=-1, keepdims=True)
                dsb = (pr * (dp[u] - delta)).astype(BF16)
                dsk = psink * delta
                for hh in range(2):
                    dsink_ref[0, hh:hh + 1, :] += jnp.broadcast_to(-jnp.sum(dsk[hh * WIN:(hh + 1) * WIN]), (1, LANES))
                dq_ref[pl.ds(q0, WIN), :] = _unstack_heads(_dot(dsb, k_ref[pl.ds(k0, SWA_KEYS), :])) * QK_SCALE
                dk_ref[pl.ds(k0, SWA_KEYS), :] += _dot_tn(dsb, qs[u]) * QK_SCALE
                dv_ref[pl.ds(k0, SWA_KEYS), :] += _dot_tn(pr.astype(BF16), dys[u])
            return carry

        lax.fori_loop(0, qbn // group, blocks_step, 0)

    specs = _swa_specs(T, qbn)
    kv_out = pl.BlockSpec((T, LANES), lambda p, j: (0, 0))
    return pl.pallas_call(
        body, name="swa_bwd", grid=(pairs, T // (WIN * qbn)),
        in_specs=specs + [specs[1]],
        out_specs=[specs[1], kv_out, kv_out, pl.BlockSpec((1, 8, LANES), lambda p, j: (p, 0, 0))],
        out_shape=[jax.ShapeDtypeStruct((T, NB_HEADS * HEAD_DIM), F32), jax.ShapeDtypeStruct((T, LANES), F32),
                   jax.ShapeDtypeStruct((T, LANES), F32), jax.ShapeDtypeStruct((pairs, 8, LANES), F32)],
        compiler_params=_params(2),
    )(sink, qb, kvb, kvb, dy)


def _merge_fwd(ya, yb, gates, wa, wb, wout, h, *, tm=512):
    T, D = h.shape
    W = ya.shape[1]

    def body(ya_ref, yb_ref, gt_ref, wa_ref, wb_ref, wo_ref, h_ref, h2_ref, mg_ref):
        pa = _dot(ya_ref[...], wa_ref[...])
        pb = _dot(yb_ref[...], wb_ref[...])
        mg = (jax.nn.sigmoid(gt_ref[:, 0:D]) * pa + jax.nn.sigmoid(gt_ref[:, D:2 * D]) * pb).astype(BF16)
        mg_ref[...] = mg
        h2_ref[...] = h_ref[...] + _dot(mg, wo_ref[...])

    def tok(n):
        return pl.BlockSpec((tm, n), lambda i: (i, 0))

    def full(r, c):
        return pl.BlockSpec((r, c), lambda i: (0, 0))

    return pl.pallas_call(
        body, name="merge_fwd", grid=(T // tm,),
        in_specs=[tok(W), tok(W), tok(2 * D), full(W, D), full(W, D), full(D, D), tok(D)],
        out_specs=[tok(D), tok(D)],
        out_shape=[jax.ShapeDtypeStruct((T, D), F32), jax.ShapeDtypeStruct((T, D), BF16)],
        compiler_params=_params(1),
    )(ya, yb, gates, wa, wb, wout, h)


def _merge_bwd(dh, ya, yb, gates, wa, wb, wout, *, tm=512):
    T, D = dh.shape
    W = ya.shape[1]

    def body(dh_ref, ya_ref, yb_ref, gt_ref, wa_ref, wb_ref, wo_ref, dya_ref, dyb_ref, dpa_ref, dpb_ref, dgt_ref):
        dmg = _dot_nt(dh_ref[...].astype(BF16), wo_ref[...])
        for y_ref, w_ref, dy_ref, dp_ref, lo in ((ya_ref, wa_ref, dya_ref, dpa_ref, 0), (yb_ref, wb_ref, dyb_ref, dpb_ref, D)):
            sg = jax.nn.sigmoid(gt_ref[:, lo:lo + D])
            dp = (dmg * sg).astype(BF16)
            dp_ref[...] = dp
            dgt_ref[:, lo:lo + D] = (dmg * _dot(y_ref[...], w_ref[...]) * (sg * (1.0 - sg))).astype(BF16)
            dy_ref[...] = _dot_nt(dp, w_ref[...]).astype(BF16)

    def tok(n):
        return pl.BlockSpec((tm, n), lambda i: (i, 0))

    def full(r, c):
        return pl.BlockSpec((r, c), lambda i: (0, 0))

    return pl.pallas_call(
        body, name="merge_bwd", grid=(T // tm,),
        in_specs=[tok(D), tok(W), tok(W), tok(2 * D), full(W, D), full(W, D), full(D, D)],
        out_specs=[tok(W), tok(W), tok(D), tok(D), tok(2 * D)],
        out_shape=[jax.ShapeDtypeStruct((T, W), BF16), jax.ShapeDtypeStruct((T, W), BF16),
                   jax.ShapeDtypeStruct((T, D), BF16), jax.ShapeDtypeStruct((T, D), BF16),
                   jax.ShapeDtypeStruct((T, 2 * D), BF16)],
        compiler_params=_params(1),
    )(dh, ya, yb, gates, wa, wb, wout)


def _pair_heads(a, axis):
    shp = a.shape
    a = a.reshape(shp[:axis] + (2, NB_HEADS // 2, HEAD_DIM) + shp[axis + 1:])
    return jnp.swapaxes(a, axis, axis + 1).reshape(shp)


def _unpair_heads(a, axis):
    shp = a.shape
    a = a.reshape(shp[:axis] + (NB_HEADS // 2, 2, HEAD_DIM) + shp[axis + 1:])
    return jnp.swapaxes(a, axis, axis + 1).reshape(shp)


FFN1 = ("ffn1_w_gate", "ffn1_w_up", "ffn1_w_down")
FFN2 = ("ffn2_w_gate", "ffn2_w_up", "ffn2_w_down")
MIXER = ("w_in", "w_branch_a", "w_branch_b", "w_out")


def _layer_grads(x, target, g1, f1, gmix, late, rpb, sink, g2, gfin, comm=None):
    T = x.shape[0]
    tables = _rope_tables(T)
    bias = _na_bias_slabs(rpb)

    h1, n1, hdn1, p1, q1, gathered = _ffn_fwd(x, g1, *f1, name="ffn1_fwd", rider=comm.late_rider if comm else None)
    w_in_t, wa, wb, wout, f2 = comm.late(gathered) if comm else late
    w_in_p = jnp.concatenate([w_in_t[:O_QB], _pair_heads(w_in_t[O_QB:O_KB], 0), w_in_t[O_KB:]], axis=0)
    wb_p = _pair_heads(wb, 0)
    u, qkva, qb, kvb, gates = _mix_in_fwd(h1, gmix, w_in_p, tables)
    ya = _na_fwd(qkva, bias)
    yb = _swa_fwd(qb, kvb, sink)
    h2, merged = _merge_fwd(ya, yb, gates, wa, wb_p, wout, h1)
    dh3, n2, hdn2, p2, q2, loss, dgfin, _ = _ffn_fwd(h2, g2, *f2, name="ffn2_fwd", head=(gfin, target))

    dh2, da2, db2, dg2, _ = _ffn_bwd(dh3, h2, g2, p2, q2, *f2, name="ffn2_bwd")
    df2 = [_wgrad_shard_a(da2, n2, name="ffn2_dwg"), _wgrad_shard_a(db2, n2, name="ffn2_dwu"),
           _wgrad_shard_a(hdn2, dh3, scale=0.5, name="ffn2_dwd")]
    red2 = comm.reduce(FFN2, df2, tag="ffn2") if comm else None
    dya, dyb, dpa, dpb, dgates = _merge_bwd(dh2, ya, yb, gates, wa, wb_p, wout)
    dwout = _wgrad_cols(merged, dh2, 1, name="dwout").reshape(N_CHIPS, D_MODEL // N_CHIPS, D_MODEL)
    dwa = _wgrad_cols(ya, dpa, N_CHIPS, name="dwa")
    dwb = _unpair_heads(_wgrad_cols(yb, dpb, N_CHIPS, name="dwb"), 1)
    dqa, dka, dva, dbias, got2 = _na_bwd(qkva, dya, bias, rider=red2.rider if comm else None)
    drpb = _rpb_fold(_na_unstack_slabs(dbias))
    dqb, dkb, dvb, dsink = _swa_bwd(qb, kvb, dyb, sink)
    dz, dh1, dgmix = _mix_in_bwd(dqa, dka, dva, dqb, dkb, dvb, dgates, h1, gmix, dh2, w_in_p, tables)
    dwin_p = _wgrad_rows(dz, u, 2, name="dwin").reshape(D_IN, D_MODEL)
    dwin = jnp.concatenate([dwin_p[:O_QB], _unpair_heads(dwin_p[O_QB:O_KB], 0), dwin_p[O_KB:]], axis=0)
    dmix = [dwin.reshape(N_CHIPS, D_IN // N_CHIPS, D_MODEL), dwa, dwb, dwout]
    dx, da1, db1, dg1, _ = _ffn_bwd(dh1, x, g1, p1, q1, *f1, name="ffn1_bwd")
    out = dict(loss=loss, dx=dx, ffn1_norm=dg1, mix_norm=dgmix, ffn2_norm=dg2, final_norm=dgfin, na_rpb=drpb,
               sink_logit=dsink[:, 0:2, 0].T.reshape(NB_HEADS))
    if comm:
        out.update(red2.finish(got2))
        redm = comm.reduce(MIXER, dmix, tag="mixer")
        dwg1, gotm = _wgrad_shard_a(da1, n1, name="ffn1_dwg", rider=redm.rider)
        out.update(redm.finish(gotm))
        dwu1 = _wgrad_shard_a(db1, n1, name="ffn1_dwu")
        red1 = comm.reduce(FFN1[:2], [dwg1, dwu1], tag="ffn1_gate_up")
        dwd1, got1 = _wgrad_shard_a(hdn1, dh1, scale=0.5, name="ffn1_dwd", rider=red1.rider)
        out.update(red1.finish(got1))
        red1 = comm.reduce(FFN1[2:], [dwd1], tag="ffn1_down")
        out.update(red1.finish(_run_rider(red1.rider, name="rs_chips_ffn1_down")))
    else:
        df1 = [_wgrad_shard_a(da1, n1, name="ffn1_dwg"), _wgrad_shard_a(db1, n1, name="ffn1_dwu"),
               _wgrad_shard_a(hdn1, dh1, scale=0.5, name="ffn1_dwd")]
        out.update(zip(FFN1 + MIXER + FFN2, df1 + dmix + df2))
    return out


ANY = pl.BlockSpec(memory_space=pl.ANY)


def _place():
    x, y, c = lax.axis_index("x"), lax.axis_index("y"), lax.axis_index("c")
    chips = [(1 - x, y), (x, 1 - y), (1 - x, 1 - y)]
    return x, y, c, 2 * x + y, chips


def _remote(src, dst, send_sems, recv_sems, k, device):
    return pltpu.make_async_remote_copy(src_ref=src, dst_ref=dst, send_sem=send_sems.at[k], recv_sem=recv_sems.at[k],
                                        device_id=device, device_id_type=MESH)


class _Rider:
    def __init__(self, inputs, out_shape, scratch, start, middle, finish):
        self.inputs, self.out_shape, self.scratch = list(inputs), list(out_shape), list(scratch)
        self.start, self.middle, self.finish = start, middle, finish


def _run_rider(rider, *, name):
    n_in, n_out = len(rider.inputs), len(rider.out_shape)

    def body(*refs):
        ins, outs, sems = refs[:n_in], refs[n_in:n_in + n_out], refs[n_in + n_out:]
        rider.start(ins, outs, sems)
        if rider.middle is not None:
            rider.middle(ins, outs, sems)
        rider.finish(ins, outs, sems)

    return pl.pallas_call(body, name=name, in_specs=[ANY] * n_in, out_specs=[ANY] * n_out, out_shape=rider.out_shape,
                          scratch_shapes=rider.scratch)(*rider.inputs)


def _ride(body, n_in, n_out, rider, grid, middle_step):
    if rider is None:
        return body, [], [], [], [], []
    r_in, r_out = len(rider.inputs), len(rider.out_shape)
    steps = grid[0] * grid[1]

    def riding(*refs):
        ins, r_ins = refs[:n_in], refs[n_in:n_in + r_in]
        outs = refs[n_in + r_in:n_in + r_in + n_out]
        r_outs = refs[n_in + r_in + n_out:n_in + r_in + n_out + r_out]
        sems = refs[n_in + r_in + n_out + r_out:]
        step = pl.program_id(0) * grid[1] + pl.program_id(1)

        @pl.when(step == 0)
        def _():
            rider.start(r_ins, r_outs, sems)

        body(*ins, *outs)

        if rider.middle is not None:
            @pl.when(step == middle_step)
            def _():
                rider.middle(r_ins, r_outs, sems)

        @pl.when(step == steps - 1)
        def _():
            rider.finish(r_ins, r_outs, sems)

    return riding, rider.inputs, [ANY] * r_in, rider.out_shape, [ANY] * r_out, rider.scratch


def _gather_rider(shards):
    n = len(shards)

    def plan(ins, outs, sems, kinds):
        send_sems, recv_sems, own_send_sems, own_recv_sems = sems
        x, y, c, mine, chips = _place()
        sibling = (x, y, 1 - c)
        made = {k: [] for k in kinds}
        for i in range(n):
            hr = shards[i].shape[0] // 2
            if "own" in made:
                made["own"].append(_remote(ins[i], outs[i].at[mine], own_send_sems, own_recv_sems, i, sibling))
            for j, (cx, cy) in enumerate(chips):
                here = outs[i].at[2 * cx + cy, pl.ds(c * hr, hr)]
                there = outs[i].at[2 * cx + cy, pl.ds((1 - c) * hr, hr)]
                if "sends" in made:
                    made["sends"].append(_remote(ins[i].at[pl.ds(c * hr, hr)], outs[i].at[mine, pl.ds(c * hr, hr)],
                                                 send_sems, recv_sems, 6 * i + j, (cx, cy, c)))
                if "landed" in made:
                    made["landed"].append(_remote(here, here, send_sems, recv_sems, 6 * i + j, (cx, cy, c)))
                if "passes" in made:
                    made["passes"].append(_remote(here, here, send_sems, recv_sems, 6 * i + 3 + j, sibling))
                if "others" in made:
                    made["others"].append(_remote(there, there, send_sems, recv_sems, 6 * i + 3 + j, sibling))
        return [made[k] for k in kinds]

    def start(ins, outs, sems):
        own, sends = plan(ins, outs, sems, ("own", "sends"))
        for cp in own + sends:
            cp.start()

    def middle(ins, outs, sems):
        landed, passes = plan(ins, outs, sems, ("landed", "passes"))
        for arrived, cp in zip(landed, passes):
            arrived.wait_recv()
            cp.start()

    def finish(ins, outs, sems):
        own, sends, passes, others = plan(ins, outs, sems, ("own", "sends", "passes", "others"))
        for arrived in others:
            arrived.wait_recv()
        for cp in sends + passes:
            cp.wait_send()
        for cp in own:
            cp.wait()

    return _Rider(shards, [jax.ShapeDtypeStruct((N_CHIPS,) + s.shape, s.dtype) for s in shards],
                  [pltpu.SemaphoreType.DMA((6 * n,)), pltpu.SemaphoreType.DMA((6 * n,)),
                   pltpu.SemaphoreType.DMA((n,)), pltpu.SemaphoreType.DMA((n,))], start, middle, finish)


def _rs_sibling(grads, *, name):
    n = len(grads)

    def body(*refs):
        ins, outs = refs[:n], refs[n:2 * n]
        send_sems, recv_sems = refs[2 * n:]
        x, y, c, _, _ = _place()
        copies = []
        for i in range(n):
            hr = grads[i].shape[1] // 2
            cp = _remote(ins[i].at[:, pl.ds((1 - c) * hr, hr)], outs[i], send_sems, recv_sems, i, (x, y, 1 - c))
            cp.start()
            copies.append(cp)
        for cp in copies:
            cp.wait()

    return pl.pallas_call(
        body, name=name,
        in_specs=[ANY] * n, out_specs=[ANY] * n,
        out_shape=[jax.ShapeDtypeStruct((g.shape[0], g.shape[1] // 2, g.shape[2]), g.dtype) for g in grads],
        scratch_shapes=[pltpu.SemaphoreType.DMA((n,)), pltpu.SemaphoreType.DMA((n,))],
    )(*grads)


def _chips_rider(parts):
    n = len(parts)

    def plan(ins, outs, sems):
        send_sems, recv_sems = sems
        _, _, c, _, chips = _place()
        return [_remote(ins[i].at[2 * cx + cy], outs[i].at[j], send_sems, recv_sems, 3 * i + j, (cx, cy, c))
                for i in range(n) for j, (cx, cy) in enumerate(chips)]

    def start(ins, outs, sems):
        for cp in plan(ins, outs, sems):
            cp.start()

    def finish(ins, outs, sems):
        for cp in plan(ins, outs, sems):
            cp.wait()

    return _Rider(parts, [jax.ShapeDtypeStruct((N_CHIPS - 1,) + p.shape[1:], p.dtype) for p in parts],
                  [pltpu.SemaphoreType.DMA((3 * n,)), pltpu.SemaphoreType.DMA((3 * n,))], start, None, finish)


class _Reduce:
    def __init__(self, names, grads, cidx, chip, *, tag):
        self.names, self.chip, self.tag = names, chip, tag
        from_sibling = _rs_sibling(grads, name="rs_sibling_" + tag)
        self.parts = [_add_sibling(g, r, cidx, name="add_sibling_" + k) for k, g, r in zip(names, grads, from_sibling)]
        self.rider = _chips_rider([p16 for _, p16 in self.parts])

    def finish(self, from_chips):
        halves = [_add_chips(p, r, self.chip, name="add_chips_" + k) for k, (p, _), r in zip(self.names, self.parts, from_chips)]
        others = _rs_share(halves, name="rs_share_" + self.tag)
        return dict(zip(self.names, zip(halves, others)))


def _rs_share(halves, *, name):
    n = len(halves)

    def body(*refs):
        ins, outs = refs[:n], refs[n:2 * n]
        send_sems, recv_sems = refs[2 * n:]
        x, y, c, _, _ = _place()
        copies = []
        for i in range(n):
            cp = _remote(ins[i], outs[i], send_sems, recv_sems, i, (x, y, 1 - c))
            cp.start()
            copies.append(cp)
        for cp in copies:
            cp.wait()

    return pl.pallas_call(
        body, name=name,
        in_specs=[ANY] * n, out_specs=[ANY] * n,
        out_shape=[jax.ShapeDtypeStruct(h.shape, h.dtype) for h in halves],
        scratch_shapes=[pltpu.SemaphoreType.DMA((n,)), pltpu.SemaphoreType.DMA((n,))],
    )(*halves)


N_DEV = 8


def _small_allreduce(vec):
    R = vec.shape[0]

    def body(v_ref, o_ref, buf, send_sems, recv_sems):
        x, y, c, _, _ = _place()
        me = 4 * x + 2 * y + c
        buf[me] = v_ref[...]
        copies = []
        for k in range(1, N_DEV):
            peer = (x ^ (k >> 2), y ^ ((k >> 1) & 1), c ^ (k & 1))
            cp = _remote(v_ref, buf.at[me], send_sems, recv_sems, k - 1, peer)
            cp.start()
            copies.append(cp)
        for k, cp in enumerate(copies, start=1):
            cp.wait_send()
            landed = buf.at[me ^ k]
            _remote(landed, landed, send_sems, recv_sems, k - 1, (x, y, c)).wait_recv()
        acc = buf[0]
        for d in range(1, N_DEV):
            acc = acc + buf[d]
        o_ref[...] = acc

    return pl.pallas_call(
        body, name="small_allreduce",
        in_specs=[pl.BlockSpec(memory_space=pltpu.VMEM)], out_specs=pl.BlockSpec(memory_space=pltpu.VMEM),
        out_shape=jax.ShapeDtypeStruct(vec.shape, vec.dtype),
        scratch_shapes=[pltpu.VMEM((N_DEV, R, LANES), F32), pltpu.SemaphoreType.DMA((N_DEV - 1,)),
                        pltpu.SemaphoreType.DMA((N_DEV - 1,))],
    )(vec)


ELEMWISE_BLOCK = 256 * 1024


def _row_tile(rows, cols):
    best = None
    for t in range(8, rows + 1, 8):
        if rows % t == 0 and t * cols <= ELEMWISE_BLOCK:
            best = t
    return best if best is not None else rows


def _add_sibling(g, r1, cidx, *, name):
    S, R, C = g.shape
    hr = R // 2
    tr = _row_tile(hr, C)
    nt = hr // tr

    def body(c_ref, g_ref, r_ref, o_ref, o16_ref):
        p = g_ref[...] + r_ref[...]
        o_ref[...] = p
        o16_ref[...] = p.astype(BF16)

    blk = pl.BlockSpec((1, tr, C), lambda s, t, c: (s, t, 0))
    return pl.pallas_call(
        body, name=name,
        grid_spec=pltpu.PrefetchScalarGridSpec(
            num_scalar_prefetch=1, grid=(S, nt),
            in_specs=[pl.BlockSpec((1, tr, C), lambda s, t, c: (s, c[0] * nt + t, 0)), blk], out_specs=[blk, blk]),
        out_shape=[jax.ShapeDtypeStruct((S, hr, C), F32), jax.ShapeDtypeStruct((S, hr, C), BF16)],
        compiler_params=_params(2),
    )(cidx, g, r1)


def _add_chips(p, r2, chip, *, name):
    _, hr, C = p.shape
    tr = _row_tile(hr, C)

    def body(chip_ref, p_ref, r_ref, o_ref):
        o_ref[...] = ((p_ref[0] + r_ref[0].astype(F32)) + r_ref[1].astype(F32)) + r_ref[2].astype(F32)

    return pl.pallas_call(
        body, name=name,
        grid_spec=pltpu.PrefetchScalarGridSpec(
            num_scalar_prefetch=1, grid=(hr // tr,),
            in_specs=[pl.BlockSpec((1, tr, C), lambda t, s: (s[0], t, 0)), pl.BlockSpec((N_CHIPS - 1, tr, C), lambda t, s: (0, t, 0))],
            out_specs=pl.BlockSpec((tr, C), lambda t, s: (t, 0))),
        out_shape=jax.ShapeDtypeStruct((hr, C), F32),
        compiler_params=_params(1),
    )(chip, p, r2)


def _adamw_math(w, g, m, v):
    mn = ADAM_B1 * m + (1.0 - ADAM_B1) * g
    vn = ADAM_B2 * v + (1.0 - ADAM_B2) * (g * g)
    m_hat = mn / (1.0 - ADAM_B1 ** ADAM_STEP)
    v_hat = vn / (1.0 - ADAM_B2 ** ADAM_STEP)
    return -ADAM_LR * (m_hat / (jnp.sqrt(v_hat) + ADAM_EPS) + ADAM_WD * w), mn, vn


def _adamw_halves(w, mine, other, m, v, cidx, *, name):
    R, C = w.shape
    hr = R // 2
    tr = _row_tile(hr, C)
    nt = hr // tr

    def body(c_ref, w_ref, a_ref, b_ref, m_ref, v_ref, g_ref, d_ref, mo_ref, vo_ref):
        gv = jnp.where(pl.program_id(0) == c_ref[0], a_ref[...], b_ref[...])
        g_ref[...] = gv
        d_ref[...], mo_ref[...], vo_ref[...] = _adamw_math(w_ref[...], gv, m_ref[...], v_ref[...])

    full = pl.BlockSpec((tr, C), lambda h, t, c: (h * nt + t, 0))
    half = pl.BlockSpec((tr, C), lambda h, t, c: (t, 0))
    shape = jax.ShapeDtypeStruct((R, C), F32)
    return pl.pallas_call(
        body, name=name,
        grid_spec=pltpu.PrefetchScalarGridSpec(
            num_scalar_prefetch=1, grid=(2, nt), in_specs=[full, half, half, full, full], out_specs=[full] * 4),
        out_shape=[shape] * 4,
        compiler_params=_params(2),
    )(cidx, w, mine, other, m, v)


def _adamw(w, g, m, v, *, name):
    R, C = w.shape
    tr = _row_tile(R, C)

    def body(w_ref, g_ref, m_ref, v_ref, d_ref, mo_ref, vo_ref):
        d_ref[...], mo_ref[...], vo_ref[...] = _adamw_math(w_ref[...], g_ref[...], m_ref[...], v_ref[...])

    blk = pl.BlockSpec((tr, C), lambda t: (t, 0))
    shape = jax.ShapeDtypeStruct((R, C), F32)
    return pl.pallas_call(
        body, name=name, grid=(R // tr,),
        in_specs=[blk] * 4, out_specs=[blk] * 3, out_shape=[shape] * 3,
        compiler_params=_params(1),
    )(w, g, m, v)


def _unstack_cols(w):
    s, r, c = w.shape
    return w.transpose(1, 0, 2).reshape(r, s * c)


def _pad_rows(a, rows):
    return jnp.pad(a, ((0, rows - a.shape[0]), (0, LANES - a.shape[1])))


BIG = ("ffn1_w_gate", "ffn1_w_up", "ffn1_w_down", "w_in", "w_branch_a", "w_branch_b", "w_out",
       "ffn2_w_gate", "ffn2_w_up", "ffn2_w_down")
TRANSPOSED = ("ffn1_w_gate", "ffn1_w_up", "w_in", "ffn2_w_gate", "ffn2_w_up")
WEIGHTS = ("ffn1_norm", "ffn1_w_gate", "ffn1_w_up", "ffn1_w_down", "mix_norm", "w_in", "na_rpb", "sink_logit",
           "w_branch_a", "w_branch_b", "w_out", "ffn2_norm", "ffn2_w_gate", "ffn2_w_up", "ffn2_w_down", "final_norm")


def kernel(x, ffn1_norm, ffn1_w_gate, ffn1_w_up, ffn1_w_down, mix_norm, w_in, na_rpb, sink_logit, w_branch_a, w_branch_b, w_out, ffn2_norm, ffn2_w_gate, ffn2_w_up, ffn2_w_down, final_norm, loss_target, m_ffn1_norm, m_ffn1_w_gate, m_ffn1_w_up, m_ffn1_w_down, m_mix_norm, m_w_in, m_na_rpb, m_sink_logit, m_w_branch_a, m_w_branch_b, m_w_out, m_ffn2_norm, m_ffn2_w_gate, m_ffn2_w_up, m_ffn2_w_down, m_final_norm, v_ffn1_norm, v_ffn1_w_gate, v_ffn1_w_up, v_ffn1_w_down, v_mix_norm, v_w_in, v_na_rpb, v_sink_logit, v_w_branch_a, v_w_branch_b, v_w_out, v_ffn2_norm, v_ffn2_w_gate, v_ffn2_w_up, v_ffn2_w_down, v_final_norm):
    args = dict(locals())
    w = {k: args[k] for k in WEIGHTS}
    mom = {k: args["m_" + k] for k in WEIGHTS}
    var = {k: args["v_" + k] for k in WEIGHTS}
    cidx = lax.axis_index("c").astype(jnp.int32).reshape(1)
    chip = (2 * lax.axis_index("x") + lax.axis_index("y")).astype(jnp.int32).reshape(1)

    def shard(a, k):
        return jnp.swapaxes(a[0], 0, 1) if k in TRANSPOSED else a[0]

    def unshard(a, k):
        return (jnp.swapaxes(a, 0, 1) if k in TRANSPOSED else a)[None]

    def bf16_shards(names):
        return [shard(w[k], k).astype(BF16) for k in names]

    class comm:
        late_rider = _gather_rider(bf16_shards(MIXER + FFN2))

        @staticmethod
        def late(gathered):
            full = dict(zip(MIXER + FFN2, gathered))
            return (full["w_in"].reshape(D_IN, D_MODEL), _unstack_cols(full["w_branch_a"]), _unstack_cols(full["w_branch_b"]),
                    full["w_out"].reshape(D_MODEL, D_MODEL), tuple(full[k] for k in FFN2))

        @staticmethod
        def reduce(names, grads, *, tag):
            return _Reduce(names, grads, cidx, chip, tag=tag)

    f1 = _run_rider(_gather_rider(bf16_shards(FFN1)), name="all_gather_ffn1")
    out = _layer_grads(x[0], loss_target[0], ffn1_norm, f1, mix_norm, None, na_rpb[0], sink_logit[0], ffn2_norm,
                       final_norm.reshape(1, D_MODEL), comm=comm)
    mine = {k: out[k][0] for k in BIG}
    other = {k: out[k][1] for k in BIG}
    grad = {}

    rows = D_MODEL // LANES
    small = jnp.concatenate([
        out["ffn1_norm"].reshape(rows, LANES), out["mix_norm"].reshape(rows, LANES), out["ffn2_norm"].reshape(rows, LANES),
        out["final_norm"].reshape(rows, LANES), out["na_rpb"].reshape(-1, LANES),
        _pad_rows(out["sink_logit"].reshape(1, NB_HEADS), 8), _pad_rows(out["loss"], 8)], axis=0)
    total = _small_allreduce(small)
    n_rpb = NA_HEADS * 2 * NA_KH
    grad["ffn1_norm"] = total[0:rows].reshape(1, D_MODEL)
    grad["mix_norm"] = total[rows:2 * rows].reshape(1, D_MODEL)
    grad["ffn2_norm"] = total[2 * rows:3 * rows].reshape(1, D_MODEL)
    grad["final_norm"] = total[3 * rows:4 * rows].reshape(1, D_MODEL)
    grad["na_rpb"] = total[4 * rows:4 * rows + n_rpb].reshape(NA_HEADS, 2 * NA_KH, LANES)[:, :2 * NA_KH - 1, :2 * NA_KW - 1]
    grad["na_rpb"] = grad["na_rpb"].reshape(NA_HEADS, -1)
    grad["sink_logit"] = total[4 * rows + n_rpb:4 * rows + n_rpb + 1, 0:NB_HEADS]
    loss = total[4 * rows + n_rpb + 8, 0]

    deltas, new_m, new_v, grads_out = {}, {}, {}, {}
    for k in WEIGHTS:
        shape = w[k].shape
        if k in mine:
            res = _adamw_halves(shard(w[k], k), mine[k], other[k], shard(mom[k], k), shard(var[k], k), cidx, name="adamw_" + k)
            grads_out[k], deltas[k], new_m[k], new_v[k] = (unshard(a, k) for a in res)
        else:
            g2d = grad[k]
            d, mn, vn = _adamw(w[k].reshape(g2d.shape), g2d, mom[k].reshape(g2d.shape), var[k].reshape(g2d.shape), name="adamw_" + k)
            grads_out[k], deltas[k], new_m[k], new_v[k] = (a.reshape(shape) for a in (g2d, d, mn, vn))
    return (loss, out["dx"].reshape(x.shape), *[grads_out[k] for k in WEIGHTS], *[deltas[k] for k in WEIGHTS],
            *[new_m[k] for k in WEIGHTS], *[new_v[k] for k in WEIGHTS])
```

```python
import functools
import math

import jax
import jax.numpy as jnp
import numpy as np
from jax import lax
from jax.experimental import pallas as pl
from jax.experimental.pallas import tpu as pltpu

F32 = jnp.float32
BF16 = jnp.bfloat16

D_MODEL = 1024
HEAD_DIM = 64
NA_HEADS = 8
NB_HEADS = 8
GRID_W = 64
NA_KH = 8
NA_KW = 16
WIN = 128
ROPE_THETA = 10000.0
EPS = 1e-6
N_CHIPS = 4
QK_SCALE = HEAD_DIM ** -0.5
NEG = -1e30
LANES = 128
VMEM_LIMIT = 56 * 1024 * 1024
HEAD_ROWS = 256

C_QKVA = 3 * NA_HEADS * HEAD_DIM
C_QB = NB_HEADS * HEAD_DIM
C_KB = 2 * HEAD_DIM
C_ROPE = C_QB + C_KB
C_GATES = 2 * D_MODEL
D_IN = C_QKVA + C_QB + 2 * C_KB + C_GATES
O_QB = C_QKVA
O_KB = O_QB + C_QB
O_VB = O_KB + C_KB
O_G = O_VB + C_KB

ADAM_LR = 0.001
ADAM_B1 = 0.9
ADAM_B2 = 0.999
ADAM_EPS = 1e-08
ADAM_WD = 0.01
ADAM_STEP = 10

MESH = pl.DeviceIdType.MESH


def _dot(a, b):
    return jnp.dot(a, b, preferred_element_type=F32)


def _dot_nt(a, b):
    return lax.dot_general(a, b, (((1,), (1,)), ((), ())), preferred_element_type=F32)


def _dot_tn(a, b):
    return lax.dot_general(a, b, (((0,), (0,)), ((), ())), preferred_element_type=F32)


def _params(n_axes):
    return pltpu.CompilerParams(dimension_semantics=("arbitrary",) * n_axes, vmem_limit_bytes=VMEM_LIMIT)


def _rstd(xf):
    return lax.rsqrt(jnp.mean(xf * xf, axis=-1, keepdims=True) + EPS)


def _norm_bwd(dn, xf, g, r):
    xhat = xf * r
    dxh = dn * g
    dx = r * (dxh - xhat * jnp.mean(dxh * xhat, axis=-1, keepdims=True))
    return dx, dn * xhat


def _sigmoid(x):
    return 0.5 * jnp.tanh(0.5 * x) + 0.5


def _loss_head(hf, gv, tgt):
    r = _rstd(hf)
    err = (hf * r) * gv - tgt
    dx, dgr = _norm_bwd(err * (1.0 / hf.shape[-1]), hf, gv, r)
    return 0.5 * jnp.mean(err * err, axis=-1, keepdims=True), dx, dgr


def _ffn_fwd(x, g, wg, wu, wd, *, name, tm=1024, sub=512, rider=None, head=None):
    T, D = x.shape
    F = wg.shape[1]
    tm = min(tm, T)
    sub = min(sub, tm)
    n_head = 0 if head is None else 2

    def body(*refs):
        x_ref, g_ref, wg_ref, wu_ref, wd_ref = refs[:5]
        h_ref, n_ref, hdn_ref, p_ref, q_ref = refs[5 + n_head:10 + n_head]
        i, s = pl.program_id(0), pl.program_id(1)
        _ffn_fwd_step(x_ref, g_ref, wg_ref, wu_ref, wd_ref, h_ref, n_ref, hdn_ref, p_ref, q_ref, s)
        if head is not None:
            gf_ref, t_ref = refs[5:7]
            loss_ref, dgf_ref = refs[10 + n_head:]

            @pl.when((i == 0) & (s == 0))
            def _():
                loss_ref[...] = jnp.zeros_like(loss_ref)
                dgf_ref[...] = jnp.zeros_like(dgf_ref)

            @pl.when(s == N_CHIPS - 1)
            def _():
                for u in range(tm // HEAD_ROWS):
                    r = pl.ds(u * HEAD_ROWS, HEAD_ROWS)
                    terms, dh, dgr = _loss_head(h_ref[r, :], gf_ref[...], t_ref[r, :])
                    loss_ref[...] += jnp.broadcast_to(jnp.sum(terms), loss_ref.shape)
                    dgf_ref[...] += jnp.sum(dgr, axis=0, keepdims=True)
                    h_ref[r, :] = dh

    def _ffn_fwd_step(x_ref, g_ref, wg_ref, wu_ref, wd_ref, h_ref, n_ref, hdn_ref, p_ref, q_ref, s):

        @pl.when(s == 0)
        def _():
            xf = x_ref[...]
            n_ref[...] = ((xf * _rstd(xf)) * g_ref[...]).astype(BF16)
            h_ref[...] = xf

        rows = [pl.ds(u * sub, sub) for u in range(tm // sub)]
        ab = [(_dot_nt(n_ref[r, :], wg_ref[0]), _dot_nt(n_ref[r, :], wu_ref[0])) for r in rows]
        hdns = []
        for r, (a, b) in zip(rows, ab):
            sg = _sigmoid(a)
            silu = a * sg
            hdn = (silu * b).astype(BF16)
            hdn_ref[0, r, :] = hdn
            p_ref[0, r, :] = (b * (sg + silu * (1.0 - sg))).astype(BF16)
            q_ref[0, r, :] = silu.astype(BF16)
            hdns.append(hdn)
        for r, hdn in zip(rows, hdns):
            h_ref[r, :] += 0.5 * _dot(hdn, wd_ref[0])

    tok = pl.BlockSpec((tm, D), lambda i, s: (i, 0))
    hid = pl.BlockSpec((1, tm, F), lambda i, s: (s, i, 0))
    wspec = pl.BlockSpec((1, F, D), lambda i, s: (s, 0, 0))
    hshape = jax.ShapeDtypeStruct((N_CHIPS, T, F), BF16)
    grid = (T // tm, N_CHIPS)
    vec = pl.BlockSpec((1, D), lambda i, s: (0, 0))
    head_in, head_in_specs, head_out, head_out_specs = [], [], [], []
    if head is not None:
        head_in, head_in_specs = list(head), [vec, tok]
        head_out = [jax.ShapeDtypeStruct((1, LANES), F32), jax.ShapeDtypeStruct((1, D), F32)]
        head_out_specs = [pl.BlockSpec((1, LANES), lambda i, s: (0, 0)), vec]
    n_main = 5 + n_head
    body, r_in, r_in_specs, r_out, r_out_specs, scratch = _ride(body, n_main, n_main, rider, grid, (grid[0] * grid[1] * 7) // 8)
    outs = pl.pallas_call(
        body, name=name, grid=grid,
        in_specs=[tok, vec, wspec, wspec, wspec] + head_in_specs + r_in_specs,
        out_specs=[tok, tok, hid, hid, hid] + head_out_specs + r_out_specs,
        out_shape=[jax.ShapeDtypeStruct((T, D), F32), jax.ShapeDtypeStruct((T, D), BF16), hshape, hshape, hshape]
        + head_out + r_out,
        scratch_shapes=scratch,
        compiler_params=_params(2),
    )(x, g, wg, wu, wd, *head_in, *r_in)
    return (*outs[:n_main], list(outs[n_main:]))


def _ffn_bwd(dh, x, g, p, q, wg, wu, wd, *, name, tm=1024, sub=256, rider=None):
    T, D = x.shape
    F = wg.shape[1]
    tm = min(tm, T)
    sub = min(sub, tm)

    def body(dh_ref, x_ref, g_ref, p_ref, q_ref, wg_ref, wu_ref, wd_ref, dx_ref, da_ref, db_ref, dg_ref):
        i, s = pl.program_id(0), pl.program_id(1)

        @pl.when((i == 0) & (s == 0))
        def _():
            dg_ref[...] = jnp.zeros_like(dg_ref)

        @pl.when(s == 0)
        def _():
            dx_ref[...] = jnp.zeros_like(dx_ref)

        rows = [pl.ds(u * sub, sub) for u in range(tm // sub)]
        dhdn = [_dot_nt((0.5 * dh_ref[r, :]).astype(BF16), wd_ref[0]) for r in rows]
        das, dbs = [], []
        for r, dd in zip(rows, dhdn):
            da = (dd * p_ref[0, r, :].astype(F32)).astype(BF16)
            db = (dd * q_ref[0, r, :].astype(F32)).astype(BF16)
            da_ref[0, r, :] = da
            db_ref[0, r, :] = db
            das.append(da)
            dbs.append(db)
        for r, da, db in zip(rows, das, dbs):
            dx_ref[r, :] += _dot(da, wg_ref[0]) + _dot(db, wu_ref[0])

        @pl.when(s == N_CHIPS - 1)
        def _():
            xf = x_ref[...]
            dx, dgr = _norm_bwd(dx_ref[...], xf, g_ref[...], _rstd(xf))
            dg_ref[...] += jnp.sum(dgr, axis=0, keepdims=True)
            dx_ref[...] = dh_ref[...] + dx

    tok = pl.BlockSpec((tm, D), lambda i, s: (i, 0))
    hid = pl.BlockSpec((1, tm, F), lambda i, s: (s, i, 0))
    vec = pl.BlockSpec((1, D), lambda i, s: (0, 0))
    hshape = jax.ShapeDtypeStruct((N_CHIPS, T, F), BF16)
    wspec = pl.BlockSpec((1, F, D), lambda i, s: (s, 0, 0))
    grid = (T // tm, N_CHIPS)
    body, r_in, r_in_specs, r_out, r_out_specs, scratch = _ride(body, 8, 4, rider, grid, None)
    outs = pl.pallas_call(
        body, name=name, grid=grid,
        in_specs=[tok, tok, vec, hid, hid, wspec, wspec, wspec] + r_in_specs,
        out_specs=[tok, hid, hid, vec] + r_out_specs,
        out_shape=[jax.ShapeDtypeStruct((T, D), F32), hshape, hshape, jax.ShapeDtypeStruct((1, D), F32)] + r_out,
        scratch_shapes=scratch,
        compiler_params=_params(2),
    )(dh, x, g, p, q, wg, wu, wd, *r_in)
    return (*outs[:4], list(outs[4:]))


def _wgrad(a, b, *, a_block, a_map, b_block, b_map, out_shape, o_block, o_map, grid, scale=1.0, name, rider=None):
    def body(a_ref, b_ref, o_ref):
        @pl.when(pl.program_id(len(grid) - 1) == 0)
        def _():
            o_ref[...] = jnp.zeros_like(o_ref)

        av = a_ref[...]
        bv = b_ref[...]
        av = av.reshape(av.shape[-2:]).astype(BF16)
        bv = bv.reshape(bv.shape[-2:])
        if scale != 1.0:
            bv = scale * bv
        o_ref[...] += _dot_tn(av, bv.astype(BF16)).reshape(o_ref.shape)

    body, r_in, r_in_specs, r_out, r_out_specs, scratch = _ride(body, 2, 1, rider, grid, None)
    outs = pl.pallas_call(
        body, name=name, grid=grid,
        in_specs=[pl.BlockSpec(a_block, a_map), pl.BlockSpec(b_block, b_map)] + r_in_specs,
        out_specs=[pl.BlockSpec(o_block, o_map)] + r_out_specs,
        out_shape=[jax.ShapeDtypeStruct(out_shape, F32)] + r_out,
        scratch_shapes=scratch,
        compiler_params=_params(len(grid)),
    )(a, b, *r_in)
    return outs[0] if rider is None else (outs[0], list(outs[1:]))


def _wgrad_rows(a, b, n_blocks, *, name, tk=1024):
    T, N = b.shape
    M = a.shape[1] // n_blocks
    tk = min(tk, T)
    return _wgrad(a, b, a_block=(tk, M), a_map=lambda s, k: (k, s), b_block=(tk, N), b_map=lambda s, k: (k, 0),
                  out_shape=(n_blocks, M, N), o_block=(1, M, N), o_map=lambda s, k: (s, 0, 0), grid=(n_blocks, T // tk), name=name)


def _wgrad_shard_a(a, b, *, name, scale=1.0, tk=2048, rider=None):
    S, T, M = a.shape
    N = b.shape[1]
    tk = min(tk, T)
    return _wgrad(a, b, a_block=(1, tk, M), a_map=lambda s, k: (s, k, 0), b_block=(tk, N), b_map=lambda s, k: (k, 0),
                  out_shape=(S, M, N), o_block=(1, M, N), o_map=lambda s, k: (s, 0, 0), grid=(S, T // tk), scale=scale,
                  name=name, rider=rider)


def _wgrad_cols(a, b, n_blocks, *, name, tk=1024):
    T, M = a.shape
    N = b.shape[1] // n_blocks
    tk = min(tk, T)

    def body(a_ref, b_ref, o_ref):
        @pl.when(pl.program_id(0) == 0)
        def _():
            o_ref[...] = jnp.zeros_like(o_ref)

        r = _dot_tn(a_ref[...].astype(BF16), b_ref[...].astype(BF16))
        for s in range(n_blocks):
            o_ref[s] += r[:, s * N:(s + 1) * N]

    return pl.pallas_call(
        body, name=name, grid=(T // tk,),
        in_specs=[pl.BlockSpec((tk, M), lambda k: (k, 0)), pl.BlockSpec((tk, n_blocks * N), lambda k: (k, 0))],
        out_specs=pl.BlockSpec((n_blocks, M, N), lambda k: (0, 0, 0)),
        out_shape=jax.ShapeDtypeStruct((n_blocks, M, N), F32),
        compiler_params=_params(1),
    )(a, b)


def _rope_tables(T):
    half = HEAD_DIM // 2
    inv = np.float32(ROPE_THETA) ** (-np.arange(half, dtype=np.float32) / np.float32(half))
    ang = np.arange(T, dtype=np.float32)[:, None] * inv[None, :]
    cos, sin, zero = np.cos(ang), np.sin(ang), np.zeros_like(ang)
    reps = LANES // HEAD_DIM
    return (jnp.asarray(np.tile(np.concatenate([cos, cos], axis=1), (1, reps))),
            jnp.asarray(np.tile(np.concatenate([-sin, zero], axis=1), (1, reps))),
            jnp.asarray(np.tile(np.concatenate([zero, sin], axis=1), (1, reps))))


def _rope(x, cos, sa, sb, sign):
    half = HEAD_DIM // 2
    return x * cos + sign * (pltpu.roll(x, LANES - half, 1) * sa + pltpu.roll(x, half, 1) * sb)


def _mix_in_fwd(h, g, w_in, tables, *, tm=256):
    T, D = h.shape

    def body(h_ref, g_ref, w_ref, cos_ref, sa_ref, sb_ref, u_ref, qkva_ref, qb_ref, kvb_ref, gates_ref):
        hf = h_ref[...]
        u = ((hf * _rstd(hf)) * g_ref[...]).astype(BF16)
        u_ref[...] = u
        qkva_ref[...] = _dot_nt(u, w_ref[0:C_QKVA, :]).astype(BF16)
        zr = _dot_nt(u, w_ref[O_QB:O_QB + C_ROPE, :])
        cos, sa, sb = cos_ref[...], sa_ref[...], sb_ref[...]
        for j in range(C_ROPE // LANES):
            rj = _rope(zr[:, j * LANES:(j + 1) * LANES], cos, sa, sb, 1.0).astype(BF16)
            if j < C_QB // LANES:
                qb_ref[:, j * LANES:(j + 1) * LANES] = rj
            else:
                kvb_ref[:, 0:C_KB] = rj
        kvb_ref[:, C_KB:2 * C_KB] = _dot_nt(u, w_ref[O_VB:O_VB + C_KB, :]).astype(BF16)
        gates_ref[...] = _dot_nt(u, w_ref[O_G:O_G + C_GATES, :])

    def tok(n):
        return pl.BlockSpec((tm, n), lambda i: (i, 0))

    return pl.pallas_call(
        body, name="mix_in_fwd", grid=(T // tm,),
        in_specs=[tok(D), pl.BlockSpec((1, D), lambda i: (0, 0)), pl.BlockSpec((D_IN, D), lambda i: (0, 0)),
                  tok(LANES), tok(LANES), tok(LANES)],
        out_specs=[tok(D), tok(C_QKVA), tok(C_QB), tok(2 * C_KB), tok(C_GATES)],
        out_shape=[jax.ShapeDtypeStruct((T, D), BF16), jax.ShapeDtypeStruct((T, C_QKVA), BF16),
                   jax.ShapeDtypeStruct((T, C_QB), BF16), jax.ShapeDtypeStruct((T, 2 * C_KB), BF16),
                   jax.ShapeDtypeStruct((T, C_GATES), F32)],
        compiler_params=_params(1),
    )(h, g, w_in, *tables)


def _mix_in_bwd(dqa, dka, dva, dqb, dkb, dvb, dgates, h, g, dres, w_in, tables, *, tm=256):
    T, D = h.shape

    def body(dqa_ref, dka_ref, dva_ref, dqb_ref, dkb_ref, dvb_ref, dgt_ref, h_ref, g_ref, dres_ref, w_ref,
             cos_ref, sa_ref, sb_ref, dz_ref, dh_ref, dg_ref):
        @pl.when(pl.program_id(0) == 0)
        def _():
            dg_ref[...] = jnp.zeros_like(dg_ref)

        na = NA_HEADS * HEAD_DIM
        dz_ref[:, 0:na] = dqa_ref[...].astype(BF16)
        dz_ref[:, na:2 * na] = dka_ref[...].astype(BF16)
        dz_ref[:, 2 * na:3 * na] = dva_ref[...].astype(BF16)
        cos, sa, sb = cos_ref[...], sa_ref[...], sb_ref[...]
        for j in range(C_QB // LANES):
            dz_ref[:, O_QB + j * LANES:O_QB + (j + 1) * LANES] = _rope(
                dqb_ref[:, j * LANES:(j + 1) * LANES], cos, sa, sb, -1.0).astype(BF16)
        dz_ref[:, O_KB:O_KB + C_KB] = _rope(dkb_ref[...], cos, sa, sb, -1.0).astype(BF16)
        dz_ref[:, O_VB:O_VB + C_KB] = dvb_ref[...].astype(BF16)
        dz_ref[:, O_G:O_G + C_GATES] = dgt_ref[...].astype(BF16)
        du = _dot(dz_ref[...], w_ref[...])
        hf = h_ref[...]
        dx, dgr = _norm_bwd(du, hf, g_ref[...], _rstd(hf))
        dg_ref[...] += jnp.sum(dgr, axis=0, keepdims=True)
        dh_ref[...] = dres_ref[...] + dx

    def tok(n):
        return pl.BlockSpec((tm, n), lambda i: (i, 0))

    vec = pl.BlockSpec((1, D), lambda i: (0, 0))
    na = NA_HEADS * HEAD_DIM
    return pl.pallas_call(
        body, name="mix_in_bwd", grid=(T // tm,),
        in_specs=[tok(na), tok(na), tok(na), tok(C_QB), tok(C_KB), tok(C_KB), tok(C_GATES), tok(D), vec, tok(D),
                  pl.BlockSpec((D_IN, D), lambda i: (0, 0)), tok(LANES), tok(LANES), tok(LANES)],
        out_specs=[tok(D_IN), tok(D), vec],
        out_shape=[jax.ShapeDtypeStruct((T, D_IN), BF16), jax.ShapeDtypeStruct((T, D), F32),
                   jax.ShapeDtypeStruct((1, D), F32)],
        compiler_params=_params(1),
    )(dqa, dka, dva, dqb, dkb, dvb, dgates, h, g, dres, w_in, *tables)


def _na_bias_slabs(rpb):
    H = rpb.shape[0]
    ncell = GRID_W * GRID_W
    cell = np.arange(ncell)
    co = cell % GRID_W - cell // GRID_W + (NA_KW - 1)
    e_co = jnp.asarray((np.arange(LANES)[:, None] == co[None, :]).astype(np.float32))
    table = jnp.pad(rpb, ((0, 0), (0, 1), (0, LANES - rpb.shape[2]))).reshape(H * 2 * NA_KH, LANES)

    def body(t_ref, e_ref, o_ref):
        o_ref[...] = jnp.dot(t_ref[...], e_ref[...], preferred_element_type=F32, precision=lax.Precision.HIGHEST)

    toeplitz = pl.pallas_call(
        body, name="rpb_unfold", out_shape=jax.ShapeDtypeStruct((H * 2 * NA_KH, ncell), F32),
        compiler_params=_params(0),
    )(table, e_co).reshape(H, 2 * NA_KH, GRID_W, GRID_W)

    def assemble(tz_ref, o_ref):
        c = lax.broadcasted_iota(jnp.int32, (GRID_W, GRID_W), 0)
        k = lax.broadcasted_iota(jnp.int32, (GRID_W, GRID_W), 1)
        cs = jnp.clip(c - NA_KW // 2, 0, GRID_W - NA_KW)
        inwin = (k >= cs) & (k < cs + NA_KW)
        for ro0 in range(NA_KH):
            for hh in range(2):
                for i in range(NA_KH):
                    o_ref[0, ro0, hh * GRID_W:(hh + 1) * GRID_W, i * GRID_W:(i + 1) * GRID_W] = jnp.where(
                        inwin, tz_ref[hh, ro0 + i], NEG)

    return pl.pallas_call(
        assemble, name="na_bias_slabs", grid=(H // 2,),
        in_specs=[pl.BlockSpec((2, 2 * NA_KH, GRID_W, GRID_W), lambda p: (p, 0, 0, 0))],
        out_specs=pl.BlockSpec((1, NA_KH, 2 * GRID_W, NA_KH * GRID_W), lambda p: (p, 0, 0, 0)),
        out_shape=jax.ShapeDtypeStruct((H // 2, NA_KH, 2 * GRID_W, NA_KH * GRID_W), F32),
        compiler_params=_params(1),
    )(toeplitz)


def _na_unstack_slabs(dslab):
    pairs = dslab.shape[0]
    d = dslab.reshape(pairs, NA_KH, 2, GRID_W, NA_KH * GRID_W).transpose(0, 2, 1, 3, 4)
    return d.reshape(2 * pairs, NA_KH, GRID_W, NA_KH * GRID_W)


def _half_masks(rows):
    lane = lax.broadcasted_iota(jnp.int32, (rows, LANES), 1)
    left = lane < HEAD_DIM
    return left, (left, jnp.logical_not(left))


def _stack_heads(x):
    left, halves = _half_masks(x.shape[0])
    xf = x.astype(F32)
    return jnp.concatenate([jnp.where(m, xf, 0.0).astype(BF16) for m in halves], axis=0)


def _unstack_heads(o):
    rows = o.shape[0] // 2
    left, _ = _half_masks(rows)
    return jnp.where(left, o[:rows], o[rows:])


def _na_row(j, t, rb, rows):
    r = j * rb + t
    rs = jnp.clip(r - NA_KH // 2, 0, rows - NA_KH)
    return pl.multiple_of(t * GRID_W, GRID_W), pl.multiple_of(rs * GRID_W, GRID_W), rs - r + (NA_KH - 1)


def _na_specs(T, rb):
    qrows = GRID_W * rb
    pairs = NA_HEADS // 2
    return ([pl.BlockSpec((qrows, LANES), lambda p, j: (j, p)),
             pl.BlockSpec((T, LANES), lambda p, j: (0, pairs + p)),
             pl.BlockSpec((T, LANES), lambda p, j: (0, 2 * pairs + p))],
            pl.BlockSpec((1, NA_KH, 2 * GRID_W, NA_KH * GRID_W), lambda p, j: (p, 0, 0, 0)))


def _softmax(s):
    p = jnp.exp(s - jnp.max(s, axis=-1, keepdims=True))
    return p / jnp.sum(p, axis=-1, keepdims=True)


def _na_probs(qs, ks, bias):
    return _softmax(_dot_nt(qs, ks) * QK_SCALE + bias)


def _na_fwd(qkva, bias, *, rb=8, group=8):
    T = qkva.shape[0]
    rows = T // GRID_W
    nkeys = NA_KH * GRID_W

    def body(q_ref, k_ref, v_ref, bias_ref, y_ref):
        j = pl.program_id(1)

        def rows_step(t, carry):
            at = [_na_row(j, t * group + u, rb, rows) for u in range(group)]
            s = [_dot_nt(_stack_heads(q_ref[pl.ds(q0, GRID_W), :]), k_ref[pl.ds(k0, nkeys), :]) for q0, k0, _ in at]
            p = [_softmax(su * QK_SCALE + bias_ref[0, ro0]) for su, (_, _, ro0) in zip(s, at)]
            o = [_dot(pu.astype(BF16), v_ref[pl.ds(k0, nkeys), :]) for pu, (_, k0, _) in zip(p, at)]
            for ou, (q0, _, _) in zip(o, at):
                y_ref[pl.ds(q0, GRID_W), :] = _unstack_heads(ou).astype(BF16)
            return carry

        lax.fori_loop(0, rb // group, rows_step, 0)

    qkv_specs, bias_spec = _na_specs(T, rb)
    return pl.pallas_call(
        body, name="na_fwd", grid=(NA_HEADS // 2, rows // rb),
        in_specs=qkv_specs + [bias_spec],
        out_specs=qkv_specs[0],
        out_shape=jax.ShapeDtypeStruct((T, NA_HEADS * HEAD_DIM), BF16),
        compiler_params=_params(2),
    )(qkva, qkva, qkva, bias)


def _na_bwd(qkva, dy, bias, *, rb=8, group=8, rider=None):
    T = qkva.shape[0]
    rows = T // GRID_W
    nkeys = NA_KH * GRID_W

    def body(q_ref, k_ref, v_ref, dy_ref, bias_ref, dq_ref, dk_ref, dv_ref, dbias_ref):
        j = pl.program_id(1)

        @pl.when(j == 0)
        def _():
            dk_ref[...] = jnp.zeros_like(dk_ref)
            dv_ref[...] = jnp.zeros_like(dv_ref)
            dbias_ref[...] = jnp.zeros_like(dbias_ref)

        def rows_step(t, carry):
            at = [_na_row(j, t * group + u, rb, rows) for u in range(group)]
            qs = [_stack_heads(q_ref[pl.ds(q0, GRID_W), :]) for q0, _, _ in at]
            dys = [_stack_heads(dy_ref[pl.ds(q0, GRID_W), :]) for q0, _, _ in at]
            s = [_dot_nt(qu, k_ref[pl.ds(k0, nkeys), :]) for qu, (_, k0, _) in zip(qs, at)]
            dp = [_dot_nt(du, v_ref[pl.ds(k0, nkeys), :]) for du, (_, k0, _) in zip(dys, at)]
            p = [_softmax(su * QK_SCALE + bias_ref[0, ro0]) for su, (_, _, ro0) in zip(s, at)]
            ds = [pu * (du - jnp.sum(pu * du, axis=-1, keepdims=True)) for pu, du in zip(p, dp)]
            for u, (q0, k0, ro0) in enumerate(at):
                dbias_ref[0, ro0] += ds[u]
                dsb = ds[u].astype(BF16)
                dq_ref[pl.ds(q0, GRID_W), :] = (_unstack_heads(_dot(dsb, k_ref[pl.ds(k0, nkeys), :])) * QK_SCALE).astype(BF16)
                dk_ref[pl.ds(k0, nkeys), :] += _dot_tn(dsb, qs[u]) * QK_SCALE
                dv_ref[pl.ds(k0, nkeys), :] += _dot_tn(p[u].astype(BF16), dys[u])
            return carry

        lax.fori_loop(0, rb // group, rows_step, 0)

    qkv_specs, bias_spec = _na_specs(T, rb)
    width = NA_HEADS * HEAD_DIM
    kv_out = pl.BlockSpec((T, LANES), lambda p, j: (0, p))
    grid = (NA_HEADS // 2, rows // rb)
    body, r_in, r_in_specs, r_out, r_out_specs, scratch = _ride(body, 5, 4, rider, grid, None)
    outs = pl.pallas_call(
        body, name="na_bwd", grid=grid,
        in_specs=qkv_specs + [qkv_specs[0], bias_spec] + r_in_specs,
        out_specs=[qkv_specs[0], kv_out, kv_out, bias_spec] + r_out_specs,
        out_shape=[jax.ShapeDtypeStruct((T, width), BF16), jax.ShapeDtypeStruct((T, width), F32),
                   jax.ShapeDtypeStruct((T, width), F32), jax.ShapeDtypeStruct(bias.shape, F32)] + r_out,
        scratch_shapes=scratch,
        compiler_params=_params(2),
    )(qkva, qkva, qkva, dy, bias, *r_in)
    return (*outs[:4], list(outs[4:]))


def _rpb_fold(dslab):
    H = dslab.shape[0]
    nro = NA_KH * NA_KH
    ncell = GRID_W * GRID_W
    xs = dslab.reshape(H, NA_KH, GRID_W, NA_KH, GRID_W).transpose(0, 1, 3, 2, 4).reshape(H, nro, ncell)
    cell = np.arange(ncell)
    co = cell % GRID_W - cell // GRID_W + (NA_KW - 1)
    e_co = jnp.asarray((co[:, None] == np.arange(LANES)[None, :]).astype(np.float32))
    pair = np.arange(nro)
    e_ro = jnp.asarray(((pair // NA_KH + pair % NA_KH)[None, :] == np.arange(2 * NA_KH)[:, None]).astype(np.float32))

    def body(x_ref, eco_ref, ero_ref, o_ref):
        y = jnp.dot(x_ref[0], eco_ref[...], preferred_element_type=F32, precision=lax.Precision.HIGHEST)
        o_ref[0] = jnp.dot(ero_ref[...], y, preferred_element_type=F32, precision=lax.Precision.HIGHEST)

    return pl.pallas_call(
        body, name="rpb_fold", grid=(H,),
        in_specs=[pl.BlockSpec((1, nro, ncell), lambda h: (h, 0, 0)), pl.BlockSpec((ncell, LANES), lambda h: (0, 0)),
                  pl.BlockSpec((2 * NA_KH, nro), lambda h: (0, 0))],
        out_specs=pl.BlockSpec((1, 2 * NA_KH, LANES), lambda h: (h, 0, 0)),
        out_shape=jax.ShapeDtypeStruct((H, 2 * NA_KH, LANES), F32),
        compiler_params=_params(1),
    )(xs, e_co, e_ro)


SWA_KEYS = 3 * WIN


def _swa_block(j, t, qbn, T):
    blk = j * qbn + t
    start = jnp.clip((blk - 1) * WIN, 0, T - SWA_KEYS)
    row = lax.broadcasted_iota(jnp.int32, (2 * WIN, SWA_KEYS), 0)
    qpos = blk * WIN + jnp.where(row < WIN, row, row - WIN)
    kpos = start + lax.broadcasted_iota(jnp.int32, (2 * WIN, SWA_KEYS), 1)
    return pl.multiple_of(t * WIN, WIN), pl.multiple_of(start, WIN), jnp.abs(qpos - kpos) <= WIN


def _swa_sinks(sink_ref, p):
    row = lax.broadcasted_iota(jnp.int32, (2 * WIN, 1), 0)
    return jnp.where(row < WIN, sink_ref[p], sink_ref[p + NB_HEADS // 2])


def _swa_probs(s, mask, sink):
    s = jnp.where(mask, s * QK_SCALE, NEG)
    m = jnp.maximum(jnp.max(s, axis=-1, keepdims=True), sink)
    e = jnp.exp(s - m)
    esink = jnp.exp(sink - m)
    den = jnp.sum(e, axis=-1, keepdims=True) + esink
    return e / den, esink / den


def _swa_specs(T, qbn):
    return [pl.BlockSpec(memory_space=pltpu.SMEM),
            pl.BlockSpec((WIN * qbn, LANES), lambda p, j: (j, p)),
            pl.BlockSpec((T, LANES), lambda p, j: (0, 0)),
            pl.BlockSpec((T, LANES), lambda p, j: (0, 1))]


def _swa_fwd(qb, kvb, sink, *, qbn=8, group=8):
    T = qb.shape[0]
    pairs = NB_HEADS // 2
    qbn = min(qbn, T // WIN)
    group = min(group, qbn)

    def body(sink_ref, q_ref, k_ref, v_ref, y_ref):
        p, j = pl.program_id(0), pl.program_id(1)
        sinks = _swa_sinks(sink_ref, p)

        def blocks_step(t, carry):
            at = [_swa_block(j, t * group + u, qbn, T) for u in range(group)]
            s = [_dot_nt(_stack_heads(q_ref[pl.ds(q0, WIN), :]), k_ref[pl.ds(k0, SWA_KEYS), :]) for q0, k0, _ in at]
            pr = [_swa_probs(su, mask, sinks)[0] for su, (_, _, mask) in zip(s, at)]
            o = [_dot(pu.astype(BF16), v_ref[pl.ds(k0, SWA_KEYS), :]) for pu, (_, k0, _) in zip(pr, at)]
            for ou, (q0, _, _) in zip(o, at):
                y_ref[pl.ds(q0, WIN), :] = _unstack_heads(ou).astype(BF16)
            return carry

        lax.fori_loop(0, qbn // group, blocks_step, 0)

    specs = _swa_specs(T, qbn)
    return pl.pallas_call(
        body, name="swa_fwd", grid=(pairs, T // (WIN * qbn)),
        in_specs=specs, out_specs=specs[1],
        out_shape=jax.ShapeDtypeStruct((T, NB_HEADS * HEAD_DIM), BF16),
        compiler_params=_params(2),
    )(sink, qb, kvb, kvb)


def _swa_bwd(qb, kvb, dy, sink, *, qbn=8, group=8, rider=None):
    T = qb.shape[0]
    pairs = NB_HEADS // 2
    qbn = min(qbn, T // WIN)
    group = min(group, qbn)

    def body(sink_ref, q_ref, k_ref, v_ref, dy_ref, dq_ref, dk_ref, dv_ref, dsink_ref):
        p, j = pl.program_id(0), pl.program_id(1)
        sinks = _swa_sinks(sink_ref, p)

        @pl.when((p == 0) & (j == 0))
        def _():
            dk_ref[...] = jnp.zeros_like(dk_ref)
            dv_ref[...] = jnp.zeros_like(dv_ref)

        @pl.when(j == 0)
        def _():
            dsink_ref[...] = jnp.zeros_like(dsink_ref)

        def blocks_step(t, carry):
            at = [_swa_block(j, t * group + u, qbn, T) for u in range(group)]
            qs = [_stack_heads(q_ref[pl.ds(q0, WIN), :]) for q0, _, _ in at]
            dys = [_stack_heads(dy_ref[pl.ds(q0, WIN), :]) for q0, _, _ in at]
            s = [_dot_nt(qu, k_ref[pl.ds(k0, SWA_KEYS), :]) for qu, (_, k0, _) in zip(qs, at)]
            dp = [_dot_nt(du, v_ref[pl.ds(k0, SWA_KEYS), :]) for du, (_, k0, _) in zip(dys, at)]
            probs = [_swa_probs(su, mask, sinks) for su, (_, _, mask) in zip(s, at)]
            for u, (q0, k0, _) in enumerate(at):
                pr, psink = probs[u]
                delta = jnp.sum(pr * dp[u], axis=-1, keepdims=True)
                dsb = (pr * (dp[u] - delta)).astype(BF16)
                dsk = psink * delta
                for hh in range(2):
                    dsink_ref[0, hh:hh + 1, :] += jnp.broadcast_to(-jnp.sum(dsk[hh * WIN:(hh + 1) * WIN]), (1, LANES))
                dq_ref[pl.ds(q0, WIN), :] = _unstack_heads(_dot(dsb, k_ref[pl.ds(k0, SWA_KEYS), :])) * QK_SCALE
                dk_ref[pl.ds(k0, SWA_KEYS), :] += _dot_tn(dsb, qs[u]) * QK_SCALE
                dv_ref[pl.ds(k0, SWA_KEYS), :] += _dot_tn(pr.astype(BF16), dys[u])
            return carry

        lax.fori_loop(0, qbn // group, blocks_step, 0)

    specs = _swa_specs(T, qbn)
    kv_out = pl.BlockSpec((T, LANES), lambda p, j: (0, 0))
    grid = (pairs, T // (WIN * qbn))
    body, r_in, r_in_specs, r_out, r_out_specs, scratch = _ride(body, 5, 4, rider, grid, None)
    outs = pl.pallas_call(
        body, name="swa_bwd", grid=grid,
        in_specs=specs + [specs[1]] + r_in_specs,
        out_specs=[specs[1], kv_out, kv_out, pl.BlockSpec((1, 8, LANES), lambda p, j: (p, 0, 0))] + r_out_specs,
        out_shape=[jax.ShapeDtypeStruct((T, NB_HEADS * HEAD_DIM), F32), jax.ShapeDtypeStruct((T, LANES), F32),
                   jax.ShapeDtypeStruct((T, LANES), F32), jax.ShapeDtypeStruct((pairs, 8, LANES), F32)] + r_out,
        scratch_shapes=scratch,
        compiler_params=_params(2),
    )(sink, qb, kvb, kvb, dy, *r_in)
    return (*outs[:4], list(outs[4:]))


def _merge_fwd(ya, yb, gates, wa, wb, wout, h, *, tm=512):
    T, D = h.shape
    W = ya.shape[1]

    def body(ya_ref, yb_ref, gt_ref, wa_ref, wb_ref, wo_ref, h_ref, h2_ref, mg_ref):
        pa = _dot(ya_ref[...], wa_ref[...])
        pb = _dot(yb_ref[...], wb_ref[...])
        mg = (jax.nn.sigmoid(gt_ref[:, 0:D]) * pa + jax.nn.sigmoid(gt_ref[:, D:2 * D]) * pb).astype(BF16)
        mg_ref[...] = mg
        h2_ref[...] = h_ref[...] + _dot(mg, wo_ref[...])

    def tok(n):
        return pl.BlockSpec((tm, n), lambda i: (i, 0))

    def full(r, c):
        return pl.BlockSpec((r, c), lambda i: (0, 0))

    return pl.pallas_call(
        body, name="merge_fwd", grid=(T // tm,),
        in_specs=[tok(W), tok(W), tok(2 * D), full(W, D), full(W, D), full(D, D), tok(D)],
        out_specs=[tok(D), tok(D)],
        out_shape=[jax.ShapeDtypeStruct((T, D), F32), jax.ShapeDtypeStruct((T, D), BF16)],
        compiler_params=_params(1),
    )(ya, yb, gates, wa, wb, wout, h)


def _merge_bwd(dh, ya, yb, gates, wa, wb, wout, *, tm=512, rider=None):
    T, D = dh.shape
    W = ya.shape[1]

    def body(dh_ref, ya_ref, yb_ref, gt_ref, wa_ref, wb_ref, wo_ref, dya_ref, dyb_ref, dpa_ref, dpb_ref, dgt_ref):
        dmg = _dot_nt(dh_ref[...].astype(BF16), wo_ref[...])
        for y_ref, w_ref, dy_ref, dp_ref, lo in ((ya_ref, wa_ref, dya_ref, dpa_ref, 0), (yb_ref, wb_ref, dyb_ref, dpb_ref, D)):
            sg = jax.nn.sigmoid(gt_ref[:, lo:lo + D])
            dp = (dmg * sg).astype(BF16)
            dp_ref[...] = dp
            dgt_ref[:, lo:lo + D] = (dmg * _dot(y_ref[...], w_ref[...]) * (sg * (1.0 - sg))).astype(BF16)
            dy_ref[...] = _dot_nt(dp, w_ref[...]).astype(BF16)

    def tok(n):
        return pl.BlockSpec((tm, n), lambda i: (i, 0))

    def full(r, c):
        return pl.BlockSpec((r, c), lambda i: (0, 0))

    grid = (T // tm,)
    body, r_in, r_in_specs, r_out, r_out_specs, scratch = _ride(body, 7, 5, rider, grid, None)
    outs = pl.pallas_call(
        body, name="merge_bwd", grid=grid,
        in_specs=[tok(D), tok(W), tok(W), tok(2 * D), full(W, D), full(W, D), full(D, D)] + r_in_specs,
        out_specs=[tok(W), tok(W), tok(D), tok(D), tok(2 * D)] + r_out_specs,
        out_shape=[jax.ShapeDtypeStruct((T, W), BF16), jax.ShapeDtypeStruct((T, W), BF16),
                   jax.ShapeDtypeStruct((T, D), BF16), jax.ShapeDtypeStruct((T, D), BF16),
                   jax.ShapeDtypeStruct((T, 2 * D), BF16)] + r_out,
        scratch_shapes=scratch,
        compiler_params=_params(1),
    )(dh, ya, yb, gates, wa, wb, wout, *r_in)
    return (*outs[:5], list(outs[5:]))


def _pair_heads(a, axis):
    shp = a.shape
    a = a.reshape(shp[:axis] + (2, NB_HEADS // 2, HEAD_DIM) + shp[axis + 1:])
    return jnp.swapaxes(a, axis, axis + 1).reshape(shp)


def _unpair_heads(a, axis):
    shp = a.shape
    a = a.reshape(shp[:axis] + (NB_HEADS // 2, 2, HEAD_DIM) + shp[axis + 1:])
    return jnp.swapaxes(a, axis, axis + 1).reshape(shp)


FFN1 = ("ffn1_w_gate", "ffn1_w_up", "ffn1_w_down")
FFN2 = ("ffn2_w_gate", "ffn2_w_up", "ffn2_w_down")
MIXER = ("w_in", "w_branch_a", "w_branch_b", "w_out")


def _layer_grads(x, target, g1, f1, gmix, late, rpb, sink, g2, gfin, comm=None):
    T = x.shape[0]
    tables = _rope_tables(T)
    bias = _na_bias_slabs(rpb)

    h1, n1, hdn1, p1, q1, gathered = _ffn_fwd(x, g1, *f1, name="ffn1_fwd", rider=comm.late_rider if comm else None)
    w_in_t, wa, wb, wout, f2 = comm.late(gathered) if comm else late
    w_in_p = jnp.concatenate([w_in_t[:O_QB], _pair_heads(w_in_t[O_QB:O_KB], 0), w_in_t[O_KB:]], axis=0)
    wb_p = _pair_heads(wb, 0)
    u, qkva, qb, kvb, gates = _mix_in_fwd(h1, gmix, w_in_p, tables)
    ya = _na_fwd(qkva, bias)
    yb = _swa_fwd(qb, kvb, sink)
    h2, merged = _merge_fwd(ya, yb, gates, wa, wb_p, wout, h1)
    dh3, n2, hdn2, p2, q2, loss, dgfin, _ = _ffn_fwd(h2, g2, *f2, name="ffn2_fwd", head=(gfin, target))

    dh2, da2, db2, dg2, _ = _ffn_bwd(dh3, h2, g2, p2, q2, *f2, name="ffn2_bwd")
    df2 = [_wgrad_shard_a(da2, n2, name="ffn2_dwg"), _wgrad_shard_a(db2, n2, name="ffn2_dwu"),
           _wgrad_shard_a(hdn2, dh3, scale=0.5, name="ffn2_dwd")]
    red2 = comm.reduce(FFN2, df2, tag="ffn2") if comm else None
    dya, dyb, dpa, dpb, dgates, got = _merge_bwd(dh2, ya, yb, gates, wa, wb_p, wout, rider=red2.sibling if comm else None)
    dwout = _wgrad_cols(merged, dh2, 1, name="dwout").reshape(N_CHIPS, D_MODEL // N_CHIPS, D_MODEL)
    dwa = _wgrad_cols(ya, dpa, N_CHIPS, name="dwa")
    dwb = _unpair_heads(_wgrad_cols(yb, dpb, N_CHIPS, name="dwb"), 1)
    dqa, dka, dva, dbias, got = _na_bwd(qkva, dya, bias, rider=red2.partial(got).chips if comm else None)
    drpb = _rpb_fold(_na_unstack_slabs(dbias))
    dqb, dkb, dvb, dsink, got = _swa_bwd(qb, kvb, dyb, sink, rider=red2.halves(got).share if comm else None)
    reduced2 = red2.result(got) if comm else None
    dz, dh1, dgmix = _mix_in_bwd(dqa, dka, dva, dqb, dkb, dvb, dgates, h1, gmix, dh2, w_in_p, tables)
    dwin_p = _wgrad_rows(dz, u, 2, name="dwin").reshape(D_IN, D_MODEL)
    dwin = jnp.concatenate([dwin_p[:O_QB], _unpair_heads(dwin_p[O_QB:O_KB], 0), dwin_p[O_KB:]], axis=0)
    dmix = [dwin.reshape(N_CHIPS, D_IN // N_CHIPS, D_MODEL), dwa, dwb, dwout]
    dx, da1, db1, dg1, _ = _ffn_bwd(dh1, x, g1, p1, q1, *f1, name="ffn1_bwd")
    out = dict(loss=loss, dx=dx, ffn1_norm=dg1, mix_norm=dgmix, ffn2_norm=dg2, final_norm=dgfin, na_rpb=drpb,
               sink_logit=dsink[:, 0:2, 0].T.reshape(NB_HEADS))
    if comm:
        out.update(reduced2)
        redm = comm.reduce(MIXER, dmix, tag="mixer").partial_now()
        dwg1, got = _wgrad_shard_a(da1, n1, name="ffn1_dwg", rider=redm.chips)
        dwu1, got = _wgrad_shard_a(db1, n1, name="ffn1_dwu", rider=redm.halves(got).share)
        out.update(redm.result(got))
        red1 = comm.reduce(FFN1[:2], [dwg1, dwu1], tag="ffn1_gate_up").partial_now()
        dwd1, got = _wgrad_shard_a(hdn1, dh1, scale=0.5, name="ffn1_dwd", rider=red1.chips)
        out.update(red1.halves(got).result_now())
        out.update(comm.reduce(FFN1[2:], [dwd1], tag="ffn1_down").partial_now().halves_now().result_now())
    else:
        df1 = [_wgrad_shard_a(da1, n1, name="ffn1_dwg"), _wgrad_shard_a(db1, n1, name="ffn1_dwu"),
               _wgrad_shard_a(hdn1, dh1, scale=0.5, name="ffn1_dwd")]
        out.update(zip(FFN1 + MIXER + FFN2, df1 + dmix + df2))
    return out


ANY = pl.BlockSpec(memory_space=pl.ANY)


def _place():
    x, y, c = lax.axis_index("x"), lax.axis_index("y"), lax.axis_index("c")
    chips = [(1 - x, y), (x, 1 - y), (1 - x, 1 - y)]
    return x, y, c, 2 * x + y, chips


def _remote(src, dst, send_sems, recv_sems, k, device):
    return pltpu.make_async_remote_copy(src_ref=src, dst_ref=dst, send_sem=send_sems.at[k], recv_sem=recv_sems.at[k],
                                        device_id=device, device_id_type=MESH)


class _Rider:
    def __init__(self, inputs, out_shape, scratch, start, middle, finish):
        self.inputs, self.out_shape, self.scratch = list(inputs), list(out_shape), list(scratch)
        self.start, self.middle, self.finish = start, middle, finish


def _run_rider(rider, *, name):
    n_in, n_out = len(rider.inputs), len(rider.out_shape)

    def body(*refs):
        ins, outs, sems = refs[:n_in], refs[n_in:n_in + n_out], refs[n_in + n_out:]
        rider.start(ins, outs, sems)
        if rider.middle is not None:
            rider.middle(ins, outs, sems)
        rider.finish(ins, outs, sems)

    return pl.pallas_call(body, name=name, in_specs=[ANY] * n_in, out_specs=[ANY] * n_out, out_shape=rider.out_shape,
                          scratch_shapes=rider.scratch)(*rider.inputs)


def _ride(body, n_in, n_out, rider, grid, middle_step):
    if rider is None:
        return body, [], [], [], [], []
    r_in, r_out = len(rider.inputs), len(rider.out_shape)
    steps = math.prod(grid)

    def riding(*refs):
        ins, r_ins = refs[:n_in], refs[n_in:n_in + r_in]
        outs = refs[n_in + r_in:n_in + r_in + n_out]
        r_outs = refs[n_in + r_in + n_out:n_in + r_in + n_out + r_out]
        sems = refs[n_in + r_in + n_out + r_out:]
        step = pl.program_id(0)
        for axis in range(1, len(grid)):
            step = step * grid[axis] + pl.program_id(axis)

        @pl.when(step == 0)
        def _():
            rider.start(r_ins, r_outs, sems)

        body(*ins, *outs)

        if rider.middle is not None:
            @pl.when(step == middle_step)
            def _():
                rider.middle(r_ins, r_outs, sems)

        @pl.when(step == steps - 1)
        def _():
            rider.finish(r_ins, r_outs, sems)

    return riding, rider.inputs, [ANY] * r_in, rider.out_shape, [ANY] * r_out, rider.scratch


def _gather_rider(shards):
    n = len(shards)

    def plan(ins, outs, sems, kinds):
        send_sems, recv_sems, own_send_sems, own_recv_sems = sems
        x, y, c, mine, chips = _place()
        sibling = (x, y, 1 - c)
        made = {k: [] for k in kinds}
        for i in range(n):
            hr = shards[i].shape[0] // 2
            if "own" in made:
                made["own"].append(_remote(ins[i], outs[i].at[mine], own_send_sems, own_recv_sems, i, sibling))
            for j, (cx, cy) in enumerate(chips):
                here = outs[i].at[2 * cx + cy, pl.ds(c * hr, hr)]
                there = outs[i].at[2 * cx + cy, pl.ds((1 - c) * hr, hr)]
                if "sends" in made:
                    made["sends"].append(_remote(ins[i].at[pl.ds(c * hr, hr)], outs[i].at[mine, pl.ds(c * hr, hr)],
                                                 send_sems, recv_sems, 6 * i + j, (cx, cy, c)))
                if "landed" in made:
                    made["landed"].append(_remote(here, here, send_sems, recv_sems, 6 * i + j, (cx, cy, c)))
                if "passes" in made:
                    made["passes"].append(_remote(here, here, send_sems, recv_sems, 6 * i + 3 + j, sibling))
                if "others" in made:
                    made["others"].append(_remote(there, there, send_sems, recv_sems, 6 * i + 3 + j, sibling))
        return [made[k] for k in kinds]

    def start(ins, outs, sems):
        own, sends = plan(ins, outs, sems, ("own", "sends"))
        for cp in own + sends:
            cp.start()

    def middle(ins, outs, sems):
        landed, passes = plan(ins, outs, sems, ("landed", "passes"))
        for arrived, cp in zip(landed, passes):
            arrived.wait_recv()
            cp.start()

    def finish(ins, outs, sems):
        own, sends, passes, others = plan(ins, outs, sems, ("own", "sends", "passes", "others"))
        for arrived in others:
            arrived.wait_recv()
        for cp in sends + passes:
            cp.wait_send()
        for cp in own:
            cp.wait()

    return _Rider(shards, [jax.ShapeDtypeStruct((N_CHIPS,) + s.shape, s.dtype) for s in shards],
                  [pltpu.SemaphoreType.DMA((6 * n,)), pltpu.SemaphoreType.DMA((6 * n,)),
                   pltpu.SemaphoreType.DMA((n,)), pltpu.SemaphoreType.DMA((n,))], start, middle, finish)


def _swap_rider(arrays, out_shape, source):
    n = len(arrays)

    def plan(ins, outs, sems):
        send_sems, recv_sems = sems
        x, y, c, _, _ = _place()
        return [_remote(source(ins[i], c, i), outs[i], send_sems, recv_sems, i, (x, y, 1 - c)) for i in range(n)]

    def start(ins, outs, sems):
        for cp in plan(ins, outs, sems):
            cp.start()

    def finish(ins, outs, sems):
        for cp in plan(ins, outs, sems):
            cp.wait()

    return _Rider(arrays, out_shape, [pltpu.SemaphoreType.DMA((n,)), pltpu.SemaphoreType.DMA((n,))], start, None, finish)


def _sibling_rider(grads):
    half = [g.shape[1] // 2 for g in grads]
    return _swap_rider(grads, [jax.ShapeDtypeStruct((g.shape[0], hr, g.shape[2]), g.dtype) for g, hr in zip(grads, half)],
                       lambda ref, c, i: ref.at[:, pl.ds((1 - c) * half[i], half[i])])


def _share_rider(halves):
    return _swap_rider(halves, [jax.ShapeDtypeStruct(h.shape, h.dtype) for h in halves], lambda ref, c, i: ref)


def _chips_rider(parts):
    n = len(parts)

    def plan(ins, outs, sems):
        send_sems, recv_sems = sems
        _, _, c, _, chips = _place()
        return [_remote(ins[i].at[2 * cx + cy], outs[i].at[j], send_sems, recv_sems, 3 * i + j, (cx, cy, c))
                for i in range(n) for j, (cx, cy) in enumerate(chips)]

    def start(ins, outs, sems):
        for cp in plan(ins, outs, sems):
            cp.start()

    def finish(ins, outs, sems):
        for cp in plan(ins, outs, sems):
            cp.wait()

    return _Rider(parts, [jax.ShapeDtypeStruct((N_CHIPS - 1,) + p.shape[1:], p.dtype) for p in parts],
                  [pltpu.SemaphoreType.DMA((3 * n,)), pltpu.SemaphoreType.DMA((3 * n,))], start, None, finish)


class _Reduce:
    def __init__(self, names, grads, cidx, chip, *, tag):
        self.names, self.grads, self.cidx, self.chip, self.tag = names, grads, cidx, chip, tag
        self.sibling = _sibling_rider(grads)

    def partial(self, from_sibling):
        self.parts = [_add_sibling(g, r, self.cidx, name="add_sibling_" + k)
                      for k, g, r in zip(self.names, self.grads, from_sibling)]
        self.chips = _chips_rider([p16 for _, p16 in self.parts])
        return self

    def halves(self, from_chips):
        self.mine = [_add_chips(p, r, self.chip, name="add_chips_" + k)
                     for k, (p, _), r in zip(self.names, self.parts, from_chips)]
        self.share = _share_rider(self.mine)
        return self

    def result(self, others):
        return dict(zip(self.names, zip(self.mine, others)))

    def partial_now(self):
        return self.partial(_run_rider(self.sibling, name="rs_sibling_" + self.tag))

    def halves_now(self):
        return self.halves(_run_rider(self.chips, name="rs_chips_" + self.tag))

    def result_now(self):
        return self.result(_run_rider(self.share, name="rs_share_" + self.tag))


N_DEV = 8


def _small_allreduce(vec):
    R = vec.shape[0]

    def body(v_ref, o_ref, buf, send_sems, recv_sems):
        x, y, c, _, _ = _place()
        me = 4 * x + 2 * y + c
        buf[me] = v_ref[...]
        copies = []
        for k in range(1, N_DEV):
            peer = (x ^ (k >> 2), y ^ ((k >> 1) & 1), c ^ (k & 1))
            cp = _remote(v_ref, buf.at[me], send_sems, recv_sems, k - 1, peer)
            cp.start()
            copies.append(cp)
        for k, cp in enumerate(copies, start=1):
            cp.wait_send()
            landed = buf.at[me ^ k]
            _remote(landed, landed, send_sems, recv_sems, k - 1, (x, y, c)).wait_recv()
        acc = buf[0]
        for d in range(1, N_DEV):
            acc = acc + buf[d]
        o_ref[...] = acc

    return pl.pallas_call(
        body, name="small_allreduce",
        in_specs=[pl.BlockSpec(memory_space=pltpu.VMEM)], out_specs=pl.BlockSpec(memory_space=pltpu.VMEM),
        out_shape=jax.ShapeDtypeStruct(vec.shape, vec.dtype),
        scratch_shapes=[pltpu.VMEM((N_DEV, R, LANES), F32), pltpu.SemaphoreType.DMA((N_DEV - 1,)),
                        pltpu.SemaphoreType.DMA((N_DEV - 1,))],
    )(vec)


ELEMWISE_BLOCK = 256 * 1024


def _row_tile(rows, cols):
    best = None
    for t in range(8, rows + 1, 8):
        if rows % t == 0 and t * cols <= ELEMWISE_BLOCK:
            best = t
    return best if best is not None else rows


def _add_sibling(g, r1, cidx, *, name):
    S, R, C = g.shape
    hr = R // 2
    tr = _row_tile(hr, C)
    nt = hr // tr

    def body(c_ref, g_ref, r_ref, o_ref, o16_ref):
        p = g_ref[...] + r_ref[...]
        o_ref[...] = p
        o16_ref[...] = p.astype(BF16)

    blk = pl.BlockSpec((1, tr, C), lambda s, t, c: (s, t, 0))
    return pl.pallas_call(
        body, name=name,
        grid_spec=pltpu.PrefetchScalarGridSpec(
            num_scalar_prefetch=1, grid=(S, nt),
            in_specs=[pl.BlockSpec((1, tr, C), lambda s, t, c: (s, c[0] * nt + t, 0)), blk], out_specs=[blk, blk]),
        out_shape=[jax.ShapeDtypeStruct((S, hr, C), F32), jax.ShapeDtypeStruct((S, hr, C), BF16)],
        compiler_params=_params(2),
    )(cidx, g, r1)


def _add_chips(p, r2, chip, *, name):
    _, hr, C = p.shape
    tr = _row_tile(hr, C)

    def body(chip_ref, p_ref, r_ref, o_ref):
        o_ref[...] = ((p_ref[0] + r_ref[0].astype(F32)) + r_ref[1].astype(F32)) + r_ref[2].astype(F32)

    return pl.pallas_call(
        body, name=name,
        grid_spec=pltpu.PrefetchScalarGridSpec(
            num_scalar_prefetch=1, grid=(hr // tr,),
            in_specs=[pl.BlockSpec((1, tr, C), lambda t, s: (s[0], t, 0)), pl.BlockSpec((N_CHIPS - 1, tr, C), lambda t, s: (0, t, 0))],
            out_specs=pl.BlockSpec((tr, C), lambda t, s: (t, 0))),
        out_shape=jax.ShapeDtypeStruct((hr, C), F32),
        compiler_params=_params(1),
    )(chip, p, r2)


def _adamw_math(w, g, m, v):
    mn = ADAM_B1 * m + (1.0 - ADAM_B1) * g
    vn = ADAM_B2 * v + (1.0 - ADAM_B2) * (g * g)
    m_hat = mn / (1.0 - ADAM_B1 ** ADAM_STEP)
    v_hat = vn / (1.0 - ADAM_B2 ** ADAM_STEP)
    return -ADAM_LR * (m_hat / (jnp.sqrt(v_hat) + ADAM_EPS) + ADAM_WD * w), mn, vn


def _adamw_halves(w, mine, other, m, v, cidx, *, name):
    R, C = w.shape
    hr = R // 2
    tr = _row_tile(hr, C)
    nt = hr // tr

    def body(c_ref, w_ref, a_ref, b_ref, m_ref, v_ref, g_ref, d_ref, mo_ref, vo_ref):
        gv = jnp.where(pl.program_id(0) == c_ref[0], a_ref[...], b_ref[...])
        g_ref[...] = gv
        d_ref[...], mo_ref[...], vo_ref[...] = _adamw_math(w_ref[...], gv, m_ref[...], v_ref[...])

    full = pl.BlockSpec((tr, C), lambda h, t, c: (h * nt + t, 0))
    half = pl.BlockSpec((tr, C), lambda h, t, c: (t, 0))
    shape = jax.ShapeDtypeStruct((R, C), F32)
    return pl.pallas_call(
        body, name=name,
        grid_spec=pltpu.PrefetchScalarGridSpec(
            num_scalar_prefetch=1, grid=(2, nt), in_specs=[full, half, half, full, full], out_specs=[full] * 4),
        out_shape=[shape] * 4,
        compiler_params=_params(2),
    )(cidx, w, mine, other, m, v)


def _adamw(w, g, m, v, *, name):
    R, C = w.shape
    tr = _row_tile(R, C)

    def body(w_ref, g_ref, m_ref, v_ref, d_ref, mo_ref, vo_ref):
        d_ref[...], mo_ref[...], vo_ref[...] = _adamw_math(w_ref[...], g_ref[...], m_ref[...], v_ref[...])

    blk = pl.BlockSpec((tr, C), lambda t: (t, 0))
    shape = jax.ShapeDtypeStruct((R, C), F32)
    return pl.pallas_call(
        body, name=name, grid=(R // tr,),
        in_specs=[blk] * 4, out_specs=[blk] * 3, out_shape=[shape] * 3,
        compiler_params=_params(1),
    )(w, g, m, v)


def _unstack_cols(w):
    s, r, c = w.shape
    return w.transpose(1, 0, 2).reshape(r, s * c)


def _pad_rows(a, rows):
    return jnp.pad(a, ((0, rows - a.shape[0]), (0, LANES - a.shape[1])))


BIG = ("ffn1_w_gate", "ffn1_w_up", "ffn1_w_down", "w_in", "w_branch_a", "w_branch_b", "w_out",
       "ffn2_w_gate", "ffn2_w_up", "ffn2_w_down")
TRANSPOSED = ("ffn1_w_gate", "ffn1_w_up", "w_in", "ffn2_w_gate", "ffn2_w_up")
WEIGHTS = ("ffn1_norm", "ffn1_w_gate", "ffn1_w_up", "ffn1_w_down", "mix_norm", "w_in", "na_rpb", "sink_logit",
           "w_branch_a", "w_branch_b", "w_out", "ffn2_norm", "ffn2_w_gate", "ffn2_w_up", "ffn2_w_down", "final_norm")


def kernel(x, ffn1_norm, ffn1_w_gate, ffn1_w_up, ffn1_w_down, mix_norm, w_in, na_rpb, sink_logit, w_branch_a, w_branch_b, w_out, ffn2_norm, ffn2_w_gate, ffn2_w_up, ffn2_w_down, final_norm, loss_target, m_ffn1_norm, m_ffn1_w_gate, m_ffn1_w_up, m_ffn1_w_down, m_mix_norm, m_w_in, m_na_rpb, m_sink_logit, m_w_branch_a, m_w_branch_b, m_w_out, m_ffn2_norm, m_ffn2_w_gate, m_ffn2_w_up, m_ffn2_w_down, m_final_norm, v_ffn1_norm, v_ffn1_w_gate, v_ffn1_w_up, v_ffn1_w_down, v_mix_norm, v_w_in, v_na_rpb, v_sink_logit, v_w_branch_a, v_w_branch_b, v_w_out, v_ffn2_norm, v_ffn2_w_gate, v_ffn2_w_up, v_ffn2_w_down, v_final_norm):
    args = dict(locals())
    w = {k: args[k] for k in WEIGHTS}
    mom = {k: args["m_" + k] for k in WEIGHTS}
    var = {k: args["v_" + k] for k in WEIGHTS}
    cidx = lax.axis_index("c").astype(jnp.int32).reshape(1)
    chip = (2 * lax.axis_index("x") + lax.axis_index("y")).astype(jnp.int32).reshape(1)

    def shard(a, k):
        return jnp.swapaxes(a[0], 0, 1) if k in TRANSPOSED else a[0]

    def unshard(a, k):
        return (jnp.swapaxes(a, 0, 1) if k in TRANSPOSED else a)[None]

    def bf16_shards(names):
        return [shard(w[k], k).astype(BF16) for k in names]

    class comm:
        late_rider = _gather_rider(bf16_shards(MIXER + FFN2))

        @staticmethod
        def late(gathered):
            full = dict(zip(MIXER + FFN2, gathered))
            return (full["w_in"].reshape(D_IN, D_MODEL), _unstack_cols(full["w_branch_a"]), _unstack_cols(full["w_branch_b"]),
                    full["w_out"].reshape(D_MODEL, D_MODEL), tuple(full[k] for k in FFN2))

        @staticmethod
        def reduce(names, grads, *, tag):
            return _Reduce(names, grads, cidx, chip, tag=tag)

    f1 = _run_rider(_gather_rider(bf16_shards(FFN1)), name="all_gather_ffn1")
    out = _layer_grads(x[0], loss_target[0], ffn1_norm, f1, mix_norm, None, na_rpb[0], sink_logit[0], ffn2_norm,
                       final_norm.reshape(1, D_MODEL), comm=comm)
    mine = {k: out[k][0] for k in BIG}
    other = {k: out[k][1] for k in BIG}
    grad = {}

    rows = D_MODEL // LANES
    small = jnp.concatenate([
        out["ffn1_norm"].reshape(rows, LANES), out["mix_norm"].reshape(rows, LANES), out["ffn2_norm"].reshape(rows, LANES),
        out["final_norm"].reshape(rows, LANES), out["na_rpb"].reshape(-1, LANES),
        _pad_rows(out["sink_logit"].reshape(1, NB_HEADS), 8), _pad_rows(out["loss"], 8)], axis=0)
    total = _small_allreduce(small)
    n_rpb = NA_HEADS * 2 * NA_KH
    grad["ffn1_norm"] = total[0:rows].reshape(1, D_MODEL)
    grad["mix_norm"] = total[rows:2 * rows].reshape(1, D_MODEL)
    grad["ffn2_norm"] = total[2 * rows:3 * rows].reshape(1, D_MODEL)
    grad["final_norm"] = total[3 * rows:4 * rows].reshape(1, D_MODEL)
    grad["na_rpb"] = total[4 * rows:4 * rows + n_rpb].reshape(NA_HEADS, 2 * NA_KH, LANES)[:, :2 * NA_KH - 1, :2 * NA_KW - 1]
    grad["na_rpb"] = grad["na_rpb"].reshape(NA_HEADS, -1)
    grad["sink_logit"] = total[4 * rows + n_rpb:4 * rows + n_rpb + 1, 0:NB_HEADS]
    loss = total[4 * rows + n_rpb + 8, 0]

    deltas, new_m, new_v, grads_out = {}, {}, {}, {}
    for k in WEIGHTS:
        shape = w[k].shape
        if k in mine:
            res = _adamw_halves(shard(w[k], k), mine[k], other[k], shard(mom[k], k), shard(var[k], k), cidx, name="adamw_" + k)
            grads_out[k], deltas[k], new_m[k], new_v[k] = (unshard(a, k) for a in res)
        else:
            g2d = grad[k]
            d, mn, vn = _adamw(w[k].reshape(g2d.shape), g2d, mom[k].reshape(g2d.shape), var[k].reshape(g2d.shape), name="adamw_" + k)
            grads_out[k], deltas[k], new_m[k], new_v[k] = (a.reshape(shape) for a in (g2d, d, mn, vn))
    return (loss, out["dx"].reshape(x.shape), *[grads_out[k] for k in WEIGHTS], *[deltas[k] for k in WEIGHTS],
            *[new_m[k] for k in WEIGHTS], *[new_v[k] for k in WEIGHTS])
```

```python
import math

import jax
import jax.numpy as jnp
import numpy as np
from jax import lax
from jax.experimental import pallas as pl
from jax.experimental.pallas import tpu as pltpu

F32 = jnp.float32
BF16 = jnp.bfloat16

D_MODEL = 1024
HEAD_DIM = 64
NA_HEADS = 8
NB_HEADS = 8
GRID_W = 64
NA_KH = 8
NA_KW = 16
WIN = 128
ROPE_THETA = 10000.0
EPS = 1e-6
N_CHIPS = 4
QK_SCALE = HEAD_DIM ** -0.5
NEG = -1e30
LANES = 128
VMEM_LIMIT = 56 * 1024 * 1024
HEAD_ROWS = 256
WGRAD_TOKENS_BYTES = 8192

C_QKVA = 3 * NA_HEADS * HEAD_DIM
C_QB = NB_HEADS * HEAD_DIM
C_KB = 2 * HEAD_DIM
C_ROPE = C_QB + C_KB
C_GATES = 2 * D_MODEL
D_IN = C_QKVA + C_QB + 2 * C_KB + C_GATES
O_QB = C_QKVA
O_KB = O_QB + C_QB
O_VB = O_KB + C_KB
O_G = O_VB + C_KB

ADAM_LR = 0.001
ADAM_B1 = 0.9
ADAM_B2 = 0.999
ADAM_EPS = 1e-08
ADAM_WD = 0.01
ADAM_STEP = 10

MESH = pl.DeviceIdType.MESH


def _dot(a, b):
    return jnp.dot(a, b, preferred_element_type=F32)


def _dot_nt(a, b):
    return lax.dot_general(a, b, (((1,), (1,)), ((), ())), preferred_element_type=F32)


def _dot_tn(a, b):
    return lax.dot_general(a, b, (((0,), (0,)), ((), ())), preferred_element_type=F32)


def _params(n_axes):
    return pltpu.CompilerParams(dimension_semantics=("arbitrary",) * n_axes, vmem_limit_bytes=VMEM_LIMIT)


def _rstd(xf):
    return lax.rsqrt(jnp.mean(xf * xf, axis=-1, keepdims=True) + EPS)


def _norm_bwd(dn, xf, g, r):
    xhat = xf * r
    dxh = dn * g
    dx = r * (dxh - xhat * jnp.mean(dxh * xhat, axis=-1, keepdims=True))
    return dx, dn * xhat


def _sigmoid(x):
    return 0.5 * jnp.tanh(0.5 * x) + 0.5


def _loss_head(hf, gv, tgt):
    r = _rstd(hf)
    err = (hf * r) * gv - tgt
    dx, dgr = _norm_bwd(err * (1.0 / hf.shape[-1]), hf, gv, r)
    return 0.5 * jnp.mean(err * err, axis=-1, keepdims=True), dx, dgr


def _ffn_fwd(x, g, wg, wu, wd, *, name, tm=1024, sub=512, rider=None, head=None):
    T, D = x.shape
    F = wg.shape[1]
    tm = min(tm, T)
    sub = min(sub, tm)
    n_head = 0 if head is None else 2

    def body(*refs):
        x_ref, g_ref, wg_ref, wu_ref, wd_ref = refs[:5]
        h_ref, n_ref, hdn_ref, p_ref, q_ref = refs[5 + n_head:10 + n_head]
        i, s = pl.program_id(0), pl.program_id(1)
        _ffn_fwd_step(x_ref, g_ref, wg_ref, wu_ref, wd_ref, h_ref, n_ref, hdn_ref, p_ref, q_ref, s)
        if head is not None:
            gf_ref, t_ref = refs[5:7]
            loss_ref, dgf_ref = refs[10 + n_head:]

            @pl.when((i == 0) & (s == 0))
            def _():
                loss_ref[...] = jnp.zeros_like(loss_ref)
                dgf_ref[...] = jnp.zeros_like(dgf_ref)

            @pl.when(s == N_CHIPS - 1)
            def _():
                for u in range(tm // HEAD_ROWS):
                    r = pl.ds(u * HEAD_ROWS, HEAD_ROWS)
                    terms, dh, dgr = _loss_head(h_ref[r, :], gf_ref[...], t_ref[r, :])
                    loss_ref[...] += jnp.broadcast_to(jnp.sum(terms), loss_ref.shape)
                    dgf_ref[...] += jnp.sum(dgr, axis=0, keepdims=True)
                    h_ref[r, :] = dh

    def _ffn_fwd_step(x_ref, g_ref, wg_ref, wu_ref, wd_ref, h_ref, n_ref, hdn_ref, p_ref, q_ref, s):

        @pl.when(s == 0)
        def _():
            xf = x_ref[...]
            n_ref[...] = ((xf * _rstd(xf)) * g_ref[...]).astype(BF16)
            h_ref[...] = xf

        rows = [pl.ds(u * sub, sub) for u in range(tm // sub)]
        ab = [(_dot_nt(n_ref[r, :], wg_ref[0]), _dot_nt(n_ref[r, :], wu_ref[0])) for r in rows]
        hdns = []
        for r, (a, b) in zip(rows, ab):
            sg = _sigmoid(a)
            silu = a * sg
            hdn = (silu * b).astype(BF16)
            hdn_ref[0, r, :] = hdn
            p_ref[0, r, :] = (b * (sg + silu * (1.0 - sg))).astype(BF16)
            q_ref[0, r, :] = silu.astype(BF16)
            hdns.append(hdn)
        for r, hdn in zip(rows, hdns):
            h_ref[r, :] += 0.5 * _dot(hdn, wd_ref[0])

    tok = pl.BlockSpec((tm, D), lambda i, s: (i, 0))
    hid = pl.BlockSpec((1, tm, F), lambda i, s: (s, i, 0))
    wspec = pl.BlockSpec((1, F, D), lambda i, s: (s, 0, 0))
    hshape = jax.ShapeDtypeStruct((N_CHIPS, T, F), BF16)
    grid = (T // tm, N_CHIPS)
    vec = pl.BlockSpec((1, D), lambda i, s: (0, 0))
    head_in, head_in_specs, head_out, head_out_specs = [], [], [], []
    if head is not None:
        head_in, head_in_specs = list(head), [vec, tok]
        head_out = [jax.ShapeDtypeStruct((1, LANES), F32), jax.ShapeDtypeStruct((1, D), F32)]
        head_out_specs = [pl.BlockSpec((1, LANES), lambda i, s: (0, 0)), vec]
    n_main = 5 + n_head
    body, r_in, r_in_specs, r_out, r_out_specs, scratch = _ride(body, n_main, n_main, rider, grid, (grid[0] * grid[1] * 7) // 8)
    outs = pl.pallas_call(
        body, name=name, grid=grid,
        in_specs=[tok, vec, wspec, wspec, wspec] + head_in_specs + r_in_specs,
        out_specs=[tok, tok, hid, hid, hid] + head_out_specs + r_out_specs,
        out_shape=[jax.ShapeDtypeStruct((T, D), F32), jax.ShapeDtypeStruct((T, D), BF16), hshape, hshape, hshape]
        + head_out + r_out,
        scratch_shapes=scratch,
        compiler_params=_params(2),
    )(x, g, wg, wu, wd, *head_in, *r_in)
    return (*outs[:n_main], list(outs[n_main:]))


def _ffn_bwd(dh, x, g, p, q, wg, wu, wd, *, name, tm=1024, sub=256, rider=None):
    T, D = x.shape
    F = wg.shape[1]
    tm = min(tm, T)
    sub = min(sub, tm)

    def body(dh_ref, x_ref, g_ref, p_ref, q_ref, wg_ref, wu_ref, wd_ref, dx_ref, da_ref, db_ref, dg_ref):
        i, s = pl.program_id(0), pl.program_id(1)

        @pl.when((i == 0) & (s == 0))
        def _():
            dg_ref[...] = jnp.zeros_like(dg_ref)

        @pl.when(s == 0)
        def _():
            dx_ref[...] = jnp.zeros_like(dx_ref)

        rows = [pl.ds(u * sub, sub) for u in range(tm // sub)]
        dhdn = [_dot_nt((0.5 * dh_ref[r, :]).astype(BF16), wd_ref[0]) for r in rows]
        das, dbs = [], []
        for r, dd in zip(rows, dhdn):
            da = (dd * p_ref[0, r, :].astype(F32)).astype(BF16)
            db = (dd * q_ref[0, r, :].astype(F32)).astype(BF16)
            da_ref[0, r, :] = da
            db_ref[0, r, :] = db
            das.append(da)
            dbs.append(db)
        for r, da, db in zip(rows, das, dbs):
            dx_ref[r, :] += _dot(da, wg_ref[0]) + _dot(db, wu_ref[0])

        @pl.when(s == N_CHIPS - 1)
        def _():
            xf = x_ref[...]
            dx, dgr = _norm_bwd(dx_ref[...], xf, g_ref[...], _rstd(xf))
            dg_ref[...] += jnp.sum(dgr, axis=0, keepdims=True)
            dx_ref[...] = dh_ref[...] + dx

    tok = pl.BlockSpec((tm, D), lambda i, s: (i, 0))
    hid = pl.BlockSpec((1, tm, F), lambda i, s: (s, i, 0))
    vec = pl.BlockSpec((1, D), lambda i, s: (0, 0))
    hshape = jax.ShapeDtypeStruct((N_CHIPS, T, F), BF16)
    wspec = pl.BlockSpec((1, F, D), lambda i, s: (s, 0, 0))
    grid = (T // tm, N_CHIPS)
    body, r_in, r_in_specs, r_out, r_out_specs, scratch = _ride(body, 8, 4, rider, grid, None)
    outs = pl.pallas_call(
        body, name=name, grid=grid,
        in_specs=[tok, tok, vec, hid, hid, wspec, wspec, wspec] + r_in_specs,
        out_specs=[tok, hid, hid, vec] + r_out_specs,
        out_shape=[jax.ShapeDtypeStruct((T, D), F32), hshape, hshape, jax.ShapeDtypeStruct((1, D), F32)] + r_out,
        scratch_shapes=scratch,
        compiler_params=_params(2),
    )(dh, x, g, p, q, wg, wu, wd, *r_in)
    return (*outs[:4], list(outs[4:]))


def _wgrad(a, b, *, a_block, a_map, b_block, b_map, out_shape, o_block, o_map, grid, scale=1.0, name, rider=None):
    def body(a_ref, b_ref, o_ref):
        @pl.when(pl.program_id(len(grid) - 1) == 0)
        def _():
            o_ref[...] = jnp.zeros_like(o_ref)

        av = a_ref[...]
        bv = b_ref[...]
        av = av.reshape(av.shape[-2:]).astype(BF16)
        bv = bv.reshape(bv.shape[-2:])
        if scale != 1.0:
            bv = scale * bv
        o_ref[...] += _dot_tn(av, bv.astype(BF16)).reshape(o_ref.shape)

    body, r_in, r_in_specs, r_out, r_out_specs, scratch = _ride(body, 2, 1, rider, grid, None)
    outs = pl.pallas_call(
        body, name=name, grid=grid,
        in_specs=[pl.BlockSpec(a_block, a_map), pl.BlockSpec(b_block, b_map)] + r_in_specs,
        out_specs=[pl.BlockSpec(o_block, o_map)] + r_out_specs,
        out_shape=[jax.ShapeDtypeStruct(out_shape, F32)] + r_out,
        scratch_shapes=scratch,
        compiler_params=_params(len(grid)),
    )(a, b, *r_in)
    return outs[0] if rider is None else (outs[0], list(outs[1:]))


def _wgrad_rows(a, b, n_blocks, *, name, tk=2048):
    T, N = b.shape
    M = a.shape[1] // n_blocks
    tk = min(tk, T)
    return _wgrad(a, b, a_block=(tk, M), a_map=lambda s, k: (k, s), b_block=(tk, N), b_map=lambda s, k: (k, 0),
                  out_shape=(n_blocks, M, N), o_block=(1, M, N), o_map=lambda s, k: (s, 0, 0), grid=(n_blocks, T // tk), name=name)


def _wgrad_shard_a(a, b, *, name, scale=1.0, rider=None):
    S, T, M = a.shape
    N = b.shape[1]
    tk = min(WGRAD_TOKENS_BYTES // b.dtype.itemsize, T)
    return _wgrad(a, b, a_block=(1, tk, M), a_map=lambda s, k: (s, k, 0), b_block=(tk, N), b_map=lambda s, k: (k, 0),
                  out_shape=(S, M, N), o_block=(1, M, N), o_map=lambda s, k: (s, 0, 0), grid=(S, T // tk), scale=scale,
                  name=name, rider=rider)


def _wgrad_cols(a, b, n_blocks, *, name, tk=1024):
    T, M = a.shape
    N = b.shape[1] // n_blocks
    tk = min(tk, T)

    def body(a_ref, b_ref, o_ref):
        @pl.when(pl.program_id(0) == 0)
        def _():
            o_ref[...] = jnp.zeros_like(o_ref)

        r = _dot_tn(a_ref[...].astype(BF16), b_ref[...].astype(BF16))
        for s in range(n_blocks):
            o_ref[s] += r[:, s * N:(s + 1) * N]

    return pl.pallas_call(
        body, name=name, grid=(T // tk,),
        in_specs=[pl.BlockSpec((tk, M), lambda k: (k, 0)), pl.BlockSpec((tk, n_blocks * N), lambda k: (k, 0))],
        out_specs=pl.BlockSpec((n_blocks, M, N), lambda k: (0, 0, 0)),
        out_shape=jax.ShapeDtypeStruct((n_blocks, M, N), F32),
        compiler_params=_params(1),
    )(a, b)


def _rope_tables(T):
    half = HEAD_DIM // 2
    inv = np.float32(ROPE_THETA) ** (-np.arange(half, dtype=np.float32) / np.float32(half))
    ang = np.arange(T, dtype=np.float32)[:, None] * inv[None, :]
    cos, sin, zero = np.cos(ang), np.sin(ang), np.zeros_like(ang)
    reps = LANES // HEAD_DIM
    return (jnp.asarray(np.tile(np.concatenate([cos, cos], axis=1), (1, reps))),
            jnp.asarray(np.tile(np.concatenate([-sin, zero], axis=1), (1, reps))),
            jnp.asarray(np.tile(np.concatenate([zero, sin], axis=1), (1, reps))))


def _rope(x, cos, sa, sb, sign):
    half = HEAD_DIM // 2
    return x * cos + sign * (pltpu.roll(x, LANES - half, 1) * sa + pltpu.roll(x, half, 1) * sb)


def _mix_in_fwd(h, g, w_in, tables, *, tm=512):
    T, D = h.shape

    def body(h_ref, g_ref, w_ref, cos_ref, sa_ref, sb_ref, u_ref, qkva_ref, qb_ref, kvb_ref, gates_ref):
        hf = h_ref[...]
        u = ((hf * _rstd(hf)) * g_ref[...]).astype(BF16)
        u_ref[...] = u
        qkva_ref[...] = _dot_nt(u, w_ref[0:C_QKVA, :]).astype(BF16)
        zr = _dot_nt(u, w_ref[O_QB:O_QB + C_ROPE, :])
        cos, sa, sb = cos_ref[...], sa_ref[...], sb_ref[...]
        for j in range(C_ROPE // LANES):
            rj = _rope(zr[:, j * LANES:(j + 1) * LANES], cos, sa, sb, 1.0).astype(BF16)
            if j < C_QB // LANES:
                qb_ref[:, j * LANES:(j + 1) * LANES] = rj
            else:
                kvb_ref[:, 0:C_KB] = rj
        kvb_ref[:, C_KB:2 * C_KB] = _dot_nt(u, w_ref[O_VB:O_VB + C_KB, :]).astype(BF16)
        gates_ref[...] = _dot_nt(u, w_ref[O_G:O_G + C_GATES, :])

    def tok(n):
        return pl.BlockSpec((tm, n), lambda i: (i, 0))

    return pl.pallas_call(
        body, name="mix_in_fwd", grid=(T // tm,),
        in_specs=[tok(D), pl.BlockSpec((1, D), lambda i: (0, 0)), pl.BlockSpec((D_IN, D), lambda i: (0, 0), pipeline_mode=pl.Buffered(1)),
                  tok(LANES), tok(LANES), tok(LANES)],
        out_specs=[tok(D), tok(C_QKVA), tok(C_QB), tok(2 * C_KB), tok(C_GATES)],
        out_shape=[jax.ShapeDtypeStruct((T, D), BF16), jax.ShapeDtypeStruct((T, C_QKVA), BF16),
                   jax.ShapeDtypeStruct((T, C_QB), BF16), jax.ShapeDtypeStruct((T, 2 * C_KB), BF16),
                   jax.ShapeDtypeStruct((T, C_GATES), F32)],
        compiler_params=_params(1),
    )(h, g, w_in, *tables)


def _mix_in_bwd(dqa, dka, dva, dqb, dkb, dvb, dgates, h, g, dres, w_in, tables, *, tm=512):
    T, D = h.shape

    def body(dqa_ref, dka_ref, dva_ref, dqb_ref, dkb_ref, dvb_ref, dgt_ref, h_ref, g_ref, dres_ref, w_ref,
             cos_ref, sa_ref, sb_ref, dz_ref, dh_ref, dg_ref):
        @pl.when(pl.program_id(0) == 0)
        def _():
            dg_ref[...] = jnp.zeros_like(dg_ref)

        na = NA_HEADS * HEAD_DIM
        dz_ref[:, 0:na] = dqa_ref[...].astype(BF16)
        dz_ref[:, na:2 * na] = dka_ref[...].astype(BF16)
        dz_ref[:, 2 * na:3 * na] = dva_ref[...].astype(BF16)
        cos, sa, sb = cos_ref[...], sa_ref[...], sb_ref[...]
        for j in range(C_QB // LANES):
            dz_ref[:, O_QB + j * LANES:O_QB + (j + 1) * LANES] = _rope(
                dqb_ref[:, j * LANES:(j + 1) * LANES], cos, sa, sb, -1.0).astype(BF16)
        dz_ref[:, O_KB:O_KB + C_KB] = _rope(dkb_ref[...], cos, sa, sb, -1.0).astype(BF16)
        dz_ref[:, O_VB:O_VB + C_KB] = dvb_ref[...].astype(BF16)
        dz_ref[:, O_G:O_G + C_GATES] = dgt_ref[...].astype(BF16)
        du = _dot(dz_ref[...], w_ref[...])
        hf = h_ref[...]
        dx, dgr = _norm_bwd(du, hf, g_ref[...], _rstd(hf))
        dg_ref[...] += jnp.sum(dgr, axis=0, keepdims=True)
        dh_ref[...] = dres_ref[...] + dx

    def tok(n):
        return pl.BlockSpec((tm, n), lambda i: (i, 0))

    vec = pl.BlockSpec((1, D), lambda i: (0, 0))
    na = NA_HEADS * HEAD_DIM
    return pl.pallas_call(
        body, name="mix_in_bwd", grid=(T // tm,),
        in_specs=[tok(na), tok(na), tok(na), tok(C_QB), tok(C_KB), tok(C_KB), tok(C_GATES), tok(D), vec, tok(D),
                  pl.BlockSpec((D_IN, D), lambda i: (0, 0), pipeline_mode=pl.Buffered(1)), tok(LANES), tok(LANES), tok(LANES)],
        out_specs=[tok(D_IN), tok(D), vec],
        out_shape=[jax.ShapeDtypeStruct((T, D_IN), BF16), jax.ShapeDtypeStruct((T, D), F32),
                   jax.ShapeDtypeStruct((1, D), F32)],
        compiler_params=_params(1),
    )(dqa, dka, dva, dqb, dkb, dvb, dgates, h, g, dres, w_in, *tables)


def _na_bias_slabs(rpb):
    H = rpb.shape[0]
    ncell = GRID_W * GRID_W
    cell = np.arange(ncell)
    co = cell % GRID_W - cell // GRID_W + (NA_KW - 1)
    e_co = jnp.asarray((np.arange(LANES)[:, None] == co[None, :]).astype(np.float32))
    table = jnp.pad(rpb, ((0, 0), (0, 1), (0, LANES - rpb.shape[2]))).reshape(H * 2 * NA_KH, LANES)

    def body(t_ref, e_ref, o_ref):
        o_ref[...] = jnp.dot(t_ref[...], e_ref[...], preferred_element_type=F32, precision=lax.Precision.HIGHEST)

    toeplitz = pl.pallas_call(
        body, name="rpb_unfold", out_shape=jax.ShapeDtypeStruct((H * 2 * NA_KH, ncell), F32),
        compiler_params=_params(0),
    )(table, e_co).reshape(H, 2 * NA_KH, GRID_W, GRID_W)

    def assemble(tz_ref, o_ref):
        c = lax.broadcasted_iota(jnp.int32, (GRID_W, GRID_W), 0)
        k = lax.broadcasted_iota(jnp.int32, (GRID_W, GRID_W), 1)
        cs = jnp.clip(c - NA_KW // 2, 0, GRID_W - NA_KW)
        inwin = (k >= cs) & (k < cs + NA_KW)
        for ro0 in range(NA_KH):
            for hh in range(2):
                for i in range(NA_KH):
                    o_ref[0, ro0, hh * GRID_W:(hh + 1) * GRID_W, i * GRID_W:(i + 1) * GRID_W] = jnp.where(
                        inwin, tz_ref[hh, ro0 + i], NEG)

    return pl.pallas_call(
        assemble, name="na_bias_slabs", grid=(H // 2,),
        in_specs=[pl.BlockSpec((2, 2 * NA_KH, GRID_W, GRID_W), lambda p: (p, 0, 0, 0))],
        out_specs=pl.BlockSpec((1, NA_KH, 2 * GRID_W, NA_KH * GRID_W), lambda p: (p, 0, 0, 0)),
        out_shape=jax.ShapeDtypeStruct((H // 2, NA_KH, 2 * GRID_W, NA_KH * GRID_W), F32),
        compiler_params=_params(1),
    )(toeplitz)


def _na_unstack_slabs(dslab):
    pairs = dslab.shape[0]
    d = dslab.reshape(pairs, NA_KH, 2, GRID_W, NA_KH * GRID_W).transpose(0, 2, 1, 3, 4)
    return d.reshape(2 * pairs, NA_KH, GRID_W, NA_KH * GRID_W)


def _half_masks(rows):
    lane = lax.broadcasted_iota(jnp.int32, (rows, LANES), 1)
    left = lane < HEAD_DIM
    return left, (left, jnp.logical_not(left))


def _stack_heads(x):
    left, halves = _half_masks(x.shape[0])
    xf = x.astype(F32)
    return jnp.concatenate([jnp.where(m, xf, 0.0).astype(BF16) for m in halves], axis=0)


def _unstack_heads(o):
    rows = o.shape[0] // 2
    left, _ = _half_masks(rows)
    return jnp.where(left, o[:rows], o[rows:])


def _na_row(j, t, rb, rows):
    r = j * rb + t
    rs = jnp.clip(r - NA_KH // 2, 0, rows - NA_KH)
    return pl.multiple_of(t * GRID_W, GRID_W), pl.multiple_of(rs * GRID_W, GRID_W), rs - r + (NA_KH - 1)


def _na_specs(T, rb):
    qrows = GRID_W * rb
    pairs = NA_HEADS // 2
    return ([pl.BlockSpec((qrows, LANES), lambda p, j: (j, p)),
             pl.BlockSpec((T, LANES), lambda p, j: (0, pairs + p)),
             pl.BlockSpec((T, LANES), lambda p, j: (0, 2 * pairs + p))],
            pl.BlockSpec((1, NA_KH, 2 * GRID_W, NA_KH * GRID_W), lambda p, j: (p, 0, 0, 0)))


def _softmax(s):
    p = jnp.exp(s - jnp.max(s, axis=-1, keepdims=True))
    return p / jnp.sum(p, axis=-1, keepdims=True)


def _na_fwd(qkva, bias, *, rb=8, group=8):
    T = qkva.shape[0]
    rows = T // GRID_W
    nkeys = NA_KH * GRID_W

    def body(q_ref, k_ref, v_ref, bias_ref, y_ref):
        j = pl.program_id(1)

        def rows_step(t, carry):
            at = [_na_row(j, t * group + u, rb, rows) for u in range(group)]
            s = [_dot_nt(_stack_heads(q_ref[pl.ds(q0, GRID_W), :]), k_ref[pl.ds(k0, nkeys), :]) for q0, k0, _ in at]
            p = [_softmax(su * QK_SCALE + bias_ref[0, ro0]) for su, (_, _, ro0) in zip(s, at)]
            o = [_dot(pu.astype(BF16), v_ref[pl.ds(k0, nkeys), :]) for pu, (_, k0, _) in zip(p, at)]
            for ou, (q0, _, _) in zip(o, at):
                y_ref[pl.ds(q0, GRID_W), :] = _unstack_heads(ou).astype(BF16)
            return carry

        lax.fori_loop(0, rb // group, rows_step, 0)

    qkv_specs, bias_spec = _na_specs(T, rb)
    return pl.pallas_call(
        body, name="na_fwd", grid=(NA_HEADS // 2, rows // rb),
        in_specs=qkv_specs + [bias_spec],
        out_specs=qkv_specs[0],
        out_shape=jax.ShapeDtypeStruct((T, NA_HEADS * HEAD_DIM), BF16),
        compiler_params=_params(2),
    )(qkva, qkva, qkva, bias)


def _na_bwd(qkva, dy, bias, *, rb=8, group=8, rider=None):
    T = qkva.shape[0]
    rows = T // GRID_W
    nkeys = NA_KH * GRID_W

    def body(q_ref, k_ref, v_ref, dy_ref, bias_ref, dq_ref, dk_ref, dv_ref, dbias_ref):
        j = pl.program_id(1)

        @pl.when(j == 0)
        def _():
            dk_ref[...] = jnp.zeros_like(dk_ref)
            dv_ref[...] = jnp.zeros_like(dv_ref)
            dbias_ref[...] = jnp.zeros_like(dbias_ref)

        def rows_step(t, carry):
            at = [_na_row(j, t * group + u, rb, rows) for u in range(group)]
            qs = [_stack_heads(q_ref[pl.ds(q0, GRID_W), :]) for q0, _, _ in at]
            dys = [_stack_heads(dy_ref[pl.ds(q0, GRID_W), :]) for q0, _, _ in at]
            s = [_dot_nt(qu, k_ref[pl.ds(k0, nkeys), :]) for qu, (_, k0, _) in zip(qs, at)]
            dp = [_dot_nt(du, v_ref[pl.ds(k0, nkeys), :]) for du, (_, k0, _) in zip(dys, at)]
            p = [_softmax(su * QK_SCALE + bias_ref[0, ro0]) for su, (_, _, ro0) in zip(s, at)]
            ds = [pu * (du - jnp.sum(pu * du, axis=-1, keepdims=True)) for pu, du in zip(p, dp)]
            for u, (q0, k0, ro0) in enumerate(at):
                dbias_ref[0, ro0] += ds[u]
                dsb = ds[u].astype(BF16)
                dq_ref[pl.ds(q0, GRID_W), :] = (_unstack_heads(_dot(dsb, k_ref[pl.ds(k0, nkeys), :])) * QK_SCALE).astype(BF16)
                dk_ref[pl.ds(k0, nkeys), :] += _dot_tn(dsb, qs[u]) * QK_SCALE
                dv_ref[pl.ds(k0, nkeys), :] += _dot_tn(p[u].astype(BF16), dys[u])
            return carry

        lax.fori_loop(0, rb // group, rows_step, 0)

    qkv_specs, bias_spec = _na_specs(T, rb)
    width = NA_HEADS * HEAD_DIM
    kv_out = pl.BlockSpec((T, LANES), lambda p, j: (0, p))
    grid = (NA_HEADS // 2, rows // rb)
    body, r_in, r_in_specs, r_out, r_out_specs, scratch = _ride(body, 5, 4, rider, grid, None)
    outs = pl.pallas_call(
        body, name="na_bwd", grid=grid,
        in_specs=qkv_specs + [qkv_specs[0], bias_spec] + r_in_specs,
        out_specs=[qkv_specs[0], kv_out, kv_out, bias_spec] + r_out_specs,
        out_shape=[jax.ShapeDtypeStruct((T, width), BF16), jax.ShapeDtypeStruct((T, width), F32),
                   jax.ShapeDtypeStruct((T, width), F32), jax.ShapeDtypeStruct(bias.shape, F32)] + r_out,
        scratch_shapes=scratch,
        compiler_params=_params(2),
    )(qkva, qkva, qkva, dy, bias, *r_in)
    return (*outs[:4], list(outs[4:]))


def _rpb_fold(dslab):
    H = dslab.shape[0]
    nro = NA_KH * NA_KH
    ncell = GRID_W * GRID_W
    xs = dslab.reshape(H, NA_KH, GRID_W, NA_KH, GRID_W).transpose(0, 1, 3, 2, 4).reshape(H, nro, ncell)
    cell = np.arange(ncell)
    co = cell % GRID_W - cell // GRID_W + (NA_KW - 1)
    e_co = jnp.asarray((co[:, None] == np.arange(LANES)[None, :]).astype(np.float32))
    pair = np.arange(nro)
    e_ro = jnp.asarray(((pair // NA_KH + pair % NA_KH)[None, :] == np.arange(2 * NA_KH)[:, None]).astype(np.float32))

    def body(x_ref, eco_ref, ero_ref, o_ref):
        y = jnp.dot(x_ref[0], eco_ref[...], preferred_element_type=F32, precision=lax.Precision.HIGHEST)
        o_ref[0] = jnp.dot(ero_ref[...], y, preferred_element_type=F32, precision=lax.Precision.HIGHEST)

    return pl.pallas_call(
        body, name="rpb_fold", grid=(H,),
        in_specs=[pl.BlockSpec((1, nro, ncell), lambda h: (h, 0, 0)), pl.BlockSpec((ncell, LANES), lambda h: (0, 0)),
                  pl.BlockSpec((2 * NA_KH, nro), lambda h: (0, 0))],
        out_specs=pl.BlockSpec((1, 2 * NA_KH, LANES), lambda h: (h, 0, 0)),
        out_shape=jax.ShapeDtypeStruct((H, 2 * NA_KH, LANES), F32),
        compiler_params=_params(1),
    )(xs, e_co, e_ro)


SWA_KEYS = 3 * WIN


def _swa_block(j, t, qbn, T):
    blk = j * qbn + t
    start = jnp.clip((blk - 1) * WIN, 0, T - SWA_KEYS)
    row = lax.broadcasted_iota(jnp.int32, (2 * WIN, SWA_KEYS), 0)
    qpos = blk * WIN + jnp.where(row < WIN, row, row - WIN)
    kpos = start + lax.broadcasted_iota(jnp.int32, (2 * WIN, SWA_KEYS), 1)
    return pl.multiple_of(t * WIN, WIN), pl.multiple_of(start, WIN), jnp.abs(qpos - kpos) <= WIN


def _swa_sinks(sink_ref, p):
    row = lax.broadcasted_iota(jnp.int32, (2 * WIN, 1), 0)
    return jnp.where(row < WIN, sink_ref[p], sink_ref[p + NB_HEADS // 2])


def _swa_probs(s, mask, sink):
    s = jnp.where(mask, s * QK_SCALE, NEG)
    m = jnp.maximum(jnp.max(s, axis=-1, keepdims=True), sink)
    e = jnp.exp(s - m)
    esink = jnp.exp(sink - m)
    den = jnp.sum(e, axis=-1, keepdims=True) + esink
    return e / den, esink / den


def _swa_specs(T, qbn):
    return [pl.BlockSpec(memory_space=pltpu.SMEM),
            pl.BlockSpec((WIN * qbn, LANES), lambda p, j: (j, p)),
            pl.BlockSpec((T, LANES), lambda p, j: (0, 0)),
            pl.BlockSpec((T, LANES), lambda p, j: (0, 1))]


def _swa_fwd(qb, kvb, sink, *, qbn=8, group=8):
    T = qb.shape[0]
    pairs = NB_HEADS // 2
    qbn = min(qbn, T // WIN)
    group = min(group, qbn)

    def body(sink_ref, q_ref, k_ref, v_ref, y_ref):
        p, j = pl.program_id(0), pl.program_id(1)
        sinks = _swa_sinks(sink_ref, p)

        def blocks_step(t, carry):
            at = [_swa_block(j, t * group + u, qbn, T) for u in range(group)]
            s = [_dot_nt(_stack_heads(q_ref[pl.ds(q0, WIN), :]), k_ref[pl.ds(k0, SWA_KEYS), :]) for q0, k0, _ in at]
            pr = [_swa_probs(su, mask, sinks)[0] for su, (_, _, mask) in zip(s, at)]
            o = [_dot(pu.astype(BF16), v_ref[pl.ds(k0, SWA_KEYS), :]) for pu, (_, k0, _) in zip(pr, at)]
            for ou, (q0, _, _) in zip(o, at):
                y_ref[pl.ds(q0, WIN), :] = _unstack_heads(ou).astype(BF16)
            return carry

        lax.fori_loop(0, qbn // group, blocks_step, 0)

    specs = _swa_specs(T, qbn)
    return pl.pallas_call(
        body, name="swa_fwd", grid=(pairs, T // (WIN * qbn)),
        in_specs=specs, out_specs=specs[1],
        out_shape=jax.ShapeDtypeStruct((T, NB_HEADS * HEAD_DIM), BF16),
        compiler_params=_params(2),
    )(sink, qb, kvb, kvb)


def _swa_bwd(qb, kvb, dy, sink, *, qbn=8, group=8, rider=None):
    T = qb.shape[0]
    pairs = NB_HEADS // 2
    qbn = min(qbn, T // WIN)
    group = min(group, qbn)

    def body(sink_ref, q_ref, k_ref, v_ref, dy_ref, dq_ref, dk_ref, dv_ref, dsink_ref):
        p, j = pl.program_id(0), pl.program_id(1)
        sinks = _swa_sinks(sink_ref, p)

        @pl.when((p == 0) & (j == 0))
        def _():
            dk_ref[...] = jnp.zeros_like(dk_ref)
            dv_ref[...] = jnp.zeros_like(dv_ref)

        @pl.when(j == 0)
        def _():
            dsink_ref[...] = jnp.zeros_like(dsink_ref)

        def blocks_step(t, carry):
            at = [_swa_block(j, t * group + u, qbn, T) for u in range(group)]
            qs = [_stack_heads(q_ref[pl.ds(q0, WIN), :]) for q0, _, _ in at]
            dys = [_stack_heads(dy_ref[pl.ds(q0, WIN), :]) for q0, _, _ in at]
            s = [_dot_nt(qu, k_ref[pl.ds(k0, SWA_KEYS), :]) for qu, (_, k0, _) in zip(qs, at)]
            dp = [_dot_nt(du, v_ref[pl.ds(k0, SWA_KEYS), :]) for du, (_, k0, _) in zip(dys, at)]
            probs = [_swa_probs(su, mask, sinks) for su, (_, _, mask) in zip(s, at)]
            for u, (q0, k0, _) in enumerate(at):
                pr, psink = probs[u]
                delta = jnp.sum(pr * dp[u], axis=-1, keepdims=True)
                dsb = (pr * (dp[u] - delta)).astype(BF16)
                dsk = psink * delta
                for hh in range(2):
                    dsink_ref[0, hh:hh + 1, :] += jnp.broadcast_to(-jnp.sum(dsk[hh * WIN:(hh + 1) * WIN]), (1, LANES))
                dq_ref[pl.ds(q0, WIN), :] = _unstack_heads(_dot(dsb, k_ref[pl.ds(k0, SWA_KEYS), :])) * QK_SCALE
                dk_ref[pl.ds(k0, SWA_KEYS), :] += _dot_tn(dsb, qs[u]) * QK_SCALE
                dv_ref[pl.ds(k0, SWA_KEYS), :] += _dot_tn(pr.astype(BF16), dys[u])
            return carry

        lax.fori_loop(0, qbn // group, blocks_step, 0)

    specs = _swa_specs(T, qbn)
    kv_out = pl.BlockSpec((T, LANES), lambda p, j: (0, 0))
    grid = (pairs, T // (WIN * qbn))
    body, r_in, r_in_specs, r_out, r_out_specs, scratch = _ride(body, 5, 4, rider, grid, None)
    outs = pl.pallas_call(
        body, name="swa_bwd", grid=grid,
        in_specs=specs + [specs[1]] + r_in_specs,
        out_specs=[specs[1], kv_out, kv_out, pl.BlockSpec((1, 8, LANES), lambda p, j: (p, 0, 0))] + r_out_specs,
        out_shape=[jax.ShapeDtypeStruct((T, NB_HEADS * HEAD_DIM), F32), jax.ShapeDtypeStruct((T, LANES), F32),
                   jax.ShapeDtypeStruct((T, LANES), F32), jax.ShapeDtypeStruct((pairs, 8, LANES), F32)] + r_out,
        scratch_shapes=scratch,
        compiler_params=_params(2),
    )(sink, qb, kvb, kvb, dy, *r_in)
    return (*outs[:4], list(outs[4:]))


def _merge_fwd(ya, yb, gates, wa, wb, wout, h, *, tm=512):
    T, D = h.shape
    W = ya.shape[1]

    def body(ya_ref, yb_ref, gt_ref, wa_ref, wb_ref, wo_ref, h_ref, h2_ref, mg_ref):
        pa = _dot(ya_ref[...], wa_ref[...])
        pb = _dot(yb_ref[...], wb_ref[...])
        mg = (jax.nn.sigmoid(gt_ref[:, 0:D]) * pa + jax.nn.sigmoid(gt_ref[:, D:2 * D]) * pb).astype(BF16)
        mg_ref[...] = mg
        h2_ref[...] = h_ref[...] + _dot(mg, wo_ref[...])

    def tok(n):
        return pl.BlockSpec((tm, n), lambda i: (i, 0))

    def full(r, c):
        return pl.BlockSpec((r, c), lambda i: (0, 0))

    return pl.pallas_call(
        body, name="merge_fwd", grid=(T // tm,),
        in_specs=[tok(W), tok(W), tok(2 * D), full(W, D), full(W, D), full(D, D), tok(D)],
        out_specs=[tok(D), tok(D)],
        out_shape=[jax.ShapeDtypeStruct((T, D), F32), jax.ShapeDtypeStruct((T, D), BF16)],
        compiler_params=_params(1),
    )(ya, yb, gates, wa, wb, wout, h)


def _merge_bwd(dh, ya, yb, gates, wa, wb, wout, *, tm=512, rider=None):
    T, D = dh.shape
    W = ya.shape[1]

    def body(dh_ref, ya_ref, yb_ref, gt_ref, wa_ref, wb_ref, wo_ref, dya_ref, dyb_ref, dpa_ref, dpb_ref, dgt_ref):
        dmg = _dot_nt(dh_ref[...].astype(BF16), wo_ref[...])
        for y_ref, w_ref, dy_ref, dp_ref, lo in ((ya_ref, wa_ref, dya_ref, dpa_ref, 0), (yb_ref, wb_ref, dyb_ref, dpb_ref, D)):
            sg = jax.nn.sigmoid(gt_ref[:, lo:lo + D])
            dp = (dmg * sg).astype(BF16)
            dp_ref[...] = dp
            dgt_ref[:, lo:lo + D] = (dmg * _dot(y_ref[...], w_ref[...]) * (sg * (1.0 - sg))).astype(BF16)
            dy_ref[...] = _dot_nt(dp, w_ref[...]).astype(BF16)

    def tok(n):
        return pl.BlockSpec((tm, n), lambda i: (i, 0))

    def full(r, c):
        return pl.BlockSpec((r, c), lambda i: (0, 0))

    grid = (T // tm,)
    body, r_in, r_in_specs, r_out, r_out_specs, scratch = _ride(body, 7, 5, rider, grid, None)
    outs = pl.pallas_call(
        body, name="merge_bwd", grid=grid,
        in_specs=[tok(D), tok(W), tok(W), tok(2 * D), full(W, D), full(W, D), full(D, D)] + r_in_specs,
        out_specs=[tok(W), tok(W), tok(D), tok(D), tok(2 * D)] + r_out_specs,
        out_shape=[jax.ShapeDtypeStruct((T, W), BF16), jax.ShapeDtypeStruct((T, W), BF16),
                   jax.ShapeDtypeStruct((T, D), BF16), jax.ShapeDtypeStruct((T, D), BF16),
                   jax.ShapeDtypeStruct((T, 2 * D), BF16)] + r_out,
        scratch_shapes=scratch,
        compiler_params=_params(1),
    )(dh, ya, yb, gates, wa, wb, wout, *r_in)
    return (*outs[:5], list(outs[5:]))


def _pair_heads(a, axis):
    shp = a.shape
    a = a.reshape(shp[:axis] + (2, NB_HEADS // 2, HEAD_DIM) + shp[axis + 1:])
    return jnp.swapaxes(a, axis, axis + 1).reshape(shp)


def _unpair_heads(a, axis):
    shp = a.shape
    a = a.reshape(shp[:axis] + (NB_HEADS // 2, 2, HEAD_DIM) + shp[axis + 1:])
    return jnp.swapaxes(a, axis, axis + 1).reshape(shp)


FFN1 = ("ffn1_w_gate", "ffn1_w_up", "ffn1_w_down")
FFN2 = ("ffn2_w_gate", "ffn2_w_up", "ffn2_w_down")
MIXER = ("w_in", "w_branch_a", "w_branch_b", "w_out")


def _layer_grads(x, target, g1, f1, gmix, late, rpb, sink, g2, gfin, comm=None):
    T = x.shape[0]
    tables = _rope_tables(T)
    bias = _na_bias_slabs(rpb)

    h1, n1, hdn1, p1, q1, gathered = _ffn_fwd(x, g1, *f1, name="ffn1_fwd", rider=comm.late_rider if comm else None)
    w_in_t, wa, wb, wout, f2 = comm.late(gathered) if comm else late
    w_in_p = jnp.concatenate([w_in_t[:O_QB], _pair_heads(w_in_t[O_QB:O_KB], 0), w_in_t[O_KB:]], axis=0)
    wb_p = _pair_heads(wb, 0)
    u, qkva, qb, kvb, gates = _mix_in_fwd(h1, gmix, w_in_p, tables)
    ya = _na_fwd(qkva, bias)
    yb = _swa_fwd(qb, kvb, sink)
    h2, merged = _merge_fwd(ya, yb, gates, wa, wb_p, wout, h1)
    dh3, n2, hdn2, p2, q2, loss, dgfin, _ = _ffn_fwd(h2, g2, *f2, name="ffn2_fwd", head=(gfin, target))

    dh2, da2, db2, dg2, _ = _ffn_bwd(dh3, h2, g2, p2, q2, *f2, name="ffn2_bwd")
    df2 = [_wgrad_shard_a(da2, n2, name="ffn2_dwg"), _wgrad_shard_a(db2, n2, name="ffn2_dwu"),
           _wgrad_shard_a(hdn2, dh3, scale=0.5, name="ffn2_dwd")]
    red2 = comm.reduce(FFN2, df2, tag="ffn2") if comm else None
    dya, dyb, dpa, dpb, dgates, got = _merge_bwd(dh2, ya, yb, gates, wa, wb_p, wout, rider=red2.sibling if comm else None)
    dwout = _wgrad_cols(merged, dh2, 1, name="dwout").reshape(N_CHIPS, D_MODEL // N_CHIPS, D_MODEL)
    dwa = _wgrad_cols(ya, dpa, N_CHIPS, name="dwa")
    dwb = _unpair_heads(_wgrad_cols(yb, dpb, N_CHIPS, name="dwb"), 1)
    dqa, dka, dva, dbias, got = _na_bwd(qkva, dya, bias, rider=red2.partial(got).chips if comm else None)
    drpb = _rpb_fold(_na_unstack_slabs(dbias))
    dqb, dkb, dvb, dsink, got = _swa_bwd(qb, kvb, dyb, sink, rider=red2.halves(got).share if comm else None)
    reduced2 = red2.result(got) if comm else None
    dz, dh1, dgmix = _mix_in_bwd(dqa, dka, dva, dqb, dkb, dvb, dgates, h1, gmix, dh2, w_in_p, tables)
    dwin_p = _wgrad_rows(dz, u, 2, name="dwin").reshape(D_IN, D_MODEL)
    dwin = jnp.concatenate([dwin_p[:O_QB], _unpair_heads(dwin_p[O_QB:O_KB], 0), dwin_p[O_KB:]], axis=0)
    dmix = [dwin.reshape(N_CHIPS, D_IN // N_CHIPS, D_MODEL), dwa, dwb, dwout]
    dx, da1, db1, dg1, _ = _ffn_bwd(dh1, x, g1, p1, q1, *f1, name="ffn1_bwd")
    out = dict(loss=loss, dx=dx, ffn1_norm=dg1, mix_norm=dgmix, ffn2_norm=dg2, final_norm=dgfin, na_rpb=drpb,
               sink_logit=dsink[:, 0:2, 0].T.reshape(NB_HEADS))
    if comm:
        out.update(reduced2)
        redm = comm.reduce(MIXER, dmix, tag="mixer").partial_now()
        dwg1, got = _wgrad_shard_a(da1, n1, name="ffn1_dwg", rider=redm.chips)
        dwu1, got = _wgrad_shard_a(db1, n1, name="ffn1_dwu", rider=redm.halves(got).share)
        out.update(redm.result(got))
        red1 = comm.reduce(FFN1[:2], [dwg1, dwu1], tag="ffn1_gate_up").partial_now()
        dwd1, got = _wgrad_shard_a(hdn1, dh1, scale=0.5, name="ffn1_dwd", rider=red1.chips)
        out.update(red1.halves(got).result_now())
        out.update(comm.reduce(FFN1[2:], [dwd1], tag="ffn1_down").partial_now().halves_now().result_now())
    else:
        df1 = [_wgrad_shard_a(da1, n1, name="ffn1_dwg"), _wgrad_shard_a(db1, n1, name="ffn1_dwu"),
               _wgrad_shard_a(hdn1, dh1, scale=0.5, name="ffn1_dwd")]
        out.update(zip(FFN1 + MIXER + FFN2, df1 + dmix + df2))
    return out


ANY = pl.BlockSpec(memory_space=pl.ANY)


def _place():
    x, y, c = lax.axis_index("x"), lax.axis_index("y"), lax.axis_index("c")
    chips = [(1 - x, y), (x, 1 - y), (1 - x, 1 - y)]
    return x, y, c, 2 * x + y, chips


def _remote(src, dst, send_sems, recv_sems, k, device):
    return pltpu.make_async_remote_copy(src_ref=src, dst_ref=dst, send_sem=send_sems.at[k], recv_sem=recv_sems.at[k],
                                        device_id=device, device_id_type=MESH)


class _Rider:
    def __init__(self, inputs, out_shape, scratch, start, middle, finish):
        self.inputs, self.out_shape, self.scratch = list(inputs), list(out_shape), list(scratch)
        self.start, self.middle, self.finish = start, middle, finish


def _run_rider(rider, *, name):
    n_in, n_out = len(rider.inputs), len(rider.out_shape)

    def body(*refs):
        ins, outs, sems = refs[:n_in], refs[n_in:n_in + n_out], refs[n_in + n_out:]
        rider.start(ins, outs, sems)
        if rider.middle is not None:
            rider.middle(ins, outs, sems)
        rider.finish(ins, outs, sems)

    return pl.pallas_call(body, name=name, in_specs=[ANY] * n_in, out_specs=[ANY] * n_out, out_shape=rider.out_shape,
                          scratch_shapes=rider.scratch)(*rider.inputs)


def _ride(body, n_in, n_out, rider, grid, middle_step):
    if rider is None:
        return body, [], [], [], [], []
    r_in, r_out = len(rider.inputs), len(rider.out_shape)
    steps = math.prod(grid)

    def riding(*refs):
        ins, r_ins = refs[:n_in], refs[n_in:n_in + r_in]
        outs = refs[n_in + r_in:n_in + r_in + n_out]
        r_outs = refs[n_in + r_in + n_out:n_in + r_in + n_out + r_out]
        sems = refs[n_in + r_in + n_out + r_out:]
        step = pl.program_id(0)
        for axis in range(1, len(grid)):
            step = step * grid[axis] + pl.program_id(axis)

        @pl.when(step == 0)
        def _():
            rider.start(r_ins, r_outs, sems)

        body(*ins, *outs)

        if rider.middle is not None:
            @pl.when(step == middle_step)
            def _():
                rider.middle(r_ins, r_outs, sems)

        @pl.when(step == steps - 1)
        def _():
            rider.finish(r_ins, r_outs, sems)

    return riding, rider.inputs, [ANY] * r_in, rider.out_shape, [ANY] * r_out, rider.scratch


def _gather_rider(shards):
    n = len(shards)

    def plan(ins, outs, sems, kinds):
        send_sems, recv_sems, own_send_sems, own_recv_sems = sems
        x, y, c, mine, chips = _place()
        sibling = (x, y, 1 - c)
        made = {k: [] for k in kinds}
        for i in range(n):
            hr = shards[i].shape[0] // 2
            if "own" in made:
                made["own"].append(_remote(ins[i], outs[i].at[mine], own_send_sems, own_recv_sems, i, sibling))
            for j, (cx, cy) in enumerate(chips):
                here = outs[i].at[2 * cx + cy, pl.ds(c * hr, hr)]
                there = outs[i].at[2 * cx + cy, pl.ds((1 - c) * hr, hr)]
                if "sends" in made:
                    made["sends"].append(_remote(ins[i].at[pl.ds(c * hr, hr)], outs[i].at[mine, pl.ds(c * hr, hr)],
                                                 send_sems, recv_sems, 6 * i + j, (cx, cy, c)))
                if "landed" in made:
                    made["landed"].append(_remote(here, here, send_sems, recv_sems, 6 * i + j, (cx, cy, c)))
                if "passes" in made:
                    made["passes"].append(_remote(here, here, send_sems, recv_sems, 6 * i + 3 + j, sibling))
                if "others" in made:
                    made["others"].append(_remote(there, there, send_sems, recv_sems, 6 * i + 3 + j, sibling))
        return [made[k] for k in kinds]

    def start(ins, outs, sems):
        own, sends = plan(ins, outs, sems, ("own", "sends"))
        for cp in own + sends:
            cp.start()

    def middle(ins, outs, sems):
        landed, passes = plan(ins, outs, sems, ("landed", "passes"))
        for arrived, cp in zip(landed, passes):
            arrived.wait_recv()
            cp.start()

    def finish(ins, outs, sems):
        own, sends, passes, others = plan(ins, outs, sems, ("own", "sends", "passes", "others"))
        for arrived in others:
            arrived.wait_recv()
        for cp in sends + passes:
            cp.wait_send()
        for cp in own:
            cp.wait()

    return _Rider(shards, [jax.ShapeDtypeStruct((N_CHIPS,) + s.shape, s.dtype) for s in shards],
                  [pltpu.SemaphoreType.DMA((6 * n,)), pltpu.SemaphoreType.DMA((6 * n,)),
                   pltpu.SemaphoreType.DMA((n,)), pltpu.SemaphoreType.DMA((n,))], start, middle, finish)


def _swap_rider(arrays, out_shape, source):
    n = len(arrays)

    def plan(ins, outs, sems):
        send_sems, recv_sems = sems
        x, y, c, _, _ = _place()
        return [_remote(source(ins[i], c, i), outs[i], send_sems, recv_sems, i, (x, y, 1 - c)) for i in range(n)]

    def start(ins, outs, sems):
        for cp in plan(ins, outs, sems):
            cp.start()

    def finish(ins, outs, sems):
        for cp in plan(ins, outs, sems):
            cp.wait()

    return _Rider(arrays, out_shape, [pltpu.SemaphoreType.DMA((n,)), pltpu.SemaphoreType.DMA((n,))], start, None, finish)


def _sibling_rider(grads):
    half = [g.shape[1] // 2 for g in grads]
    return _swap_rider(grads, [jax.ShapeDtypeStruct((g.shape[0], hr, g.shape[2]), g.dtype) for g, hr in zip(grads, half)],
                       lambda ref, c, i: ref.at[:, pl.ds((1 - c) * half[i], half[i])])


def _share_rider(halves):
    return _swap_rider(halves, [jax.ShapeDtypeStruct(h.shape, h.dtype) for h in halves], lambda ref, c, i: ref)


def _chips_rider(parts):
    n = len(parts)

    def plan(ins, outs, sems):
        send_sems, recv_sems = sems
        _, _, c, _, chips = _place()
        return [_remote(ins[i].at[2 * cx + cy], outs[i].at[j], send_sems, recv_sems, 3 * i + j, (cx, cy, c))
                for i in range(n) for j, (cx, cy) in enumerate(chips)]

    def start(ins, outs, sems):
        for cp in plan(ins, outs, sems):
            cp.start()

    def finish(ins, outs, sems):
        for cp in plan(ins, outs, sems):
            cp.wait()

    return _Rider(parts, [jax.ShapeDtypeStruct((N_CHIPS - 1,) + p.shape[1:], p.dtype) for p in parts],
                  [pltpu.SemaphoreType.DMA((3 * n,)), pltpu.SemaphoreType.DMA((3 * n,))], start, None, finish)


class _Reduce:
    def __init__(self, names, grads, cidx, chip, *, tag):
        self.names, self.grads, self.cidx, self.chip, self.tag = names, grads, cidx, chip, tag
        self.sibling = _sibling_rider(grads)

    def partial(self, from_sibling):
        self.from_sibling = from_sibling
        self.chips = _chips_rider([_add_sibling(g, r, self.cidx, name="add_sibling_" + k)
                                   for k, g, r in zip(self.names, self.grads, from_sibling)])
        return self

    def halves(self, from_chips):
        self.mine = [_add_chips(g, r1, r2, self.cidx, self.chip, name="add_chips_" + k)
                     for k, g, r1, r2 in zip(self.names, self.grads, self.from_sibling, from_chips)]
        self.share = _share_rider(self.mine)
        return self

    def result(self, others):
        return dict(zip(self.names, zip(self.mine, others)))

    def partial_now(self):
        return self.partial(_run_rider(self.sibling, name="rs_sibling_" + self.tag))

    def halves_now(self):
        return self.halves(_run_rider(self.chips, name="rs_chips_" + self.tag))

    def result_now(self):
        return self.result(_run_rider(self.share, name="rs_share_" + self.tag))


N_DEV = 8


def _small_allreduce(vec):
    R = vec.shape[0]

    def body(v_ref, o_ref, buf, send_sems, recv_sems):
        x, y, c, _, _ = _place()
        me = 4 * x + 2 * y + c
        buf[me] = v_ref[...]
        copies = []
        for k in range(1, N_DEV):
            peer = (x ^ (k >> 2), y ^ ((k >> 1) & 1), c ^ (k & 1))
            cp = _remote(v_ref, buf.at[me], send_sems, recv_sems, k - 1, peer)
            cp.start()
            copies.append(cp)
        for k, cp in enumerate(copies, start=1):
            cp.wait_send()
            landed = buf.at[me ^ k]
            _remote(landed, landed, send_sems, recv_sems, k - 1, (x, y, c)).wait_recv()
        acc = buf[0]
        for d in range(1, N_DEV):
            acc = acc + buf[d]
        o_ref[...] = acc

    return pl.pallas_call(
        body, name="small_allreduce",
        in_specs=[pl.BlockSpec(memory_space=pltpu.VMEM)], out_specs=pl.BlockSpec(memory_space=pltpu.VMEM),
        out_shape=jax.ShapeDtypeStruct(vec.shape, vec.dtype),
        scratch_shapes=[pltpu.VMEM((N_DEV, R, LANES), F32), pltpu.SemaphoreType.DMA((N_DEV - 1,)),
                        pltpu.SemaphoreType.DMA((N_DEV - 1,))],
    )(vec)


ELEMWISE_BLOCK = 256 * 1024


def _row_tile(rows, cols):
    best = None
    for t in range(8, rows + 1, 8):
        if rows % t == 0 and t * cols <= ELEMWISE_BLOCK:
            best = t
    return best if best is not None else rows


def _add_sibling(g, r1, cidx, *, name):
    S, R, C = g.shape
    hr = R // 2
    tr = _row_tile(hr, C)
    nt = hr // tr

    def body(c_ref, g_ref, r_ref, o_ref):
        o_ref[...] = (g_ref[...] + r_ref[...]).astype(BF16)

    blk = pl.BlockSpec((1, tr, C), lambda s, t, c: (s, t, 0))
    return pl.pallas_call(
        body, name=name,
        grid_spec=pltpu.PrefetchScalarGridSpec(
            num_scalar_prefetch=1, grid=(S, nt),
            in_specs=[pl.BlockSpec((1, tr, C), lambda s, t, c: (s, c[0] * nt + t, 0)), blk], out_specs=blk),
        out_shape=jax.ShapeDtypeStruct((S, hr, C), BF16),
        compiler_params=_params(2),
    )(cidx, g, r1)


def _add_chips(g, r1, r2, cidx, chip, *, name):
    _, R, C = g.shape
    hr = R // 2
    tr = _row_tile(hr, C)
    nt = hr // tr

    def body(pos_ref, g_ref, r1_ref, r2_ref, o_ref):
        own = g_ref[0] + r1_ref[0]
        o_ref[...] = ((own + r2_ref[0].astype(F32)) + r2_ref[1].astype(F32)) + r2_ref[2].astype(F32)

    pos = jnp.concatenate([cidx, chip])
    return pl.pallas_call(
        body, name=name,
        grid_spec=pltpu.PrefetchScalarGridSpec(
            num_scalar_prefetch=1, grid=(nt,),
            in_specs=[pl.BlockSpec((1, tr, C), lambda t, pos: (pos[1], pos[0] * nt + t, 0)),
                      pl.BlockSpec((1, tr, C), lambda t, pos: (pos[1], t, 0)),
                      pl.BlockSpec((N_CHIPS - 1, tr, C), lambda t, pos: (0, t, 0))],
            out_specs=pl.BlockSpec((tr, C), lambda t, pos: (t, 0))),
        out_shape=jax.ShapeDtypeStruct((hr, C), F32),
        compiler_params=_params(1),
    )(pos, g, r1, r2)


def _adamw_math(w, g, m, v):
    mn = ADAM_B1 * m + (1.0 - ADAM_B1) * g
    vn = ADAM_B2 * v + (1.0 - ADAM_B2) * (g * g)
    m_hat = mn / (1.0 - ADAM_B1 ** ADAM_STEP)
    v_hat = vn / (1.0 - ADAM_B2 ** ADAM_STEP)
    return -ADAM_LR * (m_hat / (jnp.sqrt(v_hat) + ADAM_EPS) + ADAM_WD * w), mn, vn


def _adamw_halves(w, mine, other, m, v, cidx, *, name):
    R, C = w.shape
    hr = R // 2
    tr = _row_tile(hr, C)
    nt = hr // tr

    def body(c_ref, w_ref, a_ref, b_ref, m_ref, v_ref, g_ref, d_ref, mo_ref, vo_ref):
        gv = jnp.where(pl.program_id(0) == c_ref[0], a_ref[...], b_ref[...])
        g_ref[...] = gv
        d_ref[...], mo_ref[...], vo_ref[...] = _adamw_math(w_ref[...], gv, m_ref[...], v_ref[...])

    full = pl.BlockSpec((tr, C), lambda h, t, c: (h * nt + t, 0))
    half = pl.BlockSpec((tr, C), lambda h, t, c: (t, 0))
    shape = jax.ShapeDtypeStruct((R, C), F32)
    return pl.pallas_call(
        body, name=name,
        grid_spec=pltpu.PrefetchScalarGridSpec(
            num_scalar_prefetch=1, grid=(2, nt), in_specs=[full, half, half, full, full], out_specs=[full] * 4),
        out_shape=[shape] * 4,
        compiler_params=_params(2),
    )(cidx, w, mine, other, m, v)


def _adamw(w, g, m, v, *, name):
    R, C = w.shape
    tr = _row_tile(R, C)

    def body(w_ref, g_ref, m_ref, v_ref, d_ref, mo_ref, vo_ref):
        d_ref[...], mo_ref[...], vo_ref[...] = _adamw_math(w_ref[...], g_ref[...], m_ref[...], v_ref[...])

    blk = pl.BlockSpec((tr, C), lambda t: (t, 0))
    shape = jax.ShapeDtypeStruct((R, C), F32)
    return pl.pallas_call(
        body, name=name, grid=(R // tr,),
        in_specs=[blk] * 4, out_specs=[blk] * 3, out_shape=[shape] * 3,
        compiler_params=_params(1),
    )(w, g, m, v)


def _unstack_cols(w):
    s, r, c = w.shape
    return w.transpose(1, 0, 2).reshape(r, s * c)


def _pad_rows(a, rows):
    return jnp.pad(a, ((0, rows - a.shape[0]), (0, LANES - a.shape[1])))


BIG = ("ffn1_w_gate", "ffn1_w_up", "ffn1_w_down", "w_in", "w_branch_a", "w_branch_b", "w_out",
       "ffn2_w_gate", "ffn2_w_up", "ffn2_w_down")
TRANSPOSED = ("ffn1_w_gate", "ffn1_w_up", "w_in", "ffn2_w_gate", "ffn2_w_up")
WEIGHTS = ("ffn1_norm", "ffn1_w_gate", "ffn1_w_up", "ffn1_w_down", "mix_norm", "w_in", "na_rpb", "sink_logit",
           "w_branch_a", "w_branch_b", "w_out", "ffn2_norm", "ffn2_w_gate", "ffn2_w_up", "ffn2_w_down", "final_norm")


def kernel(x, ffn1_norm, ffn1_w_gate, ffn1_w_up, ffn1_w_down, mix_norm, w_in, na_rpb, sink_logit, w_branch_a, w_branch_b, w_out, ffn2_norm, ffn2_w_gate, ffn2_w_up, ffn2_w_down, final_norm, loss_target, m_ffn1_norm, m_ffn1_w_gate, m_ffn1_w_up, m_ffn1_w_down, m_mix_norm, m_w_in, m_na_rpb, m_sink_logit, m_w_branch_a, m_w_branch_b, m_w_out, m_ffn2_norm, m_ffn2_w_gate, m_ffn2_w_up, m_ffn2_w_down, m_final_norm, v_ffn1_norm, v_ffn1_w_gate, v_ffn1_w_up, v_ffn1_w_down, v_mix_norm, v_w_in, v_na_rpb, v_sink_logit, v_w_branch_a, v_w_branch_b, v_w_out, v_ffn2_norm, v_ffn2_w_gate, v_ffn2_w_up, v_ffn2_w_down, v_final_norm):
    args = dict(locals())
    w = {k: args[k] for k in WEIGHTS}
    mom = {k: args["m_" + k] for k in WEIGHTS}
    var = {k: args["v_" + k] for k in WEIGHTS}
    cidx = lax.axis_index("c").astype(jnp.int32).reshape(1)
    chip = (2 * lax.axis_index("x") + lax.axis_index("y")).astype(jnp.int32).reshape(1)

    def shard(a, k):
        return jnp.swapaxes(a[0], 0, 1) if k in TRANSPOSED else a[0]

    def unshard(a, k):
        return (jnp.swapaxes(a, 0, 1) if k in TRANSPOSED else a)[None]

    def bf16_shards(names):
        return [shard(w[k], k).astype(BF16) for k in names]

    class comm:
        late_rider = _gather_rider(bf16_shards(MIXER + FFN2))

        @staticmethod
        def late(gathered):
            full = dict(zip(MIXER + FFN2, gathered))
            return (full["w_in"].reshape(D_IN, D_MODEL), _unstack_cols(full["w_branch_a"]), _unstack_cols(full["w_branch_b"]),
                    full["w_out"].reshape(D_MODEL, D_MODEL), tuple(full[k] for k in FFN2))

        @staticmethod
        def reduce(names, grads, *, tag):
            return _Reduce(names, grads, cidx, chip, tag=tag)

    f1 = _run_rider(_gather_rider(bf16_shards(FFN1)), name="all_gather_ffn1")
    out = _layer_grads(x[0], loss_target[0], ffn1_norm, f1, mix_norm, None, na_rpb[0], sink_logit[0], ffn2_norm,
                       final_norm.reshape(1, D_MODEL), comm=comm)
    mine = {k: out[k][0] for k in BIG}
    other = {k: out[k][1] for k in BIG}
    grad = {}

    rows = D_MODEL // LANES
    small = jnp.concatenate([
        out["ffn1_norm"].reshape(rows, LANES), out["mix_norm"].reshape(rows, LANES), out["ffn2_norm"].reshape(rows, LANES),
        out["final_norm"].reshape(rows, LANES), out["na_rpb"].reshape(-1, LANES),
        _pad_rows(out["sink_logit"].reshape(1, NB_HEADS), 8), _pad_rows(out["loss"], 8)], axis=0)
    total = _small_allreduce(small)
    n_rpb = NA_HEADS * 2 * NA_KH
    grad["ffn1_norm"] = total[0:rows].reshape(1, D_MODEL)
    grad["mix_norm"] = total[rows:2 * rows].reshape(1, D_MODEL)
    grad["ffn2_norm"] = total[2 * rows:3 * rows].reshape(1, D_MODEL)
    grad["final_norm"] = total[3 * rows:4 * rows].reshape(1, D_MODEL)
    grad["na_rpb"] = total[4 * rows:4 * rows + n_rpb].reshape(NA_HEADS, 2 * NA_KH, LANES)[:, :2 * NA_KH - 1, :2 * NA_KW - 1]
    grad["na_rpb"] = grad["na_rpb"].reshape(NA_HEADS, -1)
    grad["sink_logit"] = total[4 * rows + n_rpb:4 * rows + n_rpb + 1, 0:NB_HEADS]
    loss = total[4 * rows + n_rpb + 8, 0]

    deltas, new_m, new_v, grads_out = {}, {}, {}, {}
    for k in WEIGHTS:
        shape = w[k].shape
        if k in mine:
            res = _adamw_halves(shard(w[k], k), mine[k], other[k], shard(mom[k], k), shard(var[k], k), cidx, name="adamw_" + k)
            grads_out[k], deltas[k], new_m[k], new_v[k] = (unshard(a, k) for a in res)
        else:
            g2d = grad[k]
            d, mn, vn = _adamw(w[k].reshape(g2d.shape), g2d, mom[k].reshape(g2d.shape), var[k].reshape(g2d.shape), name="adamw_" + k)
            grads_out[k], deltas[k], new_m[k], new_v[k] = (a.reshape(shape) for a in (g2d, d, mn, vn))
    return (loss, out["dx"].reshape(x.shape), *[grads_out[k] for k in WEIGHTS], *[deltas[k] for k in WEIGHTS],
            *[new_m[k] for k in WEIGHTS], *[new_v[k] for k in WEIGHTS])
```

```python
import math

import jax
import jax.numpy as jnp
import numpy as np
from jax import lax
from jax.experimental import pallas as pl
from jax.experimental.pallas import tpu as pltpu

F32 = jnp.float32
BF16 = jnp.bfloat16

D_MODEL = 1024
HEAD_DIM = 64
NA_HEADS = 8
NB_HEADS = 8
GRID_W = 64
NA_KH = 8
NA_KW = 16
WIN = 128
ROPE_THETA = 10000.0
EPS = 1e-6
N_CHIPS = 4
QK_SCALE = HEAD_DIM ** -0.5
NEG = -1e30
LANES = 128
VMEM_LIMIT = 56 * 1024 * 1024
HEAD_ROWS = 256
WGRAD_TOKENS_BYTES = 8192

C_QKVA = 3 * NA_HEADS * HEAD_DIM
C_QB = NB_HEADS * HEAD_DIM
C_KB = 2 * HEAD_DIM
C_ROPE = C_QB + C_KB
C_GATES = 2 * D_MODEL
D_IN = C_QKVA + C_QB + 2 * C_KB + C_GATES
O_QB = C_QKVA
O_KB = O_QB + C_QB
O_VB = O_KB + C_KB
O_G = O_VB + C_KB

ADAM_LR = 0.001
ADAM_B1 = 0.9
ADAM_B2 = 0.999
ADAM_EPS = 1e-08
ADAM_WD = 0.01
ADAM_STEP = 10

MESH = pl.DeviceIdType.MESH


def _dot(a, b):
    return jnp.dot(a, b, preferred_element_type=F32)


def _dot_nt(a, b):
    return lax.dot_general(a, b, (((1,), (1,)), ((), ())), preferred_element_type=F32)


def _dot_tn(a, b):
    return lax.dot_general(a, b, (((0,), (0,)), ((), ())), preferred_element_type=F32)


def _params(n_axes):
    return pltpu.CompilerParams(dimension_semantics=("arbitrary",) * n_axes, vmem_limit_bytes=VMEM_LIMIT)


def _rstd(xf):
    return lax.rsqrt(jnp.mean(xf * xf, axis=-1, keepdims=True) + EPS)


def _norm_bwd(dn, xf, g, r):
    xhat = xf * r
    dxh = dn * g
    dx = r * (dxh - xhat * jnp.mean(dxh * xhat, axis=-1, keepdims=True))
    return dx, dn * xhat


def _sigmoid(x):
    return 0.5 * jnp.tanh(0.5 * x) + 0.5


def _loss_head(hf, gv, tgt):
    r = _rstd(hf)
    err = (hf * r) * gv - tgt
    dx, dgr = _norm_bwd(err * (1.0 / hf.shape[-1]), hf, gv, r)
    return 0.5 * jnp.mean(err * err, axis=-1, keepdims=True), dx, dgr


def _ffn_fwd(x, g, wg, wu, wd, *, name, tm=1024, sub=512, rider=None, head=None):
    T, D = x.shape
    F = wg.shape[1]
    tm = min(tm, T)
    sub = min(sub, tm)
    n_head = 0 if head is None else 2

    def body(*refs):
        x_ref, g_ref, wg_ref, wu_ref, wd_ref = refs[:5]
        h_ref, n_ref, hdn_ref, p_ref, q_ref = refs[5 + n_head:10 + n_head]
        i, s = pl.program_id(0), pl.program_id(1)
        _ffn_fwd_step(x_ref, g_ref, wg_ref, wu_ref, wd_ref, h_ref, n_ref, hdn_ref, p_ref, q_ref, s)
        if head is not None:
            gf_ref, t_ref = refs[5:7]
            loss_ref, dgf_ref = refs[10 + n_head:]

            @pl.when((i == 0) & (s == 0))
            def _():
                loss_ref[...] = jnp.zeros_like(loss_ref)
                dgf_ref[...] = jnp.zeros_like(dgf_ref)

            @pl.when(s == N_CHIPS - 1)
            def _():
                for u in range(tm // HEAD_ROWS):
                    r = pl.ds(u * HEAD_ROWS, HEAD_ROWS)
                    terms, dh, dgr = _loss_head(h_ref[r, :], gf_ref[...], t_ref[r, :])
                    loss_ref[...] += jnp.broadcast_to(jnp.sum(terms), loss_ref.shape)
                    dgf_ref[...] += jnp.sum(dgr, axis=0, keepdims=True)
                    h_ref[r, :] = dh

    def _ffn_fwd_step(x_ref, g_ref, wg_ref, wu_ref, wd_ref, h_ref, n_ref, hdn_ref, p_ref, q_ref, s):

        @pl.when(s == 0)
        def _():
            xf = x_ref[...]
            n_ref[...] = ((xf * _rstd(xf)) * g_ref[...]).astype(BF16)
            h_ref[...] = xf

        rows = [pl.ds(u * sub, sub) for u in range(tm // sub)]
        ab = [(_dot_nt(n_ref[r, :], wg_ref[0]), _dot_nt(n_ref[r, :], wu_ref[0])) for r in rows]
        hdns = []
        for r, (a, b) in zip(rows, ab):
            sg = _sigmoid(a)
            silu = a * sg
            hdn = (silu * b).astype(BF16)
            hdn_ref[0, r, :] = hdn
            p_ref[0, r, :] = (b * (sg + silu * (1.0 - sg))).astype(BF16)
            q_ref[0, r, :] = silu.astype(BF16)
            hdns.append(hdn)
        for r, hdn in zip(rows, hdns):
            h_ref[r, :] += 0.5 * _dot(hdn, wd_ref[0])

    tok = pl.BlockSpec((tm, D), lambda i, s: (i, 0))
    hid = pl.BlockSpec((1, tm, F), lambda i, s: (s, i, 0))
    wspec = pl.BlockSpec((1, F, D), lambda i, s: (s, 0, 0))
    hshape = jax.ShapeDtypeStruct((N_CHIPS, T, F), BF16)
    grid = (T // tm, N_CHIPS)
    vec = pl.BlockSpec((1, D), lambda i, s: (0, 0))
    head_in, head_in_specs, head_out, head_out_specs = [], [], [], []
    if head is not None:
        head_in, head_in_specs = list(head), [vec, tok]
        head_out = [jax.ShapeDtypeStruct((1, LANES), F32), jax.ShapeDtypeStruct((1, D), F32)]
        head_out_specs = [pl.BlockSpec((1, LANES), lambda i, s: (0, 0)), vec]
    n_main = 5 + n_head
    body, r_in, r_in_specs, r_out, r_out_specs, scratch = _ride(body, n_main, n_main, rider, grid, (grid[0] * grid[1] * 7) // 8)
    outs = pl.pallas_call(
        body, name=name, grid=grid,
        in_specs=[tok, vec, wspec, wspec, wspec] + head_in_specs + r_in_specs,
        out_specs=[tok, tok, hid, hid, hid] + head_out_specs + r_out_specs,
        out_shape=[jax.ShapeDtypeStruct((T, D), F32), jax.ShapeDtypeStruct((T, D), BF16), hshape, hshape, hshape]
        + head_out + r_out,
        scratch_shapes=scratch,
        compiler_params=_params(2),
    )(x, g, wg, wu, wd, *head_in, *r_in)
    return (*outs[:n_main], list(outs[n_main:]))


def _ffn_bwd(dh, x, g, p, q, wg, wu, wd, *, name, tm=1024, sub=256, rider=None):
    T, D = x.shape
    F = wg.shape[1]
    tm = min(tm, T)
    sub = min(sub, tm)

    def body(dh_ref, x_ref, g_ref, p_ref, q_ref, wg_ref, wu_ref, wd_ref, dx_ref, da_ref, db_ref, dg_ref):
        i, s = pl.program_id(0), pl.program_id(1)

        @pl.when((i == 0) & (s == 0))
        def _():
            dg_ref[...] = jnp.zeros_like(dg_ref)

        @pl.when(s == 0)
        def _():
            dx_ref[...] = jnp.zeros_like(dx_ref)

        rows = [pl.ds(u * sub, sub) for u in range(tm // sub)]
        dhdn = [_dot_nt((0.5 * dh_ref[r, :]).astype(BF16), wd_ref[0]) for r in rows]
        das, dbs = [], []
        for r, dd in zip(rows, dhdn):
            da = (dd * p_ref[0, r, :].astype(F32)).astype(BF16)
            db = (dd * q_ref[0, r, :].astype(F32)).astype(BF16)
            da_ref[0, r, :] = da
            db_ref[0, r, :] = db
            das.append(da)
            dbs.append(db)
        for r, da, db in zip(rows, das, dbs):
            dx_ref[r, :] += _dot(da, wg_ref[0]) + _dot(db, wu_ref[0])

        @pl.when(s == N_CHIPS - 1)
        def _():
            xf = x_ref[...]
            dx, dgr = _norm_bwd(dx_ref[...], xf, g_ref[...], _rstd(xf))
            dg_ref[...] += jnp.sum(dgr, axis=0, keepdims=True)
            dx_ref[...] = dh_ref[...] + dx

    tok = pl.BlockSpec((tm, D), lambda i, s: (i, 0))
    hid = pl.BlockSpec((1, tm, F), lambda i, s: (s, i, 0))
    vec = pl.BlockSpec((1, D), lambda i, s: (0, 0))
    hshape = jax.ShapeDtypeStruct((N_CHIPS, T, F), BF16)
    wspec = pl.BlockSpec((1, F, D), lambda i, s: (s, 0, 0))
    grid = (T // tm, N_CHIPS)
    body, r_in, r_in_specs, r_out, r_out_specs, scratch = _ride(body, 8, 4, rider, grid, None)
    outs = pl.pallas_call(
        body, name=name, grid=grid,
        in_specs=[tok, tok, vec, hid, hid, wspec, wspec, wspec] + r_in_specs,
        out_specs=[tok, hid, hid, vec] + r_out_specs,
        out_shape=[jax.ShapeDtypeStruct((T, D), F32), hshape, hshape, jax.ShapeDtypeStruct((1, D), F32)] + r_out,
        scratch_shapes=scratch,
        compiler_params=_params(2),
    )(dh, x, g, p, q, wg, wu, wd, *r_in)
    return (*outs[:4], list(outs[4:]))


def _wgrad(a, b, *, a_block, a_map, b_block, b_map, out_shape, o_block, o_map, grid, scale=1.0, name, rider=None):
    def body(a_ref, b_ref, o_ref):
        @pl.when(pl.program_id(len(grid) - 1) == 0)
        def _():
            o_ref[...] = jnp.zeros_like(o_ref)

        av = a_ref[...]
        bv = b_ref[...]
        av = av.reshape(av.shape[-2:]).astype(BF16)
        bv = bv.reshape(bv.shape[-2:])
        if scale != 1.0:
            bv = scale * bv
        o_ref[...] += _dot_tn(av, bv.astype(BF16)).reshape(o_ref.shape)

    body, r_in, r_in_specs, r_out, r_out_specs, scratch = _ride(body, 2, 1, rider, grid, None)
    outs = pl.pallas_call(
        body, name=name, grid=grid,
        in_specs=[pl.BlockSpec(a_block, a_map), pl.BlockSpec(b_block, b_map)] + r_in_specs,
        out_specs=[pl.BlockSpec(o_block, o_map)] + r_out_specs,
        out_shape=[jax.ShapeDtypeStruct(out_shape, F32)] + r_out,
        scratch_shapes=scratch,
        compiler_params=_params(len(grid)),
    )(a, b, *r_in)
    return outs[0] if rider is None else (outs[0], list(outs[1:]))


def _wgrad_rows(a, b, n_blocks, *, name, tk=2048):
    T, N = b.shape
    M = a.shape[1] // n_blocks
    tk = min(tk, T)
    return _wgrad(a, b, a_block=(tk, M), a_map=lambda s, k: (k, s), b_block=(tk, N), b_map=lambda s, k: (k, 0),
                  out_shape=(n_blocks, M, N), o_block=(1, M, N), o_map=lambda s, k: (s, 0, 0), grid=(n_blocks, T // tk), name=name)


def _wgrad_shard_a(a, b, *, name, scale=1.0, rider=None):
    S, T, M = a.shape
    N = b.shape[1]
    tk = min(WGRAD_TOKENS_BYTES // b.dtype.itemsize, T)
    return _wgrad(a, b, a_block=(1, tk, M), a_map=lambda s, k: (s, k, 0), b_block=(tk, N), b_map=lambda s, k: (k, 0),
                  out_shape=(S, M, N), o_block=(1, M, N), o_map=lambda s, k: (s, 0, 0), grid=(S, T // tk), scale=scale,
                  name=name, rider=rider)


def _wgrad_cols(a, b, n_blocks, *, name, tk=2048):
    T, M = a.shape
    N = b.shape[1] // n_blocks
    tk = min(tk, T)

    def body(a_ref, b_ref, o_ref):
        @pl.when(pl.program_id(0) == 0)
        def _():
            o_ref[...] = jnp.zeros_like(o_ref)

        r = _dot_tn(a_ref[...].astype(BF16), b_ref[...].astype(BF16))
        for s in range(n_blocks):
            o_ref[s] += r[:, s * N:(s + 1) * N]

    return pl.pallas_call(
        body, name=name, grid=(T // tk,),
        in_specs=[pl.BlockSpec((tk, M), lambda k: (k, 0)), pl.BlockSpec((tk, n_blocks * N), lambda k: (k, 0))],
        out_specs=pl.BlockSpec((n_blocks, M, N), lambda k: (0, 0, 0)),
        out_shape=jax.ShapeDtypeStruct((n_blocks, M, N), F32),
        compiler_params=_params(1),
    )(a, b)


def _rope_tables(T):
    half = HEAD_DIM // 2
    inv = np.float32(ROPE_THETA) ** (-np.arange(half, dtype=np.float32) / np.float32(half))
    ang = np.arange(T, dtype=np.float32)[:, None] * inv[None, :]
    cos, sin, zero = np.cos(ang), np.sin(ang), np.zeros_like(ang)
    reps = LANES // HEAD_DIM
    return (jnp.asarray(np.tile(np.concatenate([cos, cos], axis=1), (1, reps))),
            jnp.asarray(np.tile(np.concatenate([-sin, zero], axis=1), (1, reps))),
            jnp.asarray(np.tile(np.concatenate([zero, sin], axis=1), (1, reps))))


def _rope(x, cos, sa, sb, sign):
    half = HEAD_DIM // 2
    return x * cos + sign * (pltpu.roll(x, LANES - half, 1) * sa + pltpu.roll(x, half, 1) * sb)


def _mix_in_fwd(h, g, w_in, tables, *, tm=512):
    T, D = h.shape

    def body(h_ref, g_ref, w_ref, cos_ref, sa_ref, sb_ref, u_ref, qkva_ref, qb_ref, kvb_ref, gates_ref):
        hf = h_ref[...]
        u = ((hf * _rstd(hf)) * g_ref[...]).astype(BF16)
        u_ref[...] = u
        qkva_ref[...] = _dot_nt(u, w_ref[0:C_QKVA, :]).astype(BF16)
        zr = _dot_nt(u, w_ref[O_QB:O_QB + C_ROPE, :])
        cos, sa, sb = cos_ref[...], sa_ref[...], sb_ref[...]
        for j in range(C_ROPE // LANES):
            rj = _rope(zr[:, j * LANES:(j + 1) * LANES], cos, sa, sb, 1.0).astype(BF16)
            if j < C_QB // LANES:
                qb_ref[:, j * LANES:(j + 1) * LANES] = rj
            else:
                kvb_ref[:, 0:C_KB] = rj
        kvb_ref[:, C_KB:2 * C_KB] = _dot_nt(u, w_ref[O_VB:O_VB + C_KB, :]).astype(BF16)
        gates_ref[...] = _dot_nt(u, w_ref[O_G:O_G + C_GATES, :])

    def tok(n):
        return pl.BlockSpec((tm, n), lambda i: (i, 0))

    return pl.pallas_call(
        body, name="mix_in_fwd", grid=(T // tm,),
        in_specs=[tok(D), pl.BlockSpec((1, D), lambda i: (0, 0)), pl.BlockSpec((D_IN, D), lambda i: (0, 0), pipeline_mode=pl.Buffered(1)),
                  tok(LANES), tok(LANES), tok(LANES)],
        out_specs=[tok(D), tok(C_QKVA), tok(C_QB), tok(2 * C_KB), tok(C_GATES)],
        out_shape=[jax.ShapeDtypeStruct((T, D), BF16), jax.ShapeDtypeStruct((T, C_QKVA), BF16),
                   jax.ShapeDtypeStruct((T, C_QB), BF16), jax.ShapeDtypeStruct((T, 2 * C_KB), BF16),
                   jax.ShapeDtypeStruct((T, C_GATES), F32)],
        compiler_params=_params(1),
    )(h, g, w_in, *tables)


def _mix_in_bwd(dqa, dka, dva, dqb, dkb, dvb, dgates, h, g, dres, w_in, tables, *, tm=512):
    T, D = h.shape

    def body(dqa_ref, dka_ref, dva_ref, dqb_ref, dkb_ref, dvb_ref, dgt_ref, h_ref, g_ref, dres_ref, w_ref,
             cos_ref, sa_ref, sb_ref, dz_ref, dh_ref, dg_ref):
        @pl.when(pl.program_id(0) == 0)
        def _():
            dg_ref[...] = jnp.zeros_like(dg_ref)

        na = NA_HEADS * HEAD_DIM
        dz_ref[:, 0:na] = dqa_ref[...].astype(BF16)
        dz_ref[:, na:2 * na] = dka_ref[...].astype(BF16)
        dz_ref[:, 2 * na:3 * na] = dva_ref[...].astype(BF16)
        cos, sa, sb = cos_ref[...], sa_ref[...], sb_ref[...]
        for j in range(C_QB // LANES):
            dz_ref[:, O_QB + j * LANES:O_QB + (j + 1) * LANES] = _rope(
                dqb_ref[:, j * LANES:(j + 1) * LANES], cos, sa, sb, -1.0).astype(BF16)
        dz_ref[:, O_KB:O_KB + C_KB] = _rope(dkb_ref[...], cos, sa, sb, -1.0).astype(BF16)
        dz_ref[:, O_VB:O_VB + C_KB] = dvb_ref[...].astype(BF16)
        dz_ref[:, O_G:O_G + C_GATES] = dgt_ref[...].astype(BF16)
        du = _dot(dz_ref[...], w_ref[...])
        hf = h_ref[...]
        dx, dgr = _norm_bwd(du, hf, g_ref[...], _rstd(hf))
        dg_ref[...] += jnp.sum(dgr, axis=0, keepdims=True)
        dh_ref[...] = dres_ref[...] + dx

    def tok(n):
        return pl.BlockSpec((tm, n), lambda i: (i, 0))

    vec = pl.BlockSpec((1, D), lambda i: (0, 0))
    na = NA_HEADS * HEAD_DIM
    return pl.pallas_call(
        body, name="mix_in_bwd", grid=(T // tm,),
        in_specs=[tok(na), tok(na), tok(na), tok(C_QB), tok(C_KB), tok(C_KB), tok(C_GATES), tok(D), vec, tok(D),
                  pl.BlockSpec((D_IN, D), lambda i: (0, 0), pipeline_mode=pl.Buffered(1)), tok(LANES), tok(LANES), tok(LANES)],
        out_specs=[tok(D_IN), tok(D), vec],
        out_shape=[jax.ShapeDtypeStruct((T, D_IN), BF16), jax.ShapeDtypeStruct((T, D), F32),
                   jax.ShapeDtypeStruct((1, D), F32)],
        compiler_params=_params(1),
    )(dqa, dka, dva, dqb, dkb, dvb, dgates, h, g, dres, w_in, *tables)


def _na_bias_slabs(rpb):
    H = rpb.shape[0]
    ncell = GRID_W * GRID_W
    cell = np.arange(ncell)
    co = cell % GRID_W - cell // GRID_W + (NA_KW - 1)
    e_co = jnp.asarray((np.arange(LANES)[:, None] == co[None, :]).astype(np.float32))
    table = jnp.pad(rpb, ((0, 0), (0, 1), (0, LANES - rpb.shape[2]))).reshape(H * 2 * NA_KH, LANES)

    def body(t_ref, e_ref, o_ref):
        o_ref[...] = jnp.dot(t_ref[...], e_ref[...], preferred_element_type=F32, precision=lax.Precision.HIGHEST)

    toeplitz = pl.pallas_call(
        body, name="rpb_unfold", out_shape=jax.ShapeDtypeStruct((H * 2 * NA_KH, ncell), F32),
        compiler_params=_params(0),
    )(table, e_co).reshape(H, 2 * NA_KH, GRID_W, GRID_W)

    def assemble(tz_ref, o_ref):
        c = lax.broadcasted_iota(jnp.int32, (GRID_W, GRID_W), 0)
        k = lax.broadcasted_iota(jnp.int32, (GRID_W, GRID_W), 1)
        cs = jnp.clip(c - NA_KW // 2, 0, GRID_W - NA_KW)
        inwin = (k >= cs) & (k < cs + NA_KW)
        for ro0 in range(NA_KH):
            for hh in range(2):
                for i in range(NA_KH):
                    o_ref[0, ro0, hh * GRID_W:(hh + 1) * GRID_W, i * GRID_W:(i + 1) * GRID_W] = jnp.where(
                        inwin, tz_ref[hh, ro0 + i], NEG)

    return pl.pallas_call(
        assemble, name="na_bias_slabs", grid=(H // 2,),
        in_specs=[pl.BlockSpec((2, 2 * NA_KH, GRID_W, GRID_W), lambda p: (p, 0, 0, 0))],
        out_specs=pl.BlockSpec((1, NA_KH, 2 * GRID_W, NA_KH * GRID_W), lambda p: (p, 0, 0, 0)),
        out_shape=jax.ShapeDtypeStruct((H // 2, NA_KH, 2 * GRID_W, NA_KH * GRID_W), F32),
        compiler_params=_params(1),
    )(toeplitz)


def _half_masks(rows):
    lane = lax.broadcasted_iota(jnp.int32, (rows, LANES), 1)
    left = lane < HEAD_DIM
    return left, (left, jnp.logical_not(left))


def _stack_heads(x):
    left, halves = _half_masks(x.shape[0])
    xf = x.astype(F32)
    return jnp.concatenate([jnp.where(m, xf, 0.0).astype(BF16) for m in halves], axis=0)


def _unstack_heads(o):
    rows = o.shape[0] // 2
    left, _ = _half_masks(rows)
    return jnp.where(left, o[:rows], o[rows:])


def _na_row(j, t, rb, rows):
    r = j * rb + t
    rs = jnp.clip(r - NA_KH // 2, 0, rows - NA_KH)
    return pl.multiple_of(t * GRID_W, GRID_W), pl.multiple_of(rs * GRID_W, GRID_W), rs - r + (NA_KH - 1)


def _na_specs(T, rb):
    qrows = GRID_W * rb
    pairs = NA_HEADS // 2
    return ([pl.BlockSpec((qrows, LANES), lambda p, j: (j, p)),
             pl.BlockSpec((T, LANES), lambda p, j: (0, pairs + p)),
             pl.BlockSpec((T, LANES), lambda p, j: (0, 2 * pairs + p))],
            pl.BlockSpec((1, NA_KH, 2 * GRID_W, NA_KH * GRID_W), lambda p, j: (p, 0, 0, 0)))


def _softmax(s):
    p = jnp.exp(s - jnp.max(s, axis=-1, keepdims=True))
    return p / jnp.sum(p, axis=-1, keepdims=True)


def _na_fwd(qkva, bias, *, rb=8, group=8):
    T = qkva.shape[0]
    rows = T // GRID_W
    nkeys = NA_KH * GRID_W

    def body(q_ref, k_ref, v_ref, bias_ref, y_ref):
        j = pl.program_id(1)

        def rows_step(t, carry):
            at = [_na_row(j, t * group + u, rb, rows) for u in range(group)]
            s = [_dot_nt(_stack_heads(q_ref[pl.ds(q0, GRID_W), :]), k_ref[pl.ds(k0, nkeys), :]) for q0, k0, _ in at]
            p = [_softmax(su * QK_SCALE + bias_ref[0, ro0]) for su, (_, _, ro0) in zip(s, at)]
            o = [_dot(pu.astype(BF16), v_ref[pl.ds(k0, nkeys), :]) for pu, (_, k0, _) in zip(p, at)]
            for ou, (q0, _, _) in zip(o, at):
                y_ref[pl.ds(q0, GRID_W), :] = _unstack_heads(ou).astype(BF16)
            return carry

        lax.fori_loop(0, rb // group, rows_step, 0)

    qkv_specs, bias_spec = _na_specs(T, rb)
    return pl.pallas_call(
        body, name="na_fwd", grid=(NA_HEADS // 2, rows // rb),
        in_specs=qkv_specs + [bias_spec],
        out_specs=qkv_specs[0],
        out_shape=jax.ShapeDtypeStruct((T, NA_HEADS * HEAD_DIM), BF16),
        compiler_params=_params(2),
    )(qkva, qkva, qkva, bias)


def _na_bwd(qkva, dy, bias, *, rb=8, group=8, rider=None):
    T = qkva.shape[0]
    rows = T // GRID_W
    nkeys = NA_KH * GRID_W

    def body(q_ref, k_ref, v_ref, dy_ref, bias_ref, dq_ref, dk_ref, dv_ref, dbias_ref):
        j = pl.program_id(1)

        @pl.when(j == 0)
        def _():
            dk_ref[...] = jnp.zeros_like(dk_ref)
            dv_ref[...] = jnp.zeros_like(dv_ref)
            dbias_ref[...] = jnp.zeros_like(dbias_ref)

        def rows_step(t, carry):
            at = [_na_row(j, t * group + u, rb, rows) for u in range(group)]
            qs = [_stack_heads(q_ref[pl.ds(q0, GRID_W), :]) for q0, _, _ in at]
            dys = [_stack_heads(dy_ref[pl.ds(q0, GRID_W), :]) for q0, _, _ in at]
            s = [_dot_nt(qu, k_ref[pl.ds(k0, nkeys), :]) for qu, (_, k0, _) in zip(qs, at)]
            dp = [_dot_nt(du, v_ref[pl.ds(k0, nkeys), :]) for du, (_, k0, _) in zip(dys, at)]
            p = [_softmax(su * QK_SCALE + bias_ref[0, ro0]) for su, (_, _, ro0) in zip(s, at)]
            ds = [pu * (du - jnp.sum(pu * du, axis=-1, keepdims=True)) for pu, du in zip(p, dp)]
            for u, (q0, k0, ro0) in enumerate(at):
                dbias_ref[0, ro0] += ds[u]
                dsb = ds[u].astype(BF16)
                dq_ref[pl.ds(q0, GRID_W), :] = (_unstack_heads(_dot(dsb, k_ref[pl.ds(k0, nkeys), :])) * QK_SCALE).astype(BF16)
                dk_ref[pl.ds(k0, nkeys), :] += _dot_tn(dsb, qs[u]) * QK_SCALE
                dv_ref[pl.ds(k0, nkeys), :] += _dot_tn(p[u].astype(BF16), dys[u])
            return carry

        lax.fori_loop(0, rb // group, rows_step, 0)

    qkv_specs, bias_spec = _na_specs(T, rb)
    width = NA_HEADS * HEAD_DIM
    kv_out = pl.BlockSpec((T, LANES), lambda p, j: (0, p))
    grid = (NA_HEADS // 2, rows // rb)
    body, r_in, r_in_specs, r_out, r_out_specs, scratch = _ride(body, 5, 4, rider, grid, None)
    outs = pl.pallas_call(
        body, name="na_bwd", grid=grid,
        in_specs=qkv_specs + [qkv_specs[0], bias_spec] + r_in_specs,
        out_specs=[qkv_specs[0], kv_out, kv_out, bias_spec] + r_out_specs,
        out_shape=[jax.ShapeDtypeStruct((T, width), BF16), jax.ShapeDtypeStruct((T, width), F32),
                   jax.ShapeDtypeStruct((T, width), F32), jax.ShapeDtypeStruct(bias.shape, F32)] + r_out,
        scratch_shapes=scratch,
        compiler_params=_params(2),
    )(qkva, qkva, qkva, dy, bias, *r_in)
    return (*outs[:4], list(outs[4:]))


def _rpb_fold(dslab):
    pairs = dslab.shape[0]
    H = 2 * pairs
    ncell = GRID_W * GRID_W

    def disassemble(d_ref, tz_ref):
        tz_ref[...] = jnp.zeros_like(tz_ref)
        for ro0 in range(NA_KH):
            for hh in range(2):
                for i in range(NA_KH):
                    tz_ref[hh, ro0 + i] += d_ref[0, ro0, hh * GRID_W:(hh + 1) * GRID_W, i * GRID_W:(i + 1) * GRID_W]

    dtoeplitz = pl.pallas_call(
        disassemble, name="rpb_fold_tiles", grid=(pairs,),
        in_specs=[pl.BlockSpec((1, NA_KH, 2 * GRID_W, NA_KH * GRID_W), lambda p: (p, 0, 0, 0))],
        out_specs=pl.BlockSpec((2, 2 * NA_KH, GRID_W, GRID_W), lambda p: (p, 0, 0, 0)),
        out_shape=jax.ShapeDtypeStruct((H, 2 * NA_KH, GRID_W, GRID_W), F32),
        compiler_params=_params(1),
    )(dslab).reshape(H * 2 * NA_KH, ncell)
    cell = np.arange(ncell)
    co = cell % GRID_W - cell // GRID_W + (NA_KW - 1)
    e_co = jnp.asarray((co[:, None] == np.arange(LANES)[None, :]).astype(np.float32))

    def diagonals(x_ref, e_ref, o_ref):
        o_ref[...] = jnp.dot(x_ref[...], e_ref[...], preferred_element_type=F32, precision=lax.Precision.HIGHEST)

    return pl.pallas_call(
        diagonals, name="rpb_fold", out_shape=jax.ShapeDtypeStruct((H * 2 * NA_KH, LANES), F32),
        compiler_params=_params(0),
    )(dtoeplitz, e_co).reshape(H, 2 * NA_KH, LANES)


SWA_KEYS = 3 * WIN


def _swa_block(j, t, qbn, T):
    blk = j * qbn + t
    start = jnp.clip((blk - 1) * WIN, 0, T - SWA_KEYS)
    row = lax.broadcasted_iota(jnp.int32, (2 * WIN, SWA_KEYS), 0)
    qpos = blk * WIN + jnp.where(row < WIN, row, row - WIN)
    kpos = start + lax.broadcasted_iota(jnp.int32, (2 * WIN, SWA_KEYS), 1)
    return pl.multiple_of(t * WIN, WIN), pl.multiple_of(start, WIN), jnp.abs(qpos - kpos) <= WIN


def _swa_sinks(sink_ref, p):
    row = lax.broadcasted_iota(jnp.int32, (2 * WIN, 1), 0)
    return jnp.where(row < WIN, sink_ref[p], sink_ref[p + NB_HEADS // 2])


def _swa_probs(s, mask, sink):
    s = jnp.where(mask, s * QK_SCALE, NEG)
    m = jnp.maximum(jnp.max(s, axis=-1, keepdims=True), sink)
    e = jnp.exp(s - m)
    esink = jnp.exp(sink - m)
    den = jnp.sum(e, axis=-1, keepdims=True) + esink
    return e / den, esink / den


def _swa_specs(T, qbn):
    return [pl.BlockSpec(memory_space=pltpu.SMEM),
            pl.BlockSpec((WIN * qbn, LANES), lambda p, j: (j, p)),
            pl.BlockSpec((T, LANES), lambda p, j: (0, 0)),
            pl.BlockSpec((T, LANES), lambda p, j: (0, 1))]


def _swa_fwd(qb, kvb, sink, *, qbn=8, group=8):
    T = qb.shape[0]
    pairs = NB_HEADS // 2
    qbn = min(qbn, T // WIN)
    group = min(group, qbn)

    def body(sink_ref, q_ref, k_ref, v_ref, y_ref):
        p, j = pl.program_id(0), pl.program_id(1)
        sinks = _swa_sinks(sink_ref, p)

        def blocks_step(t, carry):
            at = [_swa_block(j, t * group + u, qbn, T) for u in range(group)]
            s = [_dot_nt(_stack_heads(q_ref[pl.ds(q0, WIN), :]), k_ref[pl.ds(k0, SWA_KEYS), :]) for q0, k0, _ in at]
            pr = [_swa_probs(su, mask, sinks)[0] for su, (_, _, mask) in zip(s, at)]
            o = [_dot(pu.astype(BF16), v_ref[pl.ds(k0, SWA_KEYS), :]) for pu, (_, k0, _) in zip(pr, at)]
            for ou, (q0, _, _) in zip(o, at):
                y_ref[pl.ds(q0, WIN), :] = _unstack_heads(ou).astype(BF16)
            return carry

        lax.fori_loop(0, qbn // group, blocks_step, 0)

    specs = _swa_specs(T, qbn)
    return pl.pallas_call(
        body, name="swa_fwd", grid=(pairs, T // (WIN * qbn)),
        in_specs=specs, out_specs=specs[1],
        out_shape=jax.ShapeDtypeStruct((T, NB_HEADS * HEAD_DIM), BF16),
        compiler_params=_params(2),
    )(sink, qb, kvb, kvb)


def _swa_bwd(qb, kvb, dy, sink, *, qbn=8, group=8, rider=None):
    T = qb.shape[0]
    pairs = NB_HEADS // 2
    qbn = min(qbn, T // WIN)
    group = min(group, qbn)

    def body(sink_ref, q_ref, k_ref, v_ref, dy_ref, dq_ref, dk_ref, dv_ref, dsink_ref):
        p, j = pl.program_id(0), pl.program_id(1)
        sinks = _swa_sinks(sink_ref, p)

        @pl.when((p == 0) & (j == 0))
        def _():
            dk_ref[...] = jnp.zeros_like(dk_ref)
            dv_ref[...] = jnp.zeros_like(dv_ref)

        @pl.when(j == 0)
        def _():
            dsink_ref[...] = jnp.zeros_like(dsink_ref)

        def blocks_step(t, carry):
            at = [_swa_block(j, t * group + u, qbn, T) for u in range(group)]
            qs = [_stack_heads(q_ref[pl.ds(q0, WIN), :]) for q0, _, _ in at]
            dys = [_stack_heads(dy_ref[pl.ds(q0, WIN), :]) for q0, _, _ in at]
            s = [_dot_nt(qu, k_ref[pl.ds(k0, SWA_KEYS), :]) for qu, (_, k0, _) in zip(qs, at)]
            dp = [_dot_nt(du, v_ref[pl.ds(k0, SWA_KEYS), :]) for du, (_, k0, _) in zip(dys, at)]
            probs = [_swa_probs(su, mask, sinks) for su, (_, _, mask) in zip(s, at)]
            for u, (q0, k0, _) in enumerate(at):
                pr, psink = probs[u]
                delta = jnp.sum(pr * dp[u], axis=-1, keepdims=True)
                dsb = (pr * (dp[u] - delta)).astype(BF16)
                dsk = psink * delta
                for hh in range(2):
                    dsink_ref[0, hh:hh + 1, :] += jnp.broadcast_to(-jnp.sum(dsk[hh * WIN:(hh + 1) * WIN]), (1, LANES))
                dq_ref[pl.ds(q0, WIN), :] = _unstack_heads(_dot(dsb, k_ref[pl.ds(k0, SWA_KEYS), :])) * QK_SCALE
                dk_ref[pl.ds(k0, SWA_KEYS), :] += _dot_tn(dsb, qs[u]) * QK_SCALE
                dv_ref[pl.ds(k0, SWA_KEYS), :] += _dot_tn(pr.astype(BF16), dys[u])
            return carry

        lax.fori_loop(0, qbn // group, blocks_step, 0)

    specs = _swa_specs(T, qbn)
    kv_out = pl.BlockSpec((T, LANES), lambda p, j: (0, 0))
    grid = (pairs, T // (WIN * qbn))
    body, r_in, r_in_specs, r_out, r_out_specs, scratch = _ride(body, 5, 4, rider, grid, None)
    outs = pl.pallas_call(
        body, name="swa_bwd", grid=grid,
        in_specs=specs + [specs[1]] + r_in_specs,
        out_specs=[specs[1], kv_out, kv_out, pl.BlockSpec((1, 8, LANES), lambda p, j: (p, 0, 0))] + r_out_specs,
        out_shape=[jax.ShapeDtypeStruct((T, NB_HEADS * HEAD_DIM), F32), jax.ShapeDtypeStruct((T, LANES), F32),
                   jax.ShapeDtypeStruct((T, LANES), F32), jax.ShapeDtypeStruct((pairs, 8, LANES), F32)] + r_out,
        scratch_shapes=scratch,
        compiler_params=_params(2),
    )(sink, qb, kvb, kvb, dy, *r_in)
    return (*outs[:4], list(outs[4:]))


def _merge_fwd(ya, yb, gates, wa, wb, wout, h, *, tm=512):
    T, D = h.shape
    W = ya.shape[1]

    def body(ya_ref, yb_ref, gt_ref, wa_ref, wb_ref, wo_ref, h_ref, h2_ref, mg_ref):
        pa = _dot(ya_ref[...], wa_ref[...])
        pb = _dot(yb_ref[...], wb_ref[...])
        mg = (jax.nn.sigmoid(gt_ref[:, 0:D]) * pa + jax.nn.sigmoid(gt_ref[:, D:2 * D]) * pb).astype(BF16)
        mg_ref[...] = mg
        h2_ref[...] = h_ref[...] + _dot(mg, wo_ref[...])

    def tok(n):
        return pl.BlockSpec((tm, n), lambda i: (i, 0))

    def full(r, c):
        return pl.BlockSpec((r, c), lambda i: (0, 0))

    return pl.pallas_call(
        body, name="merge_fwd", grid=(T // tm,),
        in_specs=[tok(W), tok(W), tok(2 * D), full(W, D), full(W, D), full(D, D), tok(D)],
        out_specs=[tok(D), tok(D)],
        out_shape=[jax.ShapeDtypeStruct((T, D), F32), jax.ShapeDtypeStruct((T, D), BF16)],
        compiler_params=_params(1),
    )(ya, yb, gates, wa, wb, wout, h)


def _merge_bwd(dh, ya, yb, gates, wa, wb, wout, *, tm=512, rider=None):
    T, D = dh.shape
    W = ya.shape[1]

    def body(dh_ref, ya_ref, yb_ref, gt_ref, wa_ref, wb_ref, wo_ref, dya_ref, dyb_ref, dpa_ref, dpb_ref, dgt_ref):
        dmg = _dot_nt(dh_ref[...].astype(BF16), wo_ref[...])
        for y_ref, w_ref, dy_ref, dp_ref, lo in ((ya_ref, wa_ref, dya_ref, dpa_ref, 0), (yb_ref, wb_ref, dyb_ref, dpb_ref, D)):
            sg = jax.nn.sigmoid(gt_ref[:, lo:lo + D])
            dp = (dmg * sg).astype(BF16)
            dp_ref[...] = dp
            dgt_ref[:, lo:lo + D] = (dmg * _dot(y_ref[...], w_ref[...]) * (sg * (1.0 - sg))).astype(BF16)
            dy_ref[...] = _dot_nt(dp, w_ref[...]).astype(BF16)

    def tok(n):
        return pl.BlockSpec((tm, n), lambda i: (i, 0))

    def full(r, c):
        return pl.BlockSpec((r, c), lambda i: (0, 0))

    grid = (T // tm,)
    body, r_in, r_in_specs, r_out, r_out_specs, scratch = _ride(body, 7, 5, rider, grid, None)
    outs = pl.pallas_call(
        body, name="merge_bwd", grid=grid,
        in_specs=[tok(D), tok(W), tok(W), tok(2 * D), full(W, D), full(W, D), full(D, D)] + r_in_specs,
        out_specs=[tok(W), tok(W), tok(D), tok(D), tok(2 * D)] + r_out_specs,
        out_shape=[jax.ShapeDtypeStruct((T, W), BF16), jax.ShapeDtypeStruct((T, W), BF16),
                   jax.ShapeDtypeStruct((T, D), BF16), jax.ShapeDtypeStruct((T, D), BF16),
                   jax.ShapeDtypeStruct((T, 2 * D), BF16)] + r_out,
        scratch_shapes=scratch,
        compiler_params=_params(1),
    )(dh, ya, yb, gates, wa, wb, wout, *r_in)
    return (*outs[:5], list(outs[5:]))


def _pair_heads(a, axis):
    shp = a.shape
    a = a.reshape(shp[:axis] + (2, NB_HEADS // 2, HEAD_DIM) + shp[axis + 1:])
    return jnp.swapaxes(a, axis, axis + 1).reshape(shp)


def _unpair_heads(a, axis):
    shp = a.shape
    a = a.reshape(shp[:axis] + (NB_HEADS // 2, 2, HEAD_DIM) + shp[axis + 1:])
    return jnp.swapaxes(a, axis, axis + 1).reshape(shp)


FFN1 = ("ffn1_w_gate", "ffn1_w_up", "ffn1_w_down")
FFN2 = ("ffn2_w_gate", "ffn2_w_up", "ffn2_w_down")
MIXER = ("w_in", "w_branch_a", "w_branch_b", "w_out")


def _layer_grads(x, target, g1, f1, gmix, late, rpb, sink, g2, gfin, comm=None):
    T = x.shape[0]
    tables = _rope_tables(T)
    bias = _na_bias_slabs(rpb)

    h1, n1, hdn1, p1, q1, gathered = _ffn_fwd(x, g1, *f1, name="ffn1_fwd", rider=comm.late_rider if comm else None)
    w_in_t, wa, wb, wout, f2 = comm.late(gathered) if comm else late
    w_in_p = jnp.concatenate([w_in_t[:O_QB], _pair_heads(w_in_t[O_QB:O_KB], 0), w_in_t[O_KB:]], axis=0)
    wb_p = _pair_heads(wb, 0)
    u, qkva, qb, kvb, gates = _mix_in_fwd(h1, gmix, w_in_p, tables)
    ya = _na_fwd(qkva, bias)
    yb = _swa_fwd(qb, kvb, sink)
    h2, merged = _merge_fwd(ya, yb, gates, wa, wb_p, wout, h1)
    dh3, n2, hdn2, p2, q2, loss, dgfin, _ = _ffn_fwd(h2, g2, *f2, name="ffn2_fwd", head=(gfin, target))

    dh2, da2, db2, dg2, _ = _ffn_bwd(dh3, h2, g2, p2, q2, *f2, name="ffn2_bwd")
    df2 = [_wgrad_shard_a(da2, n2, name="ffn2_dwg"), _wgrad_shard_a(db2, n2, name="ffn2_dwu"),
           _wgrad_shard_a(hdn2, dh3, scale=0.5, name="ffn2_dwd")]
    red2 = comm.reduce(FFN2, df2, tag="ffn2") if comm else None
    dya, dyb, dpa, dpb, dgates, got = _merge_bwd(dh2, ya, yb, gates, wa, wb_p, wout, rider=red2.sibling if comm else None)
    dwout = _wgrad_cols(merged, dh2, 1, name="dwout").reshape(N_CHIPS, D_MODEL // N_CHIPS, D_MODEL)
    dwa = _wgrad_cols(ya, dpa, N_CHIPS, name="dwa")
    dwb = _unpair_heads(_wgrad_cols(yb, dpb, N_CHIPS, name="dwb"), 1)
    dqa, dka, dva, dbias, got = _na_bwd(qkva, dya, bias, rider=red2.partial(got).chips if comm else None)
    drpb = _rpb_fold(dbias)
    dqb, dkb, dvb, dsink, got = _swa_bwd(qb, kvb, dyb, sink, rider=red2.halves(got).share if comm else None)
    reduced2 = red2.result(got) if comm else None
    dz, dh1, dgmix = _mix_in_bwd(dqa, dka, dva, dqb, dkb, dvb, dgates, h1, gmix, dh2, w_in_p, tables)
    dwin_p = _wgrad_rows(dz, u, 2, name="dwin").reshape(D_IN, D_MODEL)
    dwin = jnp.concatenate([dwin_p[:O_QB], _unpair_heads(dwin_p[O_QB:O_KB], 0), dwin_p[O_KB:]], axis=0)
    dmix = [dwin.reshape(N_CHIPS, D_IN // N_CHIPS, D_MODEL), dwa, dwb, dwout]
    dx, da1, db1, dg1, _ = _ffn_bwd(dh1, x, g1, p1, q1, *f1, name="ffn1_bwd")
    out = dict(loss=loss, dx=dx, ffn1_norm=dg1, mix_norm=dgmix, ffn2_norm=dg2, final_norm=dgfin, na_rpb=drpb,
               sink_logit=dsink[:, 0:2, 0].T.reshape(NB_HEADS))
    if comm:
        out.update(reduced2)
        redm = comm.reduce(MIXER, dmix, tag="mixer").partial_now()
        dwg1, got = _wgrad_shard_a(da1, n1, name="ffn1_dwg", rider=redm.chips)
        dwu1, got = _wgrad_shard_a(db1, n1, name="ffn1_dwu", rider=redm.halves(got).share)
        out.update(redm.result(got))
        red1 = comm.reduce(FFN1[:2], [dwg1, dwu1], tag="ffn1_gate_up").partial_now()
        dwd1, got = _wgrad_shard_a(hdn1, dh1, scale=0.5, name="ffn1_dwd", rider=red1.chips)
        out.update(red1.halves(got).result_now())
        out.update(comm.reduce(FFN1[2:], [dwd1], tag="ffn1_down").partial_now().halves_now().result_now())
    else:
        df1 = [_wgrad_shard_a(da1, n1, name="ffn1_dwg"), _wgrad_shard_a(db1, n1, name="ffn1_dwu"),
               _wgrad_shard_a(hdn1, dh1, scale=0.5, name="ffn1_dwd")]
        out.update(zip(FFN1 + MIXER + FFN2, df1 + dmix + df2))
    return out


ANY = pl.BlockSpec(memory_space=pl.ANY)


def _place():
    x, y, c = lax.axis_index("x"), lax.axis_index("y"), lax.axis_index("c")
    chips = [(1 - x, y), (x, 1 - y), (1 - x, 1 - y)]
    return x, y, c, 2 * x + y, chips


def _remote(src, dst, send_sems, recv_sems, k, device):
    return pltpu.make_async_remote_copy(src_ref=src, dst_ref=dst, send_sem=send_sems.at[k], recv_sem=recv_sems.at[k],
                                        device_id=device, device_id_type=MESH)


class _Rider:
    def __init__(self, inputs, out_shape, scratch, start, middle, finish):
        self.inputs, self.out_shape, self.scratch = list(inputs), list(out_shape), list(scratch)
        self.start, self.middle, self.finish = start, middle, finish


def _run_rider(rider, *, name):
    n_in, n_out = len(rider.inputs), len(rider.out_shape)

    def body(*refs):
        ins, outs, sems = refs[:n_in], refs[n_in:n_in + n_out], refs[n_in + n_out:]
        rider.start(ins, outs, sems)
        if rider.middle is not None:
            rider.middle(ins, outs, sems)
        rider.finish(ins, outs, sems)

    return pl.pallas_call(body, name=name, in_specs=[ANY] * n_in, out_specs=[ANY] * n_out, out_shape=rider.out_shape,
                          scratch_shapes=rider.scratch)(*rider.inputs)


def _ride(body, n_in, n_out, rider, grid, middle_step):
    if rider is None:
        return body, [], [], [], [], []
    r_in, r_out = len(rider.inputs), len(rider.out_shape)
    steps = math.prod(grid)

    def riding(*refs):
        ins, r_ins = refs[:n_in], refs[n_in:n_in + r_in]
        outs = refs[n_in + r_in:n_in + r_in + n_out]
        r_outs = refs[n_in + r_in + n_out:n_in + r_in + n_out + r_out]
        sems = refs[n_in + r_in + n_out + r_out:]
        step = pl.program_id(0)
        for axis in range(1, len(grid)):
            step = step * grid[axis] + pl.program_id(axis)

        @pl.when(step == 0)
        def _():
            rider.start(r_ins, r_outs, sems)

        body(*ins, *outs)

        if rider.middle is not None:
            @pl.when(step == middle_step)
            def _():
                rider.middle(r_ins, r_outs, sems)

        @pl.when(step == steps - 1)
        def _():
            rider.finish(r_ins, r_outs, sems)

    return riding, rider.inputs, [ANY] * r_in, rider.out_shape, [ANY] * r_out, rider.scratch


def _gather_rider(shards):
    n = len(shards)

    def plan(ins, outs, sems, kinds):
        send_sems, recv_sems, own_send_sems, own_recv_sems = sems
        x, y, c, mine, chips = _place()
        sibling = (x, y, 1 - c)
        made = {k: [] for k in kinds}
        for i in range(n):
            hr = shards[i].shape[0] // 2
            if "own" in made:
                made["own"].append(_remote(ins[i], outs[i].at[mine], own_send_sems, own_recv_sems, i, sibling))
            for j, (cx, cy) in enumerate(chips):
                here = outs[i].at[2 * cx + cy, pl.ds(c * hr, hr)]
                there = outs[i].at[2 * cx + cy, pl.ds((1 - c) * hr, hr)]
                if "sends" in made:
                    made["sends"].append(_remote(ins[i].at[pl.ds(c * hr, hr)], outs[i].at[mine, pl.ds(c * hr, hr)],
                                                 send_sems, recv_sems, 6 * i + j, (cx, cy, c)))
                if "landed" in made:
                    made["landed"].append(_remote(here, here, send_sems, recv_sems, 6 * i + j, (cx, cy, c)))
                if "passes" in made:
                    made["passes"].append(_remote(here, here, send_sems, recv_sems, 6 * i + 3 + j, sibling))
                if "others" in made:
                    made["others"].append(_remote(there, there, send_sems, recv_sems, 6 * i + 3 + j, sibling))
        return [made[k] for k in kinds]

    def start(ins, outs, sems):
        own, sends = plan(ins, outs, sems, ("own", "sends"))
        for cp in own + sends:
            cp.start()

    def middle(ins, outs, sems):
        landed, passes = plan(ins, outs, sems, ("landed", "passes"))
        for arrived, cp in zip(landed, passes):
            arrived.wait_recv()
            cp.start()

    def finish(ins, outs, sems):
        own, sends, passes, others = plan(ins, outs, sems, ("own", "sends", "passes", "others"))
        for arrived in others:
            arrived.wait_recv()
        for cp in sends + passes:
            cp.wait_send()
        for cp in own:
            cp.wait()

    return _Rider(shards, [jax.ShapeDtypeStruct((N_CHIPS,) + s.shape, s.dtype) for s in shards],
                  [pltpu.SemaphoreType.DMA((6 * n,)), pltpu.SemaphoreType.DMA((6 * n,)),
                   pltpu.SemaphoreType.DMA((n,)), pltpu.SemaphoreType.DMA((n,))], start, middle, finish)


def _swap_rider(arrays, out_shape, source):
    n = len(arrays)

    def plan(ins, outs, sems):
        send_sems, recv_sems = sems
        x, y, c, _, _ = _place()
        return [_remote(source(ins[i], c, i), outs[i], send_sems, recv_sems, i, (x, y, 1 - c)) for i in range(n)]

    def start(ins, outs, sems):
        for cp in plan(ins, outs, sems):
            cp.start()

    def finish(ins, outs, sems):
        for cp in plan(ins, outs, sems):
            cp.wait()

    return _Rider(arrays, out_shape, [pltpu.SemaphoreType.DMA((n,)), pltpu.SemaphoreType.DMA((n,))], start, None, finish)


def _sibling_rider(grads):
    half = [g.shape[1] // 2 for g in grads]
    return _swap_rider(grads, [jax.ShapeDtypeStruct((g.shape[0], hr, g.shape[2]), g.dtype) for g, hr in zip(grads, half)],
                       lambda ref, c, i: ref.at[:, pl.ds((1 - c) * half[i], half[i])])


def _share_rider(halves):
    return _swap_rider(halves, [jax.ShapeDtypeStruct(h.shape, h.dtype) for h in halves], lambda ref, c, i: ref)


def _chips_rider(parts):
    n = len(parts)

    def plan(ins, outs, sems):
        send_sems, recv_sems = sems
        _, _, c, _, chips = _place()
        return [_remote(ins[i].at[2 * cx + cy], outs[i].at[j], send_sems, recv_sems, 3 * i + j, (cx, cy, c))
                for i in range(n) for j, (cx, cy) in enumerate(chips)]

    def start(ins, outs, sems):
        for cp in plan(ins, outs, sems):
            cp.start()

    def finish(ins, outs, sems):
        for cp in plan(ins, outs, sems):
            cp.wait()

    return _Rider(parts, [jax.ShapeDtypeStruct((N_CHIPS - 1,) + p.shape[1:], p.dtype) for p in parts],
                  [pltpu.SemaphoreType.DMA((3 * n,)), pltpu.SemaphoreType.DMA((3 * n,))], start, None, finish)


class _Reduce:
    def __init__(self, names, grads, cidx, chip, *, tag):
        self.names, self.grads, self.cidx, self.chip, self.tag = names, grads, cidx, chip, tag
        self.sibling = _sibling_rider(grads)

    def partial(self, from_sibling):
        self.from_sibling = from_sibling
        self.chips = _chips_rider([_add_sibling(g, r, self.cidx, name="add_sibling_" + k)
                                   for k, g, r in zip(self.names, self.grads, from_sibling)])
        return self

    def halves(self, from_chips):
        self.mine = [_add_chips(g, r1, r2, self.cidx, self.chip, name="add_chips_" + k)
                     for k, g, r1, r2 in zip(self.names, self.grads, self.from_sibling, from_chips)]
        self.share = _share_rider(self.mine)
        return self

    def result(self, others):
        return dict(zip(self.names, zip(self.mine, others)))

    def partial_now(self):
        return self.partial(_run_rider(self.sibling, name="rs_sibling_" + self.tag))

    def halves_now(self):
        return self.halves(_run_rider(self.chips, name="rs_chips_" + self.tag))

    def result_now(self):
        return self.result(_run_rider(self.share, name="rs_share_" + self.tag))


N_DEV = 8


def _small_allreduce(vec):
    R = vec.shape[0]

    def body(v_ref, o_ref, buf, send_sems, recv_sems):
        x, y, c, _, _ = _place()
        me = 4 * x + 2 * y + c
        buf[me] = v_ref[...]
        copies = []
        for k in range(1, N_DEV):
            peer = (x ^ (k >> 2), y ^ ((k >> 1) & 1), c ^ (k & 1))
            cp = _remote(v_ref, buf.at[me], send_sems, recv_sems, k - 1, peer)
            cp.start()
            copies.append(cp)
        for k, cp in enumerate(copies, start=1):
            cp.wait_send()
            landed = buf.at[me ^ k]
            _remote(landed, landed, send_sems, recv_sems, k - 1, (x, y, c)).wait_recv()
        acc = buf[0]
        for d in range(1, N_DEV):
            acc = acc + buf[d]
        o_ref[...] = acc

    return pl.pallas_call(
        body, name="small_allreduce",
        in_specs=[pl.BlockSpec(memory_space=pltpu.VMEM)], out_specs=pl.BlockSpec(memory_space=pltpu.VMEM),
        out_shape=jax.ShapeDtypeStruct(vec.shape, vec.dtype),
        scratch_shapes=[pltpu.VMEM((N_DEV, R, LANES), F32), pltpu.SemaphoreType.DMA((N_DEV - 1,)),
                        pltpu.SemaphoreType.DMA((N_DEV - 1,))],
    )(vec)


ELEMWISE_BLOCK = 256 * 1024


def _row_tile(rows, cols):
    best = None
    for t in range(8, rows + 1, 8):
        if rows % t == 0 and t * cols <= ELEMWISE_BLOCK:
            best = t
    return best if best is not None else rows


def _add_sibling(g, r1, cidx, *, name):
    S, R, C = g.shape
    hr = R // 2
    tr = _row_tile(hr, C)
    nt = hr // tr

    def body(c_ref, g_ref, r_ref, o_ref):
        o_ref[...] = (g_ref[...] + r_ref[...]).astype(BF16)

    blk = pl.BlockSpec((1, tr, C), lambda s, t, c: (s, t, 0))
    return pl.pallas_call(
        body, name=name,
        grid_spec=pltpu.PrefetchScalarGridSpec(
            num_scalar_prefetch=1, grid=(S, nt),
            in_specs=[pl.BlockSpec((1, tr, C), lambda s, t, c: (s, c[0] * nt + t, 0)), blk], out_specs=blk),
        out_shape=jax.ShapeDtypeStruct((S, hr, C), BF16),
        compiler_params=_params(2),
    )(cidx, g, r1)


def _add_chips(g, r1, r2, cidx, chip, *, name):
    _, R, C = g.shape
    hr = R // 2
    tr = _row_tile(hr, C)
    nt = hr // tr

    def body(pos_ref, g_ref, r1_ref, r2_ref, o_ref):
        own = g_ref[0] + r1_ref[0]
        o_ref[...] = ((own + r2_ref[0].astype(F32)) + r2_ref[1].astype(F32)) + r2_ref[2].astype(F32)

    pos = jnp.concatenate([cidx, chip])
    return pl.pallas_call(
        body, name=name,
        grid_spec=pltpu.PrefetchScalarGridSpec(
            num_scalar_prefetch=1, grid=(nt,),
            in_specs=[pl.BlockSpec((1, tr, C), lambda t, pos: (pos[1], pos[0] * nt + t, 0)),
                      pl.BlockSpec((1, tr, C), lambda t, pos: (pos[1], t, 0)),
                      pl.BlockSpec((N_CHIPS - 1, tr, C), lambda t, pos: (0, t, 0))],
            out_specs=pl.BlockSpec((tr, C), lambda t, pos: (t, 0))),
        out_shape=jax.ShapeDtypeStruct((hr, C), F32),
        compiler_params=_params(1),
    )(pos, g, r1, r2)


def _adamw_math(w, g, m, v):
    mn = ADAM_B1 * m + (1.0 - ADAM_B1) * g
    vn = ADAM_B2 * v + (1.0 - ADAM_B2) * (g * g)
    m_hat = mn / (1.0 - ADAM_B1 ** ADAM_STEP)
    v_hat = vn / (1.0 - ADAM_B2 ** ADAM_STEP)
    return -ADAM_LR * (m_hat / (jnp.sqrt(v_hat) + ADAM_EPS) + ADAM_WD * w), mn, vn


def _adamw_halves(w, mine, other, m, v, cidx, *, name):
    R, C = w.shape
    hr = R // 2
    tr = _row_tile(hr, C)
    nt = hr // tr

    def body(c_ref, w_ref, a_ref, b_ref, m_ref, v_ref, g_ref, d_ref, mo_ref, vo_ref):
        gv = jnp.where(pl.program_id(0) == c_ref[0], a_ref[...], b_ref[...])
        g_ref[...] = gv
        d_ref[...], mo_ref[...], vo_ref[...] = _adamw_math(w_ref[...], gv, m_ref[...], v_ref[...])

    full = pl.BlockSpec((tr, C), lambda h, t, c: (h * nt + t, 0))
    own = pl.BlockSpec((tr, C), lambda h, t, c: (jnp.where(h == c[0], t, 0), 0))
    sib = pl.BlockSpec((tr, C), lambda h, t, c: (jnp.where(h == c[0], 0, t), 0))
    shape = jax.ShapeDtypeStruct((R, C), F32)
    return pl.pallas_call(
        body, name=name,
        grid_spec=pltpu.PrefetchScalarGridSpec(
            num_scalar_prefetch=1, grid=(2, nt), in_specs=[full, own, sib, full, full], out_specs=[full] * 4),
        out_shape=[shape] * 4,
        compiler_params=_params(2),
    )(cidx, w, mine, other, m, v)


def _adamw_small(ws, gs, ms, vs):
    n = len(ws)

    def body(*refs):
        for i in range(n):
            w_ref, g_ref, m_ref, v_ref = (refs[j * n + i] for j in range(4))
            d_ref, mo_ref, vo_ref = (refs[(4 + j) * n + i] for j in range(3))
            d_ref[...], mo_ref[...], vo_ref[...] = _adamw_math(w_ref[...], g_ref[...], m_ref[...], v_ref[...])

    shapes = [jax.ShapeDtypeStruct(a.shape, F32) for a in ws]
    outs = pl.pallas_call(body, name="adamw_small", out_shape=shapes * 3, compiler_params=_params(0))(*ws, *gs, *ms, *vs)
    return outs[:n], outs[n:2 * n], outs[2 * n:]


def _unstack_cols(w):
    s, r, c = w.shape
    return w.transpose(1, 0, 2).reshape(r, s * c)


def _pad_rows(a, rows):
    return jnp.pad(a, ((0, rows - a.shape[0]), (0, LANES - a.shape[1])))


BIG = ("ffn1_w_gate", "ffn1_w_up", "ffn1_w_down", "w_in", "w_branch_a", "w_branch_b", "w_out",
       "ffn2_w_gate", "ffn2_w_up", "ffn2_w_down")
TRANSPOSED = ("ffn1_w_gate", "ffn1_w_up", "w_in", "ffn2_w_gate", "ffn2_w_up")
WEIGHTS = ("ffn1_norm", "ffn1_w_gate", "ffn1_w_up", "ffn1_w_down", "mix_norm", "w_in", "na_rpb", "sink_logit",
           "w_branch_a", "w_branch_b", "w_out", "ffn2_norm", "ffn2_w_gate", "ffn2_w_up", "ffn2_w_down", "final_norm")


def kernel(x, ffn1_norm, ffn1_w_gate, ffn1_w_up, ffn1_w_down, mix_norm, w_in, na_rpb, sink_logit, w_branch_a, w_branch_b, w_out, ffn2_norm, ffn2_w_gate, ffn2_w_up, ffn2_w_down, final_norm, loss_target, m_ffn1_norm, m_ffn1_w_gate, m_ffn1_w_up, m_ffn1_w_down, m_mix_norm, m_w_in, m_na_rpb, m_sink_logit, m_w_branch_a, m_w_branch_b, m_w_out, m_ffn2_norm, m_ffn2_w_gate, m_ffn2_w_up, m_ffn2_w_down, m_final_norm, v_ffn1_norm, v_ffn1_w_gate, v_ffn1_w_up, v_ffn1_w_down, v_mix_norm, v_w_in, v_na_rpb, v_sink_logit, v_w_branch_a, v_w_branch_b, v_w_out, v_ffn2_norm, v_ffn2_w_gate, v_ffn2_w_up, v_ffn2_w_down, v_final_norm):
    args = dict(locals())
    w = {k: args[k] for k in WEIGHTS}
    mom = {k: args["m_" + k] for k in WEIGHTS}
    var = {k: args["v_" + k] for k in WEIGHTS}
    cidx = lax.axis_index("c").astype(jnp.int32).reshape(1)
    chip = (2 * lax.axis_index("x") + lax.axis_index("y")).astype(jnp.int32).reshape(1)

    def shard(a, k):
        return jnp.swapaxes(a[0], 0, 1) if k in TRANSPOSED else a[0]

    def unshard(a, k):
        return (jnp.swapaxes(a, 0, 1) if k in TRANSPOSED else a)[None]

    def bf16_shards(names):
        return [shard(w[k], k).astype(BF16) for k in names]

    class comm:
        late_rider = _gather_rider(bf16_shards(MIXER + FFN2))

        @staticmethod
        def late(gathered):
            full = dict(zip(MIXER + FFN2, gathered))
            return (full["w_in"].reshape(D_IN, D_MODEL), _unstack_cols(full["w_branch_a"]), _unstack_cols(full["w_branch_b"]),
                    full["w_out"].reshape(D_MODEL, D_MODEL), tuple(full[k] for k in FFN2))

        @staticmethod
        def reduce(names, grads, *, tag):
            return _Reduce(names, grads, cidx, chip, tag=tag)

    f1 = _run_rider(_gather_rider(bf16_shards(FFN1)), name="all_gather_ffn1")
    out = _layer_grads(x[0], loss_target[0], ffn1_norm, f1, mix_norm, None, na_rpb[0], sink_logit[0], ffn2_norm,
                       final_norm.reshape(1, D_MODEL), comm=comm)
    mine = {k: out[k][0] for k in BIG}
    other = {k: out[k][1] for k in BIG}
    grad = {}

    rows = D_MODEL // LANES
    small = jnp.concatenate([
        out["ffn1_norm"].reshape(rows, LANES), out["mix_norm"].reshape(rows, LANES), out["ffn2_norm"].reshape(rows, LANES),
        out["final_norm"].reshape(rows, LANES), out["na_rpb"].reshape(-1, LANES),
        _pad_rows(out["sink_logit"].reshape(1, NB_HEADS), 8), _pad_rows(out["loss"], 8)], axis=0)
    total = _small_allreduce(small)
    n_rpb = NA_HEADS * 2 * NA_KH
    grad["ffn1_norm"] = total[0:rows].reshape(1, D_MODEL)
    grad["mix_norm"] = total[rows:2 * rows].reshape(1, D_MODEL)
    grad["ffn2_norm"] = total[2 * rows:3 * rows].reshape(1, D_MODEL)
    grad["final_norm"] = total[3 * rows:4 * rows].reshape(1, D_MODEL)
    grad["na_rpb"] = total[4 * rows:4 * rows + n_rpb].reshape(NA_HEADS, 2 * NA_KH, LANES)[:, :2 * NA_KH - 1, :2 * NA_KW - 1]
    grad["na_rpb"] = grad["na_rpb"].reshape(NA_HEADS, -1)
    grad["sink_logit"] = total[4 * rows + n_rpb:4 * rows + n_rpb + 1, 0:NB_HEADS]
    loss = total[4 * rows + n_rpb + 8, 0]

    deltas, new_m, new_v, grads_out = {}, {}, {}, {}
    for k in BIG:
        res = _adamw_halves(shard(w[k], k), mine[k], other[k], shard(mom[k], k), shard(var[k], k), cidx, name="adamw_" + k)
        grads_out[k], deltas[k], new_m[k], new_v[k] = (unshard(a, k) for a in res)
    small_names = [k for k in WEIGHTS if k not in BIG]
    res = _adamw_small(*[[a[k].reshape(grad[k].shape) for k in small_names] for a in (w, grad, mom, var)])
    for i, k in enumerate(small_names):
        grads_out[k], deltas[k], new_m[k], new_v[k] = (a.reshape(w[k].shape) for a in (grad[k], res[0][i], res[1][i], res[2][i]))
    return (loss, out["dx"].reshape(x.shape), *[grads_out[k] for k in WEIGHTS], *[deltas[k] for k in WEIGHTS],
            *[new_m[k] for k in WEIGHTS], *[new_v[k] for k in WEIGHTS])
```

```python
import math

import jax
import jax.numpy as jnp
import numpy as np
from jax import lax
from jax.experimental import pallas as pl
from jax.experimental.pallas import tpu as pltpu

F32 = jnp.float32
BF16 = jnp.bfloat16

D_MODEL = 1024
HEAD_DIM = 64
NA_HEADS = 8
NB_HEADS = 8
GRID_W = 64
NA_KH = 8
NA_KW = 16
WIN = 128
ROPE_THETA = 10000.0
EPS = 1e-6
N_CHIPS = 4
QK_SCALE = HEAD_DIM ** -0.5
NEG = -1e30
LANES = 128
VMEM_LIMIT = 56 * 1024 * 1024
HEAD_ROWS = 256
WGRAD_TOKENS_BYTES = 8192

C_QKVA = 3 * NA_HEADS * HEAD_DIM
C_QB = NB_HEADS * HEAD_DIM
C_KB = 2 * HEAD_DIM
C_ROPE = C_QB + C_KB
C_GATES = 2 * D_MODEL
D_IN = C_QKVA + C_QB + 2 * C_KB + C_GATES
O_QB = C_QKVA
O_KB = O_QB + C_QB
O_VB = O_KB + C_KB
O_G = O_VB + C_KB

ADAM_LR = 0.001
ADAM_B1 = 0.9
ADAM_B2 = 0.999
ADAM_EPS = 1e-08
ADAM_WD = 0.01
ADAM_STEP = 10

MESH = pl.DeviceIdType.MESH


def _dot(a, b):
    return jnp.dot(a, b, preferred_element_type=F32)


def _dot_nt(a, b):
    return lax.dot_general(a, b, (((1,), (1,)), ((), ())), preferred_element_type=F32)


def _dot_tn(a, b):
    return lax.dot_general(a, b, (((0,), (0,)), ((), ())), preferred_element_type=F32)


def _params(n_axes):
    return pltpu.CompilerParams(dimension_semantics=("arbitrary",) * n_axes, vmem_limit_bytes=VMEM_LIMIT)


def _rstd(xf):
    return lax.rsqrt(jnp.mean(xf * xf, axis=-1, keepdims=True) + EPS)


def _norm_bwd(dn, xf, g, r):
    xhat = xf * r
    dxh = dn * g
    dx = r * (dxh - xhat * jnp.mean(dxh * xhat, axis=-1, keepdims=True))
    return dx, dn * xhat


def _sigmoid(x):
    return 0.5 * jnp.tanh(0.5 * x) + 0.5


def _loss_head(hf, gv, tgt):
    r = _rstd(hf)
    err = (hf * r) * gv - tgt
    dx, dgr = _norm_bwd(err * (1.0 / hf.shape[-1]), hf, gv, r)
    return 0.5 * jnp.mean(err * err, axis=-1, keepdims=True), dx, dgr


def _ffn_fwd(x, g, wg, wu, wd, *, name, tm=1024, sub=512, rider=None, head=None):
    T, D = x.shape
    F = wg.shape[1]
    tm = min(tm, T)
    sub = min(sub, tm)
    n_head = 0 if head is None else 2

    def body(*refs):
        x_ref, g_ref, wg_ref, wu_ref, wd_ref = refs[:5]
        h_ref, n_ref, hdn_ref, p_ref, q_ref = refs[5 + n_head:10 + n_head]
        i, s = pl.program_id(0), pl.program_id(1)
        _ffn_fwd_step(x_ref, g_ref, wg_ref, wu_ref, wd_ref, h_ref, n_ref, hdn_ref, p_ref, q_ref, s)
        if head is not None:
            gf_ref, t_ref = refs[5:7]
            loss_ref, dgf_ref = refs[10 + n_head:]

            @pl.when((i == 0) & (s == 0))
            def _():
                loss_ref[...] = jnp.zeros_like(loss_ref)
                dgf_ref[...] = jnp.zeros_like(dgf_ref)

            @pl.when(s == N_CHIPS - 1)
            def _():
                for u in range(tm // HEAD_ROWS):
                    r = pl.ds(u * HEAD_ROWS, HEAD_ROWS)
                    terms, dh, dgr = _loss_head(h_ref[r, :], gf_ref[...], t_ref[r, :])
                    loss_ref[...] += jnp.broadcast_to(jnp.sum(terms), loss_ref.shape)
                    dgf_ref[...] += jnp.sum(dgr, axis=0, keepdims=True)
                    h_ref[r, :] = dh

    def _ffn_fwd_step(x_ref, g_ref, wg_ref, wu_ref, wd_ref, h_ref, n_ref, hdn_ref, p_ref, q_ref, s):

        @pl.when(s == 0)
        def _():
            xf = x_ref[...]
            n_ref[...] = ((xf * _rstd(xf)) * g_ref[...]).astype(BF16)
            h_ref[...] = xf

        rows = [pl.ds(u * sub, sub) for u in range(tm // sub)]
        ab = [(_dot_nt(n_ref[r, :], wg_ref[0]), _dot_nt(n_ref[r, :], wu_ref[0])) for r in rows]
        hdns = []
        for r, (a, b) in zip(rows, ab):
            sg = _sigmoid(a)
            silu = a * sg
            hdn = (silu * b).astype(BF16)
            hdn_ref[0, r, :] = hdn
            p_ref[0, r, :] = (b * (sg + silu * (1.0 - sg))).astype(BF16)
            q_ref[0, r, :] = silu.astype(BF16)
            hdns.append(hdn)
        for r, hdn in zip(rows, hdns):
            h_ref[r, :] += 0.5 * _dot(hdn, wd_ref[0])

    tok = pl.BlockSpec((tm, D), lambda i, s: (i, 0))
    hid = pl.BlockSpec((1, tm, F), lambda i, s: (s, i, 0))
    wspec = pl.BlockSpec((1, F, D), lambda i, s: (s, 0, 0))
    hshape = jax.ShapeDtypeStruct((N_CHIPS, T, F), BF16)
    grid = (T // tm, N_CHIPS)
    vec = pl.BlockSpec((1, D), lambda i, s: (0, 0))
    head_in, head_in_specs, head_out, head_out_specs = [], [], [], []
    if head is not None:
        head_in, head_in_specs = list(head), [vec, tok]
        head_out = [jax.ShapeDtypeStruct((1, LANES), F32), jax.ShapeDtypeStruct((1, D), F32)]
        head_out_specs = [pl.BlockSpec((1, LANES), lambda i, s: (0, 0)), vec]
    n_main = 5 + n_head
    body, r_in, r_in_specs, r_out, r_out_specs, scratch = _ride(body, n_main, n_main, rider, grid, (grid[0] * grid[1] * 7) // 8)
    outs = pl.pallas_call(
        body, name=name, grid=grid,
        in_specs=[tok, vec, wspec, wspec, wspec] + head_in_specs + r_in_specs,
        out_specs=[tok, tok, hid, hid, hid] + head_out_specs + r_out_specs,
        out_shape=[jax.ShapeDtypeStruct((T, D), F32), jax.ShapeDtypeStruct((T, D), BF16), hshape, hshape, hshape]
        + head_out + r_out,
        scratch_shapes=scratch,
        compiler_params=_params(2),
    )(x, g, wg, wu, wd, *head_in, *r_in)
    return (*outs[:n_main], list(outs[n_main:]))


def _ffn_bwd(dh, x, g, p, q, wg, wu, wd, *, name, tm=1024, sub=256, rider=None):
    T, D = x.shape
    F = wg.shape[1]
    tm = min(tm, T)
    sub = min(sub, tm)

    def body(dh_ref, x_ref, g_ref, p_ref, q_ref, wg_ref, wu_ref, wd_ref, dx_ref, da_ref, db_ref, dg_ref):
        i, s = pl.program_id(0), pl.program_id(1)

        @pl.when((i == 0) & (s == 0))
        def _():
            dg_ref[...] = jnp.zeros_like(dg_ref)

        @pl.when(s == 0)
        def _():
            dx_ref[...] = jnp.zeros_like(dx_ref)

        rows = [pl.ds(u * sub, sub) for u in range(tm // sub)]
        dhdn = [_dot_nt((0.5 * dh_ref[r, :]).astype(BF16), wd_ref[0]) for r in rows]
        das, dbs = [], []
        for r, dd in zip(rows, dhdn):
            da = (dd * p_ref[0, r, :].astype(F32)).astype(BF16)
            db = (dd * q_ref[0, r, :].astype(F32)).astype(BF16)
            da_ref[0, r, :] = da
            db_ref[0, r, :] = db
            das.append(da)
            dbs.append(db)
        for r, da, db in zip(rows, das, dbs):
            dx_ref[r, :] += _dot(da, wg_ref[0]) + _dot(db, wu_ref[0])

        @pl.when(s == N_CHIPS - 1)
        def _():
            xf = x_ref[...]
            dx, dgr = _norm_bwd(dx_ref[...], xf, g_ref[...], _rstd(xf))
            dg_ref[...] += jnp.sum(dgr, axis=0, keepdims=True)
            dx_ref[...] = dh_ref[...] + dx

    tok = pl.BlockSpec((tm, D), lambda i, s: (i, 0))
    hid = pl.BlockSpec((1, tm, F), lambda i, s: (s, i, 0))
    vec = pl.BlockSpec((1, D), lambda i, s: (0, 0))
    hshape = jax.ShapeDtypeStruct((N_CHIPS, T, F), BF16)
    wspec = pl.BlockSpec((1, F, D), lambda i, s: (s, 0, 0))
    grid = (T // tm, N_CHIPS)
    body, r_in, r_in_specs, r_out, r_out_specs, scratch = _ride(body, 8, 4, rider, grid, None)
    outs = pl.pallas_call(
        body, name=name, grid=grid,
        in_specs=[tok, tok, vec, hid, hid, wspec, wspec, wspec] + r_in_specs,
        out_specs=[tok, hid, hid, vec] + r_out_specs,
        out_shape=[jax.ShapeDtypeStruct((T, D), F32), hshape, hshape, jax.ShapeDtypeStruct((1, D), F32)] + r_out,
        scratch_shapes=scratch,
        compiler_params=_params(2),
    )(dh, x, g, p, q, wg, wu, wd, *r_in)
    return (*outs[:4], list(outs[4:]))


def _wgrad(a, b, *, a_block, a_map, b_block, b_map, out_shape, o_block, o_map, grid, scale=1.0, name, rider=None):
    def body(a_ref, b_ref, o_ref):
        @pl.when(pl.program_id(len(grid) - 1) == 0)
        def _():
            o_ref[...] = jnp.zeros_like(o_ref)

        av = a_ref[...]
        bv = b_ref[...]
        av = av.reshape(av.shape[-2:]).astype(BF16)
        bv = bv.reshape(bv.shape[-2:])
        if scale != 1.0:
            bv = scale * bv
        o_ref[...] += _dot_tn(av, bv.astype(BF16)).reshape(o_ref.shape)

    body, r_in, r_in_specs, r_out, r_out_specs, scratch = _ride(body, 2, 1, rider, grid, None)
    outs = pl.pallas_call(
        body, name=name, grid=grid,
        in_specs=[pl.BlockSpec(a_block, a_map), pl.BlockSpec(b_block, b_map)] + r_in_specs,
        out_specs=[pl.BlockSpec(o_block, o_map)] + r_out_specs,
        out_shape=[jax.ShapeDtypeStruct(out_shape, F32)] + r_out,
        scratch_shapes=scratch,
        compiler_params=_params(len(grid)),
    )(a, b, *r_in)
    return outs[0], list(outs[1:])


def _wgrad_rows(a, b, n_blocks, *, name, tk=2048):
    T, N = b.shape
    M = a.shape[1] // n_blocks
    tk = min(tk, T)
    return _wgrad(a, b, a_block=(tk, M), a_map=lambda s, k: (k, s), b_block=(tk, N), b_map=lambda s, k: (k, 0),
                  out_shape=(n_blocks, M, N), o_block=(1, M, N), o_map=lambda s, k: (s, 0, 0), grid=(n_blocks, T // tk), name=name)


def _wgrad_shard_a(a, b, *, name, scale=1.0, rider=None):
    S, T, M = a.shape
    N = b.shape[1]
    tk = min(WGRAD_TOKENS_BYTES // b.dtype.itemsize, T)
    return _wgrad(a, b, a_block=(1, tk, M), a_map=lambda s, k: (s, k, 0), b_block=(tk, N), b_map=lambda s, k: (k, 0),
                  out_shape=(S, M, N), o_block=(1, M, N), o_map=lambda s, k: (s, 0, 0), grid=(S, T // tk), scale=scale,
                  name=name, rider=rider)


def _wgrad_cols(a, b, n_blocks, *, name, tk=2048):
    T, M = a.shape
    N = b.shape[1] // n_blocks
    tk = min(tk, T)

    def body(a_ref, b_ref, o_ref):
        @pl.when(pl.program_id(0) == 0)
        def _():
            o_ref[...] = jnp.zeros_like(o_ref)

        r = _dot_tn(a_ref[...].astype(BF16), b_ref[...].astype(BF16))
        for s in range(n_blocks):
            o_ref[s] += r[:, s * N:(s + 1) * N]

    return pl.pallas_call(
        body, name=name, grid=(T // tk,),
        in_specs=[pl.BlockSpec((tk, M), lambda k: (k, 0)), pl.BlockSpec((tk, n_blocks * N), lambda k: (k, 0))],
        out_specs=pl.BlockSpec((n_blocks, M, N), lambda k: (0, 0, 0)),
        out_shape=jax.ShapeDtypeStruct((n_blocks, M, N), F32),
        compiler_params=_params(1),
    )(a, b)


def _rope_tables(T):
    half = HEAD_DIM // 2
    inv = np.float32(ROPE_THETA) ** (-np.arange(half, dtype=np.float32) / np.float32(half))
    ang = np.arange(T, dtype=np.float32)[:, None] * inv[None, :]
    cos, sin, zero = np.cos(ang), np.sin(ang), np.zeros_like(ang)
    reps = LANES // HEAD_DIM
    return (jnp.asarray(np.tile(np.concatenate([cos, cos], axis=1), (1, reps))),
            jnp.asarray(np.tile(np.concatenate([-sin, zero], axis=1), (1, reps))),
            jnp.asarray(np.tile(np.concatenate([zero, sin], axis=1), (1, reps))))


def _rope(x, cos, sa, sb, sign):
    half = HEAD_DIM // 2
    return x * cos + sign * (pltpu.roll(x, LANES - half, 1) * sa + pltpu.roll(x, half, 1) * sb)


def _mix_in_fwd(h, g, w_in, tables, *, tm=512):
    T, D = h.shape

    def body(h_ref, g_ref, w_ref, cos_ref, sa_ref, sb_ref, u_ref, qkva_ref, qb_ref, kvb_ref, gates_ref):
        hf = h_ref[...]
        u = ((hf * _rstd(hf)) * g_ref[...]).astype(BF16)
        u_ref[...] = u
        qkva_ref[...] = _dot_nt(u, w_ref[0:C_QKVA, :]).astype(BF16)
        zr = _dot_nt(u, w_ref[O_QB:O_QB + C_ROPE, :])
        cos, sa, sb = cos_ref[...], sa_ref[...], sb_ref[...]
        for j in range(C_ROPE // LANES):
            rj = _rope(zr[:, j * LANES:(j + 1) * LANES], cos, sa, sb, 1.0).astype(BF16)
            if j < C_QB // LANES:
                qb_ref[:, j * LANES:(j + 1) * LANES] = rj
            else:
                kvb_ref[:, 0:C_KB] = rj
        kvb_ref[:, C_KB:2 * C_KB] = _dot_nt(u, w_ref[O_VB:O_VB + C_KB, :]).astype(BF16)
        gates_ref[...] = _dot_nt(u, w_ref[O_G:O_G + C_GATES, :])

    def tok(n):
        return pl.BlockSpec((tm, n), lambda i: (i, 0))

    return pl.pallas_call(
        body, name="mix_in_fwd", grid=(T // tm,),
        in_specs=[tok(D), pl.BlockSpec((1, D), lambda i: (0, 0)), pl.BlockSpec((D_IN, D), lambda i: (0, 0), pipeline_mode=pl.Buffered(1)),
                  tok(LANES), tok(LANES), tok(LANES)],
        out_specs=[tok(D), tok(C_QKVA), tok(C_QB), tok(2 * C_KB), tok(C_GATES)],
        out_shape=[jax.ShapeDtypeStruct((T, D), BF16), jax.ShapeDtypeStruct((T, C_QKVA), BF16),
                   jax.ShapeDtypeStruct((T, C_QB), BF16), jax.ShapeDtypeStruct((T, 2 * C_KB), BF16),
                   jax.ShapeDtypeStruct((T, C_GATES), F32)],
        compiler_params=_params(1),
    )(h, g, w_in, *tables)


def _mix_in_bwd(dqa, dka, dva, dqb, dkb, dvb, dgates, h, g, dres, w_in, tables, *, tm=512, rider=None):
    T, D = h.shape

    def body(dqa_ref, dka_ref, dva_ref, dqb_ref, dkb_ref, dvb_ref, dgt_ref, h_ref, g_ref, dres_ref, w_ref,
             cos_ref, sa_ref, sb_ref, dz_ref, dh_ref, dg_ref):
        @pl.when(pl.program_id(0) == 0)
        def _():
            dg_ref[...] = jnp.zeros_like(dg_ref)

        na = NA_HEADS * HEAD_DIM
        dz_ref[:, 0:na] = dqa_ref[...].astype(BF16)
        dz_ref[:, na:2 * na] = dka_ref[...].astype(BF16)
        dz_ref[:, 2 * na:3 * na] = dva_ref[...].astype(BF16)
        cos, sa, sb = cos_ref[...], sa_ref[...], sb_ref[...]
        for j in range(C_QB // LANES):
            dz_ref[:, O_QB + j * LANES:O_QB + (j + 1) * LANES] = _rope(
                dqb_ref[:, j * LANES:(j + 1) * LANES], cos, sa, sb, -1.0).astype(BF16)
        dz_ref[:, O_KB:O_KB + C_KB] = _rope(dkb_ref[...], cos, sa, sb, -1.0).astype(BF16)
        dz_ref[:, O_VB:O_VB + C_KB] = dvb_ref[...].astype(BF16)
        dz_ref[:, O_G:O_G + C_GATES] = dgt_ref[...].astype(BF16)
        du = _dot(dz_ref[...], w_ref[...])
        hf = h_ref[...]
        dx, dgr = _norm_bwd(du, hf, g_ref[...], _rstd(hf))
        dg_ref[...] += jnp.sum(dgr, axis=0, keepdims=True)
        dh_ref[...] = dres_ref[...] + dx

    def tok(n):
        return pl.BlockSpec((tm, n), lambda i: (i, 0))

    vec = pl.BlockSpec((1, D), lambda i: (0, 0))
    na = NA_HEADS * HEAD_DIM
    grid = (T // tm,)
    body, r_in, r_in_specs, r_out, r_out_specs, scratch = _ride(body, 14, 3, rider, grid, None)
    outs = pl.pallas_call(
        body, name="mix_in_bwd", grid=grid,
        in_specs=[tok(na), tok(na), tok(na), tok(C_QB), tok(C_KB), tok(C_KB), tok(C_GATES), tok(D), vec, tok(D),
                  pl.BlockSpec((D_IN, D), lambda i: (0, 0), pipeline_mode=pl.Buffered(1)), tok(LANES), tok(LANES), tok(LANES)]
        + r_in_specs,
        out_specs=[tok(D_IN), tok(D), vec] + r_out_specs,
        out_shape=[jax.ShapeDtypeStruct((T, D_IN), BF16), jax.ShapeDtypeStruct((T, D), F32),
                   jax.ShapeDtypeStruct((1, D), F32)] + r_out,
        scratch_shapes=scratch,
        compiler_params=_params(1),
    )(dqa, dka, dva, dqb, dkb, dvb, dgates, h, g, dres, w_in, *tables, *r_in)
    return (*outs[:3], list(outs[3:]))


def _na_bias_slabs(rpb):
    H = rpb.shape[0]
    ncell = GRID_W * GRID_W
    cell = np.arange(ncell)
    co = cell % GRID_W - cell // GRID_W + (NA_KW - 1)
    e_co = jnp.asarray((np.arange(LANES)[:, None] == co[None, :]).astype(np.float32))
    table = jnp.pad(rpb, ((0, 0), (0, 1), (0, LANES - rpb.shape[2]))).reshape(H * 2 * NA_KH, LANES)

    def body(t_ref, e_ref, o_ref):
        o_ref[...] = jnp.dot(t_ref[...], e_ref[...], preferred_element_type=F32, precision=lax.Precision.HIGHEST)

    toeplitz = pl.pallas_call(
        body, name="rpb_unfold", out_shape=jax.ShapeDtypeStruct((H * 2 * NA_KH, ncell), F32),
        compiler_params=_params(0),
    )(table, e_co).reshape(H, 2 * NA_KH, GRID_W, GRID_W)

    def assemble(tz_ref, o_ref):
        c = lax.broadcasted_iota(jnp.int32, (GRID_W, GRID_W), 0)
        k = lax.broadcasted_iota(jnp.int32, (GRID_W, GRID_W), 1)
        cs = jnp.clip(c - NA_KW // 2, 0, GRID_W - NA_KW)
        inwin = (k >= cs) & (k < cs + NA_KW)
        for ro0 in range(NA_KH):
            for hh in range(2):
                for i in range(NA_KH):
                    o_ref[0, ro0, hh * GRID_W:(hh + 1) * GRID_W, i * GRID_W:(i + 1) * GRID_W] = jnp.where(
                        inwin, tz_ref[hh, ro0 + i], NEG)

    return pl.pallas_call(
        assemble, name="na_bias_slabs", grid=(H // 2,),
        in_specs=[pl.BlockSpec((2, 2 * NA_KH, GRID_W, GRID_W), lambda p: (p, 0, 0, 0))],
        out_specs=pl.BlockSpec((1, NA_KH, 2 * GRID_W, NA_KH * GRID_W), lambda p: (p, 0, 0, 0)),
        out_shape=jax.ShapeDtypeStruct((H // 2, NA_KH, 2 * GRID_W, NA_KH * GRID_W), F32),
        compiler_params=_params(1),
    )(toeplitz)


def _half_masks(rows):
    lane = lax.broadcasted_iota(jnp.int32, (rows, LANES), 1)
    left = lane < HEAD_DIM
    return left, (left, jnp.logical_not(left))


def _stack_heads(x):
    left, halves = _half_masks(x.shape[0])
    xf = x.astype(F32)
    return jnp.concatenate([jnp.where(m, xf, 0.0).astype(BF16) for m in halves], axis=0)


def _unstack_heads(o):
    rows = o.shape[0] // 2
    left, _ = _half_masks(rows)
    return jnp.where(left, o[:rows], o[rows:])


def _na_row(j, t, rb, rows):
    r = j * rb + t
    rs = jnp.clip(r - NA_KH // 2, 0, rows - NA_KH)
    return pl.multiple_of(t * GRID_W, GRID_W), pl.multiple_of(rs * GRID_W, GRID_W), rs - r + (NA_KH - 1)


def _na_specs(T, rb):
    qrows = GRID_W * rb
    pairs = NA_HEADS // 2
    return ([pl.BlockSpec((qrows, LANES), lambda p, j: (j, p)),
             pl.BlockSpec((T, LANES), lambda p, j: (0, pairs + p)),
             pl.BlockSpec((T, LANES), lambda p, j: (0, 2 * pairs + p))],
            pl.BlockSpec((1, NA_KH, 2 * GRID_W, NA_KH * GRID_W), lambda p, j: (p, 0, 0, 0)))


def _softmax(s):
    p = jnp.exp(s - jnp.max(s, axis=-1, keepdims=True))
    return p / jnp.sum(p, axis=-1, keepdims=True)


def _na_fwd(qkva, bias, *, rb=16, group=16):
    T = qkva.shape[0]
    rows = T // GRID_W
    nkeys = NA_KH * GRID_W
    rb = min(rb, rows)
    group = min(group, rb)

    def body(q_ref, k_ref, v_ref, bias_ref, y_ref):
        j = pl.program_id(1)

        def rows_step(t, carry):
            at = [_na_row(j, t * group + u, rb, rows) for u in range(group)]
            s = [_dot_nt(_stack_heads(q_ref[pl.ds(q0, GRID_W), :]), k_ref[pl.ds(k0, nkeys), :]) for q0, k0, _ in at]
            p = [_softmax(su * QK_SCALE + bias_ref[0, ro0]) for su, (_, _, ro0) in zip(s, at)]
            o = [_dot(pu.astype(BF16), v_ref[pl.ds(k0, nkeys), :]) for pu, (_, k0, _) in zip(p, at)]
            for ou, (q0, _, _) in zip(o, at):
                y_ref[pl.ds(q0, GRID_W), :] = _unstack_heads(ou).astype(BF16)
            return carry

        lax.fori_loop(0, rb // group, rows_step, 0)

    qkv_specs, bias_spec = _na_specs(T, rb)
    return pl.pallas_call(
        body, name="na_fwd", grid=(NA_HEADS // 2, rows // rb),
        in_specs=qkv_specs + [bias_spec],
        out_specs=qkv_specs[0],
        out_shape=jax.ShapeDtypeStruct((T, NA_HEADS * HEAD_DIM), BF16),
        compiler_params=_params(2),
    )(qkva, qkva, qkva, bias)


def _na_bwd(qkva, dy, bias, *, rb=8, group=8, rider=None):
    T = qkva.shape[0]
    rows = T // GRID_W
    nkeys = NA_KH * GRID_W

    def body(q_ref, k_ref, v_ref, dy_ref, bias_ref, dq_ref, dk_ref, dv_ref, dbias_ref):
        j = pl.program_id(1)

        @pl.when(j == 0)
        def _():
            dk_ref[...] = jnp.zeros_like(dk_ref)
            dv_ref[...] = jnp.zeros_like(dv_ref)
            dbias_ref[...] = jnp.zeros_like(dbias_ref)

        def rows_step(t, carry):
            at = [_na_row(j, t * group + u, rb, rows) for u in range(group)]
            qs = [_stack_heads(q_ref[pl.ds(q0, GRID_W), :]) for q0, _, _ in at]
            dys = [_stack_heads(dy_ref[pl.ds(q0, GRID_W), :]) for q0, _, _ in at]
            s = [_dot_nt(qu, k_ref[pl.ds(k0, nkeys), :]) for qu, (_, k0, _) in zip(qs, at)]
            dp = [_dot_nt(du, v_ref[pl.ds(k0, nkeys), :]) for du, (_, k0, _) in zip(dys, at)]
            p = [_softmax(su * QK_SCALE + bias_ref[0, ro0]) for su, (_, _, ro0) in zip(s, at)]
            ds = [pu * (du - jnp.sum(pu * du, axis=-1, keepdims=True)) for pu, du in zip(p, dp)]
            for u, (q0, k0, ro0) in enumerate(at):
                dbias_ref[0, ro0] += ds[u]
                dsb = ds[u].astype(BF16)
                dq_ref[pl.ds(q0, GRID_W), :] = (_unstack_heads(_dot(dsb, k_ref[pl.ds(k0, nkeys), :])) * QK_SCALE).astype(BF16)
                dk_ref[pl.ds(k0, nkeys), :] += _dot_tn(dsb, qs[u]) * QK_SCALE
                dv_ref[pl.ds(k0, nkeys), :] += _dot_tn(p[u].astype(BF16), dys[u])
            return carry

        lax.fori_loop(0, rb // group, rows_step, 0)

    qkv_specs, bias_spec = _na_specs(T, rb)
    width = NA_HEADS * HEAD_DIM
    kv_out = pl.BlockSpec((T, LANES), lambda p, j: (0, p))
    grid = (NA_HEADS // 2, rows // rb)
    body, r_in, r_in_specs, r_out, r_out_specs, scratch = _ride(body, 5, 4, rider, grid, None)
    outs = pl.pallas_call(
        body, name="na_bwd", grid=grid,
        in_specs=qkv_specs + [qkv_specs[0], bias_spec] + r_in_specs,
        out_specs=[qkv_specs[0], kv_out, kv_out, bias_spec] + r_out_specs,
        out_shape=[jax.ShapeDtypeStruct((T, width), BF16), jax.ShapeDtypeStruct((T, width), F32),
                   jax.ShapeDtypeStruct((T, width), F32), jax.ShapeDtypeStruct(bias.shape, F32)] + r_out,
        scratch_shapes=scratch,
        compiler_params=_params(2),
    )(qkva, qkva, qkva, dy, bias, *r_in)
    return (*outs[:4], list(outs[4:]))


def _rpb_fold(dslab):
    pairs = dslab.shape[0]
    H = 2 * pairs
    ncell = GRID_W * GRID_W

    def disassemble(d_ref, tz_ref):
        tz_ref[...] = jnp.zeros_like(tz_ref)
        for ro0 in range(NA_KH):
            for hh in range(2):
                for i in range(NA_KH):
                    tz_ref[hh, ro0 + i] += d_ref[0, ro0, hh * GRID_W:(hh + 1) * GRID_W, i * GRID_W:(i + 1) * GRID_W]

    dtoeplitz = pl.pallas_call(
        disassemble, name="rpb_fold_tiles", grid=(pairs,),
        in_specs=[pl.BlockSpec((1, NA_KH, 2 * GRID_W, NA_KH * GRID_W), lambda p: (p, 0, 0, 0))],
        out_specs=pl.BlockSpec((2, 2 * NA_KH, GRID_W, GRID_W), lambda p: (p, 0, 0, 0)),
        out_shape=jax.ShapeDtypeStruct((H, 2 * NA_KH, GRID_W, GRID_W), F32),
        compiler_params=_params(1),
    )(dslab).reshape(H * 2 * NA_KH, ncell)
    cell = np.arange(ncell)
    co = cell % GRID_W - cell // GRID_W + (NA_KW - 1)
    e_co = jnp.asarray((co[:, None] == np.arange(LANES)[None, :]).astype(np.float32))

    def diagonals(x_ref, e_ref, o_ref):
        o_ref[...] = jnp.dot(x_ref[...], e_ref[...], preferred_element_type=F32, precision=lax.Precision.HIGHEST)

    return pl.pallas_call(
        diagonals, name="rpb_fold", out_shape=jax.ShapeDtypeStruct((H * 2 * NA_KH, LANES), F32),
        compiler_params=_params(0),
    )(dtoeplitz, e_co).reshape(H, 2 * NA_KH, LANES)


SWA_KEYS = 3 * WIN


def _swa_block(j, t, qbn, T):
    blk = j * qbn + t
    start = jnp.clip((blk - 1) * WIN, 0, T - SWA_KEYS)
    row = lax.broadcasted_iota(jnp.int32, (2 * WIN, SWA_KEYS), 0)
    qpos = blk * WIN + jnp.where(row < WIN, row, row - WIN)
    kpos = start + lax.broadcasted_iota(jnp.int32, (2 * WIN, SWA_KEYS), 1)
    return pl.multiple_of(t * WIN, WIN), pl.multiple_of(start, WIN), jnp.abs(qpos - kpos) <= WIN


def _swa_sinks(sink_ref, p):
    row = lax.broadcasted_iota(jnp.int32, (2 * WIN, 1), 0)
    return jnp.where(row < WIN, sink_ref[p], sink_ref[p + NB_HEADS // 2])


def _swa_probs(s, mask, sink):
    s = jnp.where(mask, s * QK_SCALE, NEG)
    m = jnp.maximum(jnp.max(s, axis=-1, keepdims=True), sink)
    e = jnp.exp(s - m)
    esink = jnp.exp(sink - m)
    den = jnp.sum(e, axis=-1, keepdims=True) + esink
    return e / den, esink / den


def _swa_specs(T, qbn):
    return [pl.BlockSpec(memory_space=pltpu.SMEM),
            pl.BlockSpec((WIN * qbn, LANES), lambda p, j: (j, p)),
            pl.BlockSpec((T, LANES), lambda p, j: (0, 0)),
            pl.BlockSpec((T, LANES), lambda p, j: (0, 1))]


def _swa_fwd(qb, kvb, sink, *, qbn=16, group=16):
    T = qb.shape[0]
    pairs = NB_HEADS // 2
    qbn = min(qbn, T // WIN)
    group = min(group, qbn)

    def body(sink_ref, q_ref, k_ref, v_ref, y_ref):
        p, j = pl.program_id(0), pl.program_id(1)
        sinks = _swa_sinks(sink_ref, p)

        def blocks_step(t, carry):
            at = [_swa_block(j, t * group + u, qbn, T) for u in range(group)]
            s = [_dot_nt(_stack_heads(q_ref[pl.ds(q0, WIN), :]), k_ref[pl.ds(k0, SWA_KEYS), :]) for q0, k0, _ in at]
            pr = [_swa_probs(su, mask, sinks)[0] for su, (_, _, mask) in zip(s, at)]
            o = [_dot(pu.astype(BF16), v_ref[pl.ds(k0, SWA_KEYS), :]) for pu, (_, k0, _) in zip(pr, at)]
            for ou, (q0, _, _) in zip(o, at):
                y_ref[pl.ds(q0, WIN), :] = _unstack_heads(ou).astype(BF16)
            return carry

        lax.fori_loop(0, qbn // group, blocks_step, 0)

    specs = _swa_specs(T, qbn)
    return pl.pallas_call(
        body, name="swa_fwd", grid=(pairs, T // (WIN * qbn)),
        in_specs=specs, out_specs=specs[1],
        out_shape=jax.ShapeDtypeStruct((T, NB_HEADS * HEAD_DIM), BF16),
        compiler_params=_params(2),
    )(sink, qb, kvb, kvb)


def _swa_bwd(qb, kvb, dy, sink, *, qbn=8, group=8, rider=None):
    T = qb.shape[0]
    pairs = NB_HEADS // 2
    qbn = min(qbn, T // WIN)
    group = min(group, qbn)

    def body(sink_ref, q_ref, k_ref, v_ref, dy_ref, dq_ref, dk_ref, dv_ref, dsink_ref):
        p, j = pl.program_id(0), pl.program_id(1)
        sinks = _swa_sinks(sink_ref, p)

        @pl.when((p == 0) & (j == 0))
        def _():
            dk_ref[...] = jnp.zeros_like(dk_ref)
            dv_ref[...] = jnp.zeros_like(dv_ref)

        @pl.when(j == 0)
        def _():
            dsink_ref[...] = jnp.zeros_like(dsink_ref)

        def blocks_step(t, carry):
            at = [_swa_block(j, t * group + u, qbn, T) for u in range(group)]
            qs = [_stack_heads(q_ref[pl.ds(q0, WIN), :]) for q0, _, _ in at]
            dys = [_stack_heads(dy_ref[pl.ds(q0, WIN), :]) for q0, _, _ in at]
            s = [_dot_nt(qu, k_ref[pl.ds(k0, SWA_KEYS), :]) for qu, (_, k0, _) in zip(qs, at)]
            dp = [_dot_nt(du, v_ref[pl.ds(k0, SWA_KEYS), :]) for du, (_, k0, _) in zip(dys, at)]
            probs = [_swa_probs(su, mask, sinks) for su, (_, _, mask) in zip(s, at)]
            for u, (q0, k0, _) in enumerate(at):
                pr, psink = probs[u]
                delta = jnp.sum(pr * dp[u], axis=-1, keepdims=True)
                dsb = (pr * (dp[u] - delta)).astype(BF16)
                dsk = psink * delta
                for hh in range(2):
                    dsink_ref[0, hh:hh + 1, :] += jnp.broadcast_to(-jnp.sum(dsk[hh * WIN:(hh + 1) * WIN]), (1, LANES))
                dq_ref[pl.ds(q0, WIN), :] = _unstack_heads(_dot(dsb, k_ref[pl.ds(k0, SWA_KEYS), :])) * QK_SCALE
                dk_ref[pl.ds(k0, SWA_KEYS), :] += _dot_tn(dsb, qs[u]) * QK_SCALE
                dv_ref[pl.ds(k0, SWA_KEYS), :] += _dot_tn(pr.astype(BF16), dys[u])
            return carry

        lax.fori_loop(0, qbn // group, blocks_step, 0)

    specs = _swa_specs(T, qbn)
    kv_out = pl.BlockSpec((T, LANES), lambda p, j: (0, 0))
    grid = (pairs, T // (WIN * qbn))
    body, r_in, r_in_specs, r_out, r_out_specs, scratch = _ride(body, 5, 4, rider, grid, None)
    outs = pl.pallas_call(
        body, name="swa_bwd", grid=grid,
        in_specs=specs + [specs[1]] + r_in_specs,
        out_specs=[specs[1], kv_out, kv_out, pl.BlockSpec((1, 8, LANES), lambda p, j: (p, 0, 0))] + r_out_specs,
        out_shape=[jax.ShapeDtypeStruct((T, NB_HEADS * HEAD_DIM), F32), jax.ShapeDtypeStruct((T, LANES), F32),
                   jax.ShapeDtypeStruct((T, LANES), F32), jax.ShapeDtypeStruct((pairs, 8, LANES), F32)] + r_out,
        scratch_shapes=scratch,
        compiler_params=_params(2),
    )(sink, qb, kvb, kvb, dy, *r_in)
    return (*outs[:4], list(outs[4:]))


def _merge_fwd(ya, yb, gates, wa, wb, wout, h, *, tm=512):
    T, D = h.shape
    W = ya.shape[1]

    def body(ya_ref, yb_ref, gt_ref, wa_ref, wb_ref, wo_ref, h_ref, h2_ref, mg_ref):
        pa = _dot(ya_ref[...], wa_ref[...])
        pb = _dot(yb_ref[...], wb_ref[...])
        mg = (jax.nn.sigmoid(gt_ref[:, 0:D]) * pa + jax.nn.sigmoid(gt_ref[:, D:2 * D]) * pb).astype(BF16)
        mg_ref[...] = mg
        h2_ref[...] = h_ref[...] + _dot(mg, wo_ref[...])

    def tok(n):
        return pl.BlockSpec((tm, n), lambda i: (i, 0))

    def full(r, c):
        return pl.BlockSpec((r, c), lambda i: (0, 0))

    return pl.pallas_call(
        body, name="merge_fwd", grid=(T // tm,),
        in_specs=[tok(W), tok(W), tok(2 * D), full(W, D), full(W, D), full(D, D), tok(D)],
        out_specs=[tok(D), tok(D)],
        out_shape=[jax.ShapeDtypeStruct((T, D), F32), jax.ShapeDtypeStruct((T, D), BF16)],
        compiler_params=_params(1),
    )(ya, yb, gates, wa, wb, wout, h)


def _merge_bwd(dh, ya, yb, gates, wa, wb, wout, *, tm=512, rider=None):
    T, D = dh.shape
    W = ya.shape[1]

    def body(dh_ref, ya_ref, yb_ref, gt_ref, wa_ref, wb_ref, wo_ref, dya_ref, dyb_ref, dpa_ref, dpb_ref, dgt_ref):
        dmg = _dot_nt(dh_ref[...].astype(BF16), wo_ref[...])
        for y_ref, w_ref, dy_ref, dp_ref, lo in ((ya_ref, wa_ref, dya_ref, dpa_ref, 0), (yb_ref, wb_ref, dyb_ref, dpb_ref, D)):
            sg = jax.nn.sigmoid(gt_ref[:, lo:lo + D])
            dp = (dmg * sg).astype(BF16)
            dp_ref[...] = dp
            dgt_ref[:, lo:lo + D] = (dmg * _dot(y_ref[...], w_ref[...]) * (sg * (1.0 - sg))).astype(BF16)
            dy_ref[...] = _dot_nt(dp, w_ref[...]).astype(BF16)

    def tok(n):
        return pl.BlockSpec((tm, n), lambda i: (i, 0))

    def full(r, c):
        return pl.BlockSpec((r, c), lambda i: (0, 0))

    grid = (T // tm,)
    body, r_in, r_in_specs, r_out, r_out_specs, scratch = _ride(body, 7, 5, rider, grid, None)
    outs = pl.pallas_call(
        body, name="merge_bwd", grid=grid,
        in_specs=[tok(D), tok(W), tok(W), tok(2 * D), full(W, D), full(W, D), full(D, D)] + r_in_specs,
        out_specs=[tok(W), tok(W), tok(D), tok(D), tok(2 * D)] + r_out_specs,
        out_shape=[jax.ShapeDtypeStruct((T, W), BF16), jax.ShapeDtypeStruct((T, W), BF16),
                   jax.ShapeDtypeStruct((T, D), BF16), jax.ShapeDtypeStruct((T, D), BF16),
                   jax.ShapeDtypeStruct((T, 2 * D), BF16)] + r_out,
        scratch_shapes=scratch,
        compiler_params=_params(1),
    )(dh, ya, yb, gates, wa, wb, wout, *r_in)
    return (*outs[:5], list(outs[5:]))


def _pair_heads(a, axis):
    shp = a.shape
    a = a.reshape(shp[:axis] + (2, NB_HEADS // 2, HEAD_DIM) + shp[axis + 1:])
    return jnp.swapaxes(a, axis, axis + 1).reshape(shp)


def _unpair_heads(a, axis):
    shp = a.shape
    a = a.reshape(shp[:axis] + (NB_HEADS // 2, 2, HEAD_DIM) + shp[axis + 1:])
    return jnp.swapaxes(a, axis, axis + 1).reshape(shp)


FFN1 = ("ffn1_w_gate", "ffn1_w_up", "ffn1_w_down")
FFN2 = ("ffn2_w_gate", "ffn2_w_up", "ffn2_w_down")
MIXER = ("w_in", "w_branch_a", "w_branch_b", "w_out")
BRANCH = MIXER[1:]


def _layer_grads(x, target, g1, f1, gmix, late, rpb, sink, g2, gfin, comm=None):
    T = x.shape[0]
    tables = _rope_tables(T)
    bias = _na_bias_slabs(rpb)

    comm = comm or _Local(late)
    h1, n1, hdn1, p1, q1, gathered = _ffn_fwd(x, g1, *f1, name="ffn1_fwd", rider=comm.late_rider)
    w_in_t, wa, wb, wout, f2 = comm.late(gathered)
    w_in_p = jnp.concatenate([w_in_t[:O_QB], _pair_heads(w_in_t[O_QB:O_KB], 0), w_in_t[O_KB:]], axis=0)
    wb_p = _pair_heads(wb, 0)
    u, qkva, qb, kvb, gates = _mix_in_fwd(h1, gmix, w_in_p, tables)
    ya = _na_fwd(qkva, bias)
    yb = _swa_fwd(qb, kvb, sink)
    h2, merged = _merge_fwd(ya, yb, gates, wa, wb_p, wout, h1)
    dh3, n2, hdn2, p2, q2, loss, dgfin, _ = _ffn_fwd(h2, g2, *f2, name="ffn2_fwd", head=(gfin, target))

    dh2, da2, db2, dg2, _ = _ffn_bwd(dh3, h2, g2, p2, q2, *f2, name="ffn2_bwd")
    df2 = [_wgrad_shard_a(da2, n2, name="ffn2_dwg")[0], _wgrad_shard_a(db2, n2, name="ffn2_dwu")[0],
           _wgrad_shard_a(hdn2, dh3, scale=0.5, name="ffn2_dwd")[0]]
    red2 = comm.reduce(FFN2, df2, tag="ffn2")
    dya, dyb, dpa, dpb, dgates, got = _merge_bwd(dh2, ya, yb, gates, wa, wb_p, wout, rider=red2.sibling)
    red2.partial(got)
    dwout = _wgrad_cols(merged, dh2, 1, name="dwout").reshape(N_CHIPS, D_MODEL // N_CHIPS, D_MODEL)
    dwa = _wgrad_cols(ya, dpa, N_CHIPS, name="dwa")
    dwb = _unpair_heads(_wgrad_cols(yb, dpb, N_CHIPS, name="dwb"), 1)
    redb = comm.reduce(BRANCH, [dwa, dwb, dwout], tag="branch")
    dqa, dka, dva, dbias, got = _na_bwd(qkva, dya, bias, rider=_two_riders(red2.chips, redb.sibling))
    red2.halves(got[:len(FFN2)])
    redb.partial(got[len(FFN2):])
    drpb = _rpb_fold(dbias)
    dqb, dkb, dvb, dsink, got = _swa_bwd(qb, kvb, dyb, sink, rider=_two_riders(red2.share, redb.chips))
    out = red2.result(got[:len(FFN2)])
    redb.halves(got[len(FFN2):])
    dz, dh1, dgmix, got = _mix_in_bwd(dqa, dka, dva, dqb, dkb, dvb, dgates, h1, gmix, dh2, w_in_p, tables, rider=redb.share)
    out.update(redb.result(got))
    dwin_p = _wgrad_rows(dz, u, 2, name="dwin")[0].reshape(D_IN, D_MODEL)
    dwin = jnp.concatenate([dwin_p[:O_QB], _unpair_heads(dwin_p[O_QB:O_KB], 0), dwin_p[O_KB:]], axis=0)
    dx, da1, db1, dg1, _ = _ffn_bwd(dh1, x, g1, p1, q1, *f1, name="ffn1_bwd")
    redw = comm.reduce(("w_in",), [dwin.reshape(N_CHIPS, D_IN // N_CHIPS, D_MODEL)], tag="w_in").partial_now()
    dwg1, got = _wgrad_shard_a(da1, n1, name="ffn1_dwg", rider=redw.chips)
    dwu1, got = _wgrad_shard_a(db1, n1, name="ffn1_dwu", rider=redw.halves(got).share)
    out.update(redw.result(got))
    red1 = comm.reduce(FFN1[:2], [dwg1, dwu1], tag="ffn1_gate_up").partial_now()
    dwd1, got = _wgrad_shard_a(hdn1, dh1, scale=0.5, name="ffn1_dwd", rider=red1.chips)
    out.update(red1.halves(got).result_now())
    out.update(comm.reduce(FFN1[2:], [dwd1], tag="ffn1_down").partial_now().halves_now().result_now())
    out.update(loss=loss, dx=dx, ffn1_norm=dg1, mix_norm=dgmix, ffn2_norm=dg2, final_norm=dgfin, na_rpb=drpb,
               sink_logit=dsink[:, 0:2, 0].T.reshape(NB_HEADS))
    return out


class _Local:
    late_rider = None

    def __init__(self, late):
        self._late = late

    def late(self, gathered):
        return self._late

    def reduce(self, names, grads, *, tag):
        return _LocalReduce(names, grads)


class _LocalReduce:
    sibling = chips = share = None

    def __init__(self, names, grads):
        self._result = dict(zip(names, grads))

    def partial(self, got=None):
        return self

    halves = partial_now = halves_now = partial

    def result(self, got=None):
        return self._result

    result_now = result


ANY = pl.BlockSpec(memory_space=pl.ANY)


def _place():
    x, y, c = lax.axis_index("x"), lax.axis_index("y"), lax.axis_index("c")
    chips = [(1 - x, y), (x, 1 - y), (1 - x, 1 - y)]
    return x, y, c, 2 * x + y, chips


def _remote(src, dst, send_sems, recv_sems, k, device):
    return pltpu.make_async_remote_copy(src_ref=src, dst_ref=dst, send_sem=send_sems.at[k], recv_sem=recv_sems.at[k],
                                        device_id=device, device_id_type=MESH)


class _Rider:
    def __init__(self, inputs, out_shape, scratch, start, middle, finish):
        self.inputs, self.out_shape, self.scratch = list(inputs), list(out_shape), list(scratch)
        self.start, self.middle, self.finish = start, middle, finish


def _two_riders(first, second):
    if first is None and second is None:
        return None
    assert first.middle is None and second.middle is None
    n_in, n_out, n_sem = len(first.inputs), len(first.out_shape), len(first.scratch)

    def phase(name):
        def run(ins, outs, sems):
            getattr(first, name)(ins[:n_in], outs[:n_out], sems[:n_sem])
            getattr(second, name)(ins[n_in:], outs[n_out:], sems[n_sem:])
        return run

    return _Rider(first.inputs + second.inputs, first.out_shape + second.out_shape, first.scratch + second.scratch,
                  phase("start"), None, phase("finish"))


def _run_rider(rider, *, name):
    n_in, n_out = len(rider.inputs), len(rider.out_shape)

    def body(*refs):
        ins, outs, sems = refs[:n_in], refs[n_in:n_in + n_out], refs[n_in + n_out:]
        rider.start(ins, outs, sems)
        if rider.middle is not None:
            rider.middle(ins, outs, sems)
        rider.finish(ins, outs, sems)

    return pl.pallas_call(body, name=name, in_specs=[ANY] * n_in, out_specs=[ANY] * n_out, out_shape=rider.out_shape,
                          scratch_shapes=rider.scratch)(*rider.inputs)


def _ride(body, n_in, n_out, rider, grid, middle_step):
    if rider is None:
        return body, [], [], [], [], []
    r_in, r_out = len(rider.inputs), len(rider.out_shape)
    steps = math.prod(grid)

    def riding(*refs):
        ins, r_ins = refs[:n_in], refs[n_in:n_in + r_in]
        outs = refs[n_in + r_in:n_in + r_in + n_out]
        r_outs = refs[n_in + r_in + n_out:n_in + r_in + n_out + r_out]
        sems = refs[n_in + r_in + n_out + r_out:]
        step = pl.program_id(0)
        for axis in range(1, len(grid)):
            step = step * grid[axis] + pl.program_id(axis)

        @pl.when(step == 0)
        def _():
            rider.start(r_ins, r_outs, sems)

        body(*ins, *outs)

        if rider.middle is not None:
            @pl.when(step == middle_step)
            def _():
                rider.middle(r_ins, r_outs, sems)

        @pl.when(step == steps - 1)
        def _():
            rider.finish(r_ins, r_outs, sems)

    return riding, rider.inputs, [ANY] * r_in, rider.out_shape, [ANY] * r_out, rider.scratch


def _gather_rider(shards):
    n = len(shards)

    def plan(ins, outs, sems, kinds):
        send_sems, recv_sems, own_send_sems, own_recv_sems = sems
        x, y, c, mine, chips = _place()
        sibling = (x, y, 1 - c)
        made = {k: [] for k in kinds}
        for i in range(n):
            hr = shards[i].shape[0] // 2
            if "own" in made:
                made["own"].append(_remote(ins[i], outs[i].at[mine], own_send_sems, own_recv_sems, i, sibling))
            for j, (cx, cy) in enumerate(chips):
                here = outs[i].at[2 * cx + cy, pl.ds(c * hr, hr)]
                there = outs[i].at[2 * cx + cy, pl.ds((1 - c) * hr, hr)]
                if "sends" in made:
                    made["sends"].append(_remote(ins[i].at[pl.ds(c * hr, hr)], outs[i].at[mine, pl.ds(c * hr, hr)],
                                                 send_sems, recv_sems, 6 * i + j, (cx, cy, c)))
                if "landed" in made:
                    made["landed"].append(_remote(here, here, send_sems, recv_sems, 6 * i + j, (cx, cy, c)))
                if "passes" in made:
                    made["passes"].append(_remote(here, here, send_sems, recv_sems, 6 * i + 3 + j, sibling))
                if "others" in made:
                    made["others"].append(_remote(there, there, send_sems, recv_sems, 6 * i + 3 + j, sibling))
        return [made[k] for k in kinds]

    def start(ins, outs, sems):
        own, sends = plan(ins, outs, sems, ("own", "sends"))
        for cp in own + sends:
            cp.start()

    def middle(ins, outs, sems):
        landed, passes = plan(ins, outs, sems, ("landed", "passes"))
        for arrived, cp in zip(landed, passes):
            arrived.wait_recv()
            cp.start()

    def finish(ins, outs, sems):
        own, sends, passes, others = plan(ins, outs, sems, ("own", "sends", "passes", "others"))
        for arrived in others:
            arrived.wait_recv()
        for cp in sends + passes:
            cp.wait_send()
        for cp in own:
            cp.wait()

    return _Rider(shards, [jax.ShapeDtypeStruct((N_CHIPS,) + s.shape, s.dtype) for s in shards],
                  [pltpu.SemaphoreType.DMA((6 * n,)), pltpu.SemaphoreType.DMA((6 * n,)),
                   pltpu.SemaphoreType.DMA((n,)), pltpu.SemaphoreType.DMA((n,))], start, middle, finish)


def _swap_rider(arrays, out_shape, source):
    n = len(arrays)

    def plan(ins, outs, sems):
        send_sems, recv_sems = sems
        x, y, c, _, _ = _place()
        return [_remote(source(ins[i], c, i), outs[i], send_sems, recv_sems, i, (x, y, 1 - c)) for i in range(n)]

    def start(ins, outs, sems):
        for cp in plan(ins, outs, sems):
            cp.start()

    def finish(ins, outs, sems):
        for cp in plan(ins, outs, sems):
            cp.wait()

    return _Rider(arrays, out_shape, [pltpu.SemaphoreType.DMA((n,)), pltpu.SemaphoreType.DMA((n,))], start, None, finish)


def _sibling_rider(grads):
    half = [g.shape[1] // 2 for g in grads]
    return _swap_rider(grads, [jax.ShapeDtypeStruct((g.shape[0], hr, g.shape[2]), g.dtype) for g, hr in zip(grads, half)],
                       lambda ref, c, i: ref.at[:, pl.ds((1 - c) * half[i], half[i])])


def _share_rider(halves):
    return _swap_rider(halves, [jax.ShapeDtypeStruct(h.shape, h.dtype) for h in halves], lambda ref, c, i: ref)


def _chips_rider(parts):
    n = len(parts)

    def plan(ins, outs, sems):
        send_sems, recv_sems = sems
        _, _, c, _, chips = _place()
        return [_remote(ins[i].at[2 * cx + cy], outs[i].at[j], send_sems, recv_sems, 3 * i + j, (cx, cy, c))
                for i in range(n) for j, (cx, cy) in enumerate(chips)]

    def start(ins, outs, sems):
        for cp in plan(ins, outs, sems):
            cp.start()

    def finish(ins, outs, sems):
        for cp in plan(ins, outs, sems):
            cp.wait()

    return _Rider(parts, [jax.ShapeDtypeStruct((N_CHIPS - 1,) + p.shape[1:], p.dtype) for p in parts],
                  [pltpu.SemaphoreType.DMA((3 * n,)), pltpu.SemaphoreType.DMA((3 * n,))], start, None, finish)


class _Reduce:
    def __init__(self, names, grads, cidx, chip, *, tag):
        self.names, self.grads, self.cidx, self.chip, self.tag = names, grads, cidx, chip, tag
        self.sibling = _sibling_rider(grads)

    def partial(self, from_sibling):
        self.from_sibling = from_sibling
        self.chips = _chips_rider([_add_sibling(g, r, self.cidx, name="add_sibling_" + k)
                                   for k, g, r in zip(self.names, self.grads, from_sibling)])
        return self

    def halves(self, from_chips):
        self.mine = [_add_chips(g, r1, r2, self.cidx, self.chip, name="add_chips_" + k)
                     for k, g, r1, r2 in zip(self.names, self.grads, self.from_sibling, from_chips)]
        self.share = _share_rider(self.mine)
        return self

    def result(self, others):
        return dict(zip(self.names, zip(self.mine, others)))

    def partial_now(self):
        return self.partial(_run_rider(self.sibling, name="rs_sibling_" + self.tag))

    def halves_now(self):
        return self.halves(_run_rider(self.chips, name="rs_chips_" + self.tag))

    def result_now(self):
        return self.result(_run_rider(self.share, name="rs_share_" + self.tag))


N_DEV = 8


def _small_allreduce(vec):
    R = vec.shape[0]

    def body(v_ref, o_ref, buf, send_sems, recv_sems):
        x, y, c, _, _ = _place()
        me = 4 * x + 2 * y + c
        buf[me] = v_ref[...]
        copies = []
        for k in range(1, N_DEV):
            peer = (x ^ (k >> 2), y ^ ((k >> 1) & 1), c ^ (k & 1))
            cp = _remote(v_ref, buf.at[me], send_sems, recv_sems, k - 1, peer)
            cp.start()
            copies.append(cp)
        for k, cp in enumerate(copies, start=1):
            cp.wait_send()
            landed = buf.at[me ^ k]
            _remote(landed, landed, send_sems, recv_sems, k - 1, (x, y, c)).wait_recv()
        acc = buf[0]
        for d in range(1, N_DEV):
            acc = acc + buf[d]
        o_ref[...] = acc

    return pl.pallas_call(
        body, name="small_allreduce",
        in_specs=[pl.BlockSpec(memory_space=pltpu.VMEM)], out_specs=pl.BlockSpec(memory_space=pltpu.VMEM),
        out_shape=jax.ShapeDtypeStruct(vec.shape, vec.dtype),
        scratch_shapes=[pltpu.VMEM((N_DEV, R, LANES), F32), pltpu.SemaphoreType.DMA((N_DEV - 1,)),
                        pltpu.SemaphoreType.DMA((N_DEV - 1,))],
    )(vec)


ELEMWISE_BLOCK = 256 * 1024


def _row_tile(rows, cols):
    best = None
    for t in range(8, rows + 1, 8):
        if rows % t == 0 and t * cols <= ELEMWISE_BLOCK:
            best = t
    return best if best is not None else rows


def _add_sibling(g, r1, cidx, *, name):
    S, R, C = g.shape
    hr = R // 2
    tr = _row_tile(hr, C)
    nt = hr // tr

    def body(c_ref, g_ref, r_ref, o_ref):
        o_ref[...] = (g_ref[...] + r_ref[...]).astype(BF16)

    blk = pl.BlockSpec((1, tr, C), lambda s, t, c: (s, t, 0))
    return pl.pallas_call(
        body, name=name,
        grid_spec=pltpu.PrefetchScalarGridSpec(
            num_scalar_prefetch=1, grid=(S, nt),
            in_specs=[pl.BlockSpec((1, tr, C), lambda s, t, c: (s, c[0] * nt + t, 0)), blk], out_specs=blk),
        out_shape=jax.ShapeDtypeStruct((S, hr, C), BF16),
        compiler_params=_params(2),
    )(cidx, g, r1)


def _add_chips(g, r1, r2, cidx, chip, *, name):
    _, R, C = g.shape
    hr = R // 2
    tr = _row_tile(hr, C)
    nt = hr // tr

    def body(pos_ref, g_ref, r1_ref, r2_ref, o_ref):
        own = g_ref[0] + r1_ref[0]
        o_ref[...] = ((own + r2_ref[0].astype(F32)) + r2_ref[1].astype(F32)) + r2_ref[2].astype(F32)

    pos = jnp.concatenate([cidx, chip])
    return pl.pallas_call(
        body, name=name,
        grid_spec=pltpu.PrefetchScalarGridSpec(
            num_scalar_prefetch=1, grid=(nt,),
            in_specs=[pl.BlockSpec((1, tr, C), lambda t, pos: (pos[1], pos[0] * nt + t, 0)),
                      pl.BlockSpec((1, tr, C), lambda t, pos: (pos[1], t, 0)),
                      pl.BlockSpec((N_CHIPS - 1, tr, C), lambda t, pos: (0, t, 0))],
            out_specs=pl.BlockSpec((tr, C), lambda t, pos: (t, 0))),
        out_shape=jax.ShapeDtypeStruct((hr, C), F32),
        compiler_params=_params(1),
    )(pos, g, r1, r2)


def _adamw_math(w, g, m, v):
    mn = ADAM_B1 * m + (1.0 - ADAM_B1) * g
    vn = ADAM_B2 * v + (1.0 - ADAM_B2) * (g * g)
    m_hat = mn / (1.0 - ADAM_B1 ** ADAM_STEP)
    v_hat = vn / (1.0 - ADAM_B2 ** ADAM_STEP)
    return -ADAM_LR * (m_hat / (jnp.sqrt(v_hat) + ADAM_EPS) + ADAM_WD * w), mn, vn


def _adamw_halves(w, mine, other, m, v, cidx, *, name):
    R, C = w.shape
    hr = R // 2
    tr = _row_tile(hr, C)
    nt = hr // tr

    def body(c_ref, w_ref, a_ref, b_ref, m_ref, v_ref, g_ref, d_ref, mo_ref, vo_ref):
        gv = jnp.where(pl.program_id(0) == c_ref[0], a_ref[...], b_ref[...])
        g_ref[...] = gv
        d_ref[...], mo_ref[...], vo_ref[...] = _adamw_math(w_ref[...], gv, m_ref[...], v_ref[...])

    full = pl.BlockSpec((tr, C), lambda h, t, c: (h * nt + t, 0))
    own = pl.BlockSpec((tr, C), lambda h, t, c: (jnp.where(h == c[0], t, 0), 0))
    sib = pl.BlockSpec((tr, C), lambda h, t, c: (jnp.where(h == c[0], 0, t), 0))
    shape = jax.ShapeDtypeStruct((R, C), F32)
    return pl.pallas_call(
        body, name=name,
        grid_spec=pltpu.PrefetchScalarGridSpec(
            num_scalar_prefetch=1, grid=(2, nt), in_specs=[full, own, sib, full, full], out_specs=[full] * 4),
        out_shape=[shape] * 4,
        compiler_params=_params(2),
    )(cidx, w, mine, other, m, v)


def _adamw_small(ws, gs, ms, vs):
    n = len(ws)

    def body(*refs):
        for i in range(n):
            w_ref, g_ref, m_ref, v_ref = (refs[j * n + i] for j in range(4))
            d_ref, mo_ref, vo_ref = (refs[(4 + j) * n + i] for j in range(3))
            d_ref[...], mo_ref[...], vo_ref[...] = _adamw_math(w_ref[...], g_ref[...], m_ref[...], v_ref[...])

    shapes = [jax.ShapeDtypeStruct(a.shape, F32) for a in ws]
    outs = pl.pallas_call(body, name="adamw_small", out_shape=shapes * 3, compiler_params=_params(0))(*ws, *gs, *ms, *vs)
    return outs[:n], outs[n:2 * n], outs[2 * n:]


def _unstack_cols(w):
    s, r, c = w.shape
    return w.transpose(1, 0, 2).reshape(r, s * c)


def _pad_rows(a, rows):
    return jnp.pad(a, ((0, rows - a.shape[0]), (0, LANES - a.shape[1])))


BIG = ("ffn1_w_gate", "ffn1_w_up", "ffn1_w_down", "w_in", "w_branch_a", "w_branch_b", "w_out",
       "ffn2_w_gate", "ffn2_w_up", "ffn2_w_down")
TRANSPOSED = ("ffn1_w_gate", "ffn1_w_up", "w_in", "ffn2_w_gate", "ffn2_w_up")
WEIGHTS = ("ffn1_norm", "ffn1_w_gate", "ffn1_w_up", "ffn1_w_down", "mix_norm", "w_in", "na_rpb", "sink_logit",
           "w_branch_a", "w_branch_b", "w_out", "ffn2_norm", "ffn2_w_gate", "ffn2_w_up", "ffn2_w_down", "final_norm")


def kernel(x, ffn1_norm, ffn1_w_gate, ffn1_w_up, ffn1_w_down, mix_norm, w_in, na_rpb, sink_logit, w_branch_a, w_branch_b, w_out, ffn2_norm, ffn2_w_gate, ffn2_w_up, ffn2_w_down, final_norm, loss_target, m_ffn1_norm, m_ffn1_w_gate, m_ffn1_w_up, m_ffn1_w_down, m_mix_norm, m_w_in, m_na_rpb, m_sink_logit, m_w_branch_a, m_w_branch_b, m_w_out, m_ffn2_norm, m_ffn2_w_gate, m_ffn2_w_up, m_ffn2_w_down, m_final_norm, v_ffn1_norm, v_ffn1_w_gate, v_ffn1_w_up, v_ffn1_w_down, v_mix_norm, v_w_in, v_na_rpb, v_sink_logit, v_w_branch_a, v_w_branch_b, v_w_out, v_ffn2_norm, v_ffn2_w_gate, v_ffn2_w_up, v_ffn2_w_down, v_final_norm):
    args = dict(locals())
    w = {k: args[k] for k in WEIGHTS}
    mom = {k: args["m_" + k] for k in WEIGHTS}
    var = {k: args["v_" + k] for k in WEIGHTS}
    cidx = lax.axis_index("c").astype(jnp.int32).reshape(1)
    chip = (2 * lax.axis_index("x") + lax.axis_index("y")).astype(jnp.int32).reshape(1)

    def shard(a, k):
        return jnp.swapaxes(a[0], 0, 1) if k in TRANSPOSED else a[0]

    def unshard(a, k):
        return (jnp.swapaxes(a, 0, 1) if k in TRANSPOSED else a)[None]

    def bf16_shards(names):
        return [shard(w[k], k).astype(BF16) for k in names]

    class comm:
        late_rider = _gather_rider(bf16_shards(MIXER + FFN2))

        @staticmethod
        def late(gathered):
            full = dict(zip(MIXER + FFN2, gathered))
            return (full["w_in"].reshape(D_IN, D_MODEL), _unstack_cols(full["w_branch_a"]), _unstack_cols(full["w_branch_b"]),
                    full["w_out"].reshape(D_MODEL, D_MODEL), tuple(full[k] for k in FFN2))

        @staticmethod
        def reduce(names, grads, *, tag):
            return _Reduce(names, grads, cidx, chip, tag=tag)

    f1 = _run_rider(_gather_rider(bf16_shards(FFN1)), name="all_gather_ffn1")
    out = _layer_grads(x[0], loss_target[0], ffn1_norm, f1, mix_norm, None, na_rpb[0], sink_logit[0], ffn2_norm,
                       final_norm.reshape(1, D_MODEL), comm=comm)
    mine = {k: out[k][0] for k in BIG}
    other = {k: out[k][1] for k in BIG}
    grad = {}

    rows = D_MODEL // LANES
    small = jnp.concatenate([
        out["ffn1_norm"].reshape(rows, LANES), out["mix_norm"].reshape(rows, LANES), out["ffn2_norm"].reshape(rows, LANES),
        out["final_norm"].reshape(rows, LANES), out["na_rpb"].reshape(-1, LANES),
        _pad_rows(out["sink_logit"].reshape(1, NB_HEADS), 8), _pad_rows(out["loss"], 8)], axis=0)
    total = _small_allreduce(small)
    n_rpb = NA_HEADS * 2 * NA_KH
    grad["ffn1_norm"] = total[0:rows].reshape(1, D_MODEL)
    grad["mix_norm"] = total[rows:2 * rows].reshape(1, D_MODEL)
    grad["ffn2_norm"] = total[2 * rows:3 * rows].reshape(1, D_MODEL)
    grad["final_norm"] = total[3 * rows:4 * rows].reshape(1, D_MODEL)
    grad["na_rpb"] = total[4 * rows:4 * rows + n_rpb].reshape(NA_HEADS, 2 * NA_KH, LANES)[:, :2 * NA_KH - 1, :2 * NA_KW - 1]
    grad["na_rpb"] = grad["na_rpb"].reshape(NA_HEADS, -1)
    grad["sink_logit"] = total[4 * rows + n_rpb:4 * rows + n_rpb + 1, 0:NB_HEADS]
    loss = total[4 * rows + n_rpb + 8, 0]

    deltas, new_m, new_v, grads_out = {}, {}, {}, {}
    for k in BIG:
        res = _adamw_halves(shard(w[k], k), mine[k], other[k], shard(mom[k], k), shard(var[k], k), cidx, name="adamw_" + k)
        grads_out[k], deltas[k], new_m[k], new_v[k] = (unshard(a, k) for a in res)
    small_names = [k for k in WEIGHTS if k not in BIG]
    res = _adamw_small(*[[a[k].reshape(grad[k].shape) for k in small_names] for a in (w, grad, mom, var)])
    for i, k in enumerate(small_names):
        grads_out[k], deltas[k], new_m[k], new_v[k] = (a.reshape(w[k].shape) for a in (grad[k], res[0][i], res[1][i], res[2][i]))
    return (loss, out["dx"].reshape(x.shape), *[grads_out[k] for k in WEIGHTS], *[deltas[k] for k in WEIGHTS],
            *[new_m[k] for k in WEIGHTS], *[new_v[k] for k in WEIGHTS])
```

```python
import math

import jax
import jax.numpy as jnp
import numpy as np
from jax import lax
from jax.experimental import pallas as pl
from jax.experimental.pallas import tpu as pltpu

F32 = jnp.float32
BF16 = jnp.bfloat16

D_MODEL = 1024
HEAD_DIM = 64
NA_HEADS = 8
NB_HEADS = 8
GRID_W = 64
NA_KH = 8
NA_KW = 16
WIN = 128
ROPE_THETA = 10000.0
EPS = 1e-6
N_CHIPS = 4
QK_SCALE = HEAD_DIM ** -0.5
NEG = -1e30
LANES = 128
VMEM_LIMIT = 56 * 1024 * 1024
HEAD_ROWS = 256
WGRAD_TOKENS_BYTES = 8192

C_QKVA = 3 * NA_HEADS * HEAD_DIM
C_QB = NB_HEADS * HEAD_DIM
C_KB = 2 * HEAD_DIM
C_ROPE = C_QB + C_KB
C_GATES = 2 * D_MODEL
D_IN = C_QKVA + C_QB + 2 * C_KB + C_GATES
O_QB = C_QKVA
O_KB = O_QB + C_QB
O_VB = O_KB + C_KB
O_G = O_VB + C_KB

ADAM_LR = 0.001
ADAM_B1 = 0.9
ADAM_B2 = 0.999
ADAM_EPS = 1e-08
ADAM_WD = 0.01
ADAM_STEP = 10

MESH = pl.DeviceIdType.MESH


def _dot(a, b):
    return jnp.dot(a, b, preferred_element_type=F32)


def _dot_nt(a, b):
    return lax.dot_general(a, b, (((1,), (1,)), ((), ())), preferred_element_type=F32)


def _dot_tn(a, b):
    return lax.dot_general(a, b, (((0,), (0,)), ((), ())), preferred_element_type=F32)


def _params(n_axes):
    return pltpu.CompilerParams(dimension_semantics=("arbitrary",) * n_axes, vmem_limit_bytes=VMEM_LIMIT)


def _rstd(xf):
    return lax.rsqrt(jnp.mean(xf * xf, axis=-1, keepdims=True) + EPS)


def _norm_bwd(dn, xf, g, r):
    xhat = xf * r
    dxh = dn * g
    dx = r * (dxh - xhat * jnp.mean(dxh * xhat, axis=-1, keepdims=True))
    return dx, dn * xhat


def _sigmoid(x):
    return 0.5 * jnp.tanh(0.5 * x) + 0.5


def _loss_head(hf, gv, tgt):
    r = _rstd(hf)
    err = (hf * r) * gv - tgt
    dx, dgr = _norm_bwd(err * (1.0 / hf.shape[-1]), hf, gv, r)
    return 0.5 * jnp.mean(err * err, axis=-1, keepdims=True), dx, dgr


def _ffn_fwd(x, g, wg, wu, wd, *, name, tm=1024, sub=512, rider=None, head=None):
    T, D = x.shape
    F = wg.shape[1]
    tm = min(tm, T)
    sub = min(sub, tm)
    n_head = 0 if head is None else 2

    def body(*refs):
        x_ref, g_ref, wg_ref, wu_ref, wd_ref = refs[:5]
        h_ref, n_ref, hdn_ref, p_ref, q_ref = refs[5 + n_head:10 + n_head]
        i, s = pl.program_id(0), pl.program_id(1)
        _ffn_fwd_step(x_ref, g_ref, wg_ref, wu_ref, wd_ref, h_ref, n_ref, hdn_ref, p_ref, q_ref, s)
        if head is not None:
            gf_ref, t_ref = refs[5:7]
            loss_ref, dgf_ref = refs[10 + n_head:]

            @pl.when((i == 0) & (s == 0))
            def _():
                loss_ref[...] = jnp.zeros_like(loss_ref)
                dgf_ref[...] = jnp.zeros_like(dgf_ref)

            @pl.when(s == N_CHIPS - 1)
            def _():
                for u in range(tm // HEAD_ROWS):
                    r = pl.ds(u * HEAD_ROWS, HEAD_ROWS)
                    terms, dh, dgr = _loss_head(h_ref[r, :], gf_ref[...], t_ref[r, :])
                    loss_ref[...] += jnp.broadcast_to(jnp.sum(terms), loss_ref.shape)
                    dgf_ref[...] += jnp.sum(dgr, axis=0, keepdims=True)
                    h_ref[r, :] = dh

    def _ffn_fwd_step(x_ref, g_ref, wg_ref, wu_ref, wd_ref, h_ref, n_ref, hdn_ref, p_ref, q_ref, s):

        @pl.when(s == 0)
        def _():
            xf = x_ref[...]
            n_ref[...] = ((xf * _rstd(xf)) * g_ref[...]).astype(BF16)
            h_ref[...] = xf

        rows = [pl.ds(u * sub, sub) for u in range(tm // sub)]
        ab = [(_dot_nt(n_ref[r, :], wg_ref[0]), _dot_nt(n_ref[r, :], wu_ref[0])) for r in rows]
        hdns = []
        for r, (a, b) in zip(rows, ab):
            sg = _sigmoid(a)
            silu = a * sg
            hdn = (silu * b).astype(BF16)
            hdn_ref[0, r, :] = hdn
            p_ref[0, r, :] = (b * (sg + silu * (1.0 - sg))).astype(BF16)
            q_ref[0, r, :] = silu.astype(BF16)
            hdns.append(hdn)
        for r, hdn in zip(rows, hdns):
            h_ref[r, :] += 0.5 * _dot(hdn, wd_ref[0])

    tok = pl.BlockSpec((tm, D), lambda i, s: (i, 0))
    hid = pl.BlockSpec((1, tm, F), lambda i, s: (s, i, 0))
    wspec = pl.BlockSpec((1, F, D), lambda i, s: (s, 0, 0))
    hshape = jax.ShapeDtypeStruct((N_CHIPS, T, F), BF16)
    grid = (T // tm, N_CHIPS)
    vec = pl.BlockSpec((1, D), lambda i, s: (0, 0))
    head_in, head_in_specs, head_out, head_out_specs = [], [], [], []
    if head is not None:
        head_in, head_in_specs = list(head), [vec, tok]
        head_out = [jax.ShapeDtypeStruct((1, LANES), F32), jax.ShapeDtypeStruct((1, D), F32)]
        head_out_specs = [pl.BlockSpec((1, LANES), lambda i, s: (0, 0)), vec]
    n_main = 5 + n_head
    body, r_in, r_in_specs, r_out, r_out_specs, scratch = _ride(body, n_main, n_main, rider, grid, (grid[0] * grid[1] * 7) // 8)
    outs = pl.pallas_call(
        body, name=name, grid=grid,
        in_specs=[tok, vec, wspec, wspec, wspec] + head_in_specs + r_in_specs,
        out_specs=[tok, tok, hid, hid, hid] + head_out_specs + r_out_specs,
        out_shape=[jax.ShapeDtypeStruct((T, D), F32), jax.ShapeDtypeStruct((T, D), BF16), hshape, hshape, hshape]
        + head_out + r_out,
        scratch_shapes=scratch,
        compiler_params=_params(2),
    )(x, g, wg, wu, wd, *head_in, *r_in)
    return (*outs[:n_main], list(outs[n_main:]))


def _ffn_bwd(dh, x, g, p, q, wg, wu, wd, *, name, tm=1024, sub=256, rider=None):
    T, D = x.shape
    F = wg.shape[1]
    tm = min(tm, T)
    sub = min(sub, tm)

    def body(dh_ref, x_ref, g_ref, p_ref, q_ref, wg_ref, wu_ref, wd_ref, dx_ref, da_ref, db_ref, dg_ref):
        i, s = pl.program_id(0), pl.program_id(1)

        @pl.when((i == 0) & (s == 0))
        def _():
            dg_ref[...] = jnp.zeros_like(dg_ref)

        @pl.when(s == 0)
        def _():
            dx_ref[...] = jnp.zeros_like(dx_ref)

        rows = [pl.ds(u * sub, sub) for u in range(tm // sub)]
        dhdn = [_dot_nt((0.5 * dh_ref[r, :]).astype(BF16), wd_ref[0]) for r in rows]
        das, dbs = [], []
        for r, dd in zip(rows, dhdn):
            da = (dd * p_ref[0, r, :].astype(F32)).astype(BF16)
            db = (dd * q_ref[0, r, :].astype(F32)).astype(BF16)
            da_ref[0, r, :] = da
            db_ref[0, r, :] = db
            das.append(da)
            dbs.append(db)
        for r, da, db in zip(rows, das, dbs):
            dx_ref[r, :] += _dot(da, wg_ref[0]) + _dot(db, wu_ref[0])

        @pl.when(s == N_CHIPS - 1)
        def _():
            xf = x_ref[...]
            dx, dgr = _norm_bwd(dx_ref[...], xf, g_ref[...], _rstd(xf))
            dg_ref[...] += jnp.sum(dgr, axis=0, keepdims=True)
            dx_ref[...] = dh_ref[...] + dx

    tok = pl.BlockSpec((tm, D), lambda i, s: (i, 0))
    hid = pl.BlockSpec((1, tm, F), lambda i, s: (s, i, 0))
    vec = pl.BlockSpec((1, D), lambda i, s: (0, 0))
    hshape = jax.ShapeDtypeStruct((N_CHIPS, T, F), BF16)
    wspec = pl.BlockSpec((1, F, D), lambda i, s: (s, 0, 0))
    grid = (T // tm, N_CHIPS)
    body, r_in, r_in_specs, r_out, r_out_specs, scratch = _ride(body, 8, 4, rider, grid, None)
    outs = pl.pallas_call(
        body, name=name, grid=grid,
        in_specs=[tok, tok, vec, hid, hid, wspec, wspec, wspec] + r_in_specs,
        out_specs=[tok, hid, hid, vec] + r_out_specs,
        out_shape=[jax.ShapeDtypeStruct((T, D), F32), hshape, hshape, jax.ShapeDtypeStruct((1, D), F32)] + r_out,
        scratch_shapes=scratch,
        compiler_params=_params(2),
    )(dh, x, g, p, q, wg, wu, wd, *r_in)
    return (*outs[:4], list(outs[4:]))


def _wgrad(a, b, *, a_block, a_map, b_block, b_map, out_shape, o_block, o_map, grid, scale=1.0, name, rider=None):
    def body(a_ref, b_ref, o_ref):
        @pl.when(pl.program_id(len(grid) - 1) == 0)
        def _():
            o_ref[...] = jnp.zeros_like(o_ref)

        av = a_ref[...]
        bv = b_ref[...]
        av = av.reshape(av.shape[-2:]).astype(BF16)
        bv = bv.reshape(bv.shape[-2:])
        if scale != 1.0:
            bv = scale * bv
        o_ref[...] += _dot_tn(av, bv.astype(BF16)).reshape(o_ref.shape)

    body, r_in, r_in_specs, r_out, r_out_specs, scratch = _ride(body, 2, 1, rider, grid, None)
    outs = pl.pallas_call(
        body, name=name, grid=grid,
        in_specs=[pl.BlockSpec(a_block, a_map), pl.BlockSpec(b_block, b_map)] + r_in_specs,
        out_specs=[pl.BlockSpec(o_block, o_map)] + r_out_specs,
        out_shape=[jax.ShapeDtypeStruct(out_shape, F32)] + r_out,
        scratch_shapes=scratch,
        compiler_params=_params(len(grid)),
    )(a, b, *r_in)
    return outs[0], list(outs[1:])


def _wgrad_rows(a, b, n_blocks, *, name, tk=2048):
    T, N = b.shape
    M = a.shape[1] // n_blocks
    tk = min(tk, T)
    return _wgrad(a, b, a_block=(tk, M), a_map=lambda s, k: (k, s), b_block=(tk, N), b_map=lambda s, k: (k, 0),
                  out_shape=(n_blocks, M, N), o_block=(1, M, N), o_map=lambda s, k: (s, 0, 0), grid=(n_blocks, T // tk), name=name)


def _wgrad_shard_a(a, b, *, name, scale=1.0, rider=None):
    S, T, M = a.shape
    N = b.shape[1]
    tk = min(WGRAD_TOKENS_BYTES // b.dtype.itemsize, T)
    return _wgrad(a, b, a_block=(1, tk, M), a_map=lambda s, k: (s, k, 0), b_block=(tk, N), b_map=lambda s, k: (k, 0),
                  out_shape=(S, M, N), o_block=(1, M, N), o_map=lambda s, k: (s, 0, 0), grid=(S, T // tk), scale=scale,
                  name=name, rider=rider)


def _wgrad_cols(a, b, n_blocks, *, name, tk=2048):
    T, M = a.shape
    N = b.shape[1] // n_blocks
    tk = min(tk, T)

    def body(a_ref, b_ref, o_ref):
        @pl.when(pl.program_id(0) == 0)
        def _():
            o_ref[...] = jnp.zeros_like(o_ref)

        r = _dot_tn(a_ref[...].astype(BF16), b_ref[...].astype(BF16))
        for s in range(n_blocks):
            o_ref[s] += r[:, s * N:(s + 1) * N]

    return pl.pallas_call(
        body, name=name, grid=(T // tk,),
        in_specs=[pl.BlockSpec((tk, M), lambda k: (k, 0)), pl.BlockSpec((tk, n_blocks * N), lambda k: (k, 0))],
        out_specs=pl.BlockSpec((n_blocks, M, N), lambda k: (0, 0, 0)),
        out_shape=jax.ShapeDtypeStruct((n_blocks, M, N), F32),
        compiler_params=_params(1),
    )(a, b)


def _rope_tables(T):
    half = HEAD_DIM // 2
    inv = np.float32(ROPE_THETA) ** (-np.arange(half, dtype=np.float32) / np.float32(half))
    ang = np.arange(T, dtype=np.float32)[:, None] * inv[None, :]
    cos, sin, zero = np.cos(ang), np.sin(ang), np.zeros_like(ang)
    reps = LANES // HEAD_DIM
    return (jnp.asarray(np.tile(np.concatenate([cos, cos], axis=1), (1, reps))),
            jnp.asarray(np.tile(np.concatenate([-sin, zero], axis=1), (1, reps))),
            jnp.asarray(np.tile(np.concatenate([zero, sin], axis=1), (1, reps))))


def _rope(x, cos, sa, sb, sign):
    half = HEAD_DIM // 2
    return x * cos + sign * (pltpu.roll(x, LANES - half, 1) * sa + pltpu.roll(x, half, 1) * sb)


def _mix_in_fwd(h, g, w_in, tables, *, tm=512):
    T, D = h.shape

    def body(h_ref, g_ref, w_ref, cos_ref, sa_ref, sb_ref, u_ref, qkva_ref, qb_ref, kvb_ref, gates_ref):
        hf = h_ref[...]
        u = ((hf * _rstd(hf)) * g_ref[...]).astype(BF16)
        u_ref[...] = u
        qkva_ref[...] = _dot_nt(u, w_ref[0:C_QKVA, :]).astype(BF16)
        zr = _dot_nt(u, w_ref[O_QB:O_QB + C_ROPE, :])
        cos, sa, sb = cos_ref[...], sa_ref[...], sb_ref[...]
        for j in range(C_ROPE // LANES):
            rj = _rope(zr[:, j * LANES:(j + 1) * LANES], cos, sa, sb, 1.0).astype(BF16)
            if j < C_QB // LANES:
                qb_ref[:, j * LANES:(j + 1) * LANES] = rj
            else:
                kvb_ref[:, 0:C_KB] = rj
        kvb_ref[:, C_KB:2 * C_KB] = _dot_nt(u, w_ref[O_VB:O_VB + C_KB, :]).astype(BF16)
        gates_ref[...] = _dot_nt(u, w_ref[O_G:O_G + C_GATES, :])

    def tok(n):
        return pl.BlockSpec((tm, n), lambda i: (i, 0))

    return pl.pallas_call(
        body, name="mix_in_fwd", grid=(T // tm,),
        in_specs=[tok(D), pl.BlockSpec((1, D), lambda i: (0, 0)), pl.BlockSpec((D_IN, D), lambda i: (0, 0), pipeline_mode=pl.Buffered(1)),
                  tok(LANES), tok(LANES), tok(LANES)],
        out_specs=[tok(D), tok(C_QKVA), tok(C_QB), tok(2 * C_KB), tok(C_GATES)],
        out_shape=[jax.ShapeDtypeStruct((T, D), BF16), jax.ShapeDtypeStruct((T, C_QKVA), BF16),
                   jax.ShapeDtypeStruct((T, C_QB), BF16), jax.ShapeDtypeStruct((T, 2 * C_KB), BF16),
                   jax.ShapeDtypeStruct((T, C_GATES), F32)],
        compiler_params=_params(1),
    )(h, g, w_in, *tables)


def _mix_in_bwd(dqa, dka, dva, dqb, dkb, dvb, dgates, h, g, dres, w_in, tables, *, tm=512, rider=None):
    T, D = h.shape

    def body(dqa_ref, dka_ref, dva_ref, dqb_ref, dkb_ref, dvb_ref, dgt_ref, h_ref, g_ref, dres_ref, w_ref,
             cos_ref, sa_ref, sb_ref, dz_ref, dh_ref, dg_ref):
        @pl.when(pl.program_id(0) == 0)
        def _():
            dg_ref[...] = jnp.zeros_like(dg_ref)

        na = NA_HEADS * HEAD_DIM
        dz_ref[:, 0:na] = dqa_ref[...].astype(BF16)
        dz_ref[:, na:2 * na] = dka_ref[...].astype(BF16)
        dz_ref[:, 2 * na:3 * na] = dva_ref[...].astype(BF16)
        cos, sa, sb = cos_ref[...], sa_ref[...], sb_ref[...]
        for j in range(C_QB // LANES):
            dz_ref[:, O_QB + j * LANES:O_QB + (j + 1) * LANES] = _rope(
                dqb_ref[:, j * LANES:(j + 1) * LANES], cos, sa, sb, -1.0).astype(BF16)
        dz_ref[:, O_KB:O_KB + C_KB] = _rope(dkb_ref[...], cos, sa, sb, -1.0).astype(BF16)
        dz_ref[:, O_VB:O_VB + C_KB] = dvb_ref[...].astype(BF16)
        dz_ref[:, O_G:O_G + C_GATES] = dgt_ref[...].astype(BF16)
        du = _dot(dz_ref[...], w_ref[...])
        hf = h_ref[...]
        dx, dgr = _norm_bwd(du, hf, g_ref[...], _rstd(hf))
        dg_ref[...] += jnp.sum(dgr, axis=0, keepdims=True)
        dh_ref[...] = dres_ref[...] + dx

    def tok(n):
        return pl.BlockSpec((tm, n), lambda i: (i, 0))

    vec = pl.BlockSpec((1, D), lambda i: (0, 0))
    na = NA_HEADS * HEAD_DIM
    grid = (T // tm,)
    body, r_in, r_in_specs, r_out, r_out_specs, scratch = _ride(body, 14, 3, rider, grid, None)
    outs = pl.pallas_call(
        body, name="mix_in_bwd", grid=grid,
        in_specs=[tok(na), tok(na), tok(na), tok(C_QB), tok(C_KB), tok(C_KB), tok(C_GATES), tok(D), vec, tok(D),
                  pl.BlockSpec((D_IN, D), lambda i: (0, 0), pipeline_mode=pl.Buffered(1)), tok(LANES), tok(LANES), tok(LANES)]
        + r_in_specs,
        out_specs=[tok(D_IN), tok(D), vec] + r_out_specs,
        out_shape=[jax.ShapeDtypeStruct((T, D_IN), BF16), jax.ShapeDtypeStruct((T, D), F32),
                   jax.ShapeDtypeStruct((1, D), F32)] + r_out,
        scratch_shapes=scratch,
        compiler_params=_params(1),
    )(dqa, dka, dva, dqb, dkb, dvb, dgates, h, g, dres, w_in, *tables, *r_in)
    return (*outs[:3], list(outs[3:]))


def _na_bias_slabs(rpb):
    H = rpb.shape[0]
    ncell = GRID_W * GRID_W
    cell = np.arange(ncell)
    co = cell % GRID_W - cell // GRID_W + (NA_KW - 1)
    e_co = jnp.asarray((np.arange(LANES)[:, None] == co[None, :]).astype(np.float32))
    table = jnp.pad(rpb, ((0, 0), (0, 1), (0, LANES - rpb.shape[2]))).reshape(H * 2 * NA_KH, LANES)

    def body(t_ref, e_ref, o_ref):
        o_ref[...] = jnp.dot(t_ref[...], e_ref[...], preferred_element_type=F32, precision=lax.Precision.HIGHEST)

    toeplitz = pl.pallas_call(
        body, name="rpb_unfold", out_shape=jax.ShapeDtypeStruct((H * 2 * NA_KH, ncell), F32),
        compiler_params=_params(0),
    )(table, e_co).reshape(H, 2 * NA_KH, GRID_W, GRID_W)

    def assemble(tz_ref, o_ref):
        c = lax.broadcasted_iota(jnp.int32, (GRID_W, GRID_W), 0)
        k = lax.broadcasted_iota(jnp.int32, (GRID_W, GRID_W), 1)
        cs = jnp.clip(c - NA_KW // 2, 0, GRID_W - NA_KW)
        inwin = (k >= cs) & (k < cs + NA_KW)
        for ro0 in range(NA_KH):
            for hh in range(2):
                for i in range(NA_KH):
                    o_ref[0, ro0, hh * GRID_W:(hh + 1) * GRID_W, i * GRID_W:(i + 1) * GRID_W] = jnp.where(
                        inwin, tz_ref[hh, ro0 + i], NEG)

    return pl.pallas_call(
        assemble, name="na_bias_slabs", grid=(H // 2,),
        in_specs=[pl.BlockSpec((2, 2 * NA_KH, GRID_W, GRID_W), lambda p: (p, 0, 0, 0))],
        out_specs=pl.BlockSpec((1, NA_KH, 2 * GRID_W, NA_KH * GRID_W), lambda p: (p, 0, 0, 0)),
        out_shape=jax.ShapeDtypeStruct((H // 2, NA_KH, 2 * GRID_W, NA_KH * GRID_W), F32),
        compiler_params=_params(1),
    )(toeplitz)


def _half_masks(rows):
    lane = lax.broadcasted_iota(jnp.int32, (rows, LANES), 1)
    left = lane < HEAD_DIM
    return left, (left, jnp.logical_not(left))


def _stack_heads(x):
    left, halves = _half_masks(x.shape[0])
    xf = x.astype(F32)
    return jnp.concatenate([jnp.where(m, xf, 0.0).astype(BF16) for m in halves], axis=0)


def _unstack_heads(o):
    rows = o.shape[0] // 2
    left, _ = _half_masks(rows)
    return jnp.where(left, o[:rows], o[rows:])


def _na_row(j, t, rb, rows):
    r = j * rb + t
    rs = jnp.clip(r - NA_KH // 2, 0, rows - NA_KH)
    return pl.multiple_of(t * GRID_W, GRID_W), pl.multiple_of(rs * GRID_W, GRID_W), rs - r + (NA_KH - 1)


def _na_specs(T, rb):
    qrows = GRID_W * rb
    pairs = NA_HEADS // 2
    return ([pl.BlockSpec((qrows, LANES), lambda p, j: (j, p)),
             pl.BlockSpec((T, LANES), lambda p, j: (0, pairs + p)),
             pl.BlockSpec((T, LANES), lambda p, j: (0, 2 * pairs + p))],
            pl.BlockSpec((1, NA_KH, 2 * GRID_W, NA_KH * GRID_W), lambda p, j: (p, 0, 0, 0)))


def _softmax(s):
    p = jnp.exp(s - jnp.max(s, axis=-1, keepdims=True))
    return p / jnp.sum(p, axis=-1, keepdims=True)


def _na_fwd(qkva, bias, *, rb=16, group=16):
    T = qkva.shape[0]
    rows = T // GRID_W
    nkeys = NA_KH * GRID_W
    rb = min(rb, rows)
    group = min(group, rb)

    def body(q_ref, k_ref, v_ref, bias_ref, y_ref):
        j = pl.program_id(1)

        def rows_step(t, carry):
            at = [_na_row(j, t * group + u, rb, rows) for u in range(group)]
            s = [_dot_nt(_stack_heads(q_ref[pl.ds(q0, GRID_W), :]), k_ref[pl.ds(k0, nkeys), :]) for q0, k0, _ in at]
            p = [_softmax(su * QK_SCALE + bias_ref[0, ro0]) for su, (_, _, ro0) in zip(s, at)]
            o = [_dot(pu.astype(BF16), v_ref[pl.ds(k0, nkeys), :]) for pu, (_, k0, _) in zip(p, at)]
            for ou, (q0, _, _) in zip(o, at):
                y_ref[pl.ds(q0, GRID_W), :] = _unstack_heads(ou).astype(BF16)
            return carry

        lax.fori_loop(0, rb // group, rows_step, 0)

    qkv_specs, bias_spec = _na_specs(T, rb)
    return pl.pallas_call(
        body, name="na_fwd", grid=(NA_HEADS // 2, rows // rb),
        in_specs=qkv_specs + [bias_spec],
        out_specs=qkv_specs[0],
        out_shape=jax.ShapeDtypeStruct((T, NA_HEADS * HEAD_DIM), BF16),
        compiler_params=_params(2),
    )(qkva, qkva, qkva, bias)


def _na_bwd(qkva, dy, bias, *, rb=8, group=8, rider=None):
    T = qkva.shape[0]
    rows = T // GRID_W
    nkeys = NA_KH * GRID_W

    def body(q_ref, k_ref, v_ref, dy_ref, bias_ref, dq_ref, dk_ref, dv_ref, dbias_ref):
        j = pl.program_id(1)

        @pl.when(j == 0)
        def _():
            dk_ref[...] = jnp.zeros_like(dk_ref)
            dv_ref[...] = jnp.zeros_like(dv_ref)
            dbias_ref[...] = jnp.zeros_like(dbias_ref)

        def rows_step(t, carry):
            at = [_na_row(j, t * group + u, rb, rows) for u in range(group)]
            qs = [_stack_heads(q_ref[pl.ds(q0, GRID_W), :]) for q0, _, _ in at]
            dys = [_stack_heads(dy_ref[pl.ds(q0, GRID_W), :]) for q0, _, _ in at]
            s = [_dot_nt(qu, k_ref[pl.ds(k0, nkeys), :]) for qu, (_, k0, _) in zip(qs, at)]
            dp = [_dot_nt(du, v_ref[pl.ds(k0, nkeys), :]) for du, (_, k0, _) in zip(dys, at)]
            p = [_softmax(su * QK_SCALE + bias_ref[0, ro0]) for su, (_, _, ro0) in zip(s, at)]
            ds = [pu * (du - jnp.sum(pu * du, axis=-1, keepdims=True)) for pu, du in zip(p, dp)]
            for u, (q0, k0, ro0) in enumerate(at):
                dbias_ref[0, ro0] += ds[u]
                dsb = ds[u].astype(BF16)
                dq_ref[pl.ds(q0, GRID_W), :] = (_unstack_heads(_dot(dsb, k_ref[pl.ds(k0, nkeys), :])) * QK_SCALE).astype(BF16)
                dk_ref[pl.ds(k0, nkeys), :] += _dot_tn(dsb, qs[u]) * QK_SCALE
                dv_ref[pl.ds(k0, nkeys), :] += _dot_tn(p[u].astype(BF16), dys[u])
            return carry

        lax.fori_loop(0, rb // group, rows_step, 0)

    qkv_specs, bias_spec = _na_specs(T, rb)
    width = NA_HEADS * HEAD_DIM
    kv_out = pl.BlockSpec((T, LANES), lambda p, j: (0, p))
    grid = (NA_HEADS // 2, rows // rb)
    body, r_in, r_in_specs, r_out, r_out_specs, scratch = _ride(body, 5, 4, rider, grid, None)
    outs = pl.pallas_call(
        body, name="na_bwd", grid=grid,
        in_specs=qkv_specs + [qkv_specs[0], bias_spec] + r_in_specs,
        out_specs=[qkv_specs[0], kv_out, kv_out, bias_spec] + r_out_specs,
        out_shape=[jax.ShapeDtypeStruct((T, width), BF16), jax.ShapeDtypeStruct((T, width), F32),
                   jax.ShapeDtypeStruct((T, width), F32), jax.ShapeDtypeStruct(bias.shape, F32)] + r_out,
        scratch_shapes=scratch,
        compiler_params=_params(2),
    )(qkva, qkva, qkva, dy, bias, *r_in)
    return (*outs[:4], list(outs[4:]))


def _rpb_fold(dslab):
    pairs = dslab.shape[0]
    H = 2 * pairs
    ncell = GRID_W * GRID_W

    def disassemble(d_ref, tz_ref):
        tz_ref[...] = jnp.zeros_like(tz_ref)
        for ro0 in range(NA_KH):
            for hh in range(2):
                for i in range(NA_KH):
                    tz_ref[hh, ro0 + i] += d_ref[0, ro0, hh * GRID_W:(hh + 1) * GRID_W, i * GRID_W:(i + 1) * GRID_W]

    dtoeplitz = pl.pallas_call(
        disassemble, name="rpb_fold_tiles", grid=(pairs,),
        in_specs=[pl.BlockSpec((1, NA_KH, 2 * GRID_W, NA_KH * GRID_W), lambda p: (p, 0, 0, 0))],
        out_specs=pl.BlockSpec((2, 2 * NA_KH, GRID_W, GRID_W), lambda p: (p, 0, 0, 0)),
        out_shape=jax.ShapeDtypeStruct((H, 2 * NA_KH, GRID_W, GRID_W), F32),
        compiler_params=_params(1),
    )(dslab).reshape(H * 2 * NA_KH, ncell)
    cell = np.arange(ncell)
    co = cell % GRID_W - cell // GRID_W + (NA_KW - 1)
    e_co = jnp.asarray((co[:, None] == np.arange(LANES)[None, :]).astype(np.float32))

    def diagonals(x_ref, e_ref, o_ref):
        o_ref[...] = jnp.dot(x_ref[...], e_ref[...], preferred_element_type=F32, precision=lax.Precision.HIGHEST)

    return pl.pallas_call(
        diagonals, name="rpb_fold", out_shape=jax.ShapeDtypeStruct((H * 2 * NA_KH, LANES), F32),
        compiler_params=_params(0),
    )(dtoeplitz, e_co).reshape(H, 2 * NA_KH, LANES)


SWA_KEYS = 3 * WIN


def _swa_block(j, t, qbn, T):
    blk = j * qbn + t
    start = jnp.clip((blk - 1) * WIN, 0, T - SWA_KEYS)
    row = lax.broadcasted_iota(jnp.int32, (2 * WIN, SWA_KEYS), 0)
    qpos = blk * WIN + jnp.where(row < WIN, row, row - WIN)
    kpos = start + lax.broadcasted_iota(jnp.int32, (2 * WIN, SWA_KEYS), 1)
    return pl.multiple_of(t * WIN, WIN), pl.multiple_of(start, WIN), jnp.abs(qpos - kpos) <= WIN


def _swa_sinks(sink_ref, p):
    row = lax.broadcasted_iota(jnp.int32, (2 * WIN, 1), 0)
    return jnp.where(row < WIN, sink_ref[p], sink_ref[p + NB_HEADS // 2])


def _swa_probs(s, mask, sink):
    s = jnp.where(mask, s * QK_SCALE, NEG)
    m = jnp.maximum(jnp.max(s, axis=-1, keepdims=True), sink)
    e = jnp.exp(s - m)
    esink = jnp.exp(sink - m)
    den = jnp.sum(e, axis=-1, keepdims=True) + esink
    return e / den, esink / den


def _swa_specs(T, qbn):
    return [pl.BlockSpec(memory_space=pltpu.SMEM),
            pl.BlockSpec((WIN * qbn, LANES), lambda p, j: (j, p)),
            pl.BlockSpec((T, LANES), lambda p, j: (0, 0)),
            pl.BlockSpec((T, LANES), lambda p, j: (0, 1))]


def _swa_fwd(qb, kvb, sink, *, qbn=16, group=16):
    T = qb.shape[0]
    pairs = NB_HEADS // 2
    qbn = min(qbn, T // WIN)
    group = min(group, qbn)

    def body(sink_ref, q_ref, k_ref, v_ref, y_ref):
        p, j = pl.program_id(0), pl.program_id(1)
        sinks = _swa_sinks(sink_ref, p)

        def blocks_step(t, carry):
            at = [_swa_block(j, t * group + u, qbn, T) for u in range(group)]
            s = [_dot_nt(_stack_heads(q_ref[pl.ds(q0, WIN), :]), k_ref[pl.ds(k0, SWA_KEYS), :]) for q0, k0, _ in at]
            pr = [_swa_probs(su, mask, sinks)[0] for su, (_, _, mask) in zip(s, at)]
            o = [_dot(pu.astype(BF16), v_ref[pl.ds(k0, SWA_KEYS), :]) for pu, (_, k0, _) in zip(pr, at)]
            for ou, (q0, _, _) in zip(o, at):
                y_ref[pl.ds(q0, WIN), :] = _unstack_heads(ou).astype(BF16)
            return carry

        lax.fori_loop(0, qbn // group, blocks_step, 0)

    specs = _swa_specs(T, qbn)
    return pl.pallas_call(
        body, name="swa_fwd", grid=(pairs, T // (WIN * qbn)),
        in_specs=specs, out_specs=specs[1],
        out_shape=jax.ShapeDtypeStruct((T, NB_HEADS * HEAD_DIM), BF16),
        compiler_params=_params(2),
    )(sink, qb, kvb, kvb)


def _swa_bwd(qb, kvb, dy, sink, *, qbn=8, group=8, rider=None):
    T = qb.shape[0]
    pairs = NB_HEADS // 2
    qbn = min(qbn, T // WIN)
    group = min(group, qbn)

    def body(sink_ref, q_ref, k_ref, v_ref, dy_ref, dq_ref, dk_ref, dv_ref, dsink_ref):
        p, j = pl.program_id(0), pl.program_id(1)
        sinks = _swa_sinks(sink_ref, p)

        @pl.when((p == 0) & (j == 0))
        def _():
            dk_ref[...] = jnp.zeros_like(dk_ref)
            dv_ref[...] = jnp.zeros_like(dv_ref)

        @pl.when(j == 0)
        def _():
            dsink_ref[...] = jnp.zeros_like(dsink_ref)

        def blocks_step(t, carry):
            at = [_swa_block(j, t * group + u, qbn, T) for u in range(group)]
            qs = [_stack_heads(q_ref[pl.ds(q0, WIN), :]) for q0, _, _ in at]
            dys = [_stack_heads(dy_ref[pl.ds(q0, WIN), :]) for q0, _, _ in at]
            s = [_dot_nt(qu, k_ref[pl.ds(k0, SWA_KEYS), :]) for qu, (_, k0, _) in zip(qs, at)]
            dp = [_dot_nt(du, v_ref[pl.ds(k0, SWA_KEYS), :]) for du, (_, k0, _) in zip(dys, at)]
            probs = [_swa_probs(su, mask, sinks) for su, (_, _, mask) in zip(s, at)]
            for u, (q0, k0, _) in enumerate(at):
                pr, psink = probs[u]
                delta = jnp.sum(pr * dp[u], axis=-1, keepdims=True)
                dsb = (pr * (dp[u] - delta)).astype(BF16)
                dsk = psink * delta
                for hh in range(2):
                    dsink_ref[0, hh:hh + 1, :] += jnp.broadcast_to(-jnp.sum(dsk[hh * WIN:(hh + 1) * WIN]), (1, LANES))
                dq_ref[pl.ds(q0, WIN), :] = _unstack_heads(_dot(dsb, k_ref[pl.ds(k0, SWA_KEYS), :])) * QK_SCALE
                dk_ref[pl.ds(k0, SWA_KEYS), :] += _dot_tn(dsb, qs[u]) * QK_SCALE
                dv_ref[pl.ds(k0, SWA_KEYS), :] += _dot_tn(pr.astype(BF16), dys[u])
            return carry

        lax.fori_loop(0, qbn // group, blocks_step, 0)

    specs = _swa_specs(T, qbn)
    kv_out = pl.BlockSpec((T, LANES), lambda p, j: (0, 0))
    grid = (pairs, T // (WIN * qbn))
    body, r_in, r_in_specs, r_out, r_out_specs, scratch = _ride(body, 5, 4, rider, grid, None)
    outs = pl.pallas_call(
        body, name="swa_bwd", grid=grid,
        in_specs=specs + [specs[1]] + r_in_specs,
        out_specs=[specs[1], kv_out, kv_out, pl.BlockSpec((1, 8, LANES), lambda p, j: (p, 0, 0))] + r_out_specs,
        out_shape=[jax.ShapeDtypeStruct((T, NB_HEADS * HEAD_DIM), F32), jax.ShapeDtypeStruct((T, LANES), F32),
                   jax.ShapeDtypeStruct((T, LANES), F32), jax.ShapeDtypeStruct((pairs, 8, LANES), F32)] + r_out,
        scratch_shapes=scratch,
        compiler_params=_params(2),
    )(sink, qb, kvb, kvb, dy, *r_in)
    return (*outs[:4], list(outs[4:]))


def _merge_fwd(ya, yb, gates, wa, wb, wout, h, *, tm=512):
    T, D = h.shape
    W = ya.shape[1]

    def body(ya_ref, yb_ref, gt_ref, wa_ref, wb_ref, wo_ref, h_ref, h2_ref, mg_ref):
        pa = _dot(ya_ref[...], wa_ref[...])
        pb = _dot(yb_ref[...], wb_ref[...])
        mg = (jax.nn.sigmoid(gt_ref[:, 0:D]) * pa + jax.nn.sigmoid(gt_ref[:, D:2 * D]) * pb).astype(BF16)
        mg_ref[...] = mg
        h2_ref[...] = h_ref[...] + _dot(mg, wo_ref[...])

    def tok(n):
        return pl.BlockSpec((tm, n), lambda i: (i, 0))

    def full(r, c):
        return pl.BlockSpec((r, c), lambda i: (0, 0))

    return pl.pallas_call(
        body, name="merge_fwd", grid=(T // tm,),
        in_specs=[tok(W), tok(W), tok(2 * D), full(W, D), full(W, D), full(D, D), tok(D)],
        out_specs=[tok(D), tok(D)],
        out_shape=[jax.ShapeDtypeStruct((T, D), F32), jax.ShapeDtypeStruct((T, D), BF16)],
        compiler_params=_params(1),
    )(ya, yb, gates, wa, wb, wout, h)


def _merge_bwd(dh, ya, yb, gates, wa, wb, wout, *, tm=512, rider=None):
    T, D = dh.shape
    W = ya.shape[1]

    def body(dh_ref, ya_ref, yb_ref, gt_ref, wa_ref, wb_ref, wo_ref, dya_ref, dyb_ref, dpa_ref, dpb_ref, dgt_ref):
        dmg = _dot_nt(dh_ref[...].astype(BF16), wo_ref[...])
        for y_ref, w_ref, dy_ref, dp_ref, lo in ((ya_ref, wa_ref, dya_ref, dpa_ref, 0), (yb_ref, wb_ref, dyb_ref, dpb_ref, D)):
            sg = jax.nn.sigmoid(gt_ref[:, lo:lo + D])
            dp = (dmg * sg).astype(BF16)
            dp_ref[...] = dp
            dgt_ref[:, lo:lo + D] = (dmg * _dot(y_ref[...], w_ref[...]) * (sg * (1.0 - sg))).astype(BF16)
            dy_ref[...] = _dot_nt(dp, w_ref[...]).astype(BF16)

    def tok(n):
        return pl.BlockSpec((tm, n), lambda i: (i, 0))

    def full(r, c):
        return pl.BlockSpec((r, c), lambda i: (0, 0))

    grid = (T // tm,)
    body, r_in, r_in_specs, r_out, r_out_specs, scratch = _ride(body, 7, 5, rider, grid, None)
    outs = pl.pallas_call(
        body, name="merge_bwd", grid=grid,
        in_specs=[tok(D), tok(W), tok(W), tok(2 * D), full(W, D), full(W, D), full(D, D)] + r_in_specs,
        out_specs=[tok(W), tok(W), tok(D), tok(D), tok(2 * D)] + r_out_specs,
        out_shape=[jax.ShapeDtypeStruct((T, W), BF16), jax.ShapeDtypeStruct((T, W), BF16),
                   jax.ShapeDtypeStruct((T, D), BF16), jax.ShapeDtypeStruct((T, D), BF16),
                   jax.ShapeDtypeStruct((T, 2 * D), BF16)] + r_out,
        scratch_shapes=scratch,
        compiler_params=_params(1),
    )(dh, ya, yb, gates, wa, wb, wout, *r_in)
    return (*outs[:5], list(outs[5:]))


def _pair_heads(a, axis):
    shp = a.shape
    a = a.reshape(shp[:axis] + (2, NB_HEADS // 2, HEAD_DIM) + shp[axis + 1:])
    return jnp.swapaxes(a, axis, axis + 1).reshape(shp)


def _unpair_heads(a, axis):
    shp = a.shape
    a = a.reshape(shp[:axis] + (NB_HEADS // 2, 2, HEAD_DIM) + shp[axis + 1:])
    return jnp.swapaxes(a, axis, axis + 1).reshape(shp)


FFN1 = ("ffn1_w_gate", "ffn1_w_up", "ffn1_w_down")
FFN2 = ("ffn2_w_gate", "ffn2_w_up", "ffn2_w_down")
MIXER = ("w_in", "w_branch_a", "w_branch_b", "w_out")
BRANCH = MIXER[1:]


def _layer_grads(x, target, g1, f1, gmix, late, rpb, sink, g2, gfin, comm=None):
    T = x.shape[0]
    tables = _rope_tables(T)
    bias = _na_bias_slabs(rpb)

    comm = comm or _Local(late)
    h1, n1, hdn1, p1, q1, gathered = _ffn_fwd(x, g1, *f1, name="ffn1_fwd", rider=comm.late_rider)
    w_in_t, wa, wb, wout, f2 = comm.late(gathered)
    w_in_p = jnp.concatenate([w_in_t[:O_QB], _pair_heads(w_in_t[O_QB:O_KB], 0), w_in_t[O_KB:]], axis=0)
    wb_p = _pair_heads(wb, 0)
    u, qkva, qb, kvb, gates = _mix_in_fwd(h1, gmix, w_in_p, tables)
    ya = _na_fwd(qkva, bias)
    yb = _swa_fwd(qb, kvb, sink)
    h2, merged = _merge_fwd(ya, yb, gates, wa, wb_p, wout, h1)
    dh3, n2, hdn2, p2, q2, loss, dgfin, _ = _ffn_fwd(h2, g2, *f2, name="ffn2_fwd", head=(gfin, target))

    dh2, da2, db2, dg2, _ = _ffn_bwd(dh3, h2, g2, p2, q2, *f2, name="ffn2_bwd")
    df2 = [_wgrad_shard_a(da2, n2, name="ffn2_dwg")[0], _wgrad_shard_a(db2, n2, name="ffn2_dwu")[0],
           _wgrad_shard_a(hdn2, dh3, scale=0.5, name="ffn2_dwd")[0]]
    red2 = comm.reduce(FFN2, df2, tag="ffn2")
    dya, dyb, dpa, dpb, dgates, got = _merge_bwd(dh2, ya, yb, gates, wa, wb_p, wout, rider=red2.sibling)
    red2.partial(got)
    dwout = _wgrad_cols(merged, dh2, 1, name="dwout").reshape(N_CHIPS, D_MODEL // N_CHIPS, D_MODEL)
    dwa = _wgrad_cols(ya, dpa, N_CHIPS, name="dwa")
    dwb = _unpair_heads(_wgrad_cols(yb, dpb, N_CHIPS, name="dwb"), 1)
    redb = comm.reduce(BRANCH, [dwa, dwb, dwout], tag="branch")
    dqa, dka, dva, dbias, got = _na_bwd(qkva, dya, bias, rider=_two_riders(red2.chips, redb.sibling))
    red2.halves(got[:len(FFN2)])
    redb.partial(got[len(FFN2):])
    drpb = _rpb_fold(dbias)
    dqb, dkb, dvb, dsink, got = _swa_bwd(qb, kvb, dyb, sink, rider=_two_riders(red2.share, redb.chips))
    out = red2.result(got[:len(FFN2)])
    redb.halves(got[len(FFN2):])
    dz, dh1, dgmix, got = _mix_in_bwd(dqa, dka, dva, dqb, dkb, dvb, dgates, h1, gmix, dh2, w_in_p, tables, rider=redb.share)
    out.update(redb.result(got))
    dwin_p = _wgrad_rows(dz, u, 2, name="dwin")[0].reshape(D_IN, D_MODEL)
    dwin = jnp.concatenate([dwin_p[:O_QB], _unpair_heads(dwin_p[O_QB:O_KB], 0), dwin_p[O_KB:]], axis=0)
    dx, da1, db1, dg1, _ = _ffn_bwd(dh1, x, g1, p1, q1, *f1, name="ffn1_bwd")
    redw = comm.reduce(("w_in",), [dwin.reshape(N_CHIPS, D_IN // N_CHIPS, D_MODEL)], tag="w_in")
    redw.partial(comm.update(FFN2, out, rider=redw.sibling))
    dwg1, got = _wgrad_shard_a(da1, n1, name="ffn1_dwg", rider=redw.chips)
    dwu1, got = _wgrad_shard_a(db1, n1, name="ffn1_dwu", rider=redw.halves(got).share)
    out.update(redw.result(got))
    red1 = comm.reduce(FFN1[:2], [dwg1, dwu1], tag="ffn1_gate_up")
    red1.partial(comm.update(("w_in",), out, rider=red1.sibling))
    dwd1, got = _wgrad_shard_a(hdn1, dh1, scale=0.5, name="ffn1_dwd", rider=red1.chips)
    red1.halves(got)
    redd = comm.reduce(FFN1[2:], [dwd1], tag="ffn1_down")
    got = comm.update(BRANCH[:2], out, rider=_two_riders(red1.share, redd.sibling))
    out.update(red1.result(got[:2]))
    redd.partial(got[2:])
    redd.halves(comm.update(FFN1[:2], out, rider=redd.chips))
    out.update(redd.result(comm.update(BRANCH[2:], out, rider=redd.share)))
    comm.update(FFN1[2:], out)
    out.update(loss=loss, dx=dx, ffn1_norm=dg1, mix_norm=dgmix, ffn2_norm=dg2, final_norm=dgfin, na_rpb=drpb,
               sink_logit=dsink[:, 0:2, 0].T.reshape(NB_HEADS))
    return out


class _Local:
    late_rider = None

    def __init__(self, late):
        self._late = late

    def late(self, gathered):
        return self._late

    def reduce(self, names, grads, *, tag):
        return _LocalReduce(names, grads)

    def update(self, names, reduced, rider=None):
        return []


class _LocalReduce:
    sibling = chips = share = None

    def __init__(self, names, grads):
        self._result = dict(zip(names, grads))

    def partial(self, got=None):
        return self

    halves = partial_now = halves_now = partial

    def result(self, got=None):
        return self._result

    result_now = result


ANY = pl.BlockSpec(memory_space=pl.ANY)


def _place():
    x, y, c = lax.axis_index("x"), lax.axis_index("y"), lax.axis_index("c")
    chips = [(1 - x, y), (x, 1 - y), (1 - x, 1 - y)]
    return x, y, c, 2 * x + y, chips


def _remote(src, dst, send_sems, recv_sems, k, device):
    return pltpu.make_async_remote_copy(src_ref=src, dst_ref=dst, send_sem=send_sems.at[k], recv_sem=recv_sems.at[k],
                                        device_id=device, device_id_type=MESH)


class _Rider:
    def __init__(self, inputs, out_shape, scratch, start, middle, finish):
        self.inputs, self.out_shape, self.scratch = list(inputs), list(out_shape), list(scratch)
        self.start, self.middle, self.finish = start, middle, finish


def _two_riders(first, second):
    if first is None and second is None:
        return None
    assert first.middle is None and second.middle is None
    n_in, n_out, n_sem = len(first.inputs), len(first.out_shape), len(first.scratch)

    def phase(name):
        def run(ins, outs, sems):
            getattr(first, name)(ins[:n_in], outs[:n_out], sems[:n_sem])
            getattr(second, name)(ins[n_in:], outs[n_out:], sems[n_sem:])
        return run

    return _Rider(first.inputs + second.inputs, first.out_shape + second.out_shape, first.scratch + second.scratch,
                  phase("start"), None, phase("finish"))


def _run_rider(rider, *, name):
    n_in, n_out = len(rider.inputs), len(rider.out_shape)

    def body(*refs):
        ins, outs, sems = refs[:n_in], refs[n_in:n_in + n_out], refs[n_in + n_out:]
        rider.start(ins, outs, sems)
        if rider.middle is not None:
            rider.middle(ins, outs, sems)
        rider.finish(ins, outs, sems)

    return pl.pallas_call(body, name=name, in_specs=[ANY] * n_in, out_specs=[ANY] * n_out, out_shape=rider.out_shape,
                          scratch_shapes=rider.scratch)(*rider.inputs)


def _ride(body, n_in, n_out, rider, grid, middle_step, prefetch=0):
    if rider is None:
        return body, [], [], [], [], []
    r_in, r_out = len(rider.inputs), len(rider.out_shape)
    steps = math.prod(grid)

    def riding(*refs):
        scalars, refs = refs[:prefetch], refs[prefetch:]
        ins, r_ins = refs[:n_in], refs[n_in:n_in + r_in]
        outs = refs[n_in + r_in:n_in + r_in + n_out]
        r_outs = refs[n_in + r_in + n_out:n_in + r_in + n_out + r_out]
        sems = refs[n_in + r_in + n_out + r_out:]
        step = pl.program_id(0)
        for axis in range(1, len(grid)):
            step = step * grid[axis] + pl.program_id(axis)

        @pl.when(step == 0)
        def _():
            rider.start(r_ins, r_outs, sems)

        body(*scalars, *ins, *outs)

        if rider.middle is not None:
            @pl.when(step == middle_step)
            def _():
                rider.middle(r_ins, r_outs, sems)

        @pl.when(step == steps - 1)
        def _():
            rider.finish(r_ins, r_outs, sems)

    return riding, rider.inputs, [ANY] * r_in, rider.out_shape, [ANY] * r_out, rider.scratch


def _gather_rider(shards):
    n = len(shards)

    def plan(ins, outs, sems, kinds):
        send_sems, recv_sems, own_send_sems, own_recv_sems = sems
        x, y, c, mine, chips = _place()
        sibling = (x, y, 1 - c)
        made = {k: [] for k in kinds}
        for i in range(n):
            hr = shards[i].shape[0] // 2
            if "own" in made:
                made["own"].append(_remote(ins[i], outs[i].at[mine], own_send_sems, own_recv_sems, i, sibling))
            for j, (cx, cy) in enumerate(chips):
                here = outs[i].at[2 * cx + cy, pl.ds(c * hr, hr)]
                there = outs[i].at[2 * cx + cy, pl.ds((1 - c) * hr, hr)]
                if "sends" in made:
                    made["sends"].append(_remote(ins[i].at[pl.ds(c * hr, hr)], outs[i].at[mine, pl.ds(c * hr, hr)],
                                                 send_sems, recv_sems, 6 * i + j, (cx, cy, c)))
                if "landed" in made:
                    made["landed"].append(_remote(here, here, send_sems, recv_sems, 6 * i + j, (cx, cy, c)))
                if "passes" in made:
                    made["passes"].append(_remote(here, here, send_sems, recv_sems, 6 * i + 3 + j, sibling))
                if "others" in made:
                    made["others"].append(_remote(there, there, send_sems, recv_sems, 6 * i + 3 + j, sibling))
        return [made[k] for k in kinds]

    def start(ins, outs, sems):
        own, sends = plan(ins, outs, sems, ("own", "sends"))
        for cp in own + sends:
            cp.start()

    def middle(ins, outs, sems):
        landed, passes = plan(ins, outs, sems, ("landed", "passes"))
        for arrived, cp in zip(landed, passes):
            arrived.wait_recv()
            cp.start()

    def finish(ins, outs, sems):
        own, sends, passes, others = plan(ins, outs, sems, ("own", "sends", "passes", "others"))
        for arrived in others:
            arrived.wait_recv()
        for cp in sends + passes:
            cp.wait_send()
        for cp in own:
            cp.wait()

    return _Rider(shards, [jax.ShapeDtypeStruct((N_CHIPS,) + s.shape, s.dtype) for s in shards],
                  [pltpu.SemaphoreType.DMA((6 * n,)), pltpu.SemaphoreType.DMA((6 * n,)),
                   pltpu.SemaphoreType.DMA((n,)), pltpu.SemaphoreType.DMA((n,))], start, middle, finish)


def _swap_rider(arrays, out_shape, source):
    n = len(arrays)

    def plan(ins, outs, sems):
        send_sems, recv_sems = sems
        x, y, c, _, _ = _place()
        return [_remote(source(ins[i], c, i), outs[i], send_sems, recv_sems, i, (x, y, 1 - c)) for i in range(n)]

    def start(ins, outs, sems):
        for cp in plan(ins, outs, sems):
            cp.start()

    def finish(ins, outs, sems):
        for cp in plan(ins, outs, sems):
            cp.wait()

    return _Rider(arrays, out_shape, [pltpu.SemaphoreType.DMA((n,)), pltpu.SemaphoreType.DMA((n,))], start, None, finish)


def _sibling_rider(grads):
    half = [g.shape[1] // 2 for g in grads]
    return _swap_rider(grads, [jax.ShapeDtypeStruct((g.shape[0], hr, g.shape[2]), g.dtype) for g, hr in zip(grads, half)],
                       lambda ref, c, i: ref.at[:, pl.ds((1 - c) * half[i], half[i])])


def _share_rider(halves):
    return _swap_rider(halves, [jax.ShapeDtypeStruct(h.shape, h.dtype) for h in halves], lambda ref, c, i: ref)


def _chips_rider(parts):
    n = len(parts)

    def plan(ins, outs, sems):
        send_sems, recv_sems = sems
        _, _, c, _, chips = _place()
        return [_remote(ins[i].at[2 * cx + cy], outs[i].at[j], send_sems, recv_sems, 3 * i + j, (cx, cy, c))
                for i in range(n) for j, (cx, cy) in enumerate(chips)]

    def start(ins, outs, sems):
        for cp in plan(ins, outs, sems):
            cp.start()

    def finish(ins, outs, sems):
        for cp in plan(ins, outs, sems):
            cp.wait()

    return _Rider(parts, [jax.ShapeDtypeStruct((N_CHIPS - 1,) + p.shape[1:], p.dtype) for p in parts],
                  [pltpu.SemaphoreType.DMA((3 * n,)), pltpu.SemaphoreType.DMA((3 * n,))], start, None, finish)


class _Reduce:
    def __init__(self, names, grads, cidx, chip, *, tag):
        self.names, self.grads, self.cidx, self.chip, self.tag = names, grads, cidx, chip, tag
        self.sibling = _sibling_rider(grads)

    def partial(self, from_sibling):
        self.from_sibling = from_sibling
        self.chips = _chips_rider([_add_sibling(g, r, self.cidx, name="add_sibling_" + k)
                                   for k, g, r in zip(self.names, self.grads, from_sibling)])
        return self

    def halves(self, from_chips):
        self.mine = [_add_chips(g, r1, r2, self.cidx, self.chip, name="add_chips_" + k)
                     for k, g, r1, r2 in zip(self.names, self.grads, self.from_sibling, from_chips)]
        self.share = _share_rider(self.mine)
        return self

    def result(self, others):
        return dict(zip(self.names, zip(self.mine, others)))

    def partial_now(self):
        return self.partial(_run_rider(self.sibling, name="rs_sibling_" + self.tag))

    def halves_now(self):
        return self.halves(_run_rider(self.chips, name="rs_chips_" + self.tag))

    def result_now(self):
        return self.result(_run_rider(self.share, name="rs_share_" + self.tag))


N_DEV = 8


def _small_allreduce(vec):
    R = vec.shape[0]

    def body(v_ref, o_ref, buf, send_sems, recv_sems):
        x, y, c, _, _ = _place()
        me = 4 * x + 2 * y + c
        buf[me] = v_ref[...]
        copies = []
        for k in range(1, N_DEV):
            peer = (x ^ (k >> 2), y ^ ((k >> 1) & 1), c ^ (k & 1))
            cp = _remote(v_ref, buf.at[me], send_sems, recv_sems, k - 1, peer)
            cp.start()
            copies.append(cp)
        for k, cp in enumerate(copies, start=1):
            cp.wait_send()
            landed = buf.at[me ^ k]
            _remote(landed, landed, send_sems, recv_sems, k - 1, (x, y, c)).wait_recv()
        acc = buf[0]
        for d in range(1, N_DEV):
            acc = acc + buf[d]
        o_ref[...] = acc

    return pl.pallas_call(
        body, name="small_allreduce",
        in_specs=[pl.BlockSpec(memory_space=pltpu.VMEM)], out_specs=pl.BlockSpec(memory_space=pltpu.VMEM),
        out_shape=jax.ShapeDtypeStruct(vec.shape, vec.dtype),
        scratch_shapes=[pltpu.VMEM((N_DEV, R, LANES), F32), pltpu.SemaphoreType.DMA((N_DEV - 1,)),
                        pltpu.SemaphoreType.DMA((N_DEV - 1,))],
    )(vec)


ELEMWISE_BLOCK = 512 * 1024


def _row_tile(rows, cols):
    best = None
    for t in range(8, rows + 1, 8):
        if rows % t == 0 and t * cols <= ELEMWISE_BLOCK:
            best = t
    return best if best is not None else rows


def _add_sibling(g, r1, cidx, *, name):
    S, R, C = g.shape
    hr = R // 2
    tr = _row_tile(hr, C)
    nt = hr // tr

    def body(c_ref, g_ref, r_ref, o_ref):
        o_ref[...] = (g_ref[...] + r_ref[...]).astype(BF16)

    blk = pl.BlockSpec((1, tr, C), lambda s, t, c: (s, t, 0))
    return pl.pallas_call(
        body, name=name,
        grid_spec=pltpu.PrefetchScalarGridSpec(
            num_scalar_prefetch=1, grid=(S, nt),
            in_specs=[pl.BlockSpec((1, tr, C), lambda s, t, c: (s, c[0] * nt + t, 0)), blk], out_specs=blk),
        out_shape=jax.ShapeDtypeStruct((S, hr, C), BF16),
        compiler_params=_params(2),
    )(cidx, g, r1)


def _add_chips(g, r1, r2, cidx, chip, *, name):
    _, R, C = g.shape
    hr = R // 2
    tr = _row_tile(hr, C)
    nt = hr // tr

    def body(pos_ref, g_ref, r1_ref, r2_ref, o_ref):
        own = g_ref[0] + r1_ref[0]
        o_ref[...] = ((own + r2_ref[0].astype(F32)) + r2_ref[1].astype(F32)) + r2_ref[2].astype(F32)

    pos = jnp.concatenate([cidx, chip])
    return pl.pallas_call(
        body, name=name,
        grid_spec=pltpu.PrefetchScalarGridSpec(
            num_scalar_prefetch=1, grid=(nt,),
            in_specs=[pl.BlockSpec((1, tr, C), lambda t, pos: (pos[1], pos[0] * nt + t, 0)),
                      pl.BlockSpec((1, tr, C), lambda t, pos: (pos[1], t, 0)),
                      pl.BlockSpec((N_CHIPS - 1, tr, C), lambda t, pos: (0, t, 0))],
            out_specs=pl.BlockSpec((tr, C), lambda t, pos: (t, 0))),
        out_shape=jax.ShapeDtypeStruct((hr, C), F32),
        compiler_params=_params(1),
    )(pos, g, r1, r2)


def _adamw_math(w, g, m, v):
    mn = ADAM_B1 * m + (1.0 - ADAM_B1) * g
    vn = ADAM_B2 * v + (1.0 - ADAM_B2) * (g * g)
    m_hat = mn / (1.0 - ADAM_B1 ** ADAM_STEP)
    v_hat = vn / (1.0 - ADAM_B2 ** ADAM_STEP)
    return -ADAM_LR * (m_hat / (jnp.sqrt(v_hat) + ADAM_EPS) + ADAM_WD * w), mn, vn


def _adamw_halves(ws, mines, others, ms, vs, cidx, *, name, rider=None):
    n = len(ws)
    R, C = ws[0].shape
    hr = R // 2
    tr = _row_tile(hr, C * n)
    nt = hr // tr

    def body(c_ref, *refs):
        for i in range(n):
            w_ref, a_ref, b_ref, m_ref, v_ref = (refs[j * n + i] for j in range(5))
            g_ref, d_ref, mo_ref, vo_ref = (refs[(5 + j) * n + i] for j in range(4))
            gv = jnp.where(pl.program_id(0) == c_ref[0], a_ref[...], b_ref[...])
            g_ref[...] = gv
            d_ref[...], mo_ref[...], vo_ref[...] = _adamw_math(w_ref[...], gv, m_ref[...], v_ref[...])

    full = pl.BlockSpec((tr, C), lambda h, t, c: (h * nt + t, 0))
    own = pl.BlockSpec((tr, C), lambda h, t, c: (jnp.where(h == c[0], t, 0), 0))
    sib = pl.BlockSpec((tr, C), lambda h, t, c: (jnp.where(h == c[0], 0, t), 0))
    shape = jax.ShapeDtypeStruct((R, C), F32)
    grid = (2, nt)
    body, r_in, r_in_specs, r_out, r_out_specs, scratch = _ride(body, 5 * n, 4 * n, rider, grid, None, prefetch=1)
    outs = pl.pallas_call(
        body, name=name,
        grid_spec=pltpu.PrefetchScalarGridSpec(
            num_scalar_prefetch=1, grid=grid,
            in_specs=[full] * n + [own] * n + [sib] * n + [full] * (2 * n) + r_in_specs,
            out_specs=[full] * (4 * n) + r_out_specs, scratch_shapes=scratch),
        out_shape=[shape] * (4 * n) + r_out,
        compiler_params=_params(2),
    )(cidx, *ws, *mines, *others, *ms, *vs, *r_in)
    return [list(outs[j * n:(j + 1) * n]) for j in range(4)], list(outs[4 * n:])


def _adamw_small(ws, gs, ms, vs):
    n = len(ws)

    def body(*refs):
        for i in range(n):
            w_ref, g_ref, m_ref, v_ref = (refs[j * n + i] for j in range(4))
            d_ref, mo_ref, vo_ref = (refs[(4 + j) * n + i] for j in range(3))
            d_ref[...], mo_ref[...], vo_ref[...] = _adamw_math(w_ref[...], g_ref[...], m_ref[...], v_ref[...])

    shapes = [jax.ShapeDtypeStruct(a.shape, F32) for a in ws]
    outs = pl.pallas_call(body, name="adamw_small", out_shape=shapes * 3, compiler_params=_params(0))(*ws, *gs, *ms, *vs)
    return outs[:n], outs[n:2 * n], outs[2 * n:]


def _unstack_cols(w):
    s, r, c = w.shape
    return w.transpose(1, 0, 2).reshape(r, s * c)


def _pad_rows(a, rows):
    return jnp.pad(a, ((0, rows - a.shape[0]), (0, LANES - a.shape[1])))


BIG = ("ffn1_w_gate", "ffn1_w_up", "ffn1_w_down", "w_in", "w_branch_a", "w_branch_b", "w_out",
       "ffn2_w_gate", "ffn2_w_up", "ffn2_w_down")
TRANSPOSED = ("ffn1_w_gate", "ffn1_w_up", "w_in", "ffn2_w_gate", "ffn2_w_up")
WEIGHTS = ("ffn1_norm", "ffn1_w_gate", "ffn1_w_up", "ffn1_w_down", "mix_norm", "w_in", "na_rpb", "sink_logit",
           "w_branch_a", "w_branch_b", "w_out", "ffn2_norm", "ffn2_w_gate", "ffn2_w_up", "ffn2_w_down", "final_norm")


def kernel(x, ffn1_norm, ffn1_w_gate, ffn1_w_up, ffn1_w_down, mix_norm, w_in, na_rpb, sink_logit, w_branch_a, w_branch_b, w_out, ffn2_norm, ffn2_w_gate, ffn2_w_up, ffn2_w_down, final_norm, loss_target, m_ffn1_norm, m_ffn1_w_gate, m_ffn1_w_up, m_ffn1_w_down, m_mix_norm, m_w_in, m_na_rpb, m_sink_logit, m_w_branch_a, m_w_branch_b, m_w_out, m_ffn2_norm, m_ffn2_w_gate, m_ffn2_w_up, m_ffn2_w_down, m_final_norm, v_ffn1_norm, v_ffn1_w_gate, v_ffn1_w_up, v_ffn1_w_down, v_mix_norm, v_w_in, v_na_rpb, v_sink_logit, v_w_branch_a, v_w_branch_b, v_w_out, v_ffn2_norm, v_ffn2_w_gate, v_ffn2_w_up, v_ffn2_w_down, v_final_norm):
    args = dict(locals())
    w = {k: args[k] for k in WEIGHTS}
    mom = {k: args["m_" + k] for k in WEIGHTS}
    var = {k: args["v_" + k] for k in WEIGHTS}
    cidx = lax.axis_index("c").astype(jnp.int32).reshape(1)
    chip = (2 * lax.axis_index("x") + lax.axis_index("y")).astype(jnp.int32).reshape(1)

    def shard(a, k):
        return jnp.swapaxes(a[0], 0, 1) if k in TRANSPOSED else a[0]

    def unshard(a, k):
        return (jnp.swapaxes(a, 0, 1) if k in TRANSPOSED else a)[None]

    def bf16_shards(names):
        return [shard(w[k], k).astype(BF16) for k in names]

    class comm:
        late_rider = _gather_rider(bf16_shards(MIXER + FFN2))

        @staticmethod
        def late(gathered):
            full = dict(zip(MIXER + FFN2, gathered))
            return (full["w_in"].reshape(D_IN, D_MODEL), _unstack_cols(full["w_branch_a"]), _unstack_cols(full["w_branch_b"]),
                    full["w_out"].reshape(D_MODEL, D_MODEL), tuple(full[k] for k in FFN2))

        @staticmethod
        def reduce(names, grads, *, tag):
            return _Reduce(names, grads, cidx, chip, tag=tag)

        @staticmethod
        def update(names, reduced, rider=None):
            res, got = _adamw_halves([shard(w[k], k) for k in names], [reduced[k][0] for k in names],
                                     [reduced[k][1] for k in names], [shard(mom[k], k) for k in names],
                                     [shard(var[k], k) for k in names], cidx, name="adamw_" + names[0], rider=rider)
            for i, k in enumerate(names):
                grads_out[k], deltas[k], new_m[k], new_v[k] = (unshard(a[i], k) for a in res)
            return got

    deltas, new_m, new_v, grads_out, grad = {}, {}, {}, {}, {}
    f1 = _run_rider(_gather_rider(bf16_shards(FFN1)), name="all_gather_ffn1")
    out = _layer_grads(x[0], loss_target[0], ffn1_norm, f1, mix_norm, None, na_rpb[0], sink_logit[0], ffn2_norm,
                       final_norm.reshape(1, D_MODEL), comm=comm)

    rows = D_MODEL // LANES
    small = jnp.concatenate([
        out["ffn1_norm"].reshape(rows, LANES), out["mix_norm"].reshape(rows, LANES), out["ffn2_norm"].reshape(rows, LANES),
        out["final_norm"].reshape(rows, LANES), out["na_rpb"].reshape(-1, LANES),
        _pad_rows(out["sink_logit"].reshape(1, NB_HEADS), 8), _pad_rows(out["loss"], 8)], axis=0)
    total = _small_allreduce(small)
    n_rpb = NA_HEADS * 2 * NA_KH
    grad["ffn1_norm"] = total[0:rows].reshape(1, D_MODEL)
    grad["mix_norm"] = total[rows:2 * rows].reshape(1, D_MODEL)
    grad["ffn2_norm"] = total[2 * rows:3 * rows].reshape(1, D_MODEL)
    grad["final_norm"] = total[3 * rows:4 * rows].reshape(1, D_MODEL)
    grad["na_rpb"] = total[4 * rows:4 * rows + n_rpb].reshape(NA_HEADS, 2 * NA_KH, LANES)[:, :2 * NA_KH - 1, :2 * NA_KW - 1]
    grad["na_rpb"] = grad["na_rpb"].reshape(NA_HEADS, -1)
    grad["sink_logit"] = total[4 * rows + n_rpb:4 * rows + n_rpb + 1, 0:NB_HEADS]
    loss = total[4 * rows + n_rpb + 8, 0]

    small_names = [k for k in WEIGHTS if k not in BIG]
    res = _adamw_small(*[[a[k].reshape(grad[k].shape) for k in small_names] for a in (w, grad, mom, var)])
    for i, k in enumerate(small_names):
        grads_out[k], deltas[k], new_m[k], new_v[k] = (a.reshape(w[k].shape) for a in (grad[k], res[0][i], res[1][i], res[2][i]))
    return (loss, out["dx"].reshape(x.shape), *[grads_out[k] for k in WEIGHTS], *[deltas[k] for k in WEIGHTS],
            *[new_m[k] for k in WEIGHTS], *[new_v[k] for k in WEIGHTS])
```

```python
import math

import jax
import jax.numpy as jnp
import numpy as np
from jax import lax
from jax.experimental import pallas as pl
from jax.experimental.pallas import tpu as pltpu

F32 = jnp.float32
BF16 = jnp.bfloat16

D_MODEL = 1024
HEAD_DIM = 64
NA_HEADS = 8
NB_HEADS = 8
GRID_W = 64
NA_KH = 8
NA_KW = 16
WIN = 128
ROPE_THETA = 10000.0
EPS = 1e-6
N_CHIPS = 4
QK_SCALE = HEAD_DIM ** -0.5
NEG = -1e30
LANES = 128
VMEM_LIMIT = 56 * 1024 * 1024
HEAD_ROWS = 256
WGRAD_TOKENS_BYTES = 8192

C_QKVA = 3 * NA_HEADS * HEAD_DIM
C_QB = NB_HEADS * HEAD_DIM
C_KB = 2 * HEAD_DIM
C_ROPE = C_QB + C_KB
C_GATES = 2 * D_MODEL
D_IN = C_QKVA + C_QB + 2 * C_KB + C_GATES
O_QB = C_QKVA
O_KB = O_QB + C_QB
O_VB = O_KB + C_KB
O_G = O_VB + C_KB

ADAM_LR = 0.001
ADAM_B1 = 0.9
ADAM_B2 = 0.999
ADAM_EPS = 1e-08
ADAM_WD = 0.01
ADAM_STEP = 10

MESH = pl.DeviceIdType.MESH


def _dot(a, b):
    return jnp.dot(a, b, preferred_element_type=F32)


def _dot_nt(a, b):
    return lax.dot_general(a, b, (((1,), (1,)), ((), ())), preferred_element_type=F32)


def _dot_tn(a, b):
    return lax.dot_general(a, b, (((0,), (0,)), ((), ())), preferred_element_type=F32)


def _params(n_axes):
    return pltpu.CompilerParams(dimension_semantics=("arbitrary",) * n_axes, vmem_limit_bytes=VMEM_LIMIT)


def _rstd(xf):
    return lax.rsqrt(jnp.mean(xf * xf, axis=-1, keepdims=True) + EPS)


def _norm_bwd(dn, xf, g, r):
    xhat = xf * r
    dxh = dn * g
    dx = r * (dxh - xhat * jnp.mean(dxh * xhat, axis=-1, keepdims=True))
    return dx, dn * xhat


def _sigmoid(x):
    return 0.5 * jnp.tanh(0.5 * x) + 0.5


def _loss_head(hf, gv, tgt):
    r = _rstd(hf)
    err = (hf * r) * gv - tgt
    dx, dgr = _norm_bwd(err * (1.0 / hf.shape[-1]), hf, gv, r)
    return 0.5 * jnp.mean(err * err, axis=-1, keepdims=True), dx, dgr


def _ffn_fwd(x, g, wg, wu, wd, *, name, tm=1024, sub=512, rider=None, head=None):
    T, D = x.shape
    F = wg.shape[1]
    tm = min(tm, T)
    sub = min(sub, tm)
    n_head = 0 if head is None else 2

    def body(*refs):
        x_ref, g_ref, wg_ref, wu_ref, wd_ref = refs[:5]
        h_ref, n_ref, hdn_ref, p_ref, q_ref = refs[5 + n_head:10 + n_head]
        i, s = pl.program_id(0), pl.program_id(1)
        _ffn_fwd_step(x_ref, g_ref, wg_ref, wu_ref, wd_ref, h_ref, n_ref, hdn_ref, p_ref, q_ref, s)
        if head is not None:
            gf_ref, t_ref = refs[5:7]
            loss_ref, dgf_ref = refs[10 + n_head:]

            @pl.when((i == 0) & (s == 0))
            def _():
                loss_ref[...] = jnp.zeros_like(loss_ref)
                dgf_ref[...] = jnp.zeros_like(dgf_ref)

            @pl.when(s == N_CHIPS - 1)
            def _():
                for u in range(tm // HEAD_ROWS):
                    r = pl.ds(u * HEAD_ROWS, HEAD_ROWS)
                    terms, dh, dgr = _loss_head(h_ref[r, :], gf_ref[...], t_ref[r, :])
                    loss_ref[...] += jnp.broadcast_to(jnp.sum(terms), loss_ref.shape)
                    dgf_ref[...] += jnp.sum(dgr, axis=0, keepdims=True)
                    h_ref[r, :] = dh

    def _ffn_fwd_step(x_ref, g_ref, wg_ref, wu_ref, wd_ref, h_ref, n_ref, hdn_ref, p_ref, q_ref, s):

        @pl.when(s == 0)
        def _():
            xf = x_ref[...]
            n_ref[...] = ((xf * _rstd(xf)) * g_ref[...]).astype(BF16)
            h_ref[...] = xf

        rows = [pl.ds(u * sub, sub) for u in range(tm // sub)]
        ab = [(_dot_nt(n_ref[r, :], wg_ref[0]), _dot_nt(n_ref[r, :], wu_ref[0])) for r in rows]
        hdns = []
        for r, (a, b) in zip(rows, ab):
            sg = _sigmoid(a)
            silu = a * sg
            hdn = (silu * b).astype(BF16)
            hdn_ref[0, r, :] = hdn
            p_ref[0, r, :] = (b * (sg + silu * (1.0 - sg))).astype(BF16)
            q_ref[0, r, :] = silu.astype(BF16)
            hdns.append(hdn)
        for r, hdn in zip(rows, hdns):
            h_ref[r, :] += 0.5 * _dot(hdn, wd_ref[0])

    tok = pl.BlockSpec((tm, D), lambda i, s: (i, 0))
    hid = pl.BlockSpec((1, tm, F), lambda i, s: (s, i, 0))
    wspec = pl.BlockSpec((1, F, D), lambda i, s: (s, 0, 0))
    hshape = jax.ShapeDtypeStruct((N_CHIPS, T, F), BF16)
    grid = (T // tm, N_CHIPS)
    vec = pl.BlockSpec((1, D), lambda i, s: (0, 0))
    head_in, head_in_specs, head_out, head_out_specs = [], [], [], []
    if head is not None:
        head_in, head_in_specs = list(head), [vec, tok]
        head_out = [jax.ShapeDtypeStruct((1, LANES), F32), jax.ShapeDtypeStruct((1, D), F32)]
        head_out_specs = [pl.BlockSpec((1, LANES), lambda i, s: (0, 0)), vec]
    n_main = 5 + n_head
    body, r_in, r_in_specs, r_out, r_out_specs, scratch = _ride(body, n_main, n_main, rider, grid, (grid[0] * grid[1] * 7) // 8)
    outs = pl.pallas_call(
        body, name=name, grid=grid,
        in_specs=[tok, vec, wspec, wspec, wspec] + head_in_specs + r_in_specs,
        out_specs=[tok, tok, hid, hid, hid] + head_out_specs + r_out_specs,
        out_shape=[jax.ShapeDtypeStruct((T, D), F32), jax.ShapeDtypeStruct((T, D), BF16), hshape, hshape, hshape]
        + head_out + r_out,
        scratch_shapes=scratch,
        compiler_params=_params(2),
    )(x, g, wg, wu, wd, *head_in, *r_in)
    return (*outs[:n_main], list(outs[n_main:]))


def _ffn_bwd(dh, x, g, p, q, wg, wu, wd, *, name, tm=1024, sub=256, rider=None):
    T, D = x.shape
    F = wg.shape[1]
    tm = min(tm, T)
    sub = min(sub, tm)

    def body(dh_ref, x_ref, g_ref, p_ref, q_ref, wg_ref, wu_ref, wd_ref, dx_ref, da_ref, db_ref, dg_ref):
        i, s = pl.program_id(0), pl.program_id(1)

        @pl.when((i == 0) & (s == 0))
        def _():
            dg_ref[...] = jnp.zeros_like(dg_ref)

        @pl.when(s == 0)
        def _():
            dx_ref[...] = jnp.zeros_like(dx_ref)

        rows = [pl.ds(u * sub, sub) for u in range(tm // sub)]
        dhdn = [_dot_nt((0.5 * dh_ref[r, :]).astype(BF16), wd_ref[0]) for r in rows]
        das, dbs = [], []
        for r, dd in zip(rows, dhdn):
            da = (dd * p_ref[0, r, :].astype(F32)).astype(BF16)
            db = (dd * q_ref[0, r, :].astype(F32)).astype(BF16)
            da_ref[0, r, :] = da
            db_ref[0, r, :] = db
            das.append(da)
            dbs.append(db)
        for r, da, db in zip(rows, das, dbs):
            dx_ref[r, :] += _dot(da, wg_ref[0]) + _dot(db, wu_ref[0])

        @pl.when(s == N_CHIPS - 1)
        def _():
            xf = x_ref[...]
            dx, dgr = _norm_bwd(dx_ref[...], xf, g_ref[...], _rstd(xf))
            dg_ref[...] += jnp.sum(dgr, axis=0, keepdims=True)
            dx_ref[...] = dh_ref[...] + dx

    tok = pl.BlockSpec((tm, D), lambda i, s: (i, 0))
    hid = pl.BlockSpec((1, tm, F), lambda i, s: (s, i, 0))
    vec = pl.BlockSpec((1, D), lambda i, s: (0, 0))
    hshape = jax.ShapeDtypeStruct((N_CHIPS, T, F), BF16)
    wspec = pl.BlockSpec((1, F, D), lambda i, s: (s, 0, 0))
    grid = (T // tm, N_CHIPS)
    body, r_in, r_in_specs, r_out, r_out_specs, scratch = _ride(body, 8, 4, rider, grid, None)
    outs = pl.pallas_call(
        body, name=name, grid=grid,
        in_specs=[tok, tok, vec, hid, hid, wspec, wspec, wspec] + r_in_specs,
        out_specs=[tok, hid, hid, vec] + r_out_specs,
        out_shape=[jax.ShapeDtypeStruct((T, D), F32), hshape, hshape, jax.ShapeDtypeStruct((1, D), F32)] + r_out,
        scratch_shapes=scratch,
        compiler_params=_params(2),
    )(dh, x, g, p, q, wg, wu, wd, *r_in)
    return (*outs[:4], list(outs[4:]))


def _wgrad(a, b, *, a_block, a_map, b_block, b_map, out_shape, o_block, o_map, grid, scale=1.0, name, rider=None):
    def body(a_ref, b_ref, o_ref):
        @pl.when(pl.program_id(len(grid) - 1) == 0)
        def _():
            o_ref[...] = jnp.zeros_like(o_ref)

        av = a_ref[...]
        bv = b_ref[...]
        av = av.reshape(av.shape[-2:]).astype(BF16)
        bv = bv.reshape(bv.shape[-2:])
        if scale != 1.0:
            bv = scale * bv
        o_ref[...] += _dot_tn(av, bv.astype(BF16)).reshape(o_ref.shape)

    body, r_in, r_in_specs, r_out, r_out_specs, scratch = _ride(body, 2, 1, rider, grid, None)
    outs = pl.pallas_call(
        body, name=name, grid=grid,
        in_specs=[pl.BlockSpec(a_block, a_map), pl.BlockSpec(b_block, b_map)] + r_in_specs,
        out_specs=[pl.BlockSpec(o_block, o_map)] + r_out_specs,
        out_shape=[jax.ShapeDtypeStruct(out_shape, F32)] + r_out,
        scratch_shapes=scratch,
        compiler_params=_params(len(grid)),
    )(a, b, *r_in)
    return outs[0], list(outs[1:])


def _wgrad_rows(a, b, n_blocks, *, name, tk=2048):
    T, N = b.shape
    M = a.shape[1] // n_blocks
    tk = min(tk, T)
    return _wgrad(a, b, a_block=(tk, M), a_map=lambda s, k: (k, s), b_block=(tk, N), b_map=lambda s, k: (k, 0),
                  out_shape=(n_blocks, M, N), o_block=(1, M, N), o_map=lambda s, k: (s, 0, 0), grid=(n_blocks, T // tk), name=name)


def _wgrad_shard_a(a, b, *, name, scale=1.0, rider=None):
    S, T, M = a.shape
    N = b.shape[1]
    tk = min(WGRAD_TOKENS_BYTES // b.dtype.itemsize, T)
    return _wgrad(a, b, a_block=(1, tk, M), a_map=lambda s, k: (s, k, 0), b_block=(tk, N), b_map=lambda s, k: (k, 0),
                  out_shape=(S, M, N), o_block=(1, M, N), o_map=lambda s, k: (s, 0, 0), grid=(S, T // tk), scale=scale,
                  name=name, rider=rider)


def _wgrad_cols(a, b, n_blocks, *, name, tk=2048):
    T, M = a.shape
    N = b.shape[1] // n_blocks
    tk = min(tk, T)

    def body(a_ref, b_ref, o_ref):
        @pl.when(pl.program_id(0) == 0)
        def _():
            o_ref[...] = jnp.zeros_like(o_ref)

        r = _dot_tn(a_ref[...].astype(BF16), b_ref[...].astype(BF16))
        for s in range(n_blocks):
            o_ref[s] += r[:, s * N:(s + 1) * N]

    return pl.pallas_call(
        body, name=name, grid=(T // tk,),
        in_specs=[pl.BlockSpec((tk, M), lambda k: (k, 0)), pl.BlockSpec((tk, n_blocks * N), lambda k: (k, 0))],
        out_specs=pl.BlockSpec((n_blocks, M, N), lambda k: (0, 0, 0)),
        out_shape=jax.ShapeDtypeStruct((n_blocks, M, N), F32),
        compiler_params=_params(1),
    )(a, b)


def _rope_tables(T):
    half = HEAD_DIM // 2
    inv = np.float32(ROPE_THETA) ** (-np.arange(half, dtype=np.float32) / np.float32(half))
    ang = np.arange(T, dtype=np.float32)[:, None] * inv[None, :]
    cos, sin, zero = np.cos(ang), np.sin(ang), np.zeros_like(ang)
    reps = LANES // HEAD_DIM
    return (jnp.asarray(np.tile(np.concatenate([cos, cos], axis=1), (1, reps))),
            jnp.asarray(np.tile(np.concatenate([-sin, zero], axis=1), (1, reps))),
            jnp.asarray(np.tile(np.concatenate([zero, sin], axis=1), (1, reps))))


def _rope(x, cos, sa, sb, sign):
    half = HEAD_DIM // 2
    return x * cos + sign * (pltpu.roll(x, LANES - half, 1) * sa + pltpu.roll(x, half, 1) * sb)


def _mix_in_fwd(h, g, w_in, tables, *, tm=512):
    T, D = h.shape

    def body(h_ref, g_ref, w_ref, cos_ref, sa_ref, sb_ref, u_ref, qkva_ref, qb_ref, kvb_ref, gates_ref):
        hf = h_ref[...]
        u = ((hf * _rstd(hf)) * g_ref[...]).astype(BF16)
        u_ref[...] = u
        qkva_ref[...] = _dot_nt(u, w_ref[0:C_QKVA, :]).astype(BF16)
        zr = _dot_nt(u, w_ref[O_QB:O_QB + C_ROPE, :])
        cos, sa, sb = cos_ref[...], sa_ref[...], sb_ref[...]
        for j in range(C_ROPE // LANES):
            rj = _rope(zr[:, j * LANES:(j + 1) * LANES], cos, sa, sb, 1.0).astype(BF16)
            if j < C_QB // LANES:
                qb_ref[:, j * LANES:(j + 1) * LANES] = rj
            else:
                kvb_ref[:, 0:C_KB] = rj
        kvb_ref[:, C_KB:2 * C_KB] = _dot_nt(u, w_ref[O_VB:O_VB + C_KB, :]).astype(BF16)
        gates_ref[...] = _dot_nt(u, w_ref[O_G:O_G + C_GATES, :])

    def tok(n):
        return pl.BlockSpec((tm, n), lambda i: (i, 0))

    return pl.pallas_call(
        body, name="mix_in_fwd", grid=(T // tm,),
        in_specs=[tok(D), pl.BlockSpec((1, D), lambda i: (0, 0)), pl.BlockSpec((D_IN, D), lambda i: (0, 0), pipeline_mode=pl.Buffered(1)),
                  tok(LANES), tok(LANES), tok(LANES)],
        out_specs=[tok(D), tok(C_QKVA), tok(C_QB), tok(2 * C_KB), tok(C_GATES)],
        out_shape=[jax.ShapeDtypeStruct((T, D), BF16), jax.ShapeDtypeStruct((T, C_QKVA), BF16),
                   jax.ShapeDtypeStruct((T, C_QB), BF16), jax.ShapeDtypeStruct((T, 2 * C_KB), BF16),
                   jax.ShapeDtypeStruct((T, C_GATES), F32)],
        compiler_params=_params(1),
    )(h, g, w_in, *tables)


def _mix_in_bwd(dqa, dka, dva, dqb, dkb, dvb, dgates, h, g, dres, w_in, tables, *, tm=512, rider=None):
    T, D = h.shape

    def body(dqa_ref, dka_ref, dva_ref, dqb_ref, dkb_ref, dvb_ref, dgt_ref, h_ref, g_ref, dres_ref, w_ref,
             cos_ref, sa_ref, sb_ref, dz_ref, dh_ref, dg_ref):
        @pl.when(pl.program_id(0) == 0)
        def _():
            dg_ref[...] = jnp.zeros_like(dg_ref)

        na = NA_HEADS * HEAD_DIM
        dz_ref[:, 0:na] = dqa_ref[...].astype(BF16)
        dz_ref[:, na:2 * na] = dka_ref[...].astype(BF16)
        dz_ref[:, 2 * na:3 * na] = dva_ref[...].astype(BF16)
        cos, sa, sb = cos_ref[...], sa_ref[...], sb_ref[...]
        for j in range(C_QB // LANES):
            dz_ref[:, O_QB + j * LANES:O_QB + (j + 1) * LANES] = _rope(
                dqb_ref[:, j * LANES:(j + 1) * LANES], cos, sa, sb, -1.0).astype(BF16)
        dz_ref[:, O_KB:O_KB + C_KB] = _rope(dkb_ref[...], cos, sa, sb, -1.0).astype(BF16)
        dz_ref[:, O_VB:O_VB + C_KB] = dvb_ref[...].astype(BF16)
        dz_ref[:, O_G:O_G + C_GATES] = dgt_ref[...].astype(BF16)
        du = _dot(dz_ref[...], w_ref[...])
        hf = h_ref[...]
        dx, dgr = _norm_bwd(du, hf, g_ref[...], _rstd(hf))
        dg_ref[...] += jnp.sum(dgr, axis=0, keepdims=True)
        dh_ref[...] = dres_ref[...] + dx

    def tok(n):
        return pl.BlockSpec((tm, n), lambda i: (i, 0))

    vec = pl.BlockSpec((1, D), lambda i: (0, 0))
    na = NA_HEADS * HEAD_DIM
    grid = (T // tm,)
    body, r_in, r_in_specs, r_out, r_out_specs, scratch = _ride(body, 14, 3, rider, grid, None)
    outs = pl.pallas_call(
        body, name="mix_in_bwd", grid=grid,
        in_specs=[tok(na), tok(na), tok(na), tok(C_QB), tok(C_KB), tok(C_KB), tok(C_GATES), tok(D), vec, tok(D),
                  pl.BlockSpec((D_IN, D), lambda i: (0, 0), pipeline_mode=pl.Buffered(1)), tok(LANES), tok(LANES), tok(LANES)]
        + r_in_specs,
        out_specs=[tok(D_IN), tok(D), vec] + r_out_specs,
        out_shape=[jax.ShapeDtypeStruct((T, D_IN), BF16), jax.ShapeDtypeStruct((T, D), F32),
                   jax.ShapeDtypeStruct((1, D), F32)] + r_out,
        scratch_shapes=scratch,
        compiler_params=_params(1),
    )(dqa, dka, dva, dqb, dkb, dvb, dgates, h, g, dres, w_in, *tables, *r_in)
    return (*outs[:3], list(outs[3:]))


def _na_bias_slabs(rpb):
    H = rpb.shape[0]
    ncell = GRID_W * GRID_W
    cell = np.arange(ncell)
    co = cell % GRID_W - cell // GRID_W + (NA_KW - 1)
    e_co = jnp.asarray((np.arange(LANES)[:, None] == co[None, :]).astype(np.float32))
    table = jnp.pad(rpb, ((0, 0), (0, 1), (0, LANES - rpb.shape[2]))).reshape(H * 2 * NA_KH, LANES)

    def body(t_ref, e_ref, o_ref):
        o_ref[...] = jnp.dot(t_ref[...], e_ref[...], preferred_element_type=F32, precision=lax.Precision.HIGHEST)

    toeplitz = pl.pallas_call(
        body, name="rpb_unfold", out_shape=jax.ShapeDtypeStruct((H * 2 * NA_KH, ncell), F32),
        compiler_params=_params(0),
    )(table, e_co).reshape(H, 2 * NA_KH, GRID_W, GRID_W)

    def assemble(tz_ref, o_ref):
        c = lax.broadcasted_iota(jnp.int32, (GRID_W, GRID_W), 0)
        k = lax.broadcasted_iota(jnp.int32, (GRID_W, GRID_W), 1)
        cs = jnp.clip(c - NA_KW // 2, 0, GRID_W - NA_KW)
        inwin = (k >= cs) & (k < cs + NA_KW)
        for ro0 in range(NA_KH):
            for hh in range(2):
                for i in range(NA_KH):
                    o_ref[0, ro0, hh * GRID_W:(hh + 1) * GRID_W, i * GRID_W:(i + 1) * GRID_W] = jnp.where(
                        inwin, tz_ref[hh, ro0 + i], NEG)

    return pl.pallas_call(
        assemble, name="na_bias_slabs", grid=(H // 2,),
        in_specs=[pl.BlockSpec((2, 2 * NA_KH, GRID_W, GRID_W), lambda p: (p, 0, 0, 0))],
        out_specs=pl.BlockSpec((1, NA_KH, 2 * GRID_W, NA_KH * GRID_W), lambda p: (p, 0, 0, 0)),
        out_shape=jax.ShapeDtypeStruct((H // 2, NA_KH, 2 * GRID_W, NA_KH * GRID_W), F32),
        compiler_params=_params(1),
    )(toeplitz)


def _half_masks(rows):
    lane = lax.broadcasted_iota(jnp.int32, (rows, LANES), 1)
    left = lane < HEAD_DIM
    return left, (left, jnp.logical_not(left))


def _stack_heads(x):
    left, halves = _half_masks(x.shape[0])
    xf = x.astype(F32)
    return jnp.concatenate([jnp.where(m, xf, 0.0).astype(BF16) for m in halves], axis=0)


def _unstack_heads(o):
    rows = o.shape[0] // 2
    left, _ = _half_masks(rows)
    return jnp.where(left, o[:rows], o[rows:])


def _na_row(j, t, rb, rows):
    r = j * rb + t
    rs = jnp.clip(r - NA_KH // 2, 0, rows - NA_KH)
    return pl.multiple_of(t * GRID_W, GRID_W), pl.multiple_of(rs * GRID_W, GRID_W), rs - r + (NA_KH - 1)


def _na_specs(T, rb):
    qrows = GRID_W * rb
    pairs = NA_HEADS // 2
    return ([pl.BlockSpec((qrows, LANES), lambda p, j: (j, p)),
             pl.BlockSpec((T, LANES), lambda p, j: (0, pairs + p)),
             pl.BlockSpec((T, LANES), lambda p, j: (0, 2 * pairs + p))],
            pl.BlockSpec((1, NA_KH, 2 * GRID_W, NA_KH * GRID_W), lambda p, j: (p, 0, 0, 0)))


def _softmax(s):
    p = jnp.exp(s - jnp.max(s, axis=-1, keepdims=True))
    return p / jnp.sum(p, axis=-1, keepdims=True)


def _na_fwd(qkva, bias, *, rb=16, group=16):
    T = qkva.shape[0]
    rows = T // GRID_W
    nkeys = NA_KH * GRID_W
    rb = min(rb, rows)
    group = min(group, rb)

    def body(q_ref, k_ref, v_ref, bias_ref, y_ref):
        j = pl.program_id(1)

        def rows_step(t, carry):
            at = [_na_row(j, t * group + u, rb, rows) for u in range(group)]
            s = [_dot_nt(_stack_heads(q_ref[pl.ds(q0, GRID_W), :]), k_ref[pl.ds(k0, nkeys), :]) for q0, k0, _ in at]
            p = [_softmax(su * QK_SCALE + bias_ref[0, ro0]) for su, (_, _, ro0) in zip(s, at)]
            o = [_dot(pu.astype(BF16), v_ref[pl.ds(k0, nkeys), :]) for pu, (_, k0, _) in zip(p, at)]
            for ou, (q0, _, _) in zip(o, at):
                y_ref[pl.ds(q0, GRID_W), :] = _unstack_heads(ou).astype(BF16)
            return carry

        lax.fori_loop(0, rb // group, rows_step, 0)

    qkv_specs, bias_spec = _na_specs(T, rb)
    return pl.pallas_call(
        body, name="na_fwd", grid=(NA_HEADS // 2, rows // rb),
        in_specs=qkv_specs + [bias_spec],
        out_specs=qkv_specs[0],
        out_shape=jax.ShapeDtypeStruct((T, NA_HEADS * HEAD_DIM), BF16),
        compiler_params=_params(2),
    )(qkva, qkva, qkva, bias)


def _na_bwd(qkva, dy, bias, *, rb=8, group=8, rider=None):
    T = qkva.shape[0]
    rows = T // GRID_W
    nkeys = NA_KH * GRID_W

    def body(q_ref, k_ref, v_ref, dy_ref, bias_ref, dq_ref, dk_ref, dv_ref, dbias_ref):
        j = pl.program_id(1)

        @pl.when(j == 0)
        def _():
            dk_ref[...] = jnp.zeros_like(dk_ref)
            dv_ref[...] = jnp.zeros_like(dv_ref)
            dbias_ref[...] = jnp.zeros_like(dbias_ref)

        def rows_step(t, carry):
            at = [_na_row(j, t * group + u, rb, rows) for u in range(group)]
            qs = [_stack_heads(q_ref[pl.ds(q0, GRID_W), :]) for q0, _, _ in at]
            dys = [_stack_heads(dy_ref[pl.ds(q0, GRID_W), :]) for q0, _, _ in at]
            s = [_dot_nt(qu, k_ref[pl.ds(k0, nkeys), :]) for qu, (_, k0, _) in zip(qs, at)]
            dp = [_dot_nt(du, v_ref[pl.ds(k0, nkeys), :]) for du, (_, k0, _) in zip(dys, at)]
            p = [_softmax(su * QK_SCALE + bias_ref[0, ro0]) for su, (_, _, ro0) in zip(s, at)]
            ds = [pu * (du - jnp.sum(pu * du, axis=-1, keepdims=True)) for pu, du in zip(p, dp)]
            for u, (q0, k0, ro0) in enumerate(at):
                dbias_ref[0, ro0] += ds[u]
                dsb = ds[u].astype(BF16)
                dq_ref[pl.ds(q0, GRID_W), :] = (_unstack_heads(_dot(dsb, k_ref[pl.ds(k0, nkeys), :])) * QK_SCALE).astype(BF16)
                dk_ref[pl.ds(k0, nkeys), :] += _dot_tn(dsb, qs[u]) * QK_SCALE
                dv_ref[pl.ds(k0, nkeys), :] += _dot_tn(p[u].astype(BF16), dys[u])
            return carry

        lax.fori_loop(0, rb // group, rows_step, 0)

    qkv_specs, bias_spec = _na_specs(T, rb)
    width = NA_HEADS * HEAD_DIM
    kv_out = pl.BlockSpec((T, LANES), lambda p, j: (0, p))
    grid = (NA_HEADS // 2, rows // rb)
    body, r_in, r_in_specs, r_out, r_out_specs, scratch = _ride(body, 5, 4, rider, grid, None)
    outs = pl.pallas_call(
        body, name="na_bwd", grid=grid,
        in_specs=qkv_specs + [qkv_specs[0], bias_spec] + r_in_specs,
        out_specs=[qkv_specs[0], kv_out, kv_out, bias_spec] + r_out_specs,
        out_shape=[jax.ShapeDtypeStruct((T, width), BF16), jax.ShapeDtypeStruct((T, width), F32),
                   jax.ShapeDtypeStruct((T, width), F32), jax.ShapeDtypeStruct(bias.shape, F32)] + r_out,
        scratch_shapes=scratch,
        compiler_params=_params(2),
    )(qkva, qkva, qkva, dy, bias, *r_in)
    return (*outs[:4], list(outs[4:]))


def _rpb_fold(dslab):
    pairs = dslab.shape[0]
    H = 2 * pairs
    ncell = GRID_W * GRID_W

    def disassemble(d_ref, tz_ref):
        tz_ref[...] = jnp.zeros_like(tz_ref)
        for ro0 in range(NA_KH):
            for hh in range(2):
                for i in range(NA_KH):
                    tz_ref[hh, ro0 + i] += d_ref[0, ro0, hh * GRID_W:(hh + 1) * GRID_W, i * GRID_W:(i + 1) * GRID_W]

    dtoeplitz = pl.pallas_call(
        disassemble, name="rpb_fold_tiles", grid=(pairs,),
        in_specs=[pl.BlockSpec((1, NA_KH, 2 * GRID_W, NA_KH * GRID_W), lambda p: (p, 0, 0, 0))],
        out_specs=pl.BlockSpec((2, 2 * NA_KH, GRID_W, GRID_W), lambda p: (p, 0, 0, 0)),
        out_shape=jax.ShapeDtypeStruct((H, 2 * NA_KH, GRID_W, GRID_W), F32),
        compiler_params=_params(1),
    )(dslab).reshape(H * 2 * NA_KH, ncell)
    cell = np.arange(ncell)
    co = cell % GRID_W - cell // GRID_W + (NA_KW - 1)
    e_co = jnp.asarray((co[:, None] == np.arange(LANES)[None, :]).astype(np.float32))

    def diagonals(x_ref, e_ref, o_ref):
        o_ref[...] = jnp.dot(x_ref[...], e_ref[...], preferred_element_type=F32, precision=lax.Precision.HIGHEST)

    return pl.pallas_call(
        diagonals, name="rpb_fold", out_shape=jax.ShapeDtypeStruct((H * 2 * NA_KH, LANES), F32),
        compiler_params=_params(0),
    )(dtoeplitz, e_co).reshape(H, 2 * NA_KH, LANES)


SWA_KEYS = 3 * WIN


def _swa_block(j, t, qbn, T):
    blk = j * qbn + t
    start = jnp.clip((blk - 1) * WIN, 0, T - SWA_KEYS)
    row = lax.broadcasted_iota(jnp.int32, (2 * WIN, SWA_KEYS), 0)
    qpos = blk * WIN + jnp.where(row < WIN, row, row - WIN)
    kpos = start + lax.broadcasted_iota(jnp.int32, (2 * WIN, SWA_KEYS), 1)
    return pl.multiple_of(t * WIN, WIN), pl.multiple_of(start, WIN), jnp.abs(qpos - kpos) <= WIN


def _swa_sinks(sink_ref, p):
    row = lax.broadcasted_iota(jnp.int32, (2 * WIN, 1), 0)
    return jnp.where(row < WIN, sink_ref[p], sink_ref[p + NB_HEADS // 2])


def _swa_probs(s, mask, sink):
    s = jnp.where(mask, s * QK_SCALE, NEG)
    m = jnp.maximum(jnp.max(s, axis=-1, keepdims=True), sink)
    e = jnp.exp(s - m)
    esink = jnp.exp(sink - m)
    den = jnp.sum(e, axis=-1, keepdims=True) + esink
    return e / den, esink / den


def _swa_specs(T, qbn):
    return [pl.BlockSpec(memory_space=pltpu.SMEM),
            pl.BlockSpec((WIN * qbn, LANES), lambda p, j: (j, p)),
            pl.BlockSpec((T, LANES), lambda p, j: (0, 0)),
            pl.BlockSpec((T, LANES), lambda p, j: (0, 1))]


def _swa_fwd(qb, kvb, sink, *, qbn=16, group=16):
    T = qb.shape[0]
    pairs = NB_HEADS // 2
    qbn = min(qbn, T // WIN)
    group = min(group, qbn)

    def body(sink_ref, q_ref, k_ref, v_ref, y_ref):
        p, j = pl.program_id(0), pl.program_id(1)
        sinks = _swa_sinks(sink_ref, p)

        def blocks_step(t, carry):
            at = [_swa_block(j, t * group + u, qbn, T) for u in range(group)]
            s = [_dot_nt(_stack_heads(q_ref[pl.ds(q0, WIN), :]), k_ref[pl.ds(k0, SWA_KEYS), :]) for q0, k0, _ in at]
            pr = [_swa_probs(su, mask, sinks)[0] for su, (_, _, mask) in zip(s, at)]
            o = [_dot(pu.astype(BF16), v_ref[pl.ds(k0, SWA_KEYS), :]) for pu, (_, k0, _) in zip(pr, at)]
            for ou, (q0, _, _) in zip(o, at):
                y_ref[pl.ds(q0, WIN), :] = _unstack_heads(ou).astype(BF16)
            return carry

        lax.fori_loop(0, qbn // group, blocks_step, 0)

    specs = _swa_specs(T, qbn)
    return pl.pallas_call(
        body, name="swa_fwd", grid=(pairs, T // (WIN * qbn)),
        in_specs=specs, out_specs=specs[1],
        out_shape=jax.ShapeDtypeStruct((T, NB_HEADS * HEAD_DIM), BF16),
        compiler_params=_params(2),
    )(sink, qb, kvb, kvb)


def _swa_bwd(qb, kvb, dy, sink, *, qbn=8, group=8, rider=None):
    T = qb.shape[0]
    pairs = NB_HEADS // 2
    qbn = min(qbn, T // WIN)
    group = min(group, qbn)

    def body(sink_ref, q_ref, k_ref, v_ref, dy_ref, dq_ref, dk_ref, dv_ref, dsink_ref):
        p, j = pl.program_id(0), pl.program_id(1)
        sinks = _swa_sinks(sink_ref, p)

        @pl.when((p == 0) & (j == 0))
        def _():
            dk_ref[...] = jnp.zeros_like(dk_ref)
            dv_ref[...] = jnp.zeros_like(dv_ref)

        @pl.when(j == 0)
        def _():
            dsink_ref[...] = jnp.zeros_like(dsink_ref)

        def blocks_step(t, carry):
            at = [_swa_block(j, t * group + u, qbn, T) for u in range(group)]
            qs = [_stack_heads(q_ref[pl.ds(q0, WIN), :]) for q0, _, _ in at]
            dys = [_stack_heads(dy_ref[pl.ds(q0, WIN), :]) for q0, _, _ in at]
            s = [_dot_nt(qu, k_ref[pl.ds(k0, SWA_KEYS), :]) for qu, (_, k0, _) in zip(qs, at)]
            dp = [_dot_nt(du, v_ref[pl.ds(k0, SWA_KEYS), :]) for du, (_, k0, _) in zip(dys, at)]
            probs = [_swa_probs(su, mask, sinks) for su, (_, _, mask) in zip(s, at)]
            for u, (q0, k0, _) in enumerate(at):
                pr, psink = probs[u]
                delta = jnp.sum(pr * dp[u], axis=-1, keepdims=True)
                dsb = (pr * (dp[u] - delta)).astype(BF16)
                dsk = psink * delta
                for hh in range(2):
                    dsink_ref[0, hh:hh + 1, :] += jnp.broadcast_to(-jnp.sum(dsk[hh * WIN:(hh + 1) * WIN]), (1, LANES))
                dq_ref[pl.ds(q0, WIN), :] = _unstack_heads(_dot(dsb, k_ref[pl.ds(k0, SWA_KEYS), :])) * QK_SCALE
                dk_ref[pl.ds(k0, SWA_KEYS), :] += _dot_tn(dsb, qs[u]) * QK_SCALE
                dv_ref[pl.ds(k0, SWA_KEYS), :] += _dot_tn(pr.astype(BF16), dys[u])
            return carry

        lax.fori_loop(0, qbn // group, blocks_step, 0)

    specs = _swa_specs(T, qbn)
    kv_out = pl.BlockSpec((T, LANES), lambda p, j: (0, 0))
    grid = (pairs, T // (WIN * qbn))
    body, r_in, r_in_specs, r_out, r_out_specs, scratch = _ride(body, 5, 4, rider, grid, None)
    outs = pl.pallas_call(
        body, name="swa_bwd", grid=grid,
        in_specs=specs + [specs[1]] + r_in_specs,
        out_specs=[specs[1], kv_out, kv_out, pl.BlockSpec((1, 8, LANES), lambda p, j: (p, 0, 0))] + r_out_specs,
        out_shape=[jax.ShapeDtypeStruct((T, NB_HEADS * HEAD_DIM), F32), jax.ShapeDtypeStruct((T, LANES), F32),
                   jax.ShapeDtypeStruct((T, LANES), F32), jax.ShapeDtypeStruct((pairs, 8, LANES), F32)] + r_out,
        scratch_shapes=scratch,
        compiler_params=_params(2),
    )(sink, qb, kvb, kvb, dy, *r_in)
    return (*outs[:4], list(outs[4:]))


def _merge_fwd(ya, yb, gates, wa, wb, wout, h, *, tm=512):
    T, D = h.shape
    W = ya.shape[1]

    def body(ya_ref, yb_ref, gt_ref, wa_ref, wb_ref, wo_ref, h_ref, h2_ref, mg_ref):
        pa = _dot(ya_ref[...], wa_ref[...])
        pb = _dot(yb_ref[...], wb_ref[...])
        mg = (jax.nn.sigmoid(gt_ref[:, 0:D]) * pa + jax.nn.sigmoid(gt_ref[:, D:2 * D]) * pb).astype(BF16)
        mg_ref[...] = mg
        h2_ref[...] = h_ref[...] + _dot(mg, wo_ref[...])

    def tok(n):
        return pl.BlockSpec((tm, n), lambda i: (i, 0))

    def full(r, c):
        return pl.BlockSpec((r, c), lambda i: (0, 0))

    return pl.pallas_call(
        body, name="merge_fwd", grid=(T // tm,),
        in_specs=[tok(W), tok(W), tok(2 * D), full(W, D), full(W, D), full(D, D), tok(D)],
        out_specs=[tok(D), tok(D)],
        out_shape=[jax.ShapeDtypeStruct((T, D), F32), jax.ShapeDtypeStruct((T, D), BF16)],
        compiler_params=_params(1),
    )(ya, yb, gates, wa, wb, wout, h)


def _merge_bwd(dh, ya, yb, gates, wa, wb, wout, *, tm=512, rider=None):
    T, D = dh.shape
    W = ya.shape[1]

    def body(dh_ref, ya_ref, yb_ref, gt_ref, wa_ref, wb_ref, wo_ref, dya_ref, dyb_ref, dpa_ref, dpb_ref, dgt_ref):
        dmg = _dot_nt(dh_ref[...].astype(BF16), wo_ref[...])
        for y_ref, w_ref, dy_ref, dp_ref, lo in ((ya_ref, wa_ref, dya_ref, dpa_ref, 0), (yb_ref, wb_ref, dyb_ref, dpb_ref, D)):
            sg = jax.nn.sigmoid(gt_ref[:, lo:lo + D])
            dp = (dmg * sg).astype(BF16)
            dp_ref[...] = dp
            dgt_ref[:, lo:lo + D] = (dmg * _dot(y_ref[...], w_ref[...]) * (sg * (1.0 - sg))).astype(BF16)
            dy_ref[...] = _dot_nt(dp, w_ref[...]).astype(BF16)

    def tok(n):
        return pl.BlockSpec((tm, n), lambda i: (i, 0))

    def full(r, c):
        return pl.BlockSpec((r, c), lambda i: (0, 0))

    grid = (T // tm,)
    body, r_in, r_in_specs, r_out, r_out_specs, scratch = _ride(body, 7, 5, rider, grid, None)
    outs = pl.pallas_call(
        body, name="merge_bwd", grid=grid,
        in_specs=[tok(D), tok(W), tok(W), tok(2 * D), full(W, D), full(W, D), full(D, D)] + r_in_specs,
        out_specs=[tok(W), tok(W), tok(D), tok(D), tok(2 * D)] + r_out_specs,
        out_shape=[jax.ShapeDtypeStruct((T, W), BF16), jax.ShapeDtypeStruct((T, W), BF16),
                   jax.ShapeDtypeStruct((T, D), BF16), jax.ShapeDtypeStruct((T, D), BF16),
                   jax.ShapeDtypeStruct((T, 2 * D), BF16)] + r_out,
        scratch_shapes=scratch,
        compiler_params=_params(1),
    )(dh, ya, yb, gates, wa, wb, wout, *r_in)
    return (*outs[:5], list(outs[5:]))


def _pair_heads(a, axis):
    shp = a.shape
    a = a.reshape(shp[:axis] + (2, NB_HEADS // 2, HEAD_DIM) + shp[axis + 1:])
    return jnp.swapaxes(a, axis, axis + 1).reshape(shp)


def _unpair_heads(a, axis):
    shp = a.shape
    a = a.reshape(shp[:axis] + (NB_HEADS // 2, 2, HEAD_DIM) + shp[axis + 1:])
    return jnp.swapaxes(a, axis, axis + 1).reshape(shp)


FFN1 = ("ffn1_w_gate", "ffn1_w_up", "ffn1_w_down")
FFN2 = ("ffn2_w_gate", "ffn2_w_up", "ffn2_w_down")
MIXER = ("w_in", "w_branch_a", "w_branch_b", "w_out")
BRANCH = MIXER[1:]


def _layer_grads(x, target, g1, f1, gmix, late, rpb, sink, g2, gfin, comm=None):
    T = x.shape[0]
    tables = _rope_tables(T)
    bias = _na_bias_slabs(rpb)

    comm = comm or _Local(late)
    h1, n1, hdn1, p1, q1, gathered = _ffn_fwd(x, g1, *f1, name="ffn1_fwd", rider=comm.late_rider)
    w_in_t, wa, wb, wout, f2 = comm.late(gathered)
    w_in_p = jnp.concatenate([w_in_t[:O_QB], _pair_heads(w_in_t[O_QB:O_KB], 0), w_in_t[O_KB:]], axis=0)
    wb_p = _pair_heads(wb, 0)
    u, qkva, qb, kvb, gates = _mix_in_fwd(h1, gmix, w_in_p, tables)
    ya = _na_fwd(qkva, bias)
    yb = _swa_fwd(qb, kvb, sink)
    h2, merged = _merge_fwd(ya, yb, gates, wa, wb_p, wout, h1)
    dh3, n2, hdn2, p2, q2, loss, dgfin, _ = _ffn_fwd(h2, g2, *f2, name="ffn2_fwd", head=(gfin, target))

    dh2, da2, db2, dg2, _ = _ffn_bwd(dh3, h2, g2, p2, q2, *f2, name="ffn2_bwd")
    df2 = [_wgrad_shard_a(da2, n2, name="ffn2_dwg")[0], _wgrad_shard_a(db2, n2, name="ffn2_dwu")[0],
           _wgrad_shard_a(hdn2, dh3, scale=0.5, name="ffn2_dwd")[0]]
    red2 = comm.reduce(FFN2, df2, tag="ffn2")
    dya, dyb, dpa, dpb, dgates, got = _merge_bwd(dh2, ya, yb, gates, wa, wb_p, wout, rider=red2.sibling)
    red2.partial(got)
    dwout = _wgrad_cols(merged, dh2, 1, name="dwout").reshape(N_CHIPS, D_MODEL // N_CHIPS, D_MODEL)
    dwa = _wgrad_cols(ya, dpa, N_CHIPS, name="dwa")
    dwb = _unpair_heads(_wgrad_cols(yb, dpb, N_CHIPS, name="dwb"), 1)
    redb = comm.reduce(BRANCH, [dwa, dwb, dwout], tag="branch")
    dqa, dka, dva, dbias, got = _na_bwd(qkva, dya, bias, rider=_two_riders(red2.chips, redb.sibling))
    red2.halves(got[:len(FFN2)])
    redb.partial(got[len(FFN2):])
    drpb = _rpb_fold(dbias)
    dqb, dkb, dvb, dsink, got = _swa_bwd(qb, kvb, dyb, sink, rider=_two_riders(red2.share, redb.chips))
    out = red2.result(got[:len(FFN2)])
    redb.halves(got[len(FFN2):])
    dz, dh1, dgmix, got = _mix_in_bwd(dqa, dka, dva, dqb, dkb, dvb, dgates, h1, gmix, dh2, w_in_p, tables, rider=redb.share)
    out.update(redb.result(got))
    dwin_p = _wgrad_rows(dz, u, 2, name="dwin")[0].reshape(D_IN, D_MODEL)
    dwin = jnp.concatenate([dwin_p[:O_QB], _unpair_heads(dwin_p[O_QB:O_KB], 0), dwin_p[O_KB:]], axis=0)
    dx, da1, db1, dg1, _ = _ffn_bwd(dh1, x, g1, p1, q1, *f1, name="ffn1_bwd")
    redw = comm.reduce(("w_in",), [dwin.reshape(N_CHIPS, D_IN // N_CHIPS, D_MODEL)], tag="w_in").partial_now()
    dwg1, got = _wgrad_shard_a(da1, n1, name="ffn1_dwg", rider=redw.chips)
    dwu1, got = _wgrad_shard_a(db1, n1, name="ffn1_dwu", rider=redw.halves(got).share)
    out.update(redw.result(got))
    red1 = comm.reduce(FFN1[:2], [dwg1, dwu1], tag="ffn1_gate_up").partial_now()
    dwd1, got = _wgrad_shard_a(hdn1, dh1, scale=0.5, name="ffn1_dwd", rider=red1.chips)
    out.update(red1.halves(got).result_now())
    out.update(comm.reduce(FFN1[2:], [dwd1], tag="ffn1_down").partial_now().halves_now().result_now())
    for names in (FFN2, ("w_in",), BRANCH[:2], BRANCH[2:], FFN1[:2], FFN1[2:]):
        comm.update(names, out)
    out.update(loss=loss, dx=dx, ffn1_norm=dg1, mix_norm=dgmix, ffn2_norm=dg2, final_norm=dgfin, na_rpb=drpb,
               sink_logit=dsink[:, 0:2, 0].T.reshape(NB_HEADS))
    return out


class _Local:
    late_rider = None

    def __init__(self, late):
        self._late = late

    def late(self, gathered):
        return self._late

    def reduce(self, names, grads, *, tag):
        return _LocalReduce(names, grads)

    def update(self, names, reduced):
        pass


class _LocalReduce:
    sibling = chips = share = None

    def __init__(self, names, grads):
        self._result = dict(zip(names, grads))

    def partial(self, got=None):
        return self

    halves = partial_now = halves_now = partial

    def result(self, got=None):
        return self._result

    result_now = result


ANY = pl.BlockSpec(memory_space=pl.ANY)


def _place():
    x, y, c = lax.axis_index("x"), lax.axis_index("y"), lax.axis_index("c")
    chips = [(1 - x, y), (x, 1 - y), (1 - x, 1 - y)]
    return x, y, c, 2 * x + y, chips


def _remote(src, dst, send_sems, recv_sems, k, device):
    return pltpu.make_async_remote_copy(src_ref=src, dst_ref=dst, send_sem=send_sems.at[k], recv_sem=recv_sems.at[k],
                                        device_id=device, device_id_type=MESH)


class _Rider:
    def __init__(self, inputs, out_shape, scratch, start, middle, finish):
        self.inputs, self.out_shape, self.scratch = list(inputs), list(out_shape), list(scratch)
        self.start, self.middle, self.finish = start, middle, finish


def _two_riders(first, second):
    if first is None and second is None:
        return None
    assert first.middle is None and second.middle is None
    n_in, n_out, n_sem = len(first.inputs), len(first.out_shape), len(first.scratch)

    def phase(name):
        def run(ins, outs, sems):
            getattr(first, name)(ins[:n_in], outs[:n_out], sems[:n_sem])
            getattr(second, name)(ins[n_in:], outs[n_out:], sems[n_sem:])
        return run

    return _Rider(first.inputs + second.inputs, first.out_shape + second.out_shape, first.scratch + second.scratch,
                  phase("start"), None, phase("finish"))


def _run_rider(rider, *, name):
    n_in, n_out = len(rider.inputs), len(rider.out_shape)

    def body(*refs):
        ins, outs, sems = refs[:n_in], refs[n_in:n_in + n_out], refs[n_in + n_out:]
        rider.start(ins, outs, sems)
        if rider.middle is not None:
            rider.middle(ins, outs, sems)
        rider.finish(ins, outs, sems)

    return pl.pallas_call(body, name=name, in_specs=[ANY] * n_in, out_specs=[ANY] * n_out, out_shape=rider.out_shape,
                          scratch_shapes=rider.scratch)(*rider.inputs)


def _ride(body, n_in, n_out, rider, grid, middle_step):
    if rider is None:
        return body, [], [], [], [], []
    r_in, r_out = len(rider.inputs), len(rider.out_shape)
    steps = math.prod(grid)

    def riding(*refs):
        ins, r_ins = refs[:n_in], refs[n_in:n_in + r_in]
        outs = refs[n_in + r_in:n_in + r_in + n_out]
        r_outs = refs[n_in + r_in + n_out:n_in + r_in + n_out + r_out]
        sems = refs[n_in + r_in + n_out + r_out:]
        step = pl.program_id(0)
        for axis in range(1, len(grid)):
            step = step * grid[axis] + pl.program_id(axis)

        @pl.when(step == 0)
        def _():
            rider.start(r_ins, r_outs, sems)

        body(*ins, *outs)

        if rider.middle is not None:
            @pl.when(step == middle_step)
            def _():
                rider.middle(r_ins, r_outs, sems)

        @pl.when(step == steps - 1)
        def _():
            rider.finish(r_ins, r_outs, sems)

    return riding, rider.inputs, [ANY] * r_in, rider.out_shape, [ANY] * r_out, rider.scratch


def _gather_rider(shards):
    n = len(shards)

    def plan(ins, outs, sems, kinds):
        send_sems, recv_sems, own_send_sems, own_recv_sems = sems
        x, y, c, mine, chips = _place()
        sibling = (x, y, 1 - c)
        made = {k: [] for k in kinds}
        for i in range(n):
            hr = shards[i].shape[0] // 2
            if "own" in made:
                made["own"].append(_remote(ins[i], outs[i].at[mine], own_send_sems, own_recv_sems, i, sibling))
            for j, (cx, cy) in enumerate(chips):
                here = outs[i].at[2 * cx + cy, pl.ds(c * hr, hr)]
                there = outs[i].at[2 * cx + cy, pl.ds((1 - c) * hr, hr)]
                if "sends" in made:
                    made["sends"].append(_remote(ins[i].at[pl.ds(c * hr, hr)], outs[i].at[mine, pl.ds(c * hr, hr)],
                                                 send_sems, recv_sems, 6 * i + j, (cx, cy, c)))
                if "landed" in made:
                    made["landed"].append(_remote(here, here, send_sems, recv_sems, 6 * i + j, (cx, cy, c)))
                if "passes" in made:
                    made["passes"].append(_remote(here, here, send_sems, recv_sems, 6 * i + 3 + j, sibling))
                if "others" in made:
                    made["others"].append(_remote(there, there, send_sems, recv_sems, 6 * i + 3 + j, sibling))
        return [made[k] for k in kinds]

    def start(ins, outs, sems):
        own, sends = plan(ins, outs, sems, ("own", "sends"))
        for cp in own + sends:
            cp.start()

    def middle(ins, outs, sems):
        landed, passes = plan(ins, outs, sems, ("landed", "passes"))
        for arrived, cp in zip(landed, passes):
            arrived.wait_recv()
            cp.start()

    def finish(ins, outs, sems):
        own, sends, passes, others = plan(ins, outs, sems, ("own", "sends", "passes", "others"))
        for arrived in others:
            arrived.wait_recv()
        for cp in sends + passes:
            cp.wait_send()
        for cp in own:
            cp.wait()

    return _Rider(shards, [jax.ShapeDtypeStruct((N_CHIPS,) + s.shape, s.dtype) for s in shards],
                  [pltpu.SemaphoreType.DMA((6 * n,)), pltpu.SemaphoreType.DMA((6 * n,)),
                   pltpu.SemaphoreType.DMA((n,)), pltpu.SemaphoreType.DMA((n,))], start, middle, finish)


def _swap_rider(arrays, out_shape, source):
    n = len(arrays)

    def plan(ins, outs, sems):
        send_sems, recv_sems = sems
        x, y, c, _, _ = _place()
        return [_remote(source(ins[i], c, i), outs[i], send_sems, recv_sems, i, (x, y, 1 - c)) for i in range(n)]

    def start(ins, outs, sems):
        for cp in plan(ins, outs, sems):
            cp.start()

    def finish(ins, outs, sems):
        for cp in plan(ins, outs, sems):
            cp.wait()

    return _Rider(arrays, out_shape, [pltpu.SemaphoreType.DMA((n,)), pltpu.SemaphoreType.DMA((n,))], start, None, finish)


def _sibling_rider(grads):
    half = [g.shape[1] // 2 for g in grads]
    return _swap_rider(grads, [jax.ShapeDtypeStruct((g.shape[0], hr, g.shape[2]), g.dtype) for g, hr in zip(grads, half)],
                       lambda ref, c, i: ref.at[:, pl.ds((1 - c) * half[i], half[i])])


def _share_rider(halves):
    return _swap_rider(halves, [jax.ShapeDtypeStruct(h.shape, h.dtype) for h in halves], lambda ref, c, i: ref)


def _chips_rider(parts):
    n = len(parts)

    def plan(ins, outs, sems):
        send_sems, recv_sems = sems
        _, _, c, _, chips = _place()
        return [_remote(ins[i].at[2 * cx + cy], outs[i].at[j], send_sems, recv_sems, 3 * i + j, (cx, cy, c))
                for i in range(n) for j, (cx, cy) in enumerate(chips)]

    def start(ins, outs, sems):
        for cp in plan(ins, outs, sems):
            cp.start()

    def finish(ins, outs, sems):
        for cp in plan(ins, outs, sems):
            cp.wait()

    return _Rider(parts, [jax.ShapeDtypeStruct((N_CHIPS - 1,) + p.shape[1:], p.dtype) for p in parts],
                  [pltpu.SemaphoreType.DMA((3 * n,)), pltpu.SemaphoreType.DMA((3 * n,))], start, None, finish)


class _Reduce:
    def __init__(self, names, grads, cidx, chip, *, tag):
        self.names, self.grads, self.cidx, self.chip, self.tag = names, grads, cidx, chip, tag
        self.sibling = _sibling_rider(grads)

    def _by_shape(self, fn, *lists):
        done, i = [], 0
        while i < len(self.names):
            j = i + 1
            while j < len(self.names) and self.grads[j].shape == self.grads[i].shape:
                j += 1
            done += fn(*[lst[i:j] for lst in lists], self.names[i])
            i = j
        return done

    def partial(self, from_sibling):
        self.from_sibling = from_sibling
        self.chips = _chips_rider(self._by_shape(
            lambda g, r, k: _add_sibling(g, r, self.cidx, name="add_sibling_" + k), self.grads, from_sibling))
        return self

    def halves(self, from_chips):
        self.mine = self._by_shape(
            lambda g, r1, r2, k: _add_chips(g, r1, r2, self.cidx, self.chip, name="add_chips_" + k),
            self.grads, self.from_sibling, from_chips)
        self.share = _share_rider(self.mine)
        return self

    def result(self, others):
        return dict(zip(self.names, zip(self.mine, others)))

    def partial_now(self):
        return self.partial(_run_rider(self.sibling, name="rs_sibling_" + self.tag))

    def halves_now(self):
        return self.halves(_run_rider(self.chips, name="rs_chips_" + self.tag))

    def result_now(self):
        return self.result(_run_rider(self.share, name="rs_share_" + self.tag))


N_DEV = 8


def _small_allreduce(vec):
    R = vec.shape[0]

    def body(v_ref, o_ref, buf, send_sems, recv_sems):
        x, y, c, _, _ = _place()
        me = 4 * x + 2 * y + c
        buf[me] = v_ref[...]
        copies = []
        for k in range(1, N_DEV):
            peer = (x ^ (k >> 2), y ^ ((k >> 1) & 1), c ^ (k & 1))
            cp = _remote(v_ref, buf.at[me], send_sems, recv_sems, k - 1, peer)
            cp.start()
            copies.append(cp)
        for k, cp in enumerate(copies, start=1):
            cp.wait_send()
            landed = buf.at[me ^ k]
            _remote(landed, landed, send_sems, recv_sems, k - 1, (x, y, c)).wait_recv()
        acc = buf[0]
        for d in range(1, N_DEV):
            acc = acc + buf[d]
        o_ref[...] = acc

    return pl.pallas_call(
        body, name="small_allreduce",
        in_specs=[pl.BlockSpec(memory_space=pltpu.VMEM)], out_specs=pl.BlockSpec(memory_space=pltpu.VMEM),
        out_shape=jax.ShapeDtypeStruct(vec.shape, vec.dtype),
        scratch_shapes=[pltpu.VMEM((N_DEV, R, LANES), F32), pltpu.SemaphoreType.DMA((N_DEV - 1,)),
                        pltpu.SemaphoreType.DMA((N_DEV - 1,))],
    )(vec)


ELEMWISE_BLOCK = 512 * 1024


def _row_tile(rows, cols):
    best = None
    for t in range(16, rows + 1, 16):
        if rows % t == 0 and t * cols <= ELEMWISE_BLOCK:
            best = t
    return best if best is not None else rows


def _add_sibling(gs, r1s, cidx, *, name):
    n = len(gs)
    S, R, C = gs[0].shape
    hr = R // 2
    tr = _row_tile(hr, C * n)
    nt = hr // tr

    def body(c_ref, *refs):
        for g_ref, r_ref, o_ref in zip(refs[:n], refs[n:2 * n], refs[2 * n:]):
            o_ref[...] = (g_ref[...] + r_ref[...]).astype(BF16)

    blk = pl.BlockSpec((1, tr, C), lambda s, t, c: (s, t, 0))
    mine = pl.BlockSpec((1, tr, C), lambda s, t, c: (s, c[0] * nt + t, 0))
    return list(pl.pallas_call(
        body, name=name,
        grid_spec=pltpu.PrefetchScalarGridSpec(
            num_scalar_prefetch=1, grid=(S, nt), in_specs=[mine] * n + [blk] * n, out_specs=[blk] * n),
        out_shape=[jax.ShapeDtypeStruct((S, hr, C), BF16)] * n,
        compiler_params=_params(2),
    )(cidx, *gs, *r1s))


def _add_chips(gs, r1s, r2s, cidx, chip, *, name):
    n = len(gs)
    _, R, C = gs[0].shape
    hr = R // 2
    tr = _row_tile(hr, C * n)
    nt = hr // tr

    def body(pos_ref, *refs):
        for g_ref, r1_ref, r2_ref, o_ref in zip(refs[:n], refs[n:2 * n], refs[2 * n:3 * n], refs[3 * n:]):
            own = g_ref[0] + r1_ref[0]
            o_ref[...] = ((own + r2_ref[0].astype(F32)) + r2_ref[1].astype(F32)) + r2_ref[2].astype(F32)

    pos = jnp.concatenate([cidx, chip])
    return list(pl.pallas_call(
        body, name=name,
        grid_spec=pltpu.PrefetchScalarGridSpec(
            num_scalar_prefetch=1, grid=(nt,),
            in_specs=[pl.BlockSpec((1, tr, C), lambda t, pos: (pos[1], pos[0] * nt + t, 0))] * n
            + [pl.BlockSpec((1, tr, C), lambda t, pos: (pos[1], t, 0))] * n
            + [pl.BlockSpec((N_CHIPS - 1, tr, C), lambda t, pos: (0, t, 0))] * n,
            out_specs=[pl.BlockSpec((tr, C), lambda t, pos: (t, 0))] * n),
        out_shape=[jax.ShapeDtypeStruct((hr, C), F32)] * n,
        compiler_params=_params(1),
    )(pos, *gs, *r1s, *r2s))


def _adamw_math(w, g, m, v):
    mn = ADAM_B1 * m + (1.0 - ADAM_B1) * g
    vn = ADAM_B2 * v + (1.0 - ADAM_B2) * (g * g)
    m_hat = mn / (1.0 - ADAM_B1 ** ADAM_STEP)
    v_hat = vn / (1.0 - ADAM_B2 ** ADAM_STEP)
    return -ADAM_LR * (m_hat / (jnp.sqrt(v_hat) + ADAM_EPS) + ADAM_WD * w), mn, vn


def _adamw_halves(ws, mines, others, ms, vs, cidx, *, name):
    n = len(ws)
    R, C = ws[0].shape
    hr = R // 2
    tr = _row_tile(hr, C * n)
    nt = hr // tr

    def body(c_ref, *refs):
        for i in range(n):
            w_ref, a_ref, b_ref, m_ref, v_ref = (refs[j * n + i] for j in range(5))
            g_ref, d_ref, mo_ref, vo_ref = (refs[(5 + j) * n + i] for j in range(4))
            gv = jnp.where(pl.program_id(0) == c_ref[0], a_ref[...], b_ref[...])
            g_ref[...] = gv
            d_ref[...], mo_ref[...], vo_ref[...] = _adamw_math(w_ref[...], gv, m_ref[...], v_ref[...])

    full = pl.BlockSpec((tr, C), lambda h, t, c: (h * nt + t, 0))
    own = pl.BlockSpec((tr, C), lambda h, t, c: (jnp.where(h == c[0], t, 0), 0))
    sib = pl.BlockSpec((tr, C), lambda h, t, c: (jnp.where(h == c[0], 0, t), 0))
    shape = jax.ShapeDtypeStruct((R, C), F32)
    outs = pl.pallas_call(
        body, name=name,
        grid_spec=pltpu.PrefetchScalarGridSpec(
            num_scalar_prefetch=1, grid=(2, nt),
            in_specs=[full] * n + [own] * n + [sib] * n + [full] * (2 * n), out_specs=[full] * (4 * n)),
        out_shape=[shape] * (4 * n),
        compiler_params=_params(2),
    )(cidx, *ws, *mines, *others, *ms, *vs)
    return [list(outs[j * n:(j + 1) * n]) for j in range(4)]


def _adamw_small(ws, gs, ms, vs):
    n = len(ws)

    def body(*refs):
        for i in range(n):
            w_ref, g_ref, m_ref, v_ref = (refs[j * n + i] for j in range(4))
            d_ref, mo_ref, vo_ref = (refs[(4 + j) * n + i] for j in range(3))
            d_ref[...], mo_ref[...], vo_ref[...] = _adamw_math(w_ref[...], g_ref[...], m_ref[...], v_ref[...])

    shapes = [jax.ShapeDtypeStruct(a.shape, F32) for a in ws]
    outs = pl.pallas_call(body, name="adamw_small", out_shape=shapes * 3, compiler_params=_params(0))(*ws, *gs, *ms, *vs)
    return outs[:n], outs[n:2 * n], outs[2 * n:]


def _unstack_cols(w):
    s, r, c = w.shape
    return w.transpose(1, 0, 2).reshape(r, s * c)


def _pad_rows(a, rows):
    return jnp.pad(a, ((0, rows - a.shape[0]), (0, LANES - a.shape[1])))


BIG = ("ffn1_w_gate", "ffn1_w_up", "ffn1_w_down", "w_in", "w_branch_a", "w_branch_b", "w_out",
       "ffn2_w_gate", "ffn2_w_up", "ffn2_w_down")
TRANSPOSED = ("ffn1_w_gate", "ffn1_w_up", "w_in", "ffn2_w_gate", "ffn2_w_up")
WEIGHTS = ("ffn1_norm", "ffn1_w_gate", "ffn1_w_up", "ffn1_w_down", "mix_norm", "w_in", "na_rpb", "sink_logit",
           "w_branch_a", "w_branch_b", "w_out", "ffn2_norm", "ffn2_w_gate", "ffn2_w_up", "ffn2_w_down", "final_norm")


def kernel(x, ffn1_norm, ffn1_w_gate, ffn1_w_up, ffn1_w_down, mix_norm, w_in, na_rpb, sink_logit, w_branch_a, w_branch_b, w_out, ffn2_norm, ffn2_w_gate, ffn2_w_up, ffn2_w_down, final_norm, loss_target, m_ffn1_norm, m_ffn1_w_gate, m_ffn1_w_up, m_ffn1_w_down, m_mix_norm, m_w_in, m_na_rpb, m_sink_logit, m_w_branch_a, m_w_branch_b, m_w_out, m_ffn2_norm, m_ffn2_w_gate, m_ffn2_w_up, m_ffn2_w_down, m_final_norm, v_ffn1_norm, v_ffn1_w_gate, v_ffn1_w_up, v_ffn1_w_down, v_mix_norm, v_w_in, v_na_rpb, v_sink_logit, v_w_branch_a, v_w_branch_b, v_w_out, v_ffn2_norm, v_ffn2_w_gate, v_ffn2_w_up, v_ffn2_w_down, v_final_norm):
    args = dict(locals())
    w = {k: args[k] for k in WEIGHTS}
    mom = {k: args["m_" + k] for k in WEIGHTS}
    var = {k: args["v_" + k] for k in WEIGHTS}
    cidx = lax.axis_index("c").astype(jnp.int32).reshape(1)
    chip = (2 * lax.axis_index("x") + lax.axis_index("y")).astype(jnp.int32).reshape(1)

    def shard(a, k):
        return jnp.swapaxes(a[0], 0, 1) if k in TRANSPOSED else a[0]

    def unshard(a, k):
        return (jnp.swapaxes(a, 0, 1) if k in TRANSPOSED else a)[None]

    def bf16_shards(names):
        return [shard(w[k], k).astype(BF16) for k in names]

    class comm:
        late_rider = _gather_rider(bf16_shards(MIXER + FFN2))

        @staticmethod
        def late(gathered):
            full = dict(zip(MIXER + FFN2, gathered))
            return (full["w_in"].reshape(D_IN, D_MODEL), _unstack_cols(full["w_branch_a"]), _unstack_cols(full["w_branch_b"]),
                    full["w_out"].reshape(D_MODEL, D_MODEL), tuple(full[k] for k in FFN2))

        @staticmethod
        def reduce(names, grads, *, tag):
            return _Reduce(names, grads, cidx, chip, tag=tag)

        @staticmethod
        def update(names, reduced):
            res = _adamw_halves([shard(w[k], k) for k in names], [reduced[k][0] for k in names],
                                [reduced[k][1] for k in names], [shard(mom[k], k) for k in names],
                                [shard(var[k], k) for k in names], cidx, name="adamw_" + names[0])
            for i, k in enumerate(names):
                grads_out[k], deltas[k], new_m[k], new_v[k] = (unshard(a[i], k) for a in res)

    deltas, new_m, new_v, grads_out, grad = {}, {}, {}, {}, {}
    f1 = _run_rider(_gather_rider(bf16_shards(FFN1)), name="all_gather_ffn1")
    out = _layer_grads(x[0], loss_target[0], ffn1_norm, f1, mix_norm, None, na_rpb[0], sink_logit[0], ffn2_norm,
                       final_norm.reshape(1, D_MODEL), comm=comm)

    rows = D_MODEL // LANES
    small = jnp.concatenate([
        out["ffn1_norm"].reshape(rows, LANES), out["mix_norm"].reshape(rows, LANES), out["ffn2_norm"].reshape(rows, LANES),
        out["final_norm"].reshape(rows, LANES), out["na_rpb"].reshape(-1, LANES),
        _pad_rows(out["sink_logit"].reshape(1, NB_HEADS), 8), _pad_rows(out["loss"], 8)], axis=0)
    total = _small_allreduce(small)
    n_rpb = NA_HEADS * 2 * NA_KH
    grad["ffn1_norm"] = total[0:rows].reshape(1, D_MODEL)
    grad["mix_norm"] = total[rows:2 * rows].reshape(1, D_MODEL)
    grad["ffn2_norm"] = total[2 * rows:3 * rows].reshape(1, D_MODEL)
    grad["final_norm"] = total[3 * rows:4 * rows].reshape(1, D_MODEL)
    grad["na_rpb"] = total[4 * rows:4 * rows + n_rpb].reshape(NA_HEADS, 2 * NA_KH, LANES)[:, :2 * NA_KH - 1, :2 * NA_KW - 1]
    grad["na_rpb"] = grad["na_rpb"].reshape(NA_HEADS, -1)
    grad["sink_logit"] = total[4 * rows + n_rpb:4 * rows + n_rpb + 1, 0:NB_HEADS]
    loss = total[4 * rows + n_rpb + 8, 0]

    small_names = [k for k in WEIGHTS if k not in BIG]
    res = _adamw_small(*[[a[k].reshape(grad[k].shape) for k in small_names] for a in (w, grad, mom, var)])
    for i, k in enumerate(small_names):
        grads_out[k], deltas[k], new_m[k], new_v[k] = (a.reshape(w[k].shape) for a in (grad[k], res[0][i], res[1][i], res[2][i]))
    return (loss, out["dx"].reshape(x.shape), *[grads_out[k] for k in WEIGHTS], *[deltas[k] for k in WEIGHTS],
            *[new_m[k] for k in WEIGHTS], *[new_v[k] for k in WEIGHTS])
```

```python
import math

import jax
import jax.numpy as jnp
import numpy as np
from jax import lax
from jax.experimental import pallas as pl
from jax.experimental.pallas import tpu as pltpu

F32 = jnp.float32
BF16 = jnp.bfloat16

D_MODEL = 1024
HEAD_DIM = 64
NA_HEADS = 8
NB_HEADS = 8
GRID_W = 64
NA_KH = 8
NA_KW = 16
WIN = 128
ROPE_THETA = 10000.0
EPS = 1e-6
N_CHIPS = 4
QK_SCALE = HEAD_DIM ** -0.5
NEG = -1e30
LANES = 128
VMEM_LIMIT = 56 * 1024 * 1024
HEAD_ROWS = 256
WGRAD_TOKENS_BYTES = 8192

C_QKVA = 3 * NA_HEADS * HEAD_DIM
C_QB = NB_HEADS * HEAD_DIM
C_KB = 2 * HEAD_DIM
C_ROPE = C_QB + C_KB
C_GATES = 2 * D_MODEL
D_IN = C_QKVA + C_QB + 2 * C_KB + C_GATES
O_QB = C_QKVA
O_KB = O_QB + C_QB
O_VB = O_KB + C_KB
O_G = O_VB + C_KB

ADAM_LR = 0.001
ADAM_B1 = 0.9
ADAM_B2 = 0.999
ADAM_EPS = 1e-08
ADAM_WD = 0.01
ADAM_STEP = 10

MESH = pl.DeviceIdType.MESH


def _dot(a, b):
    return jnp.dot(a, b, preferred_element_type=F32)


def _dot_nt(a, b):
    return lax.dot_general(a, b, (((1,), (1,)), ((), ())), preferred_element_type=F32)


def _dot_tn(a, b):
    return lax.dot_general(a, b, (((0,), (0,)), ((), ())), preferred_element_type=F32)


def _params(n_axes):
    return pltpu.CompilerParams(dimension_semantics=("arbitrary",) * n_axes, vmem_limit_bytes=VMEM_LIMIT)


def _rstd(xf):
    return lax.rsqrt(jnp.mean(xf * xf, axis=-1, keepdims=True) + EPS)


def _norm_bwd(dn, xf, g, r):
    xhat = xf * r
    dxh = dn * g
    dx = r * (dxh - xhat * jnp.mean(dxh * xhat, axis=-1, keepdims=True))
    return dx, dn * xhat


def _sigmoid(x):
    return 0.5 * jnp.tanh(0.5 * x) + 0.5


def _loss_head(hf, gv, tgt):
    r = _rstd(hf)
    err = (hf * r) * gv - tgt
    dx, dgr = _norm_bwd(err * (1.0 / hf.shape[-1]), hf, gv, r)
    return 0.5 * jnp.mean(err * err, axis=-1, keepdims=True), dx, dgr


def _ffn_fwd(x, g, wg, wu, wd, *, name, tm=1024, sub=512, rider=None, head=None):
    T, D = x.shape
    F = wg.shape[1]
    tm = min(tm, T)
    sub = min(sub, tm)
    n_head = 0 if head is None else 2

    def body(*refs):
        x_ref, g_ref, wg_ref, wu_ref, wd_ref = refs[:5]
        h_ref, n_ref, hdn_ref, p_ref, q_ref = refs[5 + n_head:10 + n_head]
        i, s = pl.program_id(0), pl.program_id(1)
        _ffn_fwd_step(x_ref, g_ref, wg_ref, wu_ref, wd_ref, h_ref, n_ref, hdn_ref, p_ref, q_ref, s)
        if head is not None:
            gf_ref, t_ref = refs[5:7]
            loss_ref, dgf_ref = refs[10 + n_head:]

            @pl.when((i == 0) & (s == 0))
            def _():
                loss_ref[...] = jnp.zeros_like(loss_ref)
                dgf_ref[...] = jnp.zeros_like(dgf_ref)

            @pl.when(s == N_CHIPS - 1)
            def _():
                for u in range(tm // HEAD_ROWS):
                    r = pl.ds(u * HEAD_ROWS, HEAD_ROWS)
                    terms, dh, dgr = _loss_head(h_ref[r, :], gf_ref[...], t_ref[r, :])
                    loss_ref[...] += jnp.broadcast_to(jnp.sum(terms), loss_ref.shape)
                    dgf_ref[...] += jnp.sum(dgr, axis=0, keepdims=True)
                    h_ref[r, :] = dh

    def _ffn_fwd_step(x_ref, g_ref, wg_ref, wu_ref, wd_ref, h_ref, n_ref, hdn_ref, p_ref, q_ref, s):

        @pl.when(s == 0)
        def _():
            xf = x_ref[...]
            n_ref[...] = ((xf * _rstd(xf)) * g_ref[...]).astype(BF16)
            h_ref[...] = xf

        rows = [pl.ds(u * sub, sub) for u in range(tm // sub)]
        ab = [(_dot_nt(n_ref[r, :], wg_ref[0]), _dot_nt(n_ref[r, :], wu_ref[0])) for r in rows]
        hdns = []
        for r, (a, b) in zip(rows, ab):
            sg = _sigmoid(a)
            silu = a * sg
            hdn = (silu * b).astype(BF16)
            hdn_ref[0, r, :] = hdn
            p_ref[0, r, :] = (b * (sg + silu * (1.0 - sg))).astype(BF16)
            q_ref[0, r, :] = silu.astype(BF16)
            hdns.append(hdn)
        for r, hdn in zip(rows, hdns):
            h_ref[r, :] += 0.5 * _dot(hdn, wd_ref[0])

    tok = pl.BlockSpec((tm, D), lambda i, s: (i, 0))
    hid = pl.BlockSpec((1, tm, F), lambda i, s: (s, i, 0))
    wspec = pl.BlockSpec((1, F, D), lambda i, s: (s, 0, 0))
    hshape = jax.ShapeDtypeStruct((N_CHIPS, T, F), BF16)
    grid = (T // tm, N_CHIPS)
    vec = pl.BlockSpec((1, D), lambda i, s: (0, 0))
    head_in, head_in_specs, head_out, head_out_specs = [], [], [], []
    if head is not None:
        head_in, head_in_specs = list(head), [vec, tok]
        head_out = [jax.ShapeDtypeStruct((1, LANES), F32), jax.ShapeDtypeStruct((1, D), F32)]
        head_out_specs = [pl.BlockSpec((1, LANES), lambda i, s: (0, 0)), vec]
    n_main = 5 + n_head
    body, r_in, r_in_specs, r_out, r_out_specs, scratch = _ride(body, n_main, n_main, rider, grid, (grid[0] * grid[1] * 7) // 8)
    outs = pl.pallas_call(
        body, name=name, grid=grid,
        in_specs=[tok, vec, wspec, wspec, wspec] + head_in_specs + r_in_specs,
        out_specs=[tok, tok, hid, hid, hid] + head_out_specs + r_out_specs,
        out_shape=[jax.ShapeDtypeStruct((T, D), F32), jax.ShapeDtypeStruct((T, D), BF16), hshape, hshape, hshape]
        + head_out + r_out,
        scratch_shapes=scratch,
        compiler_params=_params(2),
    )(x, g, wg, wu, wd, *head_in, *r_in)
    return (*outs[:n_main], list(outs[n_main:]))


def _ffn_bwd(dh, x, g, p, q, wg, wu, wd, *, name, tm=1024, sub=256, rider=None):
    T, D = x.shape
    F = wg.shape[1]
    tm = min(tm, T)
    sub = min(sub, tm)

    def body(dh_ref, x_ref, g_ref, p_ref, q_ref, wg_ref, wu_ref, wd_ref, dx_ref, da_ref, db_ref, dg_ref):
        i, s = pl.program_id(0), pl.program_id(1)

        @pl.when((i == 0) & (s == 0))
        def _():
            dg_ref[...] = jnp.zeros_like(dg_ref)

        @pl.when(s == 0)
        def _():
            dx_ref[...] = jnp.zeros_like(dx_ref)

        rows = [pl.ds(u * sub, sub) for u in range(tm // sub)]
        dhdn = [_dot_nt((0.5 * dh_ref[r, :]).astype(BF16), wd_ref[0]) for r in rows]
        das, dbs = [], []
        for r, dd in zip(rows, dhdn):
            da = (dd * p_ref[0, r, :].astype(F32)).astype(BF16)
            db = (dd * q_ref[0, r, :].astype(F32)).astype(BF16)
            da_ref[0, r, :] = da
            db_ref[0, r, :] = db
            das.append(da)
            dbs.append(db)
        for r, da, db in zip(rows, das, dbs):
            dx_ref[r, :] += _dot(da, wg_ref[0]) + _dot(db, wu_ref[0])

        @pl.when(s == N_CHIPS - 1)
        def _():
            xf = x_ref[...]
            dx, dgr = _norm_bwd(dx_ref[...], xf, g_ref[...], _rstd(xf))
            dg_ref[...] += jnp.sum(dgr, axis=0, keepdims=True)
            dx_ref[...] = dh_ref[...] + dx

    tok = pl.BlockSpec((tm, D), lambda i, s: (i, 0))
    hid = pl.BlockSpec((1, tm, F), lambda i, s: (s, i, 0))
    vec = pl.BlockSpec((1, D), lambda i, s: (0, 0))
    hshape = jax.ShapeDtypeStruct((N_CHIPS, T, F), BF16)
    wspec = pl.BlockSpec((1, F, D), lambda i, s: (s, 0, 0))
    grid = (T // tm, N_CHIPS)
    body, r_in, r_in_specs, r_out, r_out_specs, scratch = _ride(body, 8, 4, rider, grid, None)
    outs = pl.pallas_call(
        body, name=name, grid=grid,
        in_specs=[tok, tok, vec, hid, hid, wspec, wspec, wspec] + r_in_specs,
        out_specs=[tok, hid, hid, vec] + r_out_specs,
        out_shape=[jax.ShapeDtypeStruct((T, D), F32), hshape, hshape, jax.ShapeDtypeStruct((1, D), F32)] + r_out,
        scratch_shapes=scratch,
        compiler_params=_params(2),
    )(dh, x, g, p, q, wg, wu, wd, *r_in)
    return (*outs[:4], list(outs[4:]))


def _wgrad(a, b, *, a_block, a_map, b_block, b_map, out_shape, o_block, o_map, grid, scale=1.0, name, rider=None):
    def body(a_ref, b_ref, o_ref):
        @pl.when(pl.program_id(len(grid) - 1) == 0)
        def _():
            o_ref[...] = jnp.zeros_like(o_ref)

        av = a_ref[...]
        bv = b_ref[...]
        av = av.reshape(av.shape[-2:]).astype(BF16)
        bv = bv.reshape(bv.shape[-2:])
        if scale != 1.0:
            bv = scale * bv
        o_ref[...] += _dot_tn(av, bv.astype(BF16)).reshape(o_ref.shape)

    body, r_in, r_in_specs, r_out, r_out_specs, scratch = _ride(body, 2, 1, rider, grid, None)
    outs = pl.pallas_call(
        body, name=name, grid=grid,
        in_specs=[pl.BlockSpec(a_block, a_map), pl.BlockSpec(b_block, b_map)] + r_in_specs,
        out_specs=[pl.BlockSpec(o_block, o_map)] + r_out_specs,
        out_shape=[jax.ShapeDtypeStruct(out_shape, F32)] + r_out,
        scratch_shapes=scratch,
        compiler_params=_params(len(grid)),
    )(a, b, *r_in)
    return outs[0], list(outs[1:])


def _wgrad_rows(a, b, n_blocks, *, name, tk=2048):
    T, N = b.shape
    M = a.shape[1] // n_blocks
    tk = min(tk, T)
    return _wgrad(a, b, a_block=(tk, M), a_map=lambda s, k: (k, s), b_block=(tk, N), b_map=lambda s, k: (k, 0),
                  out_shape=(n_blocks, M, N), o_block=(1, M, N), o_map=lambda s, k: (s, 0, 0), grid=(n_blocks, T // tk), name=name)


def _wgrad_shard_a(a, b, *, name, scale=1.0, rider=None):
    S, T, M = a.shape
    N = b.shape[1]
    tk = min(WGRAD_TOKENS_BYTES // b.dtype.itemsize, T)
    return _wgrad(a, b, a_block=(1, tk, M), a_map=lambda s, k: (s, k, 0), b_block=(tk, N), b_map=lambda s, k: (k, 0),
                  out_shape=(S, M, N), o_block=(1, M, N), o_map=lambda s, k: (s, 0, 0), grid=(S, T // tk), scale=scale,
                  name=name, rider=rider)


def _wgrad_cols(a, b, n_blocks, *, name, tk=2048):
    T, M = a.shape
    N = b.shape[1] // n_blocks
    tk = min(tk, T)

    def body(a_ref, b_ref, o_ref):
        @pl.when(pl.program_id(0) == 0)
        def _():
            o_ref[...] = jnp.zeros_like(o_ref)

        r = _dot_tn(a_ref[...].astype(BF16), b_ref[...].astype(BF16))
        for s in range(n_blocks):
            o_ref[s] += r[:, s * N:(s + 1) * N]

    return pl.pallas_call(
        body, name=name, grid=(T // tk,),
        in_specs=[pl.BlockSpec((tk, M), lambda k: (k, 0)), pl.BlockSpec((tk, n_blocks * N), lambda k: (k, 0))],
        out_specs=pl.BlockSpec((n_blocks, M, N), lambda k: (0, 0, 0)),
        out_shape=jax.ShapeDtypeStruct((n_blocks, M, N), F32),
        compiler_params=_params(1),
    )(a, b)


def _rope_tables(T):
    half = HEAD_DIM // 2
    inv = np.float32(ROPE_THETA) ** (-np.arange(half, dtype=np.float32) / np.float32(half))
    ang = np.arange(T, dtype=np.float32)[:, None] * inv[None, :]
    cos, sin, zero = np.cos(ang), np.sin(ang), np.zeros_like(ang)
    reps = LANES // HEAD_DIM
    return (jnp.asarray(np.tile(np.concatenate([cos, cos], axis=1), (1, reps))),
            jnp.asarray(np.tile(np.concatenate([-sin, zero], axis=1), (1, reps))),
            jnp.asarray(np.tile(np.concatenate([zero, sin], axis=1), (1, reps))))


def _rope(x, cos, sa, sb, sign):
    half = HEAD_DIM // 2
    return x * cos + sign * (pltpu.roll(x, LANES - half, 1) * sa + pltpu.roll(x, half, 1) * sb)


def _mix_in_fwd(h, g, w_in, tables, *, tm=512):
    T, D = h.shape

    def body(h_ref, g_ref, w_ref, cos_ref, sa_ref, sb_ref, u_ref, qkva_ref, qb_ref, kvb_ref, gates_ref):
        hf = h_ref[...]
        u = ((hf * _rstd(hf)) * g_ref[...]).astype(BF16)
        u_ref[...] = u
        qkva_ref[...] = _dot_nt(u, w_ref[0:C_QKVA, :]).astype(BF16)
        zr = _dot_nt(u, w_ref[O_QB:O_QB + C_ROPE, :])
        cos, sa, sb = cos_ref[...], sa_ref[...], sb_ref[...]
        for j in range(C_ROPE // LANES):
            rj = _rope(zr[:, j * LANES:(j + 1) * LANES], cos, sa, sb, 1.0).astype(BF16)
            if j < C_QB // LANES:
                qb_ref[:, j * LANES:(j + 1) * LANES] = rj
            else:
                kvb_ref[:, 0:C_KB] = rj
        kvb_ref[:, C_KB:2 * C_KB] = _dot_nt(u, w_ref[O_VB:O_VB + C_KB, :]).astype(BF16)
        gates_ref[...] = _dot_nt(u, w_ref[O_G:O_G + C_GATES, :])

    def tok(n):
        return pl.BlockSpec((tm, n), lambda i: (i, 0))

    return pl.pallas_call(
        body, name="mix_in_fwd", grid=(T // tm,),
        in_specs=[tok(D), pl.BlockSpec((1, D), lambda i: (0, 0)), pl.BlockSpec((D_IN, D), lambda i: (0, 0), pipeline_mode=pl.Buffered(1)),
                  tok(LANES), tok(LANES), tok(LANES)],
        out_specs=[tok(D), tok(C_QKVA), tok(C_QB), tok(2 * C_KB), tok(C_GATES)],
        out_shape=[jax.ShapeDtypeStruct((T, D), BF16), jax.ShapeDtypeStruct((T, C_QKVA), BF16),
                   jax.ShapeDtypeStruct((T, C_QB), BF16), jax.ShapeDtypeStruct((T, 2 * C_KB), BF16),
                   jax.ShapeDtypeStruct((T, C_GATES), F32)],
        compiler_params=_params(1),
    )(h, g, w_in, *tables)


def _mix_in_bwd(dqa, dka, dva, dqb, dkb, dvb, dgates, h, g, dres, w_in, tables, *, tm=512, rider=None):
    T, D = h.shape

    def body(dqa_ref, dka_ref, dva_ref, dqb_ref, dkb_ref, dvb_ref, dgt_ref, h_ref, g_ref, dres_ref, w_ref,
             cos_ref, sa_ref, sb_ref, dz_ref, dh_ref, dg_ref):
        @pl.when(pl.program_id(0) == 0)
        def _():
            dg_ref[...] = jnp.zeros_like(dg_ref)

        na = NA_HEADS * HEAD_DIM
        dz_ref[:, 0:na] = dqa_ref[...].astype(BF16)
        dz_ref[:, na:2 * na] = dka_ref[...].astype(BF16)
        dz_ref[:, 2 * na:3 * na] = dva_ref[...].astype(BF16)
        cos, sa, sb = cos_ref[...], sa_ref[...], sb_ref[...]
        for j in range(C_QB // LANES):
            dz_ref[:, O_QB + j * LANES:O_QB + (j + 1) * LANES] = _rope(
                dqb_ref[:, j * LANES:(j + 1) * LANES], cos, sa, sb, -1.0).astype(BF16)
        dz_ref[:, O_KB:O_KB + C_KB] = _rope(dkb_ref[...], cos, sa, sb, -1.0).astype(BF16)
        dz_ref[:, O_VB:O_VB + C_KB] = dvb_ref[...].astype(BF16)
        dz_ref[:, O_G:O_G + C_GATES] = dgt_ref[...].astype(BF16)
        du = _dot(dz_ref[...], w_ref[...])
        hf = h_ref[...]
        dx, dgr = _norm_bwd(du, hf, g_ref[...], _rstd(hf))
        dg_ref[...] += jnp.sum(dgr, axis=0, keepdims=True)
        dh_ref[...] = dres_ref[...] + dx

    def tok(n):
        return pl.BlockSpec((tm, n), lambda i: (i, 0))

    vec = pl.BlockSpec((1, D), lambda i: (0, 0))
    na = NA_HEADS * HEAD_DIM
    grid = (T // tm,)
    body, r_in, r_in_specs, r_out, r_out_specs, scratch = _ride(body, 14, 3, rider, grid, None)
    outs = pl.pallas_call(
        body, name="mix_in_bwd", grid=grid,
        in_specs=[tok(na), tok(na), tok(na), tok(C_QB), tok(C_KB), tok(C_KB), tok(C_GATES), tok(D), vec, tok(D),
                  pl.BlockSpec((D_IN, D), lambda i: (0, 0), pipeline_mode=pl.Buffered(1)), tok(LANES), tok(LANES), tok(LANES)]
        + r_in_specs,
        out_specs=[tok(D_IN), tok(D), vec] + r_out_specs,
        out_shape=[jax.ShapeDtypeStruct((T, D_IN), BF16), jax.ShapeDtypeStruct((T, D), F32),
                   jax.ShapeDtypeStruct((1, D), F32)] + r_out,
        scratch_shapes=scratch,
        compiler_params=_params(1),
    )(dqa, dka, dva, dqb, dkb, dvb, dgates, h, g, dres, w_in, *tables, *r_in)
    return (*outs[:3], list(outs[3:]))


def _na_bias_slabs(rpb):
    H = rpb.shape[0]
    ncell = GRID_W * GRID_W
    cell = np.arange(ncell)
    co = cell % GRID_W - cell // GRID_W + (NA_KW - 1)
    e_co = jnp.asarray((np.arange(LANES)[:, None] == co[None, :]).astype(np.float32))
    table = jnp.pad(rpb, ((0, 0), (0, 1), (0, LANES - rpb.shape[2]))).reshape(H * 2 * NA_KH, LANES)

    def body(t_ref, e_ref, o_ref):
        o_ref[...] = jnp.dot(t_ref[...], e_ref[...], preferred_element_type=F32, precision=lax.Precision.HIGHEST)

    toeplitz = pl.pallas_call(
        body, name="rpb_unfold", out_shape=jax.ShapeDtypeStruct((H * 2 * NA_KH, ncell), F32),
        compiler_params=_params(0),
    )(table, e_co).reshape(H, 2 * NA_KH, GRID_W, GRID_W)

    def assemble(tz_ref, o_ref):
        c = lax.broadcasted_iota(jnp.int32, (GRID_W, GRID_W), 0)
        k = lax.broadcasted_iota(jnp.int32, (GRID_W, GRID_W), 1)
        cs = jnp.clip(c - NA_KW // 2, 0, GRID_W - NA_KW)
        inwin = (k >= cs) & (k < cs + NA_KW)
        for ro0 in range(NA_KH):
            for hh in range(2):
                for i in range(NA_KH):
                    o_ref[0, ro0, hh * GRID_W:(hh + 1) * GRID_W, i * GRID_W:(i + 1) * GRID_W] = jnp.where(
                        inwin, tz_ref[hh, ro0 + i], NEG)

    return pl.pallas_call(
        assemble, name="na_bias_slabs", grid=(H // 2,),
        in_specs=[pl.BlockSpec((2, 2 * NA_KH, GRID_W, GRID_W), lambda p: (p, 0, 0, 0))],
        out_specs=pl.BlockSpec((1, NA_KH, 2 * GRID_W, NA_KH * GRID_W), lambda p: (p, 0, 0, 0)),
        out_shape=jax.ShapeDtypeStruct((H // 2, NA_KH, 2 * GRID_W, NA_KH * GRID_W), F32),
        compiler_params=_params(1),
    )(toeplitz)


def _half_masks(rows):
    lane = lax.broadcasted_iota(jnp.int32, (rows, LANES), 1)
    left = lane < HEAD_DIM
    return left, (left, jnp.logical_not(left))


def _stack_heads(x):
    left, halves = _half_masks(x.shape[0])
    xf = x.astype(F32)
    return jnp.concatenate([jnp.where(m, xf, 0.0).astype(BF16) for m in halves], axis=0)


def _unstack_heads(o):
    rows = o.shape[0] // 2
    left, _ = _half_masks(rows)
    return jnp.where(left, o[:rows], o[rows:])


def _na_row(j, t, rb, rows):
    r = j * rb + t
    rs = jnp.clip(r - NA_KH // 2, 0, rows - NA_KH)
    return pl.multiple_of(t * GRID_W, GRID_W), pl.multiple_of(rs * GRID_W, GRID_W), rs - r + (NA_KH - 1)


def _na_specs(T, rb):
    qrows = GRID_W * rb
    pairs = NA_HEADS // 2
    return ([pl.BlockSpec((qrows, LANES), lambda p, j: (j, p)),
             pl.BlockSpec((T, LANES), lambda p, j: (0, pairs + p)),
             pl.BlockSpec((T, LANES), lambda p, j: (0, 2 * pairs + p))],
            pl.BlockSpec((1, NA_KH, 2 * GRID_W, NA_KH * GRID_W), lambda p, j: (p, 0, 0, 0)))


def _softmax(s):
    p = jnp.exp(s - jnp.max(s, axis=-1, keepdims=True))
    return p / jnp.sum(p, axis=-1, keepdims=True)


def _na_fwd(qkva, bias, *, rb=16, group=16):
    T = qkva.shape[0]
    rows = T // GRID_W
    nkeys = NA_KH * GRID_W
    rb = min(rb, rows)
    group = min(group, rb)

    def body(q_ref, k_ref, v_ref, bias_ref, y_ref):
        j = pl.program_id(1)

        def rows_step(t, carry):
            at = [_na_row(j, t * group + u, rb, rows) for u in range(group)]
            s = [_dot_nt(_stack_heads(q_ref[pl.ds(q0, GRID_W), :]), k_ref[pl.ds(k0, nkeys), :]) for q0, k0, _ in at]
            p = [_softmax(su * QK_SCALE + bias_ref[0, ro0]) for su, (_, _, ro0) in zip(s, at)]
            o = [_dot(pu.astype(BF16), v_ref[pl.ds(k0, nkeys), :]) for pu, (_, k0, _) in zip(p, at)]
            for ou, (q0, _, _) in zip(o, at):
                y_ref[pl.ds(q0, GRID_W), :] = _unstack_heads(ou).astype(BF16)
            return carry

        lax.fori_loop(0, rb // group, rows_step, 0)

    qkv_specs, bias_spec = _na_specs(T, rb)
    return pl.pallas_call(
        body, name="na_fwd", grid=(NA_HEADS // 2, rows // rb),
        in_specs=qkv_specs + [bias_spec],
        out_specs=qkv_specs[0],
        out_shape=jax.ShapeDtypeStruct((T, NA_HEADS * HEAD_DIM), BF16),
        compiler_params=_params(2),
    )(qkva, qkva, qkva, bias)


def _na_bwd(qkva, dy, bias, *, rb=16, group=16, rider=None):
    T = qkva.shape[0]
    rows = T // GRID_W
    nkeys = NA_KH * GRID_W
    rb = min(rb, rows)
    group = min(group, rb)

    def body(q_ref, k_ref, v_ref, dy_ref, bias_ref, dq_ref, dk_ref, dv_ref, dbias_ref):
        j = pl.program_id(1)

        @pl.when(j == 0)
        def _():
            dk_ref[...] = jnp.zeros_like(dk_ref)
            dv_ref[...] = jnp.zeros_like(dv_ref)
            dbias_ref[...] = jnp.zeros_like(dbias_ref)

        def rows_step(t, carry):
            at = [_na_row(j, t * group + u, rb, rows) for u in range(group)]
            qs = [_stack_heads(q_ref[pl.ds(q0, GRID_W), :]) for q0, _, _ in at]
            dys = [_stack_heads(dy_ref[pl.ds(q0, GRID_W), :]) for q0, _, _ in at]
            s = [_dot_nt(qu, k_ref[pl.ds(k0, nkeys), :]) for qu, (_, k0, _) in zip(qs, at)]
            dp = [_dot_nt(du, v_ref[pl.ds(k0, nkeys), :]) for du, (_, k0, _) in zip(dys, at)]
            p = [_softmax(su * QK_SCALE + bias_ref[0, ro0]) for su, (_, _, ro0) in zip(s, at)]
            ds = [pu * (du - jnp.sum(pu * du, axis=-1, keepdims=True)) for pu, du in zip(p, dp)]
            for u, (q0, k0, ro0) in enumerate(at):
                dbias_ref[0, ro0] += ds[u]
                dsb = ds[u].astype(BF16)
                dq_ref[pl.ds(q0, GRID_W), :] = (_unstack_heads(_dot(dsb, k_ref[pl.ds(k0, nkeys), :])) * QK_SCALE).astype(BF16)
                dk_ref[pl.ds(k0, nkeys), :] += _dot_tn(dsb, qs[u]) * QK_SCALE
                dv_ref[pl.ds(k0, nkeys), :] += _dot_tn(p[u].astype(BF16), dys[u])
            return carry

        lax.fori_loop(0, rb // group, rows_step, 0)

    qkv_specs, bias_spec = _na_specs(T, rb)
    width = NA_HEADS * HEAD_DIM
    kv_out = pl.BlockSpec((T, LANES), lambda p, j: (0, p))
    grid = (NA_HEADS // 2, rows // rb)
    body, r_in, r_in_specs, r_out, r_out_specs, scratch = _ride(body, 5, 4, rider, grid, None)
    outs = pl.pallas_call(
        body, name="na_bwd", grid=grid,
        in_specs=qkv_specs + [qkv_specs[0], bias_spec] + r_in_specs,
        out_specs=[qkv_specs[0], kv_out, kv_out, bias_spec] + r_out_specs,
        out_shape=[jax.ShapeDtypeStruct((T, width), BF16), jax.ShapeDtypeStruct((T, width), F32),
                   jax.ShapeDtypeStruct((T, width), F32), jax.ShapeDtypeStruct(bias.shape, F32)] + r_out,
        scratch_shapes=scratch,
        compiler_params=_params(2),
    )(qkva, qkva, qkva, dy, bias, *r_in)
    return (*outs[:4], list(outs[4:]))


def _rpb_fold(dslab):
    pairs = dslab.shape[0]
    H = 2 * pairs
    ncell = GRID_W * GRID_W

    def disassemble(d_ref, tz_ref):
        tz_ref[...] = jnp.zeros_like(tz_ref)
        for ro0 in range(NA_KH):
            for hh in range(2):
                for i in range(NA_KH):
                    tz_ref[hh, ro0 + i] += d_ref[0, ro0, hh * GRID_W:(hh + 1) * GRID_W, i * GRID_W:(i + 1) * GRID_W]

    dtoeplitz = pl.pallas_call(
        disassemble, name="rpb_fold_tiles", grid=(pairs,),
        in_specs=[pl.BlockSpec((1, NA_KH, 2 * GRID_W, NA_KH * GRID_W), lambda p: (p, 0, 0, 0))],
        out_specs=pl.BlockSpec((2, 2 * NA_KH, GRID_W, GRID_W), lambda p: (p, 0, 0, 0)),
        out_shape=jax.ShapeDtypeStruct((H, 2 * NA_KH, GRID_W, GRID_W), F32),
        compiler_params=_params(1),
    )(dslab).reshape(H * 2 * NA_KH, ncell)
    cell = np.arange(ncell)
    co = cell % GRID_W - cell // GRID_W + (NA_KW - 1)
    e_co = jnp.asarray((co[:, None] == np.arange(LANES)[None, :]).astype(np.float32))

    def diagonals(x_ref, e_ref, o_ref):
        o_ref[...] = jnp.dot(x_ref[...], e_ref[...], preferred_element_type=F32, precision=lax.Precision.HIGHEST)

    return pl.pallas_call(
        diagonals, name="rpb_fold", out_shape=jax.ShapeDtypeStruct((H * 2 * NA_KH, LANES), F32),
        compiler_params=_params(0),
    )(dtoeplitz, e_co).reshape(H, 2 * NA_KH, LANES)


SWA_KEYS = 3 * WIN


def _swa_block(j, t, qbn, T):
    blk = j * qbn + t
    start = jnp.clip((blk - 1) * WIN, 0, T - SWA_KEYS)
    row = lax.broadcasted_iota(jnp.int32, (2 * WIN, SWA_KEYS), 0)
    qpos = blk * WIN + jnp.where(row < WIN, row, row - WIN)
    kpos = start + lax.broadcasted_iota(jnp.int32, (2 * WIN, SWA_KEYS), 1)
    return pl.multiple_of(t * WIN, WIN), pl.multiple_of(start, WIN), jnp.abs(qpos - kpos) <= WIN


def _swa_sinks(sink_ref, p):
    row = lax.broadcasted_iota(jnp.int32, (2 * WIN, 1), 0)
    return jnp.where(row < WIN, sink_ref[p], sink_ref[p + NB_HEADS // 2])


def _swa_probs(s, mask, sink):
    s = jnp.where(mask, s * QK_SCALE, NEG)
    m = jnp.maximum(jnp.max(s, axis=-1, keepdims=True), sink)
    e = jnp.exp(s - m)
    esink = jnp.exp(sink - m)
    den = jnp.sum(e, axis=-1, keepdims=True) + esink
    return e / den, esink / den


def _swa_specs(T, qbn):
    return [pl.BlockSpec(memory_space=pltpu.SMEM),
            pl.BlockSpec((WIN * qbn, LANES), lambda p, j: (j, p)),
            pl.BlockSpec((T, LANES), lambda p, j: (0, 0)),
            pl.BlockSpec((T, LANES), lambda p, j: (0, 1))]


def _swa_fwd(qb, kvb, sink, *, qbn=16, group=16):
    T = qb.shape[0]
    pairs = NB_HEADS // 2
    qbn = min(qbn, T // WIN)
    group = min(group, qbn)

    def body(sink_ref, q_ref, k_ref, v_ref, y_ref):
        p, j = pl.program_id(0), pl.program_id(1)
        sinks = _swa_sinks(sink_ref, p)

        def blocks_step(t, carry):
            at = [_swa_block(j, t * group + u, qbn, T) for u in range(group)]
            s = [_dot_nt(_stack_heads(q_ref[pl.ds(q0, WIN), :]), k_ref[pl.ds(k0, SWA_KEYS), :]) for q0, k0, _ in at]
            pr = [_swa_probs(su, mask, sinks)[0] for su, (_, _, mask) in zip(s, at)]
            o = [_dot(pu.astype(BF16), v_ref[pl.ds(k0, SWA_KEYS), :]) for pu, (_, k0, _) in zip(pr, at)]
            for ou, (q0, _, _) in zip(o, at):
                y_ref[pl.ds(q0, WIN), :] = _unstack_heads(ou).astype(BF16)
            return carry

        lax.fori_loop(0, qbn // group, blocks_step, 0)

    specs = _swa_specs(T, qbn)
    return pl.pallas_call(
        body, name="swa_fwd", grid=(pairs, T // (WIN * qbn)),
        in_specs=specs, out_specs=specs[1],
        out_shape=jax.ShapeDtypeStruct((T, NB_HEADS * HEAD_DIM), BF16),
        compiler_params=_params(2),
    )(sink, qb, kvb, kvb)


def _swa_bwd(qb, kvb, dy, sink, *, qbn=16, group=16, rider=None):
    T = qb.shape[0]
    pairs = NB_HEADS // 2
    qbn = min(qbn, T // WIN)
    group = min(group, qbn)

    def body(sink_ref, q_ref, k_ref, v_ref, dy_ref, dq_ref, dk_ref, dv_ref, dsink_ref):
        p, j = pl.program_id(0), pl.program_id(1)
        sinks = _swa_sinks(sink_ref, p)

        @pl.when((p == 0) & (j == 0))
        def _():
            dk_ref[...] = jnp.zeros_like(dk_ref)
            dv_ref[...] = jnp.zeros_like(dv_ref)

        @pl.when(j == 0)
        def _():
            dsink_ref[...] = jnp.zeros_like(dsink_ref)

        def blocks_step(t, carry):
            at = [_swa_block(j, t * group + u, qbn, T) for u in range(group)]
            qs = [_stack_heads(q_ref[pl.ds(q0, WIN), :]) for q0, _, _ in at]
            dys = [_stack_heads(dy_ref[pl.ds(q0, WIN), :]) for q0, _, _ in at]
            s = [_dot_nt(qu, k_ref[pl.ds(k0, SWA_KEYS), :]) for qu, (_, k0, _) in zip(qs, at)]
            dp = [_dot_nt(du, v_ref[pl.ds(k0, SWA_KEYS), :]) for du, (_, k0, _) in zip(dys, at)]
            probs = [_swa_probs(su, mask, sinks) for su, (_, _, mask) in zip(s, at)]
            for u, (q0, k0, _) in enumerate(at):
                pr, psink = probs[u]
                delta = jnp.sum(pr * dp[u], axis=-1, keepdims=True)
                dsb = (pr * (dp[u] - delta)).astype(BF16)
                dsk = psink * delta
                for hh in range(2):
                    dsink_ref[0, hh:hh + 1, :] += jnp.broadcast_to(-jnp.sum(dsk[hh * WIN:(hh + 1) * WIN]), (1, LANES))
                dq_ref[pl.ds(q0, WIN), :] = _unstack_heads(_dot(dsb, k_ref[pl.ds(k0, SWA_KEYS), :])) * QK_SCALE
                dk_ref[pl.ds(k0, SWA_KEYS), :] += _dot_tn(dsb, qs[u]) * QK_SCALE
                dv_ref[pl.ds(k0, SWA_KEYS), :] += _dot_tn(pr.astype(BF16), dys[u])
            return carry

        lax.fori_loop(0, qbn // group, blocks_step, 0)

    specs = _swa_specs(T, qbn)
    kv_out = pl.BlockSpec((T, LANES), lambda p, j: (0, 0))
    grid = (pairs, T // (WIN * qbn))
    body, r_in, r_in_specs, r_out, r_out_specs, scratch = _ride(body, 5, 4, rider, grid, None)
    outs = pl.pallas_call(
        body, name="swa_bwd", grid=grid,
        in_specs=specs + [specs[1]] + r_in_specs,
        out_specs=[specs[1], kv_out, kv_out, pl.BlockSpec((1, 8, LANES), lambda p, j: (p, 0, 0))] + r_out_specs,
        out_shape=[jax.ShapeDtypeStruct((T, NB_HEADS * HEAD_DIM), F32), jax.ShapeDtypeStruct((T, LANES), F32),
                   jax.ShapeDtypeStruct((T, LANES), F32), jax.ShapeDtypeStruct((pairs, 8, LANES), F32)] + r_out,
        scratch_shapes=scratch,
        compiler_params=_params(2),
    )(sink, qb, kvb, kvb, dy, *r_in)
    return (*outs[:4], list(outs[4:]))


def _merge_fwd(ya, yb, gates, wa, wb, wout, h, *, tm=512):
    T, D = h.shape
    W = ya.shape[1]

    def body(ya_ref, yb_ref, gt_ref, wa_ref, wb_ref, wo_ref, h_ref, h2_ref, mg_ref):
        pa = _dot(ya_ref[...], wa_ref[...])
        pb = _dot(yb_ref[...], wb_ref[...])
        mg = (jax.nn.sigmoid(gt_ref[:, 0:D]) * pa + jax.nn.sigmoid(gt_ref[:, D:2 * D]) * pb).astype(BF16)
        mg_ref[...] = mg
        h2_ref[...] = h_ref[...] + _dot(mg, wo_ref[...])

    def tok(n):
        return pl.BlockSpec((tm, n), lambda i: (i, 0))

    def full(r, c):
        return pl.BlockSpec((r, c), lambda i: (0, 0))

    return pl.pallas_call(
        body, name="merge_fwd", grid=(T // tm,),
        in_specs=[tok(W), tok(W), tok(2 * D), full(W, D), full(W, D), full(D, D), tok(D)],
        out_specs=[tok(D), tok(D)],
        out_shape=[jax.ShapeDtypeStruct((T, D), F32), jax.ShapeDtypeStruct((T, D), BF16)],
        compiler_params=_params(1),
    )(ya, yb, gates, wa, wb, wout, h)


def _merge_bwd(dh, ya, yb, gates, wa, wb, wout, *, tm=512, rider=None):
    T, D = dh.shape
    W = ya.shape[1]

    def body(dh_ref, ya_ref, yb_ref, gt_ref, wa_ref, wb_ref, wo_ref, dya_ref, dyb_ref, dpa_ref, dpb_ref, dgt_ref):
        dmg = _dot_nt(dh_ref[...].astype(BF16), wo_ref[...])
        for y_ref, w_ref, dy_ref, dp_ref, lo in ((ya_ref, wa_ref, dya_ref, dpa_ref, 0), (yb_ref, wb_ref, dyb_ref, dpb_ref, D)):
            sg = jax.nn.sigmoid(gt_ref[:, lo:lo + D])
            dp = (dmg * sg).astype(BF16)
            dp_ref[...] = dp
            dgt_ref[:, lo:lo + D] = (dmg * _dot(y_ref[...], w_ref[...]) * (sg * (1.0 - sg))).astype(BF16)
            dy_ref[...] = _dot_nt(dp, w_ref[...]).astype(BF16)

    def tok(n):
        return pl.BlockSpec((tm, n), lambda i: (i, 0))

    def full(r, c):
        return pl.BlockSpec((r, c), lambda i: (0, 0))

    grid = (T // tm,)
    body, r_in, r_in_specs, r_out, r_out_specs, scratch = _ride(body, 7, 5, rider, grid, None)
    outs = pl.pallas_call(
        body, name="merge_bwd", grid=grid,
        in_specs=[tok(D), tok(W), tok(W), tok(2 * D), full(W, D), full(W, D), full(D, D)] + r_in_specs,
        out_specs=[tok(W), tok(W), tok(D), tok(D), tok(2 * D)] + r_out_specs,
        out_shape=[jax.ShapeDtypeStruct((T, W), BF16), jax.ShapeDtypeStruct((T, W), BF16),
                   jax.ShapeDtypeStruct((T, D), BF16), jax.ShapeDtypeStruct((T, D), BF16),
                   jax.ShapeDtypeStruct((T, 2 * D), BF16)] + r_out,
        scratch_shapes=scratch,
        compiler_params=_params(1),
    )(dh, ya, yb, gates, wa, wb, wout, *r_in)
    return (*outs[:5], list(outs[5:]))


def _pair_heads(a, axis):
    shp = a.shape
    a = a.reshape(shp[:axis] + (2, NB_HEADS // 2, HEAD_DIM) + shp[axis + 1:])
    return jnp.swapaxes(a, axis, axis + 1).reshape(shp)


def _unpair_heads(a, axis):
    shp = a.shape
    a = a.reshape(shp[:axis] + (NB_HEADS // 2, 2, HEAD_DIM) + shp[axis + 1:])
    return jnp.swapaxes(a, axis, axis + 1).reshape(shp)


FFN1 = ("ffn1_w_gate", "ffn1_w_up", "ffn1_w_down")
FFN2 = ("ffn2_w_gate", "ffn2_w_up", "ffn2_w_down")
MIXER = ("w_in", "w_branch_a", "w_branch_b", "w_out")
BRANCH = MIXER[1:]


def _layer_grads(x, target, g1, f1, gmix, late, rpb, sink, g2, gfin, comm=None):
    T = x.shape[0]
    tables = _rope_tables(T)
    bias = _na_bias_slabs(rpb)

    comm = comm or _Local(late)
    h1, n1, hdn1, p1, q1, gathered = _ffn_fwd(x, g1, *f1, name="ffn1_fwd", rider=comm.late_rider)
    w_in_t, wa, wb, wout, f2 = comm.late(gathered)
    w_in_p = jnp.concatenate([w_in_t[:O_QB], _pair_heads(w_in_t[O_QB:O_KB], 0), w_in_t[O_KB:]], axis=0)
    wb_p = _pair_heads(wb, 0)
    u, qkva, qb, kvb, gates = _mix_in_fwd(h1, gmix, w_in_p, tables)
    ya = _na_fwd(qkva, bias)
    yb = _swa_fwd(qb, kvb, sink)
    h2, merged = _merge_fwd(ya, yb, gates, wa, wb_p, wout, h1)
    dh3, n2, hdn2, p2, q2, loss, dgfin, _ = _ffn_fwd(h2, g2, *f2, name="ffn2_fwd", head=(gfin, target))

    dh2, da2, db2, dg2, _ = _ffn_bwd(dh3, h2, g2, p2, q2, *f2, name="ffn2_bwd")
    df2 = [_wgrad_shard_a(da2, n2, name="ffn2_dwg")[0], _wgrad_shard_a(db2, n2, name="ffn2_dwu")[0],
           _wgrad_shard_a(hdn2, dh3, scale=0.5, name="ffn2_dwd")[0]]
    red2 = comm.reduce(FFN2, df2, tag="ffn2")
    dya, dyb, dpa, dpb, dgates, got = _merge_bwd(dh2, ya, yb, gates, wa, wb_p, wout, rider=red2.sibling)
    red2.partial(got)
    dwout = _wgrad_cols(merged, dh2, 1, name="dwout").reshape(N_CHIPS, D_MODEL // N_CHIPS, D_MODEL)
    dwa = _wgrad_cols(ya, dpa, N_CHIPS, name="dwa")
    dwb = _unpair_heads(_wgrad_cols(yb, dpb, N_CHIPS, name="dwb"), 1)
    redb = comm.reduce(BRANCH, [dwa, dwb, dwout], tag="branch")
    dqa, dka, dva, dbias, got = _na_bwd(qkva, dya, bias, rider=_two_riders(red2.chips, redb.sibling))
    red2.halves(got[:len(FFN2)])
    redb.partial(got[len(FFN2):])
    drpb = _rpb_fold(dbias)
    dqb, dkb, dvb, dsink, got = _swa_bwd(qb, kvb, dyb, sink, rider=_two_riders(red2.share, redb.chips))
    out = red2.result(got[:len(FFN2)])
    redb.halves(got[len(FFN2):])
    dz, dh1, dgmix, got = _mix_in_bwd(dqa, dka, dva, dqb, dkb, dvb, dgates, h1, gmix, dh2, w_in_p, tables, rider=redb.share)
    out.update(redb.result(got))
    dwin_p = _wgrad_rows(dz, u, 2, name="dwin")[0].reshape(D_IN, D_MODEL)
    dwin = jnp.concatenate([dwin_p[:O_QB], _unpair_heads(dwin_p[O_QB:O_KB], 0), dwin_p[O_KB:]], axis=0)
    dx, da1, db1, dg1, _ = _ffn_bwd(dh1, x, g1, p1, q1, *f1, name="ffn1_bwd")
    redw = comm.reduce(("w_in",), [dwin.reshape(N_CHIPS, D_IN // N_CHIPS, D_MODEL)], tag="w_in").partial_now()
    dwg1, got = _wgrad_shard_a(da1, n1, name="ffn1_dwg", rider=redw.chips)
    dwu1, got = _wgrad_shard_a(db1, n1, name="ffn1_dwu", rider=redw.halves(got).share)
    out.update(redw.result(got))
    red1 = comm.reduce(FFN1[:2], [dwg1, dwu1], tag="ffn1_gate_up").partial_now()
    dwd1, got = _wgrad_shard_a(hdn1, dh1, scale=0.5, name="ffn1_dwd", rider=red1.chips)
    out.update(red1.halves(got).result_now())
    out.update(comm.reduce(FFN1[2:], [dwd1], tag="ffn1_down").partial_now().halves_now().result_now())
    for names in (FFN2, ("w_in",), BRANCH[:2], BRANCH[2:], FFN1[:2], FFN1[2:]):
        comm.update(names, out)
    out.update(loss=loss, dx=dx, ffn1_norm=dg1, mix_norm=dgmix, ffn2_norm=dg2, final_norm=dgfin, na_rpb=drpb,
               sink_logit=dsink[:, 0:2, 0].T.reshape(NB_HEADS))
    return out


class _Local:
    late_rider = None

    def __init__(self, late):
        self._late = late

    def late(self, gathered):
        return self._late

    def reduce(self, names, grads, *, tag):
        return _LocalReduce(names, grads)

    def update(self, names, reduced):
        pass


class _LocalReduce:
    sibling = chips = share = None

    def __init__(self, names, grads):
        self._result = dict(zip(names, grads))

    def partial(self, got=None):
        return self

    halves = partial_now = halves_now = partial

    def result(self, got=None):
        return self._result

    result_now = result


ANY = pl.BlockSpec(memory_space=pl.ANY)


def _place():
    x, y, c = lax.axis_index("x"), lax.axis_index("y"), lax.axis_index("c")
    chips = [(1 - x, y), (x, 1 - y), (1 - x, 1 - y)]
    return x, y, c, 2 * x + y, chips


def _remote(src, dst, send_sems, recv_sems, k, device):
    return pltpu.make_async_remote_copy(src_ref=src, dst_ref=dst, send_sem=send_sems.at[k], recv_sem=recv_sems.at[k],
                                        device_id=device, device_id_type=MESH)


class _Rider:
    def __init__(self, inputs, out_shape, scratch, start, middle, finish):
        self.inputs, self.out_shape, self.scratch = list(inputs), list(out_shape), list(scratch)
        self.start, self.middle, self.finish = start, middle, finish


def _two_riders(first, second):
    if first is None and second is None:
        return None
    assert first.middle is None and second.middle is None
    n_in, n_out, n_sem = len(first.inputs), len(first.out_shape), len(first.scratch)

    def phase(name):
        def run(ins, outs, sems):
            getattr(first, name)(ins[:n_in], outs[:n_out], sems[:n_sem])
            getattr(second, name)(ins[n_in:], outs[n_out:], sems[n_sem:])
        return run

    return _Rider(first.inputs + second.inputs, first.out_shape + second.out_shape, first.scratch + second.scratch,
                  phase("start"), None, phase("finish"))


def _run_rider(rider, *, name):
    n_in, n_out = len(rider.inputs), len(rider.out_shape)

    def body(*refs):
        ins, outs, sems = refs[:n_in], refs[n_in:n_in + n_out], refs[n_in + n_out:]
        rider.start(ins, outs, sems)
        if rider.middle is not None:
            rider.middle(ins, outs, sems)
        rider.finish(ins, outs, sems)

    return pl.pallas_call(body, name=name, in_specs=[ANY] * n_in, out_specs=[ANY] * n_out, out_shape=rider.out_shape,
                          scratch_shapes=rider.scratch)(*rider.inputs)


def _ride(body, n_in, n_out, rider, grid, middle_step):
    if rider is None:
        return body, [], [], [], [], []
    r_in, r_out = len(rider.inputs), len(rider.out_shape)
    steps = math.prod(grid)

    def riding(*refs):
        ins, r_ins = refs[:n_in], refs[n_in:n_in + r_in]
        outs = refs[n_in + r_in:n_in + r_in + n_out]
        r_outs = refs[n_in + r_in + n_out:n_in + r_in + n_out + r_out]
        sems = refs[n_in + r_in + n_out + r_out:]
        step = pl.program_id(0)
        for axis in range(1, len(grid)):
            step = step * grid[axis] + pl.program_id(axis)

        @pl.when(step == 0)
        def _():
            rider.start(r_ins, r_outs, sems)

        body(*ins, *outs)

        if rider.middle is not None:
            @pl.when(step == middle_step)
            def _():
                rider.middle(r_ins, r_outs, sems)

        @pl.when(step == steps - 1)
        def _():
            rider.finish(r_ins, r_outs, sems)

    return riding, rider.inputs, [ANY] * r_in, rider.out_shape, [ANY] * r_out, rider.scratch


def _gather_rider(shards):
    n = len(shards)

    def plan(ins, outs, sems, kinds):
        send_sems, recv_sems, own_send_sems, own_recv_sems = sems
        x, y, c, mine, chips = _place()
        sibling = (x, y, 1 - c)
        made = {k: [] for k in kinds}
        for i in range(n):
            hr = shards[i].shape[0] // 2
            if "own" in made:
                made["own"].append(_remote(ins[i], outs[i].at[mine], own_send_sems, own_recv_sems, i, sibling))
            for j, (cx, cy) in enumerate(chips):
                here = outs[i].at[2 * cx + cy, pl.ds(c * hr, hr)]
                there = outs[i].at[2 * cx + cy, pl.ds((1 - c) * hr, hr)]
                if "sends" in made:
                    made["sends"].append(_remote(ins[i].at[pl.ds(c * hr, hr)], outs[i].at[mine, pl.ds(c * hr, hr)],
                                                 send_sems, recv_sems, 6 * i + j, (cx, cy, c)))
                if "landed" in made:
                    made["landed"].append(_remote(here, here, send_sems, recv_sems, 6 * i + j, (cx, cy, c)))
                if "passes" in made:
                    made["passes"].append(_remote(here, here, send_sems, recv_sems, 6 * i + 3 + j, sibling))
                if "others" in made:
                    made["others"].append(_remote(there, there, send_sems, recv_sems, 6 * i + 3 + j, sibling))
        return [made[k] for k in kinds]

    def start(ins, outs, sems):
        own, sends = plan(ins, outs, sems, ("own", "sends"))
        for cp in own + sends:
            cp.start()

    def middle(ins, outs, sems):
        landed, passes = plan(ins, outs, sems, ("landed", "passes"))
        for arrived, cp in zip(landed, passes):
            arrived.wait_recv()
            cp.start()

    def finish(ins, outs, sems):
        own, sends, passes, others = plan(ins, outs, sems, ("own", "sends", "passes", "others"))
        for arrived in others:
            arrived.wait_recv()
        for cp in sends + passes:
            cp.wait_send()
        for cp in own:
            cp.wait()

    return _Rider(shards, [jax.ShapeDtypeStruct((N_CHIPS,) + s.shape, s.dtype) for s in shards],
                  [pltpu.SemaphoreType.DMA((6 * n,)), pltpu.SemaphoreType.DMA((6 * n,)),
                   pltpu.SemaphoreType.DMA((n,)), pltpu.SemaphoreType.DMA((n,))], start, middle, finish)


def _swap_rider(arrays, out_shape, source):
    n = len(arrays)

    def plan(ins, outs, sems):
        send_sems, recv_sems = sems
        x, y, c, _, _ = _place()
        return [_remote(source(ins[i], c, i), outs[i], send_sems, recv_sems, i, (x, y, 1 - c)) for i in range(n)]

    def start(ins, outs, sems):
        for cp in plan(ins, outs, sems):
            cp.start()

    def finish(ins, outs, sems):
        for cp in plan(ins, outs, sems):
            cp.wait()

    return _Rider(arrays, out_shape, [pltpu.SemaphoreType.DMA((n,)), pltpu.SemaphoreType.DMA((n,))], start, None, finish)


def _sibling_rider(grads):
    half = [g.shape[1] // 2 for g in grads]
    return _swap_rider(grads, [jax.ShapeDtypeStruct((g.shape[0], hr, g.shape[2]), g.dtype) for g, hr in zip(grads, half)],
                       lambda ref, c, i: ref.at[:, pl.ds((1 - c) * half[i], half[i])])


def _share_rider(halves):
    return _swap_rider(halves, [jax.ShapeDtypeStruct(h.shape, h.dtype) for h in halves], lambda ref, c, i: ref)


def _chips_rider(parts):
    n = len(parts)

    def plan(ins, outs, sems):
        send_sems, recv_sems = sems
        _, _, c, _, chips = _place()
        return [_remote(ins[i].at[2 * cx + cy], outs[i].at[j], send_sems, recv_sems, 3 * i + j, (cx, cy, c))
                for i in range(n) for j, (cx, cy) in enumerate(chips)]

    def start(ins, outs, sems):
        for cp in plan(ins, outs, sems):
            cp.start()

    def finish(ins, outs, sems):
        for cp in plan(ins, outs, sems):
            cp.wait()

    return _Rider(parts, [jax.ShapeDtypeStruct((N_CHIPS - 1,) + p.shape[1:], p.dtype) for p in parts],
                  [pltpu.SemaphoreType.DMA((3 * n,)), pltpu.SemaphoreType.DMA((3 * n,))], start, None, finish)


class _Reduce:
    def __init__(self, names, grads, cidx, chip, *, tag):
        self.names, self.grads, self.cidx, self.chip, self.tag = names, grads, cidx, chip, tag
        self.sibling = _sibling_rider(grads)

    def _by_shape(self, fn, *lists):
        done, i = [], 0
        while i < len(self.names):
            j = i + 1
            while j < len(self.names) and self.grads[j].shape == self.grads[i].shape:
                j += 1
            done += fn(*[lst[i:j] for lst in lists], self.names[i])
            i = j
        return done

    def partial(self, from_sibling):
        self.from_sibling = from_sibling
        self.chips = _chips_rider(self._by_shape(
            lambda g, r, k: _add_sibling(g, r, self.cidx, name="add_sibling_" + k), self.grads, from_sibling))
        return self

    def halves(self, from_chips):
        self.mine = self._by_shape(
            lambda g, r1, r2, k: _add_chips(g, r1, r2, self.cidx, self.chip, name="add_chips_" + k),
            self.grads, self.from_sibling, from_chips)
        self.share = _share_rider(self.mine)
        return self

    def result(self, others):
        return dict(zip(self.names, zip(self.mine, others)))

    def partial_now(self):
        return self.partial(_run_rider(self.sibling, name="rs_sibling_" + self.tag))

    def halves_now(self):
        return self.halves(_run_rider(self.chips, name="rs_chips_" + self.tag))

    def result_now(self):
        return self.result(_run_rider(self.share, name="rs_share_" + self.tag))


N_DEV = 8


def _small_allreduce(vec):
    R = vec.shape[0]

    def body(v_ref, o_ref, buf, send_sems, recv_sems):
        x, y, c, _, _ = _place()
        me = 4 * x + 2 * y + c
        buf[me] = v_ref[...]
        copies = []
        for k in range(1, N_DEV):
            peer = (x ^ (k >> 2), y ^ ((k >> 1) & 1), c ^ (k & 1))
            cp = _remote(v_ref, buf.at[me], send_sems, recv_sems, k - 1, peer)
            cp.start()
            copies.append(cp)
        for k, cp in enumerate(copies, start=1):
            cp.wait_send()
            landed = buf.at[me ^ k]
            _remote(landed, landed, send_sems, recv_sems, k - 1, (x, y, c)).wait_recv()
        acc = buf[0]
        for d in range(1, N_DEV):
            acc = acc + buf[d]
        o_ref[...] = acc

    return pl.pallas_call(
        body, name="small_allreduce",
        in_specs=[pl.BlockSpec(memory_space=pltpu.VMEM)], out_specs=pl.BlockSpec(memory_space=pltpu.VMEM),
        out_shape=jax.ShapeDtypeStruct(vec.shape, vec.dtype),
        scratch_shapes=[pltpu.VMEM((N_DEV, R, LANES), F32), pltpu.SemaphoreType.DMA((N_DEV - 1,)),
                        pltpu.SemaphoreType.DMA((N_DEV - 1,))],
    )(vec)


ELEMWISE_BLOCK = 512 * 1024


def _row_tile(rows, cols):
    best = None
    for t in range(16, rows + 1, 16):
        if rows % t == 0 and t * cols <= ELEMWISE_BLOCK:
            best = t
    return best if best is not None else rows


def _add_sibling(gs, r1s, cidx, *, name):
    n = len(gs)
    S, R, C = gs[0].shape
    hr = R // 2
    tr = _row_tile(hr, C)
    nt = hr // tr

    def body(c_ref, *refs):
        for g_ref, r_ref, o_ref in zip(refs[:n], refs[n:2 * n], refs[2 * n:]):
            o_ref[...] = (g_ref[...] + r_ref[...]).astype(BF16)

    blk = pl.BlockSpec((1, tr, C), lambda s, t, c: (s, t, 0))
    mine = pl.BlockSpec((1, tr, C), lambda s, t, c: (s, c[0] * nt + t, 0))
    return list(pl.pallas_call(
        body, name=name,
        grid_spec=pltpu.PrefetchScalarGridSpec(
            num_scalar_prefetch=1, grid=(S, nt), in_specs=[mine] * n + [blk] * n, out_specs=[blk] * n),
        out_shape=[jax.ShapeDtypeStruct((S, hr, C), BF16)] * n,
        compiler_params=_params(2),
    )(cidx, *gs, *r1s))


def _add_chips(gs, r1s, r2s, cidx, chip, *, name):
    n = len(gs)
    _, R, C = gs[0].shape
    hr = R // 2
    tr = _row_tile(hr, C)
    nt = hr // tr

    def body(pos_ref, *refs):
        for g_ref, r1_ref, r2_ref, o_ref in zip(refs[:n], refs[n:2 * n], refs[2 * n:3 * n], refs[3 * n:]):
            own = g_ref[0] + r1_ref[0]
            o_ref[...] = ((own + r2_ref[0].astype(F32)) + r2_ref[1].astype(F32)) + r2_ref[2].astype(F32)

    pos = jnp.concatenate([cidx, chip])
    return list(pl.pallas_call(
        body, name=name,
        grid_spec=pltpu.PrefetchScalarGridSpec(
            num_scalar_prefetch=1, grid=(nt,),
            in_specs=[pl.BlockSpec((1, tr, C), lambda t, pos: (pos[1], pos[0] * nt + t, 0))] * n
            + [pl.BlockSpec((1, tr, C), lambda t, pos: (pos[1], t, 0))] * n
            + [pl.BlockSpec((N_CHIPS - 1, tr, C), lambda t, pos: (0, t, 0))] * n,
            out_specs=[pl.BlockSpec((tr, C), lambda t, pos: (t, 0))] * n),
        out_shape=[jax.ShapeDtypeStruct((hr, C), F32)] * n,
        compiler_params=_params(1),
    )(pos, *gs, *r1s, *r2s))


def _adamw_math(w, g, m, v):
    mn = ADAM_B1 * m + (1.0 - ADAM_B1) * g
    vn = ADAM_B2 * v + (1.0 - ADAM_B2) * (g * g)
    m_hat = mn / (1.0 - ADAM_B1 ** ADAM_STEP)
    v_hat = vn / (1.0 - ADAM_B2 ** ADAM_STEP)
    return -ADAM_LR * (m_hat / (jnp.sqrt(v_hat) + ADAM_EPS) + ADAM_WD * w), mn, vn


def _adamw_halves(ws, mines, others, ms, vs, cidx, *, name):
    n = len(ws)
    R, C = ws[0].shape
    hr = R // 2
    tr = _row_tile(hr, C * n)
    nt = hr // tr

    def body(c_ref, *refs):
        for i in range(n):
            w_ref, a_ref, b_ref, m_ref, v_ref = (refs[j * n + i] for j in range(5))
            g_ref, d_ref, mo_ref, vo_ref = (refs[(5 + j) * n + i] for j in range(4))
            gv = jnp.where(pl.program_id(0) == c_ref[0], a_ref[...], b_ref[...])
            g_ref[...] = gv
            d_ref[...], mo_ref[...], vo_ref[...] = _adamw_math(w_ref[...], gv, m_ref[...], v_ref[...])

    full = pl.BlockSpec((tr, C), lambda h, t, c: (h * nt + t, 0))
    own = pl.BlockSpec((tr, C), lambda h, t, c: (jnp.where(h == c[0], t, 0), 0))
    sib = pl.BlockSpec((tr, C), lambda h, t, c: (jnp.where(h == c[0], 0, t), 0))
    shape = jax.ShapeDtypeStruct((R, C), F32)
    outs = pl.pallas_call(
        body, name=name,
        grid_spec=pltpu.PrefetchScalarGridSpec(
            num_scalar_prefetch=1, grid=(2, nt),
            in_specs=[full] * n + [own] * n + [sib] * n + [full] * (2 * n), out_specs=[full] * (4 * n)),
        out_shape=[shape] * (4 * n),
        compiler_params=_params(2),
    )(cidx, *ws, *mines, *others, *ms, *vs)
    return [list(outs[j * n:(j + 1) * n]) for j in range(4)]


def _adamw_small(ws, gs, ms, vs):
    n = len(ws)

    def body(*refs):
        for i in range(n):
            w_ref, g_ref, m_ref, v_ref = (refs[j * n + i] for j in range(4))
            d_ref, mo_ref, vo_ref = (refs[(4 + j) * n + i] for j in range(3))
            d_ref[...], mo_ref[...], vo_ref[...] = _adamw_math(w_ref[...], g_ref[...], m_ref[...], v_ref[...])

    shapes = [jax.ShapeDtypeStruct(a.shape, F32) for a in ws]
    outs = pl.pallas_call(body, name="adamw_small", out_shape=shapes * 3, compiler_params=_params(0))(*ws, *gs, *ms, *vs)
    return outs[:n], outs[n:2 * n], outs[2 * n:]


def _unstack_cols(w):
    s, r, c = w.shape
    return w.transpose(1, 0, 2).reshape(r, s * c)


def _pad_rows(a, rows):
    return jnp.pad(a, ((0, rows - a.shape[0]), (0, LANES - a.shape[1])))


BIG = ("ffn1_w_gate", "ffn1_w_up", "ffn1_w_down", "w_in", "w_branch_a", "w_branch_b", "w_out",
       "ffn2_w_gate", "ffn2_w_up", "ffn2_w_down")
TRANSPOSED = ("ffn1_w_gate", "ffn1_w_up", "w_in", "ffn2_w_gate", "ffn2_w_up")
WEIGHTS = ("ffn1_norm", "ffn1_w_gate", "ffn1_w_up", "ffn1_w_down", "mix_norm", "w_in", "na_rpb", "sink_logit",
           "w_branch_a", "w_branch_b", "w_out", "ffn2_norm", "ffn2_w_gate", "ffn2_w_up", "ffn2_w_down", "final_norm")


def kernel(x, ffn1_norm, ffn1_w_gate, ffn1_w_up, ffn1_w_down, mix_norm, w_in, na_rpb, sink_logit, w_branch_a, w_branch_b, w_out, ffn2_norm, ffn2_w_gate, ffn2_w_up, ffn2_w_down, final_norm, loss_target, m_ffn1_norm, m_ffn1_w_gate, m_ffn1_w_up, m_ffn1_w_down, m_mix_norm, m_w_in, m_na_rpb, m_sink_logit, m_w_branch_a, m_w_branch_b, m_w_out, m_ffn2_norm, m_ffn2_w_gate, m_ffn2_w_up, m_ffn2_w_down, m_final_norm, v_ffn1_norm, v_ffn1_w_gate, v_ffn1_w_up, v_ffn1_w_down, v_mix_norm, v_w_in, v_na_rpb, v_sink_logit, v_w_branch_a, v_w_branch_b, v_w_out, v_ffn2_norm, v_ffn2_w_gate, v_ffn2_w_up, v_ffn2_w_down, v_final_norm):
    args = dict(locals())
    w = {k: args[k] for k in WEIGHTS}
    mom = {k: args["m_" + k] for k in WEIGHTS}
    var = {k: args["v_" + k] for k in WEIGHTS}
    cidx = lax.axis_index("c").astype(jnp.int32).reshape(1)
    chip = (2 * lax.axis_index("x") + lax.axis_index("y")).astype(jnp.int32).reshape(1)

    def shard(a, k):
        return jnp.swapaxes(a[0], 0, 1) if k in TRANSPOSED else a[0]

    def unshard(a, k):
        return (jnp.swapaxes(a, 0, 1) if k in TRANSPOSED else a)[None]

    def bf16_shards(names):
        return [shard(w[k], k).astype(BF16) for k in names]

    class comm:
        late_rider = _gather_rider(bf16_shards(MIXER + FFN2))

        @staticmethod
        def late(gathered):
            full = dict(zip(MIXER + FFN2, gathered))
            return (full["w_in"].reshape(D_IN, D_MODEL), _unstack_cols(full["w_branch_a"]), _unstack_cols(full["w_branch_b"]),
                    full["w_out"].reshape(D_MODEL, D_MODEL), tuple(full[k] for k in FFN2))

        @staticmethod
        def reduce(names, grads, *, tag):
            return _Reduce(names, grads, cidx, chip, tag=tag)

        @staticmethod
        def update(names, reduced):
            res = _adamw_halves([shard(w[k], k) for k in names], [reduced[k][0] for k in names],
                                [reduced[k][1] for k in names], [shard(mom[k], k) for k in names],
                                [shard(var[k], k) for k in names], cidx, name="adamw_" + names[0])
            for i, k in enumerate(names):
                grads_out[k], deltas[k], new_m[k], new_v[k] = (unshard(a[i], k) for a in res)

    deltas, new_m, new_v, grads_out, grad = {}, {}, {}, {}, {}
    f1 = _run_rider(_gather_rider(bf16_shards(FFN1)), name="all_gather_ffn1")
    out = _layer_grads(x[0], loss_target[0], ffn1_norm, f1, mix_norm, None, na_rpb[0], sink_logit[0], ffn2_norm,
                       final_norm.reshape(1, D_MODEL), comm=comm)

    rows = D_MODEL // LANES
    small = jnp.concatenate([
        out["ffn1_norm"].reshape(rows, LANES), out["mix_norm"].reshape(rows, LANES), out["ffn2_norm"].reshape(rows, LANES),
        out["final_norm"].reshape(rows, LANES), out["na_rpb"].reshape(-1, LANES),
        _pad_rows(out["sink_logit"].reshape(1, NB_HEADS), 8), _pad_rows(out["loss"], 8)], axis=0)
    total = _small_allreduce(small)
    n_rpb = NA_HEADS * 2 * NA_KH
    grad["ffn1_norm"] = total[0:rows].reshape(1, D_MODEL)
    grad["mix_norm"] = total[rows:2 * rows].reshape(1, D_MODEL)
    grad["ffn2_norm"] = total[2 * rows:3 * rows].reshape(1, D_MODEL)
    grad["final_norm"] = total[3 * rows:4 * rows].reshape(1, D_MODEL)
    grad["na_rpb"] = total[4 * rows:4 * rows + n_rpb].reshape(NA_HEADS, 2 * NA_KH, LANES)[:, :2 * NA_KH - 1, :2 * NA_KW - 1]
    grad["na_rpb"] = grad["na_rpb"].reshape(NA_HEADS, -1)
    grad["sink_logit"] = total[4 * rows + n_rpb:4 * rows + n_rpb + 1, 0:NB_HEADS]
    loss = total[4 * rows + n_rpb + 8, 0]

    small_names = [k for k in WEIGHTS if k not in BIG]
    res = _adamw_small(*[[a[k].reshape(grad[k].shape) for k in small_names] for a in (w, grad, mom, var)])
    for i, k in enumerate(small_names):
        grads_out[k], deltas[k], new_m[k], new_v[k] = (a.reshape(w[k].shape) for a in (grad[k], res[0][i], res[1][i], res[2][i]))
    return (loss, out["dx"].reshape(x.shape), *[grads_out[k] for k in WEIGHTS], *[deltas[k] for k in WEIGHTS],
            *[new_m[k] for k in WEIGHTS], *[new_v[k] for k in WEIGHTS])
```

```python
import math

import jax
import jax.numpy as jnp
import numpy as np
from jax import lax
from jax.experimental import pallas as pl
from jax.experimental.pallas import tpu as pltpu

F32 = jnp.float32
BF16 = jnp.bfloat16

D_MODEL = 1024
HEAD_DIM = 64
NA_HEADS = 8
NB_HEADS = 8
GRID_W = 64
NA_KH = 8
NA_KW = 16
WIN = 128
ROPE_THETA = 10000.0
EPS = 1e-6
N_CHIPS = 4
QK_SCALE = HEAD_DIM ** -0.5
NEG = -1e30
LANES = 128
VMEM_LIMIT = 56 * 1024 * 1024
HEAD_ROWS = 256
WGRAD_TOKENS_BYTES = 8192

C_QKVA = 3 * NA_HEADS * HEAD_DIM
C_QB = NB_HEADS * HEAD_DIM
C_KB = 2 * HEAD_DIM
C_ROPE = C_QB + C_KB
C_GATES = 2 * D_MODEL
D_IN = C_QKVA + C_QB + 2 * C_KB + C_GATES
O_QB = C_QKVA
O_KB = O_QB + C_QB
O_VB = O_KB + C_KB
O_G = O_VB + C_KB

ADAM_LR = 0.001
ADAM_B1 = 0.9
ADAM_B2 = 0.999
ADAM_EPS = 1e-08
ADAM_WD = 0.01
ADAM_STEP = 10

MESH = pl.DeviceIdType.MESH


def _dot(a, b):
    return jnp.dot(a, b, preferred_element_type=F32)


def _dot_nt(a, b):
    return lax.dot_general(a, b, (((1,), (1,)), ((), ())), preferred_element_type=F32)


def _dot_tn(a, b):
    return lax.dot_general(a, b, (((0,), (0,)), ((), ())), preferred_element_type=F32)


def _params(n_axes):
    return pltpu.CompilerParams(dimension_semantics=("arbitrary",) * n_axes, vmem_limit_bytes=VMEM_LIMIT)


def _rstd(xf):
    return lax.rsqrt(jnp.mean(xf * xf, axis=-1, keepdims=True) + EPS)


def _norm_bwd(dn, xf, g, r):
    xhat = xf * r
    dxh = dn * g
    dx = r * (dxh - xhat * jnp.mean(dxh * xhat, axis=-1, keepdims=True))
    return dx, dn * xhat


def _sigmoid(x):
    return 0.5 * jnp.tanh(0.5 * x) + 0.5


def _loss_head(hf, gv, tgt):
    r = _rstd(hf)
    err = (hf * r) * gv - tgt
    dx, dgr = _norm_bwd(err * (1.0 / hf.shape[-1]), hf, gv, r)
    return 0.5 * jnp.mean(err * err, axis=-1, keepdims=True), dx, dgr


def _ffn_fwd(x, g, wg, wu, wd, *, name, tm=1024, sub=512, rider=None, head=None):
    T, D = x.shape
    F = wg.shape[1]
    tm = min(tm, T)
    sub = min(sub, tm)
    n_head = 0 if head is None else 2

    def body(*refs):
        x_ref, g_ref, wg_ref, wu_ref, wd_ref = refs[:5]
        h_ref, n_ref, hdn_ref, p_ref, q_ref = refs[5 + n_head:10 + n_head]
        i, s = pl.program_id(0), pl.program_id(1)
        _ffn_fwd_step(x_ref, g_ref, wg_ref, wu_ref, wd_ref, h_ref, n_ref, hdn_ref, p_ref, q_ref, s)
        if head is not None:
            gf_ref, t_ref = refs[5:7]
            loss_ref, dgf_ref = refs[10 + n_head:]

            @pl.when((i == 0) & (s == 0))
            def _():
                loss_ref[...] = jnp.zeros_like(loss_ref)
                dgf_ref[...] = jnp.zeros_like(dgf_ref)

            @pl.when(s == N_CHIPS - 1)
            def _():
                for u in range(tm // HEAD_ROWS):
                    r = pl.ds(u * HEAD_ROWS, HEAD_ROWS)
                    terms, dh, dgr = _loss_head(h_ref[r, :], gf_ref[...], t_ref[r, :])
                    loss_ref[...] += jnp.broadcast_to(jnp.sum(terms), loss_ref.shape)
                    dgf_ref[...] += jnp.sum(dgr, axis=0, keepdims=True)
                    h_ref[r, :] = dh

    def _ffn_fwd_step(x_ref, g_ref, wg_ref, wu_ref, wd_ref, h_ref, n_ref, hdn_ref, p_ref, q_ref, s):

        @pl.when(s == 0)
        def _():
            xf = x_ref[...]
            n_ref[...] = ((xf * _rstd(xf)) * g_ref[...]).astype(BF16)
            h_ref[...] = xf

        rows = [pl.ds(u * sub, sub) for u in range(tm // sub)]
        ab = [(_dot_nt(n_ref[r, :], wg_ref[0]), _dot_nt(n_ref[r, :], wu_ref[0])) for r in rows]
        hdns = []
        for r, (a, b) in zip(rows, ab):
            sg = _sigmoid(a)
            silu = a * sg
            hdn = (silu * b).astype(BF16)
            hdn_ref[0, r, :] = hdn
            p_ref[0, r, :] = (b * (sg + silu * (1.0 - sg))).astype(BF16)
            q_ref[0, r, :] = silu.astype(BF16)
            hdns.append(hdn)
        for r, hdn in zip(rows, hdns):
            h_ref[r, :] += 0.5 * _dot(hdn, wd_ref[0])

    tok = pl.BlockSpec((tm, D), lambda i, s: (i, 0))
    hid = pl.BlockSpec((1, tm, F), lambda i, s: (s, i, 0))
    wspec = pl.BlockSpec((1, F, D), lambda i, s: (s, 0, 0))
    hshape = jax.ShapeDtypeStruct((N_CHIPS, T, F), BF16)
    grid = (T // tm, N_CHIPS)
    vec = pl.BlockSpec((1, D), lambda i, s: (0, 0))
    head_in, head_in_specs, head_out, head_out_specs = [], [], [], []
    if head is not None:
        head_in, head_in_specs = list(head), [vec, tok]
        head_out = [jax.ShapeDtypeStruct((1, LANES), F32), jax.ShapeDtypeStruct((1, D), F32)]
        head_out_specs = [pl.BlockSpec((1, LANES), lambda i, s: (0, 0)), vec]
    n_main = 5 + n_head
    body, r_in, r_in_specs, r_out, r_out_specs, scratch = _ride(body, n_main, n_main, rider, grid, (grid[0] * grid[1] * 7) // 8)
    outs = pl.pallas_call(
        body, name=name, grid=grid,
        in_specs=[tok, vec, wspec, wspec, wspec] + head_in_specs + r_in_specs,
        out_specs=[tok, tok, hid, hid, hid] + head_out_specs + r_out_specs,
        out_shape=[jax.ShapeDtypeStruct((T, D), F32), jax.ShapeDtypeStruct((T, D), BF16), hshape, hshape, hshape]
        + head_out + r_out,
        scratch_shapes=scratch,
        compiler_params=_params(2),
    )(x, g, wg, wu, wd, *head_in, *r_in)
    return (*outs[:n_main], list(outs[n_main:]))


def _ffn_bwd(dh, x, g, p, q, wg, wu, wd, *, name, tm=1024, sub=256, rider=None):
    T, D = x.shape
    F = wg.shape[1]
    tm = min(tm, T)
    sub = min(sub, tm)

    def body(dh_ref, x_ref, g_ref, p_ref, q_ref, wg_ref, wu_ref, wd_ref, dx_ref, da_ref, db_ref, dg_ref):
        i, s = pl.program_id(0), pl.program_id(1)

        @pl.when((i == 0) & (s == 0))
        def _():
            dg_ref[...] = jnp.zeros_like(dg_ref)

        @pl.when(s == 0)
        def _():
            dx_ref[...] = jnp.zeros_like(dx_ref)

        rows = [pl.ds(u * sub, sub) for u in range(tm // sub)]
        dhdn = [_dot_nt((0.5 * dh_ref[r, :]).astype(BF16), wd_ref[0]) for r in rows]
        das, dbs = [], []
        for r, dd in zip(rows, dhdn):
            da = (dd * p_ref[0, r, :].astype(F32)).astype(BF16)
            db = (dd * q_ref[0, r, :].astype(F32)).astype(BF16)
            da_ref[0, r, :] = da
            db_ref[0, r, :] = db
            das.append(da)
            dbs.append(db)
        for r, da, db in zip(rows, das, dbs):
            dx_ref[r, :] += _dot(da, wg_ref[0]) + _dot(db, wu_ref[0])

        @pl.when(s == N_CHIPS - 1)
        def _():
            xf = x_ref[...]
            dx, dgr = _norm_bwd(dx_ref[...], xf, g_ref[...], _rstd(xf))
            dg_ref[...] += jnp.sum(dgr, axis=0, keepdims=True)
            dx_ref[...] = dh_ref[...] + dx

    tok = pl.BlockSpec((tm, D), lambda i, s: (i, 0))
    hid = pl.BlockSpec((1, tm, F), lambda i, s: (s, i, 0))
    vec = pl.BlockSpec((1, D), lambda i, s: (0, 0))
    hshape = jax.ShapeDtypeStruct((N_CHIPS, T, F), BF16)
    wspec = pl.BlockSpec((1, F, D), lambda i, s: (s, 0, 0))
    grid = (T // tm, N_CHIPS)
    body, r_in, r_in_specs, r_out, r_out_specs, scratch = _ride(body, 8, 4, rider, grid, None)
    outs = pl.pallas_call(
        body, name=name, grid=grid,
        in_specs=[tok, tok, vec, hid, hid, wspec, wspec, wspec] + r_in_specs,
        out_specs=[tok, hid, hid, vec] + r_out_specs,
        out_shape=[jax.ShapeDtypeStruct((T, D), F32), hshape, hshape, jax.ShapeDtypeStruct((1, D), F32)] + r_out,
        scratch_shapes=scratch,
        compiler_params=_params(2),
    )(dh, x, g, p, q, wg, wu, wd, *r_in)
    return (*outs[:4], list(outs[4:]))


def _wgrad(a, b, *, a_block, a_map, b_block, b_map, out_shape, o_block, o_map, grid, scale=1.0, name, rider=None):
    def body(a_ref, b_ref, o_ref):
        @pl.when(pl.program_id(len(grid) - 1) == 0)
        def _():
            o_ref[...] = jnp.zeros_like(o_ref)

        av = a_ref[...]
        bv = b_ref[...]
        av = av.reshape(av.shape[-2:]).astype(BF16)
        bv = bv.reshape(bv.shape[-2:])
        if scale != 1.0:
            bv = scale * bv
        o_ref[...] += _dot_tn(av, bv.astype(BF16)).reshape(o_ref.shape)

    body, r_in, r_in_specs, r_out, r_out_specs, scratch = _ride(body, 2, 1, rider, grid, None)
    outs = pl.pallas_call(
        body, name=name, grid=grid,
        in_specs=[pl.BlockSpec(a_block, a_map), pl.BlockSpec(b_block, b_map)] + r_in_specs,
        out_specs=[pl.BlockSpec(o_block, o_map)] + r_out_specs,
        out_shape=[jax.ShapeDtypeStruct(out_shape, F32)] + r_out,
        scratch_shapes=scratch,
        compiler_params=_params(len(grid)),
    )(a, b, *r_in)
    return outs[0], list(outs[1:])


def _wgrad_rows(a, b, n_blocks, *, name, tk=2048):
    T, N = b.shape
    M = a.shape[1] // n_blocks
    tk = min(tk, T)
    return _wgrad(a, b, a_block=(tk, M), a_map=lambda s, k: (k, s), b_block=(tk, N), b_map=lambda s, k: (k, 0),
                  out_shape=(n_blocks, M, N), o_block=(1, M, N), o_map=lambda s, k: (s, 0, 0), grid=(n_blocks, T // tk), name=name)


def _wgrad_shard_a(a, b, *, name, scale=1.0, rider=None):
    S, T, M = a.shape
    N = b.shape[1]
    tk = min(WGRAD_TOKENS_BYTES // b.dtype.itemsize, T)
    return _wgrad(a, b, a_block=(1, tk, M), a_map=lambda s, k: (s, k, 0), b_block=(tk, N), b_map=lambda s, k: (k, 0),
                  out_shape=(S, M, N), o_block=(1, M, N), o_map=lambda s, k: (s, 0, 0), grid=(S, T // tk), scale=scale,
                  name=name, rider=rider)


def _wgrad_cols(a, b, n_blocks, *, name, tk=2048):
    T, M = a.shape
    N = b.shape[1] // n_blocks
    tk = min(tk, T)

    def body(a_ref, b_ref, o_ref):
        @pl.when(pl.program_id(0) == 0)
        def _():
            o_ref[...] = jnp.zeros_like(o_ref)

        r = _dot_tn(a_ref[...].astype(BF16), b_ref[...].astype(BF16))
        for s in range(n_blocks):
            o_ref[s] += r[:, s * N:(s + 1) * N]

    return pl.pallas_call(
        body, name=name, grid=(T // tk,),
        in_specs=[pl.BlockSpec((tk, M), lambda k: (k, 0)), pl.BlockSpec((tk, n_blocks * N), lambda k: (k, 0))],
        out_specs=pl.BlockSpec((n_blocks, M, N), lambda k: (0, 0, 0)),
        out_shape=jax.ShapeDtypeStruct((n_blocks, M, N), F32),
        compiler_params=_params(1),
    )(a, b)


def _rope_tables(T):
    half = HEAD_DIM // 2
    inv = np.float32(ROPE_THETA) ** (-np.arange(half, dtype=np.float32) / np.float32(half))
    ang = np.arange(T, dtype=np.float32)[:, None] * inv[None, :]
    cos, sin, zero = np.cos(ang), np.sin(ang), np.zeros_like(ang)
    reps = LANES // HEAD_DIM
    return (jnp.asarray(np.tile(np.concatenate([cos, cos], axis=1), (1, reps))),
            jnp.asarray(np.tile(np.concatenate([-sin, zero], axis=1), (1, reps))),
            jnp.asarray(np.tile(np.concatenate([zero, sin], axis=1), (1, reps))))


def _rope(x, cos, sa, sb, sign):
    half = HEAD_DIM // 2
    return x * cos + sign * (pltpu.roll(x, LANES - half, 1) * sa + pltpu.roll(x, half, 1) * sb)


def _mix_in_fwd(h, g, w_in, tables, *, tm=512):
    T, D = h.shape

    def body(h_ref, g_ref, w_ref, cos_ref, sa_ref, sb_ref, u_ref, qkva_ref, qb_ref, kvb_ref, gates_ref):
        hf = h_ref[...]
        u = ((hf * _rstd(hf)) * g_ref[...]).astype(BF16)
        u_ref[...] = u
        qkva_ref[...] = _dot_nt(u, w_ref[0:C_QKVA, :]).astype(BF16)
        zr = _dot_nt(u, w_ref[O_QB:O_QB + C_ROPE, :])
        cos, sa, sb = cos_ref[...], sa_ref[...], sb_ref[...]
        for j in range(C_ROPE // LANES):
            rj = _rope(zr[:, j * LANES:(j + 1) * LANES], cos, sa, sb, 1.0).astype(BF16)
            if j < C_QB // LANES:
                qb_ref[:, j * LANES:(j + 1) * LANES] = rj
            else:
                kvb_ref[:, 0:C_KB] = rj
        kvb_ref[:, C_KB:2 * C_KB] = _dot_nt(u, w_ref[O_VB:O_VB + C_KB, :]).astype(BF16)
        gates_ref[...] = _dot_nt(u, w_ref[O_G:O_G + C_GATES, :])

    def tok(n):
        return pl.BlockSpec((tm, n), lambda i: (i, 0))

    return pl.pallas_call(
        body, name="mix_in_fwd", grid=(T // tm,),
        in_specs=[tok(D), pl.BlockSpec((1, D), lambda i: (0, 0)), pl.BlockSpec((D_IN, D), lambda i: (0, 0), pipeline_mode=pl.Buffered(1)),
                  tok(LANES), tok(LANES), tok(LANES)],
        out_specs=[tok(D), tok(C_QKVA), tok(C_QB), tok(2 * C_KB), tok(C_GATES)],
        out_shape=[jax.ShapeDtypeStruct((T, D), BF16), jax.ShapeDtypeStruct((T, C_QKVA), BF16),
                   jax.ShapeDtypeStruct((T, C_QB), BF16), jax.ShapeDtypeStruct((T, 2 * C_KB), BF16),
                   jax.ShapeDtypeStruct((T, C_GATES), F32)],
        compiler_params=_params(1),
    )(h, g, w_in, *tables)


def _mix_in_bwd(dqa, dka, dva, dqb, dkb, dvb, dgates, h, g, dres, w_in, tables, *, tm=512, rider=None):
    T, D = h.shape

    def body(dqa_ref, dka_ref, dva_ref, dqb_ref, dkb_ref, dvb_ref, dgt_ref, h_ref, g_ref, dres_ref, w_ref,
             cos_ref, sa_ref, sb_ref, dz_ref, dh_ref, dg_ref):
        @pl.when(pl.program_id(0) == 0)
        def _():
            dg_ref[...] = jnp.zeros_like(dg_ref)

        na = NA_HEADS * HEAD_DIM
        dz_ref[:, 0:na] = dqa_ref[...].astype(BF16)
        dz_ref[:, na:2 * na] = dka_ref[...].astype(BF16)
        dz_ref[:, 2 * na:3 * na] = dva_ref[...].astype(BF16)
        cos, sa, sb = cos_ref[...], sa_ref[...], sb_ref[...]
        for j in range(C_QB // LANES):
            dz_ref[:, O_QB + j * LANES:O_QB + (j + 1) * LANES] = _rope(
                dqb_ref[:, j * LANES:(j + 1) * LANES], cos, sa, sb, -1.0).astype(BF16)
        dz_ref[:, O_KB:O_KB + C_KB] = _rope(dkb_ref[...], cos, sa, sb, -1.0).astype(BF16)
        dz_ref[:, O_VB:O_VB + C_KB] = dvb_ref[...].astype(BF16)
        dz_ref[:, O_G:O_G + C_GATES] = dgt_ref[...].astype(BF16)
        du = _dot(dz_ref[...], w_ref[...])
        hf = h_ref[...]
        dx, dgr = _norm_bwd(du, hf, g_ref[...], _rstd(hf))
        dg_ref[...] += jnp.sum(dgr, axis=0, keepdims=True)
        dh_ref[...] = dres_ref[...] + dx

    def tok(n):
        return pl.BlockSpec((tm, n), lambda i: (i, 0))

    vec = pl.BlockSpec((1, D), lambda i: (0, 0))
    na = NA_HEADS * HEAD_DIM
    grid = (T // tm,)
    body, r_in, r_in_specs, r_out, r_out_specs, scratch = _ride(body, 14, 3, rider, grid, None)
    outs = pl.pallas_call(
        body, name="mix_in_bwd", grid=grid,
        in_specs=[tok(na), tok(na), tok(na), tok(C_QB), tok(C_KB), tok(C_KB), tok(C_GATES), tok(D), vec, tok(D),
                  pl.BlockSpec((D_IN, D), lambda i: (0, 0), pipeline_mode=pl.Buffered(1)), tok(LANES), tok(LANES), tok(LANES)]
        + r_in_specs,
        out_specs=[tok(D_IN), tok(D), vec] + r_out_specs,
        out_shape=[jax.ShapeDtypeStruct((T, D_IN), BF16), jax.ShapeDtypeStruct((T, D), F32),
                   jax.ShapeDtypeStruct((1, D), F32)] + r_out,
        scratch_shapes=scratch,
        compiler_params=_params(1),
    )(dqa, dka, dva, dqb, dkb, dvb, dgates, h, g, dres, w_in, *tables, *r_in)
    return (*outs[:3], list(outs[3:]))


def _na_bias_slabs(rpb):
    H = rpb.shape[0]
    ncell = GRID_W * GRID_W
    cell = np.arange(ncell)
    co = cell % GRID_W - cell // GRID_W + (NA_KW - 1)
    e_co = jnp.asarray((np.arange(LANES)[:, None] == co[None, :]).astype(np.float32))
    table = jnp.pad(rpb, ((0, 0), (0, 1), (0, LANES - rpb.shape[2]))).reshape(H * 2 * NA_KH, LANES)

    def body(t_ref, e_ref, o_ref):
        o_ref[...] = jnp.dot(t_ref[...], e_ref[...], preferred_element_type=F32, precision=lax.Precision.HIGHEST)

    toeplitz = pl.pallas_call(
        body, name="rpb_unfold", out_shape=jax.ShapeDtypeStruct((H * 2 * NA_KH, ncell), F32),
        compiler_params=_params(0),
    )(table, e_co).reshape(H, 2 * NA_KH, GRID_W, GRID_W)

    def assemble(tz_ref, o_ref):
        c = lax.broadcasted_iota(jnp.int32, (GRID_W, GRID_W), 0)
        k = lax.broadcasted_iota(jnp.int32, (GRID_W, GRID_W), 1)
        cs = jnp.clip(c - NA_KW // 2, 0, GRID_W - NA_KW)
        inwin = (k >= cs) & (k < cs + NA_KW)
        for ro0 in range(NA_KH):
            for hh in range(2):
                for i in range(NA_KH):
                    o_ref[0, ro0, hh * GRID_W:(hh + 1) * GRID_W, i * GRID_W:(i + 1) * GRID_W] = jnp.where(
                        inwin, tz_ref[hh, ro0 + i], NEG)

    return pl.pallas_call(
        assemble, name="na_bias_slabs", grid=(H // 2,),
        in_specs=[pl.BlockSpec((2, 2 * NA_KH, GRID_W, GRID_W), lambda p: (p, 0, 0, 0))],
        out_specs=pl.BlockSpec((1, NA_KH, 2 * GRID_W, NA_KH * GRID_W), lambda p: (p, 0, 0, 0)),
        out_shape=jax.ShapeDtypeStruct((H // 2, NA_KH, 2 * GRID_W, NA_KH * GRID_W), F32),
        compiler_params=_params(1),
    )(toeplitz)


def _half_masks(rows):
    lane = lax.broadcasted_iota(jnp.int32, (rows, LANES), 1)
    left = lane < HEAD_DIM
    return left, (left, jnp.logical_not(left))


def _stack_heads(x):
    left, halves = _half_masks(x.shape[0])
    xf = x.astype(F32)
    return jnp.concatenate([jnp.where(m, xf, 0.0).astype(BF16) for m in halves], axis=0)


def _unstack_heads(o):
    rows = o.shape[0] // 2
    left, _ = _half_masks(rows)
    return jnp.where(left, o[:rows], o[rows:])


def _na_row(j, t, rb, rows):
    r = j * rb + t
    rs = jnp.clip(r - NA_KH // 2, 0, rows - NA_KH)
    return pl.multiple_of(t * GRID_W, GRID_W), pl.multiple_of(rs * GRID_W, GRID_W), rs - r + (NA_KH - 1)


def _na_specs(T, rb):
    qrows = GRID_W * rb
    pairs = NA_HEADS // 2
    return ([pl.BlockSpec((qrows, LANES), lambda p, j: (j, p)),
             pl.BlockSpec((T, LANES), lambda p, j: (0, pairs + p)),
             pl.BlockSpec((T, LANES), lambda p, j: (0, 2 * pairs + p))],
            pl.BlockSpec((1, NA_KH, 2 * GRID_W, NA_KH * GRID_W), lambda p, j: (p, 0, 0, 0)))


def _softmax(s):
    p = jnp.exp(s - jnp.max(s, axis=-1, keepdims=True))
    return p / jnp.sum(p, axis=-1, keepdims=True)


def _na_fwd(qkva, bias, *, rb=32, group=32):
    T = qkva.shape[0]
    rows = T // GRID_W
    nkeys = NA_KH * GRID_W
    rb = min(rb, rows)
    group = min(group, rb)

    def body(q_ref, k_ref, v_ref, bias_ref, y_ref):
        j = pl.program_id(1)

        def rows_step(t, carry):
            at = [_na_row(j, t * group + u, rb, rows) for u in range(group)]
            s = [_dot_nt(_stack_heads(q_ref[pl.ds(q0, GRID_W), :]), k_ref[pl.ds(k0, nkeys), :]) for q0, k0, _ in at]
            p = [_softmax(su * QK_SCALE + bias_ref[0, ro0]) for su, (_, _, ro0) in zip(s, at)]
            o = [_dot(pu.astype(BF16), v_ref[pl.ds(k0, nkeys), :]) for pu, (_, k0, _) in zip(p, at)]
            for ou, (q0, _, _) in zip(o, at):
                y_ref[pl.ds(q0, GRID_W), :] = _unstack_heads(ou).astype(BF16)
            return carry

        lax.fori_loop(0, rb // group, rows_step, 0)

    qkv_specs, bias_spec = _na_specs(T, rb)
    return pl.pallas_call(
        body, name="na_fwd", grid=(NA_HEADS // 2, rows // rb),
        in_specs=qkv_specs + [bias_spec],
        out_specs=qkv_specs[0],
        out_shape=jax.ShapeDtypeStruct((T, NA_HEADS * HEAD_DIM), BF16),
        compiler_params=_params(2),
    )(qkva, qkva, qkva, bias)


def _na_bwd(qkva, dy, bias, *, rb=16, group=16, rider=None):
    T = qkva.shape[0]
    rows = T // GRID_W
    nkeys = NA_KH * GRID_W
    rb = min(rb, rows)
    group = min(group, rb)

    def body(q_ref, k_ref, v_ref, dy_ref, bias_ref, dq_ref, dk_ref, dv_ref, dbias_ref):
        j = pl.program_id(1)

        @pl.when(j == 0)
        def _():
            dk_ref[...] = jnp.zeros_like(dk_ref)
            dv_ref[...] = jnp.zeros_like(dv_ref)
            dbias_ref[...] = jnp.zeros_like(dbias_ref)

        def rows_step(t, carry):
            at = [_na_row(j, t * group + u, rb, rows) for u in range(group)]
            qs = [_stack_heads(q_ref[pl.ds(q0, GRID_W), :]) for q0, _, _ in at]
            dys = [_stack_heads(dy_ref[pl.ds(q0, GRID_W), :]) for q0, _, _ in at]
            s = [_dot_nt(qu, k_ref[pl.ds(k0, nkeys), :]) for qu, (_, k0, _) in zip(qs, at)]
            dp = [_dot_nt(du, v_ref[pl.ds(k0, nkeys), :]) for du, (_, k0, _) in zip(dys, at)]
            p = [_softmax(su * QK_SCALE + bias_ref[0, ro0]) for su, (_, _, ro0) in zip(s, at)]
            ds = [pu * (du - jnp.sum(pu * du, axis=-1, keepdims=True)) for pu, du in zip(p, dp)]
            for u, (q0, k0, ro0) in enumerate(at):
                dbias_ref[0, ro0] += ds[u]
                dsb = ds[u].astype(BF16)
                dq_ref[pl.ds(q0, GRID_W), :] = (_unstack_heads(_dot(dsb, k_ref[pl.ds(k0, nkeys), :])) * QK_SCALE).astype(BF16)
                dk_ref[pl.ds(k0, nkeys), :] += _dot_tn(dsb, qs[u]) * QK_SCALE
                dv_ref[pl.ds(k0, nkeys), :] += _dot_tn(p[u].astype(BF16), dys[u])
            return carry

        lax.fori_loop(0, rb // group, rows_step, 0)

    qkv_specs, bias_spec = _na_specs(T, rb)
    width = NA_HEADS * HEAD_DIM
    kv_out = pl.BlockSpec((T, LANES), lambda p, j: (0, p))
    grid = (NA_HEADS // 2, rows // rb)
    body, r_in, r_in_specs, r_out, r_out_specs, scratch = _ride(body, 5, 4, rider, grid, None)
    outs = pl.pallas_call(
        body, name="na_bwd", grid=grid,
        in_specs=qkv_specs + [qkv_specs[0], bias_spec] + r_in_specs,
        out_specs=[qkv_specs[0], kv_out, kv_out, bias_spec] + r_out_specs,
        out_shape=[jax.ShapeDtypeStruct((T, width), BF16), jax.ShapeDtypeStruct((T, width), F32),
                   jax.ShapeDtypeStruct((T, width), F32), jax.ShapeDtypeStruct(bias.shape, F32)] + r_out,
        scratch_shapes=scratch,
        compiler_params=_params(2),
    )(qkva, qkva, qkva, dy, bias, *r_in)
    return (*outs[:4], list(outs[4:]))


def _rpb_fold(dslab):
    pairs = dslab.shape[0]
    H = 2 * pairs
    ncell = GRID_W * GRID_W

    def disassemble(d_ref, tz_ref):
        tz_ref[...] = jnp.zeros_like(tz_ref)
        for ro0 in range(NA_KH):
            for hh in range(2):
                for i in range(NA_KH):
                    tz_ref[hh, ro0 + i] += d_ref[0, ro0, hh * GRID_W:(hh + 1) * GRID_W, i * GRID_W:(i + 1) * GRID_W]

    dtoeplitz = pl.pallas_call(
        disassemble, name="rpb_fold_tiles", grid=(pairs,),
        in_specs=[pl.BlockSpec((1, NA_KH, 2 * GRID_W, NA_KH * GRID_W), lambda p: (p, 0, 0, 0))],
        out_specs=pl.BlockSpec((2, 2 * NA_KH, GRID_W, GRID_W), lambda p: (p, 0, 0, 0)),
        out_shape=jax.ShapeDtypeStruct((H, 2 * NA_KH, GRID_W, GRID_W), F32),
        compiler_params=_params(1),
    )(dslab).reshape(H * 2 * NA_KH, ncell)
    cell = np.arange(ncell)
    co = cell % GRID_W - cell // GRID_W + (NA_KW - 1)
    e_co = jnp.asarray((co[:, None] == np.arange(LANES)[None, :]).astype(np.float32))

    def diagonals(x_ref, e_ref, o_ref):
        o_ref[...] = jnp.dot(x_ref[...], e_ref[...], preferred_element_type=F32, precision=lax.Precision.HIGHEST)

    return pl.pallas_call(
        diagonals, name="rpb_fold", out_shape=jax.ShapeDtypeStruct((H * 2 * NA_KH, LANES), F32),
        compiler_params=_params(0),
    )(dtoeplitz, e_co).reshape(H, 2 * NA_KH, LANES)


SWA_KEYS = 3 * WIN


def _swa_block(j, t, qbn, T):
    blk = j * qbn + t
    start = jnp.clip((blk - 1) * WIN, 0, T - SWA_KEYS)
    row = lax.broadcasted_iota(jnp.int32, (2 * WIN, SWA_KEYS), 0)
    qpos = blk * WIN + jnp.where(row < WIN, row, row - WIN)
    kpos = start + lax.broadcasted_iota(jnp.int32, (2 * WIN, SWA_KEYS), 1)
    return pl.multiple_of(t * WIN, WIN), pl.multiple_of(start, WIN), jnp.abs(qpos - kpos) <= WIN


def _swa_sinks(sink_ref, p):
    row = lax.broadcasted_iota(jnp.int32, (2 * WIN, 1), 0)
    return jnp.where(row < WIN, sink_ref[p], sink_ref[p + NB_HEADS // 2])


def _swa_probs(s, mask, sink):
    s = jnp.where(mask, s * QK_SCALE, NEG)
    m = jnp.maximum(jnp.max(s, axis=-1, keepdims=True), sink)
    e = jnp.exp(s - m)
    esink = jnp.exp(sink - m)
    den = jnp.sum(e, axis=-1, keepdims=True) + esink
    return e / den, esink / den


def _swa_specs(T, qbn):
    return [pl.BlockSpec(memory_space=pltpu.SMEM),
            pl.BlockSpec((WIN * qbn, LANES), lambda p, j: (j, p)),
            pl.BlockSpec((T, LANES), lambda p, j: (0, 0)),
            pl.BlockSpec((T, LANES), lambda p, j: (0, 1))]


def _swa_fwd(qb, kvb, sink, *, qbn=32, group=32):
    T = qb.shape[0]
    pairs = NB_HEADS // 2
    qbn = min(qbn, T // WIN)
    group = min(group, qbn)

    def body(sink_ref, q_ref, k_ref, v_ref, y_ref):
        p, j = pl.program_id(0), pl.program_id(1)
        sinks = _swa_sinks(sink_ref, p)

        def blocks_step(t, carry):
            at = [_swa_block(j, t * group + u, qbn, T) for u in range(group)]
            s = [_dot_nt(_stack_heads(q_ref[pl.ds(q0, WIN), :]), k_ref[pl.ds(k0, SWA_KEYS), :]) for q0, k0, _ in at]
            pr = [_swa_probs(su, mask, sinks)[0] for su, (_, _, mask) in zip(s, at)]
            o = [_dot(pu.astype(BF16), v_ref[pl.ds(k0, SWA_KEYS), :]) for pu, (_, k0, _) in zip(pr, at)]
            for ou, (q0, _, _) in zip(o, at):
                y_ref[pl.ds(q0, WIN), :] = _unstack_heads(ou).astype(BF16)
            return carry

        lax.fori_loop(0, qbn // group, blocks_step, 0)

    specs = _swa_specs(T, qbn)
    return pl.pallas_call(
        body, name="swa_fwd", grid=(pairs, T // (WIN * qbn)),
        in_specs=specs, out_specs=specs[1],
        out_shape=jax.ShapeDtypeStruct((T, NB_HEADS * HEAD_DIM), BF16),
        compiler_params=_params(2),
    )(sink, qb, kvb, kvb)


def _swa_bwd(qb, kvb, dy, sink, *, qbn=16, group=16, rider=None):
    T = qb.shape[0]
    pairs = NB_HEADS // 2
    qbn = min(qbn, T // WIN)
    group = min(group, qbn)

    def body(sink_ref, q_ref, k_ref, v_ref, dy_ref, dq_ref, dk_ref, dv_ref, dsink_ref):
        p, j = pl.program_id(0), pl.program_id(1)
        sinks = _swa_sinks(sink_ref, p)

        @pl.when((p == 0) & (j == 0))
        def _():
            dk_ref[...] = jnp.zeros_like(dk_ref)
            dv_ref[...] = jnp.zeros_like(dv_ref)

        @pl.when(j == 0)
        def _():
            dsink_ref[...] = jnp.zeros_like(dsink_ref)

        def blocks_step(t, carry):
            at = [_swa_block(j, t * group + u, qbn, T) for u in range(group)]
            qs = [_stack_heads(q_ref[pl.ds(q0, WIN), :]) for q0, _, _ in at]
            dys = [_stack_heads(dy_ref[pl.ds(q0, WIN), :]) for q0, _, _ in at]
            s = [_dot_nt(qu, k_ref[pl.ds(k0, SWA_KEYS), :]) for qu, (_, k0, _) in zip(qs, at)]
            dp = [_dot_nt(du, v_ref[pl.ds(k0, SWA_KEYS), :]) for du, (_, k0, _) in zip(dys, at)]
            probs = [_swa_probs(su, mask, sinks) for su, (_, _, mask) in zip(s, at)]
            for u, (q0, k0, _) in enumerate(at):
                pr, psink = probs[u]
                delta = jnp.sum(pr * dp[u], axis=-1, keepdims=True)
                dsb = (pr * (dp[u] - delta)).astype(BF16)
                dsk = psink * delta
                for hh in range(2):
                    dsink_ref[0, hh:hh + 1, :] += jnp.broadcast_to(-jnp.sum(dsk[hh * WIN:(hh + 1) * WIN]), (1, LANES))
                dq_ref[pl.ds(q0, WIN), :] = _unstack_heads(_dot(dsb, k_ref[pl.ds(k0, SWA_KEYS), :])) * QK_SCALE
                dk_ref[pl.ds(k0, SWA_KEYS), :] += _dot_tn(dsb, qs[u]) * QK_SCALE
                dv_ref[pl.ds(k0, SWA_KEYS), :] += _dot_tn(pr.astype(BF16), dys[u])
            return carry

        lax.fori_loop(0, qbn // group, blocks_step, 0)

    specs = _swa_specs(T, qbn)
    kv_out = pl.BlockSpec((T, LANES), lambda p, j: (0, 0))
    grid = (pairs, T // (WIN * qbn))
    body, r_in, r_in_specs, r_out, r_out_specs, scratch = _ride(body, 5, 4, rider, grid, None)
    outs = pl.pallas_call(
        body, name="swa_bwd", grid=grid,
        in_specs=specs + [specs[1]] + r_in_specs,
        out_specs=[specs[1], kv_out, kv_out, pl.BlockSpec((1, 8, LANES), lambda p, j: (p, 0, 0))] + r_out_specs,
        out_shape=[jax.ShapeDtypeStruct((T, NB_HEADS * HEAD_DIM), F32), jax.ShapeDtypeStruct((T, LANES), F32),
                   jax.ShapeDtypeStruct((T, LANES), F32), jax.ShapeDtypeStruct((pairs, 8, LANES), F32)] + r_out,
        scratch_shapes=scratch,
        compiler_params=_params(2),
    )(sink, qb, kvb, kvb, dy, *r_in)
    return (*outs[:4], list(outs[4:]))


def _merge_fwd(ya, yb, gates, wa, wb, wout, h, *, tm=512):
    T, D = h.shape
    W = ya.shape[1]

    def body(ya_ref, yb_ref, gt_ref, wa_ref, wb_ref, wo_ref, h_ref, h2_ref, mg_ref):
        pa = _dot(ya_ref[...], wa_ref[...])
        pb = _dot(yb_ref[...], wb_ref[...])
        mg = (jax.nn.sigmoid(gt_ref[:, 0:D]) * pa + jax.nn.sigmoid(gt_ref[:, D:2 * D]) * pb).astype(BF16)
        mg_ref[...] = mg
        h2_ref[...] = h_ref[...] + _dot(mg, wo_ref[...])

    def tok(n):
        return pl.BlockSpec((tm, n), lambda i: (i, 0))

    def full(r, c):
        return pl.BlockSpec((r, c), lambda i: (0, 0))

    return pl.pallas_call(
        body, name="merge_fwd", grid=(T // tm,),
        in_specs=[tok(W), tok(W), tok(2 * D), full(W, D), full(W, D), full(D, D), tok(D)],
        out_specs=[tok(D), tok(D)],
        out_shape=[jax.ShapeDtypeStruct((T, D), F32), jax.ShapeDtypeStruct((T, D), BF16)],
        compiler_params=_params(1),
    )(ya, yb, gates, wa, wb, wout, h)


def _merge_bwd(dh, ya, yb, gates, wa, wb, wout, *, tm=512, rider=None):
    T, D = dh.shape
    W = ya.shape[1]

    def body(dh_ref, ya_ref, yb_ref, gt_ref, wa_ref, wb_ref, wo_ref, dya_ref, dyb_ref, dpa_ref, dpb_ref, dgt_ref):
        dmg = _dot_nt(dh_ref[...].astype(BF16), wo_ref[...])
        for y_ref, w_ref, dy_ref, dp_ref, lo in ((ya_ref, wa_ref, dya_ref, dpa_ref, 0), (yb_ref, wb_ref, dyb_ref, dpb_ref, D)):
            sg = jax.nn.sigmoid(gt_ref[:, lo:lo + D])
            dp = (dmg * sg).astype(BF16)
            dp_ref[...] = dp
            dgt_ref[:, lo:lo + D] = (dmg * _dot(y_ref[...], w_ref[...]) * (sg * (1.0 - sg))).astype(BF16)
            dy_ref[...] = _dot_nt(dp, w_ref[...]).astype(BF16)

    def tok(n):
        return pl.BlockSpec((tm, n), lambda i: (i, 0))

    def full(r, c):
        return pl.BlockSpec((r, c), lambda i: (0, 0))

    grid = (T // tm,)
    body, r_in, r_in_specs, r_out, r_out_specs, scratch = _ride(body, 7, 5, rider, grid, None)
    outs = pl.pallas_call(
        body, name="merge_bwd", grid=grid,
        in_specs=[tok(D), tok(W), tok(W), tok(2 * D), full(W, D), full(W, D), full(D, D)] + r_in_specs,
        out_specs=[tok(W), tok(W), tok(D), tok(D), tok(2 * D)] + r_out_specs,
        out_shape=[jax.ShapeDtypeStruct((T, W), BF16), jax.ShapeDtypeStruct((T, W), BF16),
                   jax.ShapeDtypeStruct((T, D), BF16), jax.ShapeDtypeStruct((T, D), BF16),
                   jax.ShapeDtypeStruct((T, 2 * D), BF16)] + r_out,
        scratch_shapes=scratch,
        compiler_params=_params(1),
    )(dh, ya, yb, gates, wa, wb, wout, *r_in)
    return (*outs[:5], list(outs[5:]))


def _pair_heads(a, axis):
    shp = a.shape
    a = a.reshape(shp[:axis] + (2, NB_HEADS // 2, HEAD_DIM) + shp[axis + 1:])
    return jnp.swapaxes(a, axis, axis + 1).reshape(shp)


def _unpair_heads(a, axis):
    shp = a.shape
    a = a.reshape(shp[:axis] + (NB_HEADS // 2, 2, HEAD_DIM) + shp[axis + 1:])
    return jnp.swapaxes(a, axis, axis + 1).reshape(shp)


FFN1 = ("ffn1_w_gate", "ffn1_w_up", "ffn1_w_down")
FFN2 = ("ffn2_w_gate", "ffn2_w_up", "ffn2_w_down")
MIXER = ("w_in", "w_branch_a", "w_branch_b", "w_out")
BRANCH = MIXER[1:]


def _layer_grads(x, target, g1, f1, gmix, late, rpb, sink, g2, gfin, comm=None):
    T = x.shape[0]
    tables = _rope_tables(T)
    bias = _na_bias_slabs(rpb)

    comm = comm or _Local(late)
    h1, n1, hdn1, p1, q1, gathered = _ffn_fwd(x, g1, *f1, name="ffn1_fwd", rider=comm.late_rider)
    w_in_t, wa, wb, wout, f2 = comm.late(gathered)
    w_in_p = jnp.concatenate([w_in_t[:O_QB], _pair_heads(w_in_t[O_QB:O_KB], 0), w_in_t[O_KB:]], axis=0)
    wb_p = _pair_heads(wb, 0)
    u, qkva, qb, kvb, gates = _mix_in_fwd(h1, gmix, w_in_p, tables)
    ya = _na_fwd(qkva, bias)
    yb = _swa_fwd(qb, kvb, sink)
    h2, merged = _merge_fwd(ya, yb, gates, wa, wb_p, wout, h1)
    dh3, n2, hdn2, p2, q2, loss, dgfin, _ = _ffn_fwd(h2, g2, *f2, name="ffn2_fwd", head=(gfin, target))

    dh2, da2, db2, dg2, _ = _ffn_bwd(dh3, h2, g2, p2, q2, *f2, name="ffn2_bwd")
    df2 = [_wgrad_shard_a(da2, n2, name="ffn2_dwg")[0], _wgrad_shard_a(db2, n2, name="ffn2_dwu")[0],
           _wgrad_shard_a(hdn2, dh3, scale=0.5, name="ffn2_dwd")[0]]
    red2 = comm.reduce(FFN2, df2, tag="ffn2")
    dya, dyb, dpa, dpb, dgates, got = _merge_bwd(dh2, ya, yb, gates, wa, wb_p, wout, rider=red2.sibling)
    red2.partial(got)
    dwout = _wgrad_cols(merged, dh2, 1, name="dwout").reshape(N_CHIPS, D_MODEL // N_CHIPS, D_MODEL)
    dwa = _wgrad_cols(ya, dpa, N_CHIPS, name="dwa")
    dwb = _unpair_heads(_wgrad_cols(yb, dpb, N_CHIPS, name="dwb"), 1)
    redb = comm.reduce(BRANCH, [dwa, dwb, dwout], tag="branch")
    dqa, dka, dva, dbias, got = _na_bwd(qkva, dya, bias, rider=_two_riders(red2.chips, redb.sibling))
    red2.halves(got[:len(FFN2)])
    redb.partial(got[len(FFN2):])
    drpb = _rpb_fold(dbias)
    dqb, dkb, dvb, dsink, got = _swa_bwd(qb, kvb, dyb, sink, rider=_two_riders(red2.share, redb.chips))
    out = red2.result(got[:len(FFN2)])
    redb.halves(got[len(FFN2):])
    dz, dh1, dgmix, got = _mix_in_bwd(dqa, dka, dva, dqb, dkb, dvb, dgates, h1, gmix, dh2, w_in_p, tables, rider=redb.share)
    out.update(redb.result(got))
    dwin_p = _wgrad_rows(dz, u, 2, name="dwin")[0].reshape(D_IN, D_MODEL)
    dwin = jnp.concatenate([dwin_p[:O_QB], _unpair_heads(dwin_p[O_QB:O_KB], 0), dwin_p[O_KB:]], axis=0)
    dx, da1, db1, dg1, _ = _ffn_bwd(dh1, x, g1, p1, q1, *f1, name="ffn1_bwd")
    redw = comm.reduce(("w_in",), [dwin.reshape(N_CHIPS, D_IN // N_CHIPS, D_MODEL)], tag="w_in").partial_now()
    dwg1, got = _wgrad_shard_a(da1, n1, name="ffn1_dwg", rider=redw.chips)
    dwu1, got = _wgrad_shard_a(db1, n1, name="ffn1_dwu", rider=redw.halves(got).share)
    out.update(redw.result(got))
    red1 = comm.reduce(FFN1[:2], [dwg1, dwu1], tag="ffn1_gate_up").partial_now()
    dwd1, got = _wgrad_shard_a(hdn1, dh1, scale=0.5, name="ffn1_dwd", rider=red1.chips)
    out.update(red1.halves(got).result_now())
    out.update(comm.reduce(FFN1[2:], [dwd1], tag="ffn1_down").partial_now().halves_now().result_now())
    for names in (FFN2, ("w_in",), BRANCH[:2], BRANCH[2:], FFN1[:2], FFN1[2:]):
        comm.update(names, out)
    out.update(loss=loss, dx=dx, ffn1_norm=dg1, mix_norm=dgmix, ffn2_norm=dg2, final_norm=dgfin, na_rpb=drpb,
               sink_logit=dsink[:, 0:2, 0].T.reshape(NB_HEADS))
    return out


class _Local:
    late_rider = None

    def __init__(self, late):
        self._late = late

    def late(self, gathered):
        return self._late

    def reduce(self, names, grads, *, tag):
        return _LocalReduce(names, grads)

    def update(self, names, reduced):
        pass


class _LocalReduce:
    sibling = chips = share = None

    def __init__(self, names, grads):
        self._result = dict(zip(names, grads))

    def partial(self, got=None):
        return self

    halves = partial_now = halves_now = partial

    def result(self, got=None):
        return self._result

    result_now = result


ANY = pl.BlockSpec(memory_space=pl.ANY)


def _place():
    x, y, c = lax.axis_index("x"), lax.axis_index("y"), lax.axis_index("c")
    chips = [(1 - x, y), (x, 1 - y), (1 - x, 1 - y)]
    return x, y, c, 2 * x + y, chips


def _remote(src, dst, send_sems, recv_sems, k, device):
    return pltpu.make_async_remote_copy(src_ref=src, dst_ref=dst, send_sem=send_sems.at[k], recv_sem=recv_sems.at[k],
                                        device_id=device, device_id_type=MESH)


class _Rider:
    def __init__(self, inputs, out_shape, scratch, start, middle, finish):
        self.inputs, self.out_shape, self.scratch = list(inputs), list(out_shape), list(scratch)
        self.start, self.middle, self.finish = start, middle, finish


def _two_riders(first, second):
    if first is None and second is None:
        return None
    assert first.middle is None and second.middle is None
    n_in, n_out, n_sem = len(first.inputs), len(first.out_shape), len(first.scratch)

    def phase(name):
        def run(ins, outs, sems):
            getattr(first, name)(ins[:n_in], outs[:n_out], sems[:n_sem])
            getattr(second, name)(ins[n_in:], outs[n_out:], sems[n_sem:])
        return run

    return _Rider(first.inputs + second.inputs, first.out_shape + second.out_shape, first.scratch + second.scratch,
                  phase("start"), None, phase("finish"))


def _run_rider(rider, *, name):
    n_in, n_out = len(rider.inputs), len(rider.out_shape)

    def body(*refs):
        ins, outs, sems = refs[:n_in], refs[n_in:n_in + n_out], refs[n_in + n_out:]
        rider.start(ins, outs, sems)
        if rider.middle is not None:
            rider.middle(ins, outs, sems)
        rider.finish(ins, outs, sems)

    return pl.pallas_call(body, name=name, in_specs=[ANY] * n_in, out_specs=[ANY] * n_out, out_shape=rider.out_shape,
                          scratch_shapes=rider.scratch)(*rider.inputs)


def _ride(body, n_in, n_out, rider, grid, middle_step):
    if rider is None:
        return body, [], [], [], [], []
    r_in, r_out = len(rider.inputs), len(rider.out_shape)
    steps = math.prod(grid)

    def riding(*refs):
        ins, r_ins = refs[:n_in], refs[n_in:n_in + r_in]
        outs = refs[n_in + r_in:n_in + r_in + n_out]
        r_outs = refs[n_in + r_in + n_out:n_in + r_in + n_out + r_out]
        sems = refs[n_in + r_in + n_out + r_out:]
        step = pl.program_id(0)
        for axis in range(1, len(grid)):
            step = step * grid[axis] + pl.program_id(axis)

        @pl.when(step == 0)
        def _():
            rider.start(r_ins, r_outs, sems)

        body(*ins, *outs)

        if rider.middle is not None:
            @pl.when(step == middle_step)
            def _():
                rider.middle(r_ins, r_outs, sems)

        @pl.when(step == steps - 1)
        def _():
            rider.finish(r_ins, r_outs, sems)

    return riding, rider.inputs, [ANY] * r_in, rider.out_shape, [ANY] * r_out, rider.scratch


def _gather_rider(shards):
    n = len(shards)

    def plan(ins, outs, sems, kinds):
        send_sems, recv_sems, own_send_sems, own_recv_sems = sems
        x, y, c, mine, chips = _place()
        sibling = (x, y, 1 - c)
        made = {k: [] for k in kinds}
        for i in range(n):
            hr = shards[i].shape[0] // 2
            if "own" in made:
                made["own"].append(_remote(ins[i], outs[i].at[mine], own_send_sems, own_recv_sems, i, sibling))
            for j, (cx, cy) in enumerate(chips):
                here = outs[i].at[2 * cx + cy, pl.ds(c * hr, hr)]
                there = outs[i].at[2 * cx + cy, pl.ds((1 - c) * hr, hr)]
                if "sends" in made:
                    made["sends"].append(_remote(ins[i].at[pl.ds(c * hr, hr)], outs[i].at[mine, pl.ds(c * hr, hr)],
                                                 send_sems, recv_sems, 6 * i + j, (cx, cy, c)))
                if "landed" in made:
                    made["landed"].append(_remote(here, here, send_sems, recv_sems, 6 * i + j, (cx, cy, c)))
                if "passes" in made:
                    made["passes"].append(_remote(here, here, send_sems, recv_sems, 6 * i + 3 + j, sibling))
                if "others" in made:
                    made["others"].append(_remote(there, there, send_sems, recv_sems, 6 * i + 3 + j, sibling))
        return [made[k] for k in kinds]

    def start(ins, outs, sems):
        own, sends = plan(ins, outs, sems, ("own", "sends"))
        for cp in own + sends:
            cp.start()

    def middle(ins, outs, sems):
        landed, passes = plan(ins, outs, sems, ("landed", "passes"))
        for arrived, cp in zip(landed, passes):
            arrived.wait_recv()
            cp.start()

    def finish(ins, outs, sems):
        own, sends, passes, others = plan(ins, outs, sems, ("own", "sends", "passes", "others"))
        for arrived in others:
            arrived.wait_recv()
        for cp in sends + passes:
            cp.wait_send()
        for cp in own:
            cp.wait()

    return _Rider(shards, [jax.ShapeDtypeStruct((N_CHIPS,) + s.shape, s.dtype) for s in shards],
                  [pltpu.SemaphoreType.DMA((6 * n,)), pltpu.SemaphoreType.DMA((6 * n,)),
                   pltpu.SemaphoreType.DMA((n,)), pltpu.SemaphoreType.DMA((n,))], start, middle, finish)


def _swap_rider(arrays, out_shape, source):
    n = len(arrays)

    def plan(ins, outs, sems):
        send_sems, recv_sems = sems
        x, y, c, _, _ = _place()
        return [_remote(source(ins[i], c, i), outs[i], send_sems, recv_sems, i, (x, y, 1 - c)) for i in range(n)]

    def start(ins, outs, sems):
        for cp in plan(ins, outs, sems):
            cp.start()

    def finish(ins, outs, sems):
        for cp in plan(ins, outs, sems):
            cp.wait()

    return _Rider(arrays, out_shape, [pltpu.SemaphoreType.DMA((n,)), pltpu.SemaphoreType.DMA((n,))], start, None, finish)


def _sibling_rider(grads):
    half = [g.shape[1] // 2 for g in grads]
    return _swap_rider(grads, [jax.ShapeDtypeStruct((g.shape[0], hr, g.shape[2]), g.dtype) for g, hr in zip(grads, half)],
                       lambda ref, c, i: ref.at[:, pl.ds((1 - c) * half[i], half[i])])


def _share_rider(halves):
    return _swap_rider(halves, [jax.ShapeDtypeStruct(h.shape, h.dtype) for h in halves], lambda ref, c, i: ref)


def _chips_rider(parts):
    n = len(parts)

    def plan(ins, outs, sems):
        send_sems, recv_sems = sems
        _, _, c, _, chips = _place()
        return [_remote(ins[i].at[2 * cx + cy], outs[i].at[j], send_sems, recv_sems, 3 * i + j, (cx, cy, c))
                for i in range(n) for j, (cx, cy) in enumerate(chips)]

    def start(ins, outs, sems):
        for cp in plan(ins, outs, sems):
            cp.start()

    def finish(ins, outs, sems):
        for cp in plan(ins, outs, sems):
            cp.wait()

    return _Rider(parts, [jax.ShapeDtypeStruct((N_CHIPS - 1,) + p.shape[1:], p.dtype) for p in parts],
                  [pltpu.SemaphoreType.DMA((3 * n,)), pltpu.SemaphoreType.DMA((3 * n,))], start, None, finish)


class _Reduce:
    def __init__(self, names, grads, cidx, chip, *, tag):
        self.names, self.grads, self.cidx, self.chip, self.tag = names, grads, cidx, chip, tag
        self.sibling = _sibling_rider(grads)

    def _by_shape(self, fn, *lists):
        done, i = [], 0
        while i < len(self.names):
            j = i + 1
            while j < len(self.names) and self.grads[j].shape == self.grads[i].shape:
                j += 1
            done += fn(*[lst[i:j] for lst in lists], self.names[i])
            i = j
        return done

    def partial(self, from_sibling):
        self.from_sibling = from_sibling
        self.chips = _chips_rider(self._by_shape(
            lambda g, r, k: _add_sibling(g, r, self.cidx, name="add_sibling_" + k), self.grads, from_sibling))
        return self

    def halves(self, from_chips):
        self.mine = self._by_shape(
            lambda g, r1, r2, k: _add_chips(g, r1, r2, self.cidx, self.chip, name="add_chips_" + k),
            self.grads, self.from_sibling, from_chips)
        self.share = _share_rider(self.mine)
        return self

    def result(self, others):
        return dict(zip(self.names, zip(self.mine, others)))

    def partial_now(self):
        return self.partial(_run_rider(self.sibling, name="rs_sibling_" + self.tag))

    def halves_now(self):
        return self.halves(_run_rider(self.chips, name="rs_chips_" + self.tag))

    def result_now(self):
        return self.result(_run_rider(self.share, name="rs_share_" + self.tag))


N_DEV = 8


def _small_allreduce(vec):
    R = vec.shape[0]

    def body(v_ref, o_ref, buf, send_sems, recv_sems):
        x, y, c, _, _ = _place()
        me = 4 * x + 2 * y + c
        buf[me] = v_ref[...]
        copies = []
        for k in range(1, N_DEV):
            peer = (x ^ (k >> 2), y ^ ((k >> 1) & 1), c ^ (k & 1))
            cp = _remote(v_ref, buf.at[me], send_sems, recv_sems, k - 1, peer)
            cp.start()
            copies.append(cp)
        for k, cp in enumerate(copies, start=1):
            cp.wait_send()
            landed = buf.at[me ^ k]
            _remote(landed, landed, send_sems, recv_sems, k - 1, (x, y, c)).wait_recv()
        acc = buf[0]
        for d in range(1, N_DEV):
            acc = acc + buf[d]
        o_ref[...] = acc

    return pl.pallas_call(
        body, name="small_allreduce",
        in_specs=[pl.BlockSpec(memory_space=pltpu.VMEM)], out_specs=pl.BlockSpec(memory_space=pltpu.VMEM),
        out_shape=jax.ShapeDtypeStruct(vec.shape, vec.dtype),
        scratch_shapes=[pltpu.VMEM((N_DEV, R, LANES), F32), pltpu.SemaphoreType.DMA((N_DEV - 1,)),
                        pltpu.SemaphoreType.DMA((N_DEV - 1,))],
    )(vec)


ELEMWISE_BLOCK = 512 * 1024


def _row_tile(rows, cols):
    best = None
    for t in range(16, rows + 1, 16):
        if rows % t == 0 and t * cols <= ELEMWISE_BLOCK:
            best = t
    return best if best is not None else rows


def _add_sibling(gs, r1s, cidx, *, name):
    n = len(gs)
    S, R, C = gs[0].shape
    hr = R // 2
    tr = _row_tile(hr, C)
    nt = hr // tr

    def body(c_ref, *refs):
        for g_ref, r_ref, o_ref in zip(refs[:n], refs[n:2 * n], refs[2 * n:]):
            o_ref[...] = (g_ref[...] + r_ref[...]).astype(BF16)

    blk = pl.BlockSpec((1, tr, C), lambda s, t, c: (s, t, 0))
    mine = pl.BlockSpec((1, tr, C), lambda s, t, c: (s, c[0] * nt + t, 0))
    return list(pl.pallas_call(
        body, name=name,
        grid_spec=pltpu.PrefetchScalarGridSpec(
            num_scalar_prefetch=1, grid=(S, nt), in_specs=[mine] * n + [blk] * n, out_specs=[blk] * n),
        out_shape=[jax.ShapeDtypeStruct((S, hr, C), BF16)] * n,
        compiler_params=_params(2),
    )(cidx, *gs, *r1s))


def _add_chips(gs, r1s, r2s, cidx, chip, *, name):
    n = len(gs)
    _, R, C = gs[0].shape
    hr = R // 2
    tr = _row_tile(hr, C)
    nt = hr // tr

    def body(pos_ref, *refs):
        for g_ref, r1_ref, r2_ref, o_ref in zip(refs[:n], refs[n:2 * n], refs[2 * n:3 * n], refs[3 * n:]):
            own = g_ref[0] + r1_ref[0]
            o_ref[...] = ((own + r2_ref[0].astype(F32)) + r2_ref[1].astype(F32)) + r2_ref[2].astype(F32)

    pos = jnp.concatenate([cidx, chip])
    return list(pl.pallas_call(
        body, name=name,
        grid_spec=pltpu.PrefetchScalarGridSpec(
            num_scalar_prefetch=1, grid=(nt,),
            in_specs=[pl.BlockSpec((1, tr, C), lambda t, pos: (pos[1], pos[0] * nt + t, 0))] * n
            + [pl.BlockSpec((1, tr, C), lambda t, pos: (pos[1], t, 0))] * n
            + [pl.BlockSpec((N_CHIPS - 1, tr, C), lambda t, pos: (0, t, 0))] * n,
            out_specs=[pl.BlockSpec((tr, C), lambda t, pos: (t, 0))] * n),
        out_shape=[jax.ShapeDtypeStruct((hr, C), F32)] * n,
        compiler_params=_params(1),
    )(pos, *gs, *r1s, *r2s))


def _adamw_math(w, g, m, v):
    mn = ADAM_B1 * m + (1.0 - ADAM_B1) * g
    vn = ADAM_B2 * v + (1.0 - ADAM_B2) * (g * g)
    m_hat = mn / (1.0 - ADAM_B1 ** ADAM_STEP)
    v_hat = vn / (1.0 - ADAM_B2 ** ADAM_STEP)
    return -ADAM_LR * (m_hat / (jnp.sqrt(v_hat) + ADAM_EPS) + ADAM_WD * w), mn, vn


def _adamw_halves(ws, mines, others, ms, vs, cidx, *, name):
    n = len(ws)
    R, C = ws[0].shape
    hr = R // 2
    tr = _row_tile(hr, C * min(n, 2))
    nt = hr // tr

    def body(c_ref, *refs):
        for i in range(n):
            w_ref, a_ref, b_ref, m_ref, v_ref = (refs[j * n + i] for j in range(5))
            g_ref, d_ref, mo_ref, vo_ref = (refs[(5 + j) * n + i] for j in range(4))
            gv = jnp.where(pl.program_id(0) == c_ref[0], a_ref[...], b_ref[...])
            g_ref[...] = gv
            d_ref[...], mo_ref[...], vo_ref[...] = _adamw_math(w_ref[...], gv, m_ref[...], v_ref[...])

    full = pl.BlockSpec((tr, C), lambda h, t, c: (h * nt + t, 0))
    own = pl.BlockSpec((tr, C), lambda h, t, c: (jnp.where(h == c[0], t, 0), 0))
    sib = pl.BlockSpec((tr, C), lambda h, t, c: (jnp.where(h == c[0], 0, t), 0))
    shape = jax.ShapeDtypeStruct((R, C), F32)
    outs = pl.pallas_call(
        body, name=name,
        grid_spec=pltpu.PrefetchScalarGridSpec(
            num_scalar_prefetch=1, grid=(2, nt),
            in_specs=[full] * n + [own] * n + [sib] * n + [full] * (2 * n), out_specs=[full] * (4 * n)),
        out_shape=[shape] * (4 * n),
        compiler_params=_params(2),
    )(cidx, *ws, *mines, *others, *ms, *vs)
    return [list(outs[j * n:(j + 1) * n]) for j in range(4)]


def _adamw_small(ws, gs, ms, vs):
    n = len(ws)

    def body(*refs):
        for i in range(n):
            w_ref, g_ref, m_ref, v_ref = (refs[j * n + i] for j in range(4))
            d_ref, mo_ref, vo_ref = (refs[(4 + j) * n + i] for j in range(3))
            d_ref[...], mo_ref[...], vo_ref[...] = _adamw_math(w_ref[...], g_ref[...], m_ref[...], v_ref[...])

    shapes = [jax.ShapeDtypeStruct(a.shape, F32) for a in ws]
    outs = pl.pallas_call(body, name="adamw_small", out_shape=shapes * 3, compiler_params=_params(0))(*ws, *gs, *ms, *vs)
    return outs[:n], outs[n:2 * n], outs[2 * n:]


def _unstack_cols(w):
    s, r, c = w.shape
    return w.transpose(1, 0, 2).reshape(r, s * c)


def _pad_rows(a, rows):
    return jnp.pad(a, ((0, rows - a.shape[0]), (0, LANES - a.shape[1])))


BIG = ("ffn1_w_gate", "ffn1_w_up", "ffn1_w_down", "w_in", "w_branch_a", "w_branch_b", "w_out",
       "ffn2_w_gate", "ffn2_w_up", "ffn2_w_down")
TRANSPOSED = ("ffn1_w_gate", "ffn1_w_up", "w_in", "ffn2_w_gate", "ffn2_w_up")
WEIGHTS = ("ffn1_norm", "ffn1_w_gate", "ffn1_w_up", "ffn1_w_down", "mix_norm", "w_in", "na_rpb", "sink_logit",
           "w_branch_a", "w_branch_b", "w_out", "ffn2_norm", "ffn2_w_gate", "ffn2_w_up", "ffn2_w_down", "final_norm")


def kernel(x, ffn1_norm, ffn1_w_gate, ffn1_w_up, ffn1_w_down, mix_norm, w_in, na_rpb, sink_logit, w_branch_a, w_branch_b, w_out, ffn2_norm, ffn2_w_gate, ffn2_w_up, ffn2_w_down, final_norm, loss_target, m_ffn1_norm, m_ffn1_w_gate, m_ffn1_w_up, m_ffn1_w_down, m_mix_norm, m_w_in, m_na_rpb, m_sink_logit, m_w_branch_a, m_w_branch_b, m_w_out, m_ffn2_norm, m_ffn2_w_gate, m_ffn2_w_up, m_ffn2_w_down, m_final_norm, v_ffn1_norm, v_ffn1_w_gate, v_ffn1_w_up, v_ffn1_w_down, v_mix_norm, v_w_in, v_na_rpb, v_sink_logit, v_w_branch_a, v_w_branch_b, v_w_out, v_ffn2_norm, v_ffn2_w_gate, v_ffn2_w_up, v_ffn2_w_down, v_final_norm):
    args = dict(locals())
    w = {k: args[k] for k in WEIGHTS}
    mom = {k: args["m_" + k] for k in WEIGHTS}
    var = {k: args["v_" + k] for k in WEIGHTS}
    cidx = lax.axis_index("c").astype(jnp.int32).reshape(1)
    chip = (2 * lax.axis_index("x") + lax.axis_index("y")).astype(jnp.int32).reshape(1)

    def shard(a, k):
        return jnp.swapaxes(a[0], 0, 1) if k in TRANSPOSED else a[0]

    def unshard(a, k):
        return (jnp.swapaxes(a, 0, 1) if k in TRANSPOSED else a)[None]

    def bf16_shards(names):
        return [shard(w[k], k).astype(BF16) for k in names]

    class comm:
        late_rider = _gather_rider(bf16_shards(MIXER + FFN2))

        @staticmethod
        def late(gathered):
            full = dict(zip(MIXER + FFN2, gathered))
            return (full["w_in"].reshape(D_IN, D_MODEL), _unstack_cols(full["w_branch_a"]), _unstack_cols(full["w_branch_b"]),
                    full["w_out"].reshape(D_MODEL, D_MODEL), tuple(full[k] for k in FFN2))

        @staticmethod
        def reduce(names, grads, *, tag):
            return _Reduce(names, grads, cidx, chip, tag=tag)

        @staticmethod
        def update(names, reduced):
            res = _adamw_halves([shard(w[k], k) for k in names], [reduced[k][0] for k in names],
                                [reduced[k][1] for k in names], [shard(mom[k], k) for k in names],
                                [shard(var[k], k) for k in names], cidx, name="adamw_" + names[0])
            for i, k in enumerate(names):
                grads_out[k], deltas[k], new_m[k], new_v[k] = (unshard(a[i], k) for a in res)

    deltas, new_m, new_v, grads_out, grad = {}, {}, {}, {}, {}
    f1 = _run_rider(_gather_rider(bf16_shards(FFN1)), name="all_gather_ffn1")
    out = _layer_grads(x[0], loss_target[0], ffn1_norm, f1, mix_norm, None, na_rpb[0], sink_logit[0], ffn2_norm,
                       final_norm.reshape(1, D_MODEL), comm=comm)

    rows = D_MODEL // LANES
    small = jnp.concatenate([
        out["ffn1_norm"].reshape(rows, LANES), out["mix_norm"].reshape(rows, LANES), out["ffn2_norm"].reshape(rows, LANES),
        out["final_norm"].reshape(rows, LANES), out["na_rpb"].reshape(-1, LANES),
        _pad_rows(out["sink_logit"].reshape(1, NB_HEADS), 8), _pad_rows(out["loss"], 8)], axis=0)
    total = _small_allreduce(small)
    n_rpb = NA_HEADS * 2 * NA_KH
    grad["ffn1_norm"] = total[0:rows].reshape(1, D_MODEL)
    grad["mix_norm"] = total[rows:2 * rows].reshape(1, D_MODEL)
    grad["ffn2_norm"] = total[2 * rows:3 * rows].reshape(1, D_MODEL)
    grad["final_norm"] = total[3 * rows:4 * rows].reshape(1, D_MODEL)
    grad["na_rpb"] = total[4 * rows:4 * rows + n_rpb].reshape(NA_HEADS, 2 * NA_KH, LANES)[:, :2 * NA_KH - 1, :2 * NA_KW - 1]
    grad["na_rpb"] = grad["na_rpb"].reshape(NA_HEADS, -1)
    grad["sink_logit"] = total[4 * rows + n_rpb:4 * rows + n_rpb + 1, 0:NB_HEADS]
    loss = total[4 * rows + n_rpb + 8, 0]

    small_names = [k for k in WEIGHTS if k not in BIG]
    res = _adamw_small(*[[a[k].reshape(grad[k].shape) for k in small_names] for a in (w, grad, mom, var)])
    for i, k in enumerate(small_names):
        grads_out[k], deltas[k], new_m[k], new_v[k] = (a.reshape(w[k].shape) for a in (grad[k], res[0][i], res[1][i], res[2][i]))
    return (loss, out["dx"].reshape(x.shape), *[grads_out[k] for k in WEIGHTS], *[deltas[k] for k in WEIGHTS],
            *[new_m[k] for k in WEIGHTS], *[new_v[k] for k in WEIGHTS])
```

```python
import math

import jax
import jax.numpy as jnp
import numpy as np
from jax import lax
from jax.experimental import pallas as pl
from jax.experimental.pallas import tpu as pltpu

F32 = jnp.float32
BF16 = jnp.bfloat16

D_MODEL = 1024
HEAD_DIM = 64
NA_HEADS = 8
NB_HEADS = 8
GRID_W = 64
NA_KH = 8
NA_KW = 16
WIN = 128
ROPE_THETA = 10000.0
EPS = 1e-6
N_CHIPS = 4
QK_SCALE = HEAD_DIM ** -0.5
NEG = -1e30
LANES = 128
VMEM_LIMIT = 56 * 1024 * 1024
HEAD_ROWS = 256
WGRAD_TOKENS_BYTES = 8192

C_QKVA = 3 * NA_HEADS * HEAD_DIM
C_QB = NB_HEADS * HEAD_DIM
C_KB = 2 * HEAD_DIM
C_ROPE = C_QB + C_KB
C_GATES = 2 * D_MODEL
D_IN = C_QKVA + C_QB + 2 * C_KB + C_GATES
O_QB = C_QKVA
O_KB = O_QB + C_QB
O_VB = O_KB + C_KB
O_G = O_VB + C_KB

ADAM_LR = 0.001
ADAM_B1 = 0.9
ADAM_B2 = 0.999
ADAM_EPS = 1e-08
ADAM_WD = 0.01
ADAM_STEP = 10

MESH = pl.DeviceIdType.MESH


def _dot(a, b):
    return jnp.dot(a, b, preferred_element_type=F32)


def _dot_nt(a, b):
    return lax.dot_general(a, b, (((1,), (1,)), ((), ())), preferred_element_type=F32)


def _dot_tn(a, b):
    return lax.dot_general(a, b, (((0,), (0,)), ((), ())), preferred_element_type=F32)


def _params(n_axes):
    return pltpu.CompilerParams(dimension_semantics=("arbitrary",) * n_axes, vmem_limit_bytes=VMEM_LIMIT)


def _rstd(xf):
    return lax.rsqrt(jnp.mean(xf * xf, axis=-1, keepdims=True) + EPS)


def _norm_bwd(dn, xf, g, r):
    xhat = xf * r
    dxh = dn * g
    dx = r * (dxh - xhat * jnp.mean(dxh * xhat, axis=-1, keepdims=True))
    return dx, dn * xhat


def _sigmoid(x):
    return 0.5 * jnp.tanh(0.5 * x) + 0.5


def _loss_head(hf, gv, tgt):
    r = _rstd(hf)
    err = (hf * r) * gv - tgt
    dx, dgr = _norm_bwd(err * (1.0 / hf.shape[-1]), hf, gv, r)
    return 0.5 * jnp.mean(err * err, axis=-1, keepdims=True), dx, dgr


def _ffn_fwd(x, g, wg, wu, wd, *, name, tm=1024, sub=512, rider=None, head=None):
    T, D = x.shape
    F = wg.shape[1]
    tm = min(tm, T)
    sub = min(sub, tm)
    n_head = 0 if head is None else 2

    def body(*refs):
        x_ref, g_ref, wg_ref, wu_ref, wd_ref = refs[:5]
        h_ref, n_ref, hdn_ref, p_ref, q_ref = refs[5 + n_head:10 + n_head]
        i, s = pl.program_id(0), pl.program_id(1)
        _ffn_fwd_step(x_ref, g_ref, wg_ref, wu_ref, wd_ref, h_ref, n_ref, hdn_ref, p_ref, q_ref, s)
        if head is not None:
            gf_ref, t_ref = refs[5:7]
            loss_ref, dgf_ref = refs[10 + n_head:]

            @pl.when((i == 0) & (s == 0))
            def _():
                loss_ref[...] = jnp.zeros_like(loss_ref)
                dgf_ref[...] = jnp.zeros_like(dgf_ref)

            @pl.when(s == N_CHIPS - 1)
            def _():
                for u in range(tm // HEAD_ROWS):
                    r = pl.ds(u * HEAD_ROWS, HEAD_ROWS)
                    terms, dh, dgr = _loss_head(h_ref[r, :], gf_ref[...], t_ref[r, :])
                    loss_ref[...] += jnp.broadcast_to(jnp.sum(terms), loss_ref.shape)
                    dgf_ref[...] += jnp.sum(dgr, axis=0, keepdims=True)
                    h_ref[r, :] = dh

    def _ffn_fwd_step(x_ref, g_ref, wg_ref, wu_ref, wd_ref, h_ref, n_ref, hdn_ref, p_ref, q_ref, s):

        @pl.when(s == 0)
        def _():
            xf = x_ref[...]
            n_ref[...] = ((xf * _rstd(xf)) * g_ref[...]).astype(BF16)
            h_ref[...] = xf

        rows = [pl.ds(u * sub, sub) for u in range(tm // sub)]
        ab = [(_dot_nt(n_ref[r, :], wg_ref[0]), _dot_nt(n_ref[r, :], wu_ref[0])) for r in rows]
        hdns = []
        for r, (a, b) in zip(rows, ab):
            sg = _sigmoid(a)
            silu = a * sg
            hdn = (silu * b).astype(BF16)
            hdn_ref[0, r, :] = hdn
            p_ref[0, r, :] = (b * (sg + silu * (1.0 - sg))).astype(BF16)
            q_ref[0, r, :] = silu.astype(BF16)
            hdns.append(hdn)
        for r, hdn in zip(rows, hdns):
            h_ref[r, :] += 0.5 * _dot(hdn, wd_ref[0])

    tok = pl.BlockSpec((tm, D), lambda i, s: (i, 0))
    hid = pl.BlockSpec((1, tm, F), lambda i, s: (s, i, 0))
    wspec = pl.BlockSpec((1, F, D), lambda i, s: (s, 0, 0))
    hshape = jax.ShapeDtypeStruct((N_CHIPS, T, F), BF16)
    grid = (T // tm, N_CHIPS)
    vec = pl.BlockSpec((1, D), lambda i, s: (0, 0))
    head_in, head_in_specs, head_out, head_out_specs = [], [], [], []
    if head is not None:
        head_in, head_in_specs = list(head), [vec, tok]
        head_out = [jax.ShapeDtypeStruct((1, LANES), F32), jax.ShapeDtypeStruct((1, D), F32)]
        head_out_specs = [pl.BlockSpec((1, LANES), lambda i, s: (0, 0)), vec]
    n_main = 5 + n_head
    body, r_in, r_in_specs, r_out, r_out_specs, scratch = _ride(body, n_main, n_main, rider, grid, (grid[0] * grid[1] * 7) // 8)
    outs = pl.pallas_call(
        body, name=name, grid=grid,
        in_specs=[tok, vec, wspec, wspec, wspec] + head_in_specs + r_in_specs,
        out_specs=[tok, tok, hid, hid, hid] + head_out_specs + r_out_specs,
        out_shape=[jax.ShapeDtypeStruct((T, D), F32), jax.ShapeDtypeStruct((T, D), BF16), hshape, hshape, hshape]
        + head_out + r_out,
        scratch_shapes=scratch,
        compiler_params=_params(2),
    )(x, g, wg, wu, wd, *head_in, *r_in)
    return (*outs[:n_main], list(outs[n_main:]))


def _ffn_bwd(dh, x, g, p, q, wg, wu, wd, *, name, tm=1024, sub=256, rider=None):
    T, D = x.shape
    F = wg.shape[1]
    tm = min(tm, T)
    sub = min(sub, tm)

    def body(dh_ref, x_ref, g_ref, p_ref, q_ref, wg_ref, wu_ref, wd_ref, dx_ref, da_ref, db_ref, dg_ref):
        i, s = pl.program_id(0), pl.program_id(1)

        @pl.when((i == 0) & (s == 0))
        def _():
            dg_ref[...] = jnp.zeros_like(dg_ref)

        @pl.when(s == 0)
        def _():
            dx_ref[...] = jnp.zeros_like(dx_ref)

        rows = [pl.ds(u * sub, sub) for u in range(tm // sub)]
        dhdn = [_dot_nt((0.5 * dh_ref[r, :]).astype(BF16), wd_ref[0]) for r in rows]
        das, dbs = [], []
        for r, dd in zip(rows, dhdn):
            da = (dd * p_ref[0, r, :].astype(F32)).astype(BF16)
            db = (dd * q_ref[0, r, :].astype(F32)).astype(BF16)
            da_ref[0, r, :] = da
            db_ref[0, r, :] = db
            das.append(da)
            dbs.append(db)
        for r, da, db in zip(rows, das, dbs):
            dx_ref[r, :] += _dot(da, wg_ref[0]) + _dot(db, wu_ref[0])

        @pl.when(s == N_CHIPS - 1)
        def _():
            xf = x_ref[...]
            dx, dgr = _norm_bwd(dx_ref[...], xf, g_ref[...], _rstd(xf))
            dg_ref[...] += jnp.sum(dgr, axis=0, keepdims=True)
            dx_ref[...] = dh_ref[...] + dx

    tok = pl.BlockSpec((tm, D), lambda i, s: (i, 0))
    hid = pl.BlockSpec((1, tm, F), lambda i, s: (s, i, 0))
    vec = pl.BlockSpec((1, D), lambda i, s: (0, 0))
    hshape = jax.ShapeDtypeStruct((N_CHIPS, T, F), BF16)
    wspec = pl.BlockSpec((1, F, D), lambda i, s: (s, 0, 0))
    grid = (T // tm, N_CHIPS)
    body, r_in, r_in_specs, r_out, r_out_specs, scratch = _ride(body, 8, 4, rider, grid, None)
    outs = pl.pallas_call(
        body, name=name, grid=grid,
        in_specs=[tok, tok, vec, hid, hid, wspec, wspec, wspec] + r_in_specs,
        out_specs=[tok, hid, hid, vec] + r_out_specs,
        out_shape=[jax.ShapeDtypeStruct((T, D), F32), hshape, hshape, jax.ShapeDtypeStruct((1, D), F32)] + r_out,
        scratch_shapes=scratch,
        compiler_params=_params(2),
    )(dh, x, g, p, q, wg, wu, wd, *r_in)
    return (*outs[:4], list(outs[4:]))


def _wgrad(a, b, *, a_block, a_map, b_block, b_map, out_shape, o_block, o_map, grid, scale=1.0, name, rider=None):
    def body(a_ref, b_ref, o_ref):
        @pl.when(pl.program_id(len(grid) - 1) == 0)
        def _():
            o_ref[...] = jnp.zeros_like(o_ref)

        av = a_ref[...]
        bv = b_ref[...]
        av = av.reshape(av.shape[-2:]).astype(BF16)
        bv = bv.reshape(bv.shape[-2:])
        if scale != 1.0:
            bv = scale * bv
        o_ref[...] += _dot_tn(av, bv.astype(BF16)).reshape(o_ref.shape)

    body, r_in, r_in_specs, r_out, r_out_specs, scratch = _ride(body, 2, 1, rider, grid, None)
    outs = pl.pallas_call(
        body, name=name, grid=grid,
        in_specs=[pl.BlockSpec(a_block, a_map), pl.BlockSpec(b_block, b_map)] + r_in_specs,
        out_specs=[pl.BlockSpec(o_block, o_map)] + r_out_specs,
        out_shape=[jax.ShapeDtypeStruct(out_shape, F32)] + r_out,
        scratch_shapes=scratch,
        compiler_params=_params(len(grid)),
    )(a, b, *r_in)
    return outs[0], list(outs[1:])


def _wgrad_rows(a, b, n_blocks, *, name, tk=2048):
    T, N = b.shape
    M = a.shape[1] // n_blocks
    tk = min(tk, T)
    return _wgrad(a, b, a_block=(tk, M), a_map=lambda s, k: (k, s), b_block=(tk, N), b_map=lambda s, k: (k, 0),
                  out_shape=(n_blocks, M, N), o_block=(1, M, N), o_map=lambda s, k: (s, 0, 0), grid=(n_blocks, T // tk), name=name)


def _wgrad_shard_a(a, b, *, name, scale=1.0, rider=None):
    S, T, M = a.shape
    N = b.shape[1]
    tk = min(WGRAD_TOKENS_BYTES // b.dtype.itemsize, T)
    return _wgrad(a, b, a_block=(1, tk, M), a_map=lambda s, k: (s, k, 0), b_block=(tk, N), b_map=lambda s, k: (k, 0),
                  out_shape=(S, M, N), o_block=(1, M, N), o_map=lambda s, k: (s, 0, 0), grid=(S, T // tk), scale=scale,
                  name=name, rider=rider)


def _wgrad_cols(a, b, n_blocks, *, name, tk=2048):
    T, M = a.shape
    N = b.shape[1] // n_blocks
    tk = min(tk, T)

    def body(a_ref, b_ref, o_ref):
        @pl.when(pl.program_id(0) == 0)
        def _():
            o_ref[...] = jnp.zeros_like(o_ref)

        r = _dot_tn(a_ref[...].astype(BF16), b_ref[...].astype(BF16))
        for s in range(n_blocks):
            o_ref[s] += r[:, s * N:(s + 1) * N]

    return pl.pallas_call(
        body, name=name, grid=(T // tk,),
        in_specs=[pl.BlockSpec((tk, M), lambda k: (k, 0)), pl.BlockSpec((tk, n_blocks * N), lambda k: (k, 0))],
        out_specs=pl.BlockSpec((n_blocks, M, N), lambda k: (0, 0, 0)),
        out_shape=jax.ShapeDtypeStruct((n_blocks, M, N), F32),
        compiler_params=_params(1),
    )(a, b)


def _rope_tables(T):
    half = HEAD_DIM // 2
    inv = np.float32(ROPE_THETA) ** (-np.arange(half, dtype=np.float32) / np.float32(half))
    ang = np.arange(T, dtype=np.float32)[:, None] * inv[None, :]
    cos, sin, zero = np.cos(ang), np.sin(ang), np.zeros_like(ang)
    reps = LANES // HEAD_DIM
    return (jnp.asarray(np.tile(np.concatenate([cos, cos], axis=1), (1, reps))),
            jnp.asarray(np.tile(np.concatenate([-sin, zero], axis=1), (1, reps))),
            jnp.asarray(np.tile(np.concatenate([zero, sin], axis=1), (1, reps))))


def _rope(x, cos, sa, sb, sign):
    half = HEAD_DIM // 2
    return x * cos + sign * (pltpu.roll(x, LANES - half, 1) * sa + pltpu.roll(x, half, 1) * sb)


def _mix_in_fwd(h, g, w_in, tables, *, tm=512):
    T, D = h.shape

    def body(h_ref, g_ref, w_ref, cos_ref, sa_ref, sb_ref, u_ref, qkva_ref, qb_ref, kvb_ref, gates_ref):
        hf = h_ref[...]
        u = ((hf * _rstd(hf)) * g_ref[...]).astype(BF16)
        u_ref[...] = u
        qkva_ref[...] = _dot_nt(u, w_ref[0:C_QKVA, :]).astype(BF16)
        zr = _dot_nt(u, w_ref[O_QB:O_QB + C_ROPE, :])
        cos, sa, sb = cos_ref[...], sa_ref[...], sb_ref[...]
        for j in range(C_ROPE // LANES):
            rj = _rope(zr[:, j * LANES:(j + 1) * LANES], cos, sa, sb, 1.0).astype(BF16)
            if j < C_QB // LANES:
                qb_ref[:, j * LANES:(j + 1) * LANES] = rj
            else:
                kvb_ref[:, 0:C_KB] = rj
        kvb_ref[:, C_KB:2 * C_KB] = _dot_nt(u, w_ref[O_VB:O_VB + C_KB, :]).astype(BF16)
        gates_ref[...] = _dot_nt(u, w_ref[O_G:O_G + C_GATES, :])

    def tok(n):
        return pl.BlockSpec((tm, n), lambda i: (i, 0))

    return pl.pallas_call(
        body, name="mix_in_fwd", grid=(T // tm,),
        in_specs=[tok(D), pl.BlockSpec((1, D), lambda i: (0, 0)), pl.BlockSpec((D_IN, D), lambda i: (0, 0), pipeline_mode=pl.Buffered(1)),
                  tok(LANES), tok(LANES), tok(LANES)],
        out_specs=[tok(D), tok(C_QKVA), tok(C_QB), tok(2 * C_KB), tok(C_GATES)],
        out_shape=[jax.ShapeDtypeStruct((T, D), BF16), jax.ShapeDtypeStruct((T, C_QKVA), BF16),
                   jax.ShapeDtypeStruct((T, C_QB), BF16), jax.ShapeDtypeStruct((T, 2 * C_KB), BF16),
                   jax.ShapeDtypeStruct((T, C_GATES), F32)],
        compiler_params=_params(1),
    )(h, g, w_in, *tables)


def _mix_in_bwd(dqa, dka, dva, dqb, dkb, dvb, dgates, h, g, dres, w_in, tables, *, tm=512, rider=None):
    T, D = h.shape

    def body(dqa_ref, dka_ref, dva_ref, dqb_ref, dkb_ref, dvb_ref, dgt_ref, h_ref, g_ref, dres_ref, w_ref,
             cos_ref, sa_ref, sb_ref, dz_ref, dh_ref, dg_ref):
        @pl.when(pl.program_id(0) == 0)
        def _():
            dg_ref[...] = jnp.zeros_like(dg_ref)

        na = NA_HEADS * HEAD_DIM
        dz_ref[:, 0:na] = dqa_ref[...].astype(BF16)
        dz_ref[:, na:2 * na] = dka_ref[...].astype(BF16)
        dz_ref[:, 2 * na:3 * na] = dva_ref[...].astype(BF16)
        cos, sa, sb = cos_ref[...], sa_ref[...], sb_ref[...]
        for j in range(C_QB // LANES):
            dz_ref[:, O_QB + j * LANES:O_QB + (j + 1) * LANES] = _rope(
                dqb_ref[:, j * LANES:(j + 1) * LANES], cos, sa, sb, -1.0).astype(BF16)
        dz_ref[:, O_KB:O_KB + C_KB] = _rope(dkb_ref[...], cos, sa, sb, -1.0).astype(BF16)
        dz_ref[:, O_VB:O_VB + C_KB] = dvb_ref[...].astype(BF16)
        dz_ref[:, O_G:O_G + C_GATES] = dgt_ref[...].astype(BF16)
        du = _dot(dz_ref[...], w_ref[...])
        hf = h_ref[...]
        dx, dgr = _norm_bwd(du, hf, g_ref[...], _rstd(hf))
        dg_ref[...] += jnp.sum(dgr, axis=0, keepdims=True)
        dh_ref[...] = dres_ref[...] + dx

    def tok(n):
        return pl.BlockSpec((tm, n), lambda i: (i, 0))

    vec = pl.BlockSpec((1, D), lambda i: (0, 0))
    na = NA_HEADS * HEAD_DIM
    grid = (T // tm,)
    body, r_in, r_in_specs, r_out, r_out_specs, scratch = _ride(body, 14, 3, rider, grid, None)
    outs = pl.pallas_call(
        body, name="mix_in_bwd", grid=grid,
        in_specs=[tok(na), tok(na), tok(na), tok(C_QB), tok(C_KB), tok(C_KB), tok(C_GATES), tok(D), vec, tok(D),
                  pl.BlockSpec((D_IN, D), lambda i: (0, 0), pipeline_mode=pl.Buffered(1)), tok(LANES), tok(LANES), tok(LANES)]
        + r_in_specs,
        out_specs=[tok(D_IN), tok(D), vec] + r_out_specs,
        out_shape=[jax.ShapeDtypeStruct((T, D_IN), BF16), jax.ShapeDtypeStruct((T, D), F32),
                   jax.ShapeDtypeStruct((1, D), F32)] + r_out,
        scratch_shapes=scratch,
        compiler_params=_params(1),
    )(dqa, dka, dva, dqb, dkb, dvb, dgates, h, g, dres, w_in, *tables, *r_in)
    return (*outs[:3], list(outs[3:]))


def _na_bias_slabs(rpb):
    H = rpb.shape[0]
    ncell = GRID_W * GRID_W
    cell = np.arange(ncell)
    co = cell % GRID_W - cell // GRID_W + (NA_KW - 1)
    e_co = jnp.asarray((np.arange(LANES)[:, None] == co[None, :]).astype(np.float32))
    table = jnp.pad(rpb, ((0, 0), (0, 1), (0, LANES - rpb.shape[2]))).reshape(H * 2 * NA_KH, LANES)

    def body(t_ref, e_ref, o_ref):
        o_ref[...] = jnp.dot(t_ref[...], e_ref[...], preferred_element_type=F32, precision=lax.Precision.HIGHEST)

    toeplitz = pl.pallas_call(
        body, name="rpb_unfold", out_shape=jax.ShapeDtypeStruct((H * 2 * NA_KH, ncell), F32),
        compiler_params=_params(0),
    )(table, e_co).reshape(H, 2 * NA_KH, GRID_W, GRID_W)

    def assemble(tz_ref, o_ref):
        c = lax.broadcasted_iota(jnp.int32, (GRID_W, GRID_W), 0)
        k = lax.broadcasted_iota(jnp.int32, (GRID_W, GRID_W), 1)
        cs = jnp.clip(c - NA_KW // 2, 0, GRID_W - NA_KW)
        inwin = (k >= cs) & (k < cs + NA_KW)
        for ro0 in range(NA_KH):
            for hh in range(2):
                for i in range(NA_KH):
                    o_ref[0, ro0, hh * GRID_W:(hh + 1) * GRID_W, i * GRID_W:(i + 1) * GRID_W] = jnp.where(
                        inwin, tz_ref[hh, ro0 + i], NEG)

    return pl.pallas_call(
        assemble, name="na_bias_slabs", grid=(H // 2,),
        in_specs=[pl.BlockSpec((2, 2 * NA_KH, GRID_W, GRID_W), lambda p: (p, 0, 0, 0))],
        out_specs=pl.BlockSpec((1, NA_KH, 2 * GRID_W, NA_KH * GRID_W), lambda p: (p, 0, 0, 0)),
        out_shape=jax.ShapeDtypeStruct((H // 2, NA_KH, 2 * GRID_W, NA_KH * GRID_W), F32),
        compiler_params=_params(1),
    )(toeplitz)


def _half_masks(rows):
    lane = lax.broadcasted_iota(jnp.int32, (rows, LANES), 1)
    left = lane < HEAD_DIM
    return left, (left, jnp.logical_not(left))


def _stack_heads(x):
    left, halves = _half_masks(x.shape[0])
    xf = x.astype(F32)
    return jnp.concatenate([jnp.where(m, xf, 0.0).astype(BF16) for m in halves], axis=0)


def _unstack_heads(o):
    rows = o.shape[0] // 2
    left, _ = _half_masks(rows)
    return jnp.where(left, o[:rows], o[rows:])


def _na_row(j, t, rb, rows):
    r = j * rb + t
    rs = jnp.clip(r - NA_KH // 2, 0, rows - NA_KH)
    return pl.multiple_of(t * GRID_W, GRID_W), pl.multiple_of(rs * GRID_W, GRID_W), rs - r + (NA_KH - 1)


def _na_specs(T, rb):
    qrows = GRID_W * rb
    pairs = NA_HEADS // 2
    return ([pl.BlockSpec((qrows, LANES), lambda p, j: (j, p)),
             pl.BlockSpec((T, LANES), lambda p, j: (0, pairs + p)),
             pl.BlockSpec((T, LANES), lambda p, j: (0, 2 * pairs + p))],
            pl.BlockSpec((1, NA_KH, 2 * GRID_W, NA_KH * GRID_W), lambda p, j: (p, 0, 0, 0)))


def _softmax(s):
    p = jnp.exp(s - jnp.max(s, axis=-1, keepdims=True))
    return p / jnp.sum(p, axis=-1, keepdims=True)


def _na_fwd(qkva, bias, *, rb=32, group=32):
    T = qkva.shape[0]
    rows = T // GRID_W
    nkeys = NA_KH * GRID_W
    rb = min(rb, rows)
    group = min(group, rb)

    def body(q_ref, k_ref, v_ref, bias_ref, y_ref):
        j = pl.program_id(1)

        def rows_step(t, carry):
            at = [_na_row(j, t * group + u, rb, rows) for u in range(group)]
            s = [_dot_nt(_stack_heads(q_ref[pl.ds(q0, GRID_W), :]), k_ref[pl.ds(k0, nkeys), :]) for q0, k0, _ in at]
            p = [_softmax(su * QK_SCALE + bias_ref[0, ro0]) for su, (_, _, ro0) in zip(s, at)]
            o = [_dot(pu.astype(BF16), v_ref[pl.ds(k0, nkeys), :]) for pu, (_, k0, _) in zip(p, at)]
            for ou, (q0, _, _) in zip(o, at):
                y_ref[pl.ds(q0, GRID_W), :] = _unstack_heads(ou).astype(BF16)
            return carry

        lax.fori_loop(0, rb // group, rows_step, 0)

    qkv_specs, bias_spec = _na_specs(T, rb)
    return pl.pallas_call(
        body, name="na_fwd", grid=(NA_HEADS // 2, rows // rb),
        in_specs=qkv_specs + [bias_spec],
        out_specs=qkv_specs[0],
        out_shape=jax.ShapeDtypeStruct((T, NA_HEADS * HEAD_DIM), BF16),
        compiler_params=_params(2),
    )(qkva, qkva, qkva, bias)


def _na_bwd(qkva, dy, bias, *, rb=16, group=16, rider=None):
    T = qkva.shape[0]
    rows = T // GRID_W
    nkeys = NA_KH * GRID_W
    rb = min(rb, rows)
    group = min(group, rb)

    def body(q_ref, k_ref, v_ref, dy_ref, bias_ref, dq_ref, dk_ref, dv_ref, dbias_ref):
        j = pl.program_id(1)

        @pl.when(j == 0)
        def _():
            dk_ref[...] = jnp.zeros_like(dk_ref)
            dv_ref[...] = jnp.zeros_like(dv_ref)
            dbias_ref[...] = jnp.zeros_like(dbias_ref)

        def rows_step(t, carry):
            at = [_na_row(j, t * group + u, rb, rows) for u in range(group)]
            qs = [_stack_heads(q_ref[pl.ds(q0, GRID_W), :]) for q0, _, _ in at]
            dys = [_stack_heads(dy_ref[pl.ds(q0, GRID_W), :]) for q0, _, _ in at]
            s = [_dot_nt(qu, k_ref[pl.ds(k0, nkeys), :]) for qu, (_, k0, _) in zip(qs, at)]
            dp = [_dot_nt(du, v_ref[pl.ds(k0, nkeys), :]) for du, (_, k0, _) in zip(dys, at)]
            p = [_softmax(su * QK_SCALE + bias_ref[0, ro0]) for su, (_, _, ro0) in zip(s, at)]
            ds = [pu * (du - jnp.sum(pu * du, axis=-1, keepdims=True)) for pu, du in zip(p, dp)]
            for u, (q0, k0, ro0) in enumerate(at):
                dbias_ref[0, ro0] += ds[u]
                dsb = ds[u].astype(BF16)
                dq_ref[pl.ds(q0, GRID_W), :] = (_unstack_heads(_dot(dsb, k_ref[pl.ds(k0, nkeys), :])) * QK_SCALE).astype(BF16)
                dk_ref[pl.ds(k0, nkeys), :] += _dot_tn(dsb, qs[u]) * QK_SCALE
                dv_ref[pl.ds(k0, nkeys), :] += _dot_tn(p[u].astype(BF16), dys[u])
            return carry

        lax.fori_loop(0, rb // group, rows_step, 0)

    qkv_specs, bias_spec = _na_specs(T, rb)
    width = NA_HEADS * HEAD_DIM
    kv_out = pl.BlockSpec((T, LANES), lambda p, j: (0, p))
    grid = (NA_HEADS // 2, rows // rb)
    body, r_in, r_in_specs, r_out, r_out_specs, scratch = _ride(body, 5, 4, rider, grid, None)
    outs = pl.pallas_call(
        body, name="na_bwd", grid=grid,
        in_specs=qkv_specs + [qkv_specs[0], bias_spec] + r_in_specs,
        out_specs=[qkv_specs[0], kv_out, kv_out, bias_spec] + r_out_specs,
        out_shape=[jax.ShapeDtypeStruct((T, width), BF16), jax.ShapeDtypeStruct((T, width), F32),
                   jax.ShapeDtypeStruct((T, width), F32), jax.ShapeDtypeStruct(bias.shape, F32)] + r_out,
        scratch_shapes=scratch,
        compiler_params=_params(2),
    )(qkva, qkva, qkva, dy, bias, *r_in)
    return (*outs[:4], list(outs[4:]))


def _rpb_fold(dslab):
    pairs = dslab.shape[0]
    H = 2 * pairs
    ncell = GRID_W * GRID_W

    def disassemble(d_ref, tz_ref):
        tz_ref[...] = jnp.zeros_like(tz_ref)
        for ro0 in range(NA_KH):
            for hh in range(2):
                for i in range(NA_KH):
                    tz_ref[hh, ro0 + i] += d_ref[0, ro0, hh * GRID_W:(hh + 1) * GRID_W, i * GRID_W:(i + 1) * GRID_W]

    dtoeplitz = pl.pallas_call(
        disassemble, name="rpb_fold_tiles", grid=(pairs,),
        in_specs=[pl.BlockSpec((1, NA_KH, 2 * GRID_W, NA_KH * GRID_W), lambda p: (p, 0, 0, 0))],
        out_specs=pl.BlockSpec((2, 2 * NA_KH, GRID_W, GRID_W), lambda p: (p, 0, 0, 0)),
        out_shape=jax.ShapeDtypeStruct((H, 2 * NA_KH, GRID_W, GRID_W), F32),
        compiler_params=_params(1),
    )(dslab).reshape(H * 2 * NA_KH, ncell)
    cell = np.arange(ncell)
    co = cell % GRID_W - cell // GRID_W + (NA_KW - 1)
    e_co = jnp.asarray((co[:, None] == np.arange(LANES)[None, :]).astype(np.float32))

    def diagonals(x_ref, e_ref, o_ref):
        o_ref[...] = jnp.dot(x_ref[...], e_ref[...], preferred_element_type=F32, precision=lax.Precision.HIGHEST)

    return pl.pallas_call(
        diagonals, name="rpb_fold", out_shape=jax.ShapeDtypeStruct((H * 2 * NA_KH, LANES), F32),
        compiler_params=_params(0),
    )(dtoeplitz, e_co).reshape(H, 2 * NA_KH, LANES)


SWA_KEYS = 3 * WIN


def _swa_block(j, t, qbn, T):
    blk = j * qbn + t
    start = jnp.clip((blk - 1) * WIN, 0, T - SWA_KEYS)
    row = lax.broadcasted_iota(jnp.int32, (2 * WIN, SWA_KEYS), 0)
    qpos = blk * WIN + jnp.where(row < WIN, row, row - WIN)
    kpos = start + lax.broadcasted_iota(jnp.int32, (2 * WIN, SWA_KEYS), 1)
    return pl.multiple_of(t * WIN, WIN), pl.multiple_of(start, WIN), jnp.abs(qpos - kpos) <= WIN


def _swa_sinks(sink_ref, p):
    row = lax.broadcasted_iota(jnp.int32, (2 * WIN, 1), 0)
    return jnp.where(row < WIN, sink_ref[p], sink_ref[p + NB_HEADS // 2])


def _swa_probs(s, mask, sink):
    s = jnp.where(mask, s * QK_SCALE, NEG)
    m = jnp.maximum(jnp.max(s, axis=-1, keepdims=True), sink)
    e = jnp.exp(s - m)
    esink = jnp.exp(sink - m)
    den = jnp.sum(e, axis=-1, keepdims=True) + esink
    return e / den, esink / den


def _swa_specs(T, qbn):
    return [pl.BlockSpec(memory_space=pltpu.SMEM),
            pl.BlockSpec((WIN * qbn, LANES), lambda p, j: (j, p)),
            pl.BlockSpec((T, LANES), lambda p, j: (0, 0)),
            pl.BlockSpec((T, LANES), lambda p, j: (0, 1))]


def _swa_fwd(qb, kvb, sink, *, qbn=32, group=32):
    T = qb.shape[0]
    pairs = NB_HEADS // 2
    qbn = min(qbn, T // WIN)
    group = min(group, qbn)

    def body(sink_ref, q_ref, k_ref, v_ref, y_ref):
        p, j = pl.program_id(0), pl.program_id(1)
        sinks = _swa_sinks(sink_ref, p)

        def blocks_step(t, carry):
            at = [_swa_block(j, t * group + u, qbn, T) for u in range(group)]
            s = [_dot_nt(_stack_heads(q_ref[pl.ds(q0, WIN), :]), k_ref[pl.ds(k0, SWA_KEYS), :]) for q0, k0, _ in at]
            pr = [_swa_probs(su, mask, sinks)[0] for su, (_, _, mask) in zip(s, at)]
            o = [_dot(pu.astype(BF16), v_ref[pl.ds(k0, SWA_KEYS), :]) for pu, (_, k0, _) in zip(pr, at)]
            for ou, (q0, _, _) in zip(o, at):
                y_ref[pl.ds(q0, WIN), :] = _unstack_heads(ou).astype(BF16)
            return carry

        lax.fori_loop(0, qbn // group, blocks_step, 0)

    specs = _swa_specs(T, qbn)
    return pl.pallas_call(
        body, name="swa_fwd", grid=(pairs, T // (WIN * qbn)),
        in_specs=specs, out_specs=specs[1],
        out_shape=jax.ShapeDtypeStruct((T, NB_HEADS * HEAD_DIM), BF16),
        compiler_params=_params(2),
    )(sink, qb, kvb, kvb)


def _swa_bwd(qb, kvb, dy, sink, *, qbn=16, group=16, rider=None):
    T = qb.shape[0]
    pairs = NB_HEADS // 2
    qbn = min(qbn, T // WIN)
    group = min(group, qbn)

    def body(sink_ref, q_ref, k_ref, v_ref, dy_ref, dq_ref, dk_ref, dv_ref, dsink_ref):
        p, j = pl.program_id(0), pl.program_id(1)
        sinks = _swa_sinks(sink_ref, p)

        @pl.when((p == 0) & (j == 0))
        def _():
            dk_ref[...] = jnp.zeros_like(dk_ref)
            dv_ref[...] = jnp.zeros_like(dv_ref)

        @pl.when(j == 0)
        def _():
            dsink_ref[...] = jnp.zeros_like(dsink_ref)

        def blocks_step(t, carry):
            at = [_swa_block(j, t * group + u, qbn, T) for u in range(group)]
            qs = [_stack_heads(q_ref[pl.ds(q0, WIN), :]) for q0, _, _ in at]
            dys = [_stack_heads(dy_ref[pl.ds(q0, WIN), :]) for q0, _, _ in at]
            s = [_dot_nt(qu, k_ref[pl.ds(k0, SWA_KEYS), :]) for qu, (_, k0, _) in zip(qs, at)]
            dp = [_dot_nt(du, v_ref[pl.ds(k0, SWA_KEYS), :]) for du, (_, k0, _) in zip(dys, at)]
            probs = [_swa_probs(su, mask, sinks) for su, (_, _, mask) in zip(s, at)]
            for u, (q0, k0, _) in enumerate(at):
                pr, psink = probs[u]
                delta = jnp.sum(pr * dp[u], axis=-1, keepdims=True)
                dsb = (pr * (dp[u] - delta)).astype(BF16)
                dsk = psink * delta
                for hh in range(2):
                    dsink_ref[0, hh:hh + 1, :] += jnp.broadcast_to(-jnp.sum(dsk[hh * WIN:(hh + 1) * WIN]), (1, LANES))
                dq_ref[pl.ds(q0, WIN), :] = _unstack_heads(_dot(dsb, k_ref[pl.ds(k0, SWA_KEYS), :])) * QK_SCALE
                dk_ref[pl.ds(k0, SWA_KEYS), :] += _dot_tn(dsb, qs[u]) * QK_SCALE
                dv_ref[pl.ds(k0, SWA_KEYS), :] += _dot_tn(pr.astype(BF16), dys[u])
            return carry

        lax.fori_loop(0, qbn // group, blocks_step, 0)

    specs = _swa_specs(T, qbn)
    kv_out = pl.BlockSpec((T, LANES), lambda p, j: (0, 0))
    grid = (pairs, T // (WIN * qbn))
    body, r_in, r_in_specs, r_out, r_out_specs, scratch = _ride(body, 5, 4, rider, grid, None)
    outs = pl.pallas_call(
        body, name="swa_bwd", grid=grid,
        in_specs=specs + [specs[1]] + r_in_specs,
        out_specs=[specs[1], kv_out, kv_out, pl.BlockSpec((1, 8, LANES), lambda p, j: (p, 0, 0))] + r_out_specs,
        out_shape=[jax.ShapeDtypeStruct((T, NB_HEADS * HEAD_DIM), F32), jax.ShapeDtypeStruct((T, LANES), F32),
                   jax.ShapeDtypeStruct((T, LANES), F32), jax.ShapeDtypeStruct((pairs, 8, LANES), F32)] + r_out,
        scratch_shapes=scratch,
        compiler_params=_params(2),
    )(sink, qb, kvb, kvb, dy, *r_in)
    return (*outs[:4], list(outs[4:]))


def _merge_fwd(ya, yb, gates, wa, wb, wout, h, *, tm=512):
    T, D = h.shape
    W = ya.shape[1]

    def body(ya_ref, yb_ref, gt_ref, wa_ref, wb_ref, wo_ref, h_ref, h2_ref, mg_ref):
        pa = _dot(ya_ref[...], wa_ref[...])
        pb = _dot(yb_ref[...], wb_ref[...])
        mg = (jax.nn.sigmoid(gt_ref[:, 0:D]) * pa + jax.nn.sigmoid(gt_ref[:, D:2 * D]) * pb).astype(BF16)
        mg_ref[...] = mg
        h2_ref[...] = h_ref[...] + _dot(mg, wo_ref[...])

    def tok(n):
        return pl.BlockSpec((tm, n), lambda i: (i, 0))

    def full(r, c):
        return pl.BlockSpec((r, c), lambda i: (0, 0))

    return pl.pallas_call(
        body, name="merge_fwd", grid=(T // tm,),
        in_specs=[tok(W), tok(W), tok(2 * D), full(W, D), full(W, D), full(D, D), tok(D)],
        out_specs=[tok(D), tok(D)],
        out_shape=[jax.ShapeDtypeStruct((T, D), F32), jax.ShapeDtypeStruct((T, D), BF16)],
        compiler_params=_params(1),
    )(ya, yb, gates, wa, wb, wout, h)


def _merge_bwd(dh, ya, yb, gates, wa, wb, wout, *, tm=512, rider=None):
    T, D = dh.shape
    W = ya.shape[1]

    def body(dh_ref, ya_ref, yb_ref, gt_ref, wa_ref, wb_ref, wo_ref, dya_ref, dyb_ref, dpa_ref, dpb_ref, dgt_ref):
        dmg = _dot_nt(dh_ref[...].astype(BF16), wo_ref[...])
        for y_ref, w_ref, dy_ref, dp_ref, lo in ((ya_ref, wa_ref, dya_ref, dpa_ref, 0), (yb_ref, wb_ref, dyb_ref, dpb_ref, D)):
            sg = jax.nn.sigmoid(gt_ref[:, lo:lo + D])
            dp = (dmg * sg).astype(BF16)
            dp_ref[...] = dp
            dgt_ref[:, lo:lo + D] = (dmg * _dot(y_ref[...], w_ref[...]) * (sg * (1.0 - sg))).astype(BF16)
            dy_ref[...] = _dot_nt(dp, w_ref[...]).astype(BF16)

    def tok(n):
        return pl.BlockSpec((tm, n), lambda i: (i, 0))

    def full(r, c):
        return pl.BlockSpec((r, c), lambda i: (0, 0))

    grid = (T // tm,)
    body, r_in, r_in_specs, r_out, r_out_specs, scratch = _ride(body, 7, 5, rider, grid, None)
    outs = pl.pallas_call(
        body, name="merge_bwd", grid=grid,
        in_specs=[tok(D), tok(W), tok(W), tok(2 * D), full(W, D), full(W, D), full(D, D)] + r_in_specs,
        out_specs=[tok(W), tok(W), tok(D), tok(D), tok(2 * D)] + r_out_specs,
        out_shape=[jax.ShapeDtypeStruct((T, W), BF16), jax.ShapeDtypeStruct((T, W), BF16),
                   jax.ShapeDtypeStruct((T, D), BF16), jax.ShapeDtypeStruct((T, D), BF16),
                   jax.ShapeDtypeStruct((T, 2 * D), BF16)] + r_out,
        scratch_shapes=scratch,
        compiler_params=_params(1),
    )(dh, ya, yb, gates, wa, wb, wout, *r_in)
    return (*outs[:5], list(outs[5:]))


def _pair_heads(a, axis):
    shp = a.shape
    a = a.reshape(shp[:axis] + (2, NB_HEADS // 2, HEAD_DIM) + shp[axis + 1:])
    return jnp.swapaxes(a, axis, axis + 1).reshape(shp)


def _unpair_heads(a, axis):
    shp = a.shape
    a = a.reshape(shp[:axis] + (NB_HEADS // 2, 2, HEAD_DIM) + shp[axis + 1:])
    return jnp.swapaxes(a, axis, axis + 1).reshape(shp)


FFN1 = ("ffn1_w_gate", "ffn1_w_up", "ffn1_w_down")
FFN2 = ("ffn2_w_gate", "ffn2_w_up", "ffn2_w_down")
MIXER = ("w_in", "w_branch_a", "w_branch_b", "w_out")
BRANCH = MIXER[1:]


def _layer_grads(x, target, g1, f1, gmix, late, rpb, sink, g2, gfin, comm=None):
    T = x.shape[0]
    tables = _rope_tables(T)
    bias = _na_bias_slabs(rpb)

    comm = comm or _Local(late)
    h1, n1, hdn1, p1, q1, gathered = _ffn_fwd(x, g1, *f1, name="ffn1_fwd", rider=comm.late_rider)
    w_in_t, wa, wb, wout, f2 = comm.late(gathered)
    w_in_p = jnp.concatenate([w_in_t[:O_QB], _pair_heads(w_in_t[O_QB:O_KB], 0), w_in_t[O_KB:]], axis=0)
    wb_p = _pair_heads(wb, 0)
    u, qkva, qb, kvb, gates = _mix_in_fwd(h1, gmix, w_in_p, tables)
    ya = _na_fwd(qkva, bias)
    yb = _swa_fwd(qb, kvb, sink)
    h2, merged = _merge_fwd(ya, yb, gates, wa, wb_p, wout, h1)
    dh3, n2, hdn2, p2, q2, loss, dgfin, _ = _ffn_fwd(h2, g2, *f2, name="ffn2_fwd", head=(gfin, target))

    dh2, da2, db2, dg2, _ = _ffn_bwd(dh3, h2, g2, p2, q2, *f2, name="ffn2_bwd")
    df2 = [_wgrad_shard_a(da2, n2, name="ffn2_dwg")[0], _wgrad_shard_a(db2, n2, name="ffn2_dwu")[0],
           _wgrad_shard_a(hdn2, dh3, scale=0.5, name="ffn2_dwd")[0]]
    red2 = comm.reduce(FFN2, df2, tag="ffn2")
    dya, dyb, dpa, dpb, dgates, got = _merge_bwd(dh2, ya, yb, gates, wa, wb_p, wout, rider=red2.sibling)
    red2.partial(got)
    dwout = _wgrad_cols(merged, dh2, 1, name="dwout").reshape(N_CHIPS, D_MODEL // N_CHIPS, D_MODEL)
    dwa = _wgrad_cols(ya, dpa, N_CHIPS, name="dwa")
    dwb = _unpair_heads(_wgrad_cols(yb, dpb, N_CHIPS, name="dwb"), 1)
    redb = comm.reduce(BRANCH, [dwa, dwb, dwout], tag="branch")
    dqa, dka, dva, dbias, got = _na_bwd(qkva, dya, bias, rider=_two_riders(red2.chips, redb.sibling))
    red2.halves(got[:len(FFN2)])
    redb.partial(got[len(FFN2):])
    drpb = _rpb_fold(dbias)
    dqb, dkb, dvb, dsink, got = _swa_bwd(qb, kvb, dyb, sink, rider=_two_riders(red2.share, redb.chips))
    out = red2.result(got[:len(FFN2)])
    redb.halves(got[len(FFN2):])
    dz, dh1, dgmix, got = _mix_in_bwd(dqa, dka, dva, dqb, dkb, dvb, dgates, h1, gmix, dh2, w_in_p, tables, rider=redb.share)
    out.update(redb.result(got))
    dwin_p = _wgrad_rows(dz, u, 2, name="dwin")[0].reshape(D_IN, D_MODEL)
    dwin = jnp.concatenate([dwin_p[:O_QB], _unpair_heads(dwin_p[O_QB:O_KB], 0), dwin_p[O_KB:]], axis=0)
    dx, da1, db1, dg1, _ = _ffn_bwd(dh1, x, g1, p1, q1, *f1, name="ffn1_bwd")
    small = dict(loss=loss, ffn1_norm=dg1, mix_norm=dgmix, ffn2_norm=dg2, final_norm=dgfin, na_rpb=drpb,
                 sink_logit=dsink[:, 0:2, 0].T.reshape(NB_HEADS))
    redw = comm.reduce(("w_in",), [dwin.reshape(N_CHIPS, D_IN // N_CHIPS, D_MODEL)], tag="w_in").partial_now()
    dwg1, got = _wgrad_shard_a(da1, n1, name="ffn1_dwg", rider=_two_riders(redw.chips, comm.small(small)))
    comm.small_done(got[1:])
    dwu1, got = _wgrad_shard_a(db1, n1, name="ffn1_dwu", rider=redw.halves(got[:1]).share)
    out.update(redw.result(got))
    red1 = comm.reduce(FFN1[:2], [dwg1, dwu1], tag="ffn1_gate_up").partial_now()
    dwd1, got = _wgrad_shard_a(hdn1, dh1, scale=0.5, name="ffn1_dwd", rider=red1.chips)
    out.update(red1.halves(got).result_now())
    out.update(comm.reduce(FFN1[2:], [dwd1], tag="ffn1_down").partial_now().halves_now().result_now())
    for names in (FFN2, ("w_in",), BRANCH[:2], BRANCH[2:], FFN1[:2], FFN1[2:]):
        comm.update(names, out)
    out.update(small, dx=dx)
    return out


class _Local:
    late_rider = None

    def __init__(self, late):
        self._late = late

    def late(self, gathered):
        return self._late

    def reduce(self, names, grads, *, tag):
        return _LocalReduce(names, grads)

    def update(self, names, reduced):
        pass

    def small(self, grads):
        return None

    def small_done(self, got):
        pass


class _LocalReduce:
    sibling = chips = share = None

    def __init__(self, names, grads):
        self._result = dict(zip(names, grads))

    def partial(self, got=None):
        return self

    halves = partial_now = halves_now = partial

    def result(self, got=None):
        return self._result

    result_now = result


ANY = pl.BlockSpec(memory_space=pl.ANY)


def _place():
    x, y, c = lax.axis_index("x"), lax.axis_index("y"), lax.axis_index("c")
    chips = [(1 - x, y), (x, 1 - y), (1 - x, 1 - y)]
    return x, y, c, 2 * x + y, chips


def _remote(src, dst, send_sems, recv_sems, k, device):
    return pltpu.make_async_remote_copy(src_ref=src, dst_ref=dst, send_sem=send_sems.at[k], recv_sem=recv_sems.at[k],
                                        device_id=device, device_id_type=MESH)


class _Rider:
    def __init__(self, inputs, out_shape, scratch, start, middle, finish):
        self.inputs, self.out_shape, self.scratch = list(inputs), list(out_shape), list(scratch)
        self.start, self.middle, self.finish = start, middle, finish


def _two_riders(first, second):
    if first is None or second is None:
        return first or second
    assert first.middle is None and second.middle is None
    n_in, n_out, n_sem = len(first.inputs), len(first.out_shape), len(first.scratch)

    def phase(name):
        def run(ins, outs, sems):
            getattr(first, name)(ins[:n_in], outs[:n_out], sems[:n_sem])
            getattr(second, name)(ins[n_in:], outs[n_out:], sems[n_sem:])
        return run

    return _Rider(first.inputs + second.inputs, first.out_shape + second.out_shape, first.scratch + second.scratch,
                  phase("start"), None, phase("finish"))


def _run_rider(rider, *, name):
    n_in, n_out = len(rider.inputs), len(rider.out_shape)

    def body(*refs):
        ins, outs, sems = refs[:n_in], refs[n_in:n_in + n_out], refs[n_in + n_out:]
        rider.start(ins, outs, sems)
        if rider.middle is not None:
            rider.middle(ins, outs, sems)
        rider.finish(ins, outs, sems)

    return pl.pallas_call(body, name=name, in_specs=[ANY] * n_in, out_specs=[ANY] * n_out, out_shape=rider.out_shape,
                          scratch_shapes=rider.scratch)(*rider.inputs)


def _ride(body, n_in, n_out, rider, grid, middle_step):
    if rider is None:
        return body, [], [], [], [], []
    r_in, r_out = len(rider.inputs), len(rider.out_shape)
    steps = math.prod(grid)

    def riding(*refs):
        ins, r_ins = refs[:n_in], refs[n_in:n_in + r_in]
        outs = refs[n_in + r_in:n_in + r_in + n_out]
        r_outs = refs[n_in + r_in + n_out:n_in + r_in + n_out + r_out]
        sems = refs[n_in + r_in + n_out + r_out:]
        step = pl.program_id(0)
        for axis in range(1, len(grid)):
            step = step * grid[axis] + pl.program_id(axis)

        @pl.when(step == 0)
        def _():
            rider.start(r_ins, r_outs, sems)

        body(*ins, *outs)

        if rider.middle is not None:
            @pl.when(step == middle_step)
            def _():
                rider.middle(r_ins, r_outs, sems)

        @pl.when(step == steps - 1)
        def _():
            rider.finish(r_ins, r_outs, sems)

    return riding, rider.inputs, [ANY] * r_in, rider.out_shape, [ANY] * r_out, rider.scratch


def _gather_rider(shards):
    n = len(shards)

    def plan(ins, outs, sems, kinds):
        send_sems, recv_sems, own_send_sems, own_recv_sems = sems
        x, y, c, mine, chips = _place()
        sibling = (x, y, 1 - c)
        made = {k: [] for k in kinds}
        for i in range(n):
            hr = shards[i].shape[0] // 2
            if "own" in made:
                made["own"].append(_remote(ins[i], outs[i].at[mine], own_send_sems, own_recv_sems, i, sibling))
            for j, (cx, cy) in enumerate(chips):
                here = outs[i].at[2 * cx + cy, pl.ds(c * hr, hr)]
                there = outs[i].at[2 * cx + cy, pl.ds((1 - c) * hr, hr)]
                if "sends" in made:
                    made["sends"].append(_remote(ins[i].at[pl.ds(c * hr, hr)], outs[i].at[mine, pl.ds(c * hr, hr)],
                                                 send_sems, recv_sems, 6 * i + j, (cx, cy, c)))
                if "landed" in made:
                    made["landed"].append(_remote(here, here, send_sems, recv_sems, 6 * i + j, (cx, cy, c)))
                if "passes" in made:
                    made["passes"].append(_remote(here, here, send_sems, recv_sems, 6 * i + 3 + j, sibling))
                if "others" in made:
                    made["others"].append(_remote(there, there, send_sems, recv_sems, 6 * i + 3 + j, sibling))
        return [made[k] for k in kinds]

    def start(ins, outs, sems):
        own, sends = plan(ins, outs, sems, ("own", "sends"))
        for cp in own + sends:
            cp.start()

    def middle(ins, outs, sems):
        landed, passes = plan(ins, outs, sems, ("landed", "passes"))
        for arrived, cp in zip(landed, passes):
            arrived.wait_recv()
            cp.start()

    def finish(ins, outs, sems):
        own, sends, passes, others = plan(ins, outs, sems, ("own", "sends", "passes", "others"))
        for arrived in others:
            arrived.wait_recv()
        for cp in sends + passes:
            cp.wait_send()
        for cp in own:
            cp.wait()

    return _Rider(shards, [jax.ShapeDtypeStruct((N_CHIPS,) + s.shape, s.dtype) for s in shards],
                  [pltpu.SemaphoreType.DMA((6 * n,)), pltpu.SemaphoreType.DMA((6 * n,)),
                   pltpu.SemaphoreType.DMA((n,)), pltpu.SemaphoreType.DMA((n,))], start, middle, finish)


def _swap_rider(arrays, out_shape, source):
    n = len(arrays)

    def plan(ins, outs, sems):
        send_sems, recv_sems = sems
        x, y, c, _, _ = _place()
        return [_remote(source(ins[i], c, i), outs[i], send_sems, recv_sems, i, (x, y, 1 - c)) for i in range(n)]

    def start(ins, outs, sems):
        for cp in plan(ins, outs, sems):
            cp.start()

    def finish(ins, outs, sems):
        for cp in plan(ins, outs, sems):
            cp.wait()

    return _Rider(arrays, out_shape, [pltpu.SemaphoreType.DMA((n,)), pltpu.SemaphoreType.DMA((n,))], start, None, finish)


def _sibling_rider(grads):
    half = [g.shape[1] // 2 for g in grads]
    return _swap_rider(grads, [jax.ShapeDtypeStruct((g.shape[0], hr, g.shape[2]), g.dtype) for g, hr in zip(grads, half)],
                       lambda ref, c, i: ref.at[:, pl.ds((1 - c) * half[i], half[i])])


def _share_rider(halves):
    return _swap_rider(halves, [jax.ShapeDtypeStruct(h.shape, h.dtype) for h in halves], lambda ref, c, i: ref)


def _chips_rider(parts):
    n = len(parts)

    def plan(ins, outs, sems):
        send_sems, recv_sems = sems
        _, _, c, _, chips = _place()
        return [_remote(ins[i].at[2 * cx + cy], outs[i].at[j], send_sems, recv_sems, 3 * i + j, (cx, cy, c))
                for i in range(n) for j, (cx, cy) in enumerate(chips)]

    def start(ins, outs, sems):
        for cp in plan(ins, outs, sems):
            cp.start()

    def finish(ins, outs, sems):
        for cp in plan(ins, outs, sems):
            cp.wait()

    return _Rider(parts, [jax.ShapeDtypeStruct((N_CHIPS - 1,) + p.shape[1:], p.dtype) for p in parts],
                  [pltpu.SemaphoreType.DMA((3 * n,)), pltpu.SemaphoreType.DMA((3 * n,))], start, None, finish)


class _Reduce:
    def __init__(self, names, grads, cidx, chip, *, tag):
        self.names, self.grads, self.cidx, self.chip, self.tag = names, grads, cidx, chip, tag
        self.sibling = _sibling_rider(grads)

    def _by_shape(self, fn, *lists):
        done, i = [], 0
        while i < len(self.names):
            j = i + 1
            while j < len(self.names) and self.grads[j].shape == self.grads[i].shape:
                j += 1
            done += fn(*[lst[i:j] for lst in lists], self.names[i])
            i = j
        return done

    def partial(self, from_sibling):
        self.from_sibling = from_sibling
        self.chips = _chips_rider(self._by_shape(
            lambda g, r, k: _add_sibling(g, r, self.cidx, name="add_sibling_" + k), self.grads, from_sibling))
        return self

    def halves(self, from_chips):
        self.mine = self._by_shape(
            lambda g, r1, r2, k: _add_chips(g, r1, r2, self.cidx, self.chip, name="add_chips_" + k),
            self.grads, self.from_sibling, from_chips)
        self.share = _share_rider(self.mine)
        return self

    def result(self, others):
        return dict(zip(self.names, zip(self.mine, others)))

    def partial_now(self):
        return self.partial(_run_rider(self.sibling, name="rs_sibling_" + self.tag))

    def halves_now(self):
        return self.halves(_run_rider(self.chips, name="rs_chips_" + self.tag))

    def result_now(self):
        return self.result(_run_rider(self.share, name="rs_share_" + self.tag))


N_DEV = 8


def _small_rider(vec):
    def plan(ins, outs, sems):
        send_sems, recv_sems = sems
        x, y, c, _, _ = _place()
        return [_remote(ins[0], outs[0].at[k - 1], send_sems, recv_sems, k - 1, (x ^ (k >> 2), y ^ ((k >> 1) & 1), c ^ (k & 1)))
                for k in range(1, N_DEV)]

    def start(ins, outs, sems):
        for cp in plan(ins, outs, sems):
            cp.start()

    def finish(ins, outs, sems):
        for cp in plan(ins, outs, sems):
            cp.wait()

    return _Rider([vec], [jax.ShapeDtypeStruct((N_DEV - 1,) + vec.shape, vec.dtype)],
                  [pltpu.SemaphoreType.DMA((N_DEV - 1,)), pltpu.SemaphoreType.DMA((N_DEV - 1,))], start, None, finish)


def _small_sum(vec, others, me):
    def body(me_ref, v_ref, b_ref, o_ref):
        mine = me_ref[0]
        acc = None
        for d in range(N_DEV):
            term = jnp.where(mine == d, v_ref[...], b_ref[jnp.maximum((mine ^ d) - 1, 0)])
            acc = term if acc is None else acc + term
        o_ref[...] = acc

    vmem = pl.BlockSpec(memory_space=pltpu.VMEM)
    return pl.pallas_call(
        body, name="small_sum", in_specs=[pl.BlockSpec(memory_space=pltpu.SMEM), vmem, vmem], out_specs=vmem,
        out_shape=jax.ShapeDtypeStruct(vec.shape, vec.dtype),
    )(me, vec, others)


ELEMWISE_BLOCK = 512 * 1024


def _row_tile(rows, cols):
    best = None
    for t in range(16, rows + 1, 16):
        if rows % t == 0 and t * cols <= ELEMWISE_BLOCK:
            best = t
    return best if best is not None else rows


def _add_sibling(gs, r1s, cidx, *, name):
    n = len(gs)
    S, R, C = gs[0].shape
    hr = R // 2
    tr = _row_tile(hr, C)
    nt = hr // tr

    def body(c_ref, *refs):
        for g_ref, r_ref, o_ref in zip(refs[:n], refs[n:2 * n], refs[2 * n:]):
            o_ref[...] = (g_ref[...] + r_ref[...]).astype(BF16)

    blk = pl.BlockSpec((1, tr, C), lambda s, t, c: (s, t, 0))
    mine = pl.BlockSpec((1, tr, C), lambda s, t, c: (s, c[0] * nt + t, 0))
    return list(pl.pallas_call(
        body, name=name,
        grid_spec=pltpu.PrefetchScalarGridSpec(
            num_scalar_prefetch=1, grid=(S, nt), in_specs=[mine] * n + [blk] * n, out_specs=[blk] * n),
        out_shape=[jax.ShapeDtypeStruct((S, hr, C), BF16)] * n,
        compiler_params=_params(2),
    )(cidx, *gs, *r1s))


def _add_chips(gs, r1s, r2s, cidx, chip, *, name):
    n = len(gs)
    _, R, C = gs[0].shape
    hr = R // 2
    tr = _row_tile(hr, C)
    nt = hr // tr

    def body(pos_ref, *refs):
        for g_ref, r1_ref, r2_ref, o_ref in zip(refs[:n], refs[n:2 * n], refs[2 * n:3 * n], refs[3 * n:]):
            own = g_ref[0] + r1_ref[0]
            o_ref[...] = ((own + r2_ref[0].astype(F32)) + r2_ref[1].astype(F32)) + r2_ref[2].astype(F32)

    pos = jnp.concatenate([cidx, chip])
    return list(pl.pallas_call(
        body, name=name,
        grid_spec=pltpu.PrefetchScalarGridSpec(
            num_scalar_prefetch=1, grid=(nt,),
            in_specs=[pl.BlockSpec((1, tr, C), lambda t, pos: (pos[1], pos[0] * nt + t, 0))] * n
            + [pl.BlockSpec((1, tr, C), lambda t, pos: (pos[1], t, 0))] * n
            + [pl.BlockSpec((N_CHIPS - 1, tr, C), lambda t, pos: (0, t, 0))] * n,
            out_specs=[pl.BlockSpec((tr, C), lambda t, pos: (t, 0))] * n),
        out_shape=[jax.ShapeDtypeStruct((hr, C), F32)] * n,
        compiler_params=_params(1),
    )(pos, *gs, *r1s, *r2s))


def _adamw_math(w, g, m, v):
    mn = ADAM_B1 * m + (1.0 - ADAM_B1) * g
    vn = ADAM_B2 * v + (1.0 - ADAM_B2) * (g * g)
    m_hat = mn / (1.0 - ADAM_B1 ** ADAM_STEP)
    v_hat = vn / (1.0 - ADAM_B2 ** ADAM_STEP)
    return -ADAM_LR * (m_hat / (jnp.sqrt(v_hat) + ADAM_EPS) + ADAM_WD * w), mn, vn


def _adamw_halves(ws, mines, others, ms, vs, cidx, *, name):
    n = len(ws)
    R, C = ws[0].shape
    hr = R // 2
    tr = _row_tile(hr, C * min(n, 2))
    nt = hr // tr

    def body(c_ref, *refs):
        for i in range(n):
            w_ref, a_ref, b_ref, m_ref, v_ref = (refs[j * n + i] for j in range(5))
            g_ref, d_ref, mo_ref, vo_ref = (refs[(5 + j) * n + i] for j in range(4))
            gv = jnp.where(pl.program_id(0) == c_ref[0], a_ref[...], b_ref[...])
            g_ref[...] = gv
            d_ref[...], mo_ref[...], vo_ref[...] = _adamw_math(w_ref[...], gv, m_ref[...], v_ref[...])

    full = pl.BlockSpec((tr, C), lambda h, t, c: (h * nt + t, 0))
    own = pl.BlockSpec((tr, C), lambda h, t, c: (jnp.where(h == c[0], t, 0), 0))
    sib = pl.BlockSpec((tr, C), lambda h, t, c: (jnp.where(h == c[0], 0, t), 0))
    shape = jax.ShapeDtypeStruct((R, C), F32)
    outs = pl.pallas_call(
        body, name=name,
        grid_spec=pltpu.PrefetchScalarGridSpec(
            num_scalar_prefetch=1, grid=(2, nt),
            in_specs=[full] * n + [own] * n + [sib] * n + [full] * (2 * n), out_specs=[full] * (4 * n)),
        out_shape=[shape] * (4 * n),
        compiler_params=_params(2),
    )(cidx, *ws, *mines, *others, *ms, *vs)
    return [list(outs[j * n:(j + 1) * n]) for j in range(4)]


def _adamw_small(ws, gs, ms, vs):
    n = len(ws)

    def body(*refs):
        for i in range(n):
            w_ref, g_ref, m_ref, v_ref = (refs[j * n + i] for j in range(4))
            d_ref, mo_ref, vo_ref = (refs[(4 + j) * n + i] for j in range(3))
            d_ref[...], mo_ref[...], vo_ref[...] = _adamw_math(w_ref[...], g_ref[...], m_ref[...], v_ref[...])

    shapes = [jax.ShapeDtypeStruct(a.shape, F32) for a in ws]
    outs = pl.pallas_call(body, name="adamw_small", out_shape=shapes * 3, compiler_params=_params(0))(*ws, *gs, *ms, *vs)
    return outs[:n], outs[n:2 * n], outs[2 * n:]


def _unstack_cols(w):
    s, r, c = w.shape
    return w.transpose(1, 0, 2).reshape(r, s * c)


def _pad_rows(a, rows):
    return jnp.pad(a, ((0, rows - a.shape[0]), (0, LANES - a.shape[1])))


BIG = ("ffn1_w_gate", "ffn1_w_up", "ffn1_w_down", "w_in", "w_branch_a", "w_branch_b", "w_out",
       "ffn2_w_gate", "ffn2_w_up", "ffn2_w_down")
TRANSPOSED = ("ffn1_w_gate", "ffn1_w_up", "w_in", "ffn2_w_gate", "ffn2_w_up")
WEIGHTS = ("ffn1_norm", "ffn1_w_gate", "ffn1_w_up", "ffn1_w_down", "mix_norm", "w_in", "na_rpb", "sink_logit",
           "w_branch_a", "w_branch_b", "w_out", "ffn2_norm", "ffn2_w_gate", "ffn2_w_up", "ffn2_w_down", "final_norm")


def kernel(x, ffn1_norm, ffn1_w_gate, ffn1_w_up, ffn1_w_down, mix_norm, w_in, na_rpb, sink_logit, w_branch_a, w_branch_b, w_out, ffn2_norm, ffn2_w_gate, ffn2_w_up, ffn2_w_down, final_norm, loss_target, m_ffn1_norm, m_ffn1_w_gate, m_ffn1_w_up, m_ffn1_w_down, m_mix_norm, m_w_in, m_na_rpb, m_sink_logit, m_w_branch_a, m_w_branch_b, m_w_out, m_ffn2_norm, m_ffn2_w_gate, m_ffn2_w_up, m_ffn2_w_down, m_final_norm, v_ffn1_norm, v_ffn1_w_gate, v_ffn1_w_up, v_ffn1_w_down, v_mix_norm, v_w_in, v_na_rpb, v_sink_logit, v_w_branch_a, v_w_branch_b, v_w_out, v_ffn2_norm, v_ffn2_w_gate, v_ffn2_w_up, v_ffn2_w_down, v_final_norm):
    args = dict(locals())
    w = {k: args[k] for k in WEIGHTS}
    mom = {k: args["m_" + k] for k in WEIGHTS}
    var = {k: args["v_" + k] for k in WEIGHTS}
    cidx = lax.axis_index("c").astype(jnp.int32).reshape(1)
    chip = (2 * lax.axis_index("x") + lax.axis_index("y")).astype(jnp.int32).reshape(1)

    def shard(a, k):
        return jnp.swapaxes(a[0], 0, 1) if k in TRANSPOSED else a[0]

    def unshard(a, k):
        return (jnp.swapaxes(a, 0, 1) if k in TRANSPOSED else a)[None]

    def bf16_shards(names):
        return [shard(w[k], k).astype(BF16) for k in names]

    class comm:
        late_rider = _gather_rider(bf16_shards(MIXER + FFN2))

        @staticmethod
        def late(gathered):
            full = dict(zip(MIXER + FFN2, gathered))
            return (full["w_in"].reshape(D_IN, D_MODEL), _unstack_cols(full["w_branch_a"]), _unstack_cols(full["w_branch_b"]),
                    full["w_out"].reshape(D_MODEL, D_MODEL), tuple(full[k] for k in FFN2))

        @staticmethod
        def reduce(names, grads, *, tag):
            return _Reduce(names, grads, cidx, chip, tag=tag)

        @staticmethod
        def update(names, reduced):
            res = _adamw_halves([shard(w[k], k) for k in names], [reduced[k][0] for k in names],
                                [reduced[k][1] for k in names], [shard(mom[k], k) for k in names],
                                [shard(var[k], k) for k in names], cidx, name="adamw_" + names[0])
            for i, k in enumerate(names):
                grads_out[k], deltas[k], new_m[k], new_v[k] = (unshard(a[i], k) for a in res)

        @staticmethod
        def small(g):
            packed["mine"] = jnp.concatenate([
                g["ffn1_norm"].reshape(rows, LANES), g["mix_norm"].reshape(rows, LANES), g["ffn2_norm"].reshape(rows, LANES),
                g["final_norm"].reshape(rows, LANES), g["na_rpb"].reshape(-1, LANES),
                _pad_rows(g["sink_logit"].reshape(1, NB_HEADS), 8), _pad_rows(g["loss"], 8)], axis=0)
            return _small_rider(packed["mine"])

        @staticmethod
        def small_done(got):
            packed["others"], = got

    deltas, new_m, new_v, grads_out, grad, packed = {}, {}, {}, {}, {}, {}
    rows = D_MODEL // LANES
    f1 = _run_rider(_gather_rider(bf16_shards(FFN1)), name="all_gather_ffn1")
    out = _layer_grads(x[0], loss_target[0], ffn1_norm, f1, mix_norm, None, na_rpb[0], sink_logit[0], ffn2_norm,
                       final_norm.reshape(1, D_MODEL), comm=comm)

    me = (4 * lax.axis_index("x") + 2 * lax.axis_index("y") + lax.axis_index("c")).astype(jnp.int32).reshape(1)
    total = _small_sum(packed["mine"], packed["others"], me)
    n_rpb = NA_HEADS * 2 * NA_KH
    grad["ffn1_norm"] = total[0:rows].reshape(1, D_MODEL)
    grad["mix_norm"] = total[rows:2 * rows].reshape(1, D_MODEL)
    grad["ffn2_norm"] = total[2 * rows:3 * rows].reshape(1, D_MODEL)
    grad["final_norm"] = total[3 * rows:4 * rows].reshape(1, D_MODEL)
    grad["na_rpb"] = total[4 * rows:4 * rows + n_rpb].reshape(NA_HEADS, 2 * NA_KH, LANES)[:, :2 * NA_KH - 1, :2 * NA_KW - 1]
    grad["na_rpb"] = grad["na_rpb"].reshape(NA_HEADS, -1)
    grad["sink_logit"] = total[4 * rows + n_rpb:4 * rows + n_rpb + 1, 0:NB_HEADS]
    loss = total[4 * rows + n_rpb + 8, 0]

    small_names = [k for k in WEIGHTS if k not in BIG]
    res = _adamw_small(*[[a[k].reshape(grad[k].shape) for k in small_names] for a in (w, grad, mom, var)])
    for i, k in enumerate(small_names):
        grads_out[k], deltas[k], new_m[k], new_v[k] = (a.reshape(w[k].shape) for a in (grad[k], res[0][i], res[1][i], res[2][i]))
    return (loss, out["dx"].reshape(x.shape), *[grads_out[k] for k in WEIGHTS], *[deltas[k] for k in WEIGHTS],
            *[new_m[k] for k in WEIGHTS], *[new_v[k] for k in WEIGHTS])
```

```python
import math

import jax
import jax.numpy as jnp
import numpy as np
from jax import lax
from jax.experimental import pallas as pl
from jax.experimental.pallas import tpu as pltpu

F32 = jnp.float32
BF16 = jnp.bfloat16

D_MODEL = 1024
HEAD_DIM = 64
NA_HEADS = 8
NB_HEADS = 8
GRID_W = 64
NA_KH = 8
NA_KW = 16
WIN = 128
ROPE_THETA = 10000.0
EPS = 1e-6
N_CHIPS = 4
QK_SCALE = HEAD_DIM ** -0.5
NEG = -1e30
LANES = 128
VMEM_LIMIT = 56 * 1024 * 1024
HEAD_ROWS = 256
WGRAD_TOKENS_BYTES = 8192

C_QKVA = 3 * NA_HEADS * HEAD_DIM
C_QB = NB_HEADS * HEAD_DIM
C_KB = 2 * HEAD_DIM
C_ROPE = C_QB + C_KB
C_GATES = 2 * D_MODEL
D_IN = C_QKVA + C_QB + 2 * C_KB + C_GATES
O_QB = C_QKVA
O_KB = O_QB + C_QB
O_VB = O_KB + C_KB
O_G = O_VB + C_KB

ADAM_LR = 0.001
ADAM_B1 = 0.9
ADAM_B2 = 0.999
ADAM_EPS = 1e-08
ADAM_WD = 0.01
ADAM_STEP = 10

MESH = pl.DeviceIdType.MESH


def _dot(a, b):
    return jnp.dot(a, b, preferred_element_type=F32)


def _dot_nt(a, b):
    return lax.dot_general(a, b, (((1,), (1,)), ((), ())), preferred_element_type=F32)


def _dot_tn(a, b):
    return lax.dot_general(a, b, (((0,), (0,)), ((), ())), preferred_element_type=F32)


def _params(n_axes):
    return pltpu.CompilerParams(dimension_semantics=("arbitrary",) * n_axes, vmem_limit_bytes=VMEM_LIMIT)


def _rstd(xf):
    return lax.rsqrt(jnp.mean(xf * xf, axis=-1, keepdims=True) + EPS)


def _norm_bwd(dn, xf, g, r):
    xhat = xf * r
    dxh = dn * g
    dx = r * (dxh - xhat * jnp.mean(dxh * xhat, axis=-1, keepdims=True))
    return dx, dn * xhat


def _sigmoid(x):
    return 0.5 * jnp.tanh(0.5 * x) + 0.5


def _loss_head(hf, gv, tgt):
    r = _rstd(hf)
    err = (hf * r) * gv - tgt
    dx, dgr = _norm_bwd(err * (1.0 / hf.shape[-1]), hf, gv, r)
    return 0.5 * jnp.mean(err * err, axis=-1, keepdims=True), dx, dgr


def _ffn_fwd(x, g, wg, wu, wd, *, name, tm=1024, sub=512, rider=None, head=None):
    T, D = x.shape
    F = wg.shape[1]
    tm = min(tm, T)
    sub = min(sub, tm)
    n_head = 0 if head is None else 2

    def body(*refs):
        x_ref, g_ref, wg_ref, wu_ref, wd_ref = refs[:5]
        h_ref, n_ref, hdn_ref, p_ref, q_ref = refs[5 + n_head:10 + n_head]
        i, s = pl.program_id(0), pl.program_id(1)
        _ffn_fwd_step(x_ref, g_ref, wg_ref, wu_ref, wd_ref, h_ref, n_ref, hdn_ref, p_ref, q_ref, s)
        if head is not None:
            gf_ref, t_ref = refs[5:7]
            loss_ref, dgf_ref = refs[10 + n_head:]

            @pl.when((i == 0) & (s == 0))
            def _():
                loss_ref[...] = jnp.zeros_like(loss_ref)
                dgf_ref[...] = jnp.zeros_like(dgf_ref)

            @pl.when(s == N_CHIPS - 1)
            def _():
                for u in range(tm // HEAD_ROWS):
                    r = pl.ds(u * HEAD_ROWS, HEAD_ROWS)
                    terms, dh, dgr = _loss_head(h_ref[r, :], gf_ref[...], t_ref[r, :])
                    loss_ref[...] += jnp.broadcast_to(jnp.sum(terms), loss_ref.shape)
                    dgf_ref[...] += jnp.sum(dgr, axis=0, keepdims=True)
                    h_ref[r, :] = dh

    def _ffn_fwd_step(x_ref, g_ref, wg_ref, wu_ref, wd_ref, h_ref, n_ref, hdn_ref, p_ref, q_ref, s):

        @pl.when(s == 0)
        def _():
            xf = x_ref[...]
            n_ref[...] = ((xf * _rstd(xf)) * g_ref[...]).astype(BF16)
            h_ref[...] = xf

        rows = [pl.ds(u * sub, sub) for u in range(tm // sub)]
        ab = [(_dot_nt(n_ref[r, :], wg_ref[0]), _dot_nt(n_ref[r, :], wu_ref[0])) for r in rows]
        hdns = []
        for r, (a, b) in zip(rows, ab):
            sg = _sigmoid(a)
            silu = a * sg
            hdn = (silu * b).astype(BF16)
            hdn_ref[0, r, :] = hdn
            p_ref[0, r, :] = (b * (sg + silu * (1.0 - sg))).astype(BF16)
            q_ref[0, r, :] = silu.astype(BF16)
            hdns.append(hdn)
        for r, hdn in zip(rows, hdns):
            h_ref[r, :] += 0.5 * _dot(hdn, wd_ref[0])

    tok = pl.BlockSpec((tm, D), lambda i, s: (i, 0))
    hid = pl.BlockSpec((1, tm, F), lambda i, s: (s, i, 0))
    wspec = pl.BlockSpec((1, F, D), lambda i, s: (s, 0, 0))
    hshape = jax.ShapeDtypeStruct((N_CHIPS, T, F), BF16)
    grid = (T // tm, N_CHIPS)
    vec = pl.BlockSpec((1, D), lambda i, s: (0, 0))
    head_in, head_in_specs, head_out, head_out_specs = [], [], [], []
    if head is not None:
        head_in, head_in_specs = list(head), [vec, tok]
        head_out = [jax.ShapeDtypeStruct((1, LANES), F32), jax.ShapeDtypeStruct((1, D), F32)]
        head_out_specs = [pl.BlockSpec((1, LANES), lambda i, s: (0, 0)), vec]
    n_main = 5 + n_head
    body, r_in, r_in_specs, r_out, r_out_specs, scratch = _ride(body, n_main, n_main, rider, grid, (grid[0] * grid[1] * 7) // 8)
    outs = pl.pallas_call(
        body, name=name, grid=grid,
        in_specs=[tok, vec, wspec, wspec, wspec] + head_in_specs + r_in_specs,
        out_specs=[tok, tok, hid, hid, hid] + head_out_specs + r_out_specs,
        out_shape=[jax.ShapeDtypeStruct((T, D), F32), jax.ShapeDtypeStruct((T, D), BF16), hshape, hshape, hshape]
        + head_out + r_out,
        scratch_shapes=scratch,
        compiler_params=_params(2),
    )(x, g, wg, wu, wd, *head_in, *r_in)
    return (*outs[:n_main], list(outs[n_main:]))


def _ffn_bwd(dh, x, g, p, q, wg, wu, wd, *, name, tm=1024, sub=256, rider=None):
    T, D = x.shape
    F = wg.shape[1]
    tm = min(tm, T)
    sub = min(sub, tm)

    def body(dh_ref, x_ref, g_ref, p_ref, q_ref, wg_ref, wu_ref, wd_ref, dx_ref, da_ref, db_ref, dg_ref):
        i, s = pl.program_id(0), pl.program_id(1)

        @pl.when((i == 0) & (s == 0))
        def _():
            dg_ref[...] = jnp.zeros_like(dg_ref)

        @pl.when(s == 0)
        def _():
            dx_ref[...] = jnp.zeros_like(dx_ref)

        rows = [pl.ds(u * sub, sub) for u in range(tm // sub)]
        dhdn = [_dot_nt((0.5 * dh_ref[r, :]).astype(BF16), wd_ref[0]) for r in rows]
        das, dbs = [], []
        for r, dd in zip(rows, dhdn):
            da = (dd * p_ref[0, r, :].astype(F32)).astype(BF16)
            db = (dd * q_ref[0, r, :].astype(F32)).astype(BF16)
            da_ref[0, r, :] = da
            db_ref[0, r, :] = db
            das.append(da)
            dbs.append(db)
        for r, da, db in zip(rows, das, dbs):
            dx_ref[r, :] += _dot(da, wg_ref[0]) + _dot(db, wu_ref[0])

        @pl.when(s == N_CHIPS - 1)
        def _():
            xf = x_ref[...]
            dx, dgr = _norm_bwd(dx_ref[...], xf, g_ref[...], _rstd(xf))
            dg_ref[...] += jnp.sum(dgr, axis=0, keepdims=True)
            dx_ref[...] = dh_ref[...] + dx

    tok = pl.BlockSpec((tm, D), lambda i, s: (i, 0))
    hid = pl.BlockSpec((1, tm, F), lambda i, s: (s, i, 0))
    vec = pl.BlockSpec((1, D), lambda i, s: (0, 0))
    hshape = jax.ShapeDtypeStruct((N_CHIPS, T, F), BF16)
    wspec = pl.BlockSpec((1, F, D), lambda i, s: (s, 0, 0))
    grid = (T // tm, N_CHIPS)
    body, r_in, r_in_specs, r_out, r_out_specs, scratch = _ride(body, 8, 4, rider, grid, None)
    outs = pl.pallas_call(
        body, name=name, grid=grid,
        in_specs=[tok, tok, vec, hid, hid, wspec, wspec, wspec] + r_in_specs,
        out_specs=[tok, hid, hid, vec] + r_out_specs,
        out_shape=[jax.ShapeDtypeStruct((T, D), F32), hshape, hshape, jax.ShapeDtypeStruct((1, D), F32)] + r_out,
        scratch_shapes=scratch,
        compiler_params=_params(2),
    )(dh, x, g, p, q, wg, wu, wd, *r_in)
    return (*outs[:4], list(outs[4:]))


def _wgrad(a, b, *, a_block, a_map, b_block, b_map, out_shape, o_block, o_map, grid, scale=1.0, name, rider=None):
    def body(a_ref, b_ref, o_ref):
        @pl.when(pl.program_id(len(grid) - 1) == 0)
        def _():
            o_ref[...] = jnp.zeros_like(o_ref)

        av = a_ref[...]
        bv = b_ref[...]
        av = av.reshape(av.shape[-2:]).astype(BF16)
        bv = bv.reshape(bv.shape[-2:])
        if scale != 1.0:
            bv = scale * bv
        o_ref[...] += _dot_tn(av, bv.astype(BF16)).reshape(o_ref.shape)

    body, r_in, r_in_specs, r_out, r_out_specs, scratch = _ride(body, 2, 1, rider, grid, None)
    outs = pl.pallas_call(
        body, name=name, grid=grid,
        in_specs=[pl.BlockSpec(a_block, a_map), pl.BlockSpec(b_block, b_map)] + r_in_specs,
        out_specs=[pl.BlockSpec(o_block, o_map)] + r_out_specs,
        out_shape=[jax.ShapeDtypeStruct(out_shape, F32)] + r_out,
        scratch_shapes=scratch,
        compiler_params=_params(len(grid)),
    )(a, b, *r_in)
    return outs[0], list(outs[1:])


def _wgrad_rows(a, b, n_blocks, *, name, tk=2048):
    T, N = b.shape
    M = a.shape[1] // n_blocks
    tk = min(tk, T)
    return _wgrad(a, b, a_block=(tk, M), a_map=lambda s, k: (k, s), b_block=(tk, N), b_map=lambda s, k: (k, 0),
                  out_shape=(n_blocks, M, N), o_block=(1, M, N), o_map=lambda s, k: (s, 0, 0), grid=(n_blocks, T // tk), name=name)


def _wgrad_shard_a(a, b, *, name, scale=1.0, rider=None):
    S, T, M = a.shape
    N = b.shape[1]
    tk = min(WGRAD_TOKENS_BYTES // b.dtype.itemsize, T)
    return _wgrad(a, b, a_block=(1, tk, M), a_map=lambda s, k: (s, k, 0), b_block=(tk, N), b_map=lambda s, k: (k, 0),
                  out_shape=(S, M, N), o_block=(1, M, N), o_map=lambda s, k: (s, 0, 0), grid=(S, T // tk), scale=scale,
                  name=name, rider=rider)


def _wgrad_cols(a, b, n_blocks, *, name, tk=2048):
    T, M = a.shape
    N = b.shape[1] // n_blocks
    tk = min(tk, T)

    def body(a_ref, b_ref, o_ref):
        @pl.when(pl.program_id(0) == 0)
        def _():
            o_ref[...] = jnp.zeros_like(o_ref)

        r = _dot_tn(a_ref[...].astype(BF16), b_ref[...].astype(BF16))
        for s in range(n_blocks):
            o_ref[s] += r[:, s * N:(s + 1) * N]

    return pl.pallas_call(
        body, name=name, grid=(T // tk,),
        in_specs=[pl.BlockSpec((tk, M), lambda k: (k, 0)), pl.BlockSpec((tk, n_blocks * N), lambda k: (k, 0))],
        out_specs=pl.BlockSpec((n_blocks, M, N), lambda k: (0, 0, 0)),
        out_shape=jax.ShapeDtypeStruct((n_blocks, M, N), F32),
        compiler_params=_params(1),
    )(a, b)


def _rope_tables(T):
    half = HEAD_DIM // 2
    inv = np.float32(ROPE_THETA) ** (-np.arange(half, dtype=np.float32) / np.float32(half))
    ang = np.arange(T, dtype=np.float32)[:, None] * inv[None, :]
    cos, sin, zero = np.cos(ang), np.sin(ang), np.zeros_like(ang)
    reps = LANES // HEAD_DIM
    return (jnp.asarray(np.tile(np.concatenate([cos, cos], axis=1), (1, reps))),
            jnp.asarray(np.tile(np.concatenate([-sin, zero], axis=1), (1, reps))),
            jnp.asarray(np.tile(np.concatenate([zero, sin], axis=1), (1, reps))))


def _rope(x, cos, sa, sb, sign):
    half = HEAD_DIM // 2
    return x * cos + sign * (pltpu.roll(x, LANES - half, 1) * sa + pltpu.roll(x, half, 1) * sb)


def _mix_in_fwd(h, g, w_in, tables, *, tm=512):
    T, D = h.shape

    def body(h_ref, g_ref, w_ref, cos_ref, sa_ref, sb_ref, u_ref, qkva_ref, qb_ref, kvb_ref, gates_ref):
        hf = h_ref[...]
        u = ((hf * _rstd(hf)) * g_ref[...]).astype(BF16)
        u_ref[...] = u
        qkva_ref[...] = _dot_nt(u, w_ref[0:C_QKVA, :]).astype(BF16)
        zr = _dot_nt(u, w_ref[O_QB:O_QB + C_ROPE, :])
        cos, sa, sb = cos_ref[...], sa_ref[...], sb_ref[...]
        for j in range(C_ROPE // LANES):
            rj = _rope(zr[:, j * LANES:(j + 1) * LANES], cos, sa, sb, 1.0).astype(BF16)
            if j < C_QB // LANES:
                qb_ref[:, j * LANES:(j + 1) * LANES] = rj
            else:
                kvb_ref[:, 0:C_KB] = rj
        kvb_ref[:, C_KB:2 * C_KB] = _dot_nt(u, w_ref[O_VB:O_VB + C_KB, :]).astype(BF16)
        gates_ref[...] = _dot_nt(u, w_ref[O_G:O_G + C_GATES, :])

    def tok(n):
        return pl.BlockSpec((tm, n), lambda i: (i, 0))

    return pl.pallas_call(
        body, name="mix_in_fwd", grid=(T // tm,),
        in_specs=[tok(D), pl.BlockSpec((1, D), lambda i: (0, 0)), pl.BlockSpec((D_IN, D), lambda i: (0, 0), pipeline_mode=pl.Buffered(1)),
                  tok(LANES), tok(LANES), tok(LANES)],
        out_specs=[tok(D), tok(C_QKVA), tok(C_QB), tok(2 * C_KB), tok(C_GATES)],
        out_shape=[jax.ShapeDtypeStruct((T, D), BF16), jax.ShapeDtypeStruct((T, C_QKVA), BF16),
                   jax.ShapeDtypeStruct((T, C_QB), BF16), jax.ShapeDtypeStruct((T, 2 * C_KB), BF16),
                   jax.ShapeDtypeStruct((T, C_GATES), F32)],
        compiler_params=_params(1),
    )(h, g, w_in, *tables)


def _mix_in_bwd(dqa, dka, dva, dqb, dkb, dvb, dgates, h, g, dres, w_in, tables, *, tm=512, rider=None):
    T, D = h.shape

    def body(dqa_ref, dka_ref, dva_ref, dqb_ref, dkb_ref, dvb_ref, dgt_ref, h_ref, g_ref, dres_ref, w_ref,
             cos_ref, sa_ref, sb_ref, dz_ref, dh_ref, dg_ref):
        @pl.when(pl.program_id(0) == 0)
        def _():
            dg_ref[...] = jnp.zeros_like(dg_ref)

        na = NA_HEADS * HEAD_DIM
        dz_ref[:, 0:na] = dqa_ref[...].astype(BF16)
        dz_ref[:, na:2 * na] = dka_ref[...].astype(BF16)
        dz_ref[:, 2 * na:3 * na] = dva_ref[...].astype(BF16)
        cos, sa, sb = cos_ref[...], sa_ref[...], sb_ref[...]
        for j in range(C_QB // LANES):
            dz_ref[:, O_QB + j * LANES:O_QB + (j + 1) * LANES] = _rope(
                dqb_ref[:, j * LANES:(j + 1) * LANES], cos, sa, sb, -1.0).astype(BF16)
        dz_ref[:, O_KB:O_KB + C_KB] = _rope(dkb_ref[...], cos, sa, sb, -1.0).astype(BF16)
        dz_ref[:, O_VB:O_VB + C_KB] = dvb_ref[...].astype(BF16)
        dz_ref[:, O_G:O_G + C_GATES] = dgt_ref[...].astype(BF16)
        du = _dot(dz_ref[...], w_ref[...])
        hf = h_ref[...]
        dx, dgr = _norm_bwd(du, hf, g_ref[...], _rstd(hf))
        dg_ref[...] += jnp.sum(dgr, axis=0, keepdims=True)
        dh_ref[...] = dres_ref[...] + dx

    def tok(n):
        return pl.BlockSpec((tm, n), lambda i: (i, 0))

    vec = pl.BlockSpec((1, D), lambda i: (0, 0))
    na = NA_HEADS * HEAD_DIM
    grid = (T // tm,)
    body, r_in, r_in_specs, r_out, r_out_specs, scratch = _ride(body, 14, 3, rider, grid, None)
    outs = pl.pallas_call(
        body, name="mix_in_bwd", grid=grid,
        in_specs=[tok(na), tok(na), tok(na), tok(C_QB), tok(C_KB), tok(C_KB), tok(C_GATES), tok(D), vec, tok(D),
                  pl.BlockSpec((D_IN, D), lambda i: (0, 0), pipeline_mode=pl.Buffered(1)), tok(LANES), tok(LANES), tok(LANES)]
        + r_in_specs,
        out_specs=[tok(D_IN), tok(D), vec] + r_out_specs,
        out_shape=[jax.ShapeDtypeStruct((T, D_IN), BF16), jax.ShapeDtypeStruct((T, D), F32),
                   jax.ShapeDtypeStruct((1, D), F32)] + r_out,
        scratch_shapes=scratch,
        compiler_params=_params(1),
    )(dqa, dka, dva, dqb, dkb, dvb, dgates, h, g, dres, w_in, *tables, *r_in)
    return (*outs[:3], list(outs[3:]))


def _na_bias_slabs(rpb):
    H = rpb.shape[0]
    ncell = GRID_W * GRID_W
    cell = np.arange(ncell)
    co = cell % GRID_W - cell // GRID_W + (NA_KW - 1)
    e_co = jnp.asarray((np.arange(LANES)[:, None] == co[None, :]).astype(np.float32))
    table = jnp.pad(rpb, ((0, 0), (0, 1), (0, LANES - rpb.shape[2]))).reshape(H * 2 * NA_KH, LANES)

    def body(t_ref, e_ref, o_ref):
        o_ref[...] = jnp.dot(t_ref[...], e_ref[...], preferred_element_type=F32, precision=lax.Precision.HIGHEST)

    toeplitz = pl.pallas_call(
        body, name="rpb_unfold", out_shape=jax.ShapeDtypeStruct((H * 2 * NA_KH, ncell), F32),
        compiler_params=_params(0),
    )(table, e_co).reshape(H, 2 * NA_KH, GRID_W, GRID_W)

    def assemble(tz_ref, o_ref):
        c = lax.broadcasted_iota(jnp.int32, (GRID_W, GRID_W), 0)
        k = lax.broadcasted_iota(jnp.int32, (GRID_W, GRID_W), 1)
        cs = jnp.clip(c - NA_KW // 2, 0, GRID_W - NA_KW)
        inwin = (k >= cs) & (k < cs + NA_KW)
        for ro0 in range(NA_KH):
            for hh in range(2):
                for i in range(NA_KH):
                    o_ref[0, ro0, hh * GRID_W:(hh + 1) * GRID_W, i * GRID_W:(i + 1) * GRID_W] = jnp.where(
                        inwin, tz_ref[hh, ro0 + i], NEG)

    return pl.pallas_call(
        assemble, name="na_bias_slabs", grid=(H // 2,),
        in_specs=[pl.BlockSpec((2, 2 * NA_KH, GRID_W, GRID_W), lambda p: (p, 0, 0, 0))],
        out_specs=pl.BlockSpec((1, NA_KH, 2 * GRID_W, NA_KH * GRID_W), lambda p: (p, 0, 0, 0)),
        out_shape=jax.ShapeDtypeStruct((H // 2, NA_KH, 2 * GRID_W, NA_KH * GRID_W), F32),
        compiler_params=_params(1),
    )(toeplitz)


def _half_masks(rows):
    lane = lax.broadcasted_iota(jnp.int32, (rows, LANES), 1)
    left = lane < HEAD_DIM
    return left, (left, jnp.logical_not(left))


def _stack_heads(x):
    left, halves = _half_masks(x.shape[0])
    xf = x.astype(F32)
    return jnp.concatenate([jnp.where(m, xf, 0.0).astype(BF16) for m in halves], axis=0)


def _unstack_heads(o):
    rows = o.shape[0] // 2
    left, _ = _half_masks(rows)
    return jnp.where(left, o[:rows], o[rows:])


def _na_row(j, t, rb, rows):
    r = j * rb + t
    rs = jnp.clip(r - NA_KH // 2, 0, rows - NA_KH)
    return pl.multiple_of(t * GRID_W, GRID_W), pl.multiple_of(rs * GRID_W, GRID_W), rs - r + (NA_KH - 1)


def _na_specs(T, rb):
    qrows = GRID_W * rb
    pairs = NA_HEADS // 2
    return ([pl.BlockSpec((qrows, LANES), lambda p, j: (j, p)),
             pl.BlockSpec((T, LANES), lambda p, j: (0, pairs + p)),
             pl.BlockSpec((T, LANES), lambda p, j: (0, 2 * pairs + p))],
            pl.BlockSpec((1, NA_KH, 2 * GRID_W, NA_KH * GRID_W), lambda p, j: (p, 0, 0, 0)))


def _softmax(s):
    p = jnp.exp(s - jnp.max(s, axis=-1, keepdims=True))
    return p / jnp.sum(p, axis=-1, keepdims=True)


def _na_fwd(qkva, bias, *, rb=32, group=32):
    T = qkva.shape[0]
    rows = T // GRID_W
    nkeys = NA_KH * GRID_W
    rb = min(rb, rows)
    group = min(group, rb)

    def body(q_ref, k_ref, v_ref, bias_ref, y_ref):
        j = pl.program_id(1)

        def rows_step(t, carry):
            at = [_na_row(j, t * group + u, rb, rows) for u in range(group)]
            s = [_dot_nt(_stack_heads(q_ref[pl.ds(q0, GRID_W), :]), k_ref[pl.ds(k0, nkeys), :]) for q0, k0, _ in at]
            p = [_softmax(su * QK_SCALE + bias_ref[0, ro0]) for su, (_, _, ro0) in zip(s, at)]
            o = [_dot(pu.astype(BF16), v_ref[pl.ds(k0, nkeys), :]) for pu, (_, k0, _) in zip(p, at)]
            for ou, (q0, _, _) in zip(o, at):
                y_ref[pl.ds(q0, GRID_W), :] = _unstack_heads(ou).astype(BF16)
            return carry

        lax.fori_loop(0, rb // group, rows_step, 0)

    qkv_specs, bias_spec = _na_specs(T, rb)
    return pl.pallas_call(
        body, name="na_fwd", grid=(NA_HEADS // 2, rows // rb),
        in_specs=qkv_specs + [bias_spec],
        out_specs=qkv_specs[0],
        out_shape=jax.ShapeDtypeStruct((T, NA_HEADS * HEAD_DIM), BF16),
        compiler_params=_params(2),
    )(qkva, qkva, qkva, bias)


def _na_bwd(qkva, dy, bias, *, rb=16, group=16, rider=None):
    T = qkva.shape[0]
    rows = T // GRID_W
    nkeys = NA_KH * GRID_W
    rb = min(rb, rows)
    group = min(group, rb)

    def body(q_ref, k_ref, v_ref, dy_ref, bias_ref, dq_ref, dk_ref, dv_ref, dbias_ref):
        j = pl.program_id(1)

        @pl.when(j == 0)
        def _():
            dk_ref[...] = jnp.zeros_like(dk_ref)
            dv_ref[...] = jnp.zeros_like(dv_ref)
            dbias_ref[...] = jnp.zeros_like(dbias_ref)

        def rows_step(t, carry):
            at = [_na_row(j, t * group + u, rb, rows) for u in range(group)]
            qs = [_stack_heads(q_ref[pl.ds(q0, GRID_W), :]) for q0, _, _ in at]
            dys = [_stack_heads(dy_ref[pl.ds(q0, GRID_W), :]) for q0, _, _ in at]
            s = [_dot_nt(qu, k_ref[pl.ds(k0, nkeys), :]) for qu, (_, k0, _) in zip(qs, at)]
            dp = [_dot_nt(du, v_ref[pl.ds(k0, nkeys), :]) for du, (_, k0, _) in zip(dys, at)]
            p = [_softmax(su * QK_SCALE + bias_ref[0, ro0]) for su, (_, _, ro0) in zip(s, at)]
            ds = [pu * (du - jnp.sum(pu * du, axis=-1, keepdims=True)) for pu, du in zip(p, dp)]
            for u, (q0, k0, ro0) in enumerate(at):
                dbias_ref[0, ro0] += ds[u]
                dsb = ds[u].astype(BF16)
                dq_ref[pl.ds(q0, GRID_W), :] = (_unstack_heads(_dot(dsb, k_ref[pl.ds(k0, nkeys), :])) * QK_SCALE).astype(BF16)
                dk_ref[pl.ds(k0, nkeys), :] += _dot_tn(dsb, qs[u]) * QK_SCALE
                dv_ref[pl.ds(k0, nkeys), :] += _dot_tn(p[u].astype(BF16), dys[u])
            return carry

        lax.fori_loop(0, rb // group, rows_step, 0)

    qkv_specs, bias_spec = _na_specs(T, rb)
    width = NA_HEADS * HEAD_DIM
    kv_out = pl.BlockSpec((T, LANES), lambda p, j: (0, p))
    grid = (NA_HEADS // 2, rows // rb)
    body, r_in, r_in_specs, r_out, r_out_specs, scratch = _ride(body, 5, 4, rider, grid, None)
    outs = pl.pallas_call(
        body, name="na_bwd", grid=grid,
        in_specs=qkv_specs + [qkv_specs[0], bias_spec] + r_in_specs,
        out_specs=[qkv_specs[0], kv_out, kv_out, bias_spec] + r_out_specs,
        out_shape=[jax.ShapeDtypeStruct((T, width), BF16), jax.ShapeDtypeStruct((T, width), F32),
                   jax.ShapeDtypeStruct((T, width), F32), jax.ShapeDtypeStruct(bias.shape, F32)] + r_out,
        scratch_shapes=scratch,
        compiler_params=_params(2),
    )(qkva, qkva, qkva, dy, bias, *r_in)
    return (*outs[:4], list(outs[4:]))


def _rpb_fold(dslab):
    pairs = dslab.shape[0]
    H = 2 * pairs
    ncell = GRID_W * GRID_W

    def disassemble(d_ref, tz_ref):
        tz_ref[...] = jnp.zeros_like(tz_ref)
        for ro0 in range(NA_KH):
            for hh in range(2):
                for i in range(NA_KH):
                    tz_ref[hh, ro0 + i] += d_ref[0, ro0, hh * GRID_W:(hh + 1) * GRID_W, i * GRID_W:(i + 1) * GRID_W]

    dtoeplitz = pl.pallas_call(
        disassemble, name="rpb_fold_tiles", grid=(pairs,),
        in_specs=[pl.BlockSpec((1, NA_KH, 2 * GRID_W, NA_KH * GRID_W), lambda p: (p, 0, 0, 0))],
        out_specs=pl.BlockSpec((2, 2 * NA_KH, GRID_W, GRID_W), lambda p: (p, 0, 0, 0)),
        out_shape=jax.ShapeDtypeStruct((H, 2 * NA_KH, GRID_W, GRID_W), F32),
        compiler_params=_params(1),
    )(dslab).reshape(H * 2 * NA_KH, ncell)
    cell = np.arange(ncell)
    co = cell % GRID_W - cell // GRID_W + (NA_KW - 1)
    e_co = jnp.asarray((co[:, None] == np.arange(LANES)[None, :]).astype(np.float32))

    def diagonals(x_ref, e_ref, o_ref):
        o_ref[...] = jnp.dot(x_ref[...], e_ref[...], preferred_element_type=F32, precision=lax.Precision.HIGHEST)

    return pl.pallas_call(
        diagonals, name="rpb_fold", out_shape=jax.ShapeDtypeStruct((H * 2 * NA_KH, LANES), F32),
        compiler_params=_params(0),
    )(dtoeplitz, e_co).reshape(H, 2 * NA_KH, LANES)


SWA_KEYS = 3 * WIN


def _swa_block(j, t, qbn, T):
    blk = j * qbn + t
    start = jnp.clip((blk - 1) * WIN, 0, T - SWA_KEYS)
    row = lax.broadcasted_iota(jnp.int32, (2 * WIN, SWA_KEYS), 0)
    qpos = blk * WIN + jnp.where(row < WIN, row, row - WIN)
    kpos = start + lax.broadcasted_iota(jnp.int32, (2 * WIN, SWA_KEYS), 1)
    return pl.multiple_of(t * WIN, WIN), pl.multiple_of(start, WIN), jnp.abs(qpos - kpos) <= WIN


def _swa_sinks(sink_ref, p):
    row = lax.broadcasted_iota(jnp.int32, (2 * WIN, 1), 0)
    return jnp.where(row < WIN, sink_ref[p], sink_ref[p + NB_HEADS // 2])


def _swa_probs(s, mask, sink):
    s = jnp.where(mask, s * QK_SCALE, NEG)
    m = jnp.maximum(jnp.max(s, axis=-1, keepdims=True), sink)
    e = jnp.exp(s - m)
    esink = jnp.exp(sink - m)
    den = jnp.sum(e, axis=-1, keepdims=True) + esink
    return e / den, esink / den


def _swa_specs(T, qbn):
    return [pl.BlockSpec(memory_space=pltpu.SMEM),
            pl.BlockSpec((WIN * qbn, LANES), lambda p, j: (j, p)),
            pl.BlockSpec((T, LANES), lambda p, j: (0, 0)),
            pl.BlockSpec((T, LANES), lambda p, j: (0, 1))]


def _swa_fwd(qb, kvb, sink, *, qbn=32, group=32):
    T = qb.shape[0]
    pairs = NB_HEADS // 2
    qbn = min(qbn, T // WIN)
    group = min(group, qbn)

    def body(sink_ref, q_ref, k_ref, v_ref, y_ref):
        p, j = pl.program_id(0), pl.program_id(1)
        sinks = _swa_sinks(sink_ref, p)

        def blocks_step(t, carry):
            at = [_swa_block(j, t * group + u, qbn, T) for u in range(group)]
            s = [_dot_nt(_stack_heads(q_ref[pl.ds(q0, WIN), :]), k_ref[pl.ds(k0, SWA_KEYS), :]) for q0, k0, _ in at]
            pr = [_swa_probs(su, mask, sinks)[0] for su, (_, _, mask) in zip(s, at)]
            o = [_dot(pu.astype(BF16), v_ref[pl.ds(k0, SWA_KEYS), :]) for pu, (_, k0, _) in zip(pr, at)]
            for ou, (q0, _, _) in zip(o, at):
                y_ref[pl.ds(q0, WIN), :] = _unstack_heads(ou).astype(BF16)
            return carry

        lax.fori_loop(0, qbn // group, blocks_step, 0)

    specs = _swa_specs(T, qbn)
    return pl.pallas_call(
        body, name="swa_fwd", grid=(pairs, T // (WIN * qbn)),
        in_specs=specs, out_specs=specs[1],
        out_shape=jax.ShapeDtypeStruct((T, NB_HEADS * HEAD_DIM), BF16),
        compiler_params=_params(2),
    )(sink, qb, kvb, kvb)


def _swa_bwd(qb, kvb, dy, sink, *, qbn=16, group=16, rider=None):
    T = qb.shape[0]
    pairs = NB_HEADS // 2
    qbn = min(qbn, T // WIN)
    group = min(group, qbn)

    def body(sink_ref, q_ref, k_ref, v_ref, dy_ref, dq_ref, dk_ref, dv_ref, dsink_ref):
        p, j = pl.program_id(0), pl.program_id(1)
        sinks = _swa_sinks(sink_ref, p)

        @pl.when((p == 0) & (j == 0))
        def _():
            dk_ref[...] = jnp.zeros_like(dk_ref)
            dv_ref[...] = jnp.zeros_like(dv_ref)

        @pl.when(j == 0)
        def _():
            dsink_ref[...] = jnp.zeros_like(dsink_ref)

        def blocks_step(t, carry):
            at = [_swa_block(j, t * group + u, qbn, T) for u in range(group)]
            qs = [_stack_heads(q_ref[pl.ds(q0, WIN), :]) for q0, _, _ in at]
            dys = [_stack_heads(dy_ref[pl.ds(q0, WIN), :]) for q0, _, _ in at]
            s = [_dot_nt(qu, k_ref[pl.ds(k0, SWA_KEYS), :]) for qu, (_, k0, _) in zip(qs, at)]
            dp = [_dot_nt(du, v_ref[pl.ds(k0, SWA_KEYS), :]) for du, (_, k0, _) in zip(dys, at)]
            probs = [_swa_probs(su, mask, sinks) for su, (_, _, mask) in zip(s, at)]
            for u, (q0, k0, _) in enumerate(at):
                pr, psink = probs[u]
                delta = jnp.sum(pr * dp[u], axis=-1, keepdims=True)
                dsb = (pr * (dp[u] - delta)).astype(BF16)
                dsk = psink * delta
                for hh in range(2):
                    dsink_ref[0, hh:hh + 1, :] += jnp.broadcast_to(-jnp.sum(dsk[hh * WIN:(hh + 1) * WIN]), (1, LANES))
                dq_ref[pl.ds(q0, WIN), :] = _unstack_heads(_dot(dsb, k_ref[pl.ds(k0, SWA_KEYS), :])) * QK_SCALE
                dk_ref[pl.ds(k0, SWA_KEYS), :] += _dot_tn(dsb, qs[u]) * QK_SCALE
                dv_ref[pl.ds(k0, SWA_KEYS), :] += _dot_tn(pr.astype(BF16), dys[u])
            return carry

        lax.fori_loop(0, qbn // group, blocks_step, 0)

    specs = _swa_specs(T, qbn)
    kv_out = pl.BlockSpec((T, LANES), lambda p, j: (0, 0))
    grid = (pairs, T // (WIN * qbn))
    body, r_in, r_in_specs, r_out, r_out_specs, scratch = _ride(body, 5, 4, rider, grid, None)
    outs = pl.pallas_call(
        body, name="swa_bwd", grid=grid,
        in_specs=specs + [specs[1]] + r_in_specs,
        out_specs=[specs[1], kv_out, kv_out, pl.BlockSpec((1, 8, LANES), lambda p, j: (p, 0, 0))] + r_out_specs,
        out_shape=[jax.ShapeDtypeStruct((T, NB_HEADS * HEAD_DIM), F32), jax.ShapeDtypeStruct((T, LANES), F32),
                   jax.ShapeDtypeStruct((T, LANES), F32), jax.ShapeDtypeStruct((pairs, 8, LANES), F32)] + r_out,
        scratch_shapes=scratch,
        compiler_params=_params(2),
    )(sink, qb, kvb, kvb, dy, *r_in)
    return (*outs[:4], list(outs[4:]))


def _merge_fwd(ya, yb, gates, wa, wb, wout, h, *, tm=512):
    T, D = h.shape
    W = ya.shape[1]

    def body(ya_ref, yb_ref, gt_ref, wa_ref, wb_ref, wo_ref, h_ref, h2_ref, mg_ref):
        pa = _dot(ya_ref[...], wa_ref[...])
        pb = _dot(yb_ref[...], wb_ref[...])
        mg = (jax.nn.sigmoid(gt_ref[:, 0:D]) * pa + jax.nn.sigmoid(gt_ref[:, D:2 * D]) * pb).astype(BF16)
        mg_ref[...] = mg
        h2_ref[...] = h_ref[...] + _dot(mg, wo_ref[...])

    def tok(n):
        return pl.BlockSpec((tm, n), lambda i: (i, 0))

    def full(r, c):
        return pl.BlockSpec((r, c), lambda i: (0, 0))

    return pl.pallas_call(
        body, name="merge_fwd", grid=(T // tm,),
        in_specs=[tok(W), tok(W), tok(2 * D), full(W, D), full(W, D), full(D, D), tok(D)],
        out_specs=[tok(D), tok(D)],
        out_shape=[jax.ShapeDtypeStruct((T, D), F32), jax.ShapeDtypeStruct((T, D), BF16)],
        compiler_params=_params(1),
    )(ya, yb, gates, wa, wb, wout, h)


def _merge_bwd(dh, ya, yb, gates, wa, wb, wout, *, tm=512, rider=None):
    T, D = dh.shape
    W = ya.shape[1]

    def body(dh_ref, ya_ref, yb_ref, gt_ref, wa_ref, wb_ref, wo_ref, dya_ref, dyb_ref, dpa_ref, dpb_ref, dgt_ref):
        dmg = _dot_nt(dh_ref[...].astype(BF16), wo_ref[...])
        for y_ref, w_ref, dy_ref, dp_ref, lo in ((ya_ref, wa_ref, dya_ref, dpa_ref, 0), (yb_ref, wb_ref, dyb_ref, dpb_ref, D)):
            sg = jax.nn.sigmoid(gt_ref[:, lo:lo + D])
            dp = (dmg * sg).astype(BF16)
            dp_ref[...] = dp
            dgt_ref[:, lo:lo + D] = (dmg * _dot(y_ref[...], w_ref[...]) * (sg * (1.0 - sg))).astype(BF16)
            dy_ref[...] = _dot_nt(dp, w_ref[...]).astype(BF16)

    def tok(n):
        return pl.BlockSpec((tm, n), lambda i: (i, 0))

    def full(r, c):
        return pl.BlockSpec((r, c), lambda i: (0, 0))

    grid = (T // tm,)
    body, r_in, r_in_specs, r_out, r_out_specs, scratch = _ride(body, 7, 5, rider, grid, None)
    outs = pl.pallas_call(
        body, name="merge_bwd", grid=grid,
        in_specs=[tok(D), tok(W), tok(W), tok(2 * D), full(W, D), full(W, D), full(D, D)] + r_in_specs,
        out_specs=[tok(W), tok(W), tok(D), tok(D), tok(2 * D)] + r_out_specs,
        out_shape=[jax.ShapeDtypeStruct((T, W), BF16), jax.ShapeDtypeStruct((T, W), BF16),
                   jax.ShapeDtypeStruct((T, D), BF16), jax.ShapeDtypeStruct((T, D), BF16),
                   jax.ShapeDtypeStruct((T, 2 * D), BF16)] + r_out,
        scratch_shapes=scratch,
        compiler_params=_params(1),
    )(dh, ya, yb, gates, wa, wb, wout, *r_in)
    return (*outs[:5], list(outs[5:]))


def _pair_heads(a, axis):
    shp = a.shape
    a = a.reshape(shp[:axis] + (2, NB_HEADS // 2, HEAD_DIM) + shp[axis + 1:])
    return jnp.swapaxes(a, axis, axis + 1).reshape(shp)


def _unpair_heads(a, axis):
    shp = a.shape
    a = a.reshape(shp[:axis] + (NB_HEADS // 2, 2, HEAD_DIM) + shp[axis + 1:])
    return jnp.swapaxes(a, axis, axis + 1).reshape(shp)


FFN1 = ("ffn1_w_gate", "ffn1_w_up", "ffn1_w_down")
FFN2 = ("ffn2_w_gate", "ffn2_w_up", "ffn2_w_down")
MIXER = ("w_in", "w_branch_a", "w_branch_b", "w_out")
BRANCH = MIXER[1:]


def _layer_grads(x, target, g1, f1, gmix, late, rpb, sink, g2, gfin, comm=None):
    T = x.shape[0]
    tables = _rope_tables(T)
    bias = _na_bias_slabs(rpb)

    comm = comm or _Local(late)
    h1, n1, hdn1, p1, q1, gathered = _ffn_fwd(x, g1, *f1, name="ffn1_fwd", rider=comm.late_rider)
    w_in_t, wa, wb, wout, f2 = comm.late(gathered)
    w_in_p = jnp.concatenate([w_in_t[:O_QB], _pair_heads(w_in_t[O_QB:O_KB], 0), w_in_t[O_KB:]], axis=0)
    wb_p = _pair_heads(wb, 0)
    u, qkva, qb, kvb, gates = _mix_in_fwd(h1, gmix, w_in_p, tables)
    ya = _na_fwd(qkva, bias)
    yb = _swa_fwd(qb, kvb, sink)
    h2, merged = _merge_fwd(ya, yb, gates, wa, wb_p, wout, h1)
    dh3, n2, hdn2, p2, q2, loss, dgfin, _ = _ffn_fwd(h2, g2, *f2, name="ffn2_fwd", head=(gfin, target))

    dh2, da2, db2, dg2, _ = _ffn_bwd(dh3, h2, g2, p2, q2, *f2, name="ffn2_bwd")
    df2 = [_wgrad_shard_a(da2, n2, name="ffn2_dwg")[0], _wgrad_shard_a(db2, n2, name="ffn2_dwu")[0],
           _wgrad_shard_a(hdn2, dh3, scale=0.5, name="ffn2_dwd")[0]]
    red2 = comm.reduce(FFN2, df2, tag="ffn2")
    dya, dyb, dpa, dpb, dgates, _ = _merge_bwd(dh2, ya, yb, gates, wa, wb_p, wout)
    dwout = _wgrad_cols(merged, dh2, 1, name="dwout").reshape(N_CHIPS, D_MODEL // N_CHIPS, D_MODEL)
    dwa = _wgrad_cols(ya, dpa, N_CHIPS, name="dwa")
    dwb = _unpair_heads(_wgrad_cols(yb, dpb, N_CHIPS, name="dwb"), 1)
    redb = comm.reduce(BRANCH, [dwa, dwb, dwout], tag="branch")
    dqa, dka, dva, dbias, got = _na_bwd(qkva, dya, bias, rider=_two_riders(red2.sibling, redb.sibling))
    red2.partial(got[:len(FFN2)])
    redb.partial(got[len(FFN2):])
    drpb = _rpb_fold(dbias)
    dqb, dkb, dvb, dsink, got = _swa_bwd(qb, kvb, dyb, sink, rider=_two_riders(red2.chips, redb.chips))
    red2.halves(got[:len(FFN2)])
    redb.halves(got[len(FFN2):])
    dz, dh1, dgmix, got = _mix_in_bwd(dqa, dka, dva, dqb, dkb, dvb, dgates, h1, gmix, dh2, w_in_p, tables,
                                      rider=_two_riders(red2.share, redb.share))
    out = red2.result(got[:len(FFN2)])
    out.update(redb.result(got[len(FFN2):]))
    dwin_p = _wgrad_rows(dz, u, 2, name="dwin")[0].reshape(D_IN, D_MODEL)
    dwin = jnp.concatenate([dwin_p[:O_QB], _unpair_heads(dwin_p[O_QB:O_KB], 0), dwin_p[O_KB:]], axis=0)
    dx, da1, db1, dg1, _ = _ffn_bwd(dh1, x, g1, p1, q1, *f1, name="ffn1_bwd")
    small = dict(loss=loss, ffn1_norm=dg1, mix_norm=dgmix, ffn2_norm=dg2, final_norm=dgfin, na_rpb=drpb,
                 sink_logit=dsink[:, 0:2, 0].T.reshape(NB_HEADS))
    redw = comm.reduce(("w_in",), [dwin.reshape(N_CHIPS, D_IN // N_CHIPS, D_MODEL)], tag="w_in").partial_now()
    dwg1, got = _wgrad_shard_a(da1, n1, name="ffn1_dwg", rider=_two_riders(redw.chips, comm.small(small)))
    comm.small_done(got[1:])
    dwu1, got = _wgrad_shard_a(db1, n1, name="ffn1_dwu", rider=redw.halves(got[:1]).share)
    out.update(redw.result(got))
    red1 = comm.reduce(FFN1[:2], [dwg1, dwu1], tag="ffn1_gate_up").partial_now()
    dwd1, got = _wgrad_shard_a(hdn1, dh1, scale=0.5, name="ffn1_dwd", rider=red1.chips)
    out.update(red1.halves(got).result_now())
    out.update(comm.reduce(FFN1[2:], [dwd1], tag="ffn1_down").partial_now().halves_now().result_now())
    for names in (FFN2, ("w_in",), BRANCH[:2], BRANCH[2:], FFN1[:2], FFN1[2:]):
        comm.update(names, out)
    out.update(small, dx=dx)
    return out


class _Local:
    late_rider = None

    def __init__(self, late):
        self._late = late

    def late(self, gathered):
        return self._late

    def reduce(self, names, grads, *, tag):
        return _LocalReduce(names, grads)

    def update(self, names, reduced):
        pass

    def small(self, grads):
        return None

    def small_done(self, got):
        pass


class _LocalReduce:
    sibling = chips = share = None

    def __init__(self, names, grads):
        self._result = dict(zip(names, grads))

    def partial(self, got=None):
        return self

    halves = partial_now = halves_now = partial

    def result(self, got=None):
        return self._result

    result_now = result


ANY = pl.BlockSpec(memory_space=pl.ANY)


def _place():
    x, y, c = lax.axis_index("x"), lax.axis_index("y"), lax.axis_index("c")
    chips = [(1 - x, y), (x, 1 - y), (1 - x, 1 - y)]
    return x, y, c, 2 * x + y, chips


def _remote(src, dst, send_sems, recv_sems, k, device):
    return pltpu.make_async_remote_copy(src_ref=src, dst_ref=dst, send_sem=send_sems.at[k], recv_sem=recv_sems.at[k],
                                        device_id=device, device_id_type=MESH)


class _Rider:
    def __init__(self, inputs, out_shape, scratch, start, middle, finish):
        self.inputs, self.out_shape, self.scratch = list(inputs), list(out_shape), list(scratch)
        self.start, self.middle, self.finish = start, middle, finish


def _two_riders(first, second):
    if first is None or second is None:
        return first or second
    assert first.middle is None and second.middle is None
    n_in, n_out, n_sem = len(first.inputs), len(first.out_shape), len(first.scratch)

    def phase(name):
        def run(ins, outs, sems):
            getattr(first, name)(ins[:n_in], outs[:n_out], sems[:n_sem])
            getattr(second, name)(ins[n_in:], outs[n_out:], sems[n_sem:])
        return run

    return _Rider(first.inputs + second.inputs, first.out_shape + second.out_shape, first.scratch + second.scratch,
                  phase("start"), None, phase("finish"))


def _run_rider(rider, *, name):
    n_in, n_out = len(rider.inputs), len(rider.out_shape)

    def body(*refs):
        ins, outs, sems = refs[:n_in], refs[n_in:n_in + n_out], refs[n_in + n_out:]
        rider.start(ins, outs, sems)
        if rider.middle is not None:
            rider.middle(ins, outs, sems)
        rider.finish(ins, outs, sems)

    return pl.pallas_call(body, name=name, in_specs=[ANY] * n_in, out_specs=[ANY] * n_out, out_shape=rider.out_shape,
                          scratch_shapes=rider.scratch)(*rider.inputs)


def _ride(body, n_in, n_out, rider, grid, middle_step):
    if rider is None:
        return body, [], [], [], [], []
    r_in, r_out = len(rider.inputs), len(rider.out_shape)
    steps = math.prod(grid)

    def riding(*refs):
        ins, r_ins = refs[:n_in], refs[n_in:n_in + r_in]
        outs = refs[n_in + r_in:n_in + r_in + n_out]
        r_outs = refs[n_in + r_in + n_out:n_in + r_in + n_out + r_out]
        sems = refs[n_in + r_in + n_out + r_out:]
        step = pl.program_id(0)
        for axis in range(1, len(grid)):
            step = step * grid[axis] + pl.program_id(axis)

        @pl.when(step == 0)
        def _():
            rider.start(r_ins, r_outs, sems)

        body(*ins, *outs)

        if rider.middle is not None:
            @pl.when(step == middle_step)
            def _():
                rider.middle(r_ins, r_outs, sems)

        @pl.when(step == steps - 1)
        def _():
            rider.finish(r_ins, r_outs, sems)

    return riding, rider.inputs, [ANY] * r_in, rider.out_shape, [ANY] * r_out, rider.scratch


def _gather_rider(shards):
    n = len(shards)

    def plan(ins, outs, sems, kinds):
        send_sems, recv_sems, own_send_sems, own_recv_sems = sems
        x, y, c, mine, chips = _place()
        sibling = (x, y, 1 - c)
        made = {k: [] for k in kinds}
        for i in range(n):
            hr = shards[i].shape[0] // 2
            if "own" in made:
                made["own"].append(_remote(ins[i], outs[i].at[mine], own_send_sems, own_recv_sems, i, sibling))
            for j, (cx, cy) in enumerate(chips):
                here = outs[i].at[2 * cx + cy, pl.ds(c * hr, hr)]
                there = outs[i].at[2 * cx + cy, pl.ds((1 - c) * hr, hr)]
                if "sends" in made:
                    made["sends"].append(_remote(ins[i].at[pl.ds(c * hr, hr)], outs[i].at[mine, pl.ds(c * hr, hr)],
                                                 send_sems, recv_sems, 6 * i + j, (cx, cy, c)))
                if "landed" in made:
                    made["landed"].append(_remote(here, here, send_sems, recv_sems, 6 * i + j, (cx, cy, c)))
                if "passes" in made:
                    made["passes"].append(_remote(here, here, send_sems, recv_sems, 6 * i + 3 + j, sibling))
                if "others" in made:
                    made["others"].append(_remote(there, there, send_sems, recv_sems, 6 * i + 3 + j, sibling))
        return [made[k] for k in kinds]

    def start(ins, outs, sems):
        own, sends = plan(ins, outs, sems, ("own", "sends"))
        for cp in own + sends:
            cp.start()

    def middle(ins, outs, sems):
        landed, passes = plan(ins, outs, sems, ("landed", "passes"))
        for arrived, cp in zip(landed, passes):
            arrived.wait_recv()
            cp.start()

    def finish(ins, outs, sems):
        own, sends, passes, others = plan(ins, outs, sems, ("own", "sends", "passes", "others"))
        for arrived in others:
            arrived.wait_recv()
        for cp in sends + passes:
            cp.wait_send()
        for cp in own:
            cp.wait()

    return _Rider(shards, [jax.ShapeDtypeStruct((N_CHIPS,) + s.shape, s.dtype) for s in shards],
                  [pltpu.SemaphoreType.DMA((6 * n,)), pltpu.SemaphoreType.DMA((6 * n,)),
                   pltpu.SemaphoreType.DMA((n,)), pltpu.SemaphoreType.DMA((n,))], start, middle, finish)


def _swap_rider(arrays, out_shape, source):
    n = len(arrays)

    def plan(ins, outs, sems):
        send_sems, recv_sems = sems
        x, y, c, _, _ = _place()
        return [_remote(source(ins[i], c, i), outs[i], send_sems, recv_sems, i, (x, y, 1 - c)) for i in range(n)]

    def start(ins, outs, sems):
        for cp in plan(ins, outs, sems):
            cp.start()

    def finish(ins, outs, sems):
        for cp in plan(ins, outs, sems):
            cp.wait()

    return _Rider(arrays, out_shape, [pltpu.SemaphoreType.DMA((n,)), pltpu.SemaphoreType.DMA((n,))], start, None, finish)


def _sibling_rider(grads):
    half = [g.shape[1] // 2 for g in grads]
    return _swap_rider(grads, [jax.ShapeDtypeStruct((g.shape[0], hr, g.shape[2]), g.dtype) for g, hr in zip(grads, half)],
                       lambda ref, c, i: ref.at[:, pl.ds((1 - c) * half[i], half[i])])


def _share_rider(halves):
    return _swap_rider(halves, [jax.ShapeDtypeStruct(h.shape, h.dtype) for h in halves], lambda ref, c, i: ref)


def _chips_rider(parts):
    n = len(parts)

    def plan(ins, outs, sems):
        send_sems, recv_sems = sems
        _, _, c, _, chips = _place()
        return [_remote(ins[i].at[2 * cx + cy], outs[i].at[j], send_sems, recv_sems, 3 * i + j, (cx, cy, c))
                for i in range(n) for j, (cx, cy) in enumerate(chips)]

    def start(ins, outs, sems):
        for cp in plan(ins, outs, sems):
            cp.start()

    def finish(ins, outs, sems):
        for cp in plan(ins, outs, sems):
            cp.wait()

    return _Rider(parts, [jax.ShapeDtypeStruct((N_CHIPS - 1,) + p.shape[1:], p.dtype) for p in parts],
                  [pltpu.SemaphoreType.DMA((3 * n,)), pltpu.SemaphoreType.DMA((3 * n,))], start, None, finish)


class _Reduce:
    def __init__(self, names, grads, cidx, chip, *, tag):
        self.names, self.grads, self.cidx, self.chip, self.tag = names, grads, cidx, chip, tag
        self.sibling = _sibling_rider(grads)

    def _by_shape(self, fn, *lists):
        done, i = [], 0
        while i < len(self.names):
            j = i + 1
            while j < len(self.names) and self.grads[j].shape == self.grads[i].shape:
                j += 1
            done += fn(*[lst[i:j] for lst in lists], self.names[i])
            i = j
        return done

    def partial(self, from_sibling):
        self.from_sibling = from_sibling
        self.chips = _chips_rider(self._by_shape(
            lambda g, r, k: _add_sibling(g, r, self.cidx, name="add_sibling_" + k), self.grads, from_sibling))
        return self

    def halves(self, from_chips):
        self.mine = self._by_shape(
            lambda g, r1, r2, k: _add_chips(g, r1, r2, self.cidx, self.chip, name="add_chips_" + k),
            self.grads, self.from_sibling, from_chips)
        self.share = _share_rider(self.mine)
        return self

    def result(self, others):
        return dict(zip(self.names, zip(self.mine, others)))

    def partial_now(self):
        return self.partial(_run_rider(self.sibling, name="rs_sibling_" + self.tag))

    def halves_now(self):
        return self.halves(_run_rider(self.chips, name="rs_chips_" + self.tag))

    def result_now(self):
        return self.result(_run_rider(self.share, name="rs_share_" + self.tag))


N_DEV = 8


def _small_rider(vec):
    def plan(ins, outs, sems):
        send_sems, recv_sems = sems
        x, y, c, _, _ = _place()
        return [_remote(ins[0], outs[0].at[k - 1], send_sems, recv_sems, k - 1, (x ^ (k >> 2), y ^ ((k >> 1) & 1), c ^ (k & 1)))
                for k in range(1, N_DEV)]

    def start(ins, outs, sems):
        for cp in plan(ins, outs, sems):
            cp.start()

    def finish(ins, outs, sems):
        for cp in plan(ins, outs, sems):
            cp.wait()

    return _Rider([vec], [jax.ShapeDtypeStruct((N_DEV - 1,) + vec.shape, vec.dtype)],
                  [pltpu.SemaphoreType.DMA((N_DEV - 1,)), pltpu.SemaphoreType.DMA((N_DEV - 1,))], start, None, finish)


def _small_sum(vec, others, me):
    def body(me_ref, v_ref, b_ref, o_ref):
        mine = me_ref[0]
        acc = None
        for d in range(N_DEV):
            term = jnp.where(mine == d, v_ref[...], b_ref[jnp.maximum((mine ^ d) - 1, 0)])
            acc = term if acc is None else acc + term
        o_ref[...] = acc

    vmem = pl.BlockSpec(memory_space=pltpu.VMEM)
    return pl.pallas_call(
        body, name="small_sum", in_specs=[pl.BlockSpec(memory_space=pltpu.SMEM), vmem, vmem], out_specs=vmem,
        out_shape=jax.ShapeDtypeStruct(vec.shape, vec.dtype),
    )(me, vec, others)


ELEMWISE_BLOCK = 512 * 1024


def _row_tile(rows, cols):
    best = None
    for t in range(16, rows + 1, 16):
        if rows % t == 0 and t * cols <= ELEMWISE_BLOCK:
            best = t
    return best if best is not None else rows


def _add_sibling(gs, r1s, cidx, *, name):
    n = len(gs)
    S, R, C = gs[0].shape
    hr = R // 2
    tr = _row_tile(hr, C)
    nt = hr // tr

    def body(c_ref, *refs):
        for g_ref, r_ref, o_ref in zip(refs[:n], refs[n:2 * n], refs[2 * n:]):
            o_ref[...] = (g_ref[...] + r_ref[...]).astype(BF16)

    blk = pl.BlockSpec((1, tr, C), lambda s, t, c: (s, t, 0))
    mine = pl.BlockSpec((1, tr, C), lambda s, t, c: (s, c[0] * nt + t, 0))
    return list(pl.pallas_call(
        body, name=name,
        grid_spec=pltpu.PrefetchScalarGridSpec(
            num_scalar_prefetch=1, grid=(S, nt), in_specs=[mine] * n + [blk] * n, out_specs=[blk] * n),
        out_shape=[jax.ShapeDtypeStruct((S, hr, C), BF16)] * n,
        compiler_params=_params(2),
    )(cidx, *gs, *r1s))


def _add_chips(gs, r1s, r2s, cidx, chip, *, name):
    n = len(gs)
    _, R, C = gs[0].shape
    hr = R // 2
    tr = _row_tile(hr, C)
    nt = hr // tr

    def body(pos_ref, *refs):
        for g_ref, r1_ref, r2_ref, o_ref in zip(refs[:n], refs[n:2 * n], refs[2 * n:3 * n], refs[3 * n:]):
            own = g_ref[0] + r1_ref[0]
            o_ref[...] = ((own + r2_ref[0].astype(F32)) + r2_ref[1].astype(F32)) + r2_ref[2].astype(F32)

    pos = jnp.concatenate([cidx, chip])
    return list(pl.pallas_call(
        body, name=name,
        grid_spec=pltpu.PrefetchScalarGridSpec(
            num_scalar_prefetch=1, grid=(nt,),
            in_specs=[pl.BlockSpec((1, tr, C), lambda t, pos: (pos[1], pos[0] * nt + t, 0))] * n
            + [pl.BlockSpec((1, tr, C), lambda t, pos: (pos[1], t, 0))] * n
            + [pl.BlockSpec((N_CHIPS - 1, tr, C), lambda t, pos: (0, t, 0))] * n,
            out_specs=[pl.BlockSpec((tr, C), lambda t, pos: (t, 0))] * n),
        out_shape=[jax.ShapeDtypeStruct((hr, C), F32)] * n,
        compiler_params=_params(1),
    )(pos, *gs, *r1s, *r2s))


def _adamw_math(w, g, m, v):
    mn = ADAM_B1 * m + (1.0 - ADAM_B1) * g
    vn = ADAM_B2 * v + (1.0 - ADAM_B2) * (g * g)
    m_hat = mn / (1.0 - ADAM_B1 ** ADAM_STEP)
    v_hat = vn / (1.0 - ADAM_B2 ** ADAM_STEP)
    return -ADAM_LR * (m_hat / (jnp.sqrt(v_hat) + ADAM_EPS) + ADAM_WD * w), mn, vn


def _adamw_halves(ws, mines, others, ms, vs, cidx, *, name):
    n = len(ws)
    R, C = ws[0].shape
    hr = R // 2
    tr = _row_tile(hr, C * min(n, 2))
    nt = hr // tr

    def body(c_ref, *refs):
        for i in range(n):
            w_ref, a_ref, b_ref, m_ref, v_ref = (refs[j * n + i] for j in range(5))
            g_ref, d_ref, mo_ref, vo_ref = (refs[(5 + j) * n + i] for j in range(4))
            gv = jnp.where(pl.program_id(0) == c_ref[0], a_ref[...], b_ref[...])
            g_ref[...] = gv
            d_ref[...], mo_ref[...], vo_ref[...] = _adamw_math(w_ref[...], gv, m_ref[...], v_ref[...])

    full = pl.BlockSpec((tr, C), lambda h, t, c: (h * nt + t, 0))
    own = pl.BlockSpec((tr, C), lambda h, t, c: (jnp.where(h == c[0], t, 0), 0))
    sib = pl.BlockSpec((tr, C), lambda h, t, c: (jnp.where(h == c[0], 0, t), 0))
    shape = jax.ShapeDtypeStruct((R, C), F32)
    outs = pl.pallas_call(
        body, name=name,
        grid_spec=pltpu.PrefetchScalarGridSpec(
            num_scalar_prefetch=1, grid=(2, nt),
            in_specs=[full] * n + [own] * n + [sib] * n + [full] * (2 * n), out_specs=[full] * (4 * n)),
        out_shape=[shape] * (4 * n),
        compiler_params=_params(2),
    )(cidx, *ws, *mines, *others, *ms, *vs)
    return [list(outs[j * n:(j + 1) * n]) for j in range(4)]


def _adamw_small(ws, gs, ms, vs):
    n = len(ws)

    def body(*refs):
        for i in range(n):
            w_ref, g_ref, m_ref, v_ref = (refs[j * n + i] for j in range(4))
            d_ref, mo_ref, vo_ref = (refs[(4 + j) * n + i] for j in range(3))
            d_ref[...], mo_ref[...], vo_ref[...] = _adamw_math(w_ref[...], g_ref[...], m_ref[...], v_ref[...])

    shapes = [jax.ShapeDtypeStruct(a.shape, F32) for a in ws]
    outs = pl.pallas_call(body, name="adamw_small", out_shape=shapes * 3, compiler_params=_params(0))(*ws, *gs, *ms, *vs)
    return outs[:n], outs[n:2 * n], outs[2 * n:]


def _unstack_cols(w):
    s, r, c = w.shape
    return w.transpose(1, 0, 2).reshape(r, s * c)


def _pad_rows(a, rows):
    return jnp.pad(a, ((0, rows - a.shape[0]), (0, LANES - a.shape[1])))


BIG = ("ffn1_w_gate", "ffn1_w_up", "ffn1_w_down", "w_in", "w_branch_a", "w_branch_b", "w_out",
       "ffn2_w_gate", "ffn2_w_up", "ffn2_w_down")
TRANSPOSED = ("ffn1_w_gate", "ffn1_w_up", "w_in", "ffn2_w_gate", "ffn2_w_up")
WEIGHTS = ("ffn1_norm", "ffn1_w_gate", "ffn1_w_up", "ffn1_w_down", "mix_norm", "w_in", "na_rpb", "sink_logit",
           "w_branch_a", "w_branch_b", "w_out", "ffn2_norm", "ffn2_w_gate", "ffn2_w_up", "ffn2_w_down", "final_norm")


def kernel(x, ffn1_norm, ffn1_w_gate, ffn1_w_up, ffn1_w_down, mix_norm, w_in, na_rpb, sink_logit, w_branch_a, w_branch_b, w_out, ffn2_norm, ffn2_w_gate, ffn2_w_up, ffn2_w_down, final_norm, loss_target, m_ffn1_norm, m_ffn1_w_gate, m_ffn1_w_up, m_ffn1_w_down, m_mix_norm, m_w_in, m_na_rpb, m_sink_logit, m_w_branch_a, m_w_branch_b, m_w_out, m_ffn2_norm, m_ffn2_w_gate, m_ffn2_w_up, m_ffn2_w_down, m_final_norm, v_ffn1_norm, v_ffn1_w_gate, v_ffn1_w_up, v_ffn1_w_down, v_mix_norm, v_w_in, v_na_rpb, v_sink_logit, v_w_branch_a, v_w_branch_b, v_w_out, v_ffn2_norm, v_ffn2_w_gate, v_ffn2_w_up, v_ffn2_w_down, v_final_norm):
    args = dict(locals())
    w = {k: args[k] for k in WEIGHTS}
    mom = {k: args["m_" + k] for k in WEIGHTS}
    var = {k: args["v_" + k] for k in WEIGHTS}
    cidx = lax.axis_index("c").astype(jnp.int32).reshape(1)
    chip = (2 * lax.axis_index("x") + lax.axis_index("y")).astype(jnp.int32).reshape(1)

    def shard(a, k):
        return jnp.swapaxes(a[0], 0, 1) if k in TRANSPOSED else a[0]

    def unshard(a, k):
        return (jnp.swapaxes(a, 0, 1) if k in TRANSPOSED else a)[None]

    def bf16_shards(names):
        return [shard(w[k], k).astype(BF16) for k in names]

    class comm:
        late_rider = _gather_rider(bf16_shards(MIXER + FFN2))

        @staticmethod
        def late(gathered):
            full = dict(zip(MIXER + FFN2, gathered))
            return (full["w_in"].reshape(D_IN, D_MODEL), _unstack_cols(full["w_branch_a"]), _unstack_cols(full["w_branch_b"]),
                    full["w_out"].reshape(D_MODEL, D_MODEL), tuple(full[k] for k in FFN2))

        @staticmethod
        def reduce(names, grads, *, tag):
            return _Reduce(names, grads, cidx, chip, tag=tag)

        @staticmethod
        def update(names, reduced):
            res = _adamw_halves([shard(w[k], k) for k in names], [reduced[k][0] for k in names],
                                [reduced[k][1] for k in names], [shard(mom[k], k) for k in names],
                                [shard(var[k], k) for k in names], cidx, name="adamw_" + names[0])
            for i, k in enumerate(names):
                grads_out[k], deltas[k], new_m[k], new_v[k] = (unshard(a[i], k) for a in res)

        @staticmethod
        def small(g):
            packed["mine"] = jnp.concatenate([
                g["ffn1_norm"].reshape(rows, LANES), g["mix_norm"].reshape(rows, LANES), g["ffn2_norm"].reshape(rows, LANES),
                g["final_norm"].reshape(rows, LANES), g["na_rpb"].reshape(-1, LANES),
                _pad_rows(g["sink_logit"].reshape(1, NB_HEADS), 8), _pad_rows(g["loss"], 8)], axis=0)
            return _small_rider(packed["mine"])

        @staticmethod
        def small_done(got):
            packed["others"], = got

    deltas, new_m, new_v, grads_out, grad, packed = {}, {}, {}, {}, {}, {}
    rows = D_MODEL // LANES
    f1 = _run_rider(_gather_rider(bf16_shards(FFN1)), name="all_gather_ffn1")
    out = _layer_grads(x[0], loss_target[0], ffn1_norm, f1, mix_norm, None, na_rpb[0], sink_logit[0], ffn2_norm,
                       final_norm.reshape(1, D_MODEL), comm=comm)

    me = (4 * lax.axis_index("x") + 2 * lax.axis_index("y") + lax.axis_index("c")).astype(jnp.int32).reshape(1)
    total = _small_sum(packed["mine"], packed["others"], me)
    n_rpb = NA_HEADS * 2 * NA_KH
    grad["ffn1_norm"] = total[0:rows].reshape(1, D_MODEL)
    grad["mix_norm"] = total[rows:2 * rows].reshape(1, D_MODEL)
    grad["ffn2_norm"] = total[2 * rows:3 * rows].reshape(1, D_MODEL)
    grad["final_norm"] = total[3 * rows:4 * rows].reshape(1, D_MODEL)
    grad["na_rpb"] = total[4 * rows:4 * rows + n_rpb].reshape(NA_HEADS, 2 * NA_KH, LANES)[:, :2 * NA_KH - 1, :2 * NA_KW - 1]
    grad["na_rpb"] = grad["na_rpb"].reshape(NA_HEADS, -1)
    grad["sink_logit"] = total[4 * rows + n_rpb:4 * rows + n_rpb + 1, 0:NB_HEADS]
    loss = total[4 * rows + n_rpb + 8, 0]

    small_names = [k for k in WEIGHTS if k not in BIG]
    res = _adamw_small(*[[a[k].reshape(grad[k].shape) for k in small_names] for a in (w, grad, mom, var)])
    for i, k in enumerate(small_names):
        grads_out[k], deltas[k], new_m[k], new_v[k] = (a.reshape(w[k].shape) for a in (grad[k], res[0][i], res[1][i], res[2][i]))
    return (loss, out["dx"].reshape(x.shape), *[grads_out[k] for k in WEIGHTS], *[deltas[k] for k in WEIGHTS],
            *[new_m[k] for k in WEIGHTS], *[new_v[k] for k in WEIGHTS])
```

```python
import math

import jax
import jax.numpy as jnp
import numpy as np
from jax import lax
from jax.experimental import pallas as pl
from jax.experimental.pallas import tpu as pltpu

F32 = jnp.float32
BF16 = jnp.bfloat16

D_MODEL = 1024
HEAD_DIM = 64
NA_HEADS = 8
NB_HEADS = 8
GRID_W = 64
NA_KH = 8
NA_KW = 16
WIN = 128
ROPE_THETA = 10000.0
EPS = 1e-6
N_CHIPS = 4
QK_SCALE = HEAD_DIM ** -0.5
NEG = -1e30
LANES = 128
VMEM_LIMIT = 56 * 1024 * 1024
HEAD_ROWS = 256
WGRAD_TOKENS_BYTES = 8192

C_QKVA = 3 * NA_HEADS * HEAD_DIM
C_QB = NB_HEADS * HEAD_DIM
C_KB = 2 * HEAD_DIM
C_ROPE = C_QB + C_KB
C_GATES = 2 * D_MODEL
D_IN = C_QKVA + C_QB + 2 * C_KB + C_GATES
O_QB = C_QKVA
O_KB = O_QB + C_QB
O_VB = O_KB + C_KB
O_G = O_VB + C_KB

ADAM_LR = 0.001
ADAM_B1 = 0.9
ADAM_B2 = 0.999
ADAM_EPS = 1e-08
ADAM_WD = 0.01
ADAM_STEP = 10

MESH = pl.DeviceIdType.MESH


def _dot(a, b):
    return jnp.dot(a, b, preferred_element_type=F32)


def _dot_nt(a, b):
    return lax.dot_general(a, b, (((1,), (1,)), ((), ())), preferred_element_type=F32)


def _dot_tn(a, b):
    return lax.dot_general(a, b, (((0,), (0,)), ((), ())), preferred_element_type=F32)


def _params(n_axes):
    return pltpu.CompilerParams(dimension_semantics=("arbitrary",) * n_axes, vmem_limit_bytes=VMEM_LIMIT)


def _rstd(xf):
    return lax.rsqrt(jnp.mean(xf * xf, axis=-1, keepdims=True) + EPS)


def _norm_bwd(dn, xf, g, r):
    xhat = xf * r
    dxh = dn * g
    dx = r * (dxh - xhat * jnp.mean(dxh * xhat, axis=-1, keepdims=True))
    return dx, dn * xhat


def _sigmoid(x):
    return 0.5 * jnp.tanh(0.5 * x) + 0.5


def _loss_head(hf, gv, tgt):
    r = _rstd(hf)
    err = (hf * r) * gv - tgt
    dx, dgr = _norm_bwd(err * (1.0 / hf.shape[-1]), hf, gv, r)
    return 0.5 * jnp.mean(err * err, axis=-1, keepdims=True), dx, dgr


def _ffn_fwd(x, g, wg, wu, wd, *, name, tm=1024, sub=512, rider=None, head=None):
    T, D = x.shape
    F = wg.shape[1]
    tm = min(tm, T)
    sub = min(sub, tm)
    n_head = 0 if head is None else 2

    def body(*refs):
        x_ref, g_ref, wg_ref, wu_ref, wd_ref = refs[:5]
        h_ref, n_ref, hdn_ref, p_ref, q_ref = refs[5 + n_head:10 + n_head]
        i, s = pl.program_id(0), pl.program_id(1)
        _ffn_fwd_step(x_ref, g_ref, wg_ref, wu_ref, wd_ref, h_ref, n_ref, hdn_ref, p_ref, q_ref, s)
        if head is not None:
            gf_ref, t_ref = refs[5:7]
            loss_ref, dgf_ref = refs[10 + n_head:]

            @pl.when((i == 0) & (s == 0))
            def _():
                loss_ref[...] = jnp.zeros_like(loss_ref)
                dgf_ref[...] = jnp.zeros_like(dgf_ref)

            @pl.when(s == N_CHIPS - 1)
            def _():
                for u in range(tm // HEAD_ROWS):
                    r = pl.ds(u * HEAD_ROWS, HEAD_ROWS)
                    terms, dh, dgr = _loss_head(h_ref[r, :], gf_ref[...], t_ref[r, :])
                    loss_ref[...] += jnp.broadcast_to(jnp.sum(terms), loss_ref.shape)
                    dgf_ref[...] += jnp.sum(dgr, axis=0, keepdims=True)
                    h_ref[r, :] = dh

    def _ffn_fwd_step(x_ref, g_ref, wg_ref, wu_ref, wd_ref, h_ref, n_ref, hdn_ref, p_ref, q_ref, s):

        @pl.when(s == 0)
        def _():
            xf = x_ref[...]
            n_ref[...] = ((xf * _rstd(xf)) * g_ref[...]).astype(BF16)
            h_ref[...] = xf

        rows = [pl.ds(u * sub, sub) for u in range(tm // sub)]
        ab = [(_dot_nt(n_ref[r, :], wg_ref[0]), _dot_nt(n_ref[r, :], wu_ref[0])) for r in rows]
        hdns = []
        for r, (a, b) in zip(rows, ab):
            sg = _sigmoid(a)
            silu = a * sg
            hdn = (silu * b).astype(BF16)
            hdn_ref[0, r, :] = hdn
            p_ref[0, r, :] = (b * (sg + silu * (1.0 - sg))).astype(BF16)
            q_ref[0, r, :] = silu.astype(BF16)
            hdns.append(hdn)
        for r, hdn in zip(rows, hdns):
            h_ref[r, :] += 0.5 * _dot(hdn, wd_ref[0])

    tok = pl.BlockSpec((tm, D), lambda i, s: (i, 0))
    hid = pl.BlockSpec((1, tm, F), lambda i, s: (s, i, 0))
    wspec = pl.BlockSpec((1, F, D), lambda i, s: (s, 0, 0))
    hshape = jax.ShapeDtypeStruct((N_CHIPS, T, F), BF16)
    grid = (T // tm, N_CHIPS)
    vec = pl.BlockSpec((1, D), lambda i, s: (0, 0))
    head_in, head_in_specs, head_out, head_out_specs = [], [], [], []
    if head is not None:
        head_in, head_in_specs = list(head), [vec, tok]
        head_out = [jax.ShapeDtypeStruct((1, LANES), F32), jax.ShapeDtypeStruct((1, D), F32)]
        head_out_specs = [pl.BlockSpec((1, LANES), lambda i, s: (0, 0)), vec]
    n_main = 5 + n_head
    body, r_in, r_in_specs, r_out, r_out_specs, scratch = _ride(body, n_main, n_main, rider, grid, (grid[0] * grid[1] * 7) // 8)
    outs = pl.pallas_call(
        body, name=name, grid=grid,
        in_specs=[tok, vec, wspec, wspec, wspec] + head_in_specs + r_in_specs,
        out_specs=[tok, tok, hid, hid, hid] + head_out_specs + r_out_specs,
        out_shape=[jax.ShapeDtypeStruct((T, D), F32), jax.ShapeDtypeStruct((T, D), BF16), hshape, hshape, hshape]
        + head_out + r_out,
        scratch_shapes=scratch,
        compiler_params=_params(2),
    )(x, g, wg, wu, wd, *head_in, *r_in)
    return (*outs[:n_main], list(outs[n_main:]))


def _ffn_bwd(dh, x, g, p, q, wg, wu, wd, *, name, tm=1024, sub=256, rider=None):
    T, D = x.shape
    F = wg.shape[1]
    tm = min(tm, T)
    sub = min(sub, tm)

    def body(dh_ref, x_ref, g_ref, p_ref, q_ref, wg_ref, wu_ref, wd_ref, dx_ref, da_ref, db_ref, dg_ref):
        i, s = pl.program_id(0), pl.program_id(1)

        @pl.when((i == 0) & (s == 0))
        def _():
            dg_ref[...] = jnp.zeros_like(dg_ref)

        @pl.when(s == 0)
        def _():
            dx_ref[...] = jnp.zeros_like(dx_ref)

        rows = [pl.ds(u * sub, sub) for u in range(tm // sub)]
        dhdn = [_dot_nt((0.5 * dh_ref[r, :]).astype(BF16), wd_ref[0]) for r in rows]
        das, dbs = [], []
        for r, dd in zip(rows, dhdn):
            da = (dd * p_ref[0, r, :].astype(F32)).astype(BF16)
            db = (dd * q_ref[0, r, :].astype(F32)).astype(BF16)
            da_ref[0, r, :] = da
            db_ref[0, r, :] = db
            das.append(da)
            dbs.append(db)
        for r, da, db in zip(rows, das, dbs):
            dx_ref[r, :] += _dot(da, wg_ref[0]) + _dot(db, wu_ref[0])

        @pl.when(s == N_CHIPS - 1)
        def _():
            xf = x_ref[...]
            dx, dgr = _norm_bwd(dx_ref[...], xf, g_ref[...], _rstd(xf))
            dg_ref[...] += jnp.sum(dgr, axis=0, keepdims=True)
            dx_ref[...] = dh_ref[...] + dx

    tok = pl.BlockSpec((tm, D), lambda i, s: (i, 0))
    hid = pl.BlockSpec((1, tm, F), lambda i, s: (s, i, 0))
    vec = pl.BlockSpec((1, D), lambda i, s: (0, 0))
    hshape = jax.ShapeDtypeStruct((N_CHIPS, T, F), BF16)
    wspec = pl.BlockSpec((1, F, D), lambda i, s: (s, 0, 0))
    grid = (T // tm, N_CHIPS)
    body, r_in, r_in_specs, r_out, r_out_specs, scratch = _ride(body, 8, 4, rider, grid, None)
    outs = pl.pallas_call(
        body, name=name, grid=grid,
        in_specs=[tok, tok, vec, hid, hid, wspec, wspec, wspec] + r_in_specs,
        out_specs=[tok, hid, hid, vec] + r_out_specs,
        out_shape=[jax.ShapeDtypeStruct((T, D), F32), hshape, hshape, jax.ShapeDtypeStruct((1, D), F32)] + r_out,
        scratch_shapes=scratch,
        compiler_params=_params(2),
    )(dh, x, g, p, q, wg, wu, wd, *r_in)
    return (*outs[:4], list(outs[4:]))


def _wgrad(a, b, *, a_block, a_map, b_block, b_map, out_shape, o_block, o_map, grid, scale=1.0, name, rider=None):
    def body(a_ref, b_ref, o_ref):
        @pl.when(pl.program_id(len(grid) - 1) == 0)
        def _():
            o_ref[...] = jnp.zeros_like(o_ref)

        av = a_ref[...]
        bv = b_ref[...]
        av = av.reshape(av.shape[-2:]).astype(BF16)
        bv = bv.reshape(bv.shape[-2:])
        if scale != 1.0:
            bv = scale * bv
        o_ref[...] += _dot_tn(av, bv.astype(BF16)).reshape(o_ref.shape)

    body, r_in, r_in_specs, r_out, r_out_specs, scratch = _ride(body, 2, 1, rider, grid, None)
    outs = pl.pallas_call(
        body, name=name, grid=grid,
        in_specs=[pl.BlockSpec(a_block, a_map), pl.BlockSpec(b_block, b_map)] + r_in_specs,
        out_specs=[pl.BlockSpec(o_block, o_map)] + r_out_specs,
        out_shape=[jax.ShapeDtypeStruct(out_shape, F32)] + r_out,
        scratch_shapes=scratch,
        compiler_params=_params(len(grid)),
    )(a, b, *r_in)
    return outs[0], list(outs[1:])


def _wgrad_rows(a, b, n_blocks, *, name, tk=2048):
    T, N = b.shape
    M = a.shape[1] // n_blocks
    tk = min(tk, T)
    return _wgrad(a, b, a_block=(tk, M), a_map=lambda s, k: (k, s), b_block=(tk, N), b_map=lambda s, k: (k, 0),
                  out_shape=(n_blocks, M, N), o_block=(1, M, N), o_map=lambda s, k: (s, 0, 0), grid=(n_blocks, T // tk), name=name)


def _wgrad_shard_a(a, b, *, name, scale=1.0, rider=None):
    S, T, M = a.shape
    N = b.shape[1]
    tk = min(WGRAD_TOKENS_BYTES // b.dtype.itemsize, T)
    return _wgrad(a, b, a_block=(1, tk, M), a_map=lambda s, k: (s, k, 0), b_block=(tk, N), b_map=lambda s, k: (k, 0),
                  out_shape=(S, M, N), o_block=(1, M, N), o_map=lambda s, k: (s, 0, 0), grid=(S, T // tk), scale=scale,
                  name=name, rider=rider)


def _wgrad_cols(a, b, n_blocks, *, name, tk=2048):
    T, M = a.shape
    N = b.shape[1] // n_blocks
    tk = min(tk, T)

    def body(a_ref, b_ref, o_ref):
        @pl.when(pl.program_id(0) == 0)
        def _():
            o_ref[...] = jnp.zeros_like(o_ref)

        r = _dot_tn(a_ref[...].astype(BF16), b_ref[...].astype(BF16))
        for s in range(n_blocks):
            o_ref[s] += r[:, s * N:(s + 1) * N]

    return pl.pallas_call(
        body, name=name, grid=(T // tk,),
        in_specs=[pl.BlockSpec((tk, M), lambda k: (k, 0)), pl.BlockSpec((tk, n_blocks * N), lambda k: (k, 0))],
        out_specs=pl.BlockSpec((n_blocks, M, N), lambda k: (0, 0, 0)),
        out_shape=jax.ShapeDtypeStruct((n_blocks, M, N), F32),
        compiler_params=_params(1),
    )(a, b)


def _rope_tables(T):
    half = HEAD_DIM // 2
    inv = np.float32(ROPE_THETA) ** (-np.arange(half, dtype=np.float32) / np.float32(half))
    ang = np.arange(T, dtype=np.float32)[:, None] * inv[None, :]
    cos, sin, zero = np.cos(ang), np.sin(ang), np.zeros_like(ang)
    reps = LANES // HEAD_DIM
    return (jnp.asarray(np.tile(np.concatenate([cos, cos], axis=1), (1, reps))),
            jnp.asarray(np.tile(np.concatenate([-sin, zero], axis=1), (1, reps))),
            jnp.asarray(np.tile(np.concatenate([zero, sin], axis=1), (1, reps))))


def _rope(x, cos, sa, sb, sign):
    half = HEAD_DIM // 2
    return x * cos + sign * (pltpu.roll(x, LANES - half, 1) * sa + pltpu.roll(x, half, 1) * sb)


def _mix_in_fwd(h, g, w_in, tables, *, tm=512):
    T, D = h.shape

    def body(h_ref, g_ref, w_ref, cos_ref, sa_ref, sb_ref, u_ref, qkva_ref, qb_ref, kvb_ref, gates_ref):
        hf = h_ref[...]
        u = ((hf * _rstd(hf)) * g_ref[...]).astype(BF16)
        u_ref[...] = u
        qkva_ref[...] = _dot_nt(u, w_ref[0:C_QKVA, :]).astype(BF16)
        zr = _dot_nt(u, w_ref[O_QB:O_QB + C_ROPE, :])
        cos, sa, sb = cos_ref[...], sa_ref[...], sb_ref[...]
        for j in range(C_ROPE // LANES):
            rj = _rope(zr[:, j * LANES:(j + 1) * LANES], cos, sa, sb, 1.0).astype(BF16)
            if j < C_QB // LANES:
                qb_ref[:, j * LANES:(j + 1) * LANES] = rj
            else:
                kvb_ref[:, 0:C_KB] = rj
        kvb_ref[:, C_KB:2 * C_KB] = _dot_nt(u, w_ref[O_VB:O_VB + C_KB, :]).astype(BF16)
        gates_ref[...] = _dot_nt(u, w_ref[O_G:O_G + C_GATES, :])

    def tok(n):
        return pl.BlockSpec((tm, n), lambda i: (i, 0))

    return pl.pallas_call(
        body, name="mix_in_fwd", grid=(T // tm,),
        in_specs=[tok(D), pl.BlockSpec((1, D), lambda i: (0, 0)), pl.BlockSpec((D_IN, D), lambda i: (0, 0), pipeline_mode=pl.Buffered(1)),
                  tok(LANES), tok(LANES), tok(LANES)],
        out_specs=[tok(D), tok(C_QKVA), tok(C_QB), tok(2 * C_KB), tok(C_GATES)],
        out_shape=[jax.ShapeDtypeStruct((T, D), BF16), jax.ShapeDtypeStruct((T, C_QKVA), BF16),
                   jax.ShapeDtypeStruct((T, C_QB), BF16), jax.ShapeDtypeStruct((T, 2 * C_KB), BF16),
                   jax.ShapeDtypeStruct((T, C_GATES), F32)],
        compiler_params=_params(1),
    )(h, g, w_in, *tables)


def _mix_in_bwd(dqa, dka, dva, dqb, dkb, dvb, dgates, h, g, dres, w_in, tables, *, tm=512, rider=None):
    T, D = h.shape

    def body(dqa_ref, dka_ref, dva_ref, dqb_ref, dkb_ref, dvb_ref, dgt_ref, h_ref, g_ref, dres_ref, w_ref,
             cos_ref, sa_ref, sb_ref, dz_ref, dh_ref, dg_ref):
        @pl.when(pl.program_id(0) == 0)
        def _():
            dg_ref[...] = jnp.zeros_like(dg_ref)

        na = NA_HEADS * HEAD_DIM
        dz_ref[:, 0:na] = dqa_ref[...].astype(BF16)
        dz_ref[:, na:2 * na] = dka_ref[...].astype(BF16)
        dz_ref[:, 2 * na:3 * na] = dva_ref[...].astype(BF16)
        cos, sa, sb = cos_ref[...], sa_ref[...], sb_ref[...]
        for j in range(C_QB // LANES):
            dz_ref[:, O_QB + j * LANES:O_QB + (j + 1) * LANES] = _rope(
                dqb_ref[:, j * LANES:(j + 1) * LANES], cos, sa, sb, -1.0).astype(BF16)
        dz_ref[:, O_KB:O_KB + C_KB] = _rope(dkb_ref[...], cos, sa, sb, -1.0).astype(BF16)
        dz_ref[:, O_VB:O_VB + C_KB] = dvb_ref[...].astype(BF16)
        dz_ref[:, O_G:O_G + C_GATES] = dgt_ref[...].astype(BF16)
        du = _dot(dz_ref[...], w_ref[...])
        hf = h_ref[...]
        dx, dgr = _norm_bwd(du, hf, g_ref[...], _rstd(hf))
        dg_ref[...] += jnp.sum(dgr, axis=0, keepdims=True)
        dh_ref[...] = dres_ref[...] + dx

    def tok(n):
        return pl.BlockSpec((tm, n), lambda i: (i, 0))

    vec = pl.BlockSpec((1, D), lambda i: (0, 0))
    na = NA_HEADS * HEAD_DIM
    grid = (T // tm,)
    body, r_in, r_in_specs, r_out, r_out_specs, scratch = _ride(body, 14, 3, rider, grid, None)
    outs = pl.pallas_call(
        body, name="mix_in_bwd", grid=grid,
        in_specs=[tok(na), tok(na), tok(na), tok(C_QB), tok(C_KB), tok(C_KB), tok(C_GATES), tok(D), vec, tok(D),
                  pl.BlockSpec((D_IN, D), lambda i: (0, 0), pipeline_mode=pl.Buffered(1)), tok(LANES), tok(LANES), tok(LANES)]
        + r_in_specs,
        out_specs=[tok(D_IN), tok(D), vec] + r_out_specs,
        out_shape=[jax.ShapeDtypeStruct((T, D_IN), BF16), jax.ShapeDtypeStruct((T, D), F32),
                   jax.ShapeDtypeStruct((1, D), F32)] + r_out,
        scratch_shapes=scratch,
        compiler_params=_params(1),
    )(dqa, dka, dva, dqb, dkb, dvb, dgates, h, g, dres, w_in, *tables, *r_in)
    return (*outs[:3], list(outs[3:]))


def _na_bias_slabs(rpb):
    H = rpb.shape[0]
    ncell = GRID_W * GRID_W
    cell = np.arange(ncell)
    co = cell % GRID_W - cell // GRID_W + (NA_KW - 1)
    e_co = jnp.asarray((np.arange(LANES)[:, None] == co[None, :]).astype(np.float32))
    table = jnp.pad(rpb, ((0, 0), (0, 1), (0, LANES - rpb.shape[2]))).reshape(H * 2 * NA_KH, LANES)

    def body(t_ref, e_ref, o_ref):
        o_ref[...] = jnp.dot(t_ref[...], e_ref[...], preferred_element_type=F32, precision=lax.Precision.HIGHEST)

    toeplitz = pl.pallas_call(
        body, name="rpb_unfold", out_shape=jax.ShapeDtypeStruct((H * 2 * NA_KH, ncell), F32),
        compiler_params=_params(0),
    )(table, e_co).reshape(H, 2 * NA_KH, GRID_W, GRID_W)

    def assemble(tz_ref, o_ref):
        c = lax.broadcasted_iota(jnp.int32, (GRID_W, GRID_W), 0)
        k = lax.broadcasted_iota(jnp.int32, (GRID_W, GRID_W), 1)
        cs = jnp.clip(c - NA_KW // 2, 0, GRID_W - NA_KW)
        inwin = (k >= cs) & (k < cs + NA_KW)
        for ro0 in range(NA_KH):
            for hh in range(2):
                for i in range(NA_KH):
                    o_ref[0, ro0, hh * GRID_W:(hh + 1) * GRID_W, i * GRID_W:(i + 1) * GRID_W] = jnp.where(
                        inwin, tz_ref[hh, ro0 + i], NEG)

    return pl.pallas_call(
        assemble, name="na_bias_slabs", grid=(H // 2,),
        in_specs=[pl.BlockSpec((2, 2 * NA_KH, GRID_W, GRID_W), lambda p: (p, 0, 0, 0))],
        out_specs=pl.BlockSpec((1, NA_KH, 2 * GRID_W, NA_KH * GRID_W), lambda p: (p, 0, 0, 0)),
        out_shape=jax.ShapeDtypeStruct((H // 2, NA_KH, 2 * GRID_W, NA_KH * GRID_W), F32),
        compiler_params=_params(1),
    )(toeplitz)


def _half_masks(rows):
    lane = lax.broadcasted_iota(jnp.int32, (rows, LANES), 1)
    left = lane < HEAD_DIM
    return left, (left, jnp.logical_not(left))


def _stack_heads(x):
    left, halves = _half_masks(x.shape[0])
    xf = x.astype(F32)
    return jnp.concatenate([jnp.where(m, xf, 0.0).astype(BF16) for m in halves], axis=0)


def _unstack_heads(o):
    rows = o.shape[0] // 2
    left, _ = _half_masks(rows)
    return jnp.where(left, o[:rows], o[rows:])


def _na_row(j, t, rb, rows):
    r = j * rb + t
    rs = jnp.clip(r - NA_KH // 2, 0, rows - NA_KH)
    return pl.multiple_of(t * GRID_W, GRID_W), pl.multiple_of(rs * GRID_W, GRID_W), rs - r + (NA_KH - 1)


def _na_specs(T, rb):
    qrows = GRID_W * rb
    pairs = NA_HEADS // 2
    return ([pl.BlockSpec((qrows, LANES), lambda p, j: (j, p)),
             pl.BlockSpec((T, LANES), lambda p, j: (0, pairs + p)),
             pl.BlockSpec((T, LANES), lambda p, j: (0, 2 * pairs + p))],
            pl.BlockSpec((1, NA_KH, 2 * GRID_W, NA_KH * GRID_W), lambda p, j: (p, 0, 0, 0)))


def _softmax(s):
    p = jnp.exp(s - jnp.max(s, axis=-1, keepdims=True))
    return p / jnp.sum(p, axis=-1, keepdims=True)


def _na_fwd(qkva, bias, *, rb=32, group=32):
    T = qkva.shape[0]
    rows = T // GRID_W
    nkeys = NA_KH * GRID_W
    rb = min(rb, rows)
    group = min(group, rb)

    def body(q_ref, k_ref, v_ref, bias_ref, y_ref):
        j = pl.program_id(1)

        def rows_step(t, carry):
            at = [_na_row(j, t * group + u, rb, rows) for u in range(group)]
            s = [_dot_nt(_stack_heads(q_ref[pl.ds(q0, GRID_W), :]), k_ref[pl.ds(k0, nkeys), :]) for q0, k0, _ in at]
            p = [_softmax(su * QK_SCALE + bias_ref[0, ro0]) for su, (_, _, ro0) in zip(s, at)]
            o = [_dot(pu.astype(BF16), v_ref[pl.ds(k0, nkeys), :]) for pu, (_, k0, _) in zip(p, at)]
            for ou, (q0, _, _) in zip(o, at):
                y_ref[pl.ds(q0, GRID_W), :] = _unstack_heads(ou).astype(BF16)
            return carry

        lax.fori_loop(0, rb // group, rows_step, 0)

    qkv_specs, bias_spec = _na_specs(T, rb)
    return pl.pallas_call(
        body, name="na_fwd", grid=(NA_HEADS // 2, rows // rb),
        in_specs=qkv_specs + [bias_spec],
        out_specs=qkv_specs[0],
        out_shape=jax.ShapeDtypeStruct((T, NA_HEADS * HEAD_DIM), BF16),
        compiler_params=_params(2),
    )(qkva, qkva, qkva, bias)


def _na_bwd(qkva, dy, bias, *, rb=16, group=16, rider=None):
    T = qkva.shape[0]
    rows = T // GRID_W
    nkeys = NA_KH * GRID_W
    rb = min(rb, rows)
    group = min(group, rb)

    def body(q_ref, k_ref, v_ref, dy_ref, bias_ref, dq_ref, dk_ref, dv_ref, dbias_ref):
        j = pl.program_id(1)

        @pl.when(j == 0)
        def _():
            dk_ref[...] = jnp.zeros_like(dk_ref)
            dv_ref[...] = jnp.zeros_like(dv_ref)
            dbias_ref[...] = jnp.zeros_like(dbias_ref)

        def rows_step(t, carry):
            at = [_na_row(j, t * group + u, rb, rows) for u in range(group)]
            qs = [_stack_heads(q_ref[pl.ds(q0, GRID_W), :]) for q0, _, _ in at]
            dys = [_stack_heads(dy_ref[pl.ds(q0, GRID_W), :]) for q0, _, _ in at]
            s = [_dot_nt(qu, k_ref[pl.ds(k0, nkeys), :]) for qu, (_, k0, _) in zip(qs, at)]
            dp = [_dot_nt(du, v_ref[pl.ds(k0, nkeys), :]) for du, (_, k0, _) in zip(dys, at)]
            p = [_softmax(su * QK_SCALE + bias_ref[0, ro0]) for su, (_, _, ro0) in zip(s, at)]
            ds = [pu * (du - jnp.sum(pu * du, axis=-1, keepdims=True)) for pu, du in zip(p, dp)]
            for u, (q0, k0, ro0) in enumerate(at):
                dbias_ref[0, ro0] += ds[u]
                dsb = ds[u].astype(BF16)
                dq_ref[pl.ds(q0, GRID_W), :] = (_unstack_heads(_dot(dsb, k_ref[pl.ds(k0, nkeys), :])) * QK_SCALE).astype(BF16)
                dk_ref[pl.ds(k0, nkeys), :] += _dot_tn(dsb, qs[u]) * QK_SCALE
                dv_ref[pl.ds(k0, nkeys), :] += _dot_tn(p[u].astype(BF16), dys[u])
            return carry

        lax.fori_loop(0, rb // group, rows_step, 0)

    qkv_specs, bias_spec = _na_specs(T, rb)
    width = NA_HEADS * HEAD_DIM
    kv_out = pl.BlockSpec((T, LANES), lambda p, j: (0, p))
    grid = (NA_HEADS // 2, rows // rb)
    body, r_in, r_in_specs, r_out, r_out_specs, scratch = _ride(body, 5, 4, rider, grid, None)
    outs = pl.pallas_call(
        body, name="na_bwd", grid=grid,
        in_specs=qkv_specs + [qkv_specs[0], bias_spec] + r_in_specs,
        out_specs=[qkv_specs[0], kv_out, kv_out, bias_spec] + r_out_specs,
        out_shape=[jax.ShapeDtypeStruct((T, width), BF16), jax.ShapeDtypeStruct((T, width), F32),
                   jax.ShapeDtypeStruct((T, width), F32), jax.ShapeDtypeStruct(bias.shape, F32)] + r_out,
        scratch_shapes=scratch,
        compiler_params=_params(2),
    )(qkva, qkva, qkva, dy, bias, *r_in)
    return (*outs[:4], list(outs[4:]))


def _rpb_fold(dslab):
    pairs = dslab.shape[0]
    H = 2 * pairs
    ncell = GRID_W * GRID_W

    def disassemble(d_ref, tz_ref):
        tz_ref[...] = jnp.zeros_like(tz_ref)
        for ro0 in range(NA_KH):
            for hh in range(2):
                for i in range(NA_KH):
                    tz_ref[hh, ro0 + i] += d_ref[0, ro0, hh * GRID_W:(hh + 1) * GRID_W, i * GRID_W:(i + 1) * GRID_W]

    dtoeplitz = pl.pallas_call(
        disassemble, name="rpb_fold_tiles", grid=(pairs,),
        in_specs=[pl.BlockSpec((1, NA_KH, 2 * GRID_W, NA_KH * GRID_W), lambda p: (p, 0, 0, 0))],
        out_specs=pl.BlockSpec((2, 2 * NA_KH, GRID_W, GRID_W), lambda p: (p, 0, 0, 0)),
        out_shape=jax.ShapeDtypeStruct((H, 2 * NA_KH, GRID_W, GRID_W), F32),
        compiler_params=_params(1),
    )(dslab).reshape(H * 2 * NA_KH, ncell)
    cell = np.arange(ncell)
    co = cell % GRID_W - cell // GRID_W + (NA_KW - 1)
    e_co = jnp.asarray((co[:, None] == np.arange(LANES)[None, :]).astype(np.float32))

    def diagonals(x_ref, e_ref, o_ref):
        o_ref[...] = jnp.dot(x_ref[...], e_ref[...], preferred_element_type=F32, precision=lax.Precision.HIGHEST)

    return pl.pallas_call(
        diagonals, name="rpb_fold", out_shape=jax.ShapeDtypeStruct((H * 2 * NA_KH, LANES), F32),
        compiler_params=_params(0),
    )(dtoeplitz, e_co).reshape(H, 2 * NA_KH, LANES)


SWA_KEYS = 3 * WIN


def _swa_block(j, t, qbn, T):
    blk = j * qbn + t
    start = jnp.clip((blk - 1) * WIN, 0, T - SWA_KEYS)
    row = lax.broadcasted_iota(jnp.int32, (2 * WIN, SWA_KEYS), 0)
    qpos = blk * WIN + jnp.where(row < WIN, row, row - WIN)
    kpos = start + lax.broadcasted_iota(jnp.int32, (2 * WIN, SWA_KEYS), 1)
    return pl.multiple_of(t * WIN, WIN), pl.multiple_of(start, WIN), jnp.abs(qpos - kpos) <= WIN


def _swa_sinks(sink_ref, p):
    row = lax.broadcasted_iota(jnp.int32, (2 * WIN, 1), 0)
    return jnp.where(row < WIN, sink_ref[p], sink_ref[p + NB_HEADS // 2])


def _swa_probs(s, mask, sink):
    s = jnp.where(mask, s * QK_SCALE, NEG)
    m = jnp.maximum(jnp.max(s, axis=-1, keepdims=True), sink)
    e = jnp.exp(s - m)
    esink = jnp.exp(sink - m)
    den = jnp.sum(e, axis=-1, keepdims=True) + esink
    return e / den, esink / den


def _swa_specs(T, qbn):
    return [pl.BlockSpec(memory_space=pltpu.SMEM),
            pl.BlockSpec((WIN * qbn, LANES), lambda p, j: (j, p)),
            pl.BlockSpec((T, LANES), lambda p, j: (0, 0)),
            pl.BlockSpec((T, LANES), lambda p, j: (0, 1))]


def _swa_fwd(qb, kvb, sink, *, qbn=32, group=32):
    T = qb.shape[0]
    pairs = NB_HEADS // 2
    qbn = min(qbn, T // WIN)
    group = min(group, qbn)

    def body(sink_ref, q_ref, k_ref, v_ref, y_ref):
        p, j = pl.program_id(0), pl.program_id(1)
        sinks = _swa_sinks(sink_ref, p)

        def blocks_step(t, carry):
            at = [_swa_block(j, t * group + u, qbn, T) for u in range(group)]
            s = [_dot_nt(_stack_heads(q_ref[pl.ds(q0, WIN), :]), k_ref[pl.ds(k0, SWA_KEYS), :]) for q0, k0, _ in at]
            pr = [_swa_probs(su, mask, sinks)[0] for su, (_, _, mask) in zip(s, at)]
            o = [_dot(pu.astype(BF16), v_ref[pl.ds(k0, SWA_KEYS), :]) for pu, (_, k0, _) in zip(pr, at)]
            for ou, (q0, _, _) in zip(o, at):
                y_ref[pl.ds(q0, WIN), :] = _unstack_heads(ou).astype(BF16)
            return carry

        lax.fori_loop(0, qbn // group, blocks_step, 0)

    specs = _swa_specs(T, qbn)
    return pl.pallas_call(
        body, name="swa_fwd", grid=(pairs, T // (WIN * qbn)),
        in_specs=specs, out_specs=specs[1],
        out_shape=jax.ShapeDtypeStruct((T, NB_HEADS * HEAD_DIM), BF16),
        compiler_params=_params(2),
    )(sink, qb, kvb, kvb)


def _swa_bwd(qb, kvb, dy, sink, *, qbn=16, group=16, rider=None):
    T = qb.shape[0]
    pairs = NB_HEADS // 2
    qbn = min(qbn, T // WIN)
    group = min(group, qbn)

    def body(sink_ref, q_ref, k_ref, v_ref, dy_ref, dq_ref, dk_ref, dv_ref, dsink_ref):
        p, j = pl.program_id(0), pl.program_id(1)
        sinks = _swa_sinks(sink_ref, p)

        @pl.when((p == 0) & (j == 0))
        def _():
            dk_ref[...] = jnp.zeros_like(dk_ref)
            dv_ref[...] = jnp.zeros_like(dv_ref)

        @pl.when(j == 0)
        def _():
            dsink_ref[...] = jnp.zeros_like(dsink_ref)

        def blocks_step(t, carry):
            at = [_swa_block(j, t * group + u, qbn, T) for u in range(group)]
            qs = [_stack_heads(q_ref[pl.ds(q0, WIN), :]) for q0, _, _ in at]
            dys = [_stack_heads(dy_ref[pl.ds(q0, WIN), :]) for q0, _, _ in at]
            s = [_dot_nt(qu, k_ref[pl.ds(k0, SWA_KEYS), :]) for qu, (_, k0, _) in zip(qs, at)]
            dp = [_dot_nt(du, v_ref[pl.ds(k0, SWA_KEYS), :]) for du, (_, k0, _) in zip(dys, at)]
            probs = [_swa_probs(su, mask, sinks) for su, (_, _, mask) in zip(s, at)]
            for u, (q0, k0, _) in enumerate(at):
                pr, psink = probs[u]
                delta = jnp.sum(pr * dp[u], axis=-1, keepdims=True)
                dsb = (pr * (dp[u] - delta)).astype(BF16)
                dsk = psink * delta
                for hh in range(2):
                    dsink_ref[0, hh:hh + 1, :] += jnp.broadcast_to(-jnp.sum(dsk[hh * WIN:(hh + 1) * WIN]), (1, LANES))
                dq_ref[pl.ds(q0, WIN), :] = _unstack_heads(_dot(dsb, k_ref[pl.ds(k0, SWA_KEYS), :])) * QK_SCALE
                dk_ref[pl.ds(k0, SWA_KEYS), :] += _dot_tn(dsb, qs[u]) * QK_SCALE
                dv_ref[pl.ds(k0, SWA_KEYS), :] += _dot_tn(pr.astype(BF16), dys[u])
            return carry

        lax.fori_loop(0, qbn // group, blocks_step, 0)

    specs = _swa_specs(T, qbn)
    kv_out = pl.BlockSpec((T, LANES), lambda p, j: (0, 0))
    grid = (pairs, T // (WIN * qbn))
    body, r_in, r_in_specs, r_out, r_out_specs, scratch = _ride(body, 5, 4, rider, grid, None)
    outs = pl.pallas_call(
        body, name="swa_bwd", grid=grid,
        in_specs=specs + [specs[1]] + r_in_specs,
        out_specs=[specs[1], kv_out, kv_out, pl.BlockSpec((1, 8, LANES), lambda p, j: (p, 0, 0))] + r_out_specs,
        out_shape=[jax.ShapeDtypeStruct((T, NB_HEADS * HEAD_DIM), F32), jax.ShapeDtypeStruct((T, LANES), F32),
                   jax.ShapeDtypeStruct((T, LANES), F32), jax.ShapeDtypeStruct((pairs, 8, LANES), F32)] + r_out,
        scratch_shapes=scratch,
        compiler_params=_params(2),
    )(sink, qb, kvb, kvb, dy, *r_in)
    return (*outs[:4], list(outs[4:]))


def _merge_fwd(ya, yb, gates, wa, wb, wout, h, *, tm=512):
    T, D = h.shape
    W = ya.shape[1]

    def body(ya_ref, yb_ref, gt_ref, wa_ref, wb_ref, wo_ref, h_ref, h2_ref, mg_ref):
        pa = _dot(ya_ref[...], wa_ref[...])
        pb = _dot(yb_ref[...], wb_ref[...])
        mg = (jax.nn.sigmoid(gt_ref[:, 0:D]) * pa + jax.nn.sigmoid(gt_ref[:, D:2 * D]) * pb).astype(BF16)
        mg_ref[...] = mg
        h2_ref[...] = h_ref[...] + _dot(mg, wo_ref[...])

    def tok(n):
        return pl.BlockSpec((tm, n), lambda i: (i, 0))

    def full(r, c):
        return pl.BlockSpec((r, c), lambda i: (0, 0))

    return pl.pallas_call(
        body, name="merge_fwd", grid=(T // tm,),
        in_specs=[tok(W), tok(W), tok(2 * D), full(W, D), full(W, D), full(D, D), tok(D)],
        out_specs=[tok(D), tok(D)],
        out_shape=[jax.ShapeDtypeStruct((T, D), F32), jax.ShapeDtypeStruct((T, D), BF16)],
        compiler_params=_params(1),
    )(ya, yb, gates, wa, wb, wout, h)


def _merge_bwd(dh, ya, yb, gates, wa, wb, wout, *, tm=512, rider=None):
    T, D = dh.shape
    W = ya.shape[1]

    def body(dh_ref, ya_ref, yb_ref, gt_ref, wa_ref, wb_ref, wo_ref, dya_ref, dyb_ref, dpa_ref, dpb_ref, dgt_ref):
        dmg = _dot_nt(dh_ref[...].astype(BF16), wo_ref[...])
        for y_ref, w_ref, dy_ref, dp_ref, lo in ((ya_ref, wa_ref, dya_ref, dpa_ref, 0), (yb_ref, wb_ref, dyb_ref, dpb_ref, D)):
            sg = jax.nn.sigmoid(gt_ref[:, lo:lo + D])
            dp = (dmg * sg).astype(BF16)
            dp_ref[...] = dp
            dgt_ref[:, lo:lo + D] = (dmg * _dot(y_ref[...], w_ref[...]) * (sg * (1.0 - sg))).astype(BF16)
            dy_ref[...] = _dot_nt(dp, w_ref[...]).astype(BF16)

    def tok(n):
        return pl.BlockSpec((tm, n), lambda i: (i, 0))

    def full(r, c):
        return pl.BlockSpec((r, c), lambda i: (0, 0))

    grid = (T // tm,)
    body, r_in, r_in_specs, r_out, r_out_specs, scratch = _ride(body, 7, 5, rider, grid, None)
    outs = pl.pallas_call(
        body, name="merge_bwd", grid=grid,
        in_specs=[tok(D), tok(W), tok(W), tok(2 * D), full(W, D), full(W, D), full(D, D)] + r_in_specs,
        out_specs=[tok(W), tok(W), tok(D), tok(D), tok(2 * D)] + r_out_specs,
        out_shape=[jax.ShapeDtypeStruct((T, W), BF16), jax.ShapeDtypeStruct((T, W), BF16),
                   jax.ShapeDtypeStruct((T, D), BF16), jax.ShapeDtypeStruct((T, D), BF16),
                   jax.ShapeDtypeStruct((T, 2 * D), BF16)] + r_out,
        scratch_shapes=scratch,
        compiler_params=_params(1),
    )(dh, ya, yb, gates, wa, wb, wout, *r_in)
    return (*outs[:5], list(outs[5:]))


def _pair_heads(a, axis):
    shp = a.shape
    a = a.reshape(shp[:axis] + (2, NB_HEADS // 2, HEAD_DIM) + shp[axis + 1:])
    return jnp.swapaxes(a, axis, axis + 1).reshape(shp)


def _unpair_heads(a, axis):
    shp = a.shape
    a = a.reshape(shp[:axis] + (NB_HEADS // 2, 2, HEAD_DIM) + shp[axis + 1:])
    return jnp.swapaxes(a, axis, axis + 1).reshape(shp)


FFN1 = ("ffn1_w_gate", "ffn1_w_up", "ffn1_w_down")
FFN2 = ("ffn2_w_gate", "ffn2_w_up", "ffn2_w_down")
MIXER = ("w_in", "w_branch_a", "w_branch_b", "w_out")
BRANCH = MIXER[1:]


def _layer_grads(x, target, g1, f1, gmix, late, rpb, sink, g2, gfin, comm=None):
    T = x.shape[0]
    tables = _rope_tables(T)
    bias = _na_bias_slabs(rpb)

    comm = comm or _Local(late)
    h1, n1, hdn1, p1, q1, gathered = _ffn_fwd(x, g1, *f1, name="ffn1_fwd", rider=comm.late_rider)
    w_in_t, wa, wb, wout, f2 = comm.late(gathered)
    w_in_p = jnp.concatenate([w_in_t[:O_QB], _pair_heads(w_in_t[O_QB:O_KB], 0), w_in_t[O_KB:]], axis=0)
    wb_p = _pair_heads(wb, 0)
    u, qkva, qb, kvb, gates = _mix_in_fwd(h1, gmix, w_in_p, tables)
    ya = _na_fwd(qkva, bias)
    yb = _swa_fwd(qb, kvb, sink)
    h2, merged = _merge_fwd(ya, yb, gates, wa, wb_p, wout, h1)
    dh3, n2, hdn2, p2, q2, loss, dgfin, _ = _ffn_fwd(h2, g2, *f2, name="ffn2_fwd", head=(gfin, target))

    dh2, da2, db2, dg2, _ = _ffn_bwd(dh3, h2, g2, p2, q2, *f2, name="ffn2_bwd")
    df2 = [_wgrad_shard_a(da2, n2, name="ffn2_dwg")[0], _wgrad_shard_a(db2, n2, name="ffn2_dwu")[0],
           _wgrad_shard_a(hdn2, dh3, scale=0.5, name="ffn2_dwd")[0]]
    red2 = comm.reduce(FFN2, df2, tag="ffn2")
    dya, dyb, dpa, dpb, dgates, _ = _merge_bwd(dh2, ya, yb, gates, wa, wb_p, wout)
    dwout = _wgrad_cols(merged, dh2, 1, name="dwout").reshape(N_CHIPS, D_MODEL // N_CHIPS, D_MODEL)
    dwa = _wgrad_cols(ya, dpa, N_CHIPS, name="dwa")
    dwb = _unpair_heads(_wgrad_cols(yb, dpb, N_CHIPS, name="dwb"), 1)
    redb = comm.reduce(BRANCH, [dwa, dwb, dwout], tag="branch")
    dqa, dka, dva, dbias, got = _na_bwd(qkva, dya, bias, rider=_two_riders(red2.sibling, redb.sibling))
    red2.partial(got[:len(FFN2)])
    redb.partial(got[len(FFN2):])
    drpb = _rpb_fold(dbias)
    dqb, dkb, dvb, dsink, got = _swa_bwd(qb, kvb, dyb, sink, rider=_two_riders(red2.chips, redb.chips))
    red2.halves(got[:len(FFN2)])
    redb.halves(got[len(FFN2):])
    dz, dh1, dgmix, got = _mix_in_bwd(dqa, dka, dva, dqb, dkb, dvb, dgates, h1, gmix, dh2, w_in_p, tables,
                                      rider=_two_riders(red2.share, redb.share))
    out = red2.result(got[:len(FFN2)])
    out.update(redb.result(got[len(FFN2):]))
    dwin_p = _wgrad_rows(dz, u, 2, name="dwin")[0].reshape(D_IN, D_MODEL)
    dwin = jnp.concatenate([dwin_p[:O_QB], _unpair_heads(dwin_p[O_QB:O_KB], 0), dwin_p[O_KB:]], axis=0)
    dx, da1, db1, dg1, _ = _ffn_bwd(dh1, x, g1, p1, q1, *f1, name="ffn1_bwd")
    small = dict(loss=loss, ffn1_norm=dg1, mix_norm=dgmix, ffn2_norm=dg2, final_norm=dgfin, na_rpb=drpb,
                 sink_logit=dsink[:, 0:2, 0].T.reshape(NB_HEADS))
    redw = comm.reduce(("w_in",), [dwin.reshape(N_CHIPS, D_IN // N_CHIPS, D_MODEL)], tag="w_in")
    dwg1, got = _wgrad_shard_a(da1, n1, name="ffn1_dwg", rider=_two_riders(redw.sibling, comm.small(small)))
    redw.partial(got[:1])
    comm.small_done(got[1:])
    dwu1, got = _wgrad_shard_a(db1, n1, name="ffn1_dwu", rider=redw.chips)
    redw.halves(got)
    red1 = comm.reduce(FFN1[:2], [dwg1, dwu1], tag="ffn1_gate_up").partial_now()
    dwd1, got = _wgrad_shard_a(hdn1, dh1, scale=0.5, name="ffn1_dwd", rider=_two_riders(red1.chips, redw.share))
    out.update(redw.result(got[2:]))
    out.update(red1.halves(got[:2]).result_now())
    out.update(comm.reduce(FFN1[2:], [dwd1], tag="ffn1_down").partial_now().halves_now().result_now())
    for names in (FFN2, ("w_in",), BRANCH[:2], BRANCH[2:], FFN1[:2], FFN1[2:]):
        comm.update(names, out)
    out.update(small, dx=dx)
    return out


class _Local:
    late_rider = None

    def __init__(self, late):
        self._late = late

    def late(self, gathered):
        return self._late

    def reduce(self, names, grads, *, tag):
        return _LocalReduce(names, grads)

    def update(self, names, reduced):
        pass

    def small(self, grads):
        return None

    def small_done(self, got):
        pass


class _LocalReduce:
    sibling = chips = share = None

    def __init__(self, names, grads):
        self._result = dict(zip(names, grads))

    def partial(self, got=None):
        return self

    halves = partial_now = halves_now = partial

    def result(self, got=None):
        return self._result

    result_now = result


ANY = pl.BlockSpec(memory_space=pl.ANY)


def _place():
    x, y, c = lax.axis_index("x"), lax.axis_index("y"), lax.axis_index("c")
    chips = [(1 - x, y), (x, 1 - y), (1 - x, 1 - y)]
    return x, y, c, 2 * x + y, chips


def _remote(src, dst, send_sems, recv_sems, k, device):
    return pltpu.make_async_remote_copy(src_ref=src, dst_ref=dst, send_sem=send_sems.at[k], recv_sem=recv_sems.at[k],
                                        device_id=device, device_id_type=MESH)


class _Rider:
    def __init__(self, inputs, out_shape, scratch, start, middle, finish):
        self.inputs, self.out_shape, self.scratch = list(inputs), list(out_shape), list(scratch)
        self.start, self.middle, self.finish = start, middle, finish


def _two_riders(first, second):
    if first is None or second is None:
        return first or second
    assert first.middle is None and second.middle is None
    n_in, n_out, n_sem = len(first.inputs), len(first.out_shape), len(first.scratch)

    def phase(name):
        def run(ins, outs, sems):
            getattr(first, name)(ins[:n_in], outs[:n_out], sems[:n_sem])
            getattr(second, name)(ins[n_in:], outs[n_out:], sems[n_sem:])
        return run

    return _Rider(first.inputs + second.inputs, first.out_shape + second.out_shape, first.scratch + second.scratch,
                  phase("start"), None, phase("finish"))


def _run_rider(rider, *, name):
    n_in, n_out = len(rider.inputs), len(rider.out_shape)

    def body(*refs):
        ins, outs, sems = refs[:n_in], refs[n_in:n_in + n_out], refs[n_in + n_out:]
        rider.start(ins, outs, sems)
        if rider.middle is not None:
            rider.middle(ins, outs, sems)
        rider.finish(ins, outs, sems)

    return pl.pallas_call(body, name=name, in_specs=[ANY] * n_in, out_specs=[ANY] * n_out, out_shape=rider.out_shape,
                          scratch_shapes=rider.scratch)(*rider.inputs)


def _ride(body, n_in, n_out, rider, grid, middle_step):
    if rider is None:
        return body, [], [], [], [], []
    r_in, r_out = len(rider.inputs), len(rider.out_shape)
    steps = math.prod(grid)

    def riding(*refs):
        ins, r_ins = refs[:n_in], refs[n_in:n_in + r_in]
        outs = refs[n_in + r_in:n_in + r_in + n_out]
        r_outs = refs[n_in + r_in + n_out:n_in + r_in + n_out + r_out]
        sems = refs[n_in + r_in + n_out + r_out:]
        step = pl.program_id(0)
        for axis in range(1, len(grid)):
            step = step * grid[axis] + pl.program_id(axis)

        @pl.when(step == 0)
        def _():
            rider.start(r_ins, r_outs, sems)

        body(*ins, *outs)

        if rider.middle is not None:
            @pl.when(step == middle_step)
            def _():
                rider.middle(r_ins, r_outs, sems)

        @pl.when(step == steps - 1)
        def _():
            rider.finish(r_ins, r_outs, sems)

    return riding, rider.inputs, [ANY] * r_in, rider.out_shape, [ANY] * r_out, rider.scratch


def _gather_rider(shards):
    n = len(shards)

    def plan(ins, outs, sems, kinds):
        send_sems, recv_sems, own_send_sems, own_recv_sems = sems
        x, y, c, mine, chips = _place()
        sibling = (x, y, 1 - c)
        made = {k: [] for k in kinds}
        for i in range(n):
            hr = shards[i].shape[0] // 2
            if "own" in made:
                made["own"].append(_remote(ins[i], outs[i].at[mine], own_send_sems, own_recv_sems, i, sibling))
            for j, (cx, cy) in enumerate(chips):
                here = outs[i].at[2 * cx + cy, pl.ds(c * hr, hr)]
                there = outs[i].at[2 * cx + cy, pl.ds((1 - c) * hr, hr)]
                if "sends" in made:
                    made["sends"].append(_remote(ins[i].at[pl.ds(c * hr, hr)], outs[i].at[mine, pl.ds(c * hr, hr)],
                                                 send_sems, recv_sems, 6 * i + j, (cx, cy, c)))
                if "landed" in made:
                    made["landed"].append(_remote(here, here, send_sems, recv_sems, 6 * i + j, (cx, cy, c)))
                if "passes" in made:
                    made["passes"].append(_remote(here, here, send_sems, recv_sems, 6 * i + 3 + j, sibling))
                if "others" in made:
                    made["others"].append(_remote(there, there, send_sems, recv_sems, 6 * i + 3 + j, sibling))
        return [made[k] for k in kinds]

    def start(ins, outs, sems):
        own, sends = plan(ins, outs, sems, ("own", "sends"))
        for cp in own + sends:
            cp.start()

    def middle(ins, outs, sems):
        landed, passes = plan(ins, outs, sems, ("landed", "passes"))
        for arrived, cp in zip(landed, passes):
            arrived.wait_recv()
            cp.start()

    def finish(ins, outs, sems):
        own, sends, passes, others = plan(ins, outs, sems, ("own", "sends", "passes", "others"))
        for arrived in others:
            arrived.wait_recv()
        for cp in sends + passes:
            cp.wait_send()
        for cp in own:
            cp.wait()

    return _Rider(shards, [jax.ShapeDtypeStruct((N_CHIPS,) + s.shape, s.dtype) for s in shards],
                  [pltpu.SemaphoreType.DMA((6 * n,)), pltpu.SemaphoreType.DMA((6 * n,)),
                   pltpu.SemaphoreType.DMA((n,)), pltpu.SemaphoreType.DMA((n,))], start, middle, finish)


def _swap_rider(arrays, out_shape, source):
    n = len(arrays)

    def plan(ins, outs, sems):
        send_sems, recv_sems = sems
        x, y, c, _, _ = _place()
        return [_remote(source(ins[i], c, i), outs[i], send_sems, recv_sems, i, (x, y, 1 - c)) for i in range(n)]

    def start(ins, outs, sems):
        for cp in plan(ins, outs, sems):
            cp.start()

    def finish(ins, outs, sems):
        for cp in plan(ins, outs, sems):
            cp.wait()

    return _Rider(arrays, out_shape, [pltpu.SemaphoreType.DMA((n,)), pltpu.SemaphoreType.DMA((n,))], start, None, finish)


def _sibling_rider(grads):
    half = [g.shape[1] // 2 for g in grads]
    return _swap_rider(grads, [jax.ShapeDtypeStruct((g.shape[0], hr, g.shape[2]), g.dtype) for g, hr in zip(grads, half)],
                       lambda ref, c, i: ref.at[:, pl.ds((1 - c) * half[i], half[i])])


def _share_rider(halves):
    return _swap_rider(halves, [jax.ShapeDtypeStruct(h.shape, h.dtype) for h in halves], lambda ref, c, i: ref)


def _chips_rider(parts):
    n = len(parts)

    def plan(ins, outs, sems):
        send_sems, recv_sems = sems
        _, _, c, _, chips = _place()
        return [_remote(ins[i].at[2 * cx + cy], outs[i].at[j], send_sems, recv_sems, 3 * i + j, (cx, cy, c))
                for i in range(n) for j, (cx, cy) in enumerate(chips)]

    def start(ins, outs, sems):
        for cp in plan(ins, outs, sems):
            cp.start()

    def finish(ins, outs, sems):
        for cp in plan(ins, outs, sems):
            cp.wait()

    return _Rider(parts, [jax.ShapeDtypeStruct((N_CHIPS - 1,) + p.shape[1:], p.dtype) for p in parts],
                  [pltpu.SemaphoreType.DMA((3 * n,)), pltpu.SemaphoreType.DMA((3 * n,))], start, None, finish)


class _Reduce:
    def __init__(self, names, grads, cidx, chip, *, tag):
        self.names, self.grads, self.cidx, self.chip, self.tag = names, grads, cidx, chip, tag
        self.sibling = _sibling_rider(grads)

    def _by_shape(self, fn, *lists):
        done, i = [], 0
        while i < len(self.names):
            j = i + 1
            while j < len(self.names) and self.grads[j].shape == self.grads[i].shape:
                j += 1
            done += fn(*[lst[i:j] for lst in lists], self.names[i])
            i = j
        return done

    def partial(self, from_sibling):
        self.from_sibling = from_sibling
        self.chips = _chips_rider(self._by_shape(
            lambda g, r, k: _add_sibling(g, r, self.cidx, name="add_sibling_" + k), self.grads, from_sibling))
        return self

    def halves(self, from_chips):
        self.mine = self._by_shape(
            lambda g, r1, r2, k: _add_chips(g, r1, r2, self.cidx, self.chip, name="add_chips_" + k),
            self.grads, self.from_sibling, from_chips)
        self.share = _share_rider(self.mine)
        return self

    def result(self, others):
        return dict(zip(self.names, zip(self.mine, others)))

    def partial_now(self):
        return self.partial(_run_rider(self.sibling, name="rs_sibling_" + self.tag))

    def halves_now(self):
        return self.halves(_run_rider(self.chips, name="rs_chips_" + self.tag))

    def result_now(self):
        return self.result(_run_rider(self.share, name="rs_share_" + self.tag))


N_DEV = 8


def _small_rider(vec):
    def plan(ins, outs, sems):
        send_sems, recv_sems = sems
        x, y, c, _, _ = _place()
        return [_remote(ins[0], outs[0].at[k - 1], send_sems, recv_sems, k - 1, (x ^ (k >> 2), y ^ ((k >> 1) & 1), c ^ (k & 1)))
                for k in range(1, N_DEV)]

    def start(ins, outs, sems):
        for cp in plan(ins, outs, sems):
            cp.start()

    def finish(ins, outs, sems):
        for cp in plan(ins, outs, sems):
            cp.wait()

    return _Rider([vec], [jax.ShapeDtypeStruct((N_DEV - 1,) + vec.shape, vec.dtype)],
                  [pltpu.SemaphoreType.DMA((N_DEV - 1,)), pltpu.SemaphoreType.DMA((N_DEV - 1,))], start, None, finish)


def _small_sum(vec, others, me):
    def body(me_ref, v_ref, b_ref, o_ref):
        mine = me_ref[0]
        acc = None
        for d in range(N_DEV):
            term = jnp.where(mine == d, v_ref[...], b_ref[jnp.maximum((mine ^ d) - 1, 0)])
            acc = term if acc is None else acc + term
        o_ref[...] = acc

    vmem = pl.BlockSpec(memory_space=pltpu.VMEM)
    return pl.pallas_call(
        body, name="small_sum", in_specs=[pl.BlockSpec(memory_space=pltpu.SMEM), vmem, vmem], out_specs=vmem,
        out_shape=jax.ShapeDtypeStruct(vec.shape, vec.dtype),
    )(me, vec, others)


ELEMWISE_BLOCK = 512 * 1024


def _row_tile(rows, cols):
    best = None
    for t in range(16, rows + 1, 16):
        if rows % t == 0 and t * cols <= ELEMWISE_BLOCK:
            best = t
    return best if best is not None else rows


def _add_sibling(gs, r1s, cidx, *, name):
    n = len(gs)
    S, R, C = gs[0].shape
    hr = R // 2
    tr = _row_tile(hr, C)
    nt = hr // tr

    def body(c_ref, *refs):
        for g_ref, r_ref, o_ref in zip(refs[:n], refs[n:2 * n], refs[2 * n:]):
            o_ref[...] = (g_ref[...] + r_ref[...]).astype(BF16)

    blk = pl.BlockSpec((1, tr, C), lambda s, t, c: (s, t, 0))
    mine = pl.BlockSpec((1, tr, C), lambda s, t, c: (s, c[0] * nt + t, 0))
    return list(pl.pallas_call(
        body, name=name,
        grid_spec=pltpu.PrefetchScalarGridSpec(
            num_scalar_prefetch=1, grid=(S, nt), in_specs=[mine] * n + [blk] * n, out_specs=[blk] * n),
        out_shape=[jax.ShapeDtypeStruct((S, hr, C), BF16)] * n,
        compiler_params=_params(2),
    )(cidx, *gs, *r1s))


def _add_chips(gs, r1s, r2s, cidx, chip, *, name):
    n = len(gs)
    _, R, C = gs[0].shape
    hr = R // 2
    tr = _row_tile(hr, C)
    nt = hr // tr

    def body(pos_ref, *refs):
        for g_ref, r1_ref, r2_ref, o_ref in zip(refs[:n], refs[n:2 * n], refs[2 * n:3 * n], refs[3 * n:]):
            own = g_ref[0] + r1_ref[0]
            o_ref[...] = ((own + r2_ref[0].astype(F32)) + r2_ref[1].astype(F32)) + r2_ref[2].astype(F32)

    pos = jnp.concatenate([cidx, chip])
    return list(pl.pallas_call(
        body, name=name,
        grid_spec=pltpu.PrefetchScalarGridSpec(
            num_scalar_prefetch=1, grid=(nt,),
            in_specs=[pl.BlockSpec((1, tr, C), lambda t, pos: (pos[1], pos[0] * nt + t, 0))] * n
            + [pl.BlockSpec((1, tr, C), lambda t, pos: (pos[1], t, 0))] * n
            + [pl.BlockSpec((N_CHIPS - 1, tr, C), lambda t, pos: (0, t, 0))] * n,
            out_specs=[pl.BlockSpec((tr, C), lambda t, pos: (t, 0))] * n),
        out_shape=[jax.ShapeDtypeStruct((hr, C), F32)] * n,
        compiler_params=_params(1),
    )(pos, *gs, *r1s, *r2s))


def _adamw_math(w, g, m, v):
    mn = ADAM_B1 * m + (1.0 - ADAM_B1) * g
    vn = ADAM_B2 * v + (1.0 - ADAM_B2) * (g * g)
    m_hat = mn / (1.0 - ADAM_B1 ** ADAM_STEP)
    v_hat = vn / (1.0 - ADAM_B2 ** ADAM_STEP)
    return -ADAM_LR * (m_hat / (jnp.sqrt(v_hat) + ADAM_EPS) + ADAM_WD * w), mn, vn


def _adamw_halves(ws, mines, others, ms, vs, cidx, *, name):
    n = len(ws)
    R, C = ws[0].shape
    hr = R // 2
    tr = _row_tile(hr, C * min(n, 2))
    nt = hr // tr

    def body(c_ref, *refs):
        for i in range(n):
            w_ref, a_ref, b_ref, m_ref, v_ref = (refs[j * n + i] for j in range(5))
            g_ref, d_ref, mo_ref, vo_ref = (refs[(5 + j) * n + i] for j in range(4))
            gv = jnp.where(pl.program_id(0) == c_ref[0], a_ref[...], b_ref[...])
            g_ref[...] = gv
            d_ref[...], mo_ref[...], vo_ref[...] = _adamw_math(w_ref[...], gv, m_ref[...], v_ref[...])

    full = pl.BlockSpec((tr, C), lambda h, t, c: (h * nt + t, 0))
    own = pl.BlockSpec((tr, C), lambda h, t, c: (jnp.where(h == c[0], t, 0), 0))
    sib = pl.BlockSpec((tr, C), lambda h, t, c: (jnp.where(h == c[0], 0, t), 0))
    shape = jax.ShapeDtypeStruct((R, C), F32)
    outs = pl.pallas_call(
        body, name=name,
        grid_spec=pltpu.PrefetchScalarGridSpec(
            num_scalar_prefetch=1, grid=(2, nt),
            in_specs=[full] * n + [own] * n + [sib] * n + [full] * (2 * n), out_specs=[full] * (4 * n)),
        out_shape=[shape] * (4 * n),
        compiler_params=_params(2),
    )(cidx, *ws, *mines, *others, *ms, *vs)
    return [list(outs[j * n:(j + 1) * n]) for j in range(4)]


def _adamw_small(ws, gs, ms, vs):
    n = len(ws)

    def body(*refs):
        for i in range(n):
            w_ref, g_ref, m_ref, v_ref = (refs[j * n + i] for j in range(4))
            d_ref, mo_ref, vo_ref = (refs[(4 + j) * n + i] for j in range(3))
            d_ref[...], mo_ref[...], vo_ref[...] = _adamw_math(w_ref[...], g_ref[...], m_ref[...], v_ref[...])

    shapes = [jax.ShapeDtypeStruct(a.shape, F32) for a in ws]
    outs = pl.pallas_call(body, name="adamw_small", out_shape=shapes * 3, compiler_params=_params(0))(*ws, *gs, *ms, *vs)
    return outs[:n], outs[n:2 * n], outs[2 * n:]


def _unstack_cols(w):
    s, r, c = w.shape
    return w.transpose(1, 0, 2).reshape(r, s * c)


def _pad_rows(a, rows):
    return jnp.pad(a, ((0, rows - a.shape[0]), (0, LANES - a.shape[1])))


BIG = ("ffn1_w_gate", "ffn1_w_up", "ffn1_w_down", "w_in", "w_branch_a", "w_branch_b", "w_out",
       "ffn2_w_gate", "ffn2_w_up", "ffn2_w_down")
TRANSPOSED = ("ffn1_w_gate", "ffn1_w_up", "w_in", "ffn2_w_gate", "ffn2_w_up")
WEIGHTS = ("ffn1_norm", "ffn1_w_gate", "ffn1_w_up", "ffn1_w_down", "mix_norm", "w_in", "na_rpb", "sink_logit",
           "w_branch_a", "w_branch_b", "w_out", "ffn2_norm", "ffn2_w_gate", "ffn2_w_up", "ffn2_w_down", "final_norm")


def kernel(x, ffn1_norm, ffn1_w_gate, ffn1_w_up, ffn1_w_down, mix_norm, w_in, na_rpb, sink_logit, w_branch_a, w_branch_b, w_out, ffn2_norm, ffn2_w_gate, ffn2_w_up, ffn2_w_down, final_norm, loss_target, m_ffn1_norm, m_ffn1_w_gate, m_ffn1_w_up, m_ffn1_w_down, m_mix_norm, m_w_in, m_na_rpb, m_sink_logit, m_w_branch_a, m_w_branch_b, m_w_out, m_ffn2_norm, m_ffn2_w_gate, m_ffn2_w_up, m_ffn2_w_down, m_final_norm, v_ffn1_norm, v_ffn1_w_gate, v_ffn1_w_up, v_ffn1_w_down, v_mix_norm, v_w_in, v_na_rpb, v_sink_logit, v_w_branch_a, v_w_branch_b, v_w_out, v_ffn2_norm, v_ffn2_w_gate, v_ffn2_w_up, v_ffn2_w_down, v_final_norm):
    args = dict(locals())
    w = {k: args[k] for k in WEIGHTS}
    mom = {k: args["m_" + k] for k in WEIGHTS}
    var = {k: args["v_" + k] for k in WEIGHTS}
    cidx = lax.axis_index("c").astype(jnp.int32).reshape(1)
    chip = (2 * lax.axis_index("x") + lax.axis_index("y")).astype(jnp.int32).reshape(1)

    def shard(a, k):
        return jnp.swapaxes(a[0], 0, 1) if k in TRANSPOSED else a[0]

    def unshard(a, k):
        return (jnp.swapaxes(a, 0, 1) if k in TRANSPOSED else a)[None]

    def bf16_shards(names):
        return [shard(w[k], k).astype(BF16) for k in names]

    class comm:
        late_rider = _gather_rider(bf16_shards(MIXER + FFN2))

        @staticmethod
        def late(gathered):
            full = dict(zip(MIXER + FFN2, gathered))
            return (full["w_in"].reshape(D_IN, D_MODEL), _unstack_cols(full["w_branch_a"]), _unstack_cols(full["w_branch_b"]),
                    full["w_out"].reshape(D_MODEL, D_MODEL), tuple(full[k] for k in FFN2))

        @staticmethod
        def reduce(names, grads, *, tag):
            return _Reduce(names, grads, cidx, chip, tag=tag)

        @staticmethod
        def update(names, reduced):
            res = _adamw_halves([shard(w[k], k) for k in names], [reduced[k][0] for k in names],
                                [reduced[k][1] for k in names], [shard(mom[k], k) for k in names],
                                [shard(var[k], k) for k in names], cidx, name="adamw_" + names[0])
            for i, k in enumerate(names):
                grads_out[k], deltas[k], new_m[k], new_v[k] = (unshard(a[i], k) for a in res)

        @staticmethod
        def small(g):
            packed["mine"] = jnp.concatenate([
                g["ffn1_norm"].reshape(rows, LANES), g["mix_norm"].reshape(rows, LANES), g["ffn2_norm"].reshape(rows, LANES),
                g["final_norm"].reshape(rows, LANES), g["na_rpb"].reshape(-1, LANES),
                _pad_rows(g["sink_logit"].reshape(1, NB_HEADS), 8), _pad_rows(g["loss"], 8)], axis=0)
            return _small_rider(packed["mine"])

        @staticmethod
        def small_done(got):
            packed["others"], = got

    deltas, new_m, new_v, grads_out, grad, packed = {}, {}, {}, {}, {}, {}
    rows = D_MODEL // LANES
    f1 = _run_rider(_gather_rider(bf16_shards(FFN1)), name="all_gather_ffn1")
    out = _layer_grads(x[0], loss_target[0], ffn1_norm, f1, mix_norm, None, na_rpb[0], sink_logit[0], ffn2_norm,
                       final_norm.reshape(1, D_MODEL), comm=comm)

    me = (4 * lax.axis_index("x") + 2 * lax.axis_index("y") + lax.axis_index("c")).astype(jnp.int32).reshape(1)
    total = _small_sum(packed["mine"], packed["others"], me)
    n_rpb = NA_HEADS * 2 * NA_KH
    grad["ffn1_norm"] = total[0:rows].reshape(1, D_MODEL)
    grad["mix_norm"] = total[rows:2 * rows].reshape(1, D_MODEL)
    grad["ffn2_norm"] = total[2 * rows:3 * rows].reshape(1, D_MODEL)
    grad["final_norm"] = total[3 * rows:4 * rows].reshape(1, D_MODEL)
    grad["na_rpb"] = total[4 * rows:4 * rows + n_rpb].reshape(NA_HEADS, 2 * NA_KH, LANES)[:, :2 * NA_KH - 1, :2 * NA_KW - 1]
    grad["na_rpb"] = grad["na_rpb"].reshape(NA_HEADS, -1)
    grad["sink_logit"] = total[4 * rows + n_rpb:4 * rows + n_rpb + 1, 0:NB_HEADS]
    loss = total[4 * rows + n_rpb + 8, 0]

    small_names = [k for k in WEIGHTS if k not in BIG]
    res = _adamw_small(*[[a[k].reshape(grad[k].shape) for k in small_names] for a in (w, grad, mom, var)])
    for i, k in enumerate(small_names):
        grads_out[k], deltas[k], new_m[k], new_v[k] = (a.reshape(w[k].shape) for a in (grad[k], res[0][i], res[1][i], res[2][i]))
    return (loss, out["dx"].reshape(x.shape), *[grads_out[k] for k in WEIGHTS], *[deltas[k] for k in WEIGHTS],
            *[new_m[k] for k in WEIGHTS], *[new_v[k] for k in WEIGHTS])
```

```python
import math

import jax
import jax.numpy as jnp
import numpy as np
from jax import lax
from jax.experimental import pallas as pl
from jax.experimental.pallas import tpu as pltpu

F32 = jnp.float32
BF16 = jnp.bfloat16

D_MODEL = 1024
HEAD_DIM = 64
NA_HEADS = 8
NB_HEADS = 8
GRID_W = 64
NA_KH = 8
NA_KW = 16
WIN = 128
ROPE_THETA = 10000.0
EPS = 1e-6
N_CHIPS = 4
QK_SCALE = HEAD_DIM ** -0.5
NEG = -1e30
LANES = 128
VMEM_LIMIT = 56 * 1024 * 1024
HEAD_ROWS = 256
WGRAD_TOKENS_BYTES = 8192

C_QKVA = 3 * NA_HEADS * HEAD_DIM
C_QB = NB_HEADS * HEAD_DIM
C_KB = 2 * HEAD_DIM
C_ROPE = C_QB + C_KB
C_GATES = 2 * D_MODEL
D_IN = C_QKVA + C_QB + 2 * C_KB + C_GATES
O_QB = C_QKVA
O_KB = O_QB + C_QB
O_VB = O_KB + C_KB
O_G = O_VB + C_KB

ADAM_LR = 0.001
ADAM_B1 = 0.9
ADAM_B2 = 0.999
ADAM_EPS = 1e-08
ADAM_WD = 0.01
ADAM_STEP = 10

MESH = pl.DeviceIdType.MESH


def _dot(a, b):
    return jnp.dot(a, b, preferred_element_type=F32)


def _dot_nt(a, b):
    return lax.dot_general(a, b, (((1,), (1,)), ((), ())), preferred_element_type=F32)


def _dot_tn(a, b):
    return lax.dot_general(a, b, (((0,), (0,)), ((), ())), preferred_element_type=F32)


def _params(n_axes):
    return pltpu.CompilerParams(dimension_semantics=("arbitrary",) * n_axes, vmem_limit_bytes=VMEM_LIMIT)


def _rstd(xf):
    return lax.rsqrt(jnp.mean(xf * xf, axis=-1, keepdims=True) + EPS)


def _norm_bwd(dn, xf, g, r):
    xhat = xf * r
    dxh = dn * g
    dx = r * (dxh - xhat * jnp.mean(dxh * xhat, axis=-1, keepdims=True))
    return dx, dn * xhat


def _sigmoid(x):
    return 0.5 * jnp.tanh(0.5 * x) + 0.5


def _loss_head(hf, gv, tgt):
    r = _rstd(hf)
    err = (hf * r) * gv - tgt
    dx, dgr = _norm_bwd(err * (1.0 / hf.shape[-1]), hf, gv, r)
    return 0.5 * jnp.mean(err * err, axis=-1, keepdims=True), dx, dgr


def _ffn_fwd(x, g, wg, wu, wd, *, name, tm=1024, sub=512, rider=None, head=None):
    T, D = x.shape
    F = wg.shape[1]
    tm = min(tm, T)
    sub = min(sub, tm)
    n_head = 0 if head is None else 2

    def body(*refs):
        x_ref, g_ref, wg_ref, wu_ref, wd_ref = refs[:5]
        h_ref, n_ref, hdn_ref, p_ref, q_ref = refs[5 + n_head:10 + n_head]
        i, s = pl.program_id(0), pl.program_id(1)
        _ffn_fwd_step(x_ref, g_ref, wg_ref, wu_ref, wd_ref, h_ref, n_ref, hdn_ref, p_ref, q_ref, s)
        if head is not None:
            gf_ref, t_ref = refs[5:7]
            loss_ref, dgf_ref = refs[10 + n_head:]

            @pl.when((i == 0) & (s == 0))
            def _():
                loss_ref[...] = jnp.zeros_like(loss_ref)
                dgf_ref[...] = jnp.zeros_like(dgf_ref)

            @pl.when(s == N_CHIPS - 1)
            def _():
                for u in range(tm // HEAD_ROWS):
                    r = pl.ds(u * HEAD_ROWS, HEAD_ROWS)
                    terms, dh, dgr = _loss_head(h_ref[r, :], gf_ref[...], t_ref[r, :])
                    loss_ref[...] += jnp.broadcast_to(jnp.sum(terms), loss_ref.shape)
                    dgf_ref[...] += jnp.sum(dgr, axis=0, keepdims=True)
                    h_ref[r, :] = dh

    def _ffn_fwd_step(x_ref, g_ref, wg_ref, wu_ref, wd_ref, h_ref, n_ref, hdn_ref, p_ref, q_ref, s):

        @pl.when(s == 0)
        def _():
            xf = x_ref[...]
            n_ref[...] = ((xf * _rstd(xf)) * g_ref[...]).astype(BF16)
            h_ref[...] = xf

        rows = [pl.ds(u * sub, sub) for u in range(tm // sub)]
        ab = [(_dot_nt(n_ref[r, :], wg_ref[0]), _dot_nt(n_ref[r, :], wu_ref[0])) for r in rows]
        hdns = []
        for r, (a, b) in zip(rows, ab):
            sg = _sigmoid(a)
            silu = a * sg
            hdn = (silu * b).astype(BF16)
            hdn_ref[0, r, :] = hdn
            p_ref[0, r, :] = (b * (sg + silu * (1.0 - sg))).astype(BF16)
            q_ref[0, r, :] = silu.astype(BF16)
            hdns.append(hdn)
        for r, hdn in zip(rows, hdns):
            h_ref[r, :] += 0.5 * _dot(hdn, wd_ref[0])

    tok = pl.BlockSpec((tm, D), lambda i, s: (i, 0))
    hid = pl.BlockSpec((1, tm, F), lambda i, s: (s, i, 0))
    wspec = pl.BlockSpec((1, F, D), lambda i, s: (s, 0, 0))
    hshape = jax.ShapeDtypeStruct((N_CHIPS, T, F), BF16)
    grid = (T // tm, N_CHIPS)
    vec = pl.BlockSpec((1, D), lambda i, s: (0, 0))
    head_in, head_in_specs, head_out, head_out_specs = [], [], [], []
    if head is not None:
        head_in, head_in_specs = list(head), [vec, tok]
        head_out = [jax.ShapeDtypeStruct((1, LANES), F32), jax.ShapeDtypeStruct((1, D), F32)]
        head_out_specs = [pl.BlockSpec((1, LANES), lambda i, s: (0, 0)), vec]
    n_main = 5 + n_head
    body, r_in, r_in_specs, r_out, r_out_specs, scratch = _ride(body, n_main, n_main, rider, grid, (grid[0] * grid[1] * 7) // 8)
    outs = pl.pallas_call(
        body, name=name, grid=grid,
        in_specs=[tok, vec, wspec, wspec, wspec] + head_in_specs + r_in_specs,
        out_specs=[tok, tok, hid, hid, hid] + head_out_specs + r_out_specs,
        out_shape=[jax.ShapeDtypeStruct((T, D), F32), jax.ShapeDtypeStruct((T, D), BF16), hshape, hshape, hshape]
        + head_out + r_out,
        scratch_shapes=scratch,
        compiler_params=_params(2),
    )(x, g, wg, wu, wd, *head_in, *r_in)
    return (*outs[:n_main], list(outs[n_main:]))


def _ffn_bwd(dh, x, g, p, q, wg, wu, wd, *, name, tm=1024, sub=256, rider=None):
    T, D = x.shape
    F = wg.shape[1]
    tm = min(tm, T)
    sub = min(sub, tm)

    def body(dh_ref, x_ref, g_ref, p_ref, q_ref, wg_ref, wu_ref, wd_ref, dx_ref, da_ref, db_ref, dg_ref):
        i, s = pl.program_id(0), pl.program_id(1)

        @pl.when((i == 0) & (s == 0))
        def _():
            dg_ref[...] = jnp.zeros_like(dg_ref)

        @pl.when(s == 0)
        def _():
            dx_ref[...] = jnp.zeros_like(dx_ref)

        rows = [pl.ds(u * sub, sub) for u in range(tm // sub)]
        dhdn = [_dot_nt((0.5 * dh_ref[r, :]).astype(BF16), wd_ref[0]) for r in rows]
        das, dbs = [], []
        for r, dd in zip(rows, dhdn):
            da = (dd * p_ref[0, r, :].astype(F32)).astype(BF16)
            db = (dd * q_ref[0, r, :].astype(F32)).astype(BF16)
            da_ref[0, r, :] = da
            db_ref[0, r, :] = db
            das.append(da)
            dbs.append(db)
        for r, da, db in zip(rows, das, dbs):
            dx_ref[r, :] += _dot(da, wg_ref[0]) + _dot(db, wu_ref[0])

        @pl.when(s == N_CHIPS - 1)
        def _():
            xf = x_ref[...]
            dx, dgr = _norm_bwd(dx_ref[...], xf, g_ref[...], _rstd(xf))
            dg_ref[...] += jnp.sum(dgr, axis=0, keepdims=True)
            dx_ref[...] = dh_ref[...] + dx

    tok = pl.BlockSpec((tm, D), lambda i, s: (i, 0))
    hid = pl.BlockSpec((1, tm, F), lambda i, s: (s, i, 0))
    vec = pl.BlockSpec((1, D), lambda i, s: (0, 0))
    hshape = jax.ShapeDtypeStruct((N_CHIPS, T, F), BF16)
    wspec = pl.BlockSpec((1, F, D), lambda i, s: (s, 0, 0))
    grid = (T // tm, N_CHIPS)
    body, r_in, r_in_specs, r_out, r_out_specs, scratch = _ride(body, 8, 4, rider, grid, None)
    outs = pl.pallas_call(
        body, name=name, grid=grid,
        in_specs=[tok, tok, vec, hid, hid, wspec, wspec, wspec] + r_in_specs,
        out_specs=[tok, hid, hid, vec] + r_out_specs,
        out_shape=[jax.ShapeDtypeStruct((T, D), F32), hshape, hshape, jax.ShapeDtypeStruct((1, D), F32)] + r_out,
        scratch_shapes=scratch,
        compiler_params=_params(2),
    )(dh, x, g, p, q, wg, wu, wd, *r_in)
    return (*outs[:4], list(outs[4:]))


def _wgrad(a, b, *, a_block, a_map, b_block, b_map, out_shape, o_block, o_map, grid, scale=1.0, name, rider=None):
    def body(a_ref, b_ref, o_ref):
        @pl.when(pl.program_id(len(grid) - 1) == 0)
        def _():
            o_ref[...] = jnp.zeros_like(o_ref)

        av = a_ref[...]
        bv = b_ref[...]
        av = av.reshape(av.shape[-2:]).astype(BF16)
        bv = bv.reshape(bv.shape[-2:])
        if scale != 1.0:
            bv = scale * bv
        o_ref[...] += _dot_tn(av, bv.astype(BF16)).reshape(o_ref.shape)

    body, r_in, r_in_specs, r_out, r_out_specs, scratch = _ride(body, 2, 1, rider, grid, None)
    outs = pl.pallas_call(
        body, name=name, grid=grid,
        in_specs=[pl.BlockSpec(a_block, a_map), pl.BlockSpec(b_block, b_map)] + r_in_specs,
        out_specs=[pl.BlockSpec(o_block, o_map)] + r_out_specs,
        out_shape=[jax.ShapeDtypeStruct(out_shape, F32)] + r_out,
        scratch_shapes=scratch,
        compiler_params=_params(len(grid)),
    )(a, b, *r_in)
    return outs[0], list(outs[1:])


def _wgrad_rows(a, b, n_blocks, *, name, tk=2048):
    T, N = b.shape
    M = a.shape[1] // n_blocks
    tk = min(tk, T)
    return _wgrad(a, b, a_block=(tk, M), a_map=lambda s, k: (k, s), b_block=(tk, N), b_map=lambda s, k: (k, 0),
                  out_shape=(n_blocks, M, N), o_block=(1, M, N), o_map=lambda s, k: (s, 0, 0), grid=(n_blocks, T // tk), name=name)


def _wgrad_shard_a(a, b, *, name, scale=1.0, rider=None):
    S, T, M = a.shape
    N = b.shape[1]
    tk = min(WGRAD_TOKENS_BYTES // b.dtype.itemsize, T)
    return _wgrad(a, b, a_block=(1, tk, M), a_map=lambda s, k: (s, k, 0), b_block=(tk, N), b_map=lambda s, k: (k, 0),
                  out_shape=(S, M, N), o_block=(1, M, N), o_map=lambda s, k: (s, 0, 0), grid=(S, T // tk), scale=scale,
                  name=name, rider=rider)


def _wgrad_cols(a, b, n_blocks, *, name, tk=2048):
    T, M = a.shape
    N = b.shape[1] // n_blocks
    tk = min(tk, T)

    def body(a_ref, b_ref, o_ref):
        @pl.when(pl.program_id(0) == 0)
        def _():
            o_ref[...] = jnp.zeros_like(o_ref)

        r = _dot_tn(a_ref[...].astype(BF16), b_ref[...].astype(BF16))
        for s in range(n_blocks):
            o_ref[s] += r[:, s * N:(s + 1) * N]

    return pl.pallas_call(
        body, name=name, grid=(T // tk,),
        in_specs=[pl.BlockSpec((tk, M), lambda k: (k, 0)), pl.BlockSpec((tk, n_blocks * N), lambda k: (k, 0))],
        out_specs=pl.BlockSpec((n_blocks, M, N), lambda k: (0, 0, 0)),
        out_shape=jax.ShapeDtypeStruct((n_blocks, M, N), F32),
        compiler_params=_params(1),
    )(a, b)


def _rope_tables(T):
    half = HEAD_DIM // 2
    inv = np.float32(ROPE_THETA) ** (-np.arange(half, dtype=np.float32) / np.float32(half))
    ang = np.arange(T, dtype=np.float32)[:, None] * inv[None, :]
    cos, sin, zero = np.cos(ang), np.sin(ang), np.zeros_like(ang)
    reps = LANES // HEAD_DIM
    return (jnp.asarray(np.tile(np.concatenate([cos, cos], axis=1), (1, reps))),
            jnp.asarray(np.tile(np.concatenate([-sin, zero], axis=1), (1, reps))),
            jnp.asarray(np.tile(np.concatenate([zero, sin], axis=1), (1, reps))))


def _rope(x, cos, sa, sb, sign):
    half = HEAD_DIM // 2
    return x * cos + sign * (pltpu.roll(x, LANES - half, 1) * sa + pltpu.roll(x, half, 1) * sb)


def _mix_in_fwd(h, g, w_in, tables, *, tm=512):
    T, D = h.shape

    def body(h_ref, g_ref, w_ref, cos_ref, sa_ref, sb_ref, u_ref, qkva_ref, qb_ref, kvb_ref, gates_ref):
        hf = h_ref[...]
        u = ((hf * _rstd(hf)) * g_ref[...]).astype(BF16)
        u_ref[...] = u
        qkva_ref[...] = _dot_nt(u, w_ref[0:C_QKVA, :]).astype(BF16)
        zr = _dot_nt(u, w_ref[O_QB:O_QB + C_ROPE, :])
        cos, sa, sb = cos_ref[...], sa_ref[...], sb_ref[...]
        for j in range(C_ROPE // LANES):
            rj = _rope(zr[:, j * LANES:(j + 1) * LANES], cos, sa, sb, 1.0).astype(BF16)
            if j < C_QB // LANES:
                qb_ref[:, j * LANES:(j + 1) * LANES] = rj
            else:
                kvb_ref[:, 0:C_KB] = rj
        kvb_ref[:, C_KB:2 * C_KB] = _dot_nt(u, w_ref[O_VB:O_VB + C_KB, :]).astype(BF16)
        gates_ref[...] = _dot_nt(u, w_ref[O_G:O_G + C_GATES, :])

    def tok(n):
        return pl.BlockSpec((tm, n), lambda i: (i, 0))

    return pl.pallas_call(
        body, name="mix_in_fwd", grid=(T // tm,),
        in_specs=[tok(D), pl.BlockSpec((1, D), lambda i: (0, 0)), pl.BlockSpec((D_IN, D), lambda i: (0, 0), pipeline_mode=pl.Buffered(1)),
                  tok(LANES), tok(LANES), tok(LANES)],
        out_specs=[tok(D), tok(C_QKVA), tok(C_QB), tok(2 * C_KB), tok(C_GATES)],
        out_shape=[jax.ShapeDtypeStruct((T, D), BF16), jax.ShapeDtypeStruct((T, C_QKVA), BF16),
                   jax.ShapeDtypeStruct((T, C_QB), BF16), jax.ShapeDtypeStruct((T, 2 * C_KB), BF16),
                   jax.ShapeDtypeStruct((T, C_GATES), F32)],
        compiler_params=_params(1),
    )(h, g, w_in, *tables)


def _mix_in_bwd(dqa, dka, dva, dqb, dkb, dvb, dgates, h, g, dres, w_in, tables, *, tm=512, rider=None):
    T, D = h.shape

    def body(dqa_ref, dka_ref, dva_ref, dqb_ref, dkb_ref, dvb_ref, dgt_ref, h_ref, g_ref, dres_ref, w_ref,
             cos_ref, sa_ref, sb_ref, dz_ref, dh_ref, dg_ref):
        @pl.when(pl.program_id(0) == 0)
        def _():
            dg_ref[...] = jnp.zeros_like(dg_ref)

        na = NA_HEADS * HEAD_DIM
        dz_ref[:, 0:na] = dqa_ref[...].astype(BF16)
        dz_ref[:, na:2 * na] = dka_ref[...].astype(BF16)
        dz_ref[:, 2 * na:3 * na] = dva_ref[...].astype(BF16)
        cos, sa, sb = cos_ref[...], sa_ref[...], sb_ref[...]
        for j in range(C_QB // LANES):
            dz_ref[:, O_QB + j * LANES:O_QB + (j + 1) * LANES] = _rope(
                dqb_ref[:, j * LANES:(j + 1) * LANES], cos, sa, sb, -1.0).astype(BF16)
        dz_ref[:, O_KB:O_KB + C_KB] = _rope(dkb_ref[...], cos, sa, sb, -1.0).astype(BF16)
        dz_ref[:, O_VB:O_VB + C_KB] = dvb_ref[...].astype(BF16)
        dz_ref[:, O_G:O_G + C_GATES] = dgt_ref[...].astype(BF16)
        du = _dot(dz_ref[...], w_ref[...])
        hf = h_ref[...]
        dx, dgr = _norm_bwd(du, hf, g_ref[...], _rstd(hf))
        dg_ref[...] += jnp.sum(dgr, axis=0, keepdims=True)
        dh_ref[...] = dres_ref[...] + dx

    def tok(n):
        return pl.BlockSpec((tm, n), lambda i: (i, 0))

    vec = pl.BlockSpec((1, D), lambda i: (0, 0))
    na = NA_HEADS * HEAD_DIM
    grid = (T // tm,)
    body, r_in, r_in_specs, r_out, r_out_specs, scratch = _ride(body, 14, 3, rider, grid, None)
    outs = pl.pallas_call(
        body, name="mix_in_bwd", grid=grid,
        in_specs=[tok(na), tok(na), tok(na), tok(C_QB), tok(C_KB), tok(C_KB), tok(C_GATES), tok(D), vec, tok(D),
                  pl.BlockSpec((D_IN, D), lambda i: (0, 0), pipeline_mode=pl.Buffered(1)), tok(LANES), tok(LANES), tok(LANES)]
        + r_in_specs,
        out_specs=[tok(D_IN), tok(D), vec] + r_out_specs,
        out_shape=[jax.ShapeDtypeStruct((T, D_IN), BF16), jax.ShapeDtypeStruct((T, D), F32),
                   jax.ShapeDtypeStruct((1, D), F32)] + r_out,
        scratch_shapes=scratch,
        compiler_params=_params(1),
    )(dqa, dka, dva, dqb, dkb, dvb, dgates, h, g, dres, w_in, *tables, *r_in)
    return (*outs[:3], list(outs[3:]))


def _na_bias_slabs(rpb):
    H = rpb.shape[0]
    ncell = GRID_W * GRID_W
    cell = np.arange(ncell)
    co = cell % GRID_W - cell // GRID_W + (NA_KW - 1)
    e_co = jnp.asarray((np.arange(LANES)[:, None] == co[None, :]).astype(np.float32))
    table = jnp.pad(rpb, ((0, 0), (0, 1), (0, LANES - rpb.shape[2]))).reshape(H * 2 * NA_KH, LANES)

    def body(t_ref, e_ref, o_ref):
        o_ref[...] = jnp.dot(t_ref[...], e_ref[...], preferred_element_type=F32, precision=lax.Precision.HIGHEST)

    toeplitz = pl.pallas_call(
        body, name="rpb_unfold", out_shape=jax.ShapeDtypeStruct((H * 2 * NA_KH, ncell), F32),
        compiler_params=_params(0),
    )(table, e_co).reshape(H, 2 * NA_KH, GRID_W, GRID_W)

    def assemble(tz_ref, o_ref):
        c = lax.broadcasted_iota(jnp.int32, (GRID_W, GRID_W), 0)
        k = lax.broadcasted_iota(jnp.int32, (GRID_W, GRID_W), 1)
        cs = jnp.clip(c - NA_KW // 2, 0, GRID_W - NA_KW)
        inwin = (k >= cs) & (k < cs + NA_KW)
        for ro0 in range(NA_KH):
            for hh in range(2):
                for i in range(NA_KH):
                    o_ref[0, ro0, hh * GRID_W:(hh + 1) * GRID_W, i * GRID_W:(i + 1) * GRID_W] = jnp.where(
                        inwin, tz_ref[hh, ro0 + i], NEG)

    return pl.pallas_call(
        assemble, name="na_bias_slabs", grid=(H // 2,),
        in_specs=[pl.BlockSpec((2, 2 * NA_KH, GRID_W, GRID_W), lambda p: (p, 0, 0, 0))],
        out_specs=pl.BlockSpec((1, NA_KH, 2 * GRID_W, NA_KH * GRID_W), lambda p: (p, 0, 0, 0)),
        out_shape=jax.ShapeDtypeStruct((H // 2, NA_KH, 2 * GRID_W, NA_KH * GRID_W), F32),
        compiler_params=_params(1),
    )(toeplitz)


def _half_masks(rows):
    lane = lax.broadcasted_iota(jnp.int32, (rows, LANES), 1)
    left = lane < HEAD_DIM
    return left, (left, jnp.logical_not(left))


def _stack_heads(x):
    left, halves = _half_masks(x.shape[0])
    xf = x.astype(F32)
    return jnp.concatenate([jnp.where(m, xf, 0.0).astype(BF16) for m in halves], axis=0)


def _unstack_heads(o):
    rows = o.shape[0] // 2
    left, _ = _half_masks(rows)
    return jnp.where(left, o[:rows], o[rows:])


def _na_row(j, t, rb, rows):
    r = j * rb + t
    rs = jnp.clip(r - NA_KH // 2, 0, rows - NA_KH)
    return pl.multiple_of(t * GRID_W, GRID_W), pl.multiple_of(rs * GRID_W, GRID_W), rs - r + (NA_KH - 1)


def _na_specs(T, rb):
    qrows = GRID_W * rb
    pairs = NA_HEADS // 2
    return ([pl.BlockSpec((qrows, LANES), lambda p, j: (j, p)),
             pl.BlockSpec((T, LANES), lambda p, j: (0, pairs + p)),
             pl.BlockSpec((T, LANES), lambda p, j: (0, 2 * pairs + p))],
            pl.BlockSpec((1, NA_KH, 2 * GRID_W, NA_KH * GRID_W), lambda p, j: (p, 0, 0, 0)))


def _softmax(s):
    p = jnp.exp(s - jnp.max(s, axis=-1, keepdims=True))
    return p / jnp.sum(p, axis=-1, keepdims=True)


def _na_fwd(qkva, bias, *, rb=32, group=32):
    T = qkva.shape[0]
    rows = T // GRID_W
    nkeys = NA_KH * GRID_W
    rb = min(rb, rows)
    group = min(group, rb)

    def body(q_ref, k_ref, v_ref, bias_ref, y_ref):
        j = pl.program_id(1)

        def rows_step(t, carry):
            at = [_na_row(j, t * group + u, rb, rows) for u in range(group)]
            s = [_dot_nt(_stack_heads(q_ref[pl.ds(q0, GRID_W), :]), k_ref[pl.ds(k0, nkeys), :]) for q0, k0, _ in at]
            p = [_softmax(su * QK_SCALE + bias_ref[0, ro0]) for su, (_, _, ro0) in zip(s, at)]
            o = [_dot(pu.astype(BF16), v_ref[pl.ds(k0, nkeys), :]) for pu, (_, k0, _) in zip(p, at)]
            for ou, (q0, _, _) in zip(o, at):
                y_ref[pl.ds(q0, GRID_W), :] = _unstack_heads(ou).astype(BF16)
            return carry

        lax.fori_loop(0, rb // group, rows_step, 0)

    qkv_specs, bias_spec = _na_specs(T, rb)
    return pl.pallas_call(
        body, name="na_fwd", grid=(NA_HEADS // 2, rows // rb),
        in_specs=qkv_specs + [bias_spec],
        out_specs=qkv_specs[0],
        out_shape=jax.ShapeDtypeStruct((T, NA_HEADS * HEAD_DIM), BF16),
        compiler_params=_params(2),
    )(qkva, qkva, qkva, bias)


def _na_bwd(qkva, dy, bias, *, rb=16, group=16, rider=None):
    T = qkva.shape[0]
    rows = T // GRID_W
    nkeys = NA_KH * GRID_W
    rb = min(rb, rows)
    group = min(group, rb)

    def body(q_ref, k_ref, v_ref, dy_ref, bias_ref, dq_ref, dk_ref, dv_ref, dbias_ref):
        j = pl.program_id(1)

        @pl.when(j == 0)
        def _():
            dk_ref[...] = jnp.zeros_like(dk_ref)
            dv_ref[...] = jnp.zeros_like(dv_ref)
            dbias_ref[...] = jnp.zeros_like(dbias_ref)

        def rows_step(t, carry):
            at = [_na_row(j, t * group + u, rb, rows) for u in range(group)]
            qs = [_stack_heads(q_ref[pl.ds(q0, GRID_W), :]) for q0, _, _ in at]
            dys = [_stack_heads(dy_ref[pl.ds(q0, GRID_W), :]) for q0, _, _ in at]
            s = [_dot_nt(qu, k_ref[pl.ds(k0, nkeys), :]) for qu, (_, k0, _) in zip(qs, at)]
            dp = [_dot_nt(du, v_ref[pl.ds(k0, nkeys), :]) for du, (_, k0, _) in zip(dys, at)]
            p = [_softmax(su * QK_SCALE + bias_ref[0, ro0]) for su, (_, _, ro0) in zip(s, at)]
            ds = [pu * (du - jnp.sum(pu * du, axis=-1, keepdims=True)) for pu, du in zip(p, dp)]
            for u, (q0, k0, ro0) in enumerate(at):
                dbias_ref[0, ro0] += ds[u]
                dsb = ds[u].astype(BF16)
                dq_ref[pl.ds(q0, GRID_W), :] = (_unstack_heads(_dot(dsb, k_ref[pl.ds(k0, nkeys), :])) * QK_SCALE).astype(BF16)
                dk_ref[pl.ds(k0, nkeys), :] += _dot_tn(dsb, qs[u]) * QK_SCALE
                dv_ref[pl.ds(k0, nkeys), :] += _dot_tn(p[u].astype(BF16), dys[u])
            return carry

        lax.fori_loop(0, rb // group, rows_step, 0)

    qkv_specs, bias_spec = _na_specs(T, rb)
    width = NA_HEADS * HEAD_DIM
    kv_out = pl.BlockSpec((T, LANES), lambda p, j: (0, p))
    grid = (NA_HEADS // 2, rows // rb)
    body, r_in, r_in_specs, r_out, r_out_specs, scratch = _ride(body, 5, 4, rider, grid, None)
    outs = pl.pallas_call(
        body, name="na_bwd", grid=grid,
        in_specs=qkv_specs + [qkv_specs[0], bias_spec] + r_in_specs,
        out_specs=[qkv_specs[0], kv_out, kv_out, bias_spec] + r_out_specs,
        out_shape=[jax.ShapeDtypeStruct((T, width), BF16), jax.ShapeDtypeStruct((T, width), F32),
                   jax.ShapeDtypeStruct((T, width), F32), jax.ShapeDtypeStruct(bias.shape, F32)] + r_out,
        scratch_shapes=scratch,
        compiler_params=_params(2),
    )(qkva, qkva, qkva, dy, bias, *r_in)
    return (*outs[:4], list(outs[4:]))


def _rpb_fold(dslab):
    pairs = dslab.shape[0]
    H = 2 * pairs
    ncell = GRID_W * GRID_W

    def disassemble(d_ref, tz_ref):
        tz_ref[...] = jnp.zeros_like(tz_ref)
        for ro0 in range(NA_KH):
            for hh in range(2):
                for i in range(NA_KH):
                    tz_ref[hh, ro0 + i] += d_ref[0, ro0, hh * GRID_W:(hh + 1) * GRID_W, i * GRID_W:(i + 1) * GRID_W]

    dtoeplitz = pl.pallas_call(
        disassemble, name="rpb_fold_tiles", grid=(pairs,),
        in_specs=[pl.BlockSpec((1, NA_KH, 2 * GRID_W, NA_KH * GRID_W), lambda p: (p, 0, 0, 0))],
        out_specs=pl.BlockSpec((2, 2 * NA_KH, GRID_W, GRID_W), lambda p: (p, 0, 0, 0)),
        out_shape=jax.ShapeDtypeStruct((H, 2 * NA_KH, GRID_W, GRID_W), F32),
        compiler_params=_params(1),
    )(dslab).reshape(H * 2 * NA_KH, ncell)
    cell = np.arange(ncell)
    co = cell % GRID_W - cell // GRID_W + (NA_KW - 1)
    e_co = jnp.asarray((co[:, None] == np.arange(LANES)[None, :]).astype(np.float32))

    def diagonals(x_ref, e_ref, o_ref):
        o_ref[...] = jnp.dot(x_ref[...], e_ref[...], preferred_element_type=F32, precision=lax.Precision.HIGHEST)

    return pl.pallas_call(
        diagonals, name="rpb_fold", out_shape=jax.ShapeDtypeStruct((H * 2 * NA_KH, LANES), F32),
        compiler_params=_params(0),
    )(dtoeplitz, e_co).reshape(H, 2 * NA_KH, LANES)


SWA_KEYS = 3 * WIN


def _swa_block(j, t, qbn, T):
    blk = j * qbn + t
    start = jnp.clip((blk - 1) * WIN, 0, T - SWA_KEYS)
    row = lax.broadcasted_iota(jnp.int32, (2 * WIN, SWA_KEYS), 0)
    qpos = blk * WIN + jnp.where(row < WIN, row, row - WIN)
    kpos = start + lax.broadcasted_iota(jnp.int32, (2 * WIN, SWA_KEYS), 1)
    return pl.multiple_of(t * WIN, WIN), pl.multiple_of(start, WIN), jnp.abs(qpos - kpos) <= WIN


def _swa_sinks(sink_ref, p):
    row = lax.broadcasted_iota(jnp.int32, (2 * WIN, 1), 0)
    return jnp.where(row < WIN, sink_ref[p], sink_ref[p + NB_HEADS // 2])


def _swa_probs(s, mask, sink):
    s = jnp.where(mask, s * QK_SCALE, NEG)
    m = jnp.maximum(jnp.max(s, axis=-1, keepdims=True), sink)
    e = jnp.exp(s - m)
    esink = jnp.exp(sink - m)
    den = jnp.sum(e, axis=-1, keepdims=True) + esink
    return e / den, esink / den


def _swa_specs(T, qbn):
    return [pl.BlockSpec(memory_space=pltpu.SMEM),
            pl.BlockSpec((WIN * qbn, LANES), lambda p, j: (j, p)),
            pl.BlockSpec((T, LANES), lambda p, j: (0, 0)),
            pl.BlockSpec((T, LANES), lambda p, j: (0, 1))]


def _swa_fwd(qb, kvb, sink, *, qbn=32, group=32):
    T = qb.shape[0]
    pairs = NB_HEADS // 2
    qbn = min(qbn, T // WIN)
    group = min(group, qbn)

    def body(sink_ref, q_ref, k_ref, v_ref, y_ref):
        p, j = pl.program_id(0), pl.program_id(1)
        sinks = _swa_sinks(sink_ref, p)

        def blocks_step(t, carry):
            at = [_swa_block(j, t * group + u, qbn, T) for u in range(group)]
            s = [_dot_nt(_stack_heads(q_ref[pl.ds(q0, WIN), :]), k_ref[pl.ds(k0, SWA_KEYS), :]) for q0, k0, _ in at]
            pr = [_swa_probs(su, mask, sinks)[0] for su, (_, _, mask) in zip(s, at)]
            o = [_dot(pu.astype(BF16), v_ref[pl.ds(k0, SWA_KEYS), :]) for pu, (_, k0, _) in zip(pr, at)]
            for ou, (q0, _, _) in zip(o, at):
                y_ref[pl.ds(q0, WIN), :] = _unstack_heads(ou).astype(BF16)
            return carry

        lax.fori_loop(0, qbn // group, blocks_step, 0)

    specs = _swa_specs(T, qbn)
    return pl.pallas_call(
        body, name="swa_fwd", grid=(pairs, T // (WIN * qbn)),
        in_specs=specs, out_specs=specs[1],
        out_shape=jax.ShapeDtypeStruct((T, NB_HEADS * HEAD_DIM), BF16),
        compiler_params=_params(2),
    )(sink, qb, kvb, kvb)


def _swa_bwd(qb, kvb, dy, sink, *, qbn=16, group=16, rider=None):
    T = qb.shape[0]
    pairs = NB_HEADS // 2
    qbn = min(qbn, T // WIN)
    group = min(group, qbn)

    def body(sink_ref, q_ref, k_ref, v_ref, dy_ref, dq_ref, dk_ref, dv_ref, dsink_ref):
        p, j = pl.program_id(0), pl.program_id(1)
        sinks = _swa_sinks(sink_ref, p)

        @pl.when((p == 0) & (j == 0))
        def _():
            dk_ref[...] = jnp.zeros_like(dk_ref)
            dv_ref[...] = jnp.zeros_like(dv_ref)

        @pl.when(j == 0)
        def _():
            dsink_ref[...] = jnp.zeros_like(dsink_ref)

        def blocks_step(t, carry):
            at = [_swa_block(j, t * group + u, qbn, T) for u in range(group)]
            qs = [_stack_heads(q_ref[pl.ds(q0, WIN), :]) for q0, _, _ in at]
            dys = [_stack_heads(dy_ref[pl.ds(q0, WIN), :]) for q0, _, _ in at]
            s = [_dot_nt(qu, k_ref[pl.ds(k0, SWA_KEYS), :]) for qu, (_, k0, _) in zip(qs, at)]
            dp = [_dot_nt(du, v_ref[pl.ds(k0, SWA_KEYS), :]) for du, (_, k0, _) in zip(dys, at)]
            probs = [_swa_probs(su, mask, sinks) for su, (_, _, mask) in zip(s, at)]
            for u, (q0, k0, _) in enumerate(at):
                pr, psink = probs[u]
                delta = jnp.sum(pr * dp[u], axis=-1, keepdims=True)
                dsb = (pr * (dp[u] - delta)).astype(BF16)
                dsk = psink * delta
                for hh in range(2):
                    dsink_ref[0, hh:hh + 1, :] += jnp.broadcast_to(-jnp.sum(dsk[hh * WIN:(hh + 1) * WIN]), (1, LANES))
                dq_ref[pl.ds(q0, WIN), :] = _unstack_heads(_dot(dsb, k_ref[pl.ds(k0, SWA_KEYS), :])) * QK_SCALE
                dk_ref[pl.ds(k0, SWA_KEYS), :] += _dot_tn(dsb, qs[u]) * QK_SCALE
                dv_ref[pl.ds(k0, SWA_KEYS), :] += _dot_tn(pr.astype(BF16), dys[u])
            return carry

        lax.fori_loop(0, qbn // group, blocks_step, 0)

    specs = _swa_specs(T, qbn)
    kv_out = pl.BlockSpec((T, LANES), lambda p, j: (0, 0))
    grid = (pairs, T // (WIN * qbn))
    body, r_in, r_in_specs, r_out, r_out_specs, scratch = _ride(body, 5, 4, rider, grid, None)
    outs = pl.pallas_call(
        body, name="swa_bwd", grid=grid,
        in_specs=specs + [specs[1]] + r_in_specs,
        out_specs=[specs[1], kv_out, kv_out, pl.BlockSpec((1, 8, LANES), lambda p, j: (p, 0, 0))] + r_out_specs,
        out_shape=[jax.ShapeDtypeStruct((T, NB_HEADS * HEAD_DIM), F32), jax.ShapeDtypeStruct((T, LANES), F32),
                   jax.ShapeDtypeStruct((T, LANES), F32), jax.ShapeDtypeStruct((pairs, 8, LANES), F32)] + r_out,
        scratch_shapes=scratch,
        compiler_params=_params(2),
    )(sink, qb, kvb, kvb, dy, *r_in)
    return (*outs[:4], list(outs[4:]))


def _merge_fwd(ya, yb, gates, wa, wb, wout, h, *, tm=512):
    T, D = h.shape
    W = ya.shape[1]

    def body(ya_ref, yb_ref, gt_ref, wa_ref, wb_ref, wo_ref, h_ref, h2_ref, mg_ref):
        pa = _dot(ya_ref[...], wa_ref[...])
        pb = _dot(yb_ref[...], wb_ref[...])
        mg = (jax.nn.sigmoid(gt_ref[:, 0:D]) * pa + jax.nn.sigmoid(gt_ref[:, D:2 * D]) * pb).astype(BF16)
        mg_ref[...] = mg
        h2_ref[...] = h_ref[...] + _dot(mg, wo_ref[...])

    def tok(n):
        return pl.BlockSpec((tm, n), lambda i: (i, 0))

    def full(r, c):
        return pl.BlockSpec((r, c), lambda i: (0, 0))

    return pl.pallas_call(
        body, name="merge_fwd", grid=(T // tm,),
        in_specs=[tok(W), tok(W), tok(2 * D), full(W, D), full(W, D), full(D, D), tok(D)],
        out_specs=[tok(D), tok(D)],
        out_shape=[jax.ShapeDtypeStruct((T, D), F32), jax.ShapeDtypeStruct((T, D), BF16)],
        compiler_params=_params(1),
    )(ya, yb, gates, wa, wb, wout, h)


def _merge_bwd(dh, ya, yb, gates, wa, wb, wout, *, tm=512, rider=None):
    T, D = dh.shape
    W = ya.shape[1]

    def body(dh_ref, ya_ref, yb_ref, gt_ref, wa_ref, wb_ref, wo_ref, dya_ref, dyb_ref, dpa_ref, dpb_ref, dgt_ref):
        dmg = _dot_nt(dh_ref[...].astype(BF16), wo_ref[...])
        for y_ref, w_ref, dy_ref, dp_ref, lo in ((ya_ref, wa_ref, dya_ref, dpa_ref, 0), (yb_ref, wb_ref, dyb_ref, dpb_ref, D)):
            sg = jax.nn.sigmoid(gt_ref[:, lo:lo + D])
            dp = (dmg * sg).astype(BF16)
            dp_ref[...] = dp
            dgt_ref[:, lo:lo + D] = (dmg * _dot(y_ref[...], w_ref[...]) * (sg * (1.0 - sg))).astype(BF16)
            dy_ref[...] = _dot_nt(dp, w_ref[...]).astype(BF16)

    def tok(n):
        return pl.BlockSpec((tm, n), lambda i: (i, 0))

    def full(r, c):
        return pl.BlockSpec((r, c), lambda i: (0, 0))

    grid = (T // tm,)
    body, r_in, r_in_specs, r_out, r_out_specs, scratch = _ride(body, 7, 5, rider, grid, None)
    outs = pl.pallas_call(
        body, name="merge_bwd", grid=grid,
        in_specs=[tok(D), tok(W), tok(W), tok(2 * D), full(W, D), full(W, D), full(D, D)] + r_in_specs,
        out_specs=[tok(W), tok(W), tok(D), tok(D), tok(2 * D)] + r_out_specs,
        out_shape=[jax.ShapeDtypeStruct((T, W), BF16), jax.ShapeDtypeStruct((T, W), BF16),
                   jax.ShapeDtypeStruct((T, D), BF16), jax.ShapeDtypeStruct((T, D), BF16),
                   jax.ShapeDtypeStruct((T, 2 * D), BF16)] + r_out,
        scratch_shapes=scratch,
        compiler_params=_params(1),
    )(dh, ya, yb, gates, wa, wb, wout, *r_in)
    return (*outs[:5], list(outs[5:]))


def _pair_heads(a, axis):
    shp = a.shape
    a = a.reshape(shp[:axis] + (2, NB_HEADS // 2, HEAD_DIM) + shp[axis + 1:])
    return jnp.swapaxes(a, axis, axis + 1).reshape(shp)


def _unpair_heads(a, axis):
    shp = a.shape
    a = a.reshape(shp[:axis] + (NB_HEADS // 2, 2, HEAD_DIM) + shp[axis + 1:])
    return jnp.swapaxes(a, axis, axis + 1).reshape(shp)


FFN1 = ("ffn1_w_gate", "ffn1_w_up", "ffn1_w_down")
FFN2 = ("ffn2_w_gate", "ffn2_w_up", "ffn2_w_down")
MIXER = ("w_in", "w_branch_a", "w_branch_b", "w_out")
BRANCH = MIXER[1:]


def _layer_grads(x, target, g1, f1, gmix, late, rpb, sink, g2, gfin, comm=None):
    T = x.shape[0]
    tables = _rope_tables(T)
    bias = _na_bias_slabs(rpb)

    comm = comm or _Local(late)
    h1, n1, hdn1, p1, q1, gathered = _ffn_fwd(x, g1, *f1, name="ffn1_fwd", rider=comm.late_rider)
    w_in_t, wa, wb, wout, f2 = comm.late(gathered)
    w_in_p = jnp.concatenate([w_in_t[:O_QB], _pair_heads(w_in_t[O_QB:O_KB], 0), w_in_t[O_KB:]], axis=0)
    wb_p = _pair_heads(wb, 0)
    u, qkva, qb, kvb, gates = _mix_in_fwd(h1, gmix, w_in_p, tables)
    ya = _na_fwd(qkva, bias)
    yb = _swa_fwd(qb, kvb, sink)
    h2, merged = _merge_fwd(ya, yb, gates, wa, wb_p, wout, h1)
    dh3, n2, hdn2, p2, q2, loss, dgfin, _ = _ffn_fwd(h2, g2, *f2, name="ffn2_fwd", head=(gfin, target))

    dh2, da2, db2, dg2, _ = _ffn_bwd(dh3, h2, g2, p2, q2, *f2, name="ffn2_bwd")
    df2 = [_wgrad_shard_a(da2, n2, name="ffn2_dwg")[0], _wgrad_shard_a(db2, n2, name="ffn2_dwu")[0],
           _wgrad_shard_a(hdn2, dh3, scale=0.5, name="ffn2_dwd")[0]]
    red2 = comm.reduce(FFN2, df2, tag="ffn2")
    dya, dyb, dpa, dpb, dgates, _ = _merge_bwd(dh2, ya, yb, gates, wa, wb_p, wout)
    dwout = _wgrad_cols(merged, dh2, 1, name="dwout").reshape(N_CHIPS, D_MODEL // N_CHIPS, D_MODEL)
    dwa = _wgrad_cols(ya, dpa, N_CHIPS, name="dwa")
    dwb = _unpair_heads(_wgrad_cols(yb, dpb, N_CHIPS, name="dwb"), 1)
    redb = comm.reduce(BRANCH, [dwa, dwb, dwout], tag="branch")
    dqa, dka, dva, dbias, got = _na_bwd(qkva, dya, bias, rider=_two_riders(red2.sibling, redb.sibling))
    red2.partial(got[:len(FFN2)])
    redb.partial(got[len(FFN2):])
    drpb = _rpb_fold(dbias)
    dqb, dkb, dvb, dsink, got = _swa_bwd(qb, kvb, dyb, sink, rider=_two_riders(red2.chips, redb.chips))
    red2.halves(got[:len(FFN2)])
    redb.halves(got[len(FFN2):])
    dz, dh1, dgmix, got = _mix_in_bwd(dqa, dka, dva, dqb, dkb, dvb, dgates, h1, gmix, dh2, w_in_p, tables,
                                      rider=_two_riders(red2.share, redb.share))
    out = red2.result(got[:len(FFN2)])
    out.update(redb.result(got[len(FFN2):]))
    dwin_p = _wgrad_rows(dz, u, 2, name="dwin")[0].reshape(D_IN, D_MODEL)
    dwin = jnp.concatenate([dwin_p[:O_QB], _unpair_heads(dwin_p[O_QB:O_KB], 0), dwin_p[O_KB:]], axis=0)
    dx, da1, db1, dg1, _ = _ffn_bwd(dh1, x, g1, p1, q1, *f1, name="ffn1_bwd")
    small = dict(loss=loss, ffn1_norm=dg1, mix_norm=dgmix, ffn2_norm=dg2, final_norm=dgfin, na_rpb=drpb,
                 sink_logit=dsink[:, 0:2, 0].T.reshape(NB_HEADS))
    redw = comm.reduce(("w_in",), [dwin.reshape(N_CHIPS, D_IN // N_CHIPS, D_MODEL)], tag="w_in")
    dwg1, got = _wgrad_shard_a(da1, n1, name="ffn1_dwg", rider=_two_riders(redw.sibling, comm.small(small)))
    redw.partial(got[:1])
    comm.small_done(got[1:])
    dwu1, got = _wgrad_shard_a(db1, n1, name="ffn1_dwu", rider=redw.chips)
    redw.halves(got)
    red1 = comm.reduce(FFN1[:2], [dwg1, dwu1], tag="ffn1_gate_up").partial_now()
    dwd1, got = _wgrad_shard_a(hdn1, dh1, scale=0.5, name="ffn1_dwd", rider=_two_riders(red1.chips, redw.share))
    out.update(redw.result(got[2:]))
    red1.halves(got[:2])
    redd = comm.reduce(FFN1[2:], [dwd1], tag="ffn1_down")
    got = _run_rider(_two_riders(red1.share, redd.sibling), name="rs_share_gate_up_sibling_down")
    out.update(red1.result(got[:2]))
    out.update(redd.partial(got[2:]).halves_now().result_now())
    for names in (FFN2, ("w_in",), BRANCH[:2], BRANCH[2:], FFN1[:2], FFN1[2:]):
        comm.update(names, out)
    out.update(small, dx=dx)
    return out


class _Local:
    late_rider = None

    def __init__(self, late):
        self._late = late

    def late(self, gathered):
        return self._late

    def reduce(self, names, grads, *, tag):
        return _LocalReduce(names, grads)

    def update(self, names, reduced):
        pass

    def small(self, grads):
        return None

    def small_done(self, got):
        pass


class _LocalReduce:
    sibling = chips = share = None

    def __init__(self, names, grads):
        self._result = dict(zip(names, grads))

    def partial(self, got=None):
        return self

    halves = partial_now = halves_now = partial

    def result(self, got=None):
        return self._result

    result_now = result


ANY = pl.BlockSpec(memory_space=pl.ANY)


def _place():
    x, y, c = lax.axis_index("x"), lax.axis_index("y"), lax.axis_index("c")
    chips = [(1 - x, y), (x, 1 - y), (1 - x, 1 - y)]
    return x, y, c, 2 * x + y, chips


def _remote(src, dst, send_sems, recv_sems, k, device):
    return pltpu.make_async_remote_copy(src_ref=src, dst_ref=dst, send_sem=send_sems.at[k], recv_sem=recv_sems.at[k],
                                        device_id=device, device_id_type=MESH)


class _Rider:
    def __init__(self, inputs, out_shape, scratch, start, middle, finish):
        self.inputs, self.out_shape, self.scratch = list(inputs), list(out_shape), list(scratch)
        self.start, self.middle, self.finish = start, middle, finish


def _two_riders(first, second):
    if first is None or second is None:
        return first or second
    assert first.middle is None and second.middle is None
    n_in, n_out, n_sem = len(first.inputs), len(first.out_shape), len(first.scratch)

    def phase(name):
        def run(ins, outs, sems):
            getattr(first, name)(ins[:n_in], outs[:n_out], sems[:n_sem])
            getattr(second, name)(ins[n_in:], outs[n_out:], sems[n_sem:])
        return run

    return _Rider(first.inputs + second.inputs, first.out_shape + second.out_shape, first.scratch + second.scratch,
                  phase("start"), None, phase("finish"))


def _run_rider(rider, *, name):
    if rider is None:
        return []
    n_in, n_out = len(rider.inputs), len(rider.out_shape)

    def body(*refs):
        ins, outs, sems = refs[:n_in], refs[n_in:n_in + n_out], refs[n_in + n_out:]
        rider.start(ins, outs, sems)
        if rider.middle is not None:
            rider.middle(ins, outs, sems)
        rider.finish(ins, outs, sems)

    return pl.pallas_call(body, name=name, in_specs=[ANY] * n_in, out_specs=[ANY] * n_out, out_shape=rider.out_shape,
                          scratch_shapes=rider.scratch)(*rider.inputs)


def _ride(body, n_in, n_out, rider, grid, middle_step):
    if rider is None:
        return body, [], [], [], [], []
    r_in, r_out = len(rider.inputs), len(rider.out_shape)
    steps = math.prod(grid)

    def riding(*refs):
        ins, r_ins = refs[:n_in], refs[n_in:n_in + r_in]
        outs = refs[n_in + r_in:n_in + r_in + n_out]
        r_outs = refs[n_in + r_in + n_out:n_in + r_in + n_out + r_out]
        sems = refs[n_in + r_in + n_out + r_out:]
        step = pl.program_id(0)
        for axis in range(1, len(grid)):
            step = step * grid[axis] + pl.program_id(axis)

        @pl.when(step == 0)
        def _():
            rider.start(r_ins, r_outs, sems)

        body(*ins, *outs)

        if rider.middle is not None:
            @pl.when(step == middle_step)
            def _():
                rider.middle(r_ins, r_outs, sems)

        @pl.when(step == steps - 1)
        def _():
            rider.finish(r_ins, r_outs, sems)

    return riding, rider.inputs, [ANY] * r_in, rider.out_shape, [ANY] * r_out, rider.scratch


def _gather_rider(shards):
    n = len(shards)

    def plan(ins, outs, sems, kinds):
        send_sems, recv_sems, own_send_sems, own_recv_sems = sems
        x, y, c, mine, chips = _place()
        sibling = (x, y, 1 - c)
        made = {k: [] for k in kinds}
        for i in range(n):
            hr = shards[i].shape[0] // 2
            if "own" in made:
                made["own"].append(_remote(ins[i], outs[i].at[mine], own_send_sems, own_recv_sems, i, sibling))
            for j, (cx, cy) in enumerate(chips):
                here = outs[i].at[2 * cx + cy, pl.ds(c * hr, hr)]
                there = outs[i].at[2 * cx + cy, pl.ds((1 - c) * hr, hr)]
                if "sends" in made:
                    made["sends"].append(_remote(ins[i].at[pl.ds(c * hr, hr)], outs[i].at[mine, pl.ds(c * hr, hr)],
                                                 send_sems, recv_sems, 6 * i + j, (cx, cy, c)))
                if "landed" in made:
                    made["landed"].append(_remote(here, here, send_sems, recv_sems, 6 * i + j, (cx, cy, c)))
                if "passes" in made:
                    made["passes"].append(_remote(here, here, send_sems, recv_sems, 6 * i + 3 + j, sibling))
                if "others" in made:
                    made["others"].append(_remote(there, there, send_sems, recv_sems, 6 * i + 3 + j, sibling))
        return [made[k] for k in kinds]

    def start(ins, outs, sems):
        own, sends = plan(ins, outs, sems, ("own", "sends"))
        for cp in own + sends:
            cp.start()

    def middle(ins, outs, sems):
        landed, passes = plan(ins, outs, sems, ("landed", "passes"))
        for arrived, cp in zip(landed, passes):
            arrived.wait_recv()
            cp.start()

    def finish(ins, outs, sems):
        own, sends, passes, others = plan(ins, outs, sems, ("own", "sends", "passes", "others"))
        for arrived in others:
            arrived.wait_recv()
        for cp in sends + passes:
            cp.wait_send()
        for cp in own:
            cp.wait()

    return _Rider(shards, [jax.ShapeDtypeStruct((N_CHIPS,) + s.shape, s.dtype) for s in shards],
                  [pltpu.SemaphoreType.DMA((6 * n,)), pltpu.SemaphoreType.DMA((6 * n,)),
                   pltpu.SemaphoreType.DMA((n,)), pltpu.SemaphoreType.DMA((n,))], start, middle, finish)


def _swap_rider(arrays, out_shape, source):
    n = len(arrays)

    def plan(ins, outs, sems):
        send_sems, recv_sems = sems
        x, y, c, _, _ = _place()
        return [_remote(source(ins[i], c, i), outs[i], send_sems, recv_sems, i, (x, y, 1 - c)) for i in range(n)]

    def start(ins, outs, sems):
        for cp in plan(ins, outs, sems):
            cp.start()

    def finish(ins, outs, sems):
        for cp in plan(ins, outs, sems):
            cp.wait()

    return _Rider(arrays, out_shape, [pltpu.SemaphoreType.DMA((n,)), pltpu.SemaphoreType.DMA((n,))], start, None, finish)


def _sibling_rider(grads):
    half = [g.shape[1] // 2 for g in grads]
    return _swap_rider(grads, [jax.ShapeDtypeStruct((g.shape[0], hr, g.shape[2]), g.dtype) for g, hr in zip(grads, half)],
                       lambda ref, c, i: ref.at[:, pl.ds((1 - c) * half[i], half[i])])


def _share_rider(halves):
    return _swap_rider(halves, [jax.ShapeDtypeStruct(h.shape, h.dtype) for h in halves], lambda ref, c, i: ref)


def _chips_rider(parts):
    n = len(parts)

    def plan(ins, outs, sems):
        send_sems, recv_sems = sems
        _, _, c, _, chips = _place()
        return [_remote(ins[i].at[2 * cx + cy], outs[i].at[j], send_sems, recv_sems, 3 * i + j, (cx, cy, c))
                for i in range(n) for j, (cx, cy) in enumerate(chips)]

    def start(ins, outs, sems):
        for cp in plan(ins, outs, sems):
            cp.start()

    def finish(ins, outs, sems):
        for cp in plan(ins, outs, sems):
            cp.wait()

    return _Rider(parts, [jax.ShapeDtypeStruct((N_CHIPS - 1,) + p.shape[1:], p.dtype) for p in parts],
                  [pltpu.SemaphoreType.DMA((3 * n,)), pltpu.SemaphoreType.DMA((3 * n,))], start, None, finish)


class _Reduce:
    def __init__(self, names, grads, cidx, chip, *, tag):
        self.names, self.grads, self.cidx, self.chip, self.tag = names, grads, cidx, chip, tag
        self.sibling = _sibling_rider(grads)

    def _by_shape(self, fn, *lists):
        done, i = [], 0
        while i < len(self.names):
            j = i + 1
            while j < len(self.names) and self.grads[j].shape == self.grads[i].shape:
                j += 1
            done += fn(*[lst[i:j] for lst in lists], self.names[i])
            i = j
        return done

    def partial(self, from_sibling):
        self.from_sibling = from_sibling
        self.chips = _chips_rider(self._by_shape(
            lambda g, r, k: _add_sibling(g, r, self.cidx, name="add_sibling_" + k), self.grads, from_sibling))
        return self

    def halves(self, from_chips):
        self.mine = self._by_shape(
            lambda g, r1, r2, k: _add_chips(g, r1, r2, self.cidx, self.chip, name="add_chips_" + k),
            self.grads, self.from_sibling, from_chips)
        self.share = _share_rider(self.mine)
        return self

    def result(self, others):
        return dict(zip(self.names, zip(self.mine, others)))

    def partial_now(self):
        return self.partial(_run_rider(self.sibling, name="rs_sibling_" + self.tag))

    def halves_now(self):
        return self.halves(_run_rider(self.chips, name="rs_chips_" + self.tag))

    def result_now(self):
        return self.result(_run_rider(self.share, name="rs_share_" + self.tag))


N_DEV = 8


def _small_rider(vec):
    def plan(ins, outs, sems):
        send_sems, recv_sems = sems
        x, y, c, _, _ = _place()
        return [_remote(ins[0], outs[0].at[k - 1], send_sems, recv_sems, k - 1, (x ^ (k >> 2), y ^ ((k >> 1) & 1), c ^ (k & 1)))
                for k in range(1, N_DEV)]

    def start(ins, outs, sems):
        for cp in plan(ins, outs, sems):
            cp.start()

    def finish(ins, outs, sems):
        for cp in plan(ins, outs, sems):
            cp.wait()

    return _Rider([vec], [jax.ShapeDtypeStruct((N_DEV - 1,) + vec.shape, vec.dtype)],
                  [pltpu.SemaphoreType.DMA((N_DEV - 1,)), pltpu.SemaphoreType.DMA((N_DEV - 1,))], start, None, finish)


def _small_sum(vec, others, me):
    def body(me_ref, v_ref, b_ref, o_ref):
        mine = me_ref[0]
        acc = None
        for d in range(N_DEV):
            term = jnp.where(mine == d, v_ref[...], b_ref[jnp.maximum((mine ^ d) - 1, 0)])
            acc = term if acc is None else acc + term
        o_ref[...] = acc

    vmem = pl.BlockSpec(memory_space=pltpu.VMEM)
    return pl.pallas_call(
        body, name="small_sum", in_specs=[pl.BlockSpec(memory_space=pltpu.SMEM), vmem, vmem], out_specs=vmem,
        out_shape=jax.ShapeDtypeStruct(vec.shape, vec.dtype),
    )(me, vec, others)


ELEMWISE_BLOCK = 512 * 1024


def _row_tile(rows, cols):
    best = None
    for t in range(16, rows + 1, 16):
        if rows % t == 0 and t * cols <= ELEMWISE_BLOCK:
            best = t
    return best if best is not None else rows


def _add_sibling(gs, r1s, cidx, *, name):
    n = len(gs)
    S, R, C = gs[0].shape
    hr = R // 2
    tr = _row_tile(hr, C)
    nt = hr // tr

    def body(c_ref, *refs):
        for g_ref, r_ref, o_ref in zip(refs[:n], refs[n:2 * n], refs[2 * n:]):
            o_ref[...] = (g_ref[...] + r_ref[...]).astype(BF16)

    blk = pl.BlockSpec((1, tr, C), lambda s, t, c: (s, t, 0))
    mine = pl.BlockSpec((1, tr, C), lambda s, t, c: (s, c[0] * nt + t, 0))
    return list(pl.pallas_call(
        body, name=name,
        grid_spec=pltpu.PrefetchScalarGridSpec(
            num_scalar_prefetch=1, grid=(S, nt), in_specs=[mine] * n + [blk] * n, out_specs=[blk] * n),
        out_shape=[jax.ShapeDtypeStruct((S, hr, C), BF16)] * n,
        compiler_params=_params(2),
    )(cidx, *gs, *r1s))


def _add_chips(gs, r1s, r2s, cidx, chip, *, name):
    n = len(gs)
    _, R, C = gs[0].shape
    hr = R // 2
    tr = _row_tile(hr, C)
    nt = hr // tr

    def body(pos_ref, *refs):
        for g_ref, r1_ref, r2_ref, o_ref in zip(refs[:n], refs[n:2 * n], refs[2 * n:3 * n], refs[3 * n:]):
            own = g_ref[0] + r1_ref[0]
            o_ref[...] = ((own + r2_ref[0].astype(F32)) + r2_ref[1].astype(F32)) + r2_ref[2].astype(F32)

    pos = jnp.concatenate([cidx, chip])
    return list(pl.pallas_call(
        body, name=name,
        grid_spec=pltpu.PrefetchScalarGridSpec(
            num_scalar_prefetch=1, grid=(nt,),
            in_specs=[pl.BlockSpec((1, tr, C), lambda t, pos: (pos[1], pos[0] * nt + t, 0))] * n
            + [pl.BlockSpec((1, tr, C), lambda t, pos: (pos[1], t, 0))] * n
            + [pl.BlockSpec((N_CHIPS - 1, tr, C), lambda t, pos: (0, t, 0))] * n,
            out_specs=[pl.BlockSpec((tr, C), lambda t, pos: (t, 0))] * n),
        out_shape=[jax.ShapeDtypeStruct((hr, C), F32)] * n,
        compiler_params=_params(1),
    )(pos, *gs, *r1s, *r2s))


def _adamw_math(w, g, m, v):
    mn = ADAM_B1 * m + (1.0 - ADAM_B1) * g
    vn = ADAM_B2 * v + (1.0 - ADAM_B2) * (g * g)
    m_hat = mn / (1.0 - ADAM_B1 ** ADAM_STEP)
    v_hat = vn / (1.0 - ADAM_B2 ** ADAM_STEP)
    return -ADAM_LR * (m_hat / (jnp.sqrt(v_hat) + ADAM_EPS) + ADAM_WD * w), mn, vn


def _adamw_halves(ws, mines, others, ms, vs, cidx, *, name):
    n = len(ws)
    R, C = ws[0].shape
    hr = R // 2
    tr = _row_tile(hr, C * min(n, 2))
    nt = hr // tr

    def body(c_ref, *refs):
        for i in range(n):
            w_ref, a_ref, b_ref, m_ref, v_ref = (refs[j * n + i] for j in range(5))
            g_ref, d_ref, mo_ref, vo_ref = (refs[(5 + j) * n + i] for j in range(4))
            gv = jnp.where(pl.program_id(0) == c_ref[0], a_ref[...], b_ref[...])
            g_ref[...] = gv
            d_ref[...], mo_ref[...], vo_ref[...] = _adamw_math(w_ref[...], gv, m_ref[...], v_ref[...])

    full = pl.BlockSpec((tr, C), lambda h, t, c: (h * nt + t, 0))
    own = pl.BlockSpec((tr, C), lambda h, t, c: (jnp.where(h == c[0], t, 0), 0))
    sib = pl.BlockSpec((tr, C), lambda h, t, c: (jnp.where(h == c[0], 0, t), 0))
    shape = jax.ShapeDtypeStruct((R, C), F32)
    outs = pl.pallas_call(
        body, name=name,
        grid_spec=pltpu.PrefetchScalarGridSpec(
            num_scalar_prefetch=1, grid=(2, nt),
            in_specs=[full] * n + [own] * n + [sib] * n + [full] * (2 * n), out_specs=[full] * (4 * n)),
        out_shape=[shape] * (4 * n),
        compiler_params=_params(2),
    )(cidx, *ws, *mines, *others, *ms, *vs)
    return [list(outs[j * n:(j + 1) * n]) for j in range(4)]


def _adamw_small(ws, gs, ms, vs):
    n = len(ws)

    def body(*refs):
        for i in range(n):
            w_ref, g_ref, m_ref, v_ref = (refs[j * n + i] for j in range(4))
            d_ref, mo_ref, vo_ref = (refs[(4 + j) * n + i] for j in range(3))
            d_ref[...], mo_ref[...], vo_ref[...] = _adamw_math(w_ref[...], g_ref[...], m_ref[...], v_ref[...])

    shapes = [jax.ShapeDtypeStruct(a.shape, F32) for a in ws]
    outs = pl.pallas_call(body, name="adamw_small", out_shape=shapes * 3, compiler_params=_params(0))(*ws, *gs, *ms, *vs)
    return outs[:n], outs[n:2 * n], outs[2 * n:]


def _unstack_cols(w):
    s, r, c = w.shape
    return w.transpose(1, 0, 2).reshape(r, s * c)


def _pad_rows(a, rows):
    return jnp.pad(a, ((0, rows - a.shape[0]), (0, LANES - a.shape[1])))


BIG = ("ffn1_w_gate", "ffn1_w_up", "ffn1_w_down", "w_in", "w_branch_a", "w_branch_b", "w_out",
       "ffn2_w_gate", "ffn2_w_up", "ffn2_w_down")
TRANSPOSED = ("ffn1_w_gate", "ffn1_w_up", "w_in", "ffn2_w_gate", "ffn2_w_up")
WEIGHTS = ("ffn1_norm", "ffn1_w_gate", "ffn1_w_up", "ffn1_w_down", "mix_norm", "w_in", "na_rpb", "sink_logit",
           "w_branch_a", "w_branch_b", "w_out", "ffn2_norm", "ffn2_w_gate", "ffn2_w_up", "ffn2_w_down", "final_norm")


def kernel(x, ffn1_norm, ffn1_w_gate, ffn1_w_up, ffn1_w_down, mix_norm, w_in, na_rpb, sink_logit, w_branch_a, w_branch_b, w_out, ffn2_norm, ffn2_w_gate, ffn2_w_up, ffn2_w_down, final_norm, loss_target, m_ffn1_norm, m_ffn1_w_gate, m_ffn1_w_up, m_ffn1_w_down, m_mix_norm, m_w_in, m_na_rpb, m_sink_logit, m_w_branch_a, m_w_branch_b, m_w_out, m_ffn2_norm, m_ffn2_w_gate, m_ffn2_w_up, m_ffn2_w_down, m_final_norm, v_ffn1_norm, v_ffn1_w_gate, v_ffn1_w_up, v_ffn1_w_down, v_mix_norm, v_w_in, v_na_rpb, v_sink_logit, v_w_branch_a, v_w_branch_b, v_w_out, v_ffn2_norm, v_ffn2_w_gate, v_ffn2_w_up, v_ffn2_w_down, v_final_norm):
    args = dict(locals())
    w = {k: args[k] for k in WEIGHTS}
    mom = {k: args["m_" + k] for k in WEIGHTS}
    var = {k: args["v_" + k] for k in WEIGHTS}
    cidx = lax.axis_index("c").astype(jnp.int32).reshape(1)
    chip = (2 * lax.axis_index("x") + lax.axis_index("y")).astype(jnp.int32).reshape(1)

    def shard(a, k):
        return jnp.swapaxes(a[0], 0, 1) if k in TRANSPOSED else a[0]

    def unshard(a, k):
        return (jnp.swapaxes(a, 0, 1) if k in TRANSPOSED else a)[None]

    def bf16_shards(names):
        return [shard(w[k], k).astype(BF16) for k in names]

    class comm:
        late_rider = _gather_rider(bf16_shards(MIXER + FFN2))

        @staticmethod
        def late(gathered):
            full = dict(zip(MIXER + FFN2, gathered))
            return (full["w_in"].reshape(D_IN, D_MODEL), _unstack_cols(full["w_branch_a"]), _unstack_cols(full["w_branch_b"]),
                    full["w_out"].reshape(D_MODEL, D_MODEL), tuple(full[k] for k in FFN2))

        @staticmethod
        def reduce(names, grads, *, tag):
            return _Reduce(names, grads, cidx, chip, tag=tag)

        @staticmethod
        def update(names, reduced):
            res = _adamw_halves([shard(w[k], k) for k in names], [reduced[k][0] for k in names],
                                [reduced[k][1] for k in names], [shard(mom[k], k) for k in names],
                                [shard(var[k], k) for k in names], cidx, name="adamw_" + names[0])
            for i, k in enumerate(names):
                grads_out[k], deltas[k], new_m[k], new_v[k] = (unshard(a[i], k) for a in res)

        @staticmethod
        def small(g):
            packed["mine"] = jnp.concatenate([
                g["ffn1_norm"].reshape(rows, LANES), g["mix_norm"].reshape(rows, LANES), g["ffn2_norm"].reshape(rows, LANES),
                g["final_norm"].reshape(rows, LANES), g["na_rpb"].reshape(-1, LANES),
                _pad_rows(g["sink_logit"].reshape(1, NB_HEADS), 8), _pad_rows(g["loss"], 8)], axis=0)
            return _small_rider(packed["mine"])

        @staticmethod
        def small_done(got):
            packed["others"], = got

    deltas, new_m, new_v, grads_out, grad, packed = {}, {}, {}, {}, {}, {}
    rows = D_MODEL // LANES
    f1 = _run_rider(_gather_rider(bf16_shards(FFN1)), name="all_gather_ffn1")
    out = _layer_grads(x[0], loss_target[0], ffn1_norm, f1, mix_norm, None, na_rpb[0], sink_logit[0], ffn2_norm,
                       final_norm.reshape(1, D_MODEL), comm=comm)

    me = (4 * lax.axis_index("x") + 2 * lax.axis_index("y") + lax.axis_index("c")).astype(jnp.int32).reshape(1)
    total = _small_sum(packed["mine"], packed["others"], me)
    n_rpb = NA_HEADS * 2 * NA_KH
    grad["ffn1_norm"] = total[0:rows].reshape(1, D_MODEL)
    grad["mix_norm"] = total[rows:2 * rows].reshape(1, D_MODEL)
    grad["ffn2_norm"] = total[2 * rows:3 * rows].reshape(1, D_MODEL)
    grad["final_norm"] = total[3 * rows:4 * rows].reshape(1, D_MODEL)
    grad["na_rpb"] = total[4 * rows:4 * rows + n_rpb].reshape(NA_HEADS, 2 * NA_KH, LANES)[:, :2 * NA_KH - 1, :2 * NA_KW - 1]
    grad["na_rpb"] = grad["na_rpb"].reshape(NA_HEADS, -1)
    grad["sink_logit"] = total[4 * rows + n_rpb:4 * rows + n_rpb + 1, 0:NB_HEADS]
    loss = total[4 * rows + n_rpb + 8, 0]

    small_names = [k for k in WEIGHTS if k not in BIG]
    res = _adamw_small(*[[a[k].reshape(grad[k].shape) for k in small_names] for a in (w, grad, mom, var)])
    for i, k in enumerate(small_names):
        grads_out[k], deltas[k], new_m[k], new_v[k] = (a.reshape(w[k].shape) for a in (grad[k], res[0][i], res[1][i], res[2][i]))
    return (loss, out["dx"].reshape(x.shape), *[grads_out[k] for k in WEIGHTS], *[deltas[k] for k in WEIGHTS],
            *[new_m[k] for k in WEIGHTS], *[new_v[k] for k in WEIGHTS])
```

```python
import math

import jax
import jax.numpy as jnp
import numpy as np
from jax import lax
from jax.experimental import pallas as pl
from jax.experimental.pallas import tpu as pltpu

F32 = jnp.float32
BF16 = jnp.bfloat16

D_MODEL = 1024
HEAD_DIM = 64
NA_HEADS = 8
NB_HEADS = 8
GRID_W = 64
NA_KH = 8
NA_KW = 16
WIN = 128
ROPE_THETA = 10000.0
EPS = 1e-6
N_CHIPS = 4
QK_SCALE = HEAD_DIM ** -0.5
NEG = -1e30
LANES = 128
VMEM_LIMIT = 56 * 1024 * 1024
HEAD_ROWS = 256
MXU_WIDTH = 256
WGRAD_TOKENS_BYTES = 8192

C_QKVA = 3 * NA_HEADS * HEAD_DIM
C_QB = NB_HEADS * HEAD_DIM
C_KB = 2 * HEAD_DIM
C_ROPE = C_QB + C_KB
C_GATES = 2 * D_MODEL
D_IN = C_QKVA + C_QB + 2 * C_KB + C_GATES
O_QB = C_QKVA
O_KB = O_QB + C_QB
O_VB = O_KB + C_KB
O_G = O_VB + C_KB

ADAM_LR = 0.001
ADAM_B1 = 0.9
ADAM_B2 = 0.999
ADAM_EPS = 1e-08
ADAM_WD = 0.01
ADAM_STEP = 10

MESH = pl.DeviceIdType.MESH


def _dot(a, b):
    return jnp.dot(a, b, preferred_element_type=F32)


def _dot_nt(a, b):
    return lax.dot_general(a, b, (((1,), (1,)), ((), ())), preferred_element_type=F32)


def _dot_tn(a, b):
    return lax.dot_general(a, b, (((0,), (0,)), ((), ())), preferred_element_type=F32)


def _params(n_axes):
    return pltpu.CompilerParams(dimension_semantics=("arbitrary",) * n_axes, vmem_limit_bytes=VMEM_LIMIT)


def _rstd(xf):
    return lax.rsqrt(jnp.mean(xf * xf, axis=-1, keepdims=True) + EPS)


def _norm_bwd(dn, xf, g, r):
    xhat = xf * r
    dxh = dn * g
    dx = r * (dxh - xhat * jnp.mean(dxh * xhat, axis=-1, keepdims=True))
    return dx, dn * xhat


def _sigmoid(x):
    return 0.5 * jnp.tanh(0.5 * x) + 0.5


def _loss_head(hf, gv, tgt):
    r = _rstd(hf)
    err = (hf * r) * gv - tgt
    dx, dgr = _norm_bwd(err * (1.0 / hf.shape[-1]), hf, gv, r)
    return 0.5 * jnp.mean(err * err, axis=-1, keepdims=True), dx, dgr


def _ffn_fwd(x, g, wg, wu, wd, *, name, tm=1024, sub=512, rider=None, head=None):
    T, D = x.shape
    F = wg.shape[1]
    tm = min(tm, T)
    sub = min(sub, tm)
    n_head = 0 if head is None else 2

    def body(*refs):
        x_ref, g_ref, wg_ref, wu_ref, wd_ref = refs[:5]
        h_ref, n_ref, hdn_ref, p_ref, q_ref = refs[5 + n_head:10 + n_head]
        i, s = pl.program_id(0), pl.program_id(1)
        _ffn_fwd_step(x_ref, g_ref, wg_ref, wu_ref, wd_ref, h_ref, n_ref, hdn_ref, p_ref, q_ref, s)
        if head is not None:
            gf_ref, t_ref = refs[5:7]
            loss_ref, dgf_ref = refs[10 + n_head:]

            @pl.when((i == 0) & (s == 0))
            def _():
                loss_ref[...] = jnp.zeros_like(loss_ref)
                dgf_ref[...] = jnp.zeros_like(dgf_ref)

            @pl.when(s == N_CHIPS - 1)
            def _():
                for u in range(tm // HEAD_ROWS):
                    r = pl.ds(u * HEAD_ROWS, HEAD_ROWS)
                    terms, dh, dgr = _loss_head(h_ref[r, :], gf_ref[...], t_ref[r, :])
                    loss_ref[...] += jnp.broadcast_to(jnp.sum(terms), loss_ref.shape)
                    dgf_ref[...] += jnp.sum(dgr, axis=0, keepdims=True)
                    h_ref[r, :] = dh

    def _ffn_fwd_step(x_ref, g_ref, wg_ref, wu_ref, wd_ref, h_ref, n_ref, hdn_ref, p_ref, q_ref, s):

        @pl.when(s == 0)
        def _():
            xf = x_ref[...]
            n_ref[...] = ((xf * _rstd(xf)) * g_ref[...]).astype(BF16)
            h_ref[...] = xf

        rows = [pl.ds(u * sub, sub) for u in range(tm // sub)]
        cols = [pl.ds(c0, min(MXU_WIDTH, F - c0)) for c0 in range(0, F, MXU_WIDTH)]
        units = [(r, c) for r in rows for c in cols]
        ab = [(_dot_nt(n_ref[r, :], wg_ref[0, c, :]), _dot_nt(n_ref[r, :], wu_ref[0, c, :])) for r, c in units]
        hdns = []
        for (r, c), (a, b) in zip(units, ab):
            sg = _sigmoid(a)
            silu = a * sg
            hdn = (silu * b).astype(BF16)
            hdn_ref[0, r, c] = hdn
            p_ref[0, r, c] = (b * (sg + silu * (1.0 - sg))).astype(BF16)
            q_ref[0, r, c] = silu.astype(BF16)
            hdns.append(hdn)
        for u, r in enumerate(rows):
            down = [_dot(hdns[u * len(cols) + j], wd_ref[0, c, :]) for j, c in enumerate(cols)]
            h_ref[r, :] += 0.5 * sum(down[1:], down[0])

    tok = pl.BlockSpec((tm, D), lambda i, s: (i, 0))
    hid = pl.BlockSpec((1, tm, F), lambda i, s: (s, i, 0))
    wspec = pl.BlockSpec((1, F, D), lambda i, s: (s, 0, 0))
    hshape = jax.ShapeDtypeStruct((N_CHIPS, T, F), BF16)
    grid = (T // tm, N_CHIPS)
    vec = pl.BlockSpec((1, D), lambda i, s: (0, 0))
    head_in, head_in_specs, head_out, head_out_specs = [], [], [], []
    if head is not None:
        head_in, head_in_specs = list(head), [vec, tok]
        head_out = [jax.ShapeDtypeStruct((1, LANES), F32), jax.ShapeDtypeStruct((1, D), F32)]
        head_out_specs = [pl.BlockSpec((1, LANES), lambda i, s: (0, 0)), vec]
    n_main = 5 + n_head
    body, r_in, r_in_specs, r_out, r_out_specs, scratch = _ride(body, n_main, n_main, rider, grid, (grid[0] * grid[1] * 7) // 8)
    outs = pl.pallas_call(
        body, name=name, grid=grid,
        in_specs=[tok, vec, wspec, wspec, wspec] + head_in_specs + r_in_specs,
        out_specs=[tok, tok, hid, hid, hid] + head_out_specs + r_out_specs,
        out_shape=[jax.ShapeDtypeStruct((T, D), F32), jax.ShapeDtypeStruct((T, D), BF16), hshape, hshape, hshape]
        + head_out + r_out,
        scratch_shapes=scratch,
        compiler_params=_params(2),
    )(x, g, wg, wu, wd, *head_in, *r_in)
    return (*outs[:n_main], list(outs[n_main:]))


def _ffn_bwd(dh, x, g, p, q, wg, wu, wd, *, name, tm=1024, sub=256, rider=None):
    T, D = x.shape
    F = wg.shape[1]
    tm = min(tm, T)
    sub = min(sub, tm)

    def body(dh_ref, x_ref, g_ref, p_ref, q_ref, wg_ref, wu_ref, wd_ref, dx_ref, da_ref, db_ref, dg_ref):
        i, s = pl.program_id(0), pl.program_id(1)

        @pl.when((i == 0) & (s == 0))
        def _():
            dg_ref[...] = jnp.zeros_like(dg_ref)

        @pl.when(s == 0)
        def _():
            dx_ref[...] = jnp.zeros_like(dx_ref)

        rows = [pl.ds(u * sub, sub) for u in range(tm // sub)]
        dhdn = [_dot_nt((0.5 * dh_ref[r, :]).astype(BF16), wd_ref[0]) for r in rows]
        das, dbs = [], []
        for r, dd in zip(rows, dhdn):
            da = (dd * p_ref[0, r, :].astype(F32)).astype(BF16)
            db = (dd * q_ref[0, r, :].astype(F32)).astype(BF16)
            da_ref[0, r, :] = da
            db_ref[0, r, :] = db
            das.append(da)
            dbs.append(db)
        for r, da, db in zip(rows, das, dbs):
            dx_ref[r, :] += _dot(da, wg_ref[0]) + _dot(db, wu_ref[0])

        @pl.when(s == N_CHIPS - 1)
        def _():
            xf = x_ref[...]
            dx, dgr = _norm_bwd(dx_ref[...], xf, g_ref[...], _rstd(xf))
            dg_ref[...] += jnp.sum(dgr, axis=0, keepdims=True)
            dx_ref[...] = dh_ref[...] + dx

    tok = pl.BlockSpec((tm, D), lambda i, s: (i, 0))
    hid = pl.BlockSpec((1, tm, F), lambda i, s: (s, i, 0))
    vec = pl.BlockSpec((1, D), lambda i, s: (0, 0))
    hshape = jax.ShapeDtypeStruct((N_CHIPS, T, F), BF16)
    wspec = pl.BlockSpec((1, F, D), lambda i, s: (s, 0, 0))
    grid = (T // tm, N_CHIPS)
    body, r_in, r_in_specs, r_out, r_out_specs, scratch = _ride(body, 8, 4, rider, grid, None)
    outs = pl.pallas_call(
        body, name=name, grid=grid,
        in_specs=[tok, tok, vec, hid, hid, wspec, wspec, wspec] + r_in_specs,
        out_specs=[tok, hid, hid, vec] + r_out_specs,
        out_shape=[jax.ShapeDtypeStruct((T, D), F32), hshape, hshape, jax.ShapeDtypeStruct((1, D), F32)] + r_out,
        scratch_shapes=scratch,
        compiler_params=_params(2),
    )(dh, x, g, p, q, wg, wu, wd, *r_in)
    return (*outs[:4], list(outs[4:]))


def _wgrad(a, b, *, a_block, a_map, b_block, b_map, out_shape, o_block, o_map, grid, scale=1.0, name, rider=None):
    def body(a_ref, b_ref, o_ref):
        @pl.when(pl.program_id(len(grid) - 1) == 0)
        def _():
            o_ref[...] = jnp.zeros_like(o_ref)

        av = a_ref[...]
        bv = b_ref[...]
        av = av.reshape(av.shape[-2:]).astype(BF16)
        bv = bv.reshape(bv.shape[-2:])
        if scale != 1.0:
            bv = scale * bv
        o_ref[...] += _dot_tn(av, bv.astype(BF16)).reshape(o_ref.shape)

    body, r_in, r_in_specs, r_out, r_out_specs, scratch = _ride(body, 2, 1, rider, grid, None)
    outs = pl.pallas_call(
        body, name=name, grid=grid,
        in_specs=[pl.BlockSpec(a_block, a_map), pl.BlockSpec(b_block, b_map)] + r_in_specs,
        out_specs=[pl.BlockSpec(o_block, o_map)] + r_out_specs,
        out_shape=[jax.ShapeDtypeStruct(out_shape, F32)] + r_out,
        scratch_shapes=scratch,
        compiler_params=_params(len(grid)),
    )(a, b, *r_in)
    return outs[0], list(outs[1:])


def _wgrad_rows(a, b, n_blocks, *, name, tk=2048):
    T, N = b.shape
    M = a.shape[1] // n_blocks
    tk = min(tk, T)
    return _wgrad(a, b, a_block=(tk, M), a_map=lambda s, k: (k, s), b_block=(tk, N), b_map=lambda s, k: (k, 0),
                  out_shape=(n_blocks, M, N), o_block=(1, M, N), o_map=lambda s, k: (s, 0, 0), grid=(n_blocks, T // tk), name=name)


def _wgrad_shard_a(a, b, *, name, scale=1.0, rider=None):
    S, T, M = a.shape
    N = b.shape[1]
    tk = min(WGRAD_TOKENS_BYTES // b.dtype.itemsize, T)
    return _wgrad(a, b, a_block=(1, tk, M), a_map=lambda s, k: (s, k, 0), b_block=(tk, N), b_map=lambda s, k: (k, 0),
                  out_shape=(S, M, N), o_block=(1, M, N), o_map=lambda s, k: (s, 0, 0), grid=(S, T // tk), scale=scale,
                  name=name, rider=rider)


def _wgrad_cols(a, b, n_blocks, *, name, tk=2048):
    T, M = a.shape
    N = b.shape[1] // n_blocks
    tk = min(tk, T)

    def body(a_ref, b_ref, o_ref):
        @pl.when(pl.program_id(0) == 0)
        def _():
            o_ref[...] = jnp.zeros_like(o_ref)

        r = _dot_tn(a_ref[...].astype(BF16), b_ref[...].astype(BF16))
        for s in range(n_blocks):
            o_ref[s] += r[:, s * N:(s + 1) * N]

    return pl.pallas_call(
        body, name=name, grid=(T // tk,),
        in_specs=[pl.BlockSpec((tk, M), lambda k: (k, 0)), pl.BlockSpec((tk, n_blocks * N), lambda k: (k, 0))],
        out_specs=pl.BlockSpec((n_blocks, M, N), lambda k: (0, 0, 0)),
        out_shape=jax.ShapeDtypeStruct((n_blocks, M, N), F32),
        compiler_params=_params(1),
    )(a, b)


def _rope_tables(T):
    half = HEAD_DIM // 2
    inv = np.float32(ROPE_THETA) ** (-np.arange(half, dtype=np.float32) / np.float32(half))
    ang = np.arange(T, dtype=np.float32)[:, None] * inv[None, :]
    cos, sin, zero = np.cos(ang), np.sin(ang), np.zeros_like(ang)
    reps = LANES // HEAD_DIM
    return (jnp.asarray(np.tile(np.concatenate([cos, cos], axis=1), (1, reps))),
            jnp.asarray(np.tile(np.concatenate([-sin, zero], axis=1), (1, reps))),
            jnp.asarray(np.tile(np.concatenate([zero, sin], axis=1), (1, reps))))


def _rope(x, cos, sa, sb, sign):
    half = HEAD_DIM // 2
    return x * cos + sign * (pltpu.roll(x, LANES - half, 1) * sa + pltpu.roll(x, half, 1) * sb)


def _mix_in_fwd(h, g, w_in, tables, *, tm=512):
    T, D = h.shape

    def body(h_ref, g_ref, w_ref, cos_ref, sa_ref, sb_ref, u_ref, qkva_ref, qb_ref, kvb_ref, gates_ref):
        hf = h_ref[...]
        u = ((hf * _rstd(hf)) * g_ref[...]).astype(BF16)
        u_ref[...] = u
        qkva_ref[...] = _dot_nt(u, w_ref[0:C_QKVA, :]).astype(BF16)
        zr = _dot_nt(u, w_ref[O_QB:O_QB + C_ROPE, :])
        cos, sa, sb = cos_ref[...], sa_ref[...], sb_ref[...]
        for j in range(C_ROPE // LANES):
            rj = _rope(zr[:, j * LANES:(j + 1) * LANES], cos, sa, sb, 1.0).astype(BF16)
            if j < C_QB // LANES:
                qb_ref[:, j * LANES:(j + 1) * LANES] = rj
            else:
                kvb_ref[:, 0:C_KB] = rj
        kvb_ref[:, C_KB:2 * C_KB] = _dot_nt(u, w_ref[O_VB:O_VB + C_KB, :]).astype(BF16)
        gates_ref[...] = _dot_nt(u, w_ref[O_G:O_G + C_GATES, :])

    def tok(n):
        return pl.BlockSpec((tm, n), lambda i: (i, 0))

    return pl.pallas_call(
        body, name="mix_in_fwd", grid=(T // tm,),
        in_specs=[tok(D), pl.BlockSpec((1, D), lambda i: (0, 0)), pl.BlockSpec((D_IN, D), lambda i: (0, 0), pipeline_mode=pl.Buffered(1)),
                  tok(LANES), tok(LANES), tok(LANES)],
        out_specs=[tok(D), tok(C_QKVA), tok(C_QB), tok(2 * C_KB), tok(C_GATES)],
        out_shape=[jax.ShapeDtypeStruct((T, D), BF16), jax.ShapeDtypeStruct((T, C_QKVA), BF16),
                   jax.ShapeDtypeStruct((T, C_QB), BF16), jax.ShapeDtypeStruct((T, 2 * C_KB), BF16),
                   jax.ShapeDtypeStruct((T, C_GATES), F32)],
        compiler_params=_params(1),
    )(h, g, w_in, *tables)


def _mix_in_bwd(dqa, dka, dva, dqb, dkb, dvb, dgates, h, g, dres, w_in, tables, *, tm=512, rider=None):
    T, D = h.shape

    def body(dqa_ref, dka_ref, dva_ref, dqb_ref, dkb_ref, dvb_ref, dgt_ref, h_ref, g_ref, dres_ref, w_ref,
             cos_ref, sa_ref, sb_ref, dz_ref, dh_ref, dg_ref):
        @pl.when(pl.program_id(0) == 0)
        def _():
            dg_ref[...] = jnp.zeros_like(dg_ref)

        na = NA_HEADS * HEAD_DIM
        dz_ref[:, 0:na] = dqa_ref[...].astype(BF16)
        dz_ref[:, na:2 * na] = dka_ref[...].astype(BF16)
        dz_ref[:, 2 * na:3 * na] = dva_ref[...].astype(BF16)
        cos, sa, sb = cos_ref[...], sa_ref[...], sb_ref[...]
        for j in range(C_QB // LANES):
            dz_ref[:, O_QB + j * LANES:O_QB + (j + 1) * LANES] = _rope(
                dqb_ref[:, j * LANES:(j + 1) * LANES], cos, sa, sb, -1.0).astype(BF16)
        dz_ref[:, O_KB:O_KB + C_KB] = _rope(dkb_ref[...], cos, sa, sb, -1.0).astype(BF16)
        dz_ref[:, O_VB:O_VB + C_KB] = dvb_ref[...].astype(BF16)
        dz_ref[:, O_G:O_G + C_GATES] = dgt_ref[...].astype(BF16)
        du = _dot(dz_ref[...], w_ref[...])
        hf = h_ref[...]
        dx, dgr = _norm_bwd(du, hf, g_ref[...], _rstd(hf))
        dg_ref[...] += jnp.sum(dgr, axis=0, keepdims=True)
        dh_ref[...] = dres_ref[...] + dx

    def tok(n):
        return pl.BlockSpec((tm, n), lambda i: (i, 0))

    vec = pl.BlockSpec((1, D), lambda i: (0, 0))
    na = NA_HEADS * HEAD_DIM
    grid = (T // tm,)
    body, r_in, r_in_specs, r_out, r_out_specs, scratch = _ride(body, 14, 3, rider, grid, None)
    outs = pl.pallas_call(
        body, name="mix_in_bwd", grid=grid,
        in_specs=[tok(na), tok(na), tok(na), tok(C_QB), tok(C_KB), tok(C_KB), tok(C_GATES), tok(D), vec, tok(D),
                  pl.BlockSpec((D_IN, D), lambda i: (0, 0), pipeline_mode=pl.Buffered(1)), tok(LANES), tok(LANES), tok(LANES)]
        + r_in_specs,
        out_specs=[tok(D_IN), tok(D), vec] + r_out_specs,
        out_shape=[jax.ShapeDtypeStruct((T, D_IN), BF16), jax.ShapeDtypeStruct((T, D), F32),
                   jax.ShapeDtypeStruct((1, D), F32)] + r_out,
        scratch_shapes=scratch,
        compiler_params=_params(1),
    )(dqa, dka, dva, dqb, dkb, dvb, dgates, h, g, dres, w_in, *tables, *r_in)
    return (*outs[:3], list(outs[3:]))


def _na_bias_slabs(rpb):
    H = rpb.shape[0]
    ncell = GRID_W * GRID_W
    cell = np.arange(ncell)
    co = cell % GRID_W - cell // GRID_W + (NA_KW - 1)
    e_co = jnp.asarray((np.arange(LANES)[:, None] == co[None, :]).astype(np.float32))
    table = jnp.pad(rpb, ((0, 0), (0, 1), (0, LANES - rpb.shape[2]))).reshape(H * 2 * NA_KH, LANES)

    def body(t_ref, e_ref, o_ref):
        o_ref[...] = jnp.dot(t_ref[...], e_ref[...], preferred_element_type=F32, precision=lax.Precision.HIGHEST)

    toeplitz = pl.pallas_call(
        body, name="rpb_unfold", out_shape=jax.ShapeDtypeStruct((H * 2 * NA_KH, ncell), F32),
        compiler_params=_params(0),
    )(table, e_co).reshape(H, 2 * NA_KH, GRID_W, GRID_W)

    def assemble(tz_ref, o_ref):
        c = lax.broadcasted_iota(jnp.int32, (GRID_W, GRID_W), 0)
        k = lax.broadcasted_iota(jnp.int32, (GRID_W, GRID_W), 1)
        cs = jnp.clip(c - NA_KW // 2, 0, GRID_W - NA_KW)
        inwin = (k >= cs) & (k < cs + NA_KW)
        for ro0 in range(NA_KH):
            for hh in range(2):
                for i in range(NA_KH):
                    o_ref[0, ro0, hh * GRID_W:(hh + 1) * GRID_W, i * GRID_W:(i + 1) * GRID_W] = jnp.where(
                        inwin, tz_ref[hh, ro0 + i], NEG)

    return pl.pallas_call(
        assemble, name="na_bias_slabs", grid=(H // 2,),
        in_specs=[pl.BlockSpec((2, 2 * NA_KH, GRID_W, GRID_W), lambda p: (p, 0, 0, 0))],
        out_specs=pl.BlockSpec((1, NA_KH, 2 * GRID_W, NA_KH * GRID_W), lambda p: (p, 0, 0, 0)),
        out_shape=jax.ShapeDtypeStruct((H // 2, NA_KH, 2 * GRID_W, NA_KH * GRID_W), F32),
        compiler_params=_params(1),
    )(toeplitz)


def _half_masks(rows):
    lane = lax.broadcasted_iota(jnp.int32, (rows, LANES), 1)
    left = lane < HEAD_DIM
    return left, (left, jnp.logical_not(left))


def _stack_heads(x):
    left, halves = _half_masks(x.shape[0])
    xf = x.astype(F32)
    return jnp.concatenate([jnp.where(m, xf, 0.0).astype(BF16) for m in halves], axis=0)


def _unstack_heads(o):
    rows = o.shape[0] // 2
    left, _ = _half_masks(rows)
    return jnp.where(left, o[:rows], o[rows:])


def _na_row(j, t, rb, rows):
    r = j * rb + t
    rs = jnp.clip(r - NA_KH // 2, 0, rows - NA_KH)
    return pl.multiple_of(t * GRID_W, GRID_W), pl.multiple_of(rs * GRID_W, GRID_W), rs - r + (NA_KH - 1)


def _na_specs(T, rb):
    qrows = GRID_W * rb
    pairs = NA_HEADS // 2
    return ([pl.BlockSpec((qrows, LANES), lambda p, j: (j, p)),
             pl.BlockSpec((T, LANES), lambda p, j: (0, pairs + p)),
             pl.BlockSpec((T, LANES), lambda p, j: (0, 2 * pairs + p))],
            pl.BlockSpec((1, NA_KH, 2 * GRID_W, NA_KH * GRID_W), lambda p, j: (p, 0, 0, 0)))


def _softmax(s):
    p = jnp.exp(s - jnp.max(s, axis=-1, keepdims=True))
    return p / jnp.sum(p, axis=-1, keepdims=True)


def _na_fwd(qkva, bias, *, rb=32, group=32):
    T = qkva.shape[0]
    rows = T // GRID_W
    nkeys = NA_KH * GRID_W
    rb = min(rb, rows)
    group = min(group, rb)

    def body(q_ref, k_ref, v_ref, bias_ref, y_ref):
        j = pl.program_id(1)

        def rows_step(t, carry):
            at = [_na_row(j, t * group + u, rb, rows) for u in range(group)]
            s = [_dot_nt(_stack_heads(q_ref[pl.ds(q0, GRID_W), :]), k_ref[pl.ds(k0, nkeys), :]) for q0, k0, _ in at]
            p = [_softmax(su * QK_SCALE + bias_ref[0, ro0]) for su, (_, _, ro0) in zip(s, at)]
            o = [_dot(pu.astype(BF16), v_ref[pl.ds(k0, nkeys), :]) for pu, (_, k0, _) in zip(p, at)]
            for ou, (q0, _, _) in zip(o, at):
                y_ref[pl.ds(q0, GRID_W), :] = _unstack_heads(ou).astype(BF16)
            return carry

        lax.fori_loop(0, rb // group, rows_step, 0)

    qkv_specs, bias_spec = _na_specs(T, rb)
    return pl.pallas_call(
        body, name="na_fwd", grid=(NA_HEADS // 2, rows // rb),
        in_specs=qkv_specs + [bias_spec],
        out_specs=qkv_specs[0],
        out_shape=jax.ShapeDtypeStruct((T, NA_HEADS * HEAD_DIM), BF16),
        compiler_params=_params(2),
    )(qkva, qkva, qkva, bias)


def _na_bwd(qkva, dy, bias, *, rb=16, group=16, rider=None):
    T = qkva.shape[0]
    rows = T // GRID_W
    nkeys = NA_KH * GRID_W
    rb = min(rb, rows)
    group = min(group, rb)

    def body(q_ref, k_ref, v_ref, dy_ref, bias_ref, dq_ref, dk_ref, dv_ref, dbias_ref):
        j = pl.program_id(1)

        @pl.when(j == 0)
        def _():
            dk_ref[...] = jnp.zeros_like(dk_ref)
            dv_ref[...] = jnp.zeros_like(dv_ref)
            dbias_ref[...] = jnp.zeros_like(dbias_ref)

        def rows_step(t, carry):
            at = [_na_row(j, t * group + u, rb, rows) for u in range(group)]
            qs = [_stack_heads(q_ref[pl.ds(q0, GRID_W), :]) for q0, _, _ in at]
            dys = [_stack_heads(dy_ref[pl.ds(q0, GRID_W), :]) for q0, _, _ in at]
            s = [_dot_nt(qu, k_ref[pl.ds(k0, nkeys), :]) for qu, (_, k0, _) in zip(qs, at)]
            dp = [_dot_nt(du, v_ref[pl.ds(k0, nkeys), :]) for du, (_, k0, _) in zip(dys, at)]
            p = [_softmax(su * QK_SCALE + bias_ref[0, ro0]) for su, (_, _, ro0) in zip(s, at)]
            ds = [pu * (du - jnp.sum(pu * du, axis=-1, keepdims=True)) for pu, du in zip(p, dp)]
            for u, (q0, k0, ro0) in enumerate(at):
                dbias_ref[0, ro0] += ds[u]
                dsb = ds[u].astype(BF16)
                dq_ref[pl.ds(q0, GRID_W), :] = (_unstack_heads(_dot(dsb, k_ref[pl.ds(k0, nkeys), :])) * QK_SCALE).astype(BF16)
                dk_ref[pl.ds(k0, nkeys), :] += _dot_tn(dsb, qs[u]) * QK_SCALE
                dv_ref[pl.ds(k0, nkeys), :] += _dot_tn(p[u].astype(BF16), dys[u])
            return carry

        lax.fori_loop(0, rb // group, rows_step, 0)

    qkv_specs, bias_spec = _na_specs(T, rb)
    width = NA_HEADS * HEAD_DIM
    kv_out = pl.BlockSpec((T, LANES), lambda p, j: (0, p))
    grid = (NA_HEADS // 2, rows // rb)
    body, r_in, r_in_specs, r_out, r_out_specs, scratch = _ride(body, 5, 4, rider, grid, None)
    outs = pl.pallas_call(
        body, name="na_bwd", grid=grid,
        in_specs=qkv_specs + [qkv_specs[0], bias_spec] + r_in_specs,
        out_specs=[qkv_specs[0], kv_out, kv_out, bias_spec] + r_out_specs,
        out_shape=[jax.ShapeDtypeStruct((T, width), BF16), jax.ShapeDtypeStruct((T, width), F32),
                   jax.ShapeDtypeStruct((T, width), F32), jax.ShapeDtypeStruct(bias.shape, F32)] + r_out,
        scratch_shapes=scratch,
        compiler_params=_params(2),
    )(qkva, qkva, qkva, dy, bias, *r_in)
    return (*outs[:4], list(outs[4:]))


def _rpb_fold(dslab):
    pairs = dslab.shape[0]
    H = 2 * pairs
    ncell = GRID_W * GRID_W

    def disassemble(d_ref, tz_ref):
        tz_ref[...] = jnp.zeros_like(tz_ref)
        for ro0 in range(NA_KH):
            for hh in range(2):
                for i in range(NA_KH):
                    tz_ref[hh, ro0 + i] += d_ref[0, ro0, hh * GRID_W:(hh + 1) * GRID_W, i * GRID_W:(i + 1) * GRID_W]

    dtoeplitz = pl.pallas_call(
        disassemble, name="rpb_fold_tiles", grid=(pairs,),
        in_specs=[pl.BlockSpec((1, NA_KH, 2 * GRID_W, NA_KH * GRID_W), lambda p: (p, 0, 0, 0))],
        out_specs=pl.BlockSpec((2, 2 * NA_KH, GRID_W, GRID_W), lambda p: (p, 0, 0, 0)),
        out_shape=jax.ShapeDtypeStruct((H, 2 * NA_KH, GRID_W, GRID_W), F32),
        compiler_params=_params(1),
    )(dslab).reshape(H * 2 * NA_KH, ncell)
    cell = np.arange(ncell)
    co = cell % GRID_W - cell // GRID_W + (NA_KW - 1)
    e_co = jnp.asarray((co[:, None] == np.arange(LANES)[None, :]).astype(np.float32))

    def diagonals(x_ref, e_ref, o_ref):
        o_ref[...] = jnp.dot(x_ref[...], e_ref[...], preferred_element_type=F32, precision=lax.Precision.HIGHEST)

    return pl.pallas_call(
        diagonals, name="rpb_fold", out_shape=jax.ShapeDtypeStruct((H * 2 * NA_KH, LANES), F32),
        compiler_params=_params(0),
    )(dtoeplitz, e_co).reshape(H, 2 * NA_KH, LANES)


SWA_KEYS = 3 * WIN


def _swa_block(j, t, qbn, T):
    blk = j * qbn + t
    start = jnp.clip((blk - 1) * WIN, 0, T - SWA_KEYS)
    row = lax.broadcasted_iota(jnp.int32, (2 * WIN, SWA_KEYS), 0)
    qpos = blk * WIN + jnp.where(row < WIN, row, row - WIN)
    kpos = start + lax.broadcasted_iota(jnp.int32, (2 * WIN, SWA_KEYS), 1)
    return pl.multiple_of(t * WIN, WIN), pl.multiple_of(start, WIN), jnp.abs(qpos - kpos) <= WIN


def _swa_sinks(sink_ref, p):
    row = lax.broadcasted_iota(jnp.int32, (2 * WIN, 1), 0)
    return jnp.where(row < WIN, sink_ref[p], sink_ref[p + NB_HEADS // 2])


def _swa_probs(s, mask, sink):
    s = jnp.where(mask, s * QK_SCALE, NEG)
    m = jnp.maximum(jnp.max(s, axis=-1, keepdims=True), sink)
    e = jnp.exp(s - m)
    esink = jnp.exp(sink - m)
    den = jnp.sum(e, axis=-1, keepdims=True) + esink
    return e / den, esink / den


def _swa_specs(T, qbn):
    return [pl.BlockSpec(memory_space=pltpu.SMEM),
            pl.BlockSpec((WIN * qbn, LANES), lambda p, j: (j, p)),
            pl.BlockSpec((T, LANES), lambda p, j: (0, 0)),
            pl.BlockSpec((T, LANES), lambda p, j: (0, 1))]


def _swa_fwd(qb, kvb, sink, *, qbn=32, group=32):
    T = qb.shape[0]
    pairs = NB_HEADS // 2
    qbn = min(qbn, T // WIN)
    group = min(group, qbn)

    def body(sink_ref, q_ref, k_ref, v_ref, y_ref):
        p, j = pl.program_id(0), pl.program_id(1)
        sinks = _swa_sinks(sink_ref, p)

        def blocks_step(t, carry):
            at = [_swa_block(j, t * group + u, qbn, T) for u in range(group)]
            s = [_dot_nt(_stack_heads(q_ref[pl.ds(q0, WIN), :]), k_ref[pl.ds(k0, SWA_KEYS), :]) for q0, k0, _ in at]
            pr = [_swa_probs(su, mask, sinks)[0] for su, (_, _, mask) in zip(s, at)]
            o = [_dot(pu.astype(BF16), v_ref[pl.ds(k0, SWA_KEYS), :]) for pu, (_, k0, _) in zip(pr, at)]
            for ou, (q0, _, _) in zip(o, at):
                y_ref[pl.ds(q0, WIN), :] = _unstack_heads(ou).astype(BF16)
            return carry

        lax.fori_loop(0, qbn // group, blocks_step, 0)

    specs = _swa_specs(T, qbn)
    return pl.pallas_call(
        body, name="swa_fwd", grid=(pairs, T // (WIN * qbn)),
        in_specs=specs, out_specs=specs[1],
        out_shape=jax.ShapeDtypeStruct((T, NB_HEADS * HEAD_DIM), BF16),
        compiler_params=_params(2),
    )(sink, qb, kvb, kvb)


def _swa_bwd(qb, kvb, dy, sink, *, qbn=16, group=16, rider=None):
    T = qb.shape[0]
    pairs = NB_HEADS // 2
    qbn = min(qbn, T // WIN)
    group = min(group, qbn)

    def body(sink_ref, q_ref, k_ref, v_ref, dy_ref, dq_ref, dk_ref, dv_ref, dsink_ref):
        p, j = pl.program_id(0), pl.program_id(1)
        sinks = _swa_sinks(sink_ref, p)

        @pl.when((p == 0) & (j == 0))
        def _():
            dk_ref[...] = jnp.zeros_like(dk_ref)
            dv_ref[...] = jnp.zeros_like(dv_ref)

        @pl.when(j == 0)
        def _():
            dsink_ref[...] = jnp.zeros_like(dsink_ref)

        def blocks_step(t, carry):
            at = [_swa_block(j, t * group + u, qbn, T) for u in range(group)]
            qs = [_stack_heads(q_ref[pl.ds(q0, WIN), :]) for q0, _, _ in at]
            dys = [_stack_heads(dy_ref[pl.ds(q0, WIN), :]) for q0, _, _ in at]
            s = [_dot_nt(qu, k_ref[pl.ds(k0, SWA_KEYS), :]) for qu, (_, k0, _) in zip(qs, at)]
            dp = [_dot_nt(du, v_ref[pl.ds(k0, SWA_KEYS), :]) for du, (_, k0, _) in zip(dys, at)]
            probs = [_swa_probs(su, mask, sinks) for su, (_, _, mask) in zip(s, at)]
            for u, (q0, k0, _) in enumerate(at):
                pr, psink = probs[u]
                delta = jnp.sum(pr * dp[u], axis=-1, keepdims=True)
                dsb = (pr * (dp[u] - delta)).astype(BF16)
                dsk = psink * delta
                for hh in range(2):
                    dsink_ref[0, hh:hh + 1, :] += jnp.broadcast_to(-jnp.sum(dsk[hh * WIN:(hh + 1) * WIN]), (1, LANES))
                dq_ref[pl.ds(q0, WIN), :] = _unstack_heads(_dot(dsb, k_ref[pl.ds(k0, SWA_KEYS), :])) * QK_SCALE
                dk_ref[pl.ds(k0, SWA_KEYS), :] += _dot_tn(dsb, qs[u]) * QK_SCALE
                dv_ref[pl.ds(k0, SWA_KEYS), :] += _dot_tn(pr.astype(BF16), dys[u])
            return carry

        lax.fori_loop(0, qbn // group, blocks_step, 0)

    specs = _swa_specs(T, qbn)
    kv_out = pl.BlockSpec((T, LANES), lambda p, j: (0, 0))
    grid = (pairs, T // (WIN * qbn))
    body, r_in, r_in_specs, r_out, r_out_specs, scratch = _ride(body, 5, 4, rider, grid, None)
    outs = pl.pallas_call(
        body, name="swa_bwd", grid=grid,
        in_specs=specs + [specs[1]] + r_in_specs,
        out_specs=[specs[1], kv_out, kv_out, pl.BlockSpec((1, 8, LANES), lambda p, j: (p, 0, 0))] + r_out_specs,
        out_shape=[jax.ShapeDtypeStruct((T, NB_HEADS * HEAD_DIM), F32), jax.ShapeDtypeStruct((T, LANES), F32),
                   jax.ShapeDtypeStruct((T, LANES), F32), jax.ShapeDtypeStruct((pairs, 8, LANES), F32)] + r_out,
        scratch_shapes=scratch,
        compiler_params=_params(2),
    )(sink, qb, kvb, kvb, dy, *r_in)
    return (*outs[:4], list(outs[4:]))


def _merge_fwd(ya, yb, gates, wa, wb, wout, h, *, tm=512):
    T, D = h.shape
    W = ya.shape[1]

    def body(ya_ref, yb_ref, gt_ref, wa_ref, wb_ref, wo_ref, h_ref, h2_ref, mg_ref):
        pa = _dot(ya_ref[...], wa_ref[...])
        pb = _dot(yb_ref[...], wb_ref[...])
        mg = (jax.nn.sigmoid(gt_ref[:, 0:D]) * pa + jax.nn.sigmoid(gt_ref[:, D:2 * D]) * pb).astype(BF16)
        mg_ref[...] = mg
        h2_ref[...] = h_ref[...] + _dot(mg, wo_ref[...])

    def tok(n):
        return pl.BlockSpec((tm, n), lambda i: (i, 0))

    def full(r, c):
        return pl.BlockSpec((r, c), lambda i: (0, 0))

    return pl.pallas_call(
        body, name="merge_fwd", grid=(T // tm,),
        in_specs=[tok(W), tok(W), tok(2 * D), full(W, D), full(W, D), full(D, D), tok(D)],
        out_specs=[tok(D), tok(D)],
        out_shape=[jax.ShapeDtypeStruct((T, D), F32), jax.ShapeDtypeStruct((T, D), BF16)],
        compiler_params=_params(1),
    )(ya, yb, gates, wa, wb, wout, h)


def _merge_bwd(dh, ya, yb, gates, wa, wb, wout, *, tm=512, rider=None):
    T, D = dh.shape
    W = ya.shape[1]

    def body(dh_ref, ya_ref, yb_ref, gt_ref, wa_ref, wb_ref, wo_ref, dya_ref, dyb_ref, dpa_ref, dpb_ref, dgt_ref):
        dmg = _dot_nt(dh_ref[...].astype(BF16), wo_ref[...])
        for y_ref, w_ref, dy_ref, dp_ref, lo in ((ya_ref, wa_ref, dya_ref, dpa_ref, 0), (yb_ref, wb_ref, dyb_ref, dpb_ref, D)):
            sg = jax.nn.sigmoid(gt_ref[:, lo:lo + D])
            dp = (dmg * sg).astype(BF16)
            dp_ref[...] = dp
            dgt_ref[:, lo:lo + D] = (dmg * _dot(y_ref[...], w_ref[...]) * (sg * (1.0 - sg))).astype(BF16)
            dy_ref[...] = _dot_nt(dp, w_ref[...]).astype(BF16)

    def tok(n):
        return pl.BlockSpec((tm, n), lambda i: (i, 0))

    def full(r, c):
        return pl.BlockSpec((r, c), lambda i: (0, 0))

    grid = (T // tm,)
    body, r_in, r_in_specs, r_out, r_out_specs, scratch = _ride(body, 7, 5, rider, grid, None)
    outs = pl.pallas_call(
        body, name="merge_bwd", grid=grid,
        in_specs=[tok(D), tok(W), tok(W), tok(2 * D), full(W, D), full(W, D), full(D, D)] + r_in_specs,
        out_specs=[tok(W), tok(W), tok(D), tok(D), tok(2 * D)] + r_out_specs,
        out_shape=[jax.ShapeDtypeStruct((T, W), BF16), jax.ShapeDtypeStruct((T, W), BF16),
                   jax.ShapeDtypeStruct((T, D), BF16), jax.ShapeDtypeStruct((T, D), BF16),
                   jax.ShapeDtypeStruct((T, 2 * D), BF16)] + r_out,
        scratch_shapes=scratch,
        compiler_params=_params(1),
    )(dh, ya, yb, gates, wa, wb, wout, *r_in)
    return (*outs[:5], list(outs[5:]))


def _pair_heads(a, axis):
    shp = a.shape
    a = a.reshape(shp[:axis] + (2, NB_HEADS // 2, HEAD_DIM) + shp[axis + 1:])
    return jnp.swapaxes(a, axis, axis + 1).reshape(shp)


def _unpair_heads(a, axis):
    shp = a.shape
    a = a.reshape(shp[:axis] + (NB_HEADS // 2, 2, HEAD_DIM) + shp[axis + 1:])
    return jnp.swapaxes(a, axis, axis + 1).reshape(shp)


FFN1 = ("ffn1_w_gate", "ffn1_w_up", "ffn1_w_down")
FFN2 = ("ffn2_w_gate", "ffn2_w_up", "ffn2_w_down")
MIXER = ("w_in", "w_branch_a", "w_branch_b", "w_out")
BRANCH = MIXER[1:]


def _layer_grads(x, target, g1, f1, gmix, late, rpb, sink, g2, gfin, comm=None):
    T = x.shape[0]
    tables = _rope_tables(T)
    bias = _na_bias_slabs(rpb)

    comm = comm or _Local(late)
    h1, n1, hdn1, p1, q1, gathered = _ffn_fwd(x, g1, *f1, name="ffn1_fwd", rider=comm.late_rider)
    w_in_t, wa, wb, wout, f2 = comm.late(gathered)
    w_in_p = jnp.concatenate([w_in_t[:O_QB], _pair_heads(w_in_t[O_QB:O_KB], 0), w_in_t[O_KB:]], axis=0)
    wb_p = _pair_heads(wb, 0)
    u, qkva, qb, kvb, gates = _mix_in_fwd(h1, gmix, w_in_p, tables)
    ya = _na_fwd(qkva, bias)
    yb = _swa_fwd(qb, kvb, sink)
    h2, merged = _merge_fwd(ya, yb, gates, wa, wb_p, wout, h1)
    dh3, n2, hdn2, p2, q2, loss, dgfin, _ = _ffn_fwd(h2, g2, *f2, name="ffn2_fwd", head=(gfin, target))

    dh2, da2, db2, dg2, _ = _ffn_bwd(dh3, h2, g2, p2, q2, *f2, name="ffn2_bwd")
    df2 = [_wgrad_shard_a(da2, n2, name="ffn2_dwg")[0], _wgrad_shard_a(db2, n2, name="ffn2_dwu")[0],
           _wgrad_shard_a(hdn2, dh3, scale=0.5, name="ffn2_dwd")[0]]
    red2 = comm.reduce(FFN2, df2, tag="ffn2")
    dya, dyb, dpa, dpb, dgates, _ = _merge_bwd(dh2, ya, yb, gates, wa, wb_p, wout)
    dwout = _wgrad_cols(merged, dh2, 1, name="dwout").reshape(N_CHIPS, D_MODEL // N_CHIPS, D_MODEL)
    dwa = _wgrad_cols(ya, dpa, N_CHIPS, name="dwa")
    dwb = _unpair_heads(_wgrad_cols(yb, dpb, N_CHIPS, name="dwb"), 1)
    redb = comm.reduce(BRANCH, [dwa, dwb, dwout], tag="branch")
    dqa, dka, dva, dbias, got = _na_bwd(qkva, dya, bias, rider=_two_riders(red2.sibling, redb.sibling))
    red2.partial(got[:len(FFN2)])
    redb.partial(got[len(FFN2):])
    drpb = _rpb_fold(dbias)
    dqb, dkb, dvb, dsink, got = _swa_bwd(qb, kvb, dyb, sink, rider=_two_riders(red2.chips, redb.chips))
    red2.halves(got[:len(FFN2)])
    redb.halves(got[len(FFN2):])
    dz, dh1, dgmix, got = _mix_in_bwd(dqa, dka, dva, dqb, dkb, dvb, dgates, h1, gmix, dh2, w_in_p, tables,
                                      rider=_two_riders(red2.share, redb.share))
    out = red2.result(got[:len(FFN2)])
    out.update(redb.result(got[len(FFN2):]))
    dwin_p = _wgrad_rows(dz, u, 2, name="dwin")[0].reshape(D_IN, D_MODEL)
    dwin = jnp.concatenate([dwin_p[:O_QB], _unpair_heads(dwin_p[O_QB:O_KB], 0), dwin_p[O_KB:]], axis=0)
    dx, da1, db1, dg1, _ = _ffn_bwd(dh1, x, g1, p1, q1, *f1, name="ffn1_bwd")
    small = dict(loss=loss, ffn1_norm=dg1, mix_norm=dgmix, ffn2_norm=dg2, final_norm=dgfin, na_rpb=drpb,
                 sink_logit=dsink[:, 0:2, 0].T.reshape(NB_HEADS))
    redw = comm.reduce(("w_in",), [dwin.reshape(N_CHIPS, D_IN // N_CHIPS, D_MODEL)], tag="w_in")
    dwg1, got = _wgrad_shard_a(da1, n1, name="ffn1_dwg", rider=_two_riders(redw.sibling, comm.small(small)))
    redw.partial(got[:1])
    comm.small_done(got[1:])
    dwu1, got = _wgrad_shard_a(db1, n1, name="ffn1_dwu", rider=redw.chips)
    redw.halves(got)
    red1 = comm.reduce(FFN1[:2], [dwg1, dwu1], tag="ffn1_gate_up").partial_now()
    dwd1, got = _wgrad_shard_a(hdn1, dh1, scale=0.5, name="ffn1_dwd", rider=_two_riders(red1.chips, redw.share))
    out.update(redw.result(got[2:]))
    red1.halves(got[:2])
    redd = comm.reduce(FFN1[2:], [dwd1], tag="ffn1_down")
    got = _run_rider(_two_riders(red1.share, redd.sibling), name="rs_share_gate_up_sibling_down")
    out.update(red1.result(got[:2]))
    out.update(redd.partial(got[2:]).halves_now().result_now())
    for names in (FFN2, ("w_in",), BRANCH[:2], BRANCH[2:], FFN1[:2], FFN1[2:]):
        comm.update(names, out)
    out.update(small, dx=dx)
    return out


class _Local:
    late_rider = None

    def __init__(self, late):
        self._late = late

    def late(self, gathered):
        return self._late

    def reduce(self, names, grads, *, tag):
        return _LocalReduce(names, grads)

    def update(self, names, reduced):
        pass

    def small(self, grads):
        return None

    def small_done(self, got):
        pass


class _LocalReduce:
    sibling = chips = share = None

    def __init__(self, names, grads):
        self._result = dict(zip(names, grads))

    def partial(self, got=None):
        return self

    halves = partial_now = halves_now = partial

    def result(self, got=None):
        return self._result

    result_now = result


ANY = pl.BlockSpec(memory_space=pl.ANY)


def _place():
    x, y, c = lax.axis_index("x"), lax.axis_index("y"), lax.axis_index("c")
    chips = [(1 - x, y), (x, 1 - y), (1 - x, 1 - y)]
    return x, y, c, 2 * x + y, chips


def _remote(src, dst, send_sems, recv_sems, k, device):
    return pltpu.make_async_remote_copy(src_ref=src, dst_ref=dst, send_sem=send_sems.at[k], recv_sem=recv_sems.at[k],
                                        device_id=device, device_id_type=MESH)


class _Rider:
    def __init__(self, inputs, out_shape, scratch, start, middle, finish):
        self.inputs, self.out_shape, self.scratch = list(inputs), list(out_shape), list(scratch)
        self.start, self.middle, self.finish = start, middle, finish


def _two_riders(first, second):
    if first is None or second is None:
        return first or second
    assert first.middle is None and second.middle is None
    n_in, n_out, n_sem = len(first.inputs), len(first.out_shape), len(first.scratch)

    def phase(name):
        def run(ins, outs, sems):
            getattr(first, name)(ins[:n_in], outs[:n_out], sems[:n_sem])
            getattr(second, name)(ins[n_in:], outs[n_out:], sems[n_sem:])
        return run

    return _Rider(first.inputs + second.inputs, first.out_shape + second.out_shape, first.scratch + second.scratch,
                  phase("start"), None, phase("finish"))


def _run_rider(rider, *, name):
    if rider is None:
        return []
    n_in, n_out = len(rider.inputs), len(rider.out_shape)

    def body(*refs):
        ins, outs, sems = refs[:n_in], refs[n_in:n_in + n_out], refs[n_in + n_out:]
        rider.start(ins, outs, sems)
        if rider.middle is not None:
            rider.middle(ins, outs, sems)
        rider.finish(ins, outs, sems)

    return pl.pallas_call(body, name=name, in_specs=[ANY] * n_in, out_specs=[ANY] * n_out, out_shape=rider.out_shape,
                          scratch_shapes=rider.scratch)(*rider.inputs)


def _ride(body, n_in, n_out, rider, grid, middle_step):
    if rider is None:
        return body, [], [], [], [], []
    r_in, r_out = len(rider.inputs), len(rider.out_shape)
    steps = math.prod(grid)

    def riding(*refs):
        ins, r_ins = refs[:n_in], refs[n_in:n_in + r_in]
        outs = refs[n_in + r_in:n_in + r_in + n_out]
        r_outs = refs[n_in + r_in + n_out:n_in + r_in + n_out + r_out]
        sems = refs[n_in + r_in + n_out + r_out:]
        step = pl.program_id(0)
        for axis in range(1, len(grid)):
            step = step * grid[axis] + pl.program_id(axis)

        @pl.when(step == 0)
        def _():
            rider.start(r_ins, r_outs, sems)

        body(*ins, *outs)

        if rider.middle is not None:
            @pl.when(step == middle_step)
            def _():
                rider.middle(r_ins, r_outs, sems)

        @pl.when(step == steps - 1)
        def _():
            rider.finish(r_ins, r_outs, sems)

    return riding, rider.inputs, [ANY] * r_in, rider.out_shape, [ANY] * r_out, rider.scratch


def _gather_rider(shards):
    n = len(shards)

    def plan(ins, outs, sems, kinds):
        send_sems, recv_sems, own_send_sems, own_recv_sems = sems
        x, y, c, mine, chips = _place()
        sibling = (x, y, 1 - c)
        made = {k: [] for k in kinds}
        for i in range(n):
            hr = shards[i].shape[0] // 2
            if "own" in made:
                made["own"].append(_remote(ins[i], outs[i].at[mine], own_send_sems, own_recv_sems, i, sibling))
            for j, (cx, cy) in enumerate(chips):
                here = outs[i].at[2 * cx + cy, pl.ds(c * hr, hr)]
                there = outs[i].at[2 * cx + cy, pl.ds((1 - c) * hr, hr)]
                if "sends" in made:
                    made["sends"].append(_remote(ins[i].at[pl.ds(c * hr, hr)], outs[i].at[mine, pl.ds(c * hr, hr)],
                                                 send_sems, recv_sems, 6 * i + j, (cx, cy, c)))
                if "landed" in made:
                    made["landed"].append(_remote(here, here, send_sems, recv_sems, 6 * i + j, (cx, cy, c)))
                if "passes" in made:
                    made["passes"].append(_remote(here, here, send_sems, recv_sems, 6 * i + 3 + j, sibling))
                if "others" in made:
                    made["others"].append(_remote(there, there, send_sems, recv_sems, 6 * i + 3 + j, sibling))
        return [made[k] for k in kinds]

    def start(ins, outs, sems):
        own, sends = plan(ins, outs, sems, ("own", "sends"))
        for cp in own + sends:
            cp.start()

    def middle(ins, outs, sems):
        landed, passes = plan(ins, outs, sems, ("landed", "passes"))
        for arrived, cp in zip(landed, passes):
            arrived.wait_recv()
            cp.start()

    def finish(ins, outs, sems):
        own, sends, passes, others = plan(ins, outs, sems, ("own", "sends", "passes", "others"))
        for arrived in others:
            arrived.wait_recv()
        for cp in sends + passes:
            cp.wait_send()
        for cp in own:
            cp.wait()

    return _Rider(shards, [jax.ShapeDtypeStruct((N_CHIPS,) + s.shape, s.dtype) for s in shards],
                  [pltpu.SemaphoreType.DMA((6 * n,)), pltpu.SemaphoreType.DMA((6 * n,)),
                   pltpu.SemaphoreType.DMA((n,)), pltpu.SemaphoreType.DMA((n,))], start, middle, finish)


def _swap_rider(arrays, out_shape, source):
    n = len(arrays)

    def plan(ins, outs, sems):
        send_sems, recv_sems = sems
        x, y, c, _, _ = _place()
        return [_remote(source(ins[i], c, i), outs[i], send_sems, recv_sems, i, (x, y, 1 - c)) for i in range(n)]

    def start(ins, outs, sems):
        for cp in plan(ins, outs, sems):
            cp.start()

    def finish(ins, outs, sems):
        for cp in plan(ins, outs, sems):
            cp.wait()

    return _Rider(arrays, out_shape, [pltpu.SemaphoreType.DMA((n,)), pltpu.SemaphoreType.DMA((n,))], start, None, finish)


def _sibling_rider(grads):
    half = [g.shape[1] // 2 for g in grads]
    return _swap_rider(grads, [jax.ShapeDtypeStruct((g.shape[0], hr, g.shape[2]), g.dtype) for g, hr in zip(grads, half)],
                       lambda ref, c, i: ref.at[:, pl.ds((1 - c) * half[i], half[i])])


def _share_rider(halves):
    return _swap_rider(halves, [jax.ShapeDtypeStruct(h.shape, h.dtype) for h in halves], lambda ref, c, i: ref)


def _chips_rider(parts):
    n = len(parts)

    def plan(ins, outs, sems):
        send_sems, recv_sems = sems
        _, _, c, _, chips = _place()
        return [_remote(ins[i].at[2 * cx + cy], outs[i].at[j], send_sems, recv_sems, 3 * i + j, (cx, cy, c))
                for i in range(n) for j, (cx, cy) in enumerate(chips)]

    def start(ins, outs, sems):
        for cp in plan(ins, outs, sems):
            cp.start()

    def finish(ins, outs, sems):
        for cp in plan(ins, outs, sems):
            cp.wait()

    return _Rider(parts, [jax.ShapeDtypeStruct((N_CHIPS - 1,) + p.shape[1:], p.dtype) for p in parts],
                  [pltpu.SemaphoreType.DMA((3 * n,)), pltpu.SemaphoreType.DMA((3 * n,))], start, None, finish)


class _Reduce:
    def __init__(self, names, grads, cidx, chip, *, tag):
        self.names, self.grads, self.cidx, self.chip, self.tag = names, grads, cidx, chip, tag
        self.sibling = _sibling_rider(grads)

    def _by_shape(self, fn, *lists):
        done, i = [], 0
        while i < len(self.names):
            j = i + 1
            while j < len(self.names) and self.grads[j].shape == self.grads[i].shape:
                j += 1
            done += fn(*[lst[i:j] for lst in lists], self.names[i])
            i = j
        return done

    def partial(self, from_sibling):
        self.from_sibling = from_sibling
        self.chips = _chips_rider(self._by_shape(
            lambda g, r, k: _add_sibling(g, r, self.cidx, name="add_sibling_" + k), self.grads, from_sibling))
        return self

    def halves(self, from_chips):
        self.mine = self._by_shape(
            lambda g, r1, r2, k: _add_chips(g, r1, r2, self.cidx, self.chip, name="add_chips_" + k),
            self.grads, self.from_sibling, from_chips)
        self.share = _share_rider(self.mine)
        return self

    def result(self, others):
        return dict(zip(self.names, zip(self.mine, others)))

    def partial_now(self):
        return self.partial(_run_rider(self.sibling, name="rs_sibling_" + self.tag))

    def halves_now(self):
        return self.halves(_run_rider(self.chips, name="rs_chips_" + self.tag))

    def result_now(self):
        return self.result(_run_rider(self.share, name="rs_share_" + self.tag))


N_DEV = 8


def _small_rider(vec):
    def plan(ins, outs, sems):
        send_sems, recv_sems = sems
        x, y, c, _, _ = _place()
        return [_remote(ins[0], outs[0].at[k - 1], send_sems, recv_sems, k - 1, (x ^ (k >> 2), y ^ ((k >> 1) & 1), c ^ (k & 1)))
                for k in range(1, N_DEV)]

    def start(ins, outs, sems):
        for cp in plan(ins, outs, sems):
            cp.start()

    def finish(ins, outs, sems):
        for cp in plan(ins, outs, sems):
            cp.wait()

    return _Rider([vec], [jax.ShapeDtypeStruct((N_DEV - 1,) + vec.shape, vec.dtype)],
                  [pltpu.SemaphoreType.DMA((N_DEV - 1,)), pltpu.SemaphoreType.DMA((N_DEV - 1,))], start, None, finish)


def _small_sum(vec, others, me):
    def body(me_ref, v_ref, b_ref, o_ref):
        mine = me_ref[0]
        acc = None
        for d in range(N_DEV):
            term = jnp.where(mine == d, v_ref[...], b_ref[jnp.maximum((mine ^ d) - 1, 0)])
            acc = term if acc is None else acc + term
        o_ref[...] = acc

    vmem = pl.BlockSpec(memory_space=pltpu.VMEM)
    return pl.pallas_call(
        body, name="small_sum", in_specs=[pl.BlockSpec(memory_space=pltpu.SMEM), vmem, vmem], out_specs=vmem,
        out_shape=jax.ShapeDtypeStruct(vec.shape, vec.dtype),
    )(me, vec, others)


ELEMWISE_BLOCK = 512 * 1024


def _row_tile(rows, cols):
    best = None
    for t in range(16, rows + 1, 16):
        if rows % t == 0 and t * cols <= ELEMWISE_BLOCK:
            best = t
    return best if best is not None else rows


def _add_sibling(gs, r1s, cidx, *, name):
    n = len(gs)
    S, R, C = gs[0].shape
    hr = R // 2
    tr = _row_tile(hr, C)
    nt = hr // tr

    def body(c_ref, *refs):
        for g_ref, r_ref, o_ref in zip(refs[:n], refs[n:2 * n], refs[2 * n:]):
            o_ref[...] = (g_ref[...] + r_ref[...]).astype(BF16)

    blk = pl.BlockSpec((1, tr, C), lambda s, t, c: (s, t, 0))
    mine = pl.BlockSpec((1, tr, C), lambda s, t, c: (s, c[0] * nt + t, 0))
    return list(pl.pallas_call(
        body, name=name,
        grid_spec=pltpu.PrefetchScalarGridSpec(
            num_scalar_prefetch=1, grid=(S, nt), in_specs=[mine] * n + [blk] * n, out_specs=[blk] * n),
        out_shape=[jax.ShapeDtypeStruct((S, hr, C), BF16)] * n,
        compiler_params=_params(2),
    )(cidx, *gs, *r1s))


def _add_chips(gs, r1s, r2s, cidx, chip, *, name):
    n = len(gs)
    _, R, C = gs[0].shape
    hr = R // 2
    tr = _row_tile(hr, C)
    nt = hr // tr

    def body(pos_ref, *refs):
        for g_ref, r1_ref, r2_ref, o_ref in zip(refs[:n], refs[n:2 * n], refs[2 * n:3 * n], refs[3 * n:]):
            own = g_ref[0] + r1_ref[0]
            o_ref[...] = ((own + r2_ref[0].astype(F32)) + r2_ref[1].astype(F32)) + r2_ref[2].astype(F32)

    pos = jnp.concatenate([cidx, chip])
    return list(pl.pallas_call(
        body, name=name,
        grid_spec=pltpu.PrefetchScalarGridSpec(
            num_scalar_prefetch=1, grid=(nt,),
            in_specs=[pl.BlockSpec((1, tr, C), lambda t, pos: (pos[1], pos[0] * nt + t, 0))] * n
            + [pl.BlockSpec((1, tr, C), lambda t, pos: (pos[1], t, 0))] * n
            + [pl.BlockSpec((N_CHIPS - 1, tr, C), lambda t, pos: (0, t, 0))] * n,
            out_specs=[pl.BlockSpec((tr, C), lambda t, pos: (t, 0))] * n),
        out_shape=[jax.ShapeDtypeStruct((hr, C), F32)] * n,
        compiler_params=_params(1),
    )(pos, *gs, *r1s, *r2s))


def _adamw_math(w, g, m, v):
    mn = ADAM_B1 * m + (1.0 - ADAM_B1) * g
    vn = ADAM_B2 * v + (1.0 - ADAM_B2) * (g * g)
    m_hat = mn / (1.0 - ADAM_B1 ** ADAM_STEP)
    v_hat = vn / (1.0 - ADAM_B2 ** ADAM_STEP)
    return -ADAM_LR * (m_hat / (jnp.sqrt(v_hat) + ADAM_EPS) + ADAM_WD * w), mn, vn


def _adamw_halves(ws, mines, others, ms, vs, cidx, *, name):
    n = len(ws)
    R, C = ws[0].shape
    hr = R // 2
    tr = _row_tile(hr, C * min(n, 2))
    nt = hr // tr

    def body(c_ref, *refs):
        for i in range(n):
            w_ref, a_ref, b_ref, m_ref, v_ref = (refs[j * n + i] for j in range(5))
            g_ref, d_ref, mo_ref, vo_ref = (refs[(5 + j) * n + i] for j in range(4))
            gv = jnp.where(pl.program_id(0) == c_ref[0], a_ref[...], b_ref[...])
            g_ref[...] = gv
            d_ref[...], mo_ref[...], vo_ref[...] = _adamw_math(w_ref[...], gv, m_ref[...], v_ref[...])

    full = pl.BlockSpec((tr, C), lambda h, t, c: (h * nt + t, 0))
    own = pl.BlockSpec((tr, C), lambda h, t, c: (jnp.where(h == c[0], t, 0), 0))
    sib = pl.BlockSpec((tr, C), lambda h, t, c: (jnp.where(h == c[0], 0, t), 0))
    shape = jax.ShapeDtypeStruct((R, C), F32)
    outs = pl.pallas_call(
        body, name=name,
        grid_spec=pltpu.PrefetchScalarGridSpec(
            num_scalar_prefetch=1, grid=(2, nt),
            in_specs=[full] * n + [own] * n + [sib] * n + [full] * (2 * n), out_specs=[full] * (4 * n)),
        out_shape=[shape] * (4 * n),
        compiler_params=_params(2),
    )(cidx, *ws, *mines, *others, *ms, *vs)
    return [list(outs[j * n:(j + 1) * n]) for j in range(4)]


def _adamw_small(ws, gs, ms, vs):
    n = len(ws)

    def body(*refs):
        for i in range(n):
            w_ref, g_ref, m_ref, v_ref = (refs[j * n + i] for j in range(4))
            d_ref, mo_ref, vo_ref = (refs[(4 + j) * n + i] for j in range(3))
            d_ref[...], mo_ref[...], vo_ref[...] = _adamw_math(w_ref[...], g_ref[...], m_ref[...], v_ref[...])

    shapes = [jax.ShapeDtypeStruct(a.shape, F32) for a in ws]
    outs = pl.pallas_call(body, name="adamw_small", out_shape=shapes * 3, compiler_params=_params(0))(*ws, *gs, *ms, *vs)
    return outs[:n], outs[n:2 * n], outs[2 * n:]


def _unstack_cols(w):
    s, r, c = w.shape
    return w.transpose(1, 0, 2).reshape(r, s * c)


def _pad_rows(a, rows):
    return jnp.pad(a, ((0, rows - a.shape[0]), (0, LANES - a.shape[1])))


BIG = ("ffn1_w_gate", "ffn1_w_up", "ffn1_w_down", "w_in", "w_branch_a", "w_branch_b", "w_out",
       "ffn2_w_gate", "ffn2_w_up", "ffn2_w_down")
TRANSPOSED = ("ffn1_w_gate", "ffn1_w_up", "w_in", "ffn2_w_gate", "ffn2_w_up")
WEIGHTS = ("ffn1_norm", "ffn1_w_gate", "ffn1_w_up", "ffn1_w_down", "mix_norm", "w_in", "na_rpb", "sink_logit",
           "w_branch_a", "w_branch_b", "w_out", "ffn2_norm", "ffn2_w_gate", "ffn2_w_up", "ffn2_w_down", "final_norm")


def kernel(x, ffn1_norm, ffn1_w_gate, ffn1_w_up, ffn1_w_down, mix_norm, w_in, na_rpb, sink_logit, w_branch_a, w_branch_b, w_out, ffn2_norm, ffn2_w_gate, ffn2_w_up, ffn2_w_down, final_norm, loss_target, m_ffn1_norm, m_ffn1_w_gate, m_ffn1_w_up, m_ffn1_w_down, m_mix_norm, m_w_in, m_na_rpb, m_sink_logit, m_w_branch_a, m_w_branch_b, m_w_out, m_ffn2_norm, m_ffn2_w_gate, m_ffn2_w_up, m_ffn2_w_down, m_final_norm, v_ffn1_norm, v_ffn1_w_gate, v_ffn1_w_up, v_ffn1_w_down, v_mix_norm, v_w_in, v_na_rpb, v_sink_logit, v_w_branch_a, v_w_branch_b, v_w_out, v_ffn2_norm, v_ffn2_w_gate, v_ffn2_w_up, v_ffn2_w_down, v_final_norm):
    args = dict(locals())
    w = {k: args[k] for k in WEIGHTS}
    mom = {k: args["m_" + k] for k in WEIGHTS}
    var = {k: args["v_" + k] for k in WEIGHTS}
    cidx = lax.axis_index("c").astype(jnp.int32).reshape(1)
    chip = (2 * lax.axis_index("x") + lax.axis_index("y")).astype(jnp.int32).reshape(1)

    def shard(a, k):
        return jnp.swapaxes(a[0], 0, 1) if k in TRANSPOSED else a[0]

    def unshard(a, k):
        return (jnp.swapaxes(a, 0, 1) if k in TRANSPOSED else a)[None]

    def bf16_shards(names):
        return [shard(w[k], k).astype(BF16) for k in names]

    class comm:
        late_rider = _gather_rider(bf16_shards(MIXER + FFN2))

        @staticmethod
        def late(gathered):
            full = dict(zip(MIXER + FFN2, gathered))
            return (full["w_in"].reshape(D_IN, D_MODEL), _unstack_cols(full["w_branch_a"]), _unstack_cols(full["w_branch_b"]),
                    full["w_out"].reshape(D_MODEL, D_MODEL), tuple(full[k] for k in FFN2))

        @staticmethod
        def reduce(names, grads, *, tag):
            return _Reduce(names, grads, cidx, chip, tag=tag)

        @staticmethod
        def update(names, reduced):
            res = _adamw_halves([shard(w[k], k) for k in names], [reduced[k][0] for k in names],
                                [reduced[k][1] for k in names], [shard(mom[k], k) for k in names],
                                [shard(var[k], k) for k in names], cidx, name="adamw_" + names[0])
            for i, k in enumerate(names):
                grads_out[k], deltas[k], new_m[k], new_v[k] = (unshard(a[i], k) for a in res)

        @staticmethod
        def small(g):
            packed["mine"] = jnp.concatenate([
                g["ffn1_norm"].reshape(rows, LANES), g["mix_norm"].reshape(rows, LANES), g["ffn2_norm"].reshape(rows, LANES),
                g["final_norm"].reshape(rows, LANES), g["na_rpb"].reshape(-1, LANES),
                _pad_rows(g["sink_logit"].reshape(1, NB_HEADS), 8), _pad_rows(g["loss"], 8)], axis=0)
            return _small_rider(packed["mine"])

        @staticmethod
        def small_done(got):
            packed["others"], = got

    deltas, new_m, new_v, grads_out, grad, packed = {}, {}, {}, {}, {}, {}
    rows = D_MODEL // LANES
    f1 = _run_rider(_gather_rider(bf16_shards(FFN1)), name="all_gather_ffn1")
    out = _layer_grads(x[0], loss_target[0], ffn1_norm, f1, mix_norm, None, na_rpb[0], sink_logit[0], ffn2_norm,
                       final_norm.reshape(1, D_MODEL), comm=comm)

    me = (4 * lax.axis_index("x") + 2 * lax.axis_index("y") + lax.axis_index("c")).astype(jnp.int32).reshape(1)
    total = _small_sum(packed["mine"], packed["others"], me)
    n_rpb = NA_HEADS * 2 * NA_KH
    grad["ffn1_norm"] = total[0:rows].reshape(1, D_MODEL)
    grad["mix_norm"] = total[rows:2 * rows].reshape(1, D_MODEL)
    grad["ffn2_norm"] = total[2 * rows:3 * rows].reshape(1, D_MODEL)
    grad["final_norm"] = total[3 * rows:4 * rows].reshape(1, D_MODEL)
    grad["na_rpb"] = total[4 * rows:4 * rows + n_rpb].reshape(NA_HEADS, 2 * NA_KH, LANES)[:, :2 * NA_KH - 1, :2 * NA_KW - 1]
    grad["na_rpb"] = grad["na_rpb"].reshape(NA_HEADS, -1)
    grad["sink_logit"] = total[4 * rows + n_rpb:4 * rows + n_rpb + 1, 0:NB_HEADS]
    loss = total[4 * rows + n_rpb + 8, 0]

    small_names = [k for k in WEIGHTS if k not in BIG]
    res = _adamw_small(*[[a[k].reshape(grad[k].shape) for k in small_names] for a in (w, grad, mom, var)])
    for i, k in enumerate(small_names):
        grads_out[k], deltas[k], new_m[k], new_v[k] = (a.reshape(w[k].shape) for a in (grad[k], res[0][i], res[1][i], res[2][i]))
    return (loss, out["dx"].reshape(x.shape), *[grads_out[k] for k in WEIGHTS], *[deltas[k] for k in WEIGHTS],
            *[new_m[k] for k in WEIGHTS], *[new_v[k] for k in WEIGHTS])
```
